```python
import jax, jax.numpy as jnp
from jax import lax
import numpy as np

D_MODEL = 2048
BATCH = 8
SEQ = 2048
DEPTH = 2

GRID_W = 64
CTX_LEN = 256
RET_HEADS = 8
RET_HEAD_DIM = 128
RET_W = RET_HEADS * RET_HEAD_DIM
K_SCALE = RET_HEAD_DIM ** -0.5
CHUNK = 128
ROPE_BASE = 10000.0
CONV_GROUPS = 8
CONV_W = D_MODEL - RET_W
CONV_WIDTH = 3
IN_W = 4 * RET_W + 3 * CONV_W
POOL_WINDOWS = (2, 4, 8, 16)
POOL_GROUPS = len(POOL_WINDOWS)
POOL_GROUP_W = D_MODEL // POOL_GROUPS
D_FF = 5632
MACARON = 0.5
N_MOD = 9
N_EVEN = (DEPTH + 1) // 2
N_ODD = DEPTH // 2
EPS = 1e-6
GN_EPS = 1e-5

kernel_name = "hybrid_retention_shortconv_pool_macaron_dit"


def rmsnorm(x, g):
    xf = x.astype(jnp.float32)
    n = xf * lax.rsqrt(jnp.mean(xf * xf, axis=-1, keepdims=True) + EPS)
    return (n * g.astype(jnp.float32)).astype(x.dtype)


def modulate(h, shift, scale):
    return h * (1.0 + scale) + shift


def adaln(cond, w_mod_l, b_mod_l):
    m = jax.nn.silu(cond) @ w_mod_l + b_mod_l
    return jnp.split(m[..., None, :], N_MOD, axis=-1)


def ffn_sublayer(x, g_norm, shift, scale, gate, w_gate, w_up, w_down):
    h = modulate(rmsnorm(x, g_norm), shift, scale)
    y = (jax.nn.silu(h @ w_gate) * (h @ w_up)) @ w_down
    return x + MACARON * gate * y


def to_heads(t):
    b, l, _ = t.shape
    return t.reshape(b, l, RET_HEADS, RET_HEAD_DIM).transpose(0, 2, 1, 3).astype(jnp.float32)


def axial_rope_tables(rows, cols):
    quarter = RET_HEAD_DIM // 4
    inv = ROPE_BASE ** (-jnp.arange(quarter, dtype=jnp.float32) / quarter)
    ang = jnp.concatenate([rows.astype(jnp.float32)[:, None] * inv,
                           cols.astype(jnp.float32)[:, None] * inv], axis=-1)
    return jnp.cos(ang), jnp.sin(ang)


def apply_rope(t, cos, sin):
    half = RET_HEAD_DIM // 2
    t1, t2 = t[..., :half], t[..., half:]
    return jnp.concatenate([t1 * cos - t2 * sin, t1 * sin + t2 * cos], axis=-1)


def retention_chunked(q, k, v, log_gamma, s0):
    b, h, l, dk = q.shape
    dv = v.shape[-1]
    n = l // CHUNK
    lg = log_gamma.astype(jnp.float32)
    qc = q.reshape(b, h, n, CHUNK, dk)
    kc = k.reshape(b, h, n, CHUNK, dk)
    vc = v.reshape(b, h, n, CHUNK, dv)
    idx = jnp.arange(CHUNK, dtype=jnp.float32)
    diff = idx[:, None] - idx[None, :]
    decay_intra = jnp.where(diff >= 0, jnp.exp(lg[:, None, None] * jnp.maximum(diff, 0.0)), 0.0)
    q_decay = jnp.exp(lg[:, None] * (idx + 1.0))
    k_decay = jnp.exp(lg[:, None] * (CHUNK - 1.0 - idx))
    chunk_decay = jnp.exp(lg * CHUNK)
    scores = jnp.einsum('bhncd,bhnmd->bhncm', qc, kc) * decay_intra[None, :, None]
    intra = jnp.einsum('bhncm,bhnmv->bhncv', scores, vc)
    kv_chunk = jnp.einsum('bhncd,bhncv->bhndv', kc * k_decay[None, :, None, :, None], vc)

    def step(s, kv_n):
        return s * chunk_decay[None, :, None, None] + kv_n, s

    _, s_before = lax.scan(step, s0.astype(jnp.float32), jnp.moveaxis(kv_chunk, 2, 0))
    cross = jnp.einsum('bhncd,nbhdv->bhncv', qc * q_decay[None, :, None, :, None], s_before)
    return (intra + cross).reshape(b, h, l, dv)


def retention_bidir(q, k, v, lg_fwd, lg_bwd, s_fwd, s_bwd):
    out_f = retention_chunked(q, k, v, lg_fwd, s_fwd)
    out_b = retention_chunked(jnp.flip(q, 2), jnp.flip(k, 2), jnp.flip(v, 2), lg_bwd, s_bwd)
    return out_f + jnp.flip(out_b, 2)


def context_states(kc, vc, lg_fwd, lg_bwd):
    lc = kc.shape[2]
    pos = jnp.arange(lc, dtype=jnp.float32)
    w_f = jnp.exp(lg_fwd.astype(jnp.float32)[:, None] * (lc - 1.0 - pos))
    w_b = jnp.exp(lg_bwd.astype(jnp.float32)[:, None] * pos)
    s_f = jnp.einsum('hl,bhlk,bhlv->bhkv', w_f, kc, vc)
    s_b = jnp.einsum('hl,bhlk,bhlv->bhkv', w_b, kc, vc)
    return s_f, s_b


def group_norm_heads(o):
    mu = jnp.mean(o, axis=-1, keepdims=True)
    var = jnp.mean(jnp.square(o - mu), axis=-1, keepdims=True)
    return (o - mu) * lax.rsqrt(var + GN_EPS)


def conv3_centred(u, w):
    up = jnp.pad(u, ((0, 0), (1, 1), (0, 0)))
    return up[:, :-2] * w[0] + up[:, 1:-1] * w[1] + up[:, 2:] * w[2]


def even_mixer(p, w_conv, w_out, lg_fwd, lg_bwd, s_fwd, s_bwd, rope):
    b, l, _ = p.shape
    q, k, v, g, bg, cg, u = jnp.split(
        p, [RET_W, 2 * RET_W, 3 * RET_W, 4 * RET_W, 4 * RET_W + CONV_W, 4 * RET_W + 2 * CONV_W], axis=-1)
    qh, kh, vh = to_heads(q), to_heads(k) * K_SCALE, to_heads(v)
    if rope is not None:
        qh, kh = apply_rope(qh, *rope), apply_rope(kh, *rope)
    o = group_norm_heads(retention_bidir(qh, kh, vh, lg_fwd, lg_bwd, s_fwd, s_bwd))
    ret = o.transpose(0, 2, 1, 3).reshape(b, l, RET_W).astype(p.dtype) * jax.nn.silu(g)
    conv = bg * conv3_centred(cg * u, w_conv)
    return jnp.concatenate([ret, conv], axis=-1) @ w_out


def centred_window_mean(u, w):
    l = u.shape[1]
    cs = jnp.cumsum(u.astype(jnp.float32), axis=1)
    cs = jnp.concatenate([jnp.zeros_like(cs[:, :1]), cs], axis=1)
    t = jnp.arange(l)
    lo = jnp.clip(t - w // 2, 0, l)
    hi = jnp.clip(t + (w - w // 2), 0, l)
    s = jnp.take(cs, hi, axis=1) - jnp.take(cs, lo, axis=1)
    cnt = (hi - lo).astype(jnp.float32)[None, :, None]
    return (s / cnt).astype(u.dtype)


def pool_mixer(h, w_groups, scale):
    b, l, d = h.shape
    hg = h.reshape(b, l, POOL_GROUPS, POOL_GROUP_W)
    pooled = jnp.stack([centred_window_mean(hg[:, :, i], w) - hg[:, :, i]
                        for i, w in enumerate(POOL_WINDOWS)], axis=2)
    y = jnp.einsum('blgc,gcd->blgd', pooled, w_groups).reshape(b, l, d)
    return y * scale


def _fwd_setup_inputs(seed: int = 0) -> dict:
    key = jax.random.key(seed)
    ks = jax.random.split(key, 24)
    f32 = jnp.float32

    def nrm(k, shape, fan_in, scale=1.0):
        return jax.random.normal(k, shape, f32) * (scale * fan_in ** -0.5)

    gamma0 = 1.0 - 2.0 ** (-5.0 - np.arange(RET_HEADS, dtype=np.float32))
    decay_logit0 = jnp.asarray(np.log(gamma0 / (1.0 - gamma0)).astype(np.float32))
    return {
        "x": jax.random.normal(ks[0], (BATCH, SEQ, D_MODEL), f32),
        "c": jax.random.normal(ks[1], (BATCH, D_MODEL), f32),
        "ctx": jax.random.normal(ks[2], (BATCH, CTX_LEN, D_MODEL), f32),
        "c_ctx": jax.random.normal(ks[3], (D_MODEL,), f32),
        "w_mod": nrm(ks[4], (DEPTH, D_MODEL, N_MOD * D_MODEL), D_MODEL, 0.5),
        "b_mod": 0.02 * jax.random.normal(ks[5], (DEPTH, N_MOD * D_MODEL), f32),
        "norm_ffn1": 1.0 + 0.05 * jax.random.normal(ks[6], (DEPTH, D_MODEL), f32),
        "norm_mix": 1.0 + 0.05 * jax.random.normal(ks[7], (DEPTH, D_MODEL), f32),
        "norm_ffn2": 1.0 + 0.05 * jax.random.normal(ks[8], (DEPTH, D_MODEL), f32),
        "ffn1_w_gate": nrm(ks[9], (DEPTH, D_MODEL, D_FF), D_MODEL),
        "ffn1_w_up": nrm(ks[10], (DEPTH, D_MODEL, D_FF), D_MODEL),
        "ffn1_w_down": nrm(ks[11], (DEPTH, D_FF, D_MODEL), D_FF),
        "ffn2_w_gate": nrm(ks[12], (DEPTH, D_MODEL, D_FF), D_MODEL),
        "ffn2_w_up": nrm(ks[13], (DEPTH, D_MODEL, D_FF), D_MODEL),
        "ffn2_w_down": nrm(ks[14], (DEPTH, D_FF, D_MODEL), D_FF),
        "mix_w_in": nrm(ks[15], (N_EVEN, D_MODEL, IN_W), D_MODEL),
        "mix_w_conv": nrm(ks[16], (N_EVEN, CONV_WIDTH, CONV_W), CONV_WIDTH),
        "mix_w_out": nrm(ks[17], (N_EVEN, D_MODEL, D_MODEL), D_MODEL),
        "ret_decay_fwd": decay_logit0 + 0.1 * jax.random.normal(ks[18], (N_EVEN, RET_HEADS), f32),
        "ret_decay_bwd": decay_logit0 + 0.1 * jax.random.normal(ks[19], (N_EVEN, RET_HEADS), f32),
        "pool_w": nrm(ks[20], (N_ODD, POOL_GROUPS, POOL_GROUP_W, POOL_GROUP_W), POOL_GROUP_W),
        "pool_scale": 1.0 + 0.1 * jax.random.normal(ks[21], (N_ODD, D_MODEL), f32),
        "final_norm": 1.0 + 0.05 * jax.random.normal(ks[22], (D_MODEL,), f32),
    }


def _fwd_reference(x, c, ctx, c_ctx, w_mod, b_mod, norm_ffn1, norm_mix, norm_ffn2,
              ffn1_w_gate, ffn1_w_up, ffn1_w_down, ffn2_w_gate, ffn2_w_up, ffn2_w_down,
              mix_w_in, mix_w_conv, mix_w_out, ret_decay_fwd, ret_decay_bwd,
              pool_w, pool_scale, final_norm):
    b, l, _ = x.shape
    ROWS = l // GRID_W
    rows = jnp.repeat(jnp.arange(ROWS), GRID_W)
    cols = jnp.tile(jnp.arange(GRID_W), ROWS)
    rope = axial_rope_tables(rows, cols)
    last_even = ((DEPTH - 1) // 2) * 2
    xc = ctx
    for li in range(DEPTH):
        ctx_needed = li <= last_even
        ctx_full = li < last_even
        m = adaln(c, w_mod[li], b_mod[li])
        x = ffn_sublayer(x, norm_ffn1[li], m[0], m[1], m[2], ffn1_w_gate[li], ffn1_w_up[li], ffn1_w_down[li])
        if ctx_needed:
            mc = adaln(c_ctx, w_mod[li], b_mod[li])
            xc = ffn_sublayer(xc, norm_ffn1[li], mc[0], mc[1], mc[2],
                              ffn1_w_gate[li], ffn1_w_up[li], ffn1_w_down[li])
        h = modulate(rmsnorm(x, norm_mix[li]), m[3], m[4])
        if li % 2 == 0:
            e = li // 2
            lg_f = jax.nn.log_sigmoid(ret_decay_fwd[e])
            lg_b = jax.nn.log_sigmoid(ret_decay_bwd[e])
            hc = modulate(rmsnorm(xc, norm_mix[li]), mc[3], mc[4])
            if ctx_full:
                pc = hc @ mix_w_in[e]
                kc, vc = pc[..., RET_W:2 * RET_W], pc[..., 2 * RET_W:3 * RET_W]
                zeros = jnp.zeros((b, RET_HEADS, RET_HEAD_DIM, RET_HEAD_DIM), jnp.float32)
                yc = even_mixer(pc, mix_w_conv[e], mix_w_out[e], lg_f, lg_b, zeros, zeros, None)
            else:
                kc = hc @ mix_w_in[e][:, RET_W:2 * RET_W]
                vc = hc @ mix_w_in[e][:, 2 * RET_W:3 * RET_W]
            s_f, s_b = context_states(to_heads(kc) * K_SCALE, to_heads(vc), lg_f, lg_b)
            y = even_mixer(h @ mix_w_in[e], mix_w_conv[e], mix_w_out[e], lg_f, lg_b, s_f, s_b, rope)
        else:
            o = li // 2
            y = pool_mixer(h, pool_w[o], pool_scale[o])
            if ctx_full:
                hc = modulate(rmsnorm(xc, norm_mix[li]), mc[3], mc[4])
                yc = pool_mixer(hc, pool_w[o], pool_scale[o])
        x = x + m[5] * y.astype(x.dtype)
        x = ffn_sublayer(x, norm_ffn2[li], m[6], m[7], m[8], ffn2_w_gate[li], ffn2_w_up[li], ffn2_w_down[li])
        if ctx_full:
            xc = xc + mc[5] * yc.astype(xc.dtype)
            xc = ffn_sublayer(xc, norm_ffn2[li], mc[6], mc[7], mc[8],
                              ffn2_w_gate[li], ffn2_w_up[li], ffn2_w_down[li])
    return rmsnorm(x, final_norm)


import jax as _jax
import jax.numpy as _jnp

TWIN_FORMAT = 'train_step'
FWD_PARAMS = ['x', 'c', 'ctx', 'c_ctx', 'w_mod', 'b_mod', 'norm_ffn1', 'norm_mix', 'norm_ffn2', 'ffn1_w_gate', 'ffn1_w_up', 'ffn1_w_down', 'ffn2_w_gate', 'ffn2_w_up', 'ffn2_w_down', 'mix_w_in', 'mix_w_conv', 'mix_w_out', 'ret_decay_fwd', 'ret_decay_bwd', 'pool_w', 'pool_scale', 'final_norm']
TWIN_WEIGHTS = ['c_ctx', 'w_mod', 'b_mod', 'norm_ffn1', 'norm_mix', 'norm_ffn2', 'ffn1_w_gate', 'ffn1_w_up', 'ffn1_w_down', 'ffn2_w_gate', 'ffn2_w_up', 'ffn2_w_down', 'mix_w_in', 'mix_w_conv', 'mix_w_out', 'ret_decay_fwd', 'ret_decay_bwd', 'pool_w', 'pool_scale', 'final_norm']
TWIN_DIFF_INPUT = 'x'
TWIN_INPUTS = ['x', 'c', 'ctx', 'c_ctx', 'w_mod', 'b_mod', 'norm_ffn1', 'norm_mix', 'norm_ffn2', 'ffn1_w_gate', 'ffn1_w_up', 'ffn1_w_down', 'ffn2_w_gate', 'ffn2_w_up', 'ffn2_w_down', 'mix_w_in', 'mix_w_conv', 'mix_w_out', 'ret_decay_fwd', 'ret_decay_bwd', 'pool_w', 'pool_scale', 'final_norm', 'loss_target', 'm_c_ctx', 'm_w_mod', 'm_b_mod', 'm_norm_ffn1', 'm_norm_mix', 'm_norm_ffn2', 'm_ffn1_w_gate', 'm_ffn1_w_up', 'm_ffn1_w_down', 'm_ffn2_w_gate', 'm_ffn2_w_up', 'm_ffn2_w_down', 'm_mix_w_in', 'm_mix_w_conv', 'm_mix_w_out', 'm_ret_decay_fwd', 'm_ret_decay_bwd', 'm_pool_w', 'm_pool_scale', 'm_final_norm', 'v_c_ctx', 'v_w_mod', 'v_b_mod', 'v_norm_ffn1', 'v_norm_mix', 'v_norm_ffn2', 'v_ffn1_w_gate', 'v_ffn1_w_up', 'v_ffn1_w_down', 'v_ffn2_w_gate', 'v_ffn2_w_up', 'v_ffn2_w_down', 'v_mix_w_in', 'v_mix_w_conv', 'v_mix_w_out', 'v_ret_decay_fwd', 'v_ret_decay_bwd', 'v_pool_w', 'v_pool_scale', 'v_final_norm']
TWIN_OUTPUTS = ['loss', 'grad_x', 'grad_c_ctx', 'grad_w_mod', 'grad_b_mod', 'grad_norm_ffn1', 'grad_norm_mix', 'grad_norm_ffn2', 'grad_ffn1_w_gate', 'grad_ffn1_w_up', 'grad_ffn1_w_down', 'grad_ffn2_w_gate', 'grad_ffn2_w_up', 'grad_ffn2_w_down', 'grad_mix_w_in', 'grad_mix_w_conv', 'grad_mix_w_out', 'grad_ret_decay_fwd', 'grad_ret_decay_bwd', 'grad_pool_w', 'grad_pool_scale', 'grad_final_norm', 'delta_c_ctx', 'delta_w_mod', 'delta_b_mod', 'delta_norm_ffn1', 'delta_norm_mix', 'delta_norm_ffn2', 'delta_ffn1_w_gate', 'delta_ffn1_w_up', 'delta_ffn1_w_down', 'delta_ffn2_w_gate', 'delta_ffn2_w_up', 'delta_ffn2_w_down', 'delta_mix_w_in', 'delta_mix_w_conv', 'delta_mix_w_out', 'delta_ret_decay_fwd', 'delta_ret_decay_bwd', 'delta_pool_w', 'delta_pool_scale', 'delta_final_norm', 'new_m_c_ctx', 'new_m_w_mod', 'new_m_b_mod', 'new_m_norm_ffn1', 'new_m_norm_mix', 'new_m_norm_ffn2', 'new_m_ffn1_w_gate', 'new_m_ffn1_w_up', 'new_m_ffn1_w_down', 'new_m_ffn2_w_gate', 'new_m_ffn2_w_up', 'new_m_ffn2_w_down', 'new_m_mix_w_in', 'new_m_mix_w_conv', 'new_m_mix_w_out', 'new_m_ret_decay_fwd', 'new_m_ret_decay_bwd', 'new_m_pool_w', 'new_m_pool_scale', 'new_m_final_norm', 'new_v_c_ctx', 'new_v_w_mod', 'new_v_b_mod', 'new_v_norm_ffn1', 'new_v_norm_mix', 'new_v_norm_ffn2', 'new_v_ffn1_w_gate', 'new_v_ffn1_w_up', 'new_v_ffn1_w_down', 'new_v_ffn2_w_gate', 'new_v_ffn2_w_up', 'new_v_ffn2_w_down', 'new_v_mix_w_in', 'new_v_mix_w_conv', 'new_v_mix_w_out', 'new_v_ret_decay_fwd', 'new_v_ret_decay_bwd', 'new_v_pool_w', 'new_v_pool_scale', 'new_v_final_norm']
TWIN_LEAF_KINDS = {'loss': 'loss', 'grad_x': 'grad_x', 'grad_c_ctx': 'grad_w', 'grad_w_mod': 'grad_w', 'grad_b_mod': 'grad_w', 'grad_norm_ffn1': 'grad_w', 'grad_norm_mix': 'grad_w', 'grad_norm_ffn2': 'grad_w', 'grad_ffn1_w_gate': 'grad_w', 'grad_ffn1_w_up': 'grad_w', 'grad_ffn1_w_down': 'grad_w', 'grad_ffn2_w_gate': 'grad_w', 'grad_ffn2_w_up': 'grad_w', 'grad_ffn2_w_down': 'grad_w', 'grad_mix_w_in': 'grad_w', 'grad_mix_w_conv': 'grad_w', 'grad_mix_w_out': 'grad_w', 'grad_ret_decay_fwd': 'grad_w', 'grad_ret_decay_bwd': 'grad_w', 'grad_pool_w': 'grad_w', 'grad_pool_scale': 'grad_w', 'grad_final_norm': 'grad_w', 'delta_c_ctx': 'delta_w', 'delta_w_mod': 'delta_w', 'delta_b_mod': 'delta_w', 'delta_norm_ffn1': 'delta_w', 'delta_norm_mix': 'delta_w', 'delta_norm_ffn2': 'delta_w', 'delta_ffn1_w_gate': 'delta_w', 'delta_ffn1_w_up': 'delta_w', 'delta_ffn1_w_down': 'delta_w', 'delta_ffn2_w_gate': 'delta_w', 'delta_ffn2_w_up': 'delta_w', 'delta_ffn2_w_down': 'delta_w', 'delta_mix_w_in': 'delta_w', 'delta_mix_w_conv': 'delta_w', 'delta_mix_w_out': 'delta_w', 'delta_ret_decay_fwd': 'delta_w', 'delta_ret_decay_bwd': 'delta_w', 'delta_pool_w': 'delta_w', 'delta_pool_scale': 'delta_w', 'delta_final_norm': 'delta_w', 'new_m_c_ctx': 'new_m', 'new_m_w_mod': 'new_m', 'new_m_b_mod': 'new_m', 'new_m_norm_ffn1': 'new_m', 'new_m_norm_mix': 'new_m', 'new_m_norm_ffn2': 'new_m', 'new_m_ffn1_w_gate': 'new_m', 'new_m_ffn1_w_up': 'new_m', 'new_m_ffn1_w_down': 'new_m', 'new_m_ffn2_w_gate': 'new_m', 'new_m_ffn2_w_up': 'new_m', 'new_m_ffn2_w_down': 'new_m', 'new_m_mix_w_in': 'new_m', 'new_m_mix_w_conv': 'new_m', 'new_m_mix_w_out': 'new_m', 'new_m_ret_decay_fwd': 'new_m', 'new_m_ret_decay_bwd': 'new_m', 'new_m_pool_w': 'new_m', 'new_m_pool_scale': 'new_m', 'new_m_final_norm': 'new_m', 'new_v_c_ctx': 'new_v', 'new_v_w_mod': 'new_v', 'new_v_b_mod': 'new_v', 'new_v_norm_ffn1': 'new_v', 'new_v_norm_mix': 'new_v', 'new_v_norm_ffn2': 'new_v', 'new_v_ffn1_w_gate': 'new_v', 'new_v_ffn1_w_up': 'new_v', 'new_v_ffn1_w_down': 'new_v', 'new_v_ffn2_w_gate': 'new_v', 'new_v_ffn2_w_up': 'new_v', 'new_v_ffn2_w_down': 'new_v', 'new_v_mix_w_in': 'new_v', 'new_v_mix_w_conv': 'new_v', 'new_v_mix_w_out': 'new_v', 'new_v_ret_decay_fwd': 'new_v', 'new_v_ret_decay_bwd': 'new_v', 'new_v_pool_w': 'new_v', 'new_v_pool_scale': 'new_v', 'new_v_final_norm': 'new_v'}


def _forward(args):
    return _fwd_reference(*[args[k] for k in FWD_PARAMS])


def _output_shape():
    out = _jax.eval_shape(lambda: _forward(_fwd_setup_inputs(0)))
    return out.shape, out.dtype

N_MICROBATCH = 1
ADAM_LR = 0.001
ADAM_B1 = 0.9
ADAM_B2 = 0.999
ADAM_EPS = 1e-08
ADAM_WD = 0.01
ADAM_STEP = 10
PER_EXAMPLE_BATCH_AXIS = {'x': 0, 'c': 0, 'ctx': 0, 'loss_target': 0}
SHARED_INPUTS = []
_WEIGHT_DTYPES = {'c_ctx': _jnp.float32, 'w_mod': _jnp.float32, 'b_mod': _jnp.float32, 'norm_ffn1': _jnp.float32, 'norm_mix': _jnp.float32, 'norm_ffn2': _jnp.float32, 'ffn1_w_gate': _jnp.float32, 'ffn1_w_up': _jnp.float32, 'ffn1_w_down': _jnp.float32, 'ffn2_w_gate': _jnp.float32, 'ffn2_w_up': _jnp.float32, 'ffn2_w_down': _jnp.float32, 'mix_w_in': _jnp.float32, 'mix_w_conv': _jnp.float32, 'mix_w_out': _jnp.float32, 'ret_decay_fwd': _jnp.float32, 'ret_decay_bwd': _jnp.float32, 'pool_w': _jnp.float32, 'pool_scale': _jnp.float32, 'final_norm': _jnp.float32}
MOMENT_SCALE = {'c_ctx': 6.312714e-03, 'w_mod': 2.100550e-02, 'b_mod': 3.744531e-02, 'norm_ffn1': 9.598537e-03, 'norm_mix': 2.512009e-02, 'norm_ffn2': 8.829779e-03, 'ffn1_w_gate': 4.369998e-03, 'ffn1_w_up': 4.234234e-03, 'ffn1_w_down': 7.011415e-03, 'ffn2_w_gate': 4.012527e-03, 'ffn2_w_up': 3.885221e-03, 'ffn2_w_down': 6.441391e-03, 'mix_w_in': 1.820759e-02, 'mix_w_conv': 2.409018e-02, 'mix_w_out': 1.854480e-02, 'ret_decay_fwd': 5.522121e-02, 'ret_decay_bwd': 5.323362e-02, 'pool_w': 1.637098e-02, 'pool_scale': 5.722426e-02, 'final_norm': 8.024347e+00}


def _to_microbatches(a, axis):
    t = _jnp.moveaxis(a, axis, 0)
    t = t.reshape((N_MICROBATCH, t.shape[0] // N_MICROBATCH) + t.shape[1:])
    return _jnp.moveaxis(t, 1, axis + 1)


def setup_inputs(seed: int = 0) -> dict:
    inp = _fwd_setup_inputs(seed)
    key = _jax.random.fold_in(_jax.random.key(seed), 7919)
    shape, _ = _output_shape()
    out = dict(inp)
    out["loss_target"] = _jax.random.normal(_jax.random.fold_in(key, 0), shape, _jnp.float32)
    for i, name in enumerate(TWIN_WEIGHTS):
        w = inp[name].astype(_jnp.float32)
        if MOMENT_SCALE is None:
            s = _jnp.sqrt(_jnp.mean(_jnp.square(w)) + 1e-30)
        else:
            s = MOMENT_SCALE[name]
        km, kv = _jax.random.split(_jax.random.fold_in(key, i + 1))
        out[name] = w
        out["m_" + name] = s * _jax.random.normal(km, w.shape, _jnp.float32)
        out["v_" + name] = (s * s) * _jax.random.uniform(kv, w.shape, _jnp.float32, 0.5, 1.5)
    if N_MICROBATCH > 1:
        for name, axis in PER_EXAMPLE_BATCH_AXIS.items():
            out[name] = _to_microbatches(out[name], axis)
    return {'x': out['x'], 'c': out['c'], 'ctx': out['ctx'], 'c_ctx': out['c_ctx'], 'w_mod': out['w_mod'], 'b_mod': out['b_mod'], 'norm_ffn1': out['norm_ffn1'], 'norm_mix': out['norm_mix'], 'norm_ffn2': out['norm_ffn2'], 'ffn1_w_gate': out['ffn1_w_gate'], 'ffn1_w_up': out['ffn1_w_up'], 'ffn1_w_down': out['ffn1_w_down'], 'ffn2_w_gate': out['ffn2_w_gate'], 'ffn2_w_up': out['ffn2_w_up'], 'ffn2_w_down': out['ffn2_w_down'], 'mix_w_in': out['mix_w_in'], 'mix_w_conv': out['mix_w_conv'], 'mix_w_out': out['mix_w_out'], 'ret_decay_fwd': out['ret_decay_fwd'], 'ret_decay_bwd': out['ret_decay_bwd'], 'pool_w': out['pool_w'], 'pool_scale': out['pool_scale'], 'final_norm': out['final_norm'], 'loss_target': out['loss_target'], 'm_c_ctx': out['m_c_ctx'], 'm_w_mod': out['m_w_mod'], 'm_b_mod': out['m_b_mod'], 'm_norm_ffn1': out['m_norm_ffn1'], 'm_norm_mix': out['m_norm_mix'], 'm_norm_ffn2': out['m_norm_ffn2'], 'm_ffn1_w_gate': out['m_ffn1_w_gate'], 'm_ffn1_w_up': out['m_ffn1_w_up'], 'm_ffn1_w_down': out['m_ffn1_w_down'], 'm_ffn2_w_gate': out['m_ffn2_w_gate'], 'm_ffn2_w_up': out['m_ffn2_w_up'], 'm_ffn2_w_down': out['m_ffn2_w_down'], 'm_mix_w_in': out['m_mix_w_in'], 'm_mix_w_conv': out['m_mix_w_conv'], 'm_mix_w_out': out['m_mix_w_out'], 'm_ret_decay_fwd': out['m_ret_decay_fwd'], 'm_ret_decay_bwd': out['m_ret_decay_bwd'], 'm_pool_w': out['m_pool_w'], 'm_pool_scale': out['m_pool_scale'], 'm_final_norm': out['m_final_norm'], 'v_c_ctx': out['v_c_ctx'], 'v_w_mod': out['v_w_mod'], 'v_b_mod': out['v_b_mod'], 'v_norm_ffn1': out['v_norm_ffn1'], 'v_norm_mix': out['v_norm_mix'], 'v_norm_ffn2': out['v_norm_ffn2'], 'v_ffn1_w_gate': out['v_ffn1_w_gate'], 'v_ffn1_w_up': out['v_ffn1_w_up'], 'v_ffn1_w_down': out['v_ffn1_w_down'], 'v_ffn2_w_gate': out['v_ffn2_w_gate'], 'v_ffn2_w_up': out['v_ffn2_w_up'], 'v_ffn2_w_down': out['v_ffn2_w_down'], 'v_mix_w_in': out['v_mix_w_in'], 'v_mix_w_conv': out['v_mix_w_conv'], 'v_mix_w_out': out['v_mix_w_out'], 'v_ret_decay_fwd': out['v_ret_decay_fwd'], 'v_ret_decay_bwd': out['v_ret_decay_bwd'], 'v_pool_w': out['v_pool_w'], 'v_pool_scale': out['v_pool_scale'], 'v_final_norm': out['v_final_norm']}


def _loss(weights, diff, rest, loss_target):
    with _jax.named_scope("forward"):
        args = {**rest, TWIN_DIFF_INPUT: diff, **{k: w.astype(_WEIGHT_DTYPES[k]) for k, w in weights.items()}}
        y = _forward(args)
    with _jax.named_scope("loss_head"):
        err = _jnp.square(y.astype(_jnp.float32) - loss_target)
        return 0.5 * _jnp.sum(_jnp.mean(err, axis=-1)) if err.ndim else 0.5 * err


def _adamw(w, g, m, v):
    m = ADAM_B1 * m + (1.0 - ADAM_B1) * g
    v = ADAM_B2 * v + (1.0 - ADAM_B2) * _jnp.square(g)
    m_hat = m / (1.0 - ADAM_B1 ** ADAM_STEP)
    v_hat = v / (1.0 - ADAM_B2 ** ADAM_STEP)
    delta = -ADAM_LR * (m_hat / (_jnp.sqrt(v_hat) + ADAM_EPS) + ADAM_WD * w)
    return delta, m, v


def reference(x, c, ctx, c_ctx, w_mod, b_mod, norm_ffn1, norm_mix, norm_ffn2, ffn1_w_gate, ffn1_w_up, ffn1_w_down, ffn2_w_gate, ffn2_w_up, ffn2_w_down, mix_w_in, mix_w_conv, mix_w_out, ret_decay_fwd, ret_decay_bwd, pool_w, pool_scale, final_norm, loss_target, m_c_ctx, m_w_mod, m_b_mod, m_norm_ffn1, m_norm_mix, m_norm_ffn2, m_ffn1_w_gate, m_ffn1_w_up, m_ffn1_w_down, m_ffn2_w_gate, m_ffn2_w_up, m_ffn2_w_down, m_mix_w_in, m_mix_w_conv, m_mix_w_out, m_ret_decay_fwd, m_ret_decay_bwd, m_pool_w, m_pool_scale, m_final_norm, v_c_ctx, v_w_mod, v_b_mod, v_norm_ffn1, v_norm_mix, v_norm_ffn2, v_ffn1_w_gate, v_ffn1_w_up, v_ffn1_w_down, v_ffn2_w_gate, v_ffn2_w_up, v_ffn2_w_down, v_mix_w_in, v_mix_w_conv, v_mix_w_out, v_ret_decay_fwd, v_ret_decay_bwd, v_pool_w, v_pool_scale, v_final_norm):
    given = dict(x=x, c=c, ctx=ctx, c_ctx=c_ctx, w_mod=w_mod, b_mod=b_mod, norm_ffn1=norm_ffn1, norm_mix=norm_mix, norm_ffn2=norm_ffn2, ffn1_w_gate=ffn1_w_gate, ffn1_w_up=ffn1_w_up, ffn1_w_down=ffn1_w_down, ffn2_w_gate=ffn2_w_gate, ffn2_w_up=ffn2_w_up, ffn2_w_down=ffn2_w_down, mix_w_in=mix_w_in, mix_w_conv=mix_w_conv, mix_w_out=mix_w_out, ret_decay_fwd=ret_decay_fwd, ret_decay_bwd=ret_decay_bwd, pool_w=pool_w, pool_scale=pool_scale, final_norm=final_norm, loss_target=loss_target, m_c_ctx=m_c_ctx, m_w_mod=m_w_mod, m_b_mod=m_b_mod, m_norm_ffn1=m_norm_ffn1, m_norm_mix=m_norm_mix, m_norm_ffn2=m_norm_ffn2, m_ffn1_w_gate=m_ffn1_w_gate, m_ffn1_w_up=m_ffn1_w_up, m_ffn1_w_down=m_ffn1_w_down, m_ffn2_w_gate=m_ffn2_w_gate, m_ffn2_w_up=m_ffn2_w_up, m_ffn2_w_down=m_ffn2_w_down, m_mix_w_in=m_mix_w_in, m_mix_w_conv=m_mix_w_conv, m_mix_w_out=m_mix_w_out, m_ret_decay_fwd=m_ret_decay_fwd, m_ret_decay_bwd=m_ret_decay_bwd, m_pool_w=m_pool_w, m_pool_scale=m_pool_scale, m_final_norm=m_final_norm, v_c_ctx=v_c_ctx, v_w_mod=v_w_mod, v_b_mod=v_b_mod, v_norm_ffn1=v_norm_ffn1, v_norm_mix=v_norm_mix, v_norm_ffn2=v_norm_ffn2, v_ffn1_w_gate=v_ffn1_w_gate, v_ffn1_w_up=v_ffn1_w_up, v_ffn1_w_down=v_ffn1_w_down, v_ffn2_w_gate=v_ffn2_w_gate, v_ffn2_w_up=v_ffn2_w_up, v_ffn2_w_down=v_ffn2_w_down, v_mix_w_in=v_mix_w_in, v_mix_w_conv=v_mix_w_conv, v_mix_w_out=v_mix_w_out, v_ret_decay_fwd=v_ret_decay_fwd, v_ret_decay_bwd=v_ret_decay_bwd, v_pool_w=v_pool_w, v_pool_scale=v_pool_scale, v_final_norm=v_final_norm)
    weights = {n: given[n] for n in TWIN_WEIGHTS}
    shared = {n: given[n] for n in SHARED_INPUTS}
    per_example = {n: given[n] for n in ['x', 'c', 'ctx']}
    grad_fn = _jax.value_and_grad(_loss, argnums=(0, 1))

    def one_microbatch(ex, loss_target):
        ex = dict(ex)
        diff = ex.pop(TWIN_DIFF_INPUT)
        return grad_fn(weights, diff, {**shared, **ex}, loss_target)

    if N_MICROBATCH == 1:
        loss, (grad_w, grad_x) = one_microbatch(per_example, given["loss_target"])
    else:
        def body(carry, xs):
            loss_sum, grad_sum = carry
            l_k, (gw_k, gx_k) = one_microbatch(xs[0], xs[1])
            with _jax.named_scope("update"):
                return (loss_sum + l_k, _jax.tree.map(_jnp.add, grad_sum, gw_k)), gx_k

        init = (_jnp.zeros((), _jnp.float32), _jax.tree.map(_jnp.zeros_like, weights))
        (loss, grad_w), grad_x = _jax.lax.scan(body, init, (per_example, given["loss_target"]))
    with _jax.named_scope("update"):
        delta_w, new_m, new_v = {}, {}, {}
        for n in TWIN_WEIGHTS:
            delta_w[n], new_m[n], new_v[n] = _adamw(weights[n], grad_w[n], given["m_" + n], given["v_" + n])
    return (loss, grad_x, *[grad_w[n] for n in TWIN_WEIGHTS], *[delta_w[n] for n in TWIN_WEIGHTS],
            *[new_m[n] for n in TWIN_WEIGHTS], *[new_v[n] for n in TWIN_WEIGHTS])
```

```python
import jax
import jax.numpy as jnp
from jax import lax
from jax.experimental import pallas as pl
from jax.experimental.pallas import tpu as pltpu

F32 = jnp.float32
BF16 = jnp.bfloat16
MESH = pl.DeviceIdType.MESH

N_DEV = 8
N_MOD = 9
EPS = 1e-6
GN_EPS = 1e-5
MACARON = 0.5
HEAD_DIM = 128
K_SCALE = HEAD_DIM ** -0.5
ROPE_BASE = 10000.0
GRID_W = 64
CONV_WIDTH = 3
POOL_WINDOWS = (2, 4, 8, 16)
POOL_PAD = 16

ADAM_LR = 0.001
ADAM_B1 = 0.9
ADAM_B2 = 0.999
ADAM_EPS = 1e-08
ADAM_WD = 0.01
ADAM_STEP = 10

LANE = 128
VMEM_LIMIT = 56 * 1024 * 1024
ANY = pl.BlockSpec(memory_space=pl.ANY)


def _params(sem=None):
    kw = dict(vmem_limit_bytes=VMEM_LIMIT)
    if sem is not None:
        kw["dimension_semantics"] = sem
    return pltpu.CompilerParams(**kw)


def _nt(a, b):
    return lax.dot_general(a, b, (((1,), (1,)), ((), ())), preferred_element_type=F32)


def _tn(a, b):
    return lax.dot_general(a, b, (((0,), (0,)), ((), ())), preferred_element_type=F32)


def _nn(a, b):
    return jnp.dot(a, b, preferred_element_type=F32)


def _silu_parts(a):
    sig = jax.nn.sigmoid(a)
    return sig, a * sig


def _dsilu(a, sig):
    return sig * (1.0 + a * (1.0 - sig))


def _norm_mod(x, g, shift, scale):
    r = lax.rsqrt(jnp.mean(x * x, axis=-1, keepdims=True) + EPS)
    return (x * r * g) * (1.0 + scale) + shift


def _norm_mod_bwd(x, dh, g, scale):
    r = lax.rsqrt(jnp.mean(x * x, axis=-1, keepdims=True) + EPS)
    xhat = x * r
    dn = dh * (1.0 + scale)
    dshift = jnp.sum(dh, axis=0, keepdims=True)
    dscale = jnp.sum(dh * (xhat * g), axis=0, keepdims=True)
    dg = jnp.sum(dn * xhat, axis=0, keepdims=True)
    dxh = dn * g
    dx = r * (dxh - xhat * jnp.mean(dxh * xhat, axis=-1, keepdims=True))
    return dx, dshift, dscale, dg


def _group_index(i, tiles_first, n_groups):
    if n_groups == 1:
        return 0
    return jnp.where(i >= tiles_first, 1, 0)


def _mesh_pos():
    x, y, c = lax.axis_index("x"), lax.axis_index("y"), lax.axis_index("c")
    chips = [(1 - x, y), (x, 1 - y), (1 - x, 1 - y)]
    return x, y, c, chips


def _flat(px, py, pc):
    return 4 * px + 2 * py + pc


def all_gather(shards, name):
    n = len(shards)

    def body(*refs):
        ins, outs = refs[:n], refs[n:2 * n]
        send_sems, recv_sems, local_sems = refs[2 * n:]
        x, y, c, chips = _mesh_pos()
        me, sibling = (x, y, c), (x, y, 1 - c)

        def copy(t, k, block, to, src=None):
            dst = outs[t].at[_flat(*block)]
            return pltpu.make_async_remote_copy(
                src_ref=dst if src is None else src, dst_ref=dst,
                send_sem=send_sems.at[t, k], recv_sem=recv_sems.at[t, k],
                device_id=to, device_id_type=MESH)

        mine = [pltpu.make_async_copy(ins[t], outs[t].at[_flat(*me)], local_sems.at[t]) for t in range(n)]
        for cp in mine:
            cp.start()
        first = []
        for t in range(n):
            first.append(copy(t, 0, me, sibling, src=ins[t]))
            first += [copy(t, 1 + j, me, (*chip, c), src=ins[t]) for j, chip in enumerate(chips)]
        for cp in first:
            cp.start()
        passed = []
        for t in range(n):
            for j, chip in enumerate(chips):
                copy(t, 1 + j, (*chip, c), me).wait_recv()
                fwd = copy(t, 4 + j, (*chip, c), sibling)
                fwd.start()
                passed.append(fwd)
        for t in range(n):
            copy(t, 0, sibling, me).wait_recv()
            for j, chip in enumerate(chips):
                copy(t, 4 + j, (*chip, 1 - c), me).wait_recv()
        for cp in first + passed:
            cp.wait_send()
        for cp in mine:
            cp.wait()

    return pl.pallas_call(
        body, name=name,
        out_shape=[jax.ShapeDtypeStruct((N_DEV,) + s.shape, s.dtype) for s in shards],
        in_specs=[ANY] * n, out_specs=[ANY] * n,
        scratch_shapes=[pltpu.SemaphoreType.DMA((n, 7)), pltpu.SemaphoreType.DMA((n, 7)), pltpu.SemaphoreType.DMA((n,))],
    )(*shards)


def rs_sibling_exchange(grads, name):
    n = len(grads)

    def body(*refs):
        ins, stages, directs = refs[:n], refs[n:2 * n], refs[2 * n:3 * n]
        send_sems, recv_sems = refs[3 * n:]
        x, y, c, chips = _mesh_pos()
        sibling = (x, y, 1 - c)

        def copy(t, k):
            if k < 3:
                src, dst = ins[t].at[_flat(*chips[k], 1 - c)], stages[t].at[k]
            else:
                src, dst = ins[t].at[_flat(x, y, 1 - c)], directs[t]
            return pltpu.make_async_remote_copy(
                src_ref=src, dst_ref=dst, send_sem=send_sems.at[t, k], recv_sem=recv_sems.at[t, k],
                device_id=sibling, device_id_type=MESH)

        cps = [copy(t, k) for t in range(n) for k in range(4)]
        for cp in cps:
            cp.start()
        for cp in cps:
            cp.wait_recv()
        for cp in cps:
            cp.wait_send()

    return pl.pallas_call(
        body, name=name,
        out_shape=[jax.ShapeDtypeStruct((3,) + g.shape[1:], g.dtype) for g in grads]
        + [jax.ShapeDtypeStruct(g.shape[1:], g.dtype) for g in grads],
        in_specs=[ANY] * n, out_specs=[ANY] * (2 * n),
        scratch_shapes=[pltpu.SemaphoreType.DMA((n, 4)), pltpu.SemaphoreType.DMA((n, 4))],
    )(*grads)


def rs_chip_exchange(sums, name):
    n = len(sums)

    def body(*refs):
        ins, outs = refs[:n], refs[n:2 * n]
        send_sems, recv_sems = refs[2 * n:]
        x, y, c, chips = _mesh_pos()

        def copy(t, j):
            return pltpu.make_async_remote_copy(
                src_ref=ins[t].at[j], dst_ref=outs[t].at[j], send_sem=send_sems.at[t, j], recv_sem=recv_sems.at[t, j],
                device_id=(*chips[j], c), device_id_type=MESH)

        cps = [copy(t, j) for t in range(n) for j in range(3)]
        for cp in cps:
            cp.start()
        for cp in cps:
            cp.wait_recv()
        for cp in cps:
            cp.wait_send()

    return pl.pallas_call(
        body, name=name,
        out_shape=[jax.ShapeDtypeStruct(s.shape, s.dtype) for s in sums],
        in_specs=[ANY] * n, out_specs=[ANY] * n,
        scratch_shapes=[pltpu.SemaphoreType.DMA((n, 3)), pltpu.SemaphoreType.DMA((n, 3))],
    )(*sums)


def _row_tile(rows, cols, limit_bytes=3 << 19):
    best = None
    for t in range(16, rows + 1, 16):
        if rows % t == 0 and t * cols * 4 <= limit_bytes:
            best = t
    return best if best is not None else rows


def chip_sum(grad, stage, peer_blocks, name):
    _, R, C = grad.shape
    tr = _row_tile(R, C)

    def body(idx_ref, g_ref, s_ref, o_ref):
        o_ref[...] = (g_ref[...].astype(F32) + s_ref[...].astype(F32)).astype(BF16)

    return pl.pallas_call(
        body, name=name,
        grid_spec=pltpu.PrefetchScalarGridSpec(
            num_scalar_prefetch=1, grid=(3, R // tr),
            in_specs=[pl.BlockSpec((None, tr, C), lambda j, i, idx: (idx[j], i, 0)),
                      pl.BlockSpec((None, tr, C), lambda j, i, idx: (j, i, 0))],
            out_specs=pl.BlockSpec((None, tr, C), lambda j, i, idx: (j, i, 0))),
        out_shape=jax.ShapeDtypeStruct((3, R, C), BF16),
        compiler_params=_params(("arbitrary", "arbitrary")),
    )(peer_blocks, grad, stage)


def _adamw(w, g, m, v):
    m2 = ADAM_B1 * m + (1.0 - ADAM_B1) * g
    v2 = ADAM_B2 * v + (1.0 - ADAM_B2) * (g * g)
    m_hat = m2 / (1.0 - ADAM_B1 ** ADAM_STEP)
    v_hat = v2 / (1.0 - ADAM_B2 ** ADAM_STEP)
    delta = -ADAM_LR * (m_hat / (jnp.sqrt(v_hat) + ADAM_EPS) + ADAM_WD * w)
    return delta, m2, v2


def adam_reduced(own_block, grad, direct, landed, w, m, v, layer, prev, name):
    L, R, C = w.shape
    tr = _row_tile(R, C, 1 << 20)
    first = prev is None

    def body(idx_ref, g_ref, d_ref, l_ref, w_ref, m_ref, v_ref, *rest):
        og, od, om, ov = rest[-4:]
        g = g_ref[...].astype(F32) + d_ref[...].astype(F32)
        for j in range(3):
            g = g + l_ref[j].astype(F32)
        delta, m2, v2 = _adamw(w_ref[...], g, m_ref[...], v_ref[...])
        og[...] = g
        od[...] = delta
        om[...] = m2
        ov[...] = v2

    lay = pl.BlockSpec((None, tr, C), lambda i, idx: (layer, i, 0))
    in_specs = [pl.BlockSpec((None, tr, C), lambda i, idx: (idx[0], i, 0)),
                pl.BlockSpec((tr, C), lambda i, idx: (i, 0)),
                pl.BlockSpec((3, tr, C), lambda i, idx: (0, i, 0)),
                lay, lay, lay]
    args = [own_block, grad, direct, landed, w, m, v]
    aliases = {}
    if not first:
        in_specs += [ANY] * 4
        args += list(prev)
        aliases = {7 + k: k for k in range(4)}
    return pl.pallas_call(
        body, name=name,
        grid_spec=pltpu.PrefetchScalarGridSpec(
            num_scalar_prefetch=1, grid=(R // tr,), in_specs=in_specs, out_specs=[lay] * 4),
        out_shape=[jax.ShapeDtypeStruct((L, R, C), F32)] * 4,
        input_output_aliases=aliases,
        compiler_params=_params(("arbitrary",)),
    )(*args)


def sum_partials(parts, name):
    _, _, N = parts.shape
    tn = 8192

    def body(p_ref, o_ref):
        g = p_ref[0]
        for k in range(1, N_DEV):
            g = g + p_ref[k]
        o_ref[...] = g

    return pl.pallas_call(
        body, name=name, grid=(N // tn,),
        in_specs=[pl.BlockSpec((N_DEV, 1, tn), lambda i: (0, 0, i))],
        out_specs=pl.BlockSpec((1, tn), lambda i: (0, i)),
        out_shape=jax.ShapeDtypeStruct((1, N), F32),
    )(parts)


def adam_plain(g, w, m, v, name):
    _, N = g.shape

    def body(g_ref, w_ref, m_ref, v_ref, od, om, ov):
        delta, m2, v2 = _adamw(w_ref[...], g_ref[...], m_ref[...], v_ref[...])
        od[...] = delta
        om[...] = m2
        ov[...] = v2

    return pl.pallas_call(
        body, name=name, out_shape=[jax.ShapeDtypeStruct((1, N), F32)] * 3,
    )(g, w, m, v)


def adaln_fwd(cond16, w_mod, b_blk, name):
    L, D, W = w_mod.shape
    tn = 768 if W % 768 == 0 else W

    def body(c_ref, w_ref, b_ref, o_ref):
        c = c_ref[...]
        sc = (c * jax.nn.sigmoid(c)).astype(BF16)
        o_ref[...] = _nn(sc, w_ref[...].astype(BF16)) + b_ref[...]

    return pl.pallas_call(
        body, name=name, grid=(L, W // tn),
        in_specs=[pl.BlockSpec((16, D), lambda l, i: (0, 0)),
                  pl.BlockSpec((None, D, tn), lambda l, i: (l, 0, i)),
                  pl.BlockSpec((None, 1, tn), lambda l, i: (l, 0, i))],
        out_specs=pl.BlockSpec((None, 16, tn), lambda l, i: (l, 0, i)),
        out_shape=jax.ShapeDtypeStruct((L, 16, W), F32),
        compiler_params=_params(("arbitrary", "arbitrary")),
    )(cond16, w_mod, b_blk)


def adaln_bwd(cond16, dm_x, dm_c, w_mod, m_mod, v_mod, name):
    L, D, W = w_mod.shape
    tn = 256 if W % 256 == 0 else W
    nt = W // tn

    def body(c_ref, cT_ref, dx_ref, dc_ref, w_ref, m_ref, v_ref, og, od, om, ov, pc_ref):
        l, i = pl.program_id(0), pl.program_id(1)
        c = c_ref[...]
        sig, sl = _silu_parts(c)
        cT = cT_ref[...]
        sigT = jax.nn.sigmoid(cT)
        scT = (cT * sigT).astype(BF16)
        dmc = jnp.sum(dc_ref[...], axis=0, keepdims=True)
        dm16 = jnp.concatenate([dx_ref[...], jnp.broadcast_to(dmc, (8, tn))], axis=0)
        row = lax.broadcasted_iota(jnp.int32, (16, tn), 0)
        dm16 = jnp.where(row <= 8, dm16, 0.0).astype(BF16)
        w = w_ref[...]
        g = _nn(scT, dm16)
        delta, m2, v2 = _adamw(w, g, m_ref[...], v_ref[...])
        og[...] = g
        od[...] = delta
        om[...] = m2
        ov[...] = v2

        @pl.when((l == 0) & (i == 0))
        def _():
            pc_ref[...] = jnp.zeros_like(pc_ref)

        @pl.when(l == 0)
        def _():
            back = _nt(jnp.broadcast_to(dmc, (8, tn)).astype(BF16), w.astype(BF16))
            pc_ref[...] += back * _dsilu(c[8:9, :], sig[8:9, :])

    col = pl.BlockSpec((None, D, tn), lambda l, i: (l, 0, i))
    row8 = pl.BlockSpec((None, 8, tn), lambda l, i: (l, 0, i))
    return pl.pallas_call(
        body, name=name, grid=(L, nt),
        in_specs=[pl.BlockSpec((16, D), lambda l, i: (0, 0)), pl.BlockSpec((D, 16), lambda l, i: (0, 0)),
                  row8, row8, col, col, col],
        out_specs=[col, col, col, col, pl.BlockSpec((8, D), lambda l, i: (0, 0))],
        out_shape=[jax.ShapeDtypeStruct((L, D, W), F32)] * 4 + [jax.ShapeDtypeStruct((8, D), F32)],
        compiler_params=_params(("arbitrary", "arbitrary")),
    )(cond16, cond16.T, dm_x, dm_c, w_mod, m_mod, v_mod)


def ffn_fwd(x, mod, gn, wg, wu, wd, *, tm, t_first, name):
    T, D = x.shape
    nb, _, cw = wg.shape
    G = mod.shape[0]
    nt = T // tm
    first = t_first // tm

    def body(x_ref, mod_ref, g_ref, wg_ref, wu_ref, wd_ref, xo_ref, h_ref, a_ref, b_ref, y_ref, hs, acc):
        j = pl.program_id(1)

        @pl.when(j == 0)
        def _():
            hb = _norm_mod(x_ref[...], g_ref[...], mod_ref[0:1, :], mod_ref[1:2, :]).astype(BF16)
            hs[...] = hb
            h_ref[...] = hb
            acc[...] = jnp.zeros_like(acc)

        hb = hs[...]
        a = _nn(hb, wg_ref[...])
        b = _nn(hb, wu_ref[...])
        a_ref[...] = a.astype(BF16)
        b_ref[...] = b.astype(BF16)
        _, sl = _silu_parts(a)
        acc[...] += _nn((sl * b).astype(BF16), wd_ref[...])

        @pl.when(j == nb - 1)
        def _():
            y = acc[...]
            y_ref[...] = y.astype(BF16)
            xo_ref[...] = x_ref[...] + (MACARON * mod_ref[2:3, :]) * y

    tok = pl.BlockSpec((tm, D), lambda i, j: (i, 0))
    act = pl.BlockSpec((None, tm, cw), lambda i, j: (j, i, 0))
    return pl.pallas_call(
        body, name=name, grid=(nt, nb),
        in_specs=[tok, pl.BlockSpec((None, 3, D), lambda i, j: (_group_index(i, first, G), 0, 0)),
                  pl.BlockSpec((1, D), lambda i, j: (0, 0)),
                  pl.BlockSpec((None, D, cw), lambda i, j: (j, 0, 0)),
                  pl.BlockSpec((None, D, cw), lambda i, j: (j, 0, 0)),
                  pl.BlockSpec((None, cw, D), lambda i, j: (j, 0, 0))],
        out_specs=[tok, tok, act, act, tok],
        out_shape=[jax.ShapeDtypeStruct((T, D), F32), jax.ShapeDtypeStruct((T, D), BF16),
                   jax.ShapeDtypeStruct((nb, T, cw), BF16), jax.ShapeDtypeStruct((nb, T, cw), BF16),
                   jax.ShapeDtypeStruct((T, D), BF16)],
        scratch_shapes=[pltpu.VMEM((tm, D), BF16), pltpu.VMEM((tm, D), F32)],
        compiler_params=_params(("arbitrary", "arbitrary")),
    )(x, mod, gn, wg, wu, wd)


def ffn_bwd_tok(x, dxo, y, a, b, mod, gn, wg, wu, wd, *, tm, t_first, name):
    T, D = x.shape
    nb, _, cw = wg.shape
    G = mod.shape[0]
    nt = T // tm
    first = t_first // tm

    def body(x_ref, dxo_ref, y_ref, a_ref, b_ref, mod_ref, g_ref, wg_ref, wu_ref, wd_ref,
             dx_ref, da_ref, db_ref, s_ref, dy_ref, red_ref, dys, dh):
        i, j = pl.program_id(0), pl.program_id(1)

        @pl.when(j == 0)
        def _():
            dyb = ((MACARON * mod_ref[2:3, :]) * dxo_ref[...]).astype(BF16)
            dys[...] = dyb
            dy_ref[...] = dyb
            dh[...] = jnp.zeros_like(dh)

        ds = _nt(dys[...], wd_ref[...])
        av = a_ref[...].astype(F32)
        bv = b_ref[...].astype(F32)
        sig, sl = _silu_parts(av)
        s_ref[...] = (sl * bv).astype(BF16)
        dab = (ds * bv * _dsilu(av, sig)).astype(BF16)
        dbb = (ds * sl).astype(BF16)
        da_ref[...] = dab
        db_ref[...] = dbb
        dh[...] += _nt(dab, wg_ref[...]) + _nt(dbb, wu_ref[...])

        @pl.when((j == 0) & ((i == 0) | (i == first)))
        def _():
            red_ref[...] = jnp.zeros_like(red_ref)

        @pl.when(j == nb - 1)
        def _():
            dxo_v = dxo_ref[...]
            dxn, dshift, dscale, dg = _norm_mod_bwd(x_ref[...], dh[...], g_ref[...], mod_ref[1:2, :])
            dx_ref[...] = dxo_v + dxn
            red_ref[0:1, :] += dshift
            red_ref[1:2, :] += dscale
            red_ref[2:3, :] += jnp.sum((MACARON * dxo_v) * y_ref[...].astype(F32), axis=0, keepdims=True)
            red_ref[3:4, :] += dg

    tok = pl.BlockSpec((tm, D), lambda i, j: (i, 0))
    act = pl.BlockSpec((None, tm, cw), lambda i, j: (j, i, 0))
    wcol = pl.BlockSpec((None, D, cw), lambda i, j: (j, 0, 0))
    return pl.pallas_call(
        body, name=name, grid=(nt, nb),
        in_specs=[tok, tok, tok, act, act,
                  pl.BlockSpec((None, 3, D), lambda i, j: (_group_index(i, first, G), 0, 0)),
                  pl.BlockSpec((1, D), lambda i, j: (0, 0)), wcol, wcol,
                  pl.BlockSpec((None, cw, D), lambda i, j: (j, 0, 0))],
        out_specs=[tok, act, act, act, tok,
                   pl.BlockSpec((None, 8, D), lambda i, j: (_group_index(i, first, G), 0, 0))],
        out_shape=[jax.ShapeDtypeStruct((T, D), F32)] + [jax.ShapeDtypeStruct((nb, T, cw), BF16)] * 3
        + [jax.ShapeDtypeStruct((T, D), BF16), jax.ShapeDtypeStruct((G, 8, D), F32)],
        scratch_shapes=[pltpu.VMEM((tm, D), BF16), pltpu.VMEM((tm, D), F32)],
        compiler_params=_params(("arbitrary", "arbitrary")),
    )(x, dxo, y, a, b, mod, gn, wg, wu, wd)


def matmul_tn(a, b, *, nb, m, n, a_blocked, b_blocked, tk, name):
    T = a.shape[-2]

    def spec(arr, blocked, width):
        if blocked:
            return pl.BlockSpec((None, tk, width), lambda j, k: (j, k, 0))
        if arr.shape[-1] == width:
            return pl.BlockSpec((tk, width), lambda j, k: (k, 0))
        return pl.BlockSpec((tk, width), lambda j, k: (k, j))

    def body(a_ref, b_ref, o_ref, acc):
        k = pl.program_id(1)

        @pl.when(k == 0)
        def _():
            acc[...] = jnp.zeros_like(acc)

        acc[...] += _tn(a_ref[...], b_ref[...])

        @pl.when(k == T // tk - 1)
        def _():
            o_ref[...] = acc[...].astype(BF16)

    return pl.pallas_call(
        body, name=name, grid=(nb, T // tk),
        in_specs=[spec(a, a_blocked, m), spec(b, b_blocked, n)],
        out_specs=pl.BlockSpec((None, m, n), lambda j, k: (j, 0, 0)),
        out_shape=jax.ShapeDtypeStruct((nb, m, n), BF16),
        scratch_shapes=[pltpu.VMEM((m, n), F32)],
        compiler_params=_params(("arbitrary", "arbitrary")),
    )(a, b)


def proj_in(x, mod, gn, w_in, *, tm, t_first, name):
    T, D = x.shape
    nb, _, cw = w_in.shape
    G = mod.shape[0]
    nt = T // tm
    first = t_first // tm

    def body(x_ref, mod_ref, g_ref, w_ref, p_ref, h_ref, hs):
        @pl.when(pl.program_id(1) == 0)
        def _():
            hb = _norm_mod(x_ref[...], g_ref[...], mod_ref[0:1, :], mod_ref[1:2, :]).astype(BF16)
            hs[...] = hb
            h_ref[...] = hb

        p_ref[...] = _nn(hs[...], w_ref[...])

    tok = pl.BlockSpec((tm, D), lambda i, j: (i, 0))
    return pl.pallas_call(
        body, name=name, grid=(nt, nb),
        in_specs=[tok, pl.BlockSpec((None, 3, D), lambda i, j: (_group_index(i, first, G), 0, 0)),
                  pl.BlockSpec((1, D), lambda i, j: (0, 0)),
                  pl.BlockSpec((None, D, cw), lambda i, j: (j, 0, 0))],
        out_specs=[pl.BlockSpec((tm, cw), lambda i, j: (i, j)), tok],
        out_shape=[jax.ShapeDtypeStruct((T, nb * cw), F32), jax.ShapeDtypeStruct((T, D), BF16)],
        scratch_shapes=[pltpu.VMEM((tm, D), BF16)],
        compiler_params=_params(("arbitrary", "arbitrary")),
    )(x, mod, gn, w_in)


def proj_bwd_tok(x, dxo, dp, mod, gn, w_in, *, tm, name):
    T, D = x.shape
    nb, _, cw = w_in.shape
    G = mod.shape[0]
    nt = T // tm
    first = dxo.shape[0] // tm

    def body(x_ref, dxo_ref, dp_ref, mod_ref, g_ref, w_ref, dx_ref, red_ref, dh):
        i, j = pl.program_id(0), pl.program_id(1)

        @pl.when(j == 0)
        def _():
            dh[...] = jnp.zeros_like(dh)

        dh[...] += _nt(dp_ref[...], w_ref[...])

        @pl.when((j == 0) & ((i == 0) | (i == first)))
        def _():
            red_ref[...] = jnp.zeros_like(red_ref)

        @pl.when(j == nb - 1)
        def _():
            dxn, dshift, dscale, dg = _norm_mod_bwd(x_ref[...], dh[...], g_ref[...], mod_ref[1:2, :])
            dx_ref[...] = jnp.where(i < first, dxo_ref[...], 0.0) + dxn
            red_ref[0:1, :] += dshift
            red_ref[1:2, :] += dscale
            red_ref[3:4, :] += dg

    tok = pl.BlockSpec((tm, D), lambda i, j: (i, 0))
    return pl.pallas_call(
        body, name=name, grid=(nt, nb),
        in_specs=[tok, pl.BlockSpec((tm, D), lambda i, j: (jnp.minimum(i, first - 1), 0)),
                  pl.BlockSpec((tm, cw), lambda i, j: (i, j)),
                  pl.BlockSpec((None, 3, D), lambda i, j: (_group_index(i, first, G), 0, 0)),
                  pl.BlockSpec((1, D), lambda i, j: (0, 0)),
                  pl.BlockSpec((None, D, cw), lambda i, j: (j, 0, 0))],
        out_specs=[tok, pl.BlockSpec((None, 8, D), lambda i, j: (_group_index(i, first, G), 0, 0))],
        out_shape=[jax.ShapeDtypeStruct((T, D), F32), jax.ShapeDtypeStruct((G, 8, D), F32)],
        scratch_shapes=[pltpu.VMEM((tm, D), F32)],
        compiler_params=_params(("arbitrary", "arbitrary")),
    )(x, dxo, dp, mod, gn, w_in)


def _rope(t, cos, sin):
    return t * cos + pltpu.roll(t, HEAD_DIM // 2, axis=1) * sin


def ret_prep(p, cos, sin, *, heads, tm, name):
    T = p.shape[0]

    def body(q_ref, k_ref, v_ref, c_ref, s_ref, qo, ko, vo):
        cos_v, sin_v = c_ref[...], s_ref[...]
        qo[...] = _rope(q_ref[...], cos_v, sin_v).astype(BF16)
        ko[...] = _rope(k_ref[...] * K_SCALE, cos_v, sin_v).astype(BF16)
        vo[...] = v_ref[...].astype(BF16)

    def col(part):
        return pl.BlockSpec((tm, HEAD_DIM), lambda h, i: (i, part * heads + h))

    tab = pl.BlockSpec((tm, HEAD_DIM), lambda h, i: (i, 0))
    out = pl.BlockSpec((None, tm, HEAD_DIM), lambda h, i: (h, i, 0))
    return pl.pallas_call(
        body, name=name, grid=(heads, T // tm),
        in_specs=[col(0), col(1), col(2), tab, tab], out_specs=[out, out, out],
        out_shape=[jax.ShapeDtypeStruct((heads, T, HEAD_DIM), BF16)] * 3,
        compiler_params=_params(("arbitrary", "arbitrary")),
    )(p, p, p, cos, sin)


def _decay(n, m, lgf, lgb, t_lat, t_ctx):
    df = jnp.where(m < t_lat, n - m, n - m + (t_lat + t_ctx))
    db = m - n
    ef = jnp.where(df >= 0, jnp.exp(lgf * df), 0.0)
    eb = jnp.where(db >= 0, jnp.exp(lgb * db), 0.0)
    return ef, eb, df, db


def ret_fwd(q, k, v, lg, *, t_lat, tq, tk, name):
    H, T, _ = k.shape
    t_ctx = T - t_lat

    def body(lg_ref, q_ref, k_ref, v_ref, o_ref):
        h, qi, kj = pl.program_id(0), pl.program_id(1), pl.program_id(2)

        @pl.when(kj == 0)
        def _():
            o_ref[...] = jnp.zeros_like(o_ref)

        s = _nt(q_ref[...], k_ref[...])
        n = (qi * tq + lax.broadcasted_iota(jnp.int32, (tq, tk), 0)).astype(F32)
        m = (kj * tk + lax.broadcasted_iota(jnp.int32, (tq, tk), 1)).astype(F32)
        ef, eb, _, _ = _decay(n, m, lg_ref[0, h], lg_ref[1, h], t_lat, t_ctx)
        o_ref[...] += _nn((s * (ef + eb)).astype(BF16), v_ref[...])

    return pl.pallas_call(
        body, name=name, grid=(H, t_lat // tq, T // tk),
        in_specs=[pl.BlockSpec(memory_space=pltpu.SMEM),
                  pl.BlockSpec((None, tq, HEAD_DIM), lambda h, i, j: (h, i, 0)),
                  pl.BlockSpec((None, tk, HEAD_DIM), lambda h, i, j: (h, j, 0)),
                  pl.BlockSpec((None, tk, HEAD_DIM), lambda h, i, j: (h, j, 0))],
        out_specs=pl.BlockSpec((None, tq, HEAD_DIM), lambda h, i, j: (h, i, 0)),
        out_shape=jax.ShapeDtypeStruct((H, t_lat, HEAD_DIM), F32),
        compiler_params=_params(("arbitrary", "arbitrary", "arbitrary")),
    )(lg, q, k, v)


def ret_bwd(q, k, v, do, lg, *, t_lat, tq, tk, name):
    H, T, _ = k.shape
    t_ctx = T - t_lat

    def body(lg_ref, q_ref, k_ref, v_ref, do_ref, dq_ref, dk_ref, dv_ref, dlg_ref):
        h, kj, qi = pl.program_id(0), pl.program_id(1), pl.program_id(2)

        @pl.when((kj == 0) & (qi == 0))
        def _():
            dq_ref[...] = jnp.zeros_like(dq_ref)
            dlg_ref[...] = jnp.zeros_like(dlg_ref)

        @pl.when(qi == 0)
        def _():
            dk_ref[...] = jnp.zeros_like(dk_ref)
            dv_ref[...] = jnp.zeros_like(dv_ref)

        qv, kv, vv = q_ref[...], k_ref[...], v_ref[...]
        dob = do_ref[...].astype(BF16)
        st = _nt(kv, qv)
        dwt = _nt(vv, dob)
        m = (kj * tk + lax.broadcasted_iota(jnp.int32, (tk, tq), 0)).astype(F32)
        n = (qi * tq + lax.broadcasted_iota(jnp.int32, (tk, tq), 1)).astype(F32)
        ef, eb, df, db = _decay(n, m, lg_ref[0, h], lg_ref[1, h], t_lat, t_ctx)
        dec = ef + eb
        dv_ref[...] += _nn((st * dec).astype(BF16), dob)
        dst = (dwt * dec).astype(BF16)
        dk_ref[...] += _nn(dst, qv)
        rows = pl.ds(pl.multiple_of(qi * tq, tq), tq)
        dq_ref[rows, :] += _tn(dst, kv)
        gs = dwt * st
        dlg_ref[0:1, :] += jnp.sum(gs * (ef * df))
        dlg_ref[1:2, :] += jnp.sum(gs * (eb * db))

    kspec = pl.BlockSpec((None, tk, HEAD_DIM), lambda h, j, i: (h, j, 0))
    qspec = pl.BlockSpec((None, tq, HEAD_DIM), lambda h, j, i: (h, i, 0))
    return pl.pallas_call(
        body, name=name, grid=(H, T // tk, t_lat // tq),
        in_specs=[pl.BlockSpec(memory_space=pltpu.SMEM), qspec, kspec, kspec, qspec],
        out_specs=[pl.BlockSpec((None, t_lat, HEAD_DIM), lambda h, j, i: (h, 0, 0)), kspec, kspec,
                   pl.BlockSpec((None, 8, LANE), lambda h, j, i: (h, 0, 0))],
        out_shape=[jax.ShapeDtypeStruct((H, t_lat, HEAD_DIM), F32), jax.ShapeDtypeStruct((H, T, HEAD_DIM), F32),
                   jax.ShapeDtypeStruct((H, T, HEAD_DIM), F32), jax.ShapeDtypeStruct((H, 8, LANE), F32)],
        compiler_params=_params(("arbitrary", "arbitrary", "arbitrary")),
    )(lg, q, k, v, do)


def _group_norm(o):
    mu = jnp.mean(o, axis=-1, keepdims=True)
    ctr = o - mu
    r = lax.rsqrt(jnp.mean(ctr * ctr, axis=-1, keepdims=True) + GN_EPS)
    return ctr * r, r


def ret_post_fwd(o, p, *, heads, d_model, name):
    H, t_lat, _ = o.shape
    T = p.shape[0]

    def body(o_ref, g_ref, z_ref):
        on, _ = _group_norm(o_ref[...])
        _, sl = _silu_parts(g_ref[0:t_lat, :])
        z_ref[...] = (on * sl).astype(BF16)

    return pl.pallas_call(
        body, name=name, grid=(H,),
        in_specs=[pl.BlockSpec((None, t_lat, HEAD_DIM), lambda h: (h, 0, 0)),
                  pl.BlockSpec((T, HEAD_DIM), lambda h: (0, 3 * heads + h))],
        out_specs=pl.BlockSpec((t_lat, HEAD_DIM), lambda h: (0, h)),
        out_shape=jax.ShapeDtypeStruct((t_lat, d_model), BF16),
        compiler_params=_params(("arbitrary",)),
    )(o, p)


def _shift_rows(u, k):
    rows = u.shape[0]
    row = lax.broadcasted_iota(jnp.int32, u.shape, 0)
    rolled = pltpu.roll(u, k % rows, axis=0)
    return jnp.where((row >= k) & (row < rows + k), rolled, 0.0)


def conv_fwd(z, p, w_conv, *, heads, t_lat, name):
    T = p.shape[0]
    cb_n = w_conv.shape[1] // LANE
    base = 4 * heads

    def body(z_in, bg_ref, cg_ref, u_ref, w_ref, z_ref):
        cu = cg_ref[0:t_lat, :] * u_ref[0:t_lat, :]
        c3 = _shift_rows(cu, 1) * w_ref[0:1, :] + cu * w_ref[1:2, :] + _shift_rows(cu, -1) * w_ref[2:3, :]
        z_ref[...] = (bg_ref[0:t_lat, :] * c3).astype(BF16)

    def col(part):
        return pl.BlockSpec((T, LANE), lambda cb: (0, base + part * cb_n + cb))

    return pl.pallas_call(
        body, name=name, grid=(cb_n,),
        in_specs=[ANY, col(0), col(1), col(2), pl.BlockSpec((CONV_WIDTH, LANE), lambda cb: (0, cb))],
        out_specs=pl.BlockSpec((t_lat, LANE), lambda cb: (0, heads + cb)),
        out_shape=jax.ShapeDtypeStruct(z.shape, z.dtype),
        input_output_aliases={0: 0},
        compiler_params=_params(("arbitrary",)),
    )(z, p, p, p, w_conv)


def out_proj(z, w_out, x, mod, *, tm, name):
    t_lat, D = z.shape

    def body(z_ref, w_ref, x_ref, mod_ref, xo_ref, y_ref):
        y = _nn(z_ref[...], w_ref[...])
        y_ref[...] = y.astype(BF16)
        xo_ref[...] = x_ref[...] + mod_ref[2:3, :] * y

    tok = pl.BlockSpec((tm, D), lambda i: (i, 0))
    return pl.pallas_call(
        body, name=name, grid=(t_lat // tm,),
        in_specs=[tok, pl.BlockSpec((D, D), lambda i: (0, 0)), tok, pl.BlockSpec((None, 3, D), lambda i: (0, 0, 0))],
        out_specs=[tok, tok],
        out_shape=[jax.ShapeDtypeStruct((t_lat, D), F32), jax.ShapeDtypeStruct((t_lat, D), BF16)],
        compiler_params=_params(("arbitrary",)),
    )(z, w_out, x, mod)


def out_proj_bwd(dxo, y, w_out, mod, *, tm, name):
    t_lat, D = dxo.shape

    def body(dxo_ref, y_ref, w_ref, mod_ref, dz_ref, dy_ref, red_ref):
        @pl.when(pl.program_id(0) == 0)
        def _():
            red_ref[...] = jnp.zeros_like(red_ref)

        dxo_v = dxo_ref[...]
        dyb = (mod_ref[2:3, :] * dxo_v).astype(BF16)
        dy_ref[...] = dyb
        dz_ref[...] = _nt(dyb, w_ref[...])
        red_ref[2:3, :] += jnp.sum(dxo_v * y_ref[...].astype(F32), axis=0, keepdims=True)

    tok = pl.BlockSpec((tm, D), lambda i: (i, 0))
    return pl.pallas_call(
        body, name=name, grid=(t_lat // tm,),
        in_specs=[tok, tok, pl.BlockSpec((D, D), lambda i: (0, 0)), pl.BlockSpec((None, 3, D), lambda i: (0, 0, 0))],
        out_specs=[tok, tok, pl.BlockSpec((8, D), lambda i: (0, 0))],
        out_shape=[jax.ShapeDtypeStruct((t_lat, D), F32), jax.ShapeDtypeStruct((t_lat, D), BF16),
                   jax.ShapeDtypeStruct((8, D), F32)],
        compiler_params=_params(("arbitrary",)),
    )(dxo, y, w_out, mod)


def ret_post_bwd(dz, o, p, *, heads, name):
    H, t_lat, _ = o.shape
    T, in_w = p.shape

    def body(dz_ref, o_ref, g_ref, do_ref, dp_ref):
        on, r = _group_norm(o_ref[...])
        gg = g_ref[0:t_lat, :]
        sig, sl = _silu_parts(gg)
        dret = dz_ref[...]
        don = dret * sl
        do_ref[...] = r * (don - jnp.mean(don, axis=-1, keepdims=True)
                           - on * jnp.mean(don * on, axis=-1, keepdims=True))
        dp_ref[0:t_lat, :] = (dret * on * _dsilu(gg, sig)).astype(BF16)
        dp_ref[t_lat:T, :] = jnp.zeros((T - t_lat, HEAD_DIM), BF16)

    return pl.pallas_call(
        body, name=name, grid=(H,),
        in_specs=[pl.BlockSpec((t_lat, HEAD_DIM), lambda h: (0, h)),
                  pl.BlockSpec((None, t_lat, HEAD_DIM), lambda h: (h, 0, 0)),
                  pl.BlockSpec((T, HEAD_DIM), lambda h: (0, 3 * heads + h))],
        out_specs=[pl.BlockSpec((None, t_lat, HEAD_DIM), lambda h: (h, 0, 0)),
                   pl.BlockSpec((T, HEAD_DIM), lambda h: (0, 3 * heads + h))],
        out_shape=[jax.ShapeDtypeStruct((H, t_lat, HEAD_DIM), F32), jax.ShapeDtypeStruct((T, in_w), BF16)],
        compiler_params=_params(("arbitrary",)),
    )(dz, o, p)


def conv_bwd(dp, dz, p, w_conv, *, heads, t_lat, name):
    T = p.shape[0]
    cb_n = w_conv.shape[1] // LANE
    base = 4 * heads

    def body(dp_in, dz_ref, bg_ref, cg_ref, u_ref, w_ref, dp_ref, dw_ref):
        part = pl.program_id(1)
        cg, u = cg_ref[0:t_lat, :], u_ref[0:t_lat, :]
        cu = cg * u
        dconv = dz_ref[...]
        dp_ref[t_lat:T, :] = jnp.zeros((T - t_lat, LANE), BF16)

        @pl.when(part == 0)
        def _():
            c3 = _shift_rows(cu, 1) * w_ref[0:1, :] + cu * w_ref[1:2, :] + _shift_rows(cu, -1) * w_ref[2:3, :]
            dp_ref[0:t_lat, :] = (dconv * c3).astype(BF16)
            dc3 = dconv * bg_ref[0:t_lat, :]
            dw_ref[0:1, :] = jnp.sum(dc3 * _shift_rows(cu, 1), axis=0, keepdims=True)
            dw_ref[1:2, :] = jnp.sum(dc3 * cu, axis=0, keepdims=True)
            dw_ref[2:3, :] = jnp.sum(dc3 * _shift_rows(cu, -1), axis=0, keepdims=True)

        @pl.when(part > 0)
        def _():
            dc3 = dconv * bg_ref[0:t_lat, :]
            dcu = (_shift_rows(dc3, -1) * w_ref[0:1, :] + dc3 * w_ref[1:2, :] + _shift_rows(dc3, 1) * w_ref[2:3, :])
            dp_ref[0:t_lat, :] = (dcu * jnp.where(part == 1, u, cg)).astype(BF16)

    def col(part):
        return pl.BlockSpec((T, LANE), lambda cb, pt: (0, base + part * cb_n + cb))

    return pl.pallas_call(
        body, name=name, grid=(cb_n, 3),
        in_specs=[ANY, pl.BlockSpec((t_lat, LANE), lambda cb, pt: (0, heads + cb)), col(0), col(1), col(2),
                  pl.BlockSpec((CONV_WIDTH, LANE), lambda cb, pt: (0, cb))],
        out_specs=[pl.BlockSpec((T, LANE), lambda cb, pt: (0, base + pt * cb_n + cb)),
                   pl.BlockSpec((CONV_WIDTH, LANE), lambda cb, pt: (0, cb))],
        out_shape=[jax.ShapeDtypeStruct(dp.shape, dp.dtype), jax.ShapeDtypeStruct(w_conv.shape, F32)],
        input_output_aliases={0: 0},
        compiler_params=_params(("arbitrary", "arbitrary")),
    )(dp, dz, p, p, p, w_conv)


def ret_unprep(dp, dq, dk, dv, cos, sin, *, t_lat, tm, name):
    H, T, _ = dk.shape
    n_lat = t_lat // tm

    def body(dp_in, dq_ref, dk_ref, dv_ref, c_ref, s_ref, dp_ref):
        part, i = pl.program_id(0), pl.program_id(2)
        cos_v, sin_v = c_ref[...], s_ref[...]

        @pl.when(part == 0)
        def _():
            d = jnp.where(i < n_lat, dq_ref[...], 0.0)
            dp_ref[...] = _rope(d, cos_v, -sin_v).astype(BF16)

        @pl.when(part == 1)
        def _():
            dp_ref[...] = (_rope(dk_ref[...], cos_v, -sin_v) * K_SCALE).astype(BF16)

        @pl.when(part == 2)
        def _():
            dp_ref[...] = dv_ref[...].astype(BF16)

    full = pl.BlockSpec((None, tm, HEAD_DIM), lambda pt, h, i: (h, i, 0))
    tab = pl.BlockSpec((tm, HEAD_DIM), lambda pt, h, i: (i, 0))
    return pl.pallas_call(
        body, name=name, grid=(3, H, T // tm),
        in_specs=[ANY, pl.BlockSpec((None, tm, HEAD_DIM), lambda pt, h, i: (h, jnp.minimum(i, n_lat - 1), 0)),
                  full, full, tab, tab],
        out_specs=pl.BlockSpec((tm, HEAD_DIM), lambda pt, h, i: (i, pt * H + h)),
        out_shape=jax.ShapeDtypeStruct(dp.shape, dp.dtype),
        input_output_aliases={0: 0},
        compiler_params=_params(("arbitrary", "arbitrary", "arbitrary")),
    )(dp, dq, dk, dv, cos, sin)


def norm_mod_fwd(x, mod, gn, *, tm, name):
    T, D = x.shape

    def body(x_ref, mod_ref, g_ref, h_ref):
        h_ref[...] = _norm_mod(x_ref[...], g_ref[...], mod_ref[0:1, :], mod_ref[1:2, :])

    tok = pl.BlockSpec((tm, D), lambda i: (i, 0))
    return pl.pallas_call(
        body, name=name, grid=(T // tm,),
        in_specs=[tok, pl.BlockSpec((None, 3, D), lambda i: (0, 0, 0)), pl.BlockSpec((1, D), lambda i: (0, 0))],
        out_specs=tok, out_shape=jax.ShapeDtypeStruct((T, D), F32),
        compiler_params=_params(("arbitrary",)),
    )(x, mod, gn)


def norm_mod_bwd(x, dh, dxo, mod, gn, *, tm, name):
    T, D = x.shape

    def body(x_ref, dh_ref, dxo_ref, mod_ref, g_ref, dx_ref, red_ref):
        @pl.when(pl.program_id(0) == 0)
        def _():
            red_ref[...] = jnp.zeros_like(red_ref)

        dxn, dshift, dscale, dg = _norm_mod_bwd(x_ref[...], dh_ref[...], g_ref[...], mod_ref[1:2, :])
        dx_ref[...] = dxo_ref[...] + dxn
        red_ref[0:1, :] += dshift
        red_ref[1:2, :] += dscale
        red_ref[3:4, :] += dg

    tok = pl.BlockSpec((tm, D), lambda i: (i, 0))
    return pl.pallas_call(
        body, name=name, grid=(T // tm,),
        in_specs=[tok, tok, tok, pl.BlockSpec((None, 3, D), lambda i: (0, 0, 0)), pl.BlockSpec((1, D), lambda i: (0, 0))],
        out_specs=[tok, pl.BlockSpec((8, D), lambda i: (0, 0))],
        out_shape=[jax.ShapeDtypeStruct((T, D), F32), jax.ShapeDtypeStruct((8, D), F32)],
        compiler_params=_params(("arbitrary",)),
    )(x, dh, dxo, mod, gn)


def _window_sum(u, w, lead):
    T, C = u.shape
    ext = jnp.concatenate([u, jnp.zeros((POOL_PAD, C), F32)], axis=0)
    k = 1
    while k < w:
        ext = ext + _shift_rows(ext, k)
        k *= 2
    return _shift_rows(ext, -lead)[0:T, :]


def _window_count(T, C, w):
    t = lax.broadcasted_iota(jnp.int32, (T, C), 0)
    lo = jnp.clip(t - w // 2, 0, T)
    hi = jnp.clip(t + (w - w // 2), 0, T)
    return (hi - lo).astype(F32)


def pool_fwd(h, x, pool_w, scale, mod, *, name):
    T, D = h.shape
    G, Cg, _ = pool_w.shape
    ns = Cg // LANE

    def body(h_ref, x_ref, w_ref, sc_ref, mod_ref, xo_ref, pl_ref, yl_ref, acc):
        g, s = pl.program_id(0), pl.program_id(1)
        hv = h_ref[...]
        for gi, win in enumerate(POOL_WINDOWS):
            @pl.when(g == gi)
            def _():
                mean = _window_sum(hv, win, win // 2 - 1) / _window_count(T, LANE, win)
                pooled = (mean - hv).astype(BF16)
                pl_ref[...] = pooled
                contrib = _nn(pooled, w_ref[...])

                @pl.when(s == 0)
                def _():
                    acc[...] = contrib

                @pl.when(s > 0)
                def _():
                    acc[...] += contrib

        @pl.when(s == ns - 1)
        def _():
            yl = acc[...]
            yl_ref[...] = yl.astype(BF16)
            xo_ref[...] = x_ref[...] + mod_ref[2:3, :] * (yl * sc_ref[...])

    grp = pl.BlockSpec((T, Cg), lambda g, s: (0, g))
    sub = pl.BlockSpec((T, LANE), lambda g, s: (0, g * ns + s))
    return pl.pallas_call(
        body, name=name, grid=(G, ns),
        in_specs=[sub, grp, pl.BlockSpec((None, LANE, Cg), lambda g, s: (g, s, 0)),
                  pl.BlockSpec((1, Cg), lambda g, s: (0, g)), pl.BlockSpec((None, 3, Cg), lambda g, s: (0, 0, g))],
        out_specs=[grp, sub, grp],
        out_shape=[jax.ShapeDtypeStruct((T, D), F32), jax.ShapeDtypeStruct((T, D), BF16),
                   jax.ShapeDtypeStruct((T, D), BF16)],
        scratch_shapes=[pltpu.VMEM((T, Cg), F32)],
        compiler_params=_params(("arbitrary", "arbitrary")),
    )(h, x, pool_w, scale, mod)


def pool_bwd(dxo, pooled, yl, pool_w, scale, mod, *, name):
    T, D = dxo.shape
    G, Cg, _ = pool_w.shape
    ns = Cg // LANE

    def body(dxo_ref, pl_ref, yl_ref, w_ref, sc_ref, mod_ref, dh_ref, dw_ref, red_ref, dyl):
        g, s = pl.program_id(0), pl.program_id(1)

        @pl.when(s == 0)
        def _():
            dxo_v = dxo_ref[...]
            ylv = yl_ref[...].astype(F32)
            dy = mod_ref[2:3, :] * dxo_v
            dyl[...] = (dy * sc_ref[...]).astype(BF16)
            red_ref[...] = jnp.zeros_like(red_ref)
            red_ref[2:3, :] = jnp.sum(dxo_v * (ylv * sc_ref[...]), axis=0, keepdims=True)
            red_ref[4:5, :] = jnp.sum(dy * ylv, axis=0, keepdims=True)

        dylv = dyl[...]
        dpool = _nt(dylv, w_ref[...])
        dw_ref[...] = _tn(pl_ref[...], dylv).astype(BF16)
        for gi, win in enumerate(POOL_WINDOWS):
            @pl.when(g == gi)
            def _():
                spread = _window_sum(dpool / _window_count(T, LANE, win), win, win // 2)
                dh_ref[...] = spread - dpool

    grp = pl.BlockSpec((T, Cg), lambda g, s: (0, g))
    sub = pl.BlockSpec((T, LANE), lambda g, s: (0, g * ns + s))
    wsub = pl.BlockSpec((None, LANE, Cg), lambda g, s: (g, s, 0))
    return pl.pallas_call(
        body, name=name, grid=(G, ns),
        in_specs=[grp, sub, grp, wsub, pl.BlockSpec((1, Cg), lambda g, s: (0, g)),
                  pl.BlockSpec((None, 3, Cg), lambda g, s: (0, 0, g))],
        out_specs=[sub, wsub, pl.BlockSpec((8, Cg), lambda g, s: (0, g))],
        out_shape=[jax.ShapeDtypeStruct((T, D), F32), jax.ShapeDtypeStruct((G, Cg, Cg), BF16),
                   jax.ShapeDtypeStruct((8, D), F32)],
        scratch_shapes=[pltpu.VMEM((T, Cg), BF16)],
        compiler_params=_params(("arbitrary", "arbitrary")),
    )(dxo, pooled, yl, pool_w, scale, mod)


def final_loss(x, gn, target, *, tm, name):
    T, D = x.shape

    def body(x_ref, g_ref, t_ref, loss_ref, dx_ref, red_ref):
        @pl.when(pl.program_id(0) == 0)
        def _():
            loss_ref[...] = jnp.zeros_like(loss_ref)
            red_ref[...] = jnp.zeros_like(red_ref)

        xx, g = x_ref[...], g_ref[...]
        r = lax.rsqrt(jnp.mean(xx * xx, axis=-1, keepdims=True) + EPS)
        xhat = xx * r
        err = xhat * g - t_ref[...]
        loss_ref[...] += 0.5 * jnp.sum(jnp.mean(err * err, axis=-1, keepdims=True))
        dy = err / D
        red_ref[0:1, :] += jnp.sum(dy * xhat, axis=0, keepdims=True)
        dxh = dy * g
        dx_ref[...] = r * (dxh - xhat * jnp.mean(dxh * xhat, axis=-1, keepdims=True))

    tok = pl.BlockSpec((tm, D), lambda i: (i, 0))
    return pl.pallas_call(
        body, name=name, grid=(T // tm,),
        in_specs=[tok, pl.BlockSpec((1, D), lambda i: (0, 0)), tok],
        out_specs=[pl.BlockSpec((8, LANE), lambda i: (0, 0)), tok, pl.BlockSpec((8, D), lambda i: (0, 0))],
        out_shape=[jax.ShapeDtypeStruct((8, LANE), F32), jax.ShapeDtypeStruct((T, D), F32),
                   jax.ShapeDtypeStruct((8, D), F32)],
        compiler_params=_params(("arbitrary",)),
    )(x, gn, target)


def _rope_tables(t_lat, t_ctx):
    quarter = HEAD_DIM // 4
    pos = jnp.arange(t_lat)
    inv = ROPE_BASE ** (-jnp.arange(quarter, dtype=F32) / quarter)
    ang = jnp.concatenate([(pos // GRID_W).astype(F32)[:, None] * inv, (pos % GRID_W).astype(F32)[:, None] * inv], axis=-1)
    cos, sin = jnp.cos(ang), jnp.sin(ang)
    cos = jnp.concatenate([jnp.concatenate([cos, cos], axis=-1), jnp.ones((t_ctx, HEAD_DIM), F32)], axis=0)
    sin = jnp.concatenate([jnp.concatenate([-sin, sin], axis=-1), jnp.zeros((t_ctx, HEAD_DIM), F32)], axis=0)
    return cos, sin


def _ffn_grads(h, da, db, s, dy, tag):
    nb, T, cw = da.shape
    D = h.shape[1]
    tk = 512 if T % 512 == 0 else 256
    g_gate = matmul_tn(h, da, nb=nb, m=D, n=cw, a_blocked=False, b_blocked=True, tk=tk, name=f"wgrad_gate_{tag}")
    g_up = matmul_tn(h, db, nb=nb, m=D, n=cw, a_blocked=False, b_blocked=True, tk=tk, name=f"wgrad_up_{tag}")
    g_down = matmul_tn(s, dy, nb=nb, m=cw, n=D, a_blocked=True, b_blocked=False, tk=tk, name=f"wgrad_down_{tag}")
    return g_gate, g_up, g_down


def _reduce_scatter(grads, peer_blocks, tag):
    res = rs_sibling_exchange(grads, name=f"rs_sibling_{tag}")
    n = len(grads)
    stages, directs = res[:n], res[n:]
    sums = [chip_sum(g, st, peer_blocks, name=f"chip_sum_{tag}_{t}") for t, (g, st) in enumerate(zip(grads, stages))]
    landed = rs_chip_exchange(sums, name=f"rs_chip_{tag}")
    return list(zip(directs, landed))


def kernel(x, c, ctx, c_ctx, w_mod, b_mod, norm_ffn1, norm_mix, norm_ffn2, ffn1_w_gate, ffn1_w_up, ffn1_w_down, ffn2_w_gate, ffn2_w_up, ffn2_w_down, mix_w_in, mix_w_conv, mix_w_out, ret_decay_fwd, ret_decay_bwd, pool_w, pool_scale, final_norm, loss_target, m_c_ctx, m_w_mod, m_b_mod, m_norm_ffn1, m_norm_mix, m_norm_ffn2, m_ffn1_w_gate, m_ffn1_w_up, m_ffn1_w_down, m_ffn2_w_gate, m_ffn2_w_up, m_ffn2_w_down, m_mix_w_in, m_mix_w_conv, m_mix_w_out, m_ret_decay_fwd, m_ret_decay_bwd, m_pool_w, m_pool_scale, m_final_norm, v_c_ctx, v_w_mod, v_b_mod, v_norm_ffn1, v_norm_mix, v_norm_ffn2, v_ffn1_w_gate, v_ffn1_w_up, v_ffn1_w_down, v_ffn2_w_gate, v_ffn2_w_up, v_ffn2_w_down, v_mix_w_in, v_mix_w_conv, v_mix_w_out, v_ret_decay_fwd, v_ret_decay_bwd, v_pool_w, v_pool_scale, v_final_norm):
    t_lat, D = x.shape[1], x.shape[2]
    t_ctx = ctx.shape[1]
    T = t_lat + t_ctx
    heads = ret_decay_fwd.shape[1]
    mod_w = w_mod.shape[2]
    tm = 256
    tq = 512 if t_lat % 512 == 0 else 256
    tk = T // 3 if (T % 3 == 0 and (T // 3) % 256 == 0) else 256

    ax, ay, ac = lax.axis_index("x"), lax.axis_index("y"), lax.axis_index("c")
    me = 4 * ax + 2 * ay + ac
    own_block = jnp.reshape(me, (1,)).astype(jnp.int32)
    peer_blocks = jnp.stack([4 * (1 - ax) + 2 * ay + ac, 4 * ax + 2 * (1 - ay) + ac,
                             4 * (1 - ax) + 2 * (1 - ay) + ac]).astype(jnp.int32)

    (c_all,) = all_gather([c], name="gather_cond")
    cond16 = jnp.concatenate([c_all.reshape(N_DEV, D), c_ctx[None, :], jnp.zeros((7, D), F32)], axis=0)
    b_blk = lax.dynamic_slice_in_dim(b_mod, me * mod_w, mod_w, axis=1)[:, None, :]
    m_blk = adaln_fwd(cond16, w_mod, b_blk, name="adaln_fwd")
    (m_all,) = all_gather([m_blk], name="gather_mod")
    mods = jnp.transpose(m_all, (1, 2, 0, 3)).reshape(2, 16, N_MOD, D)
    mod_x = lax.dynamic_index_in_dim(mods, me, axis=1, keepdims=False)
    mod_c = mods[0, 8]

    def lay(w, l):
        return w[l].astype(BF16)

    gathered0 = all_gather(
        [lay(ffn1_w_gate, 0), lay(ffn1_w_up, 0), lay(ffn1_w_down, 0), lay(mix_w_in, 0), lay(mix_w_out, 0),
         mix_w_conv[0], lay(ffn2_w_gate, 0), lay(ffn2_w_up, 0), lay(ffn2_w_down, 0)], name="gather_weights_0")
    wg01, wu01, wd01, w_in, w_out, w_conv, wg02, wu02, wd02 = gathered0
    w_out = w_out.reshape(D, D)
    w_conv = jnp.transpose(w_conv, (1, 0, 2)).reshape(CONV_WIDTH, -1)
    gathered1 = all_gather(
        [lay(ffn1_w_gate, 1), lay(ffn1_w_up, 1), lay(ffn1_w_down, 1), lay(pool_w, 0), pool_scale,
         lay(ffn2_w_gate, 1), lay(ffn2_w_up, 1), lay(ffn2_w_down, 1)], name="gather_weights_1")
    wg11, wu11, wd11, pw, pscale, wg12, wu12, wd12 = gathered1
    n_grp, grp_rows, grp_w = pool_w.shape[1:]
    pw = jnp.transpose(pw, (1, 0, 2, 3)).reshape(n_grp, N_DEV * grp_rows, grp_w)
    pscale = pscale.reshape(1, D)

    lg = jnp.concatenate([jax.nn.log_sigmoid(ret_decay_fwd), jax.nn.log_sigmoid(ret_decay_bwd)], axis=0)
    cos, sin = _rope_tables(t_lat, t_ctx)

    def mod3(l, k, with_ctx=False):
        rows = mod_x[l, 3 * k:3 * k + 3][None]
        if with_ctx:
            rows = jnp.concatenate([rows, mod_c[3 * k:3 * k + 3][None]], axis=0)
        return rows

    x0 = jnp.concatenate([x[0], ctx[0]], axis=0)
    x1, h1, a1, b1, y1 = ffn_fwd(x0, mod3(0, 0, True), norm_ffn1[0:1], wg01, wu01, wd01, tm=tm, t_first=t_lat, name="ffn_fwd_01")
    p, hm = proj_in(x1, mod3(0, 1, True), norm_mix[0:1], w_in, tm=tm, t_first=t_lat, name="proj_in")
    qr, kr, vr = ret_prep(p, cos, sin, heads=heads, tm=tm, name="ret_prep")
    o = ret_fwd(qr, kr, vr, lg, t_lat=t_lat, tq=tq, tk=tk, name="ret_fwd")
    z = ret_post_fwd(o, p, heads=heads, d_model=D, name="ret_post_fwd")
    z = conv_fwd(z, p, w_conv, heads=heads, t_lat=t_lat, name="conv_fwd")
    x2, ym = out_proj(z, w_out, x1, mod3(0, 1), tm=tm, name="out_proj")
    x3, h3, a3, b3, y3 = ffn_fwd(x2, mod3(0, 2), norm_ffn2[0:1], wg02, wu02, wd02, tm=tm, t_first=t_lat, name="ffn_fwd_02")
    x4, h4, a4, b4, y4 = ffn_fwd(x3, mod3(1, 0), norm_ffn1[1:2], wg11, wu11, wd11, tm=tm, t_first=t_lat, name="ffn_fwd_11")
    hp = norm_mod_fwd(x4, mod3(1, 1), norm_mix[1:2], tm=tm, name="pool_norm_fwd")
    x5, pooled, yl = pool_fwd(hp, x4, pw, pscale, mod3(1, 1), name="pool_fwd")
    x6, h6, a6, b6, y6 = ffn_fwd(x5, mod3(1, 2), norm_ffn2[1:2], wg12, wu12, wd12, tm=tm, t_first=t_lat, name="ffn_fwd_12")
    loss_part, dx6, red_fn = final_loss(x6, final_norm[None, :], loss_target[0], tm=tm, name="final_loss")
    loss = lax.psum(loss_part[0, 0], ("x", "y", "c"))

    dx5, da6, db6, s6, dy6, red12 = ffn_bwd_tok(x5, dx6, y6, a6, b6, mod3(1, 2), norm_ffn2[1:2], wg12, wu12, wd12,
                                                tm=tm, t_first=t_lat, name="ffn_bwd_12")
    g_g12, g_u12, g_d12 = _ffn_grads(h6, da6, db6, s6, dy6, "12")
    dhp, g_pw, red_pool = pool_bwd(dx5, pooled, yl, pw, pscale, mod3(1, 1), name="pool_bwd")
    dx4, red_pn = norm_mod_bwd(x4, dhp, dx5, mod3(1, 1), norm_mix[1:2], tm=tm, name="pool_norm_bwd")
    dx3, da4, db4, s4, dy4, red11 = ffn_bwd_tok(x3, dx4, y4, a4, b4, mod3(1, 0), norm_ffn1[1:2], wg11, wu11, wd11,
                                                tm=tm, t_first=t_lat, name="ffn_bwd_11")
    g_g11, g_u11, g_d11 = _ffn_grads(h4, da4, db4, s4, dy4, "11")
    g_pw = jnp.transpose(g_pw.reshape(n_grp, N_DEV, grp_rows, grp_w), (1, 0, 2, 3)).reshape(N_DEV, n_grp * grp_rows, grp_w)
    red1 = _reduce_scatter([g_g11, g_u11, g_d11, g_g12, g_u12, g_d12, g_pw], peer_blocks, "1")

    dx2, da3, db3, s3, dy3, red02 = ffn_bwd_tok(x2, dx3, y3, a3, b3, mod3(0, 2), norm_ffn2[0:1], wg02, wu02, wd02,
                                                tm=tm, t_first=t_lat, name="ffn_bwd_02")
    g_g02, g_u02, g_d02 = _ffn_grads(h3, da3, db3, s3, dy3, "02")
    dz, dym, red_op = out_proj_bwd(dx2, ym, w_out, mod3(0, 1), tm=tm, name="out_proj_bwd")
    g_wout = matmul_tn(z, dym, nb=N_DEV, m=D // N_DEV, n=D, a_blocked=False, b_blocked=False,
                       tk=512 if t_lat % 512 == 0 else 256, name="wgrad_out")
    do, dp = ret_post_bwd(dz, o, p, heads=heads, name="ret_post_bwd")
    dp, g_conv = conv_bwd(dp, dz, p, w_conv, heads=heads, t_lat=t_lat, name="conv_bwd")
    dq, dk, dv, dlg = ret_bwd(qr, kr, vr, do, lg, t_lat=t_lat, tq=tq, tk=tk, name="ret_bwd")
    dp = ret_unprep(dp, dq, dk, dv, cos, sin, t_lat=t_lat, tm=tm, name="ret_unprep")
    dx1, red_mix = proj_bwd_tok(x1, dx2, dp, mod3(0, 1, True), norm_mix[0:1], w_in, tm=tm, name="proj_bwd")
    g_win = matmul_tn(hm, dp, nb=N_DEV, m=D, n=w_in.shape[2], a_blocked=False, b_blocked=False, tk=256, name="wgrad_in")
    dx0, da1, db1, s1, dy1, red01 = ffn_bwd_tok(x0, dx1, y1, a1, b1, mod3(0, 0, True), norm_ffn1[0:1], wg01, wu01, wd01,
                                                tm=tm, t_first=t_lat, name="ffn_bwd_01")
    g_g01, g_u01, g_d01 = _ffn_grads(h1, da1, db1, s1, dy1, "01")
    red0 = _reduce_scatter([g_g01, g_u01, g_d01, g_g02, g_u02, g_d02, g_win, g_wout], peer_blocks, "0")
    grad_x = dx0[:t_lat][None]

    def big(w, m, v, l, grad, parts, prev, nm):
        L, R = w.shape[0], w.shape[1:]
        w2, m2, v2 = (a.reshape(L, -1, R[-1]) for a in (w, m, v))
        outs = adam_reduced(own_block, grad, parts[0], parts[1], w2, m2, v2, l, prev, name=nm)
        return outs

    names = ["ffn1_w_gate", "ffn1_w_up", "ffn1_w_down", "ffn2_w_gate", "ffn2_w_up", "ffn2_w_down"]
    wmv = dict(ffn1_w_gate=(ffn1_w_gate, m_ffn1_w_gate, v_ffn1_w_gate), ffn1_w_up=(ffn1_w_up, m_ffn1_w_up, v_ffn1_w_up),
               ffn1_w_down=(ffn1_w_down, m_ffn1_w_down, v_ffn1_w_down), ffn2_w_gate=(ffn2_w_gate, m_ffn2_w_gate, v_ffn2_w_gate),
               ffn2_w_up=(ffn2_w_up, m_ffn2_w_up, v_ffn2_w_up), ffn2_w_down=(ffn2_w_down, m_ffn2_w_down, v_ffn2_w_down))
    grads1 = [g_g11, g_u11, g_d11, g_g12, g_u12, g_d12]
    grads0 = [g_g01, g_u01, g_d01, g_g02, g_u02, g_d02]
    res = {}
    for t, nm in enumerate(names):
        w, m, v = wmv[nm]
        first = big(w, m, v, 1, grads1[t], red1[t], None, f"adam_{nm}_1")
        res[nm] = [a.reshape(w.shape) for a in big(w, m, v, 0, grads0[t], red0[t], first, f"adam_{nm}_0")]
    res["pool_w"] = [a.reshape(pool_w.shape) for a in
                     big(pool_w.reshape(1, n_grp * grp_rows, grp_w), m_pool_w.reshape(1, n_grp * grp_rows, grp_w),
                         v_pool_w.reshape(1, n_grp * grp_rows, grp_w), 0, g_pw, red1[6], None, "adam_pool_w")]
    res["mix_w_in"] = big(mix_w_in, m_mix_w_in, v_mix_w_in, 0, g_win, red0[6], None, "adam_mix_w_in")
    res["mix_w_out"] = big(mix_w_out, m_mix_w_out, v_mix_w_out, 0, g_wout, red0[7], None, "adam_mix_w_out")

    dm_x = jnp.stack([jnp.concatenate([red01[0, 0:3], red_mix[0, 0:2], red_op[2:3], red02[0, 0:3]], axis=0),
                      jnp.concatenate([red11[0, 0:3], red_pn[0:2], red_pool[2:3], red12[0, 0:3]], axis=0)])
    dm_c = jnp.concatenate([red01[1, 0:3], red_mix[1, 0:2], jnp.zeros((4, D), F32)], axis=0)
    d_lg = dlg[:, 0:2, 0].T
    d_dec_f = d_lg[0:1] * jax.nn.sigmoid(-ret_decay_fwd)
    d_dec_b = d_lg[1:2] * jax.nn.sigmoid(-ret_decay_bwd)
    pieces = [dm_x.reshape(-1), dm_c.reshape(-1),
              jnp.stack([red01[0, 3] + red01[1, 3], red11[0, 3]]).reshape(-1),
              jnp.stack([red_mix[0, 3] + red_mix[1, 3], red_pn[3]]).reshape(-1),
              jnp.stack([red02[0, 3], red12[0, 3]]).reshape(-1),
              red_fn[0], d_dec_f.reshape(-1), d_dec_b.reshape(-1), g_conv.reshape(-1), red_pool[4]]
    sizes = [int(a.shape[0]) for a in pieces]
    n_pack = sum(sizes)
    n_pad = -n_pack % 8192
    packed = jnp.concatenate(pieces + [jnp.zeros((n_pad,), F32)])[None, :]
    (packed_all,) = all_gather([packed], name="gather_partials")
    total = sum_partials(packed_all, name="sum_partials")[0]
    offs = [0]
    for s in sizes:
        offs.append(offs[-1] + s)
    seg = [total[offs[i]:offs[i + 1]] for i in range(len(sizes))]
    g_dm = seg[0].reshape(2, N_MOD * D)
    g_dmc = seg[1].reshape(N_MOD * D)
    g_b_mod = g_dm.at[0].add(g_dmc)
    g_norm_ffn1, g_norm_mix, g_norm_ffn2 = (seg[k].reshape(2, D) for k in (2, 3, 4))
    g_final = seg[5]
    g_dec_f, g_dec_b = seg[6].reshape(1, heads), seg[7].reshape(1, heads)
    g_conv_all = seg[8].reshape(CONV_WIDTH, -1)
    g_pscale_all = seg[9]
    conv_w = mix_w_conv.shape[2]
    g_w_conv = lax.dynamic_slice_in_dim(g_conv_all, me * conv_w, conv_w, axis=1)[None]
    ps_w = pool_scale.shape[1]
    g_pool_scale = lax.dynamic_slice_in_dim(g_pscale_all, me * ps_w, ps_w, axis=0)[None]

    dm_rows = packed_all[:, 0, 0:offs[1]].reshape(N_DEV, 2, N_MOD * D)
    dmc_rows = packed_all[:, 0, offs[1]:offs[2]].reshape(N_DEV, N_MOD * D)
    dm_x_blk = jnp.transpose(lax.dynamic_slice_in_dim(dm_rows, me * mod_w, mod_w, axis=2), (1, 0, 2))
    dm_c_blk = jnp.stack([lax.dynamic_slice_in_dim(dmc_rows, me * mod_w, mod_w, axis=1),
                          jnp.zeros((N_DEV, mod_w), F32)])
    g_w_mod, d_w_mod, nm_w_mod, nv_w_mod, cctx_part = adaln_bwd(cond16, dm_x_blk, dm_c_blk, w_mod, m_w_mod, v_w_mod,
                                                                name="adaln_bwd")
    cpad = jnp.concatenate([cctx_part[0], jnp.zeros((8192 - D,), F32)])[None, :] if D < 8192 else cctx_part[0:1]
    (cctx_all,) = all_gather([cpad], name="gather_cctx")
    g_c_ctx = sum_partials(cctx_all, name="sum_cctx")[0, :D]

    small = [("c_ctx", g_c_ctx, c_ctx, m_c_ctx, v_c_ctx), ("b_mod", g_b_mod, b_mod, m_b_mod, v_b_mod),
             ("norm_ffn1", g_norm_ffn1, norm_ffn1, m_norm_ffn1, v_norm_ffn1),
             ("norm_mix", g_norm_mix, norm_mix, m_norm_mix, v_norm_mix),
             ("norm_ffn2", g_norm_ffn2, norm_ffn2, m_norm_ffn2, v_norm_ffn2),
             ("mix_w_conv", g_w_conv, mix_w_conv, m_mix_w_conv, v_mix_w_conv),
             ("ret_decay_fwd", g_dec_f, ret_decay_fwd, m_ret_decay_fwd, v_ret_decay_fwd),
             ("ret_decay_bwd", g_dec_b, ret_decay_bwd, m_ret_decay_bwd, v_ret_decay_bwd),
             ("pool_scale", g_pool_scale, pool_scale, m_pool_scale, v_pool_scale),
             ("final_norm", g_final, final_norm, m_final_norm, v_final_norm)]
    ssz = [int(a[1].size) for a in small]
    spad = -sum(ssz) % LANE

    def pack(k):
        return jnp.concatenate([a[k].reshape(-1) for a in small] + [jnp.ones((spad,), F32)])[None, :]

    sd, sm, sv = adam_plain(pack(1), pack(2), pack(3), pack(4), name="adam_small")
    soff = [0]
    for s in ssz:
        soff.append(soff[-1] + s)
    for i, (nm, g, w, _, _) in enumerate(small):
        res[nm] = [g.reshape(w.shape)] + [a[0, soff[i]:soff[i + 1]].reshape(w.shape) for a in (sd, sm, sv)]
    res["w_mod"] = [g_w_mod, d_w_mod, nm_w_mod, nv_w_mod]

    order = ["c_ctx", "w_mod", "b_mod", "norm_ffn1", "norm_mix", "norm_ffn2", "ffn1_w_gate", "ffn1_w_up", "ffn1_w_down",
             "ffn2_w_gate", "ffn2_w_up", "ffn2_w_down", "mix_w_in", "mix_w_conv", "mix_w_out", "ret_decay_fwd",
             "ret_decay_bwd", "pool_w", "pool_scale", "final_norm"]
    return (loss, grad_x, *[res[n][0] for n in order], *[res[n][1] for n in order],
            *[res[n][2] for n in order], *[res[n][3] for n in order])
```

```python
import jax
import jax.numpy as jnp
from jax import lax
from jax.experimental import pallas as pl
from jax.experimental.pallas import tpu as pltpu

F32 = jnp.float32
BF16 = jnp.bfloat16
MESH = pl.DeviceIdType.MESH

N_DEV = 8
N_MOD = 9
EPS = 1e-6
GN_EPS = 1e-5
MACARON = 0.5
HEAD_DIM = 128
K_SCALE = HEAD_DIM ** -0.5
ROPE_BASE = 10000.0
GRID_W = 64
CONV_WIDTH = 3
POOL_WINDOWS = (2, 4, 8, 16)
POOL_PAD = 16

ADAM_LR = 0.001
ADAM_B1 = 0.9
ADAM_B2 = 0.999
ADAM_EPS = 1e-08
ADAM_WD = 0.01
ADAM_STEP = 10

LANE = 128
ROW_CHUNK = 128
VMEM_LIMIT = 56 * 1024 * 1024
ANY = pl.BlockSpec(memory_space=pl.ANY)


def _params(sem=None):
    kw = dict(vmem_limit_bytes=VMEM_LIMIT)
    if sem is not None:
        kw["dimension_semantics"] = sem
    return pltpu.CompilerParams(**kw)


def _nt(a, b):
    return lax.dot_general(a, b, (((1,), (1,)), ((), ())), preferred_element_type=F32)


def _tn(a, b):
    return lax.dot_general(a, b, (((0,), (0,)), ((), ())), preferred_element_type=F32)


def _nn(a, b):
    return jnp.dot(a, b, preferred_element_type=F32)


def _silu_parts(a):
    sig = jax.nn.sigmoid(a)
    return sig, a * sig


def _dsilu(a, sig):
    return sig * (1.0 + a * (1.0 - sig))


def _norm_mod(x, g, shift, scale):
    r = lax.rsqrt(jnp.mean(x * x, axis=-1, keepdims=True) + EPS)
    return (x * r * g) * (1.0 + scale) + shift


def _norm_mod_bwd(x, dh, g, scale):
    r = lax.rsqrt(jnp.mean(x * x, axis=-1, keepdims=True) + EPS)
    xhat = x * r
    dn = dh * (1.0 + scale)
    dshift = jnp.sum(dh, axis=0, keepdims=True)
    dscale = jnp.sum(dh * (xhat * g), axis=0, keepdims=True)
    dg = jnp.sum(dn * xhat, axis=0, keepdims=True)
    dxh = dn * g
    dx = r * (dxh - xhat * jnp.mean(dxh * xhat, axis=-1, keepdims=True))
    return dx, dshift, dscale, dg


def _group_index(i, tiles_first, n_groups):
    if n_groups == 1:
        return 0
    return jnp.where(i >= tiles_first, 1, 0)


def _mesh_pos():
    x, y, c = lax.axis_index("x"), lax.axis_index("y"), lax.axis_index("c")
    chips = [(1 - x, y), (x, 1 - y), (1 - x, 1 - y)]
    return x, y, c, chips


def _flat(px, py, pc):
    return 4 * px + 2 * py + pc


def all_gather(shards, name):
    n = len(shards)

    def body(*refs):
        ins, outs = refs[:n], refs[n:2 * n]
        send_sems, recv_sems, local_sems = refs[2 * n:]
        x, y, c, chips = _mesh_pos()
        me, sibling = (x, y, c), (x, y, 1 - c)

        def copy(t, k, block, to, src=None):
            dst = outs[t].at[_flat(*block)]
            return pltpu.make_async_remote_copy(
                src_ref=dst if src is None else src, dst_ref=dst,
                send_sem=send_sems.at[t, k], recv_sem=recv_sems.at[t, k],
                device_id=to, device_id_type=MESH)

        mine = [pltpu.make_async_copy(ins[t], outs[t].at[_flat(*me)], local_sems.at[t]) for t in range(n)]
        for cp in mine:
            cp.start()
        first = []
        for t in range(n):
            first.append(copy(t, 0, me, sibling, src=ins[t]))
            first += [copy(t, 1 + j, me, (*chip, c), src=ins[t]) for j, chip in enumerate(chips)]
        for cp in first:
            cp.start()
        passed = []
        for t in range(n):
            for j, chip in enumerate(chips):
                copy(t, 1 + j, (*chip, c), me).wait_recv()
                fwd = copy(t, 4 + j, (*chip, c), sibling)
                fwd.start()
                passed.append(fwd)
        for t in range(n):
            copy(t, 0, sibling, me).wait_recv()
            for j, chip in enumerate(chips):
                copy(t, 4 + j, (*chip, 1 - c), me).wait_recv()
        for cp in first + passed:
            cp.wait_send()
        for cp in mine:
            cp.wait()

    return pl.pallas_call(
        body, name=name,
        out_shape=[jax.ShapeDtypeStruct((N_DEV,) + s.shape, s.dtype) for s in shards],
        in_specs=[ANY] * n, out_specs=[ANY] * n,
        scratch_shapes=[pltpu.SemaphoreType.DMA((n, 7)), pltpu.SemaphoreType.DMA((n, 7)), pltpu.SemaphoreType.DMA((n,))],
    )(*shards)


HBM_SPEC = pl.BlockSpec(memory_space=pltpu.HBM)
SEM_SPEC = pl.BlockSpec(memory_space=pltpu.SEMAPHORE)
DATAFLOW = pltpu.SideEffectType.DATAFLOW_SIDE_EFFECTING


def _in_hbm(a):
    return pltpu.with_memory_space_constraint(a, pltpu.HBM)


def _push_peers():
    x, y, c, chips = _mesh_pos()
    return [(*chip, c) for chip in chips] + [(x, y, 1 - c)]


def gather_start(shards, name):
    n = len(shards)
    lands = [lax.empty((N_DEV,) + s.shape, s.dtype) for s in shards]

    def body(*refs):
        ins, lz = refs[:n], refs[n:2 * n]
        send_sems, recv_sems = refs[2 * n], refs[2 * n + 1]
        token = refs[-1]
        x, y, c, _ = _mesh_pos()
        for t in range(n):
            for k, peer in enumerate(_push_peers()):
                pltpu.make_async_remote_copy(
                    src_ref=ins[t], dst_ref=lz[t].at[_flat(x, y, c)], send_sem=send_sems.at[4 * t + k],
                    recv_sem=recv_sems.at[4 * t + k], device_id=peer, device_id_type=MESH).start()
        token[...] = jnp.zeros_like(token)

    return pl.pallas_call(
        body, name=name,
        out_shape=(pltpu.SemaphoreType.DMA((4 * n,)), pltpu.SemaphoreType.DMA((4 * n,)),
                   *[pltpu.HBM(s.shape, s.dtype) for s in shards], *[pltpu.HBM(l.shape, l.dtype) for l in lands],
                   jax.ShapeDtypeStruct((8, LANE), F32)),
        in_specs=[HBM_SPEC] * (2 * n),
        out_specs=(SEM_SPEC, SEM_SPEC, *[HBM_SPEC] * (2 * n), pl.BlockSpec(memory_space=pltpu.VMEM)),
        input_output_aliases={i: 2 + i for i in range(2 * n)},
        compiler_params=pltpu.CompilerParams(has_side_effects=DATAFLOW),
    )(*[_in_hbm(s) for s in shards], *[_in_hbm(l) for l in lands])


def gather_wait(handle, after, name):
    send, recv = handle[0], handle[1]
    n = (len(handle) - 3) // 2
    bufs = handle[2:2 + 2 * n]

    def body(*refs):
        ins, lz = refs[:n], refs[n:2 * n]
        send_sems, recv_sems = refs[2 * n], refs[2 * n + 1]
        for t in range(n):
            for k, peer in enumerate(_push_peers()):
                cp = pltpu.make_async_remote_copy(
                    src_ref=ins[t], dst_ref=lz[t].at[_flat(*peer)], send_sem=send_sems.at[4 * t + k],
                    recv_sem=recv_sems.at[4 * t + k], device_id=peer, device_id_type=MESH)
                cp.wait_send()
                cp.wait_recv()

    outs = pl.pallas_call(
        body, name=name,
        out_shape=tuple(pltpu.HBM(b.shape, b.dtype) for b in bufs),
        in_specs=[HBM_SPEC] * (2 * n) + [SEM_SPEC, SEM_SPEC, ANY],
        out_specs=tuple([HBM_SPEC] * (2 * n)),
        input_output_aliases={i: i for i in range(2 * n)},
        compiler_params=pltpu.CompilerParams(has_side_effects=DATAFLOW),
    )(*bufs, send, recv, after)
    return list(outs[:n]), list(outs[n:])


def gather_finish(shards, lands, name):
    n = len(shards)

    def body(*refs):
        ins, outs = refs[:n], refs[2 * n:3 * n]
        send_sems, recv_sems, local_sems = refs[3 * n:]
        x, y, c, chips = _mesh_pos()
        sibling = (x, y, 1 - c)

        def copy(t, j, core):
            blk = outs[t].at[_flat(*chips[j], core)]
            return pltpu.make_async_remote_copy(
                src_ref=blk, dst_ref=blk, send_sem=send_sems.at[t, j], recv_sem=recv_sems.at[t, j],
                device_id=sibling, device_id_type=MESH)

        mine = [pltpu.make_async_copy(ins[t], outs[t].at[_flat(x, y, c)], local_sems.at[t]) for t in range(n)]
        sends = [copy(t, j, c) for t in range(n) for j in range(3)]
        for cp in mine + sends:
            cp.start()
        for t in range(n):
            for j in range(3):
                copy(t, j, 1 - c).wait_recv()
        for cp in sends:
            cp.wait_send()
        for cp in mine:
            cp.wait()

    return pl.pallas_call(
        body, name=name,
        out_shape=[jax.ShapeDtypeStruct(l.shape, l.dtype) for l in lands],
        in_specs=[ANY] * (2 * n), out_specs=[ANY] * n,
        input_output_aliases={n + t: t for t in range(n)},
        scratch_shapes=[pltpu.SemaphoreType.DMA((n, 3)), pltpu.SemaphoreType.DMA((n, 3)), pltpu.SemaphoreType.DMA((n,))],
    )(*shards, *lands)


def chip_exchange_start(sums, name):
    n = len(sums)
    lands = [lax.empty(s.shape, s.dtype) for s in sums]

    def body(*refs):
        ins, lz = refs[:n], refs[n:2 * n]
        send_sems, recv_sems = refs[2 * n], refs[2 * n + 1]
        token = refs[-1]
        peers = _push_peers()
        for t in range(n):
            for j in range(3):
                pltpu.make_async_remote_copy(
                    src_ref=ins[t].at[j], dst_ref=lz[t].at[j], send_sem=send_sems.at[3 * t + j],
                    recv_sem=recv_sems.at[3 * t + j], device_id=peers[j], device_id_type=MESH).start()
        token[...] = jnp.zeros_like(token)

    return pl.pallas_call(
        body, name=name,
        out_shape=(pltpu.SemaphoreType.DMA((3 * n,)), pltpu.SemaphoreType.DMA((3 * n,)),
                   *[pltpu.HBM(s.shape, s.dtype) for s in sums], *[pltpu.HBM(s.shape, s.dtype) for s in sums],
                   jax.ShapeDtypeStruct((8, LANE), F32)),
        in_specs=[HBM_SPEC] * (2 * n),
        out_specs=(SEM_SPEC, SEM_SPEC, *[HBM_SPEC] * (2 * n), pl.BlockSpec(memory_space=pltpu.VMEM)),
        input_output_aliases={i: 2 + i for i in range(2 * n)},
        compiler_params=pltpu.CompilerParams(has_side_effects=DATAFLOW),
    )(*[_in_hbm(s) for s in sums], *[_in_hbm(l) for l in lands])


def chip_exchange_wait(handle, after, name):
    send, recv = handle[0], handle[1]
    n = (len(handle) - 3) // 2
    bufs = handle[2:2 + 2 * n]

    def body(*refs):
        ins, lz = refs[:n], refs[n:2 * n]
        send_sems, recv_sems = refs[2 * n], refs[2 * n + 1]
        peers = _push_peers()
        for t in range(n):
            for j in range(3):
                cp = pltpu.make_async_remote_copy(
                    src_ref=ins[t].at[j], dst_ref=lz[t].at[j], send_sem=send_sems.at[3 * t + j],
                    recv_sem=recv_sems.at[3 * t + j], device_id=peers[j], device_id_type=MESH)
                cp.wait_send()
                cp.wait_recv()

    outs = pl.pallas_call(
        body, name=name,
        out_shape=tuple(pltpu.HBM(b.shape, b.dtype) for b in bufs),
        in_specs=[HBM_SPEC] * (2 * n) + [SEM_SPEC, SEM_SPEC, ANY],
        out_specs=tuple([HBM_SPEC] * (2 * n)),
        input_output_aliases={i: i for i in range(2 * n)},
        compiler_params=pltpu.CompilerParams(has_side_effects=DATAFLOW),
    )(*bufs, send, recv, after)
    return list(outs[n:])


def rs_sibling_exchange(grads, name):
    n = len(grads)

    def body(*refs):
        ins, stages, directs = refs[:n], refs[n:2 * n], refs[2 * n:3 * n]
        send_sems, recv_sems = refs[3 * n:]
        x, y, c, chips = _mesh_pos()
        sibling = (x, y, 1 - c)

        def copy(t, k):
            if k < 3:
                src, dst = ins[t].at[_flat(*chips[k], 1 - c)], stages[t].at[k]
            else:
                src, dst = ins[t].at[_flat(x, y, 1 - c)], directs[t]
            return pltpu.make_async_remote_copy(
                src_ref=src, dst_ref=dst, send_sem=send_sems.at[t, k], recv_sem=recv_sems.at[t, k],
                device_id=sibling, device_id_type=MESH)

        cps = [copy(t, k) for t in range(n) for k in range(4)]
        for cp in cps:
            cp.start()
        for cp in cps:
            cp.wait_recv()
        for cp in cps:
            cp.wait_send()

    return pl.pallas_call(
        body, name=name,
        out_shape=[jax.ShapeDtypeStruct((3,) + g.shape[1:], g.dtype) for g in grads]
        + [jax.ShapeDtypeStruct(g.shape[1:], g.dtype) for g in grads],
        in_specs=[ANY] * n, out_specs=[ANY] * (2 * n),
        scratch_shapes=[pltpu.SemaphoreType.DMA((n, 4)), pltpu.SemaphoreType.DMA((n, 4))],
    )(*grads)


def _row_tile(rows, cols, limit_bytes=3 << 19):
    best = None
    for t in range(16, rows + 1, 16):
        if rows % t == 0 and t * cols * 4 <= limit_bytes:
            best = t
    return best if best is not None else rows


def chip_sum(grad, stage, peer_blocks, name):
    _, R, C = grad.shape
    tr = _row_tile(R, C)

    def body(idx_ref, g_ref, s_ref, o_ref):
        o_ref[...] = (g_ref[...].astype(F32) + s_ref[...].astype(F32)).astype(BF16)

    return pl.pallas_call(
        body, name=name,
        grid_spec=pltpu.PrefetchScalarGridSpec(
            num_scalar_prefetch=1, grid=(3, R // tr),
            in_specs=[pl.BlockSpec((None, tr, C), lambda j, i, idx: (idx[j], i, 0)),
                      pl.BlockSpec((None, tr, C), lambda j, i, idx: (j, i, 0))],
            out_specs=pl.BlockSpec((None, tr, C), lambda j, i, idx: (j, i, 0))),
        out_shape=jax.ShapeDtypeStruct((3, R, C), BF16),
        compiler_params=_params(("arbitrary", "arbitrary")),
    )(peer_blocks, grad, stage)


def _adamw(w, g, m, v):
    m2 = ADAM_B1 * m + (1.0 - ADAM_B1) * g
    v2 = ADAM_B2 * v + (1.0 - ADAM_B2) * (g * g)
    m_hat = m2 / (1.0 - ADAM_B1 ** ADAM_STEP)
    v_hat = v2 / (1.0 - ADAM_B2 ** ADAM_STEP)
    delta = -ADAM_LR * (m_hat / (jnp.sqrt(v_hat) + ADAM_EPS) + ADAM_WD * w)
    return delta, m2, v2


def adam_reduced(own_block, grad, direct, landed, w, m, v, layer, prev, name):
    L, R, C = w.shape
    tr = _row_tile(R, C, 1 << 20)
    first = prev is None

    def body(idx_ref, g_ref, d_ref, l_ref, w_ref, m_ref, v_ref, *rest):
        og, od, om, ov = rest[-4:]
        g = g_ref[...].astype(F32) + d_ref[...].astype(F32)
        for j in range(3):
            g = g + l_ref[j].astype(F32)
        delta, m2, v2 = _adamw(w_ref[...], g, m_ref[...], v_ref[...])
        og[...] = g
        od[...] = delta
        om[...] = m2
        ov[...] = v2

    lay = pl.BlockSpec((None, tr, C), lambda i, idx: (layer, i, 0))
    in_specs = [pl.BlockSpec((None, tr, C), lambda i, idx: (idx[0], i, 0)),
                pl.BlockSpec((tr, C), lambda i, idx: (i, 0)),
                pl.BlockSpec((3, tr, C), lambda i, idx: (0, i, 0)),
                lay, lay, lay]
    args = [own_block, grad, direct, landed, w, m, v]
    aliases = {}
    if not first:
        in_specs += [ANY] * 4
        args += list(prev)
        aliases = {7 + k: k for k in range(4)}
    return pl.pallas_call(
        body, name=name,
        grid_spec=pltpu.PrefetchScalarGridSpec(
            num_scalar_prefetch=1, grid=(R // tr,), in_specs=in_specs, out_specs=[lay] * 4),
        out_shape=[jax.ShapeDtypeStruct((L, R, C), F32)] * 4,
        input_output_aliases=aliases,
        compiler_params=_params(("arbitrary",)),
    )(*args)


def sum_partials(parts, name):
    _, _, N = parts.shape
    tn = 8192

    def body(p_ref, o_ref):
        g = p_ref[0]
        for k in range(1, N_DEV):
            g = g + p_ref[k]
        o_ref[...] = g

    return pl.pallas_call(
        body, name=name, grid=(N // tn,),
        in_specs=[pl.BlockSpec((N_DEV, 1, tn), lambda i: (0, 0, i))],
        out_specs=pl.BlockSpec((1, tn), lambda i: (0, i)),
        out_shape=jax.ShapeDtypeStruct((1, N), F32),
    )(parts)


def adam_plain(g, w, m, v, name):
    _, N = g.shape

    def body(g_ref, w_ref, m_ref, v_ref, od, om, ov):
        delta, m2, v2 = _adamw(w_ref[...], g_ref[...], m_ref[...], v_ref[...])
        od[...] = delta
        om[...] = m2
        ov[...] = v2

    return pl.pallas_call(
        body, name=name, out_shape=[jax.ShapeDtypeStruct((1, N), F32)] * 3,
    )(g, w, m, v)


def adaln_fwd(cond16, w_mod, b_blk, name):
    L, D, W = w_mod.shape
    tn = 768 if W % 768 == 0 else W

    def body(c_ref, w_ref, b_ref, o_ref):
        c = c_ref[...]
        sc = (c * jax.nn.sigmoid(c)).astype(BF16)
        o_ref[...] = _nn(sc, w_ref[...].astype(BF16)) + b_ref[...]

    return pl.pallas_call(
        body, name=name, grid=(L, W // tn),
        in_specs=[pl.BlockSpec((16, D), lambda l, i: (0, 0)),
                  pl.BlockSpec((None, D, tn), lambda l, i: (l, 0, i)),
                  pl.BlockSpec((None, 1, tn), lambda l, i: (l, 0, i))],
        out_specs=pl.BlockSpec((None, 16, tn), lambda l, i: (l, 0, i)),
        out_shape=jax.ShapeDtypeStruct((L, 16, W), F32),
        compiler_params=_params(("arbitrary", "arbitrary")),
    )(cond16, w_mod, b_blk)


def adaln_bwd(cond16, dm_x, dm_c, w_mod, m_mod, v_mod, name):
    L, D, W = w_mod.shape
    tn = 256 if W % 256 == 0 else W
    nt = W // tn

    def body(c_ref, cT_ref, dx_ref, dc_ref, w_ref, m_ref, v_ref, og, od, om, ov, pc_ref):
        l, i = pl.program_id(0), pl.program_id(1)
        c = c_ref[...]
        sig, sl = _silu_parts(c)
        cT = cT_ref[...]
        sigT = jax.nn.sigmoid(cT)
        scT = (cT * sigT).astype(BF16)
        dmc = jnp.sum(dc_ref[...], axis=0, keepdims=True)
        dm16 = jnp.concatenate([dx_ref[...], jnp.broadcast_to(dmc, (8, tn))], axis=0)
        row = lax.broadcasted_iota(jnp.int32, (16, tn), 0)
        dm16 = jnp.where(row <= 8, dm16, 0.0).astype(BF16)
        w = w_ref[...]
        g = _nn(scT, dm16)
        delta, m2, v2 = _adamw(w, g, m_ref[...], v_ref[...])
        og[...] = g
        od[...] = delta
        om[...] = m2
        ov[...] = v2

        @pl.when((l == 0) & (i == 0))
        def _():
            pc_ref[...] = jnp.zeros_like(pc_ref)

        @pl.when(l == 0)
        def _():
            back = _nt(jnp.broadcast_to(dmc, (8, tn)).astype(BF16), w.astype(BF16))
            pc_ref[...] += back * _dsilu(c[8:9, :], sig[8:9, :])

    col = pl.BlockSpec((None, D, tn), lambda l, i: (l, 0, i))
    row8 = pl.BlockSpec((None, 8, tn), lambda l, i: (l, 0, i))
    return pl.pallas_call(
        body, name=name, grid=(L, nt),
        in_specs=[pl.BlockSpec((16, D), lambda l, i: (0, 0)), pl.BlockSpec((D, 16), lambda l, i: (0, 0)),
                  row8, row8, col, col, col],
        out_specs=[col, col, col, col, pl.BlockSpec((8, D), lambda l, i: (0, 0))],
        out_shape=[jax.ShapeDtypeStruct((L, D, W), F32)] * 4 + [jax.ShapeDtypeStruct((8, D), F32)],
        compiler_params=_params(("arbitrary", "arbitrary")),
    )(cond16, cond16.T, dm_x, dm_c, w_mod, m_mod, v_mod)


def _token_spec(tm, D):
    if tm > 256:
        return pl.BlockSpec((tm, D), lambda i, j: (i, 0), pipeline_mode=pl.Buffered(1))
    return pl.BlockSpec((tm, D), lambda i, j: (i, 0))


def ffn_fwd(x, mod, gn, wg, wu, wd, *, tm, t_first, name):
    T, D = x.shape
    nb, cw, _ = wg.shape
    G = mod.shape[0]
    nt = T // tm
    first = t_first // tm

    def body(x_ref, mod_ref, g_ref, wg_ref, wu_ref, wd_ref, xo_ref, h_ref, a_ref, b_ref, y_ref, hs, acc, a_s, b_s, s_s):
        j = pl.program_id(1)

        @pl.when(j == 0)
        def _():
            def head(r, carry):
                rows = pl.ds(pl.multiple_of(r * ROW_CHUNK, ROW_CHUNK), ROW_CHUNK)
                hb = _norm_mod(x_ref[rows, :], g_ref[...], mod_ref[0:1, :], mod_ref[1:2, :]).astype(BF16)
                hs[rows, :] = hb
                h_ref[rows, :] = hb
                return carry

            lax.fori_loop(0, tm // ROW_CHUNK, head, 0)
            acc[...] = jnp.zeros_like(acc)

        a_s[...] = _nt(hs[...], wg_ref[...])
        b_s[...] = _nt(hs[...], wu_ref[...])

        def chunk(r, carry):
            rows = pl.ds(pl.multiple_of(r * ROW_CHUNK, ROW_CHUNK), ROW_CHUNK)
            av, bv = a_s[rows, :], b_s[rows, :]
            a_ref[rows, :] = av.astype(BF16)
            b_ref[rows, :] = bv.astype(BF16)
            _, sl = _silu_parts(av)
            s_s[rows, :] = (sl * bv).astype(BF16)
            return carry

        lax.fori_loop(0, tm // ROW_CHUNK, chunk, 0)
        acc[...] += _nn(s_s[...], wd_ref[...])

        @pl.when(j == nb - 1)
        def _():
            def tail(r, carry):
                rows = pl.ds(pl.multiple_of(r * ROW_CHUNK, ROW_CHUNK), ROW_CHUNK)
                y = acc[rows, :]
                y_ref[rows, :] = y.astype(BF16)
                xo_ref[rows, :] = x_ref[rows, :] + (MACARON * mod_ref[2:3, :]) * y
                return carry

            lax.fori_loop(0, tm // ROW_CHUNK, tail, 0)

    tok = _token_spec(tm, D)
    act = pl.BlockSpec((None, tm, cw), lambda i, j: (j, i, 0))
    wblk = pl.BlockSpec((None, cw, D), lambda i, j: (j, 0, 0))
    return pl.pallas_call(
        body, name=name, grid=(nt, nb),
        in_specs=[tok, pl.BlockSpec((None, 3, D), lambda i, j: (_group_index(i, first, G), 0, 0)),
                  pl.BlockSpec((1, D), lambda i, j: (0, 0)), wblk, wblk, wblk],
        out_specs=[tok, tok, act, act, tok],
        out_shape=[jax.ShapeDtypeStruct((T, D), F32), jax.ShapeDtypeStruct((T, D), BF16),
                   jax.ShapeDtypeStruct((nb, T, cw), BF16), jax.ShapeDtypeStruct((nb, T, cw), BF16),
                   jax.ShapeDtypeStruct((T, D), BF16)],
        scratch_shapes=[pltpu.VMEM((tm, D), BF16), pltpu.VMEM((tm, D), F32), pltpu.VMEM((tm, cw), F32),
                        pltpu.VMEM((tm, cw), F32), pltpu.VMEM((tm, cw), BF16)],
        compiler_params=_params(("arbitrary", "arbitrary")),
    )(x, mod, gn, wg, wu, wd)


def ffn_bwd_tok(x, dxo, y, a, b, mod, gn, wg, wu, wd, *, tm, t_first, name):
    T, D = x.shape
    nb, cw, _ = wg.shape
    G = mod.shape[0]
    nt = T // tm
    first = t_first // tm

    def body(x_ref, dxo_ref, y_ref, a_ref, b_ref, mod_ref, g_ref, wg_ref, wu_ref, wd_ref,
             dx_ref, da_ref, db_ref, s_ref, dy_ref, red_ref, dys, dh, ds_s):
        i, j = pl.program_id(0), pl.program_id(1)

        @pl.when(j == 0)
        def _():
            def head(r, carry):
                rows = pl.ds(pl.multiple_of(r * ROW_CHUNK, ROW_CHUNK), ROW_CHUNK)
                dyb = ((MACARON * mod_ref[2:3, :]) * dxo_ref[rows, :]).astype(BF16)
                dys[rows, :] = dyb
                dy_ref[rows, :] = dyb
                return carry

            lax.fori_loop(0, tm // ROW_CHUNK, head, 0)
            dh[...] = jnp.zeros_like(dh)

        ds_s[...] = _nt(dys[...], wd_ref[...])

        def chunk(r, carry):
            rows = pl.ds(pl.multiple_of(r * ROW_CHUNK, ROW_CHUNK), ROW_CHUNK)
            av = a_ref[rows, :].astype(F32)
            bv = b_ref[rows, :].astype(F32)
            ds = ds_s[rows, :]
            sig, sl = _silu_parts(av)
            s_ref[rows, :] = (sl * bv).astype(BF16)
            da_ref[rows, :] = (ds * bv * _dsilu(av, sig)).astype(BF16)
            db_ref[rows, :] = (ds * sl).astype(BF16)
            return carry

        lax.fori_loop(0, tm // ROW_CHUNK, chunk, 0)
        dh[...] += _nn(da_ref[...], wg_ref[...]) + _nn(db_ref[...], wu_ref[...])

        @pl.when((j == 0) & ((i == 0) | (i == first)))
        def _():
            red_ref[...] = jnp.zeros_like(red_ref)

        @pl.when(j == nb - 1)
        def _():
            def tail(r, carry):
                rows = pl.ds(pl.multiple_of(r * ROW_CHUNK, ROW_CHUNK), ROW_CHUNK)
                dxo_v = dxo_ref[rows, :]
                dxn, dshift, dscale, dg = _norm_mod_bwd(x_ref[rows, :], dh[rows, :], g_ref[...], mod_ref[1:2, :])
                dx_ref[rows, :] = dxo_v + dxn
                red_ref[0:1, :] += dshift
                red_ref[1:2, :] += dscale
                red_ref[2:3, :] += jnp.sum((MACARON * dxo_v) * y_ref[rows, :].astype(F32), axis=0, keepdims=True)
                red_ref[3:4, :] += dg
                return carry

            lax.fori_loop(0, tm // ROW_CHUNK, tail, 0)

    tok = _token_spec(tm, D)
    act = pl.BlockSpec((None, tm, cw), lambda i, j: (j, i, 0))
    wblk = pl.BlockSpec((None, cw, D), lambda i, j: (j, 0, 0))
    return pl.pallas_call(
        body, name=name, grid=(nt, nb),
        in_specs=[tok, tok, tok, act, act,
                  pl.BlockSpec((None, 3, D), lambda i, j: (_group_index(i, first, G), 0, 0)),
                  pl.BlockSpec((1, D), lambda i, j: (0, 0)), wblk, wblk, wblk],
        out_specs=[tok, act, act, act, tok,
                   pl.BlockSpec((None, 8, D), lambda i, j: (_group_index(i, first, G), 0, 0))],
        out_shape=[jax.ShapeDtypeStruct((T, D), F32)] + [jax.ShapeDtypeStruct((nb, T, cw), BF16)] * 3
        + [jax.ShapeDtypeStruct((T, D), BF16), jax.ShapeDtypeStruct((G, 8, D), F32)],
        scratch_shapes=[pltpu.VMEM((tm, D), BF16), pltpu.VMEM((tm, D), F32), pltpu.VMEM((tm, cw), F32)],
        compiler_params=_params(("arbitrary", "arbitrary")),
    )(x, dxo, y, a, b, mod, gn, wg, wu, wd)


def matmul_tn(a, b, *, nb, m, n, a_blocked, b_blocked, tk, name):
    T = a.shape[-2]

    def spec(arr, blocked, width):
        if blocked:
            return pl.BlockSpec((None, tk, width), lambda j, k: (j, k, 0))
        if arr.shape[-1] == width:
            return pl.BlockSpec((tk, width), lambda j, k: (k, 0))
        return pl.BlockSpec((tk, width), lambda j, k: (k, j))

    def body(a_ref, b_ref, o_ref, acc):
        k = pl.program_id(1)

        @pl.when(k == 0)
        def _():
            acc[...] = jnp.zeros_like(acc)

        acc[...] += _tn(a_ref[...], b_ref[...])

        @pl.when(k == T // tk - 1)
        def _():
            o_ref[...] = acc[...].astype(BF16)

    return pl.pallas_call(
        body, name=name, grid=(nb, T // tk),
        in_specs=[spec(a, a_blocked, m), spec(b, b_blocked, n)],
        out_specs=pl.BlockSpec((None, m, n), lambda j, k: (j, 0, 0)),
        out_shape=jax.ShapeDtypeStruct((nb, m, n), BF16),
        scratch_shapes=[pltpu.VMEM((m, n), F32)],
        compiler_params=_params(("arbitrary", "arbitrary")),
    )(a, b)


def proj_in(x, mod, gn, w_in, *, tm, t_first, name):
    T, D = x.shape
    nb, _, cw = w_in.shape
    G = mod.shape[0]
    nt = T // tm
    first = t_first // tm

    def body(x_ref, mod_ref, g_ref, w_ref, p_ref, h_ref, hs):
        @pl.when(pl.program_id(1) == 0)
        def _():
            hb = _norm_mod(x_ref[...], g_ref[...], mod_ref[0:1, :], mod_ref[1:2, :]).astype(BF16)
            hs[...] = hb
            h_ref[...] = hb

        p_ref[...] = _nn(hs[...], w_ref[...])

    tok = pl.BlockSpec((tm, D), lambda i, j: (i, 0))
    return pl.pallas_call(
        body, name=name, grid=(nt, nb),
        in_specs=[tok, pl.BlockSpec((None, 3, D), lambda i, j: (_group_index(i, first, G), 0, 0)),
                  pl.BlockSpec((1, D), lambda i, j: (0, 0)),
                  pl.BlockSpec((None, D, cw), lambda i, j: (j, 0, 0))],
        out_specs=[pl.BlockSpec((tm, cw), lambda i, j: (i, j)), tok],
        out_shape=[jax.ShapeDtypeStruct((T, nb * cw), F32), jax.ShapeDtypeStruct((T, D), BF16)],
        scratch_shapes=[pltpu.VMEM((tm, D), BF16)],
        compiler_params=_params(("arbitrary", "arbitrary")),
    )(x, mod, gn, w_in)


def proj_bwd_tok(x, dxo, dp, mod, gn, w_in, *, tm, name):
    T, D = x.shape
    nb, _, cw = w_in.shape
    G = mod.shape[0]
    nt = T // tm
    first = dxo.shape[0] // tm

    def body(x_ref, dxo_ref, dp_ref, mod_ref, g_ref, w_ref, dx_ref, red_ref, dh):
        i, j = pl.program_id(0), pl.program_id(1)

        @pl.when(j == 0)
        def _():
            dh[...] = jnp.zeros_like(dh)

        dh[...] += _nt(dp_ref[...], w_ref[...])

        @pl.when((j == 0) & ((i == 0) | (i == first)))
        def _():
            red_ref[...] = jnp.zeros_like(red_ref)

        @pl.when(j == nb - 1)
        def _():
            dxn, dshift, dscale, dg = _norm_mod_bwd(x_ref[...], dh[...], g_ref[...], mod_ref[1:2, :])
            dx_ref[...] = jnp.where(i < first, dxo_ref[...], 0.0) + dxn
            red_ref[0:1, :] += dshift
            red_ref[1:2, :] += dscale
            red_ref[3:4, :] += dg

    tok = pl.BlockSpec((tm, D), lambda i, j: (i, 0))
    return pl.pallas_call(
        body, name=name, grid=(nt, nb),
        in_specs=[tok, pl.BlockSpec((tm, D), lambda i, j: (jnp.minimum(i, first - 1), 0)),
                  pl.BlockSpec((tm, cw), lambda i, j: (i, j)),
                  pl.BlockSpec((None, 3, D), lambda i, j: (_group_index(i, first, G), 0, 0)),
                  pl.BlockSpec((1, D), lambda i, j: (0, 0)),
                  pl.BlockSpec((None, D, cw), lambda i, j: (j, 0, 0))],
        out_specs=[tok, pl.BlockSpec((None, 8, D), lambda i, j: (_group_index(i, first, G), 0, 0))],
        out_shape=[jax.ShapeDtypeStruct((T, D), F32), jax.ShapeDtypeStruct((G, 8, D), F32)],
        scratch_shapes=[pltpu.VMEM((tm, D), F32)],
        compiler_params=_params(("arbitrary", "arbitrary")),
    )(x, dxo, dp, mod, gn, w_in)


def _rope(t, cos, sin):
    return t * cos + pltpu.roll(t, HEAD_DIM // 2, axis=1) * sin


def ret_prep(p, cos, sin, *, heads, tm, name):
    T = p.shape[0]

    def body(q_ref, k_ref, v_ref, c_ref, s_ref, qo, ko, vo):
        cos_v, sin_v = c_ref[...], s_ref[...]
        qo[...] = _rope(q_ref[...], cos_v, sin_v).astype(BF16)
        ko[...] = _rope(k_ref[...] * K_SCALE, cos_v, sin_v).astype(BF16)
        vo[...] = v_ref[...].astype(BF16)

    def col(part):
        return pl.BlockSpec((tm, HEAD_DIM), lambda h, i: (i, part * heads + h))

    tab = pl.BlockSpec((tm, HEAD_DIM), lambda h, i: (i, 0))
    out = pl.BlockSpec((None, tm, HEAD_DIM), lambda h, i: (h, i, 0))
    return pl.pallas_call(
        body, name=name, grid=(heads, T // tm),
        in_specs=[col(0), col(1), col(2), tab, tab], out_specs=[out, out, out],
        out_shape=[jax.ShapeDtypeStruct((heads, T, HEAD_DIM), BF16)] * 3,
        compiler_params=_params(("arbitrary", "arbitrary")),
    )(p, p, p, cos, sin)


def _decay(n, m, lgf, lgb, t_lat, t_ctx):
    df = jnp.where(m < t_lat, n - m, n - m + (t_lat + t_ctx))
    db = m - n
    ef = jnp.where(df >= 0, jnp.exp(lgf * df), 0.0)
    eb = jnp.where(db >= 0, jnp.exp(lgb * db), 0.0)
    return ef, eb, df, db


def ret_fwd(q, k, v, lg, *, t_lat, tq, tk, name):
    H, T, _ = k.shape
    t_ctx = T - t_lat

    def body(lg_ref, q_ref, k_ref, v_ref, o_ref):
        h, qi, kj = pl.program_id(0), pl.program_id(1), pl.program_id(2)

        @pl.when(kj == 0)
        def _():
            o_ref[...] = jnp.zeros_like(o_ref)

        s = _nt(q_ref[...], k_ref[...])
        n = (qi * tq + lax.broadcasted_iota(jnp.int32, (tq, tk), 0)).astype(F32)
        m = (kj * tk + lax.broadcasted_iota(jnp.int32, (tq, tk), 1)).astype(F32)
        ef, eb, _, _ = _decay(n, m, lg_ref[0, h], lg_ref[1, h], t_lat, t_ctx)
        o_ref[...] += _nn((s * (ef + eb)).astype(BF16), v_ref[...])

    return pl.pallas_call(
        body, name=name, grid=(H, t_lat // tq, T // tk),
        in_specs=[pl.BlockSpec(memory_space=pltpu.SMEM),
                  pl.BlockSpec((None, tq, HEAD_DIM), lambda h, i, j: (h, i, 0)),
                  pl.BlockSpec((None, tk, HEAD_DIM), lambda h, i, j: (h, j, 0)),
                  pl.BlockSpec((None, tk, HEAD_DIM), lambda h, i, j: (h, j, 0))],
        out_specs=pl.BlockSpec((None, tq, HEAD_DIM), lambda h, i, j: (h, i, 0)),
        out_shape=jax.ShapeDtypeStruct((H, t_lat, HEAD_DIM), F32),
        compiler_params=_params(("arbitrary", "arbitrary", "arbitrary")),
    )(lg, q, k, v)


def ret_bwd(q, k, v, do, lg, *, t_lat, tq, tk, name):
    H, T, _ = k.shape
    t_ctx = T - t_lat

    def body(lg_ref, q_ref, k_ref, v_ref, do_ref, dq_ref, dk_ref, dv_ref, dlg_ref):
        h, kj, qi = pl.program_id(0), pl.program_id(1), pl.program_id(2)

        @pl.when((kj == 0) & (qi == 0))
        def _():
            dq_ref[...] = jnp.zeros_like(dq_ref)
            dlg_ref[...] = jnp.zeros_like(dlg_ref)

        @pl.when(qi == 0)
        def _():
            dk_ref[...] = jnp.zeros_like(dk_ref)
            dv_ref[...] = jnp.zeros_like(dv_ref)

        qv, kv, vv = q_ref[...], k_ref[...], v_ref[...]
        dob = do_ref[...].astype(BF16)
        st = _nt(kv, qv)
        dwt = _nt(vv, dob)
        m = (kj * tk + lax.broadcasted_iota(jnp.int32, (tk, tq), 0)).astype(F32)
        n = (qi * tq + lax.broadcasted_iota(jnp.int32, (tk, tq), 1)).astype(F32)
        ef, eb, df, db = _decay(n, m, lg_ref[0, h], lg_ref[1, h], t_lat, t_ctx)
        dec = ef + eb
        dv_ref[...] += _nn((st * dec).astype(BF16), dob)
        dst = (dwt * dec).astype(BF16)
        dk_ref[...] += _nn(dst, qv)
        rows = pl.ds(pl.multiple_of(qi * tq, tq), tq)
        dq_ref[rows, :] += _tn(dst, kv)
        gs = dwt * st
        dlg_ref[0:1, :] += jnp.sum(gs * (ef * df))
        dlg_ref[1:2, :] += jnp.sum(gs * (eb * db))

    kspec = pl.BlockSpec((None, tk, HEAD_DIM), lambda h, j, i: (h, j, 0))
    qspec = pl.BlockSpec((None, tq, HEAD_DIM), lambda h, j, i: (h, i, 0))
    return pl.pallas_call(
        body, name=name, grid=(H, T // tk, t_lat // tq),
        in_specs=[pl.BlockSpec(memory_space=pltpu.SMEM), qspec, kspec, kspec, qspec],
        out_specs=[pl.BlockSpec((None, t_lat, HEAD_DIM), lambda h, j, i: (h, 0, 0)), kspec, kspec,
                   pl.BlockSpec((None, 8, LANE), lambda h, j, i: (h, 0, 0))],
        out_shape=[jax.ShapeDtypeStruct((H, t_lat, HEAD_DIM), F32), jax.ShapeDtypeStruct((H, T, HEAD_DIM), F32),
                   jax.ShapeDtypeStruct((H, T, HEAD_DIM), F32), jax.ShapeDtypeStruct((H, 8, LANE), F32)],
        compiler_params=_params(("arbitrary", "arbitrary", "arbitrary")),
    )(lg, q, k, v, do)


def _group_norm(o):
    mu = jnp.mean(o, axis=-1, keepdims=True)
    ctr = o - mu
    r = lax.rsqrt(jnp.mean(ctr * ctr, axis=-1, keepdims=True) + GN_EPS)
    return ctr * r, r


def ret_post_fwd(o, p, *, heads, d_model, name):
    H, t_lat, _ = o.shape
    T = p.shape[0]

    def body(o_ref, g_ref, z_ref):
        on, _ = _group_norm(o_ref[...])
        _, sl = _silu_parts(g_ref[0:t_lat, :])
        z_ref[...] = (on * sl).astype(BF16)

    return pl.pallas_call(
        body, name=name, grid=(H,),
        in_specs=[pl.BlockSpec((None, t_lat, HEAD_DIM), lambda h: (h, 0, 0)),
                  pl.BlockSpec((T, HEAD_DIM), lambda h: (0, 3 * heads + h))],
        out_specs=pl.BlockSpec((t_lat, HEAD_DIM), lambda h: (0, h)),
        out_shape=jax.ShapeDtypeStruct((t_lat, d_model), BF16),
        compiler_params=_params(("arbitrary",)),
    )(o, p)


def _shift_rows(u, k):
    rows = u.shape[0]
    row = lax.broadcasted_iota(jnp.int32, u.shape, 0)
    rolled = pltpu.roll(u, k % rows, axis=0)
    return jnp.where((row >= k) & (row < rows + k), rolled, 0.0)


def conv_fwd(z, p, w_conv, *, heads, t_lat, name):
    T = p.shape[0]
    cb_n = w_conv.shape[1] // LANE
    base = 4 * heads

    def body(z_in, bg_ref, cg_ref, u_ref, w_ref, z_ref):
        cu = cg_ref[0:t_lat, :] * u_ref[0:t_lat, :]
        c3 = _shift_rows(cu, 1) * w_ref[0:1, :] + cu * w_ref[1:2, :] + _shift_rows(cu, -1) * w_ref[2:3, :]
        z_ref[...] = (bg_ref[0:t_lat, :] * c3).astype(BF16)

    def col(part):
        return pl.BlockSpec((T, LANE), lambda cb: (0, base + part * cb_n + cb))

    return pl.pallas_call(
        body, name=name, grid=(cb_n,),
        in_specs=[ANY, col(0), col(1), col(2), pl.BlockSpec((CONV_WIDTH, LANE), lambda cb: (0, cb))],
        out_specs=pl.BlockSpec((t_lat, LANE), lambda cb: (0, heads + cb)),
        out_shape=jax.ShapeDtypeStruct(z.shape, z.dtype),
        input_output_aliases={0: 0},
        compiler_params=_params(("arbitrary",)),
    )(z, p, p, p, w_conv)


def out_proj(z, w_out, x, mod, *, tm, name):
    t_lat, D = z.shape

    def body(z_ref, w_ref, x_ref, mod_ref, xo_ref, y_ref):
        y = _nn(z_ref[...], w_ref[...])
        y_ref[...] = y.astype(BF16)
        xo_ref[...] = x_ref[...] + mod_ref[2:3, :] * y

    tok = pl.BlockSpec((tm, D), lambda i: (i, 0))
    return pl.pallas_call(
        body, name=name, grid=(t_lat // tm,),
        in_specs=[tok, pl.BlockSpec((D, D), lambda i: (0, 0)), tok, pl.BlockSpec((None, 3, D), lambda i: (0, 0, 0))],
        out_specs=[tok, tok],
        out_shape=[jax.ShapeDtypeStruct((t_lat, D), F32), jax.ShapeDtypeStruct((t_lat, D), BF16)],
        compiler_params=_params(("arbitrary",)),
    )(z, w_out, x, mod)


def out_proj_bwd(dxo, y, w_out, mod, *, tm, name):
    t_lat, D = dxo.shape

    def body(dxo_ref, y_ref, w_ref, mod_ref, dz_ref, dy_ref, red_ref):
        @pl.when(pl.program_id(0) == 0)
        def _():
            red_ref[...] = jnp.zeros_like(red_ref)

        dxo_v = dxo_ref[...]
        dyb = (mod_ref[2:3, :] * dxo_v).astype(BF16)
        dy_ref[...] = dyb
        dz_ref[...] = _nt(dyb, w_ref[...])
        red_ref[2:3, :] += jnp.sum(dxo_v * y_ref[...].astype(F32), axis=0, keepdims=True)

    tok = pl.BlockSpec((tm, D), lambda i: (i, 0))
    return pl.pallas_call(
        body, name=name, grid=(t_lat // tm,),
        in_specs=[tok, tok, pl.BlockSpec((D, D), lambda i: (0, 0)), pl.BlockSpec((None, 3, D), lambda i: (0, 0, 0))],
        out_specs=[tok, tok, pl.BlockSpec((8, D), lambda i: (0, 0))],
        out_shape=[jax.ShapeDtypeStruct((t_lat, D), F32), jax.ShapeDtypeStruct((t_lat, D), BF16),
                   jax.ShapeDtypeStruct((8, D), F32)],
        compiler_params=_params(("arbitrary",)),
    )(dxo, y, w_out, mod)


def ret_post_bwd(dz, o, p, *, heads, name):
    H, t_lat, _ = o.shape
    T, in_w = p.shape

    def body(dz_ref, o_ref, g_ref, do_ref, dp_ref):
        on, r = _group_norm(o_ref[...])
        gg = g_ref[0:t_lat, :]
        sig, sl = _silu_parts(gg)
        dret = dz_ref[...]
        don = dret * sl
        do_ref[...] = r * (don - jnp.mean(don, axis=-1, keepdims=True)
                           - on * jnp.mean(don * on, axis=-1, keepdims=True))
        dp_ref[0:t_lat, :] = (dret * on * _dsilu(gg, sig)).astype(BF16)
        dp_ref[t_lat:T, :] = jnp.zeros((T - t_lat, HEAD_DIM), BF16)

    return pl.pallas_call(
        body, name=name, grid=(H,),
        in_specs=[pl.BlockSpec((t_lat, HEAD_DIM), lambda h: (0, h)),
                  pl.BlockSpec((None, t_lat, HEAD_DIM), lambda h: (h, 0, 0)),
                  pl.BlockSpec((T, HEAD_DIM), lambda h: (0, 3 * heads + h))],
        out_specs=[pl.BlockSpec((None, t_lat, HEAD_DIM), lambda h: (h, 0, 0)),
                   pl.BlockSpec((T, HEAD_DIM), lambda h: (0, 3 * heads + h))],
        out_shape=[jax.ShapeDtypeStruct((H, t_lat, HEAD_DIM), F32), jax.ShapeDtypeStruct((T, in_w), BF16)],
        compiler_params=_params(("arbitrary",)),
    )(dz, o, p)


def conv_bwd(dp, dz, p, w_conv, *, heads, t_lat, name):
    T = p.shape[0]
    cb_n = w_conv.shape[1] // LANE
    base = 4 * heads

    def body(dp_in, dz_ref, bg_ref, cg_ref, u_ref, w_ref, dp_ref, dw_ref):
        part = pl.program_id(1)
        cg, u = cg_ref[0:t_lat, :], u_ref[0:t_lat, :]
        cu = cg * u
        dconv = dz_ref[...]
        dp_ref[t_lat:T, :] = jnp.zeros((T - t_lat, LANE), BF16)

        @pl.when(part == 0)
        def _():
            c3 = _shift_rows(cu, 1) * w_ref[0:1, :] + cu * w_ref[1:2, :] + _shift_rows(cu, -1) * w_ref[2:3, :]
            dp_ref[0:t_lat, :] = (dconv * c3).astype(BF16)
            dc3 = dconv * bg_ref[0:t_lat, :]
            dw_ref[0:1, :] = jnp.sum(dc3 * _shift_rows(cu, 1), axis=0, keepdims=True)
            dw_ref[1:2, :] = jnp.sum(dc3 * cu, axis=0, keepdims=True)
            dw_ref[2:3, :] = jnp.sum(dc3 * _shift_rows(cu, -1), axis=0, keepdims=True)

        @pl.when(part > 0)
        def _():
            dc3 = dconv * bg_ref[0:t_lat, :]
            dcu = (_shift_rows(dc3, -1) * w_ref[0:1, :] + dc3 * w_ref[1:2, :] + _shift_rows(dc3, 1) * w_ref[2:3, :])
            dp_ref[0:t_lat, :] = (dcu * jnp.where(part == 1, u, cg)).astype(BF16)

    def col(part):
        return pl.BlockSpec((T, LANE), lambda cb, pt: (0, base + part * cb_n + cb))

    return pl.pallas_call(
        body, name=name, grid=(cb_n, 3),
        in_specs=[ANY, pl.BlockSpec((t_lat, LANE), lambda cb, pt: (0, heads + cb)), col(0), col(1), col(2),
                  pl.BlockSpec((CONV_WIDTH, LANE), lambda cb, pt: (0, cb))],
        out_specs=[pl.BlockSpec((T, LANE), lambda cb, pt: (0, base + pt * cb_n + cb)),
                   pl.BlockSpec((CONV_WIDTH, LANE), lambda cb, pt: (0, cb))],
        out_shape=[jax.ShapeDtypeStruct(dp.shape, dp.dtype), jax.ShapeDtypeStruct(w_conv.shape, F32)],
        input_output_aliases={0: 0},
        compiler_params=_params(("arbitrary", "arbitrary")),
    )(dp, dz, p, p, p, w_conv)


def ret_unprep(dp, dq, dk, dv, cos, sin, *, t_lat, tm, name):
    H, T, _ = dk.shape
    n_lat = t_lat // tm

    def body(dp_in, dq_ref, dk_ref, dv_ref, c_ref, s_ref, dp_ref):
        part, i = pl.program_id(0), pl.program_id(2)
        cos_v, sin_v = c_ref[...], s_ref[...]

        @pl.when(part == 0)
        def _():
            d = jnp.where(i < n_lat, dq_ref[...], 0.0)
            dp_ref[...] = _rope(d, cos_v, -sin_v).astype(BF16)

        @pl.when(part == 1)
        def _():
            dp_ref[...] = (_rope(dk_ref[...], cos_v, -sin_v) * K_SCALE).astype(BF16)

        @pl.when(part == 2)
        def _():
            dp_ref[...] = dv_ref[...].astype(BF16)

    full = pl.BlockSpec((None, tm, HEAD_DIM), lambda pt, h, i: (h, i, 0))
    tab = pl.BlockSpec((tm, HEAD_DIM), lambda pt, h, i: (i, 0))
    return pl.pallas_call(
        body, name=name, grid=(3, H, T // tm),
        in_specs=[ANY, pl.BlockSpec((None, tm, HEAD_DIM), lambda pt, h, i: (h, jnp.minimum(i, n_lat - 1), 0)),
                  full, full, tab, tab],
        out_specs=pl.BlockSpec((tm, HEAD_DIM), lambda pt, h, i: (i, pt * H + h)),
        out_shape=jax.ShapeDtypeStruct(dp.shape, dp.dtype),
        input_output_aliases={0: 0},
        compiler_params=_params(("arbitrary", "arbitrary", "arbitrary")),
    )(dp, dq, dk, dv, cos, sin)


def norm_mod_fwd(x, mod, gn, *, tm, name):
    T, D = x.shape

    def body(x_ref, mod_ref, g_ref, h_ref):
        h_ref[...] = _norm_mod(x_ref[...], g_ref[...], mod_ref[0:1, :], mod_ref[1:2, :])

    tok = pl.BlockSpec((tm, D), lambda i: (i, 0))
    return pl.pallas_call(
        body, name=name, grid=(T // tm,),
        in_specs=[tok, pl.BlockSpec((None, 3, D), lambda i: (0, 0, 0)), pl.BlockSpec((1, D), lambda i: (0, 0))],
        out_specs=tok, out_shape=jax.ShapeDtypeStruct((T, D), F32),
        compiler_params=_params(("arbitrary",)),
    )(x, mod, gn)


def norm_mod_bwd(x, dh, dxo, mod, gn, *, tm, name):
    T, D = x.shape

    def body(x_ref, dh_ref, dxo_ref, mod_ref, g_ref, dx_ref, red_ref):
        @pl.when(pl.program_id(0) == 0)
        def _():
            red_ref[...] = jnp.zeros_like(red_ref)

        dxn, dshift, dscale, dg = _norm_mod_bwd(x_ref[...], dh_ref[...], g_ref[...], mod_ref[1:2, :])
        dx_ref[...] = dxo_ref[...] + dxn
        red_ref[0:1, :] += dshift
        red_ref[1:2, :] += dscale
        red_ref[3:4, :] += dg

    tok = pl.BlockSpec((tm, D), lambda i: (i, 0))
    return pl.pallas_call(
        body, name=name, grid=(T // tm,),
        in_specs=[tok, tok, tok, pl.BlockSpec((None, 3, D), lambda i: (0, 0, 0)), pl.BlockSpec((1, D), lambda i: (0, 0))],
        out_specs=[tok, pl.BlockSpec((8, D), lambda i: (0, 0))],
        out_shape=[jax.ShapeDtypeStruct((T, D), F32), jax.ShapeDtypeStruct((8, D), F32)],
        compiler_params=_params(("arbitrary",)),
    )(x, dh, dxo, mod, gn)


def _window_sum(u, w, lead):
    T, C = u.shape
    ext = jnp.concatenate([u, jnp.zeros((POOL_PAD, C), F32)], axis=0)
    k = 1
    while k < w:
        ext = ext + _shift_rows(ext, k)
        k *= 2
    return _shift_rows(ext, -lead)[0:T, :]


def _window_count(T, C, w):
    t = lax.broadcasted_iota(jnp.int32, (T, C), 0)
    lo = jnp.clip(t - w // 2, 0, T)
    hi = jnp.clip(t + (w - w // 2), 0, T)
    return (hi - lo).astype(F32)


def pool_fwd(h, x, pool_w, scale, mod, *, name):
    T, D = h.shape
    G, Cg, _ = pool_w.shape
    ns = Cg // LANE

    def body(h_ref, x_ref, w_ref, sc_ref, mod_ref, xo_ref, pl_ref, yl_ref, acc):
        g, s = pl.program_id(0), pl.program_id(1)
        hv = h_ref[...]
        for gi, win in enumerate(POOL_WINDOWS):
            @pl.when(g == gi)
            def _():
                mean = _window_sum(hv, win, win // 2 - 1) / _window_count(T, LANE, win)
                pooled = (mean - hv).astype(BF16)
                pl_ref[...] = pooled
                contrib = _nn(pooled, w_ref[...])

                @pl.when(s == 0)
                def _():
                    acc[...] = contrib

                @pl.when(s > 0)
                def _():
                    acc[...] += contrib

        @pl.when(s == ns - 1)
        def _():
            yl = acc[...]
            yl_ref[...] = yl.astype(BF16)
            xo_ref[...] = x_ref[...] + mod_ref[2:3, :] * (yl * sc_ref[...])

    grp = pl.BlockSpec((T, Cg), lambda g, s: (0, g))
    sub = pl.BlockSpec((T, LANE), lambda g, s: (0, g * ns + s))
    return pl.pallas_call(
        body, name=name, grid=(G, ns),
        in_specs=[sub, grp, pl.BlockSpec((None, LANE, Cg), lambda g, s: (g, s, 0)),
                  pl.BlockSpec((1, Cg), lambda g, s: (0, g)), pl.BlockSpec((None, 3, Cg), lambda g, s: (0, 0, g))],
        out_specs=[grp, sub, grp],
        out_shape=[jax.ShapeDtypeStruct((T, D), F32), jax.ShapeDtypeStruct((T, D), BF16),
                   jax.ShapeDtypeStruct((T, D), BF16)],
        scratch_shapes=[pltpu.VMEM((T, Cg), F32)],
        compiler_params=_params(("arbitrary", "arbitrary")),
    )(h, x, pool_w, scale, mod)


def pool_bwd(dxo, pooled, yl, pool_w, scale, mod, *, name):
    T, D = dxo.shape
    G, Cg, _ = pool_w.shape
    ns = Cg // LANE

    def body(dxo_ref, pl_ref, yl_ref, w_ref, sc_ref, mod_ref, dh_ref, dw_ref, red_ref, dyl):
        g, s = pl.program_id(0), pl.program_id(1)

        @pl.when(s == 0)
        def _():
            dxo_v = dxo_ref[...]
            ylv = yl_ref[...].astype(F32)
            dy = mod_ref[2:3, :] * dxo_v
            dyl[...] = (dy * sc_ref[...]).astype(BF16)
            red_ref[...] = jnp.zeros_like(red_ref)
            red_ref[2:3, :] = jnp.sum(dxo_v * (ylv * sc_ref[...]), axis=0, keepdims=True)
            red_ref[4:5, :] = jnp.sum(dy * ylv, axis=0, keepdims=True)

        dylv = dyl[...]
        dpool = _nt(dylv, w_ref[...])
        dw_ref[...] = _tn(pl_ref[...], dylv).astype(BF16)
        for gi, win in enumerate(POOL_WINDOWS):
            @pl.when(g == gi)
            def _():
                spread = _window_sum(dpool / _window_count(T, LANE, win), win, win // 2)
                dh_ref[...] = spread - dpool

    grp = pl.BlockSpec((T, Cg), lambda g, s: (0, g))
    sub = pl.BlockSpec((T, LANE), lambda g, s: (0, g * ns + s))
    wsub = pl.BlockSpec((None, LANE, Cg), lambda g, s: (g, s, 0))
    return pl.pallas_call(
        body, name=name, grid=(G, ns),
        in_specs=[grp, sub, grp, wsub, pl.BlockSpec((1, Cg), lambda g, s: (0, g)),
                  pl.BlockSpec((None, 3, Cg), lambda g, s: (0, 0, g))],
        out_specs=[sub, wsub, pl.BlockSpec((8, Cg), lambda g, s: (0, g))],
        out_shape=[jax.ShapeDtypeStruct((T, D), F32), jax.ShapeDtypeStruct((G, Cg, Cg), BF16),
                   jax.ShapeDtypeStruct((8, D), F32)],
        scratch_shapes=[pltpu.VMEM((T, Cg), BF16)],
        compiler_params=_params(("arbitrary", "arbitrary")),
    )(dxo, pooled, yl, pool_w, scale, mod)


def final_loss(x, gn, target, *, tm, name):
    T, D = x.shape

    def body(x_ref, g_ref, t_ref, loss_ref, dx_ref, red_ref):
        @pl.when(pl.program_id(0) == 0)
        def _():
            loss_ref[...] = jnp.zeros_like(loss_ref)
            red_ref[...] = jnp.zeros_like(red_ref)

        xx, g = x_ref[...], g_ref[...]
        r = lax.rsqrt(jnp.mean(xx * xx, axis=-1, keepdims=True) + EPS)
        xhat = xx * r
        err = xhat * g - t_ref[...]
        loss_ref[...] += 0.5 * jnp.sum(jnp.mean(err * err, axis=-1, keepdims=True))
        dy = err / D
        red_ref[0:1, :] += jnp.sum(dy * xhat, axis=0, keepdims=True)
        dxh = dy * g
        dx_ref[...] = r * (dxh - xhat * jnp.mean(dxh * xhat, axis=-1, keepdims=True))

    tok = pl.BlockSpec((tm, D), lambda i: (i, 0))
    return pl.pallas_call(
        body, name=name, grid=(T // tm,),
        in_specs=[tok, pl.BlockSpec((1, D), lambda i: (0, 0)), tok],
        out_specs=[pl.BlockSpec((8, LANE), lambda i: (0, 0)), tok, pl.BlockSpec((8, D), lambda i: (0, 0))],
        out_shape=[jax.ShapeDtypeStruct((8, LANE), F32), jax.ShapeDtypeStruct((T, D), F32),
                   jax.ShapeDtypeStruct((8, D), F32)],
        compiler_params=_params(("arbitrary",)),
    )(x, gn, target)


def _rope_tables(t_lat, t_ctx):
    quarter = HEAD_DIM // 4
    pos = jnp.arange(t_lat)
    inv = ROPE_BASE ** (-jnp.arange(quarter, dtype=F32) / quarter)
    ang = jnp.concatenate([(pos // GRID_W).astype(F32)[:, None] * inv, (pos % GRID_W).astype(F32)[:, None] * inv], axis=-1)
    cos, sin = jnp.cos(ang), jnp.sin(ang)
    cos = jnp.concatenate([jnp.concatenate([cos, cos], axis=-1), jnp.ones((t_ctx, HEAD_DIM), F32)], axis=0)
    sin = jnp.concatenate([jnp.concatenate([-sin, sin], axis=-1), jnp.zeros((t_ctx, HEAD_DIM), F32)], axis=0)
    return cos, sin


def _ffn_grads(h, da, db, s, dy, tag):
    nb, T, cw = da.shape
    D = h.shape[1]
    tk = 512 if T % 512 == 0 else 256
    g_gate = matmul_tn(da, h, nb=nb, m=cw, n=D, a_blocked=True, b_blocked=False, tk=tk, name=f"wgrad_gate_{tag}")
    g_up = matmul_tn(db, h, nb=nb, m=cw, n=D, a_blocked=True, b_blocked=False, tk=tk, name=f"wgrad_up_{tag}")
    g_down = matmul_tn(s, dy, nb=nb, m=cw, n=D, a_blocked=True, b_blocked=False, tk=tk, name=f"wgrad_down_{tag}")
    return [g_gate, g_up, g_down]


def kernel(x, c, ctx, c_ctx, w_mod, b_mod, norm_ffn1, norm_mix, norm_ffn2, ffn1_w_gate, ffn1_w_up, ffn1_w_down, ffn2_w_gate, ffn2_w_up, ffn2_w_down, mix_w_in, mix_w_conv, mix_w_out, ret_decay_fwd, ret_decay_bwd, pool_w, pool_scale, final_norm, loss_target, m_c_ctx, m_w_mod, m_b_mod, m_norm_ffn1, m_norm_mix, m_norm_ffn2, m_ffn1_w_gate, m_ffn1_w_up, m_ffn1_w_down, m_ffn2_w_gate, m_ffn2_w_up, m_ffn2_w_down, m_mix_w_in, m_mix_w_conv, m_mix_w_out, m_ret_decay_fwd, m_ret_decay_bwd, m_pool_w, m_pool_scale, m_final_norm, v_c_ctx, v_w_mod, v_b_mod, v_norm_ffn1, v_norm_mix, v_norm_ffn2, v_ffn1_w_gate, v_ffn1_w_up, v_ffn1_w_down, v_ffn2_w_gate, v_ffn2_w_up, v_ffn2_w_down, v_mix_w_in, v_mix_w_conv, v_mix_w_out, v_ret_decay_fwd, v_ret_decay_bwd, v_pool_w, v_pool_scale, v_final_norm):
    t_lat, D = x.shape[1], x.shape[2]
    t_ctx = ctx.shape[1]
    T = t_lat + t_ctx
    heads = ret_decay_fwd.shape[1]
    mod_w = w_mod.shape[2]
    tm = 256
    tmf = 512 if t_lat % 512 == 0 else 256
    tq = 512 if t_lat % 512 == 0 else 256
    tk = T // 3 if (T % 3 == 0 and (T // 3) % 256 == 0) else 256

    ax, ay, ac = lax.axis_index("x"), lax.axis_index("y"), lax.axis_index("c")
    me = 4 * ax + 2 * ay + ac
    own_block = jnp.reshape(me, (1,)).astype(jnp.int32)
    peer_blocks = jnp.stack([4 * (1 - ax) + 2 * ay + ac, 4 * ax + 2 * (1 - ay) + ac,
                             4 * (1 - ax) + 2 * (1 - ay) + ac]).astype(jnp.int32)

    (c_all,) = all_gather([c], name="gather_cond")
    cond16 = jnp.concatenate([c_all.reshape(N_DEV, D), c_ctx[None, :], jnp.zeros((7, D), F32)], axis=0)
    b_blk = lax.dynamic_slice_in_dim(b_mod, me * mod_w, mod_w, axis=1)[:, None, :]
    m_blk = adaln_fwd(cond16, w_mod, b_blk, name="adaln_fwd")
    m_all, w_conv, pscale = all_gather([m_blk, mix_w_conv[0], pool_scale], name="gather_mod")
    mods = jnp.transpose(m_all, (1, 2, 0, 3)).reshape(2, 16, N_MOD, D)
    mod_x = lax.dynamic_index_in_dim(mods, me, axis=1, keepdims=False)
    mod_c = mods[0, 8]
    w_conv = jnp.transpose(w_conv, (1, 0, 2)).reshape(CONV_WIDTH, -1)
    pscale = pscale.reshape(1, D)

    def ffn_shards(wg, wu, wd, l):
        return [jnp.swapaxes(wg[l], 0, 1).astype(BF16), jnp.swapaxes(wu[l], 0, 1).astype(BF16), wd[l].astype(BF16)]

    n_grp, grp_rows, grp_w = pool_w.shape[1:]
    shards = {
        "01": ffn_shards(ffn1_w_gate, ffn1_w_up, ffn1_w_down, 0),
        "mix": [mix_w_in[0].astype(BF16), mix_w_out[0].astype(BF16)],
        "02": ffn_shards(ffn2_w_gate, ffn2_w_up, ffn2_w_down, 0),
        "11": ffn_shards(ffn1_w_gate, ffn1_w_up, ffn1_w_down, 1) + [pool_w[0].astype(BF16).reshape(n_grp * grp_rows, grp_w)],
        "12": ffn_shards(ffn2_w_gate, ffn2_w_up, ffn2_w_down, 1),
    }
    started = {}

    def start(tag):
        started[tag] = gather_start(shards[tag], name=f"gather_start_{tag}")
        return started[tag][-1][0:1, 0:1]

    def finish(tag, after):
        own, lands = gather_wait(started[tag], after, name=f"gather_wait_{tag}")
        return gather_finish(own, lands, name=f"gather_finish_{tag}")

    lg = jnp.concatenate([jax.nn.log_sigmoid(ret_decay_fwd), jax.nn.log_sigmoid(ret_decay_bwd)], axis=0)
    cos, sin = _rope_tables(t_lat, t_ctx)

    def mod3(l, k, with_ctx=False, tie=None):
        rows = mod_x[l, 3 * k:3 * k + 3][None]
        if with_ctx:
            rows = jnp.concatenate([rows, mod_c[3 * k:3 * k + 3][None]], axis=0)
        return rows if tie is None else rows + tie

    tie_01 = start("01")
    tie_mix = start("mix")
    x0 = jnp.concatenate([x[0], ctx[0]], axis=0) + (tie_01 + tie_mix)
    wg01, wu01, wd01 = finish("01", x0)
    tie_02 = start("02")
    x1, h1, a1, b1, y1 = ffn_fwd(x0, mod3(0, 0, True, tie_02), norm_ffn1[0:1], wg01, wu01, wd01,
                                 tm=tm, t_first=t_lat, name="ffn_fwd_01")
    w_in, w_out = finish("mix", h1)
    w_out = w_out.reshape(D, D)
    tie_11 = start("11")
    p, hm = proj_in(x1, mod3(0, 1, True, tie_11), norm_mix[0:1], w_in, tm=tm, t_first=t_lat, name="proj_in")
    qr, kr, vr = ret_prep(p, cos, sin, heads=heads, tm=tm, name="ret_prep")
    o = ret_fwd(qr, kr, vr, lg, t_lat=t_lat, tq=tq, tk=tk, name="ret_fwd")
    z = ret_post_fwd(o, p, heads=heads, d_model=D, name="ret_post_fwd")
    z = conv_fwd(z, p, w_conv, heads=heads, t_lat=t_lat, name="conv_fwd")
    x2, ym = out_proj(z, w_out, x1, mod3(0, 1), tm=tm, name="out_proj")
    wg02, wu02, wd02 = finish("02", ym)
    tie_12 = start("12")
    x3, h3, a3, b3, y3 = ffn_fwd(x2, mod3(0, 2, tie=tie_12), norm_ffn2[0:1], wg02, wu02, wd02, tm=tmf, t_first=t_lat, name="ffn_fwd_02")
    wg11, wu11, wd11, pw = finish("11", h3)
    pw = jnp.transpose(pw.reshape(N_DEV, n_grp, grp_rows, grp_w), (1, 0, 2, 3)).reshape(n_grp, N_DEV * grp_rows, grp_w)
    x4, h4, a4, b4, y4 = ffn_fwd(x3, mod3(1, 0), norm_ffn1[1:2], wg11, wu11, wd11, tm=tmf, t_first=t_lat, name="ffn_fwd_11")
    hp = norm_mod_fwd(x4, mod3(1, 1), norm_mix[1:2], tm=tm, name="pool_norm_fwd")
    x5, pooled, yl = pool_fwd(hp, x4, pw, pscale, mod3(1, 1), name="pool_fwd")
    wg12, wu12, wd12 = finish("12", yl)
    x6, h6, a6, b6, y6 = ffn_fwd(x5, mod3(1, 2), norm_ffn2[1:2], wg12, wu12, wd12, tm=tmf, t_first=t_lat, name="ffn_fwd_12")
    loss_part, dx6, red_fn = final_loss(x6, final_norm[None, :], loss_target[0], tm=tm, name="final_loss")
    loss = lax.psum(loss_part[0, 0], ("x", "y", "c"))

    reducing = {}

    def reduce_start(tag, grads):
        res = rs_sibling_exchange(grads, name=f"rs_sibling_{tag}")
        n = len(grads)
        stages, directs = res[:n], res[n:]
        sums = [chip_sum(g, st, peer_blocks, name=f"chip_sum_{tag}_{t}") for t, (g, st) in enumerate(zip(grads, stages))]
        handle = chip_exchange_start(sums, name=f"rs_start_{tag}")
        reducing[tag] = (grads, directs, handle)
        return handle[-1][0:1, 0:1]

    dx5, da6, db6, s6, dy6, red12 = ffn_bwd_tok(x5, dx6, y6, a6, b6, mod3(1, 2), norm_ffn2[1:2], wg12, wu12, wd12,
                                                tm=tmf, t_first=t_lat, name="ffn_bwd_12")
    tie = reduce_start("12", _ffn_grads(h6, da6, db6, s6, dy6, "12"))
    dhp, g_pw, red_pool = pool_bwd(dx5, pooled, yl, pw, pscale, mod3(1, 1, tie=tie), name="pool_bwd")
    dx4, red_pn = norm_mod_bwd(x4, dhp, dx5, mod3(1, 1), norm_mix[1:2], tm=tm, name="pool_norm_bwd")
    dx3, da4, db4, s4, dy4, red11 = ffn_bwd_tok(x3, dx4, y4, a4, b4, mod3(1, 0), norm_ffn1[1:2], wg11, wu11, wd11,
                                                tm=tmf, t_first=t_lat, name="ffn_bwd_11")
    g_pw = jnp.transpose(g_pw.reshape(n_grp, N_DEV, grp_rows, grp_w), (1, 0, 2, 3)).reshape(N_DEV, n_grp * grp_rows, grp_w)
    tie = reduce_start("11", list(_ffn_grads(h4, da4, db4, s4, dy4, "11")) + [g_pw])
    dx2, da3, db3, s3, dy3, red02 = ffn_bwd_tok(x2, dx3, y3, a3, b3, mod3(0, 2, tie=tie), norm_ffn2[0:1], wg02, wu02, wd02,
                                                tm=tmf, t_first=t_lat, name="ffn_bwd_02")
    tie = reduce_start("02", _ffn_grads(h3, da3, db3, s3, dy3, "02"))
    dz, dym, red_op = out_proj_bwd(dx2, ym, w_out, mod3(0, 1, tie=tie), tm=tm, name="out_proj_bwd")
    g_wout = matmul_tn(z, dym, nb=N_DEV, m=D // N_DEV, n=D, a_blocked=False, b_blocked=False,
                       tk=512 if t_lat % 512 == 0 else 256, name="wgrad_out")
    do, dp = ret_post_bwd(dz, o, p, heads=heads, name="ret_post_bwd")
    dp, g_conv = conv_bwd(dp, dz, p, w_conv, heads=heads, t_lat=t_lat, name="conv_bwd")
    dq, dk, dv, dlg = ret_bwd(qr, kr, vr, do, lg, t_lat=t_lat, tq=tq, tk=tk, name="ret_bwd")
    dp = ret_unprep(dp, dq, dk, dv, cos, sin, t_lat=t_lat, tm=tm, name="ret_unprep")
    dx1, red_mix = proj_bwd_tok(x1, dx2, dp, mod3(0, 1, True), norm_mix[0:1], w_in, tm=tm, name="proj_bwd")
    g_win = matmul_tn(hm, dp, nb=N_DEV, m=D, n=w_in.shape[2], a_blocked=False, b_blocked=False, tk=256, name="wgrad_in")
    tie = reduce_start("mix", [g_win, g_wout])
    dx0, da1, db1, s1, dy1, red01 = ffn_bwd_tok(x0, dx1, y1, a1, b1, mod3(0, 0, True, tie), norm_ffn1[0:1], wg01, wu01, wd01,
                                                tm=tm, t_first=t_lat, name="ffn_bwd_01")
    reduce_start("01", _ffn_grads(h1, da1, db1, s1, dy1, "01"))
    grad_x = dx0[:t_lat][None]

    res = {}
    last = [grad_x]

    def reduce_finish(tag):
        grads, directs, handle = reducing[tag]
        landed = chip_exchange_wait(handle, last[0], name=f"rs_wait_{tag}")
        return list(zip(grads, directs, landed))

    def big(w, m, v, l, part, prev, nm, transposed=False):
        if transposed:
            w, m, v = (jnp.swapaxes(a, 1, 2) for a in (w, m, v))
        outs = adam_reduced(own_block, part[0], part[1], part[2], w, m, v, l, prev, name=nm)
        last[0] = outs[0]
        return outs

    ffn1 = [(ffn1_w_gate, m_ffn1_w_gate, v_ffn1_w_gate), (ffn1_w_up, m_ffn1_w_up, v_ffn1_w_up),
            (ffn1_w_down, m_ffn1_w_down, v_ffn1_w_down)]
    ffn2 = [(ffn2_w_gate, m_ffn2_w_gate, v_ffn2_w_gate), (ffn2_w_up, m_ffn2_w_up, v_ffn2_w_up),
            (ffn2_w_down, m_ffn2_w_down, v_ffn2_w_down)]
    kinds = ["gate", "up", "down"]
    half = {}
    parts = reduce_finish("12")
    for t, (w, m, v) in enumerate(ffn2):
        half[f"ffn2_w_{kinds[t]}"] = big(w, m, v, 1, parts[t], None, f"adam_ffn2_w_{kinds[t]}_1", transposed=t < 2)
    parts = reduce_finish("11")
    for t, (w, m, v) in enumerate(ffn1):
        half[f"ffn1_w_{kinds[t]}"] = big(w, m, v, 1, parts[t], None, f"adam_ffn1_w_{kinds[t]}_1", transposed=t < 2)
    res["pool_w"] = [a.reshape(pool_w.shape) for a in
                     big(pool_w.reshape(1, n_grp * grp_rows, grp_w), m_pool_w.reshape(1, n_grp * grp_rows, grp_w),
                         v_pool_w.reshape(1, n_grp * grp_rows, grp_w), 0, parts[3], None, "adam_pool_w")]
    parts = reduce_finish("02")
    for t, (w, m, v) in enumerate(ffn2):
        nm = f"ffn2_w_{kinds[t]}"
        outs = big(w, m, v, 0, parts[t], half[nm], f"adam_{nm}_0", transposed=t < 2)
        res[nm] = [jnp.swapaxes(a, 1, 2) for a in outs] if t < 2 else outs
    parts = reduce_finish("mix")
    res["mix_w_in"] = big(mix_w_in, m_mix_w_in, v_mix_w_in, 0, parts[0], None, "adam_mix_w_in")
    res["mix_w_out"] = big(mix_w_out, m_mix_w_out, v_mix_w_out, 0, parts[1], None, "adam_mix_w_out")
    parts = reduce_finish("01")
    for t, (w, m, v) in enumerate(ffn1):
        nm = f"ffn1_w_{kinds[t]}"
        outs = big(w, m, v, 0, parts[t], half[nm], f"adam_{nm}_0", transposed=t < 2)
        res[nm] = [jnp.swapaxes(a, 1, 2) for a in outs] if t < 2 else outs

    dm_x = jnp.stack([jnp.concatenate([red01[0, 0:3], red_mix[0, 0:2], red_op[2:3], red02[0, 0:3]], axis=0),
                      jnp.concatenate([red11[0, 0:3], red_pn[0:2], red_pool[2:3], red12[0, 0:3]], axis=0)])
    dm_c = jnp.concatenate([red01[1, 0:3], red_mix[1, 0:2], jnp.zeros((4, D), F32)], axis=0)
    d_lg = dlg[:, 0:2, 0].T
    d_dec_f = d_lg[0:1] * jax.nn.sigmoid(-ret_decay_fwd)
    d_dec_b = d_lg[1:2] * jax.nn.sigmoid(-ret_decay_bwd)
    pieces = [dm_x.reshape(-1), dm_c.reshape(-1),
              jnp.stack([red01[0, 3] + red01[1, 3], red11[0, 3]]).reshape(-1),
              jnp.stack([red_mix[0, 3] + red_mix[1, 3], red_pn[3]]).reshape(-1),
              jnp.stack([red02[0, 3], red12[0, 3]]).reshape(-1),
              red_fn[0], d_dec_f.reshape(-1), d_dec_b.reshape(-1), g_conv.reshape(-1), red_pool[4]]
    sizes = [int(a.shape[0]) for a in pieces]
    n_pack = sum(sizes)
    n_pad = -n_pack % 8192
    packed = jnp.concatenate(pieces + [jnp.zeros((n_pad,), F32)])[None, :]
    (packed_all,) = all_gather([packed], name="gather_partials")
    total = sum_partials(packed_all, name="sum_partials")[0]
    offs = [0]
    for s in sizes:
        offs.append(offs[-1] + s)
    seg = [total[offs[i]:offs[i + 1]] for i in range(len(sizes))]
    g_dm = seg[0].reshape(2, N_MOD * D)
    g_dmc = seg[1].reshape(N_MOD * D)
    g_b_mod = g_dm.at[0].add(g_dmc)
    g_norm_ffn1, g_norm_mix, g_norm_ffn2 = (seg[k].reshape(2, D) for k in (2, 3, 4))
    g_final = seg[5]
    g_dec_f, g_dec_b = seg[6].reshape(1, heads), seg[7].reshape(1, heads)
    g_conv_all = seg[8].reshape(CONV_WIDTH, -1)
    g_pscale_all = seg[9]
    conv_w = mix_w_conv.shape[2]
    g_w_conv = lax.dynamic_slice_in_dim(g_conv_all, me * conv_w, conv_w, axis=1)[None]
    ps_w = pool_scale.shape[1]
    g_pool_scale = lax.dynamic_slice_in_dim(g_pscale_all, me * ps_w, ps_w, axis=0)[None]

    dm_rows = packed_all[:, 0, 0:offs[1]].reshape(N_DEV, 2, N_MOD * D)
    dmc_rows = packed_all[:, 0, offs[1]:offs[2]].reshape(N_DEV, N_MOD * D)
    dm_x_blk = jnp.transpose(lax.dynamic_slice_in_dim(dm_rows, me * mod_w, mod_w, axis=2), (1, 0, 2))
    dm_c_blk = jnp.stack([lax.dynamic_slice_in_dim(dmc_rows, me * mod_w, mod_w, axis=1),
                          jnp.zeros((N_DEV, mod_w), F32)])
    g_w_mod, d_w_mod, nm_w_mod, nv_w_mod, cctx_part = adaln_bwd(cond16, dm_x_blk, dm_c_blk, w_mod, m_w_mod, v_w_mod,
                                                                name="adaln_bwd")
    cpad = jnp.concatenate([cctx_part[0], jnp.zeros((8192 - D,), F32)])[None, :] if D < 8192 else cctx_part[0:1]
    (cctx_all,) = all_gather([cpad], name="gather_cctx")
    g_c_ctx = sum_partials(cctx_all, name="sum_cctx")[0, :D]

    small = [("c_ctx", g_c_ctx, c_ctx, m_c_ctx, v_c_ctx), ("b_mod", g_b_mod, b_mod, m_b_mod, v_b_mod),
             ("norm_ffn1", g_norm_ffn1, norm_ffn1, m_norm_ffn1, v_norm_ffn1),
             ("norm_mix", g_norm_mix, norm_mix, m_norm_mix, v_norm_mix),
             ("norm_ffn2", g_norm_ffn2, norm_ffn2, m_norm_ffn2, v_norm_ffn2),
             ("mix_w_conv", g_w_conv, mix_w_conv, m_mix_w_conv, v_mix_w_conv),
             ("ret_decay_fwd", g_dec_f, ret_decay_fwd, m_ret_decay_fwd, v_ret_decay_fwd),
             ("ret_decay_bwd", g_dec_b, ret_decay_bwd, m_ret_decay_bwd, v_ret_decay_bwd),
             ("pool_scale", g_pool_scale, pool_scale, m_pool_scale, v_pool_scale),
             ("final_norm", g_final, final_norm, m_final_norm, v_final_norm)]
    ssz = [int(a[1].size) for a in small]
    spad = -sum(ssz) % LANE

    def pack(k):
        return jnp.concatenate([a[k].reshape(-1) for a in small] + [jnp.ones((spad,), F32)])[None, :]

    sd, sm, sv = adam_plain(pack(1), pack(2), pack(3), pack(4), name="adam_small")
    soff = [0]
    for s in ssz:
        soff.append(soff[-1] + s)
    for i, (nm, g, w, _, _) in enumerate(small):
        res[nm] = [g.reshape(w.shape)] + [a[0, soff[i]:soff[i + 1]].reshape(w.shape) for a in (sd, sm, sv)]
    res["w_mod"] = [g_w_mod, d_w_mod, nm_w_mod, nv_w_mod]

    order = ["c_ctx", "w_mod", "b_mod", "norm_ffn1", "norm_mix", "norm_ffn2", "ffn1_w_gate", "ffn1_w_up", "ffn1_w_down",
             "ffn2_w_gate", "ffn2_w_up", "ffn2_w_down", "mix_w_in", "mix_w_conv", "mix_w_out", "ret_decay_fwd",
             "ret_decay_bwd", "pool_w", "pool_scale", "final_norm"]
    return (loss, grad_x, *[res[n][0] for n in order], *[res[n][1] for n in order],
            *[res[n][2] for n in order], *[res[n][3] for n in order])
```

```python
import jax
import jax.numpy as jnp
from jax import lax
from jax.experimental import pallas as pl
from jax.experimental.pallas import tpu as pltpu

F32 = jnp.float32
BF16 = jnp.bfloat16
MESH = pl.DeviceIdType.MESH

N_DEV = 8
N_MOD = 9
EPS = 1e-6
GN_EPS = 1e-5
MACARON = 0.5
HEAD_DIM = 128
K_SCALE = HEAD_DIM ** -0.5
ROPE_BASE = 10000.0
GRID_W = 64
CONV_WIDTH = 3
POOL_WINDOWS = (2, 4, 8, 16)
POOL_PAD = 16

ADAM_LR = 0.001
ADAM_B1 = 0.9
ADAM_B2 = 0.999
ADAM_EPS = 1e-08
ADAM_WD = 0.01
ADAM_STEP = 10

LANE = 128
ROW_CHUNK = 128
VMEM_LIMIT = 56 * 1024 * 1024
ANY = pl.BlockSpec(memory_space=pl.ANY)


def _params(sem=None):
    kw = dict(vmem_limit_bytes=VMEM_LIMIT)
    if sem is not None:
        kw["dimension_semantics"] = sem
    return pltpu.CompilerParams(**kw)


def _nt(a, b):
    return lax.dot_general(a, b, (((1,), (1,)), ((), ())), preferred_element_type=F32)


def _tn(a, b):
    return lax.dot_general(a, b, (((0,), (0,)), ((), ())), preferred_element_type=F32)


def _nn(a, b):
    return jnp.dot(a, b, preferred_element_type=F32)


def _silu_parts(a):
    sig = jax.nn.sigmoid(a)
    return sig, a * sig


def _dsilu(a, sig):
    return sig * (1.0 + a * (1.0 - sig))


def _norm_mod(x, g, shift, scale):
    r = lax.rsqrt(jnp.mean(x * x, axis=-1, keepdims=True) + EPS)
    return (x * r * g) * (1.0 + scale) + shift


def _norm_mod_bwd(x, dh, g, scale):
    r = lax.rsqrt(jnp.mean(x * x, axis=-1, keepdims=True) + EPS)
    xhat = x * r
    dn = dh * (1.0 + scale)
    dshift = jnp.sum(dh, axis=0, keepdims=True)
    dscale = jnp.sum(dh * (xhat * g), axis=0, keepdims=True)
    dg = jnp.sum(dn * xhat, axis=0, keepdims=True)
    dxh = dn * g
    dx = r * (dxh - xhat * jnp.mean(dxh * xhat, axis=-1, keepdims=True))
    return dx, dshift, dscale, dg


def _group_index(i, tiles_first, n_groups):
    if n_groups == 1:
        return 0
    return jnp.where(i >= tiles_first, 1, 0)


def _mesh_pos():
    x, y, c = lax.axis_index("x"), lax.axis_index("y"), lax.axis_index("c")
    chips = [(1 - x, y), (x, 1 - y), (1 - x, 1 - y)]
    return x, y, c, chips


def _flat(px, py, pc):
    return 4 * px + 2 * py + pc


def all_gather(shards, name):
    n = len(shards)

    def body(*refs):
        ins, outs = refs[:n], refs[n:2 * n]
        send_sems, recv_sems, local_sems = refs[2 * n:]
        x, y, c, chips = _mesh_pos()
        me, sibling = (x, y, c), (x, y, 1 - c)

        def copy(t, k, block, to, src=None):
            dst = outs[t].at[_flat(*block)]
            return pltpu.make_async_remote_copy(
                src_ref=dst if src is None else src, dst_ref=dst,
                send_sem=send_sems.at[t, k], recv_sem=recv_sems.at[t, k],
                device_id=to, device_id_type=MESH)

        mine = [pltpu.make_async_copy(ins[t], outs[t].at[_flat(*me)], local_sems.at[t]) for t in range(n)]
        for cp in mine:
            cp.start()
        first = []
        for t in range(n):
            first.append(copy(t, 0, me, sibling, src=ins[t]))
            first += [copy(t, 1 + j, me, (*chip, c), src=ins[t]) for j, chip in enumerate(chips)]
        for cp in first:
            cp.start()
        passed = []
        for t in range(n):
            for j, chip in enumerate(chips):
                copy(t, 1 + j, (*chip, c), me).wait_recv()
                fwd = copy(t, 4 + j, (*chip, c), sibling)
                fwd.start()
                passed.append(fwd)
        for t in range(n):
            copy(t, 0, sibling, me).wait_recv()
            for j, chip in enumerate(chips):
                copy(t, 4 + j, (*chip, 1 - c), me).wait_recv()
        for cp in first + passed:
            cp.wait_send()
        for cp in mine:
            cp.wait()

    return pl.pallas_call(
        body, name=name,
        out_shape=[jax.ShapeDtypeStruct((N_DEV,) + s.shape, s.dtype) for s in shards],
        in_specs=[ANY] * n, out_specs=[ANY] * n,
        scratch_shapes=[pltpu.SemaphoreType.DMA((n, 7)), pltpu.SemaphoreType.DMA((n, 7)), pltpu.SemaphoreType.DMA((n,))],
    )(*shards)


HBM_SPEC = pl.BlockSpec(memory_space=pltpu.HBM)
SEM_SPEC = pl.BlockSpec(memory_space=pltpu.SEMAPHORE)
DATAFLOW = pltpu.SideEffectType.DATAFLOW_SIDE_EFFECTING


def _in_hbm(a):
    return pltpu.with_memory_space_constraint(a, pltpu.HBM)


def _push_peers():
    x, y, c, chips = _mesh_pos()
    return [(*chip, c) for chip in chips] + [(x, y, 1 - c)]


def gather_start(shards, after, name):
    n = len(shards)
    lands = [lax.empty((N_DEV,) + s.shape, s.dtype) for s in shards]

    def body(*refs):
        ins, lz = refs[:n], refs[n:2 * n]
        send_sems, recv_sems = refs[2 * n + 1], refs[2 * n + 2]
        token = refs[-1]
        x, y, c, _ = _mesh_pos()
        for t in range(n):
            for k, peer in enumerate(_push_peers()):
                pltpu.make_async_remote_copy(
                    src_ref=ins[t], dst_ref=lz[t].at[_flat(x, y, c)], send_sem=send_sems.at[4 * t + k],
                    recv_sem=recv_sems.at[4 * t + k], device_id=peer, device_id_type=MESH).start()
        token[...] = jnp.zeros_like(token)

    return pl.pallas_call(
        body, name=name,
        out_shape=(pltpu.SemaphoreType.DMA((4 * n,)), pltpu.SemaphoreType.DMA((4 * n,)),
                   *[pltpu.HBM(s.shape, s.dtype) for s in shards], *[pltpu.HBM(l.shape, l.dtype) for l in lands],
                   jax.ShapeDtypeStruct((8, LANE), F32)),
        in_specs=[HBM_SPEC] * (2 * n) + [ANY],
        out_specs=(SEM_SPEC, SEM_SPEC, *[HBM_SPEC] * (2 * n), pl.BlockSpec(memory_space=pltpu.VMEM)),
        input_output_aliases={i: 2 + i for i in range(2 * n)},
        compiler_params=pltpu.CompilerParams(has_side_effects=DATAFLOW),
    )(*[_in_hbm(s) for s in shards], *[_in_hbm(l) for l in lands], after)


def gather_wait(handle, after, name):
    send, recv = handle[0], handle[1]
    n = (len(handle) - 3) // 2
    bufs = handle[2:2 + 2 * n]

    def body(*refs):
        ins, lz = refs[:n], refs[n:2 * n]
        send_sems, recv_sems = refs[2 * n], refs[2 * n + 1]
        for t in range(n):
            for k, peer in enumerate(_push_peers()):
                cp = pltpu.make_async_remote_copy(
                    src_ref=ins[t], dst_ref=lz[t].at[_flat(*peer)], send_sem=send_sems.at[4 * t + k],
                    recv_sem=recv_sems.at[4 * t + k], device_id=peer, device_id_type=MESH)
                cp.wait_send()
                cp.wait_recv()

    outs = pl.pallas_call(
        body, name=name,
        out_shape=tuple(pltpu.HBM(b.shape, b.dtype) for b in bufs),
        in_specs=[HBM_SPEC] * (2 * n) + [SEM_SPEC, SEM_SPEC, ANY],
        out_specs=tuple([HBM_SPEC] * (2 * n)),
        input_output_aliases={i: i for i in range(2 * n)},
        compiler_params=pltpu.CompilerParams(has_side_effects=DATAFLOW),
    )(*bufs, send, recv, after)
    return list(outs[:n]), list(outs[n:])


def gather_finish(shards, lands, name):
    n = len(shards)

    def body(*refs):
        ins, outs = refs[:n], refs[2 * n:3 * n]
        send_sems, recv_sems, local_sems = refs[3 * n:]
        x, y, c, chips = _mesh_pos()
        sibling = (x, y, 1 - c)

        def copy(t, j, core):
            blk = outs[t].at[_flat(*chips[j], core)]
            return pltpu.make_async_remote_copy(
                src_ref=blk, dst_ref=blk, send_sem=send_sems.at[t, j], recv_sem=recv_sems.at[t, j],
                device_id=sibling, device_id_type=MESH)

        mine = [pltpu.make_async_copy(ins[t], outs[t].at[_flat(x, y, c)], local_sems.at[t]) for t in range(n)]
        sends = [copy(t, j, c) for t in range(n) for j in range(3)]
        for cp in mine + sends:
            cp.start()
        for t in range(n):
            for j in range(3):
                copy(t, j, 1 - c).wait_recv()
        for cp in sends:
            cp.wait_send()
        for cp in mine:
            cp.wait()

    return pl.pallas_call(
        body, name=name,
        out_shape=[jax.ShapeDtypeStruct(l.shape, l.dtype) for l in lands],
        in_specs=[ANY] * (2 * n), out_specs=[ANY] * n,
        input_output_aliases={n + t: t for t in range(n)},
        scratch_shapes=[pltpu.SemaphoreType.DMA((n, 3)), pltpu.SemaphoreType.DMA((n, 3)), pltpu.SemaphoreType.DMA((n,))],
    )(*shards, *lands)


def chip_exchange_start(sums, name):
    n = len(sums)
    lands = [lax.empty(s.shape, s.dtype) for s in sums]

    def body(*refs):
        ins, lz = refs[:n], refs[n:2 * n]
        send_sems, recv_sems = refs[2 * n], refs[2 * n + 1]
        token = refs[-1]
        peers = _push_peers()
        for t in range(n):
            for j in range(3):
                pltpu.make_async_remote_copy(
                    src_ref=ins[t].at[j], dst_ref=lz[t].at[j], send_sem=send_sems.at[3 * t + j],
                    recv_sem=recv_sems.at[3 * t + j], device_id=peers[j], device_id_type=MESH).start()
        token[...] = jnp.zeros_like(token)

    return pl.pallas_call(
        body, name=name,
        out_shape=(pltpu.SemaphoreType.DMA((3 * n,)), pltpu.SemaphoreType.DMA((3 * n,)),
                   *[pltpu.HBM(s.shape, s.dtype) for s in sums], *[pltpu.HBM(s.shape, s.dtype) for s in sums],
                   jax.ShapeDtypeStruct((8, LANE), F32)),
        in_specs=[HBM_SPEC] * (2 * n),
        out_specs=(SEM_SPEC, SEM_SPEC, *[HBM_SPEC] * (2 * n), pl.BlockSpec(memory_space=pltpu.VMEM)),
        input_output_aliases={i: 2 + i for i in range(2 * n)},
        compiler_params=pltpu.CompilerParams(has_side_effects=DATAFLOW),
    )(*[_in_hbm(s) for s in sums], *[_in_hbm(l) for l in lands])


def chip_exchange_wait(handle, after, name):
    send, recv = handle[0], handle[1]
    n = (len(handle) - 3) // 2
    bufs = handle[2:2 + 2 * n]

    def body(*refs):
        ins, lz = refs[:n], refs[n:2 * n]
        send_sems, recv_sems = refs[2 * n], refs[2 * n + 1]
        peers = _push_peers()
        for t in range(n):
            for j in range(3):
                cp = pltpu.make_async_remote_copy(
                    src_ref=ins[t].at[j], dst_ref=lz[t].at[j], send_sem=send_sems.at[3 * t + j],
                    recv_sem=recv_sems.at[3 * t + j], device_id=peers[j], device_id_type=MESH)
                cp.wait_send()
                cp.wait_recv()

    outs = pl.pallas_call(
        body, name=name,
        out_shape=tuple(pltpu.HBM(b.shape, b.dtype) for b in bufs),
        in_specs=[HBM_SPEC] * (2 * n) + [SEM_SPEC, SEM_SPEC, ANY],
        out_specs=tuple([HBM_SPEC] * (2 * n)),
        input_output_aliases={i: i for i in range(2 * n)},
        compiler_params=pltpu.CompilerParams(has_side_effects=DATAFLOW),
    )(*bufs, send, recv, after)
    return list(outs[n:])


def rs_sibling_exchange(grads, after, name):
    n = len(grads)

    def body(*refs):
        ins, stages, directs = refs[:n], refs[n + 1:2 * n + 1], refs[2 * n + 1:3 * n + 1]
        send_sems, recv_sems = refs[3 * n + 1:]
        x, y, c, chips = _mesh_pos()
        sibling = (x, y, 1 - c)

        def copy(t, k):
            if k < 3:
                src, dst = ins[t].at[_flat(*chips[k], 1 - c)], stages[t].at[k]
            else:
                src, dst = ins[t].at[_flat(x, y, 1 - c)], directs[t]
            return pltpu.make_async_remote_copy(
                src_ref=src, dst_ref=dst, send_sem=send_sems.at[t, k], recv_sem=recv_sems.at[t, k],
                device_id=sibling, device_id_type=MESH)

        cps = [copy(t, k) for t in range(n) for k in range(4)]
        for cp in cps:
            cp.start()
        for cp in cps:
            cp.wait_recv()
        for cp in cps:
            cp.wait_send()

    return pl.pallas_call(
        body, name=name,
        out_shape=[jax.ShapeDtypeStruct((3,) + g.shape[1:], g.dtype) for g in grads]
        + [jax.ShapeDtypeStruct(g.shape[1:], g.dtype) for g in grads],
        in_specs=[ANY] * (n + 1), out_specs=[ANY] * (2 * n),
        scratch_shapes=[pltpu.SemaphoreType.DMA((n, 4)), pltpu.SemaphoreType.DMA((n, 4))],
    )(*grads, after)


def _row_tile(rows, cols, limit_bytes=3 << 19):
    best = None
    for t in range(16, rows + 1, 16):
        if rows % t == 0 and t * cols * 4 <= limit_bytes:
            best = t
    return best if best is not None else rows


def chip_sum(grad, stage, peer_blocks, name):
    _, R, C = grad.shape
    tr = _row_tile(R, C)

    def body(idx_ref, g_ref, s_ref, o_ref):
        o_ref[...] = (g_ref[...].astype(F32) + s_ref[...].astype(F32)).astype(BF16)

    return pl.pallas_call(
        body, name=name,
        grid_spec=pltpu.PrefetchScalarGridSpec(
            num_scalar_prefetch=1, grid=(3, R // tr),
            in_specs=[pl.BlockSpec((None, tr, C), lambda j, i, idx: (idx[j], i, 0)),
                      pl.BlockSpec((None, tr, C), lambda j, i, idx: (j, i, 0))],
            out_specs=pl.BlockSpec((None, tr, C), lambda j, i, idx: (j, i, 0))),
        out_shape=jax.ShapeDtypeStruct((3, R, C), BF16),
        compiler_params=_params(("arbitrary", "arbitrary")),
    )(peer_blocks, grad, stage)


def _adamw(w, g, m, v):
    m2 = ADAM_B1 * m + (1.0 - ADAM_B1) * g
    v2 = ADAM_B2 * v + (1.0 - ADAM_B2) * (g * g)
    m_hat = m2 / (1.0 - ADAM_B1 ** ADAM_STEP)
    v_hat = v2 / (1.0 - ADAM_B2 ** ADAM_STEP)
    delta = -ADAM_LR * (m_hat / (jnp.sqrt(v_hat) + ADAM_EPS) + ADAM_WD * w)
    return delta, m2, v2


def adam_reduced(own_block, grad, direct, landed, w, m, v, layer, prev, name):
    L, R, C = w.shape
    tr = _row_tile(R, C, 1 << 20)
    first = prev is None

    def body(idx_ref, g_ref, d_ref, l_ref, w_ref, m_ref, v_ref, *rest):
        og, od, om, ov = rest[-4:]
        g = g_ref[...].astype(F32) + d_ref[...].astype(F32)
        for j in range(3):
            g = g + l_ref[j].astype(F32)
        delta, m2, v2 = _adamw(w_ref[...], g, m_ref[...], v_ref[...])
        og[...] = g
        od[...] = delta
        om[...] = m2
        ov[...] = v2

    lay = pl.BlockSpec((None, tr, C), lambda i, idx: (layer, i, 0))
    in_specs = [pl.BlockSpec((None, tr, C), lambda i, idx: (idx[0], i, 0)),
                pl.BlockSpec((tr, C), lambda i, idx: (i, 0)),
                pl.BlockSpec((3, tr, C), lambda i, idx: (0, i, 0)),
                lay, lay, lay]
    args = [own_block, grad, direct, landed, w, m, v]
    aliases = {}
    if not first:
        in_specs += [ANY] * 4
        args += list(prev)
        aliases = {7 + k: k for k in range(4)}
    return pl.pallas_call(
        body, name=name,
        grid_spec=pltpu.PrefetchScalarGridSpec(
            num_scalar_prefetch=1, grid=(R // tr,), in_specs=in_specs, out_specs=[lay] * 4),
        out_shape=[jax.ShapeDtypeStruct((L, R, C), F32)] * 4,
        input_output_aliases=aliases,
        compiler_params=_params(("arbitrary",)),
    )(*args)


def sum_partials(parts, name):
    _, _, N = parts.shape
    tn = 8192

    def body(p_ref, o_ref):
        g = p_ref[0]
        for k in range(1, N_DEV):
            g = g + p_ref[k]
        o_ref[...] = g

    return pl.pallas_call(
        body, name=name, grid=(N // tn,),
        in_specs=[pl.BlockSpec((N_DEV, 1, tn), lambda i: (0, 0, i))],
        out_specs=pl.BlockSpec((1, tn), lambda i: (0, i)),
        out_shape=jax.ShapeDtypeStruct((1, N), F32),
    )(parts)


def adam_plain(g, w, m, v, name):
    _, N = g.shape

    def body(g_ref, w_ref, m_ref, v_ref, od, om, ov):
        delta, m2, v2 = _adamw(w_ref[...], g_ref[...], m_ref[...], v_ref[...])
        od[...] = delta
        om[...] = m2
        ov[...] = v2

    return pl.pallas_call(
        body, name=name, out_shape=[jax.ShapeDtypeStruct((1, N), F32)] * 3,
    )(g, w, m, v)


def adaln_fwd(cond16, w_mod, b_blk, name):
    L, D, W = w_mod.shape
    tn = 768 if W % 768 == 0 else W

    def body(c_ref, w_ref, b_ref, o_ref):
        c = c_ref[...]
        sc = (c * jax.nn.sigmoid(c)).astype(BF16)
        o_ref[...] = _nn(sc, w_ref[...].astype(BF16)) + b_ref[...]

    return pl.pallas_call(
        body, name=name, grid=(L, W // tn),
        in_specs=[pl.BlockSpec((16, D), lambda l, i: (0, 0)),
                  pl.BlockSpec((None, D, tn), lambda l, i: (l, 0, i)),
                  pl.BlockSpec((None, 1, tn), lambda l, i: (l, 0, i))],
        out_specs=pl.BlockSpec((None, 16, tn), lambda l, i: (l, 0, i)),
        out_shape=jax.ShapeDtypeStruct((L, 16, W), F32),
        compiler_params=_params(("arbitrary", "arbitrary")),
    )(cond16, w_mod, b_blk)


def adaln_bwd(cond16, dm_x, dm_c, w_mod, m_mod, v_mod, name):
    L, D, W = w_mod.shape
    tn = 256 if W % 256 == 0 else W
    nt = W // tn

    def body(c_ref, cT_ref, dx_ref, dc_ref, w_ref, m_ref, v_ref, og, od, om, ov, pc_ref):
        l, i = pl.program_id(0), pl.program_id(1)
        c = c_ref[...]
        sig, sl = _silu_parts(c)
        cT = cT_ref[...]
        sigT = jax.nn.sigmoid(cT)
        scT = (cT * sigT).astype(BF16)
        dmc = jnp.sum(dc_ref[...], axis=0, keepdims=True)
        dm16 = jnp.concatenate([dx_ref[...], jnp.broadcast_to(dmc, (8, tn))], axis=0)
        row = lax.broadcasted_iota(jnp.int32, (16, tn), 0)
        dm16 = jnp.where(row <= 8, dm16, 0.0).astype(BF16)
        w = w_ref[...]
        g = _nn(scT, dm16)
        delta, m2, v2 = _adamw(w, g, m_ref[...], v_ref[...])
        og[...] = g
        od[...] = delta
        om[...] = m2
        ov[...] = v2

        @pl.when((l == 0) & (i == 0))
        def _():
            pc_ref[...] = jnp.zeros_like(pc_ref)

        @pl.when(l == 0)
        def _():
            back = _nt(jnp.broadcast_to(dmc, (8, tn)).astype(BF16), w.astype(BF16))
            pc_ref[...] += back * _dsilu(c[8:9, :], sig[8:9, :])

    col = pl.BlockSpec((None, D, tn), lambda l, i: (l, 0, i))
    row8 = pl.BlockSpec((None, 8, tn), lambda l, i: (l, 0, i))
    return pl.pallas_call(
        body, name=name, grid=(L, nt),
        in_specs=[pl.BlockSpec((16, D), lambda l, i: (0, 0)), pl.BlockSpec((D, 16), lambda l, i: (0, 0)),
                  row8, row8, col, col, col],
        out_specs=[col, col, col, col, pl.BlockSpec((8, D), lambda l, i: (0, 0))],
        out_shape=[jax.ShapeDtypeStruct((L, D, W), F32)] * 4 + [jax.ShapeDtypeStruct((8, D), F32)],
        compiler_params=_params(("arbitrary", "arbitrary")),
    )(cond16, cond16.T, dm_x, dm_c, w_mod, m_mod, v_mod)


def _token_spec(tm, D):
    if tm > 256:
        return pl.BlockSpec((tm, D), lambda i, j: (i, 0), pipeline_mode=pl.Buffered(1))
    return pl.BlockSpec((tm, D), lambda i, j: (i, 0))


def ffn_fwd(x, mod, gn, wg, wu, wd, *, tm, t_first, name):
    T, D = x.shape
    nb, cw, _ = wg.shape
    G = mod.shape[0]
    nt = T // tm
    first = t_first // tm

    def body(x_ref, mod_ref, g_ref, wg_ref, wu_ref, wd_ref, xo_ref, h_ref, a_ref, b_ref, y_ref, hs, acc, a_s, b_s, s_s):
        j = pl.program_id(1)

        @pl.when(j == 0)
        def _():
            def head(r, carry):
                rows = pl.ds(pl.multiple_of(r * ROW_CHUNK, ROW_CHUNK), ROW_CHUNK)
                hb = _norm_mod(x_ref[rows, :], g_ref[...], mod_ref[0:1, :], mod_ref[1:2, :]).astype(BF16)
                hs[rows, :] = hb
                h_ref[rows, :] = hb
                return carry

            lax.fori_loop(0, tm // ROW_CHUNK, head, 0)
            acc[...] = jnp.zeros_like(acc)

        a_s[...] = _nt(hs[...], wg_ref[...])
        b_s[...] = _nt(hs[...], wu_ref[...])

        def chunk(r, carry):
            rows = pl.ds(pl.multiple_of(r * ROW_CHUNK, ROW_CHUNK), ROW_CHUNK)
            av, bv = a_s[rows, :], b_s[rows, :]
            a_ref[rows, :] = av.astype(BF16)
            b_ref[rows, :] = bv.astype(BF16)
            _, sl = _silu_parts(av)
            s_s[rows, :] = (sl * bv).astype(BF16)
            return carry

        lax.fori_loop(0, tm // ROW_CHUNK, chunk, 0)
        acc[...] += _nn(s_s[...], wd_ref[...])

        @pl.when(j == nb - 1)
        def _():
            def tail(r, carry):
                rows = pl.ds(pl.multiple_of(r * ROW_CHUNK, ROW_CHUNK), ROW_CHUNK)
                y = acc[rows, :]
                y_ref[rows, :] = y.astype(BF16)
                xo_ref[rows, :] = x_ref[rows, :] + (MACARON * mod_ref[2:3, :]) * y
                return carry

            lax.fori_loop(0, tm // ROW_CHUNK, tail, 0)

    tok = _token_spec(tm, D)
    act = pl.BlockSpec((None, tm, cw), lambda i, j: (j, i, 0))
    wblk = pl.BlockSpec((None, cw, D), lambda i, j: (j, 0, 0))
    return pl.pallas_call(
        body, name=name, grid=(nt, nb),
        in_specs=[tok, pl.BlockSpec((None, 3, D), lambda i, j: (_group_index(i, first, G), 0, 0)),
                  pl.BlockSpec((1, D), lambda i, j: (0, 0)), wblk, wblk, wblk],
        out_specs=[tok, tok, act, act, tok],
        out_shape=[jax.ShapeDtypeStruct((T, D), F32), jax.ShapeDtypeStruct((T, D), BF16),
                   jax.ShapeDtypeStruct((nb, T, cw), BF16), jax.ShapeDtypeStruct((nb, T, cw), BF16),
                   jax.ShapeDtypeStruct((T, D), BF16)],
        scratch_shapes=[pltpu.VMEM((tm, D), BF16), pltpu.VMEM((tm, D), F32), pltpu.VMEM((tm, cw), F32),
                        pltpu.VMEM((tm, cw), F32), pltpu.VMEM((tm, cw), BF16)],
        compiler_params=_params(("arbitrary", "arbitrary")),
    )(x, mod, gn, wg, wu, wd)


def ffn_bwd_tok(x, dxo, y, a, b, mod, gn, wg, wu, wd, *, tm, t_first, name):
    T, D = x.shape
    nb, cw, _ = wg.shape
    G = mod.shape[0]
    nt = T // tm
    first = t_first // tm

    def body(x_ref, dxo_ref, y_ref, a_ref, b_ref, mod_ref, g_ref, wg_ref, wu_ref, wd_ref,
             dx_ref, da_ref, db_ref, s_ref, dy_ref, red_ref, dys, dh, ds_s):
        i, j = pl.program_id(0), pl.program_id(1)

        @pl.when(j == 0)
        def _():
            def head(r, carry):
                rows = pl.ds(pl.multiple_of(r * ROW_CHUNK, ROW_CHUNK), ROW_CHUNK)
                dyb = ((MACARON * mod_ref[2:3, :]) * dxo_ref[rows, :]).astype(BF16)
                dys[rows, :] = dyb
                dy_ref[rows, :] = dyb
                return carry

            lax.fori_loop(0, tm // ROW_CHUNK, head, 0)
            dh[...] = jnp.zeros_like(dh)

        ds_s[...] = _nt(dys[...], wd_ref[...])

        def chunk(r, carry):
            rows = pl.ds(pl.multiple_of(r * ROW_CHUNK, ROW_CHUNK), ROW_CHUNK)
            av = a_ref[rows, :].astype(F32)
            bv = b_ref[rows, :].astype(F32)
            ds = ds_s[rows, :]
            sig, sl = _silu_parts(av)
            s_ref[rows, :] = (sl * bv).astype(BF16)
            da_ref[rows, :] = (ds * bv * _dsilu(av, sig)).astype(BF16)
            db_ref[rows, :] = (ds * sl).astype(BF16)
            return carry

        lax.fori_loop(0, tm // ROW_CHUNK, chunk, 0)
        dh[...] += _nn(da_ref[...], wg_ref[...]) + _nn(db_ref[...], wu_ref[...])

        @pl.when((j == 0) & ((i == 0) | (i == first)))
        def _():
            red_ref[...] = jnp.zeros_like(red_ref)

        @pl.when(j == nb - 1)
        def _():
            def tail(r, carry):
                rows = pl.ds(pl.multiple_of(r * ROW_CHUNK, ROW_CHUNK), ROW_CHUNK)
                dxo_v = dxo_ref[rows, :]
                dxn, dshift, dscale, dg = _norm_mod_bwd(x_ref[rows, :], dh[rows, :], g_ref[...], mod_ref[1:2, :])
                dx_ref[rows, :] = dxo_v + dxn
                red_ref[0:1, :] += dshift
                red_ref[1:2, :] += dscale
                red_ref[2:3, :] += jnp.sum((MACARON * dxo_v) * y_ref[rows, :].astype(F32), axis=0, keepdims=True)
                red_ref[3:4, :] += dg
                return carry

            lax.fori_loop(0, tm // ROW_CHUNK, tail, 0)

    tok = _token_spec(tm, D)
    act = pl.BlockSpec((None, tm, cw), lambda i, j: (j, i, 0))
    wblk = pl.BlockSpec((None, cw, D), lambda i, j: (j, 0, 0))
    return pl.pallas_call(
        body, name=name, grid=(nt, nb),
        in_specs=[tok, tok, tok, act, act,
                  pl.BlockSpec((None, 3, D), lambda i, j: (_group_index(i, first, G), 0, 0)),
                  pl.BlockSpec((1, D), lambda i, j: (0, 0)), wblk, wblk, wblk],
        out_specs=[tok, act, act, act, tok,
                   pl.BlockSpec((None, 8, D), lambda i, j: (_group_index(i, first, G), 0, 0))],
        out_shape=[jax.ShapeDtypeStruct((T, D), F32)] + [jax.ShapeDtypeStruct((nb, T, cw), BF16)] * 3
        + [jax.ShapeDtypeStruct((T, D), BF16), jax.ShapeDtypeStruct((G, 8, D), F32)],
        scratch_shapes=[pltpu.VMEM((tm, D), BF16), pltpu.VMEM((tm, D), F32), pltpu.VMEM((tm, cw), F32)],
        compiler_params=_params(("arbitrary", "arbitrary")),
    )(x, dxo, y, a, b, mod, gn, wg, wu, wd)


def matmul_tn(a, b, *, nb, m, n, a_blocked, b_blocked, tk, name, after=None):
    T = a.shape[-2]
    extra = [] if after is None else [after]

    def spec(arr, blocked, width):
        if blocked:
            return pl.BlockSpec((None, tk, width), lambda j, k: (j, k, 0))
        if arr.shape[-1] == width:
            return pl.BlockSpec((tk, width), lambda j, k: (k, 0))
        return pl.BlockSpec((tk, width), lambda j, k: (k, j))

    def body(a_ref, b_ref, *rest):
        o_ref, acc = rest[-2:]
        k = pl.program_id(1)

        @pl.when(k == 0)
        def _():
            acc[...] = jnp.zeros_like(acc)

        acc[...] += _tn(a_ref[...], b_ref[...])

        @pl.when(k == T // tk - 1)
        def _():
            o_ref[...] = acc[...].astype(BF16)

    return pl.pallas_call(
        body, name=name, grid=(nb, T // tk),
        in_specs=[spec(a, a_blocked, m), spec(b, b_blocked, n)] + [ANY] * len(extra),
        out_specs=pl.BlockSpec((None, m, n), lambda j, k: (j, 0, 0)),
        out_shape=jax.ShapeDtypeStruct((nb, m, n), BF16),
        scratch_shapes=[pltpu.VMEM((m, n), F32)],
        compiler_params=_params(("arbitrary", "arbitrary")),
    )(a, b, *extra)


def proj_in(x, mod, gn, w_in, *, tm, t_first, name):
    T, D = x.shape
    nb, _, cw = w_in.shape
    G = mod.shape[0]
    nt = T // tm
    first = t_first // tm

    def body(x_ref, mod_ref, g_ref, w_ref, p_ref, h_ref, hs):
        @pl.when(pl.program_id(1) == 0)
        def _():
            hb = _norm_mod(x_ref[...], g_ref[...], mod_ref[0:1, :], mod_ref[1:2, :]).astype(BF16)
            hs[...] = hb
            h_ref[...] = hb

        p_ref[...] = _nn(hs[...], w_ref[...])

    tok = pl.BlockSpec((tm, D), lambda i, j: (i, 0))
    return pl.pallas_call(
        body, name=name, grid=(nt, nb),
        in_specs=[tok, pl.BlockSpec((None, 3, D), lambda i, j: (_group_index(i, first, G), 0, 0)),
                  pl.BlockSpec((1, D), lambda i, j: (0, 0)),
                  pl.BlockSpec((None, D, cw), lambda i, j: (j, 0, 0))],
        out_specs=[pl.BlockSpec((tm, cw), lambda i, j: (i, j)), tok],
        out_shape=[jax.ShapeDtypeStruct((T, nb * cw), F32), jax.ShapeDtypeStruct((T, D), BF16)],
        scratch_shapes=[pltpu.VMEM((tm, D), BF16)],
        compiler_params=_params(("arbitrary", "arbitrary")),
    )(x, mod, gn, w_in)


def proj_bwd_tok(x, dxo, dp, mod, gn, w_in, *, tm, name):
    T, D = x.shape
    nb, _, cw = w_in.shape
    G = mod.shape[0]
    nt = T // tm
    first = dxo.shape[0] // tm

    def body(x_ref, dxo_ref, dp_ref, mod_ref, g_ref, w_ref, dx_ref, red_ref, dh):
        i, j = pl.program_id(0), pl.program_id(1)

        @pl.when(j == 0)
        def _():
            dh[...] = jnp.zeros_like(dh)

        dh[...] += _nt(dp_ref[...], w_ref[...])

        @pl.when((j == 0) & ((i == 0) | (i == first)))
        def _():
            red_ref[...] = jnp.zeros_like(red_ref)

        @pl.when(j == nb - 1)
        def _():
            dxn, dshift, dscale, dg = _norm_mod_bwd(x_ref[...], dh[...], g_ref[...], mod_ref[1:2, :])
            dx_ref[...] = jnp.where(i < first, dxo_ref[...], 0.0) + dxn
            red_ref[0:1, :] += dshift
            red_ref[1:2, :] += dscale
            red_ref[3:4, :] += dg

    tok = pl.BlockSpec((tm, D), lambda i, j: (i, 0))
    return pl.pallas_call(
        body, name=name, grid=(nt, nb),
        in_specs=[tok, pl.BlockSpec((tm, D), lambda i, j: (jnp.minimum(i, first - 1), 0)),
                  pl.BlockSpec((tm, cw), lambda i, j: (i, j)),
                  pl.BlockSpec((None, 3, D), lambda i, j: (_group_index(i, first, G), 0, 0)),
                  pl.BlockSpec((1, D), lambda i, j: (0, 0)),
                  pl.BlockSpec((None, D, cw), lambda i, j: (j, 0, 0))],
        out_specs=[tok, pl.BlockSpec((None, 8, D), lambda i, j: (_group_index(i, first, G), 0, 0))],
        out_shape=[jax.ShapeDtypeStruct((T, D), F32), jax.ShapeDtypeStruct((G, 8, D), F32)],
        scratch_shapes=[pltpu.VMEM((tm, D), F32)],
        compiler_params=_params(("arbitrary", "arbitrary")),
    )(x, dxo, dp, mod, gn, w_in)


def _rope(t, cos, sin):
    return t * cos + pltpu.roll(t, HEAD_DIM // 2, axis=1) * sin


def ret_prep(p, cos, sin, *, heads, tm, name):
    T = p.shape[0]

    def body(q_ref, k_ref, v_ref, c_ref, s_ref, qo, ko, vo):
        cos_v, sin_v = c_ref[...], s_ref[...]
        qo[...] = _rope(q_ref[...], cos_v, sin_v).astype(BF16)
        ko[...] = _rope(k_ref[...] * K_SCALE, cos_v, sin_v).astype(BF16)
        vo[...] = v_ref[...].astype(BF16)

    def col(part):
        return pl.BlockSpec((tm, HEAD_DIM), lambda h, i: (i, part * heads + h))

    tab = pl.BlockSpec((tm, HEAD_DIM), lambda h, i: (i, 0))
    out = pl.BlockSpec((None, tm, HEAD_DIM), lambda h, i: (h, i, 0))
    return pl.pallas_call(
        body, name=name, grid=(heads, T // tm),
        in_specs=[col(0), col(1), col(2), tab, tab], out_specs=[out, out, out],
        out_shape=[jax.ShapeDtypeStruct((heads, T, HEAD_DIM), BF16)] * 3,
        compiler_params=_params(("arbitrary", "arbitrary")),
    )(p, p, p, cos, sin)


def _decay(n, m, lgf, lgb, t_lat, t_ctx):
    df = jnp.where(m < t_lat, n - m, n - m + (t_lat + t_ctx))
    db = m - n
    ef = jnp.where(df >= 0, jnp.exp(lgf * df), 0.0)
    eb = jnp.where(db >= 0, jnp.exp(lgb * db), 0.0)
    return ef, eb, df, db


def ret_fwd(q, k, v, lg, *, t_lat, tq, tk, name):
    H, T, _ = k.shape
    t_ctx = T - t_lat

    def body(lg_ref, q_ref, k_ref, v_ref, o_ref):
        h, qi, kj = pl.program_id(0), pl.program_id(1), pl.program_id(2)

        @pl.when(kj == 0)
        def _():
            o_ref[...] = jnp.zeros_like(o_ref)

        s = _nt(q_ref[...], k_ref[...])
        n = (qi * tq + lax.broadcasted_iota(jnp.int32, (tq, tk), 0)).astype(F32)
        m = (kj * tk + lax.broadcasted_iota(jnp.int32, (tq, tk), 1)).astype(F32)
        ef, eb, _, _ = _decay(n, m, lg_ref[0, h], lg_ref[1, h], t_lat, t_ctx)
        o_ref[...] += _nn((s * (ef + eb)).astype(BF16), v_ref[...])

    return pl.pallas_call(
        body, name=name, grid=(H, t_lat // tq, T // tk),
        in_specs=[pl.BlockSpec(memory_space=pltpu.SMEM),
                  pl.BlockSpec((None, tq, HEAD_DIM), lambda h, i, j: (h, i, 0)),
                  pl.BlockSpec((None, tk, HEAD_DIM), lambda h, i, j: (h, j, 0)),
                  pl.BlockSpec((None, tk, HEAD_DIM), lambda h, i, j: (h, j, 0))],
        out_specs=pl.BlockSpec((None, tq, HEAD_DIM), lambda h, i, j: (h, i, 0)),
        out_shape=jax.ShapeDtypeStruct((H, t_lat, HEAD_DIM), F32),
        compiler_params=_params(("arbitrary", "arbitrary", "arbitrary")),
    )(lg, q, k, v)


def ret_bwd(q, k, v, do, lg, *, t_lat, tq, tk, name):
    H, T, _ = k.shape
    t_ctx = T - t_lat

    def body(lg_ref, q_ref, k_ref, v_ref, do_ref, dq_ref, dk_ref, dv_ref, dlg_ref):
        h, kj, qi = pl.program_id(0), pl.program_id(1), pl.program_id(2)

        @pl.when((kj == 0) & (qi == 0))
        def _():
            dq_ref[...] = jnp.zeros_like(dq_ref)
            dlg_ref[...] = jnp.zeros_like(dlg_ref)

        @pl.when(qi == 0)
        def _():
            dk_ref[...] = jnp.zeros_like(dk_ref)
            dv_ref[...] = jnp.zeros_like(dv_ref)

        qv, kv, vv = q_ref[...], k_ref[...], v_ref[...]
        dob = do_ref[...].astype(BF16)
        st = _nt(kv, qv)
        dwt = _nt(vv, dob)
        m = (kj * tk + lax.broadcasted_iota(jnp.int32, (tk, tq), 0)).astype(F32)
        n = (qi * tq + lax.broadcasted_iota(jnp.int32, (tk, tq), 1)).astype(F32)
        ef, eb, df, db = _decay(n, m, lg_ref[0, h], lg_ref[1, h], t_lat, t_ctx)
        dec = ef + eb
        dv_ref[...] += _nn((st * dec).astype(BF16), dob)
        dst = (dwt * dec).astype(BF16)
        dk_ref[...] += _nn(dst, qv)
        rows = pl.ds(pl.multiple_of(qi * tq, tq), tq)
        dq_ref[rows, :] += _tn(dst, kv)
        gs = dwt * st
        dlg_ref[0:1, :] += jnp.sum(gs * (ef * df))
        dlg_ref[1:2, :] += jnp.sum(gs * (eb * db))

    kspec = pl.BlockSpec((None, tk, HEAD_DIM), lambda h, j, i: (h, j, 0))
    qspec = pl.BlockSpec((None, tq, HEAD_DIM), lambda h, j, i: (h, i, 0))
    return pl.pallas_call(
        body, name=name, grid=(H, T // tk, t_lat // tq),
        in_specs=[pl.BlockSpec(memory_space=pltpu.SMEM), qspec, kspec, kspec, qspec],
        out_specs=[pl.BlockSpec((None, t_lat, HEAD_DIM), lambda h, j, i: (h, 0, 0)), kspec, kspec,
                   pl.BlockSpec((None, 8, LANE), lambda h, j, i: (h, 0, 0))],
        out_shape=[jax.ShapeDtypeStruct((H, t_lat, HEAD_DIM), F32), jax.ShapeDtypeStruct((H, T, HEAD_DIM), F32),
                   jax.ShapeDtypeStruct((H, T, HEAD_DIM), F32), jax.ShapeDtypeStruct((H, 8, LANE), F32)],
        compiler_params=_params(("arbitrary", "arbitrary", "arbitrary")),
    )(lg, q, k, v, do)


def _group_norm(o):
    mu = jnp.mean(o, axis=-1, keepdims=True)
    ctr = o - mu
    r = lax.rsqrt(jnp.mean(ctr * ctr, axis=-1, keepdims=True) + GN_EPS)
    return ctr * r, r


def ret_post_fwd(o, p, *, heads, d_model, name):
    H, t_lat, _ = o.shape
    T = p.shape[0]

    def body(o_ref, g_ref, z_ref):
        on, _ = _group_norm(o_ref[...])
        _, sl = _silu_parts(g_ref[0:t_lat, :])
        z_ref[...] = (on * sl).astype(BF16)

    return pl.pallas_call(
        body, name=name, grid=(H,),
        in_specs=[pl.BlockSpec((None, t_lat, HEAD_DIM), lambda h: (h, 0, 0)),
                  pl.BlockSpec((T, HEAD_DIM), lambda h: (0, 3 * heads + h))],
        out_specs=pl.BlockSpec((t_lat, HEAD_DIM), lambda h: (0, h)),
        out_shape=jax.ShapeDtypeStruct((t_lat, d_model), BF16),
        compiler_params=_params(("arbitrary",)),
    )(o, p)


def _shift_rows(u, k):
    rows = u.shape[0]
    row = lax.broadcasted_iota(jnp.int32, u.shape, 0)
    rolled = pltpu.roll(u, k % rows, axis=0)
    return jnp.where((row >= k) & (row < rows + k), rolled, 0.0)


def conv_fwd(z, p, w_conv, *, heads, t_lat, name):
    T = p.shape[0]
    cb_n = w_conv.shape[1] // LANE
    base = 4 * heads

    def body(z_in, bg_ref, cg_ref, u_ref, w_ref, z_ref):
        cu = cg_ref[0:t_lat, :] * u_ref[0:t_lat, :]
        c3 = _shift_rows(cu, 1) * w_ref[0:1, :] + cu * w_ref[1:2, :] + _shift_rows(cu, -1) * w_ref[2:3, :]
        z_ref[...] = (bg_ref[0:t_lat, :] * c3).astype(BF16)

    def col(part):
        return pl.BlockSpec((T, LANE), lambda cb: (0, base + part * cb_n + cb))

    return pl.pallas_call(
        body, name=name, grid=(cb_n,),
        in_specs=[ANY, col(0), col(1), col(2), pl.BlockSpec((CONV_WIDTH, LANE), lambda cb: (0, cb))],
        out_specs=pl.BlockSpec((t_lat, LANE), lambda cb: (0, heads + cb)),
        out_shape=jax.ShapeDtypeStruct(z.shape, z.dtype),
        input_output_aliases={0: 0},
        compiler_params=_params(("arbitrary",)),
    )(z, p, p, p, w_conv)


def out_proj(z, w_out, x, mod, *, tm, name):
    t_lat, D = z.shape

    def body(z_ref, w_ref, x_ref, mod_ref, xo_ref, y_ref):
        y = _nn(z_ref[...], w_ref[...])
        y_ref[...] = y.astype(BF16)
        xo_ref[...] = x_ref[...] + mod_ref[2:3, :] * y

    tok = pl.BlockSpec((tm, D), lambda i: (i, 0))
    return pl.pallas_call(
        body, name=name, grid=(t_lat // tm,),
        in_specs=[tok, pl.BlockSpec((D, D), lambda i: (0, 0)), tok, pl.BlockSpec((None, 3, D), lambda i: (0, 0, 0))],
        out_specs=[tok, tok],
        out_shape=[jax.ShapeDtypeStruct((t_lat, D), F32), jax.ShapeDtypeStruct((t_lat, D), BF16)],
        compiler_params=_params(("arbitrary",)),
    )(z, w_out, x, mod)


def out_proj_bwd(dxo, y, w_out, mod, *, tm, name):
    t_lat, D = dxo.shape

    def body(dxo_ref, y_ref, w_ref, mod_ref, dz_ref, dy_ref, red_ref):
        @pl.when(pl.program_id(0) == 0)
        def _():
            red_ref[...] = jnp.zeros_like(red_ref)

        dxo_v = dxo_ref[...]
        dyb = (mod_ref[2:3, :] * dxo_v).astype(BF16)
        dy_ref[...] = dyb
        dz_ref[...] = _nt(dyb, w_ref[...])
        red_ref[2:3, :] += jnp.sum(dxo_v * y_ref[...].astype(F32), axis=0, keepdims=True)

    tok = pl.BlockSpec((tm, D), lambda i: (i, 0))
    return pl.pallas_call(
        body, name=name, grid=(t_lat // tm,),
        in_specs=[tok, tok, pl.BlockSpec((D, D), lambda i: (0, 0)), pl.BlockSpec((None, 3, D), lambda i: (0, 0, 0))],
        out_specs=[tok, tok, pl.BlockSpec((8, D), lambda i: (0, 0))],
        out_shape=[jax.ShapeDtypeStruct((t_lat, D), F32), jax.ShapeDtypeStruct((t_lat, D), BF16),
                   jax.ShapeDtypeStruct((8, D), F32)],
        compiler_params=_params(("arbitrary",)),
    )(dxo, y, w_out, mod)


def ret_post_bwd(dz, o, p, *, heads, name):
    H, t_lat, _ = o.shape
    T, in_w = p.shape

    def body(dz_ref, o_ref, g_ref, do_ref, dp_ref):
        on, r = _group_norm(o_ref[...])
        gg = g_ref[0:t_lat, :]
        sig, sl = _silu_parts(gg)
        dret = dz_ref[...]
        don = dret * sl
        do_ref[...] = r * (don - jnp.mean(don, axis=-1, keepdims=True)
                           - on * jnp.mean(don * on, axis=-1, keepdims=True))
        dp_ref[0:t_lat, :] = (dret * on * _dsilu(gg, sig)).astype(BF16)
        dp_ref[t_lat:T, :] = jnp.zeros((T - t_lat, HEAD_DIM), BF16)

    return pl.pallas_call(
        body, name=name, grid=(H,),
        in_specs=[pl.BlockSpec((t_lat, HEAD_DIM), lambda h: (0, h)),
                  pl.BlockSpec((None, t_lat, HEAD_DIM), lambda h: (h, 0, 0)),
                  pl.BlockSpec((T, HEAD_DIM), lambda h: (0, 3 * heads + h))],
        out_specs=[pl.BlockSpec((None, t_lat, HEAD_DIM), lambda h: (h, 0, 0)),
                   pl.BlockSpec((T, HEAD_DIM), lambda h: (0, 3 * heads + h))],
        out_shape=[jax.ShapeDtypeStruct((H, t_lat, HEAD_DIM), F32), jax.ShapeDtypeStruct((T, in_w), BF16)],
        compiler_params=_params(("arbitrary",)),
    )(dz, o, p)


def conv_bwd(dp, dz, p, w_conv, *, heads, t_lat, name):
    T = p.shape[0]
    cb_n = w_conv.shape[1] // LANE
    base = 4 * heads

    def body(dp_in, dz_ref, bg_ref, cg_ref, u_ref, w_ref, dp_ref, dw_ref):
        part = pl.program_id(1)
        cg, u = cg_ref[0:t_lat, :], u_ref[0:t_lat, :]
        cu = cg * u
        dconv = dz_ref[...]
        dp_ref[t_lat:T, :] = jnp.zeros((T - t_lat, LANE), BF16)

        @pl.when(part == 0)
        def _():
            c3 = _shift_rows(cu, 1) * w_ref[0:1, :] + cu * w_ref[1:2, :] + _shift_rows(cu, -1) * w_ref[2:3, :]
            dp_ref[0:t_lat, :] = (dconv * c3).astype(BF16)
            dc3 = dconv * bg_ref[0:t_lat, :]
            dw_ref[0:1, :] = jnp.sum(dc3 * _shift_rows(cu, 1), axis=0, keepdims=True)
            dw_ref[1:2, :] = jnp.sum(dc3 * cu, axis=0, keepdims=True)
            dw_ref[2:3, :] = jnp.sum(dc3 * _shift_rows(cu, -1), axis=0, keepdims=True)

        @pl.when(part > 0)
        def _():
            dc3 = dconv * bg_ref[0:t_lat, :]
            dcu = (_shift_rows(dc3, -1) * w_ref[0:1, :] + dc3 * w_ref[1:2, :] + _shift_rows(dc3, 1) * w_ref[2:3, :])
            dp_ref[0:t_lat, :] = (dcu * jnp.where(part == 1, u, cg)).astype(BF16)

    def col(part):
        return pl.BlockSpec((T, LANE), lambda cb, pt: (0, base + part * cb_n + cb))

    return pl.pallas_call(
        body, name=name, grid=(cb_n, 3),
        in_specs=[ANY, pl.BlockSpec((t_lat, LANE), lambda cb, pt: (0, heads + cb)), col(0), col(1), col(2),
                  pl.BlockSpec((CONV_WIDTH, LANE), lambda cb, pt: (0, cb))],
        out_specs=[pl.BlockSpec((T, LANE), lambda cb, pt: (0, base + pt * cb_n + cb)),
                   pl.BlockSpec((CONV_WIDTH, LANE), lambda cb, pt: (0, cb))],
        out_shape=[jax.ShapeDtypeStruct(dp.shape, dp.dtype), jax.ShapeDtypeStruct(w_conv.shape, F32)],
        input_output_aliases={0: 0},
        compiler_params=_params(("arbitrary", "arbitrary")),
    )(dp, dz, p, p, p, w_conv)


def ret_unprep(dp, dq, dk, dv, cos, sin, *, t_lat, tm, name):
    H, T, _ = dk.shape
    n_lat = t_lat // tm

    def body(dp_in, dq_ref, dk_ref, dv_ref, c_ref, s_ref, dp_ref):
        part, i = pl.program_id(0), pl.program_id(2)
        cos_v, sin_v = c_ref[...], s_ref[...]

        @pl.when(part == 0)
        def _():
            d = jnp.where(i < n_lat, dq_ref[...], 0.0)
            dp_ref[...] = _rope(d, cos_v, -sin_v).astype(BF16)

        @pl.when(part == 1)
        def _():
            dp_ref[...] = (_rope(dk_ref[...], cos_v, -sin_v) * K_SCALE).astype(BF16)

        @pl.when(part == 2)
        def _():
            dp_ref[...] = dv_ref[...].astype(BF16)

    full = pl.BlockSpec((None, tm, HEAD_DIM), lambda pt, h, i: (h, i, 0))
    tab = pl.BlockSpec((tm, HEAD_DIM), lambda pt, h, i: (i, 0))
    return pl.pallas_call(
        body, name=name, grid=(3, H, T // tm),
        in_specs=[ANY, pl.BlockSpec((None, tm, HEAD_DIM), lambda pt, h, i: (h, jnp.minimum(i, n_lat - 1), 0)),
                  full, full, tab, tab],
        out_specs=pl.BlockSpec((tm, HEAD_DIM), lambda pt, h, i: (i, pt * H + h)),
        out_shape=jax.ShapeDtypeStruct(dp.shape, dp.dtype),
        input_output_aliases={0: 0},
        compiler_params=_params(("arbitrary", "arbitrary", "arbitrary")),
    )(dp, dq, dk, dv, cos, sin)


def norm_mod_fwd(x, mod, gn, *, tm, name):
    T, D = x.shape

    def body(x_ref, mod_ref, g_ref, h_ref):
        h_ref[...] = _norm_mod(x_ref[...], g_ref[...], mod_ref[0:1, :], mod_ref[1:2, :])

    tok = pl.BlockSpec((tm, D), lambda i: (i, 0))
    return pl.pallas_call(
        body, name=name, grid=(T // tm,),
        in_specs=[tok, pl.BlockSpec((None, 3, D), lambda i: (0, 0, 0)), pl.BlockSpec((1, D), lambda i: (0, 0))],
        out_specs=tok, out_shape=jax.ShapeDtypeStruct((T, D), F32),
        compiler_params=_params(("arbitrary",)),
    )(x, mod, gn)


def norm_mod_bwd(x, dh, dxo, mod, gn, *, tm, name):
    T, D = x.shape

    def body(x_ref, dh_ref, dxo_ref, mod_ref, g_ref, dx_ref, red_ref):
        @pl.when(pl.program_id(0) == 0)
        def _():
            red_ref[...] = jnp.zeros_like(red_ref)

        dxn, dshift, dscale, dg = _norm_mod_bwd(x_ref[...], dh_ref[...], g_ref[...], mod_ref[1:2, :])
        dx_ref[...] = dxo_ref[...] + dxn
        red_ref[0:1, :] += dshift
        red_ref[1:2, :] += dscale
        red_ref[3:4, :] += dg

    tok = pl.BlockSpec((tm, D), lambda i: (i, 0))
    return pl.pallas_call(
        body, name=name, grid=(T // tm,),
        in_specs=[tok, tok, tok, pl.BlockSpec((None, 3, D), lambda i: (0, 0, 0)), pl.BlockSpec((1, D), lambda i: (0, 0))],
        out_specs=[tok, pl.BlockSpec((8, D), lambda i: (0, 0))],
        out_shape=[jax.ShapeDtypeStruct((T, D), F32), jax.ShapeDtypeStruct((8, D), F32)],
        compiler_params=_params(("arbitrary",)),
    )(x, dh, dxo, mod, gn)


def _window_sum(u, w, lead):
    T, C = u.shape
    ext = jnp.concatenate([u, jnp.zeros((POOL_PAD, C), F32)], axis=0)
    k = 1
    while k < w:
        ext = ext + _shift_rows(ext, k)
        k *= 2
    return _shift_rows(ext, -lead)[0:T, :]


def _window_count(T, C, w):
    t = lax.broadcasted_iota(jnp.int32, (T, C), 0)
    lo = jnp.clip(t - w // 2, 0, T)
    hi = jnp.clip(t + (w - w // 2), 0, T)
    return (hi - lo).astype(F32)


def pool_fwd(h, x, pool_w, scale, mod, *, name):
    T, D = h.shape
    G, Cg, _ = pool_w.shape
    ns = Cg // LANE

    def body(h_ref, x_ref, w_ref, sc_ref, mod_ref, xo_ref, pl_ref, yl_ref, acc):
        g, s = pl.program_id(0), pl.program_id(1)
        hv = h_ref[...]
        for gi, win in enumerate(POOL_WINDOWS):
            @pl.when(g == gi)
            def _():
                mean = _window_sum(hv, win, win // 2 - 1) / _window_count(T, LANE, win)
                pooled = (mean - hv).astype(BF16)
                pl_ref[...] = pooled
                contrib = _nn(pooled, w_ref[...])

                @pl.when(s == 0)
                def _():
                    acc[...] = contrib

                @pl.when(s > 0)
                def _():
                    acc[...] += contrib

        @pl.when(s == ns - 1)
        def _():
            yl = acc[...]
            yl_ref[...] = yl.astype(BF16)
            xo_ref[...] = x_ref[...] + mod_ref[2:3, :] * (yl * sc_ref[...])

    grp = pl.BlockSpec((T, Cg), lambda g, s: (0, g))
    sub = pl.BlockSpec((T, LANE), lambda g, s: (0, g * ns + s))
    return pl.pallas_call(
        body, name=name, grid=(G, ns),
        in_specs=[sub, grp, pl.BlockSpec((None, LANE, Cg), lambda g, s: (g, s, 0)),
                  pl.BlockSpec((1, Cg), lambda g, s: (0, g)), pl.BlockSpec((None, 3, Cg), lambda g, s: (0, 0, g))],
        out_specs=[grp, sub, grp],
        out_shape=[jax.ShapeDtypeStruct((T, D), F32), jax.ShapeDtypeStruct((T, D), BF16),
                   jax.ShapeDtypeStruct((T, D), BF16)],
        scratch_shapes=[pltpu.VMEM((T, Cg), F32)],
        compiler_params=_params(("arbitrary", "arbitrary")),
    )(h, x, pool_w, scale, mod)


def pool_bwd(dxo, pooled, yl, pool_w, scale, mod, *, name):
    T, D = dxo.shape
    G, Cg, _ = pool_w.shape
    ns = Cg // LANE

    def body(dxo_ref, pl_ref, yl_ref, w_ref, sc_ref, mod_ref, dh_ref, dw_ref, red_ref, dyl):
        g, s = pl.program_id(0), pl.program_id(1)

        @pl.when(s == 0)
        def _():
            dxo_v = dxo_ref[...]
            ylv = yl_ref[...].astype(F32)
            dy = mod_ref[2:3, :] * dxo_v
            dyl[...] = (dy * sc_ref[...]).astype(BF16)
            red_ref[...] = jnp.zeros_like(red_ref)
            red_ref[2:3, :] = jnp.sum(dxo_v * (ylv * sc_ref[...]), axis=0, keepdims=True)
            red_ref[4:5, :] = jnp.sum(dy * ylv, axis=0, keepdims=True)

        dylv = dyl[...]
        dpool = _nt(dylv, w_ref[...])
        dw_ref[...] = _tn(pl_ref[...], dylv).astype(BF16)
        for gi, win in enumerate(POOL_WINDOWS):
            @pl.when(g == gi)
            def _():
                spread = _window_sum(dpool / _window_count(T, LANE, win), win, win // 2)
                dh_ref[...] = spread - dpool

    grp = pl.BlockSpec((T, Cg), lambda g, s: (0, g))
    sub = pl.BlockSpec((T, LANE), lambda g, s: (0, g * ns + s))
    wsub = pl.BlockSpec((None, LANE, Cg), lambda g, s: (g, s, 0))
    return pl.pallas_call(
        body, name=name, grid=(G, ns),
        in_specs=[grp, sub, grp, wsub, pl.BlockSpec((1, Cg), lambda g, s: (0, g)),
                  pl.BlockSpec((None, 3, Cg), lambda g, s: (0, 0, g))],
        out_specs=[sub, wsub, pl.BlockSpec((8, Cg), lambda g, s: (0, g))],
        out_shape=[jax.ShapeDtypeStruct((T, D), F32), jax.ShapeDtypeStruct((G, Cg, Cg), BF16),
                   jax.ShapeDtypeStruct((8, D), F32)],
        scratch_shapes=[pltpu.VMEM((T, Cg), BF16)],
        compiler_params=_params(("arbitrary", "arbitrary")),
    )(dxo, pooled, yl, pool_w, scale, mod)


def final_loss(x, gn, target, *, tm, name):
    T, D = x.shape

    def body(x_ref, g_ref, t_ref, loss_ref, dx_ref, red_ref):
        @pl.when(pl.program_id(0) == 0)
        def _():
            loss_ref[...] = jnp.zeros_like(loss_ref)
            red_ref[...] = jnp.zeros_like(red_ref)

        xx, g = x_ref[...], g_ref[...]
        r = lax.rsqrt(jnp.mean(xx * xx, axis=-1, keepdims=True) + EPS)
        xhat = xx * r
        err = xhat * g - t_ref[...]
        loss_ref[...] += 0.5 * jnp.sum(jnp.mean(err * err, axis=-1, keepdims=True))
        dy = err / D
        red_ref[0:1, :] += jnp.sum(dy * xhat, axis=0, keepdims=True)
        dxh = dy * g
        dx_ref[...] = r * (dxh - xhat * jnp.mean(dxh * xhat, axis=-1, keepdims=True))

    tok = pl.BlockSpec((tm, D), lambda i: (i, 0))
    return pl.pallas_call(
        body, name=name, grid=(T // tm,),
        in_specs=[tok, pl.BlockSpec((1, D), lambda i: (0, 0)), tok],
        out_specs=[pl.BlockSpec((8, LANE), lambda i: (0, 0)), tok, pl.BlockSpec((8, D), lambda i: (0, 0))],
        out_shape=[jax.ShapeDtypeStruct((8, LANE), F32), jax.ShapeDtypeStruct((T, D), F32),
                   jax.ShapeDtypeStruct((8, D), F32)],
        compiler_params=_params(("arbitrary",)),
    )(x, gn, target)


def _rope_tables(t_lat, t_ctx):
    quarter = HEAD_DIM // 4
    pos = jnp.arange(t_lat)
    inv = ROPE_BASE ** (-jnp.arange(quarter, dtype=F32) / quarter)
    ang = jnp.concatenate([(pos // GRID_W).astype(F32)[:, None] * inv, (pos % GRID_W).astype(F32)[:, None] * inv], axis=-1)
    cos, sin = jnp.cos(ang), jnp.sin(ang)
    cos = jnp.concatenate([jnp.concatenate([cos, cos], axis=-1), jnp.ones((t_ctx, HEAD_DIM), F32)], axis=0)
    sin = jnp.concatenate([jnp.concatenate([-sin, sin], axis=-1), jnp.zeros((t_ctx, HEAD_DIM), F32)], axis=0)
    return cos, sin


def _ffn_grads(h, da, db, s, dy, tag):
    nb, T, cw = da.shape
    D = h.shape[1]
    tk = 512 if T % 512 == 0 else 256
    g_gate = matmul_tn(da, h, nb=nb, m=cw, n=D, a_blocked=True, b_blocked=False, tk=tk, name=f"wgrad_gate_{tag}")
    g_up = matmul_tn(db, h, nb=nb, m=cw, n=D, a_blocked=True, b_blocked=False, tk=tk, name=f"wgrad_up_{tag}")
    g_down = matmul_tn(s, dy, nb=nb, m=cw, n=D, a_blocked=True, b_blocked=False, tk=tk, name=f"wgrad_down_{tag}")
    return [g_gate, g_up, g_down]


def kernel(x, c, ctx, c_ctx, w_mod, b_mod, norm_ffn1, norm_mix, norm_ffn2, ffn1_w_gate, ffn1_w_up, ffn1_w_down, ffn2_w_gate, ffn2_w_up, ffn2_w_down, mix_w_in, mix_w_conv, mix_w_out, ret_decay_fwd, ret_decay_bwd, pool_w, pool_scale, final_norm, loss_target, m_c_ctx, m_w_mod, m_b_mod, m_norm_ffn1, m_norm_mix, m_norm_ffn2, m_ffn1_w_gate, m_ffn1_w_up, m_ffn1_w_down, m_ffn2_w_gate, m_ffn2_w_up, m_ffn2_w_down, m_mix_w_in, m_mix_w_conv, m_mix_w_out, m_ret_decay_fwd, m_ret_decay_bwd, m_pool_w, m_pool_scale, m_final_norm, v_c_ctx, v_w_mod, v_b_mod, v_norm_ffn1, v_norm_mix, v_norm_ffn2, v_ffn1_w_gate, v_ffn1_w_up, v_ffn1_w_down, v_ffn2_w_gate, v_ffn2_w_up, v_ffn2_w_down, v_mix_w_in, v_mix_w_conv, v_mix_w_out, v_ret_decay_fwd, v_ret_decay_bwd, v_pool_w, v_pool_scale, v_final_norm):
    t_lat, D = x.shape[1], x.shape[2]
    t_ctx = ctx.shape[1]
    T = t_lat + t_ctx
    heads = ret_decay_fwd.shape[1]
    mod_w = w_mod.shape[2]
    tm = 256
    tmf = 512 if t_lat % 512 == 0 else 256
    tq = 512 if t_lat % 512 == 0 else 256
    tk = T // 3 if (T % 3 == 0 and (T // 3) % 256 == 0) else 256

    ax, ay, ac = lax.axis_index("x"), lax.axis_index("y"), lax.axis_index("c")
    me = 4 * ax + 2 * ay + ac
    own_block = jnp.reshape(me, (1,)).astype(jnp.int32)
    peer_blocks = jnp.stack([4 * (1 - ax) + 2 * ay + ac, 4 * ax + 2 * (1 - ay) + ac,
                             4 * (1 - ax) + 2 * (1 - ay) + ac]).astype(jnp.int32)

    (c_all,) = all_gather([c], name="gather_cond")
    cond16 = jnp.concatenate([c_all.reshape(N_DEV, D), c_ctx[None, :], jnp.zeros((7, D), F32)], axis=0)
    b_blk = lax.dynamic_slice_in_dim(b_mod, me * mod_w, mod_w, axis=1)[:, None, :]
    m_blk = adaln_fwd(cond16, w_mod, b_blk, name="adaln_fwd")
    m_all, w_conv, pscale = all_gather([m_blk, mix_w_conv[0], pool_scale], name="gather_mod")
    mods = jnp.transpose(m_all, (1, 2, 0, 3)).reshape(2, 16, N_MOD, D)
    mod_x = lax.dynamic_index_in_dim(mods, me, axis=1, keepdims=False)
    mod_c = mods[0, 8]
    w_conv = jnp.transpose(w_conv, (1, 0, 2)).reshape(CONV_WIDTH, -1)
    pscale = pscale.reshape(1, D)

    def ffn_shards(wg, wu, wd, l):
        return [jnp.swapaxes(wg[l], 0, 1).astype(BF16), jnp.swapaxes(wu[l], 0, 1).astype(BF16), wd[l].astype(BF16)]

    n_grp, grp_rows, grp_w = pool_w.shape[1:]
    shards = {
        "01": ffn_shards(ffn1_w_gate, ffn1_w_up, ffn1_w_down, 0),
        "mix": [mix_w_in[0].astype(BF16), mix_w_out[0].astype(BF16)],
        "02": ffn_shards(ffn2_w_gate, ffn2_w_up, ffn2_w_down, 0),
        "11": ffn_shards(ffn1_w_gate, ffn1_w_up, ffn1_w_down, 1) + [pool_w[0].astype(BF16).reshape(n_grp * grp_rows, grp_w)],
        "12": ffn_shards(ffn2_w_gate, ffn2_w_up, ffn2_w_down, 1),
    }
    started = {}

    def start(tag, after):
        started[tag] = gather_start(shards[tag], after, name=f"gather_start_{tag}")
        return started[tag][-1][0:1, 0:1]

    def finish(tag, after):
        own, lands = gather_wait(started[tag], after, name=f"gather_wait_{tag}")
        return gather_finish(own, lands, name=f"gather_finish_{tag}")

    lg = jnp.concatenate([jax.nn.log_sigmoid(ret_decay_fwd), jax.nn.log_sigmoid(ret_decay_bwd)], axis=0)
    cos, sin = _rope_tables(t_lat, t_ctx)

    def mod3(l, k, with_ctx=False, tie=None):
        rows = mod_x[l, 3 * k:3 * k + 3][None]
        if with_ctx:
            rows = jnp.concatenate([rows, mod_c[3 * k:3 * k + 3][None]], axis=0)
        return rows if tie is None else rows + tie

    tie = start("01", m_all)
    x0 = jnp.concatenate([x[0], ctx[0]], axis=0) + tie
    wg01, wu01, wd01 = finish("01", x0)
    tie = start("mix", wd01)
    x1, h1, a1, b1, y1 = ffn_fwd(x0, mod3(0, 0, True, tie), norm_ffn1[0:1], wg01, wu01, wd01,
                                 tm=tm, t_first=t_lat, name="ffn_fwd_01")
    w_in, w_out = finish("mix", h1)
    tie = start("02", w_out)
    w_out = w_out.reshape(D, D)
    p, hm = proj_in(x1, mod3(0, 1, True, tie), norm_mix[0:1], w_in, tm=tm, t_first=t_lat, name="proj_in")
    qr, kr, vr = ret_prep(p, cos, sin, heads=heads, tm=tm, name="ret_prep")
    o = ret_fwd(qr, kr, vr, lg, t_lat=t_lat, tq=tq, tk=tk, name="ret_fwd")
    z = ret_post_fwd(o, p, heads=heads, d_model=D, name="ret_post_fwd")
    z = conv_fwd(z, p, w_conv, heads=heads, t_lat=t_lat, name="conv_fwd")
    x2, ym = out_proj(z, w_out, x1, mod3(0, 1), tm=tm, name="out_proj")
    wg02, wu02, wd02 = finish("02", ym)
    tie = start("11", wd02)
    x3, h3, a3, b3, y3 = ffn_fwd(x2, mod3(0, 2, tie=tie), norm_ffn2[0:1], wg02, wu02, wd02, tm=tmf, t_first=t_lat, name="ffn_fwd_02")
    wg11, wu11, wd11, pw = finish("11", h3)
    tie = start("12", wd11)
    pw = jnp.transpose(pw.reshape(N_DEV, n_grp, grp_rows, grp_w), (1, 0, 2, 3)).reshape(n_grp, N_DEV * grp_rows, grp_w)
    x4, h4, a4, b4, y4 = ffn_fwd(x3, mod3(1, 0, tie=tie), norm_ffn1[1:2], wg11, wu11, wd11, tm=tmf, t_first=t_lat, name="ffn_fwd_11")
    hp = norm_mod_fwd(x4, mod3(1, 1), norm_mix[1:2], tm=tm, name="pool_norm_fwd")
    x5, pooled, yl = pool_fwd(hp, x4, pw, pscale, mod3(1, 1), name="pool_fwd")
    wg12, wu12, wd12 = finish("12", yl)
    x6, h6, a6, b6, y6 = ffn_fwd(x5, mod3(1, 2), norm_ffn2[1:2], wg12, wu12, wd12, tm=tmf, t_first=t_lat, name="ffn_fwd_12")
    loss_part, dx6, red_fn = final_loss(x6, final_norm[None, :], loss_target[0], tm=tm, name="final_loss")
    loss = lax.psum(loss_part[0, 0], ("x", "y", "c"))

    reducing = {}

    def reduce_start(tag, grads, after=None):
        res = rs_sibling_exchange(grads, grads[0] if after is None else after, name=f"rs_sibling_{tag}")
        n = len(grads)
        stages, directs = res[:n], res[n:]
        sums = [chip_sum(g, st, peer_blocks, name=f"chip_sum_{tag}_{t}") for t, (g, st) in enumerate(zip(grads, stages))]
        handle = chip_exchange_start(sums, name=f"rs_start_{tag}")
        reducing[tag] = (grads, directs, handle)
        return handle[-1][0:1, 0:1]

    dx5, da6, db6, s6, dy6, red12 = ffn_bwd_tok(x5, dx6, y6, a6, b6, mod3(1, 2), norm_ffn2[1:2], wg12, wu12, wd12,
                                                tm=tmf, t_first=t_lat, name="ffn_bwd_12")
    tie = reduce_start("12", _ffn_grads(h6, da6, db6, s6, dy6, "12"))
    dhp, g_pw, red_pool = pool_bwd(dx5, pooled, yl, pw, pscale, mod3(1, 1, tie=tie), name="pool_bwd")
    dx4, red_pn = norm_mod_bwd(x4, dhp, dx5, mod3(1, 1), norm_mix[1:2], tm=tm, name="pool_norm_bwd")
    dx3, da4, db4, s4, dy4, red11 = ffn_bwd_tok(x3, dx4, y4, a4, b4, mod3(1, 0), norm_ffn1[1:2], wg11, wu11, wd11,
                                                tm=tmf, t_first=t_lat, name="ffn_bwd_11")
    g_pw = jnp.transpose(g_pw.reshape(n_grp, N_DEV, grp_rows, grp_w), (1, 0, 2, 3)).reshape(N_DEV, n_grp * grp_rows, grp_w)
    tie = reduce_start("11", list(_ffn_grads(h4, da4, db4, s4, dy4, "11")) + [g_pw])
    dx2, da3, db3, s3, dy3, red02 = ffn_bwd_tok(x2, dx3, y3, a3, b3, mod3(0, 2, tie=tie), norm_ffn2[0:1], wg02, wu02, wd02,
                                                tm=tmf, t_first=t_lat, name="ffn_bwd_02")
    tie = reduce_start("02", _ffn_grads(h3, da3, db3, s3, dy3, "02"))
    dz, dym, red_op = out_proj_bwd(dx2, ym, w_out, mod3(0, 1, tie=tie), tm=tm, name="out_proj_bwd")
    g_wout = matmul_tn(z, dym, nb=N_DEV, m=D // N_DEV, n=D, a_blocked=False, b_blocked=False,
                       tk=512 if t_lat % 512 == 0 else 256, name="wgrad_out")
    do, dp = ret_post_bwd(dz, o, p, heads=heads, name="ret_post_bwd")
    dp, g_conv = conv_bwd(dp, dz, p, w_conv, heads=heads, t_lat=t_lat, name="conv_bwd")
    dq, dk, dv, dlg = ret_bwd(qr, kr, vr, do, lg, t_lat=t_lat, tq=tq, tk=tk, name="ret_bwd")
    dp = ret_unprep(dp, dq, dk, dv, cos, sin, t_lat=t_lat, tm=tm, name="ret_unprep")
    dx1, red_mix = proj_bwd_tok(x1, dx2, dp, mod3(0, 1, True), norm_mix[0:1], w_in, tm=tm, name="proj_bwd")
    g_win = matmul_tn(hm, dp, nb=N_DEV, m=D, n=w_in.shape[2], a_blocked=False, b_blocked=False, tk=256, name="wgrad_in")
    tie = reduce_start("mix", [g_win, g_wout])
    dx0, da1, db1, s1, dy1, red01 = ffn_bwd_tok(x0, dx1, y1, a1, b1, mod3(0, 0, True, tie), norm_ffn1[0:1], wg01, wu01, wd01,
                                                tm=tm, t_first=t_lat, name="ffn_bwd_01")
    res = {}

    dm_x = jnp.stack([jnp.concatenate([red01[0, 0:3], red_mix[0, 0:2], red_op[2:3], red02[0, 0:3]], axis=0),
                      jnp.concatenate([red11[0, 0:3], red_pn[0:2], red_pool[2:3], red12[0, 0:3]], axis=0)])
    dm_c = jnp.concatenate([red01[1, 0:3], red_mix[1, 0:2], jnp.zeros((4, D), F32)], axis=0)
    d_lg = dlg[:, 0:2, 0].T
    d_dec_f = d_lg[0:1] * jax.nn.sigmoid(-ret_decay_fwd)
    d_dec_b = d_lg[1:2] * jax.nn.sigmoid(-ret_decay_bwd)
    pieces = [dm_x.reshape(-1), dm_c.reshape(-1),
              jnp.stack([red01[0, 3] + red01[1, 3], red11[0, 3]]).reshape(-1),
              jnp.stack([red_mix[0, 3] + red_mix[1, 3], red_pn[3]]).reshape(-1),
              jnp.stack([red02[0, 3], red12[0, 3]]).reshape(-1),
              red_fn[0], d_dec_f.reshape(-1), d_dec_b.reshape(-1), g_conv.reshape(-1), red_pool[4]]
    sizes = [int(a.shape[0]) for a in pieces]
    n_pack = sum(sizes)
    n_pad = -n_pack % 8192
    packed = jnp.concatenate(pieces + [jnp.zeros((n_pad,), F32)])[None, :]
    (packed_all,) = all_gather([packed], name="gather_partials")
    total = sum_partials(packed_all, name="sum_partials")[0]
    offs = [0]
    for s in sizes:
        offs.append(offs[-1] + s)
    seg = [total[offs[i]:offs[i + 1]] for i in range(len(sizes))]
    g_dm = seg[0].reshape(2, N_MOD * D)
    g_dmc = seg[1].reshape(N_MOD * D)
    g_b_mod = g_dm.at[0].add(g_dmc)
    g_norm_ffn1, g_norm_mix, g_norm_ffn2 = (seg[k].reshape(2, D) for k in (2, 3, 4))
    g_final = seg[5]
    g_dec_f, g_dec_b = seg[6].reshape(1, heads), seg[7].reshape(1, heads)
    g_conv_all = seg[8].reshape(CONV_WIDTH, -1)
    g_pscale_all = seg[9]
    conv_w = mix_w_conv.shape[2]
    g_w_conv = lax.dynamic_slice_in_dim(g_conv_all, me * conv_w, conv_w, axis=1)[None]
    ps_w = pool_scale.shape[1]
    g_pool_scale = lax.dynamic_slice_in_dim(g_pscale_all, me * ps_w, ps_w, axis=0)[None]

    dm_rows = packed_all[:, 0, 0:offs[1]].reshape(N_DEV, 2, N_MOD * D)
    dmc_rows = packed_all[:, 0, offs[1]:offs[2]].reshape(N_DEV, N_MOD * D)
    dm_x_blk = jnp.transpose(lax.dynamic_slice_in_dim(dm_rows, me * mod_w, mod_w, axis=2), (1, 0, 2))
    dm_c_blk = jnp.stack([lax.dynamic_slice_in_dim(dmc_rows, me * mod_w, mod_w, axis=1),
                          jnp.zeros((N_DEV, mod_w), F32)])
    g_w_mod, d_w_mod, nm_w_mod, nv_w_mod, cctx_part = adaln_bwd(cond16, dm_x_blk, dm_c_blk, w_mod, m_w_mod, v_w_mod,
                                                                name="adaln_bwd")
    cpad = jnp.concatenate([cctx_part[0], jnp.zeros((8192 - D,), F32)])[None, :] if D < 8192 else cctx_part[0:1]
    (cctx_all,) = all_gather([cpad], name="gather_cctx")
    g_c_ctx = sum_partials(cctx_all, name="sum_cctx")[0, :D]

    small = [("c_ctx", g_c_ctx, c_ctx, m_c_ctx, v_c_ctx), ("b_mod", g_b_mod, b_mod, m_b_mod, v_b_mod),
             ("norm_ffn1", g_norm_ffn1, norm_ffn1, m_norm_ffn1, v_norm_ffn1),
             ("norm_mix", g_norm_mix, norm_mix, m_norm_mix, v_norm_mix),
             ("norm_ffn2", g_norm_ffn2, norm_ffn2, m_norm_ffn2, v_norm_ffn2),
             ("mix_w_conv", g_w_conv, mix_w_conv, m_mix_w_conv, v_mix_w_conv),
             ("ret_decay_fwd", g_dec_f, ret_decay_fwd, m_ret_decay_fwd, v_ret_decay_fwd),
             ("ret_decay_bwd", g_dec_b, ret_decay_bwd, m_ret_decay_bwd, v_ret_decay_bwd),
             ("pool_scale", g_pool_scale, pool_scale, m_pool_scale, v_pool_scale),
             ("final_norm", g_final, final_norm, m_final_norm, v_final_norm)]
    ssz = [int(a[1].size) for a in small]
    spad = -sum(ssz) % LANE

    def pack(k):
        return jnp.concatenate([a[k].reshape(-1) for a in small] + [jnp.ones((spad,), F32)])[None, :]

    sd, sm, sv = adam_plain(pack(1), pack(2), pack(3), pack(4), name="adam_small")
    soff = [0]
    for s in ssz:
        soff.append(soff[-1] + s)
    for i, (nm, g, w, _, _) in enumerate(small):
        res[nm] = [g.reshape(w.shape)] + [a[0, soff[i]:soff[i + 1]].reshape(w.shape) for a in (sd, sm, sv)]
    res["w_mod"] = [g_w_mod, d_w_mod, nm_w_mod, nv_w_mod]

    after = sd
    tk01 = 512 if T % 512 == 0 else 256
    for kind, lhs, rhs in (("gate", da1, h1), ("up", db1, h1), ("down", s1, dy1)):
        g = matmul_tn(lhs, rhs, nb=N_DEV, m=lhs.shape[2], n=D, a_blocked=True, b_blocked=False, tk=tk01,
                      name=f"wgrad_{kind}_01", after=after)
        tie = reduce_start(f"01_{kind}", [g], g)
        after = reducing[f"01_{kind}"][2][2]
    grad_x = dx0[:t_lat][None] + tie

    last = [grad_x]

    def reduce_finish(tag):
        grads, directs, handle = reducing[tag]
        landed = chip_exchange_wait(handle, last[0], name=f"rs_wait_{tag}")
        return list(zip(grads, directs, landed))

    def big(w, m, v, l, part, prev, nm, transposed=False):
        if transposed:
            w, m, v = (jnp.swapaxes(a, 1, 2) for a in (w, m, v))
        outs = adam_reduced(own_block, part[0], part[1], part[2], w, m, v, l, prev, name=nm)
        last[0] = outs[0]
        return outs

    ffn1 = [(ffn1_w_gate, m_ffn1_w_gate, v_ffn1_w_gate), (ffn1_w_up, m_ffn1_w_up, v_ffn1_w_up),
            (ffn1_w_down, m_ffn1_w_down, v_ffn1_w_down)]
    ffn2 = [(ffn2_w_gate, m_ffn2_w_gate, v_ffn2_w_gate), (ffn2_w_up, m_ffn2_w_up, v_ffn2_w_up),
            (ffn2_w_down, m_ffn2_w_down, v_ffn2_w_down)]
    kinds = ["gate", "up", "down"]
    half = {}
    parts = reduce_finish("12")
    for t, (w, m, v) in enumerate(ffn2):
        half[f"ffn2_w_{kinds[t]}"] = big(w, m, v, 1, parts[t], None, f"adam_ffn2_w_{kinds[t]}_1", transposed=t < 2)
    parts = reduce_finish("11")
    for t, (w, m, v) in enumerate(ffn1):
        half[f"ffn1_w_{kinds[t]}"] = big(w, m, v, 1, parts[t], None, f"adam_ffn1_w_{kinds[t]}_1", transposed=t < 2)
    res["pool_w"] = [a.reshape(pool_w.shape) for a in
                     big(pool_w.reshape(1, n_grp * grp_rows, grp_w), m_pool_w.reshape(1, n_grp * grp_rows, grp_w),
                         v_pool_w.reshape(1, n_grp * grp_rows, grp_w), 0, parts[3], None, "adam_pool_w")]
    parts = reduce_finish("02")
    for t, (w, m, v) in enumerate(ffn2):
        nm = f"ffn2_w_{kinds[t]}"
        outs = big(w, m, v, 0, parts[t], half[nm], f"adam_{nm}_0", transposed=t < 2)
        res[nm] = [jnp.swapaxes(a, 1, 2) for a in outs] if t < 2 else outs
    parts = reduce_finish("mix")
    res["mix_w_in"] = big(mix_w_in, m_mix_w_in, v_mix_w_in, 0, parts[0], None, "adam_mix_w_in")
    res["mix_w_out"] = big(mix_w_out, m_mix_w_out, v_mix_w_out, 0, parts[1], None, "adam_mix_w_out")
    for t, (w, m, v) in enumerate(ffn1):
        nm = f"ffn1_w_{kinds[t]}"
        outs = big(w, m, v, 0, reduce_finish(f"01_{kinds[t]}")[0], half[nm], f"adam_{nm}_0", transposed=t < 2)
        res[nm] = [jnp.swapaxes(a, 1, 2) for a in outs] if t < 2 else outs

    order = ["c_ctx", "w_mod", "b_mod", "norm_ffn1", "norm_mix", "norm_ffn2", "ffn1_w_gate", "ffn1_w_up", "ffn1_w_down",
             "ffn2_w_gate", "ffn2_w_up", "ffn2_w_down", "mix_w_in", "mix_w_conv", "mix_w_out", "ret_decay_fwd",
             "ret_decay_bwd", "pool_w", "pool_scale", "final_norm"]
    return (loss, grad_x, *[res[n][0] for n in order], *[res[n][1] for n in order],
            *[res[n][2] for n in order], *[res[n][3] for n in order])
```

```python
import jax
import jax.numpy as jnp
from jax import lax
from jax.experimental import pallas as pl
from jax.experimental.pallas import tpu as pltpu

F32 = jnp.float32
BF16 = jnp.bfloat16
MESH = pl.DeviceIdType.MESH

N_DEV = 8
N_MOD = 9
EPS = 1e-6
GN_EPS = 1e-5
MACARON = 0.5
HEAD_DIM = 128
K_SCALE = HEAD_DIM ** -0.5
ROPE_BASE = 10000.0
GRID_W = 64
CONV_WIDTH = 3
POOL_WINDOWS = (2, 4, 8, 16)
POOL_PAD = 16

ADAM_LR = 0.001
ADAM_B1 = 0.9
ADAM_B2 = 0.999
ADAM_EPS = 1e-08
ADAM_WD = 0.01
ADAM_STEP = 10

LANE = 128
ROW_CHUNK = 128
VMEM_LIMIT = 56 * 1024 * 1024
ANY = pl.BlockSpec(memory_space=pl.ANY)


def _params(sem=None):
    kw = dict(vmem_limit_bytes=VMEM_LIMIT)
    if sem is not None:
        kw["dimension_semantics"] = sem
    return pltpu.CompilerParams(**kw)


def _nt(a, b):
    return lax.dot_general(a, b, (((1,), (1,)), ((), ())), preferred_element_type=F32)


def _tn(a, b):
    return lax.dot_general(a, b, (((0,), (0,)), ((), ())), preferred_element_type=F32)


def _nn(a, b):
    return jnp.dot(a, b, preferred_element_type=F32)


def _silu_parts(a):
    sig = jax.nn.sigmoid(a)
    return sig, a * sig


def _dsilu(a, sig):
    return sig * (1.0 + a * (1.0 - sig))


def _norm_mod(x, g, shift, scale):
    r = lax.rsqrt(jnp.mean(x * x, axis=-1, keepdims=True) + EPS)
    return (x * r * g) * (1.0 + scale) + shift


def _norm_mod_bwd(x, dh, g, scale):
    r = lax.rsqrt(jnp.mean(x * x, axis=-1, keepdims=True) + EPS)
    xhat = x * r
    dn = dh * (1.0 + scale)
    dshift = jnp.sum(dh, axis=0, keepdims=True)
    dscale = jnp.sum(dh * (xhat * g), axis=0, keepdims=True)
    dg = jnp.sum(dn * xhat, axis=0, keepdims=True)
    dxh = dn * g
    dx = r * (dxh - xhat * jnp.mean(dxh * xhat, axis=-1, keepdims=True))
    return dx, dshift, dscale, dg


def _group_index(i, tiles_first, n_groups):
    if n_groups == 1:
        return 0
    return jnp.where(i >= tiles_first, 1, 0)


def _mesh_pos():
    x, y, c = lax.axis_index("x"), lax.axis_index("y"), lax.axis_index("c")
    chips = [(1 - x, y), (x, 1 - y), (1 - x, 1 - y)]
    return x, y, c, chips


def _flat(px, py, pc):
    return 4 * px + 2 * py + pc


def all_gather(shards, name):
    n = len(shards)

    def body(*refs):
        ins, outs = refs[:n], refs[n:2 * n]
        send_sems, recv_sems, local_sems = refs[2 * n:]
        x, y, c, chips = _mesh_pos()
        me, sibling = (x, y, c), (x, y, 1 - c)

        def copy(t, k, block, to, src=None):
            dst = outs[t].at[_flat(*block)]
            return pltpu.make_async_remote_copy(
                src_ref=dst if src is None else src, dst_ref=dst,
                send_sem=send_sems.at[t, k], recv_sem=recv_sems.at[t, k],
                device_id=to, device_id_type=MESH)

        mine = [pltpu.make_async_copy(ins[t], outs[t].at[_flat(*me)], local_sems.at[t]) for t in range(n)]
        for cp in mine:
            cp.start()
        first = []
        for t in range(n):
            first.append(copy(t, 0, me, sibling, src=ins[t]))
            first += [copy(t, 1 + j, me, (*chip, c), src=ins[t]) for j, chip in enumerate(chips)]
        for cp in first:
            cp.start()
        passed = []
        for t in range(n):
            for j, chip in enumerate(chips):
                copy(t, 1 + j, (*chip, c), me).wait_recv()
                fwd = copy(t, 4 + j, (*chip, c), sibling)
                fwd.start()
                passed.append(fwd)
        for t in range(n):
            copy(t, 0, sibling, me).wait_recv()
            for j, chip in enumerate(chips):
                copy(t, 4 + j, (*chip, 1 - c), me).wait_recv()
        for cp in first + passed:
            cp.wait_send()
        for cp in mine:
            cp.wait()

    return pl.pallas_call(
        body, name=name,
        out_shape=[jax.ShapeDtypeStruct((N_DEV,) + s.shape, s.dtype) for s in shards],
        in_specs=[ANY] * n, out_specs=[ANY] * n,
        scratch_shapes=[pltpu.SemaphoreType.DMA((n, 7)), pltpu.SemaphoreType.DMA((n, 7)), pltpu.SemaphoreType.DMA((n,))],
    )(*shards)


HBM_SPEC = pl.BlockSpec(memory_space=pltpu.HBM)
SEM_SPEC = pl.BlockSpec(memory_space=pltpu.SEMAPHORE)
DATAFLOW = pltpu.SideEffectType.DATAFLOW_SIDE_EFFECTING


def _in_hbm(a):
    return pltpu.with_memory_space_constraint(a, pltpu.HBM)


def _push_peers():
    x, y, c, chips = _mesh_pos()
    return [(*chip, c) for chip in chips] + [(x, y, 1 - c)]


def cast_place(w, layer, own_block, name):
    _, R, C = w.shape
    tr = _row_tile(R, C)

    def body(idx_ref, w_ref, o_ref):
        o_ref[...] = w_ref[...].astype(BF16)

    return pl.pallas_call(
        body, name=name,
        grid_spec=pltpu.PrefetchScalarGridSpec(
            num_scalar_prefetch=1, grid=(R // tr,),
            in_specs=[pl.BlockSpec((None, tr, C), lambda i, idx: (layer, i, 0))],
            out_specs=pl.BlockSpec((None, tr, C), lambda i, idx: (idx[0], i, 0))),
        out_shape=jax.ShapeDtypeStruct((N_DEV, R, C), BF16),
        compiler_params=_params(("arbitrary",)),
    )(own_block, w)


def gather_start(lands, after, name):
    n = len(lands)

    def body(*refs):
        lz = refs[:n]
        send_sems, recv_sems = refs[n + 1], refs[n + 2]
        token = refs[-1]
        x, y, c, _ = _mesh_pos()
        for t in range(n):
            mine = lz[t].at[_flat(x, y, c)]
            for k, peer in enumerate(_push_peers()):
                pltpu.make_async_remote_copy(
                    src_ref=mine, dst_ref=mine, send_sem=send_sems.at[4 * t + k],
                    recv_sem=recv_sems.at[4 * t + k], device_id=peer, device_id_type=MESH).start()
        token[...] = jnp.zeros_like(token)

    return pl.pallas_call(
        body, name=name,
        out_shape=(pltpu.SemaphoreType.DMA((4 * n,)), pltpu.SemaphoreType.DMA((4 * n,)),
                   *[pltpu.HBM(l.shape, l.dtype) for l in lands], jax.ShapeDtypeStruct((8, LANE), F32)),
        in_specs=[HBM_SPEC] * n + [ANY],
        out_specs=(SEM_SPEC, SEM_SPEC, *[HBM_SPEC] * n, pl.BlockSpec(memory_space=pltpu.VMEM)),
        input_output_aliases={i: 2 + i for i in range(n)},
        compiler_params=pltpu.CompilerParams(has_side_effects=DATAFLOW),
    )(*[_in_hbm(l) for l in lands], after)


def gather_wait(handle, after, name):
    send, recv = handle[0], handle[1]
    bufs = handle[2:-1]
    n = len(bufs)

    def body(*refs):
        lz = refs[:n]
        send_sems, recv_sems = refs[n], refs[n + 1]
        x, y, c, _ = _mesh_pos()
        for t in range(n):
            for k, peer in enumerate(_push_peers()):
                cp = pltpu.make_async_remote_copy(
                    src_ref=lz[t].at[_flat(x, y, c)], dst_ref=lz[t].at[_flat(*peer)], send_sem=send_sems.at[4 * t + k],
                    recv_sem=recv_sems.at[4 * t + k], device_id=peer, device_id_type=MESH)
                cp.wait_send()
                cp.wait_recv()

    outs = pl.pallas_call(
        body, name=name,
        out_shape=tuple(pltpu.HBM(b.shape, b.dtype) for b in bufs),
        in_specs=[HBM_SPEC] * n + [SEM_SPEC, SEM_SPEC, ANY],
        out_specs=tuple([HBM_SPEC] * n),
        input_output_aliases={i: i for i in range(n)},
        compiler_params=pltpu.CompilerParams(has_side_effects=DATAFLOW),
    )(*bufs, send, recv, after)
    return list(outs)


def gather_finish(lands, name):
    n = len(lands)

    def body(*refs):
        outs = refs[n:2 * n]
        send_sems, recv_sems = refs[2 * n:]
        x, y, c, chips = _mesh_pos()
        sibling = (x, y, 1 - c)

        def copy(t, j, core):
            blk = outs[t].at[_flat(*chips[j], core)]
            return pltpu.make_async_remote_copy(
                src_ref=blk, dst_ref=blk, send_sem=send_sems.at[t, j], recv_sem=recv_sems.at[t, j],
                device_id=sibling, device_id_type=MESH)

        sends = [copy(t, j, c) for t in range(n) for j in range(3)]
        for cp in sends:
            cp.start()
        for t in range(n):
            for j in range(3):
                copy(t, j, 1 - c).wait_recv()
        for cp in sends:
            cp.wait_send()

    return pl.pallas_call(
        body, name=name,
        out_shape=[jax.ShapeDtypeStruct(l.shape, l.dtype) for l in lands],
        in_specs=[ANY] * n, out_specs=[ANY] * n,
        input_output_aliases={t: t for t in range(n)},
        scratch_shapes=[pltpu.SemaphoreType.DMA((n, 3)), pltpu.SemaphoreType.DMA((n, 3))],
    )(*lands)


def chip_exchange_start(sums, name):
    n = len(sums)
    lands = [lax.empty(s.shape, s.dtype) for s in sums]

    def body(*refs):
        ins, lz = refs[:n], refs[n:2 * n]
        send_sems, recv_sems = refs[2 * n], refs[2 * n + 1]
        token = refs[-1]
        peers = _push_peers()
        for t in range(n):
            for j in range(3):
                pltpu.make_async_remote_copy(
                    src_ref=ins[t].at[j], dst_ref=lz[t].at[j], send_sem=send_sems.at[3 * t + j],
                    recv_sem=recv_sems.at[3 * t + j], device_id=peers[j], device_id_type=MESH).start()
        token[...] = jnp.zeros_like(token)

    return pl.pallas_call(
        body, name=name,
        out_shape=(pltpu.SemaphoreType.DMA((3 * n,)), pltpu.SemaphoreType.DMA((3 * n,)),
                   *[pltpu.HBM(s.shape, s.dtype) for s in sums], *[pltpu.HBM(s.shape, s.dtype) for s in sums],
                   jax.ShapeDtypeStruct((8, LANE), F32)),
        in_specs=[HBM_SPEC] * (2 * n),
        out_specs=(SEM_SPEC, SEM_SPEC, *[HBM_SPEC] * (2 * n), pl.BlockSpec(memory_space=pltpu.VMEM)),
        input_output_aliases={i: 2 + i for i in range(2 * n)},
        compiler_params=pltpu.CompilerParams(has_side_effects=DATAFLOW),
    )(*[_in_hbm(s) for s in sums], *[_in_hbm(l) for l in lands])


def chip_exchange_wait(handle, after, name):
    send, recv = handle[0], handle[1]
    n = (len(handle) - 3) // 2
    bufs = handle[2:2 + 2 * n]

    def body(*refs):
        ins, lz = refs[:n], refs[n:2 * n]
        send_sems, recv_sems = refs[2 * n], refs[2 * n + 1]
        peers = _push_peers()
        for t in range(n):
            for j in range(3):
                cp = pltpu.make_async_remote_copy(
                    src_ref=ins[t].at[j], dst_ref=lz[t].at[j], send_sem=send_sems.at[3 * t + j],
                    recv_sem=recv_sems.at[3 * t + j], device_id=peers[j], device_id_type=MESH)
                cp.wait_send()
                cp.wait_recv()

    outs = pl.pallas_call(
        body, name=name,
        out_shape=tuple(pltpu.HBM(b.shape, b.dtype) for b in bufs),
        in_specs=[HBM_SPEC] * (2 * n) + [SEM_SPEC, SEM_SPEC, ANY],
        out_specs=tuple([HBM_SPEC] * (2 * n)),
        input_output_aliases={i: i for i in range(2 * n)},
        compiler_params=pltpu.CompilerParams(has_side_effects=DATAFLOW),
    )(*bufs, send, recv, after)
    return list(outs[n:])


def rs_sibling_exchange(grads, after, name):
    n = len(grads)

    def body(*refs):
        ins, stages, directs = refs[:n], refs[n + 1:2 * n + 1], refs[2 * n + 1:3 * n + 1]
        send_sems, recv_sems = refs[3 * n + 1:]
        x, y, c, chips = _mesh_pos()
        sibling = (x, y, 1 - c)

        def copy(t, k):
            if k < 3:
                src, dst = ins[t].at[_flat(*chips[k], 1 - c)], stages[t].at[k]
            else:
                src, dst = ins[t].at[_flat(x, y, 1 - c)], directs[t]
            return pltpu.make_async_remote_copy(
                src_ref=src, dst_ref=dst, send_sem=send_sems.at[t, k], recv_sem=recv_sems.at[t, k],
                device_id=sibling, device_id_type=MESH)

        cps = [copy(t, k) for t in range(n) for k in range(4)]
        for cp in cps:
            cp.start()
        for cp in cps:
            cp.wait_recv()
        for cp in cps:
            cp.wait_send()

    return pl.pallas_call(
        body, name=name,
        out_shape=[jax.ShapeDtypeStruct((3,) + g.shape[1:], g.dtype) for g in grads]
        + [jax.ShapeDtypeStruct(g.shape[1:], g.dtype) for g in grads],
        in_specs=[ANY] * (n + 1), out_specs=[ANY] * (2 * n),
        scratch_shapes=[pltpu.SemaphoreType.DMA((n, 4)), pltpu.SemaphoreType.DMA((n, 4))],
    )(*grads, after)


def _row_tile(rows, cols, limit_bytes=3 << 19):
    best = None
    for t in range(16, rows + 1, 16):
        if rows % t == 0 and t * cols * 4 <= limit_bytes:
            best = t
    return best if best is not None else rows


def chip_sum(grad, stage, peer_blocks, name):
    _, R, C = grad.shape
    tr = _row_tile(R, C)

    def body(idx_ref, g_ref, s_ref, o_ref):
        o_ref[...] = (g_ref[...].astype(F32) + s_ref[...].astype(F32)).astype(BF16)

    return pl.pallas_call(
        body, name=name,
        grid_spec=pltpu.PrefetchScalarGridSpec(
            num_scalar_prefetch=1, grid=(3, R // tr),
            in_specs=[pl.BlockSpec((None, tr, C), lambda j, i, idx: (idx[j], i, 0)),
                      pl.BlockSpec((None, tr, C), lambda j, i, idx: (j, i, 0))],
            out_specs=pl.BlockSpec((None, tr, C), lambda j, i, idx: (j, i, 0))),
        out_shape=jax.ShapeDtypeStruct((3, R, C), BF16),
        compiler_params=_params(("arbitrary", "arbitrary")),
    )(peer_blocks, grad, stage)


def _adamw(w, g, m, v):
    m2 = ADAM_B1 * m + (1.0 - ADAM_B1) * g
    v2 = ADAM_B2 * v + (1.0 - ADAM_B2) * (g * g)
    m_hat = m2 / (1.0 - ADAM_B1 ** ADAM_STEP)
    v_hat = v2 / (1.0 - ADAM_B2 ** ADAM_STEP)
    delta = -ADAM_LR * (m_hat / (jnp.sqrt(v_hat) + ADAM_EPS) + ADAM_WD * w)
    return delta, m2, v2


def adam_reduced(own_block, grad, direct, landed, w, m, v, layer, prev, name):
    L, R, C = w.shape
    tr = _row_tile(R, C, 1 << 20)
    first = prev is None

    def body(idx_ref, g_ref, d_ref, l_ref, w_ref, m_ref, v_ref, *rest):
        og, od, om, ov = rest[-4:]
        g = g_ref[...].astype(F32) + d_ref[...].astype(F32)
        for j in range(3):
            g = g + l_ref[j].astype(F32)
        delta, m2, v2 = _adamw(w_ref[...], g, m_ref[...], v_ref[...])
        og[...] = g
        od[...] = delta
        om[...] = m2
        ov[...] = v2

    lay = pl.BlockSpec((None, tr, C), lambda i, idx: (layer, i, 0))
    in_specs = [pl.BlockSpec((None, tr, C), lambda i, idx: (idx[0], i, 0)),
                pl.BlockSpec((tr, C), lambda i, idx: (i, 0)),
                pl.BlockSpec((3, tr, C), lambda i, idx: (0, i, 0)),
                lay, lay, lay]
    args = [own_block, grad, direct, landed, w, m, v]
    aliases = {}
    if not first:
        in_specs += [ANY] * 4
        args += list(prev)
        aliases = {7 + k: k for k in range(4)}
    return pl.pallas_call(
        body, name=name,
        grid_spec=pltpu.PrefetchScalarGridSpec(
            num_scalar_prefetch=1, grid=(R // tr,), in_specs=in_specs, out_specs=[lay] * 4),
        out_shape=[jax.ShapeDtypeStruct((L, R, C), F32)] * 4,
        input_output_aliases=aliases,
        compiler_params=_params(("arbitrary",)),
    )(*args)


def sum_partials(parts, name):
    _, _, N = parts.shape
    tn = 8192

    def body(p_ref, o_ref):
        g = p_ref[0]
        for k in range(1, N_DEV):
            g = g + p_ref[k]
        o_ref[...] = g

    return pl.pallas_call(
        body, name=name, grid=(N // tn,),
        in_specs=[pl.BlockSpec((N_DEV, 1, tn), lambda i: (0, 0, i))],
        out_specs=pl.BlockSpec((1, tn), lambda i: (0, i)),
        out_shape=jax.ShapeDtypeStruct((1, N), F32),
    )(parts)


def adam_plain(g, w, m, v, name):
    _, N = g.shape

    def body(g_ref, w_ref, m_ref, v_ref, od, om, ov):
        delta, m2, v2 = _adamw(w_ref[...], g_ref[...], m_ref[...], v_ref[...])
        od[...] = delta
        om[...] = m2
        ov[...] = v2

    return pl.pallas_call(
        body, name=name, out_shape=[jax.ShapeDtypeStruct((1, N), F32)] * 3,
    )(g, w, m, v)


def adaln_fwd(cond16, w_mod, b_blk, name):
    L, D, W = w_mod.shape
    tn = 768 if W % 768 == 0 else W

    def body(c_ref, w_ref, b_ref, o_ref):
        c = c_ref[...]
        sc = (c * jax.nn.sigmoid(c)).astype(BF16)
        o_ref[...] = _nn(sc, w_ref[...].astype(BF16)) + b_ref[...]

    return pl.pallas_call(
        body, name=name, grid=(L, W // tn),
        in_specs=[pl.BlockSpec((16, D), lambda l, i: (0, 0)),
                  pl.BlockSpec((None, D, tn), lambda l, i: (l, 0, i)),
                  pl.BlockSpec((None, 1, tn), lambda l, i: (l, 0, i))],
        out_specs=pl.BlockSpec((None, 16, tn), lambda l, i: (l, 0, i)),
        out_shape=jax.ShapeDtypeStruct((L, 16, W), F32),
        compiler_params=_params(("arbitrary", "arbitrary")),
    )(cond16, w_mod, b_blk)


def adaln_bwd(cond16, dm_x, dm_c, w_mod, m_mod, v_mod, name):
    L, D, W = w_mod.shape
    tn = 256 if W % 256 == 0 else W
    nt = W // tn

    def body(c_ref, cT_ref, dx_ref, dc_ref, w_ref, m_ref, v_ref, og, od, om, ov, pc_ref):
        l, i = pl.program_id(0), pl.program_id(1)
        c = c_ref[...]
        sig, sl = _silu_parts(c)
        cT = cT_ref[...]
        sigT = jax.nn.sigmoid(cT)
        scT = (cT * sigT).astype(BF16)
        dmc = jnp.sum(dc_ref[...], axis=0, keepdims=True)
        dm16 = jnp.concatenate([dx_ref[...], jnp.broadcast_to(dmc, (8, tn))], axis=0)
        row = lax.broadcasted_iota(jnp.int32, (16, tn), 0)
        dm16 = jnp.where(row <= 8, dm16, 0.0).astype(BF16)
        w = w_ref[...]
        g = _nn(scT, dm16)
        delta, m2, v2 = _adamw(w, g, m_ref[...], v_ref[...])
        og[...] = g
        od[...] = delta
        om[...] = m2
        ov[...] = v2

        @pl.when((l == 0) & (i == 0))
        def _():
            pc_ref[...] = jnp.zeros_like(pc_ref)

        @pl.when(l == 0)
        def _():
            back = _nt(jnp.broadcast_to(dmc, (8, tn)).astype(BF16), w.astype(BF16))
            pc_ref[...] += back * _dsilu(c[8:9, :], sig[8:9, :])

    col = pl.BlockSpec((None, D, tn), lambda l, i: (l, 0, i))
    row8 = pl.BlockSpec((None, 8, tn), lambda l, i: (l, 0, i))
    return pl.pallas_call(
        body, name=name, grid=(L, nt),
        in_specs=[pl.BlockSpec((16, D), lambda l, i: (0, 0)), pl.BlockSpec((D, 16), lambda l, i: (0, 0)),
                  row8, row8, col, col, col],
        out_specs=[col, col, col, col, pl.BlockSpec((8, D), lambda l, i: (0, 0))],
        out_shape=[jax.ShapeDtypeStruct((L, D, W), F32)] * 4 + [jax.ShapeDtypeStruct((8, D), F32)],
        compiler_params=_params(("arbitrary", "arbitrary")),
    )(cond16, cond16.T, dm_x, dm_c, w_mod, m_mod, v_mod)


def _token_spec(tm, D):
    if tm > 256:
        return pl.BlockSpec((tm, D), lambda i, j: (i, 0), pipeline_mode=pl.Buffered(1))
    return pl.BlockSpec((tm, D), lambda i, j: (i, 0))


def ffn_fwd(x, mod, gn, wg, wu, wd, *, tm, t_first, name):
    T, D = x.shape
    nb, cw, _ = wg.shape
    G = mod.shape[0]
    nt = T // tm
    first = t_first // tm

    def body(x_ref, mod_ref, g_ref, wg_ref, wu_ref, wd_ref, xo_ref, h_ref, a_ref, b_ref, y_ref, hs, acc, a_s, b_s, s_s):
        j = pl.program_id(1)

        @pl.when(j == 0)
        def _():
            def head(r, carry):
                rows = pl.ds(pl.multiple_of(r * ROW_CHUNK, ROW_CHUNK), ROW_CHUNK)
                hb = _norm_mod(x_ref[rows, :], g_ref[...], mod_ref[0:1, :], mod_ref[1:2, :]).astype(BF16)
                hs[rows, :] = hb
                h_ref[rows, :] = hb
                return carry

            lax.fori_loop(0, tm // ROW_CHUNK, head, 0)
            acc[...] = jnp.zeros_like(acc)

        a_s[...] = _nt(hs[...], wg_ref[...])
        b_s[...] = _nt(hs[...], wu_ref[...])

        def chunk(r, carry):
            rows = pl.ds(pl.multiple_of(r * ROW_CHUNK, ROW_CHUNK), ROW_CHUNK)
            av, bv = a_s[rows, :], b_s[rows, :]
            a_ref[rows, :] = av.astype(BF16)
            b_ref[rows, :] = bv.astype(BF16)
            _, sl = _silu_parts(av)
            s_s[rows, :] = (sl * bv).astype(BF16)
            return carry

        lax.fori_loop(0, tm // ROW_CHUNK, chunk, 0)
        acc[...] += _nn(s_s[...], wd_ref[...])

        @pl.when(j == nb - 1)
        def _():
            def tail(r, carry):
                rows = pl.ds(pl.multiple_of(r * ROW_CHUNK, ROW_CHUNK), ROW_CHUNK)
                y = acc[rows, :]
                y_ref[rows, :] = y.astype(BF16)
                xo_ref[rows, :] = x_ref[rows, :] + (MACARON * mod_ref[2:3, :]) * y
                return carry

            lax.fori_loop(0, tm // ROW_CHUNK, tail, 0)

    tok = _token_spec(tm, D)
    act = pl.BlockSpec((None, tm, cw), lambda i, j: (j, i, 0))
    wblk = pl.BlockSpec((None, cw, D), lambda i, j: (j, 0, 0))
    return pl.pallas_call(
        body, name=name, grid=(nt, nb),
        in_specs=[tok, pl.BlockSpec((None, 3, D), lambda i, j: (_group_index(i, first, G), 0, 0)),
                  pl.BlockSpec((1, D), lambda i, j: (0, 0)), wblk, wblk, wblk],
        out_specs=[tok, tok, act, act, tok],
        out_shape=[jax.ShapeDtypeStruct((T, D), F32), jax.ShapeDtypeStruct((T, D), BF16),
                   jax.ShapeDtypeStruct((nb, T, cw), BF16), jax.ShapeDtypeStruct((nb, T, cw), BF16),
                   jax.ShapeDtypeStruct((T, D), BF16)],
        scratch_shapes=[pltpu.VMEM((tm, D), BF16), pltpu.VMEM((tm, D), F32), pltpu.VMEM((tm, cw), F32),
                        pltpu.VMEM((tm, cw), F32), pltpu.VMEM((tm, cw), BF16)],
        compiler_params=_params(("arbitrary", "arbitrary")),
    )(x, mod, gn, wg, wu, wd)


def ffn_bwd_tok(x, dxo, y, a, b, mod, gn, wg, wu, wd, *, tm, t_first, name):
    T, D = x.shape
    nb, cw, _ = wg.shape
    G = mod.shape[0]
    nt = T // tm
    first = t_first // tm

    def body(x_ref, dxo_ref, y_ref, a_ref, b_ref, mod_ref, g_ref, wg_ref, wu_ref, wd_ref,
             dx_ref, da_ref, db_ref, s_ref, dy_ref, red_ref, dys, dh, ds_s):
        i, j = pl.program_id(0), pl.program_id(1)

        @pl.when(j == 0)
        def _():
            def head(r, carry):
                rows = pl.ds(pl.multiple_of(r * ROW_CHUNK, ROW_CHUNK), ROW_CHUNK)
                dyb = ((MACARON * mod_ref[2:3, :]) * dxo_ref[rows, :]).astype(BF16)
                dys[rows, :] = dyb
                dy_ref[rows, :] = dyb
                return carry

            lax.fori_loop(0, tm // ROW_CHUNK, head, 0)
            dh[...] = jnp.zeros_like(dh)

        ds_s[...] = _nt(dys[...], wd_ref[...])

        def chunk(r, carry):
            rows = pl.ds(pl.multiple_of(r * ROW_CHUNK, ROW_CHUNK), ROW_CHUNK)
            av = a_ref[rows, :].astype(F32)
            bv = b_ref[rows, :].astype(F32)
            ds = ds_s[rows, :]
            sig, sl = _silu_parts(av)
            s_ref[rows, :] = (sl * bv).astype(BF16)
            da_ref[rows, :] = (ds * bv * _dsilu(av, sig)).astype(BF16)
            db_ref[rows, :] = (ds * sl).astype(BF16)
            return carry

        lax.fori_loop(0, tm // ROW_CHUNK, chunk, 0)
        dh[...] += _nn(da_ref[...], wg_ref[...]) + _nn(db_ref[...], wu_ref[...])

        @pl.when((j == 0) & ((i == 0) | (i == first)))
        def _():
            red_ref[...] = jnp.zeros_like(red_ref)

        @pl.when(j == nb - 1)
        def _():
            def tail(r, carry):
                rows = pl.ds(pl.multiple_of(r * ROW_CHUNK, ROW_CHUNK), ROW_CHUNK)
                dxo_v = dxo_ref[rows, :]
                dxn, dshift, dscale, dg = _norm_mod_bwd(x_ref[rows, :], dh[rows, :], g_ref[...], mod_ref[1:2, :])
                dx_ref[rows, :] = dxo_v + dxn
                red_ref[0:1, :] += dshift
                red_ref[1:2, :] += dscale
                red_ref[2:3, :] += jnp.sum((MACARON * dxo_v) * y_ref[rows, :].astype(F32), axis=0, keepdims=True)
                red_ref[3:4, :] += dg
                return carry

            lax.fori_loop(0, tm // ROW_CHUNK, tail, 0)

    tok = _token_spec(tm, D)
    act = pl.BlockSpec((None, tm, cw), lambda i, j: (j, i, 0))
    wblk = pl.BlockSpec((None, cw, D), lambda i, j: (j, 0, 0))
    return pl.pallas_call(
        body, name=name, grid=(nt, nb),
        in_specs=[tok, tok, tok, act, act,
                  pl.BlockSpec((None, 3, D), lambda i, j: (_group_index(i, first, G), 0, 0)),
                  pl.BlockSpec((1, D), lambda i, j: (0, 0)), wblk, wblk, wblk],
        out_specs=[tok, act, act, act, tok,
                   pl.BlockSpec((None, 8, D), lambda i, j: (_group_index(i, first, G), 0, 0))],
        out_shape=[jax.ShapeDtypeStruct((T, D), F32)] + [jax.ShapeDtypeStruct((nb, T, cw), BF16)] * 3
        + [jax.ShapeDtypeStruct((T, D), BF16), jax.ShapeDtypeStruct((G, 8, D), F32)],
        scratch_shapes=[pltpu.VMEM((tm, D), BF16), pltpu.VMEM((tm, D), F32), pltpu.VMEM((tm, cw), F32)],
        compiler_params=_params(("arbitrary", "arbitrary")),
    )(x, dxo, y, a, b, mod, gn, wg, wu, wd)


def matmul_tn(a, b, *, nb, m, n, a_blocked, b_blocked, tk, name, after=None):
    T = a.shape[-2]
    extra = [] if after is None else [after]

    def spec(arr, blocked, width):
        if blocked:
            return pl.BlockSpec((None, tk, width), lambda j, k: (j, k, 0))
        if arr.shape[-1] == width:
            return pl.BlockSpec((tk, width), lambda j, k: (k, 0))
        return pl.BlockSpec((tk, width), lambda j, k: (k, j))

    def body(a_ref, b_ref, *rest):
        o_ref, acc = rest[-2:]
        k = pl.program_id(1)

        @pl.when(k == 0)
        def _():
            acc[...] = jnp.zeros_like(acc)

        acc[...] += _tn(a_ref[...], b_ref[...])

        @pl.when(k == T // tk - 1)
        def _():
            o_ref[...] = acc[...].astype(BF16)

    return pl.pallas_call(
        body, name=name, grid=(nb, T // tk),
        in_specs=[spec(a, a_blocked, m), spec(b, b_blocked, n)] + [ANY] * len(extra),
        out_specs=pl.BlockSpec((None, m, n), lambda j, k: (j, 0, 0)),
        out_shape=jax.ShapeDtypeStruct((nb, m, n), BF16),
        scratch_shapes=[pltpu.VMEM((m, n), F32)],
        compiler_params=_params(("arbitrary", "arbitrary")),
    )(a, b, *extra)


def proj_in(x, mod, gn, w_in, *, tm, t_first, name):
    T, D = x.shape
    nb, _, cw = w_in.shape
    G = mod.shape[0]
    nt = T // tm
    first = t_first // tm

    def body(x_ref, mod_ref, g_ref, w_ref, p_ref, h_ref, hs):
        @pl.when(pl.program_id(1) == 0)
        def _():
            hb = _norm_mod(x_ref[...], g_ref[...], mod_ref[0:1, :], mod_ref[1:2, :]).astype(BF16)
            hs[...] = hb
            h_ref[...] = hb

        p_ref[...] = _nn(hs[...], w_ref[...])

    tok = pl.BlockSpec((tm, D), lambda i, j: (i, 0))
    return pl.pallas_call(
        body, name=name, grid=(nt, nb),
        in_specs=[tok, pl.BlockSpec((None, 3, D), lambda i, j: (_group_index(i, first, G), 0, 0)),
                  pl.BlockSpec((1, D), lambda i, j: (0, 0)),
                  pl.BlockSpec((None, D, cw), lambda i, j: (j, 0, 0))],
        out_specs=[pl.BlockSpec((tm, cw), lambda i, j: (i, j)), tok],
        out_shape=[jax.ShapeDtypeStruct((T, nb * cw), F32), jax.ShapeDtypeStruct((T, D), BF16)],
        scratch_shapes=[pltpu.VMEM((tm, D), BF16)],
        compiler_params=_params(("arbitrary", "arbitrary")),
    )(x, mod, gn, w_in)


def proj_bwd_tok(x, dxo, dp, mod, gn, w_in, *, tm, name):
    T, D = x.shape
    nb, _, cw = w_in.shape
    G = mod.shape[0]
    nt = T // tm
    first = dxo.shape[0] // tm

    def body(x_ref, dxo_ref, dp_ref, mod_ref, g_ref, w_ref, dx_ref, red_ref, dh):
        i, j = pl.program_id(0), pl.program_id(1)

        @pl.when(j == 0)
        def _():
            dh[...] = jnp.zeros_like(dh)

        dh[...] += _nt(dp_ref[...], w_ref[...])

        @pl.when((j == 0) & ((i == 0) | (i == first)))
        def _():
            red_ref[...] = jnp.zeros_like(red_ref)

        @pl.when(j == nb - 1)
        def _():
            dxn, dshift, dscale, dg = _norm_mod_bwd(x_ref[...], dh[...], g_ref[...], mod_ref[1:2, :])
            dx_ref[...] = jnp.where(i < first, dxo_ref[...], 0.0) + dxn
            red_ref[0:1, :] += dshift
            red_ref[1:2, :] += dscale
            red_ref[3:4, :] += dg

    tok = pl.BlockSpec((tm, D), lambda i, j: (i, 0))
    return pl.pallas_call(
        body, name=name, grid=(nt, nb),
        in_specs=[tok, pl.BlockSpec((tm, D), lambda i, j: (jnp.minimum(i, first - 1), 0)),
                  pl.BlockSpec((tm, cw), lambda i, j: (i, j)),
                  pl.BlockSpec((None, 3, D), lambda i, j: (_group_index(i, first, G), 0, 0)),
                  pl.BlockSpec((1, D), lambda i, j: (0, 0)),
                  pl.BlockSpec((None, D, cw), lambda i, j: (j, 0, 0))],
        out_specs=[tok, pl.BlockSpec((None, 8, D), lambda i, j: (_group_index(i, first, G), 0, 0))],
        out_shape=[jax.ShapeDtypeStruct((T, D), F32), jax.ShapeDtypeStruct((G, 8, D), F32)],
        scratch_shapes=[pltpu.VMEM((tm, D), F32)],
        compiler_params=_params(("arbitrary", "arbitrary")),
    )(x, dxo, dp, mod, gn, w_in)


def _rope(t, cos, sin):
    return t * cos + pltpu.roll(t, HEAD_DIM // 2, axis=1) * sin


def ret_prep(p, cos, sin, *, heads, tm, name):
    T = p.shape[0]

    def body(q_ref, k_ref, v_ref, c_ref, s_ref, qo, ko, vo):
        cos_v, sin_v = c_ref[...], s_ref[...]
        qo[...] = _rope(q_ref[...], cos_v, sin_v).astype(BF16)
        ko[...] = _rope(k_ref[...] * K_SCALE, cos_v, sin_v).astype(BF16)
        vo[...] = v_ref[...].astype(BF16)

    def col(part):
        return pl.BlockSpec((tm, HEAD_DIM), lambda h, i: (i, part * heads + h))

    tab = pl.BlockSpec((tm, HEAD_DIM), lambda h, i: (i, 0))
    out = pl.BlockSpec((None, tm, HEAD_DIM), lambda h, i: (h, i, 0))
    return pl.pallas_call(
        body, name=name, grid=(heads, T // tm),
        in_specs=[col(0), col(1), col(2), tab, tab], out_specs=[out, out, out],
        out_shape=[jax.ShapeDtypeStruct((heads, T, HEAD_DIM), BF16)] * 3,
        compiler_params=_params(("arbitrary", "arbitrary")),
    )(p, p, p, cos, sin)


def _decay(n, m, lgf, lgb, t_lat, t_ctx):
    df = jnp.where(m < t_lat, n - m, n - m + (t_lat + t_ctx))
    db = m - n
    ef = jnp.where(df >= 0, jnp.exp(lgf * df), 0.0)
    eb = jnp.where(db >= 0, jnp.exp(lgb * db), 0.0)
    return ef, eb, df, db


def ret_fwd(q, k, v, lg, *, t_lat, tq, tk, name):
    H, T, _ = k.shape
    t_ctx = T - t_lat

    def body(lg_ref, q_ref, k_ref, v_ref, o_ref):
        h, qi, kj = pl.program_id(0), pl.program_id(1), pl.program_id(2)

        @pl.when(kj == 0)
        def _():
            o_ref[...] = jnp.zeros_like(o_ref)

        s = _nt(q_ref[...], k_ref[...])
        n = (qi * tq + lax.broadcasted_iota(jnp.int32, (tq, tk), 0)).astype(F32)
        m = (kj * tk + lax.broadcasted_iota(jnp.int32, (tq, tk), 1)).astype(F32)
        ef, eb, _, _ = _decay(n, m, lg_ref[0, h], lg_ref[1, h], t_lat, t_ctx)
        o_ref[...] += _nn((s * (ef + eb)).astype(BF16), v_ref[...])

    return pl.pallas_call(
        body, name=name, grid=(H, t_lat // tq, T // tk),
        in_specs=[pl.BlockSpec(memory_space=pltpu.SMEM),
                  pl.BlockSpec((None, tq, HEAD_DIM), lambda h, i, j: (h, i, 0)),
                  pl.BlockSpec((None, tk, HEAD_DIM), lambda h, i, j: (h, j, 0)),
                  pl.BlockSpec((None, tk, HEAD_DIM), lambda h, i, j: (h, j, 0))],
        out_specs=pl.BlockSpec((None, tq, HEAD_DIM), lambda h, i, j: (h, i, 0)),
        out_shape=jax.ShapeDtypeStruct((H, t_lat, HEAD_DIM), F32),
        compiler_params=_params(("arbitrary", "arbitrary", "arbitrary")),
    )(lg, q, k, v)


def ret_bwd(q, k, v, do, lg, *, t_lat, tq, tk, name):
    H, T, _ = k.shape
    t_ctx = T - t_lat

    def body(lg_ref, q_ref, k_ref, v_ref, do_ref, dq_ref, dk_ref, dv_ref, dlg_ref):
        h, kj, qi = pl.program_id(0), pl.program_id(1), pl.program_id(2)

        @pl.when((kj == 0) & (qi == 0))
        def _():
            dq_ref[...] = jnp.zeros_like(dq_ref)
            dlg_ref[...] = jnp.zeros_like(dlg_ref)

        @pl.when(qi == 0)
        def _():
            dk_ref[...] = jnp.zeros_like(dk_ref)
            dv_ref[...] = jnp.zeros_like(dv_ref)

        qv, kv, vv = q_ref[...], k_ref[...], v_ref[...]
        dob = do_ref[...].astype(BF16)
        st = _nt(kv, qv)
        dwt = _nt(vv, dob)
        m = (kj * tk + lax.broadcasted_iota(jnp.int32, (tk, tq), 0)).astype(F32)
        n = (qi * tq + lax.broadcasted_iota(jnp.int32, (tk, tq), 1)).astype(F32)
        ef, eb, df, db = _decay(n, m, lg_ref[0, h], lg_ref[1, h], t_lat, t_ctx)
        dec = ef + eb
        dv_ref[...] += _nn((st * dec).astype(BF16), dob)
        dst = (dwt * dec).astype(BF16)
        dk_ref[...] += _nn(dst, qv)
        rows = pl.ds(pl.multiple_of(qi * tq, tq), tq)
        dq_ref[rows, :] += _tn(dst, kv)
        gs = dwt * st
        dlg_ref[0:1, :] += jnp.sum(gs * (ef * df))
        dlg_ref[1:2, :] += jnp.sum(gs * (eb * db))

    kspec = pl.BlockSpec((None, tk, HEAD_DIM), lambda h, j, i: (h, j, 0))
    qspec = pl.BlockSpec((None, tq, HEAD_DIM), lambda h, j, i: (h, i, 0))
    return pl.pallas_call(
        body, name=name, grid=(H, T // tk, t_lat // tq),
        in_specs=[pl.BlockSpec(memory_space=pltpu.SMEM), qspec, kspec, kspec, qspec],
        out_specs=[pl.BlockSpec((None, t_lat, HEAD_DIM), lambda h, j, i: (h, 0, 0)), kspec, kspec,
                   pl.BlockSpec((None, 8, LANE), lambda h, j, i: (h, 0, 0))],
        out_shape=[jax.ShapeDtypeStruct((H, t_lat, HEAD_DIM), F32), jax.ShapeDtypeStruct((H, T, HEAD_DIM), F32),
                   jax.ShapeDtypeStruct((H, T, HEAD_DIM), F32), jax.ShapeDtypeStruct((H, 8, LANE), F32)],
        compiler_params=_params(("arbitrary", "arbitrary", "arbitrary")),
    )(lg, q, k, v, do)


def _group_norm(o):
    mu = jnp.mean(o, axis=-1, keepdims=True)
    ctr = o - mu
    r = lax.rsqrt(jnp.mean(ctr * ctr, axis=-1, keepdims=True) + GN_EPS)
    return ctr * r, r


def ret_post_fwd(o, p, *, heads, d_model, name):
    H, t_lat, _ = o.shape
    T = p.shape[0]

    def body(o_ref, g_ref, z_ref):
        on, _ = _group_norm(o_ref[...])
        _, sl = _silu_parts(g_ref[0:t_lat, :])
        z_ref[...] = (on * sl).astype(BF16)

    return pl.pallas_call(
        body, name=name, grid=(H,),
        in_specs=[pl.BlockSpec((None, t_lat, HEAD_DIM), lambda h: (h, 0, 0)),
                  pl.BlockSpec((T, HEAD_DIM), lambda h: (0, 3 * heads + h))],
        out_specs=pl.BlockSpec((t_lat, HEAD_DIM), lambda h: (0, h)),
        out_shape=jax.ShapeDtypeStruct((t_lat, d_model), BF16),
        compiler_params=_params(("arbitrary",)),
    )(o, p)


def _shift_rows(u, k):
    rows = u.shape[0]
    row = lax.broadcasted_iota(jnp.int32, u.shape, 0)
    rolled = pltpu.roll(u, k % rows, axis=0)
    return jnp.where((row >= k) & (row < rows + k), rolled, 0.0)


def conv_fwd(z, p, w_conv, *, heads, t_lat, name):
    T = p.shape[0]
    cb_n = w_conv.shape[1] // LANE
    base = 4 * heads

    def body(z_in, bg_ref, cg_ref, u_ref, w_ref, z_ref):
        cu = cg_ref[0:t_lat, :] * u_ref[0:t_lat, :]
        c3 = _shift_rows(cu, 1) * w_ref[0:1, :] + cu * w_ref[1:2, :] + _shift_rows(cu, -1) * w_ref[2:3, :]
        z_ref[...] = (bg_ref[0:t_lat, :] * c3).astype(BF16)

    def col(part):
        return pl.BlockSpec((T, LANE), lambda cb: (0, base + part * cb_n + cb))

    return pl.pallas_call(
        body, name=name, grid=(cb_n,),
        in_specs=[ANY, col(0), col(1), col(2), pl.BlockSpec((CONV_WIDTH, LANE), lambda cb: (0, cb))],
        out_specs=pl.BlockSpec((t_lat, LANE), lambda cb: (0, heads + cb)),
        out_shape=jax.ShapeDtypeStruct(z.shape, z.dtype),
        input_output_aliases={0: 0},
        compiler_params=_params(("arbitrary",)),
    )(z, p, p, p, w_conv)


def out_proj(z, w_out, x, mod, *, tm, name):
    t_lat, D = z.shape

    def body(z_ref, w_ref, x_ref, mod_ref, xo_ref, y_ref):
        y = _nn(z_ref[...], w_ref[...])
        y_ref[...] = y.astype(BF16)
        xo_ref[...] = x_ref[...] + mod_ref[2:3, :] * y

    tok = pl.BlockSpec((tm, D), lambda i: (i, 0))
    return pl.pallas_call(
        body, name=name, grid=(t_lat // tm,),
        in_specs=[tok, pl.BlockSpec((D, D), lambda i: (0, 0)), tok, pl.BlockSpec((None, 3, D), lambda i: (0, 0, 0))],
        out_specs=[tok, tok],
        out_shape=[jax.ShapeDtypeStruct((t_lat, D), F32), jax.ShapeDtypeStruct((t_lat, D), BF16)],
        compiler_params=_params(("arbitrary",)),
    )(z, w_out, x, mod)


def out_proj_bwd(dxo, y, w_out, mod, *, tm, name):
    t_lat, D = dxo.shape

    def body(dxo_ref, y_ref, w_ref, mod_ref, dz_ref, dy_ref, red_ref):
        @pl.when(pl.program_id(0) == 0)
        def _():
            red_ref[...] = jnp.zeros_like(red_ref)

        dxo_v = dxo_ref[...]
        dyb = (mod_ref[2:3, :] * dxo_v).astype(BF16)
        dy_ref[...] = dyb
        dz_ref[...] = _nt(dyb, w_ref[...])
        red_ref[2:3, :] += jnp.sum(dxo_v * y_ref[...].astype(F32), axis=0, keepdims=True)

    tok = pl.BlockSpec((tm, D), lambda i: (i, 0))
    return pl.pallas_call(
        body, name=name, grid=(t_lat // tm,),
        in_specs=[tok, tok, pl.BlockSpec((D, D), lambda i: (0, 0)), pl.BlockSpec((None, 3, D), lambda i: (0, 0, 0))],
        out_specs=[tok, tok, pl.BlockSpec((8, D), lambda i: (0, 0))],
        out_shape=[jax.ShapeDtypeStruct((t_lat, D), F32), jax.ShapeDtypeStruct((t_lat, D), BF16),
                   jax.ShapeDtypeStruct((8, D), F32)],
        compiler_params=_params(("arbitrary",)),
    )(dxo, y, w_out, mod)


def ret_post_bwd(dz, o, p, *, heads, name):
    H, t_lat, _ = o.shape
    T, in_w = p.shape

    def body(dz_ref, o_ref, g_ref, do_ref, dp_ref):
        on, r = _group_norm(o_ref[...])
        gg = g_ref[0:t_lat, :]
        sig, sl = _silu_parts(gg)
        dret = dz_ref[...]
        don = dret * sl
        do_ref[...] = r * (don - jnp.mean(don, axis=-1, keepdims=True)
                           - on * jnp.mean(don * on, axis=-1, keepdims=True))
        dp_ref[0:t_lat, :] = (dret * on * _dsilu(gg, sig)).astype(BF16)
        dp_ref[t_lat:T, :] = jnp.zeros((T - t_lat, HEAD_DIM), BF16)

    return pl.pallas_call(
        body, name=name, grid=(H,),
        in_specs=[pl.BlockSpec((t_lat, HEAD_DIM), lambda h: (0, h)),
                  pl.BlockSpec((None, t_lat, HEAD_DIM), lambda h: (h, 0, 0)),
                  pl.BlockSpec((T, HEAD_DIM), lambda h: (0, 3 * heads + h))],
        out_specs=[pl.BlockSpec((None, t_lat, HEAD_DIM), lambda h: (h, 0, 0)),
                   pl.BlockSpec((T, HEAD_DIM), lambda h: (0, 3 * heads + h))],
        out_shape=[jax.ShapeDtypeStruct((H, t_lat, HEAD_DIM), F32), jax.ShapeDtypeStruct((T, in_w), BF16)],
        compiler_params=_params(("arbitrary",)),
    )(dz, o, p)


def conv_bwd(dp, dz, p, w_conv, *, heads, t_lat, name):
    T = p.shape[0]
    cb_n = w_conv.shape[1] // LANE
    base = 4 * heads

    def body(dp_in, dz_ref, bg_ref, cg_ref, u_ref, w_ref, dp_ref, dw_ref):
        part = pl.program_id(1)
        cg, u = cg_ref[0:t_lat, :], u_ref[0:t_lat, :]
        cu = cg * u
        dconv = dz_ref[...]
        dp_ref[t_lat:T, :] = jnp.zeros((T - t_lat, LANE), BF16)

        @pl.when(part == 0)
        def _():
            c3 = _shift_rows(cu, 1) * w_ref[0:1, :] + cu * w_ref[1:2, :] + _shift_rows(cu, -1) * w_ref[2:3, :]
            dp_ref[0:t_lat, :] = (dconv * c3).astype(BF16)
            dc3 = dconv * bg_ref[0:t_lat, :]
            dw_ref[0:1, :] = jnp.sum(dc3 * _shift_rows(cu, 1), axis=0, keepdims=True)
            dw_ref[1:2, :] = jnp.sum(dc3 * cu, axis=0, keepdims=True)
            dw_ref[2:3, :] = jnp.sum(dc3 * _shift_rows(cu, -1), axis=0, keepdims=True)

        @pl.when(part > 0)
        def _():
            dc3 = dconv * bg_ref[0:t_lat, :]
            dcu = (_shift_rows(dc3, -1) * w_ref[0:1, :] + dc3 * w_ref[1:2, :] + _shift_rows(dc3, 1) * w_ref[2:3, :])
            dp_ref[0:t_lat, :] = (dcu * jnp.where(part == 1, u, cg)).astype(BF16)

    def col(part):
        return pl.BlockSpec((T, LANE), lambda cb, pt: (0, base + part * cb_n + cb))

    return pl.pallas_call(
        body, name=name, grid=(cb_n, 3),
        in_specs=[ANY, pl.BlockSpec((t_lat, LANE), lambda cb, pt: (0, heads + cb)), col(0), col(1), col(2),
                  pl.BlockSpec((CONV_WIDTH, LANE), lambda cb, pt: (0, cb))],
        out_specs=[pl.BlockSpec((T, LANE), lambda cb, pt: (0, base + pt * cb_n + cb)),
                   pl.BlockSpec((CONV_WIDTH, LANE), lambda cb, pt: (0, cb))],
        out_shape=[jax.ShapeDtypeStruct(dp.shape, dp.dtype), jax.ShapeDtypeStruct(w_conv.shape, F32)],
        input_output_aliases={0: 0},
        compiler_params=_params(("arbitrary", "arbitrary")),
    )(dp, dz, p, p, p, w_conv)


def ret_unprep(dp, dq, dk, dv, cos, sin, *, t_lat, tm, name):
    H, T, _ = dk.shape
    n_lat = t_lat // tm

    def body(dp_in, dq_ref, dk_ref, dv_ref, c_ref, s_ref, dp_ref):
        part, i = pl.program_id(0), pl.program_id(2)
        cos_v, sin_v = c_ref[...], s_ref[...]

        @pl.when(part == 0)
        def _():
            d = jnp.where(i < n_lat, dq_ref[...], 0.0)
            dp_ref[...] = _rope(d, cos_v, -sin_v).astype(BF16)

        @pl.when(part == 1)
        def _():
            dp_ref[...] = (_rope(dk_ref[...], cos_v, -sin_v) * K_SCALE).astype(BF16)

        @pl.when(part == 2)
        def _():
            dp_ref[...] = dv_ref[...].astype(BF16)

    full = pl.BlockSpec((None, tm, HEAD_DIM), lambda pt, h, i: (h, i, 0))
    tab = pl.BlockSpec((tm, HEAD_DIM), lambda pt, h, i: (i, 0))
    return pl.pallas_call(
        body, name=name, grid=(3, H, T // tm),
        in_specs=[ANY, pl.BlockSpec((None, tm, HEAD_DIM), lambda pt, h, i: (h, jnp.minimum(i, n_lat - 1), 0)),
                  full, full, tab, tab],
        out_specs=pl.BlockSpec((tm, HEAD_DIM), lambda pt, h, i: (i, pt * H + h)),
        out_shape=jax.ShapeDtypeStruct(dp.shape, dp.dtype),
        input_output_aliases={0: 0},
        compiler_params=_params(("arbitrary", "arbitrary", "arbitrary")),
    )(dp, dq, dk, dv, cos, sin)


def norm_mod_fwd(x, mod, gn, *, tm, name):
    T, D = x.shape

    def body(x_ref, mod_ref, g_ref, h_ref):
        h_ref[...] = _norm_mod(x_ref[...], g_ref[...], mod_ref[0:1, :], mod_ref[1:2, :])

    tok = pl.BlockSpec((tm, D), lambda i: (i, 0))
    return pl.pallas_call(
        body, name=name, grid=(T // tm,),
        in_specs=[tok, pl.BlockSpec((None, 3, D), lambda i: (0, 0, 0)), pl.BlockSpec((1, D), lambda i: (0, 0))],
        out_specs=tok, out_shape=jax.ShapeDtypeStruct((T, D), F32),
        compiler_params=_params(("arbitrary",)),
    )(x, mod, gn)


def norm_mod_bwd(x, dh, dxo, mod, gn, *, tm, name):
    T, D = x.shape

    def body(x_ref, dh_ref, dxo_ref, mod_ref, g_ref, dx_ref, red_ref):
        @pl.when(pl.program_id(0) == 0)
        def _():
            red_ref[...] = jnp.zeros_like(red_ref)

        dxn, dshift, dscale, dg = _norm_mod_bwd(x_ref[...], dh_ref[...], g_ref[...], mod_ref[1:2, :])
        dx_ref[...] = dxo_ref[...] + dxn
        red_ref[0:1, :] += dshift
        red_ref[1:2, :] += dscale
        red_ref[3:4, :] += dg

    tok = pl.BlockSpec((tm, D), lambda i: (i, 0))
    return pl.pallas_call(
        body, name=name, grid=(T // tm,),
        in_specs=[tok, tok, tok, pl.BlockSpec((None, 3, D), lambda i: (0, 0, 0)), pl.BlockSpec((1, D), lambda i: (0, 0))],
        out_specs=[tok, pl.BlockSpec((8, D), lambda i: (0, 0))],
        out_shape=[jax.ShapeDtypeStruct((T, D), F32), jax.ShapeDtypeStruct((8, D), F32)],
        compiler_params=_params(("arbitrary",)),
    )(x, dh, dxo, mod, gn)


def _window_sum(u, w, lead):
    T, C = u.shape
    ext = jnp.concatenate([u, jnp.zeros((POOL_PAD, C), F32)], axis=0)
    k = 1
    while k < w:
        ext = ext + _shift_rows(ext, k)
        k *= 2
    return _shift_rows(ext, -lead)[0:T, :]


def _window_count(T, C, w):
    t = lax.broadcasted_iota(jnp.int32, (T, C), 0)
    lo = jnp.clip(t - w // 2, 0, T)
    hi = jnp.clip(t + (w - w // 2), 0, T)
    return (hi - lo).astype(F32)


def pool_fwd(h, x, pool_w, scale, mod, *, name):
    T, D = h.shape
    G, Cg, _ = pool_w.shape
    ns = Cg // LANE

    def body(h_ref, x_ref, w_ref, sc_ref, mod_ref, xo_ref, pl_ref, yl_ref, acc):
        g, s = pl.program_id(0), pl.program_id(1)
        hv = h_ref[...]
        for gi, win in enumerate(POOL_WINDOWS):
            @pl.when(g == gi)
            def _():
                mean = _window_sum(hv, win, win // 2 - 1) / _window_count(T, LANE, win)
                pooled = (mean - hv).astype(BF16)
                pl_ref[...] = pooled
                contrib = _nn(pooled, w_ref[...])

                @pl.when(s == 0)
                def _():
                    acc[...] = contrib

                @pl.when(s > 0)
                def _():
                    acc[...] += contrib

        @pl.when(s == ns - 1)
        def _():
            yl = acc[...]
            yl_ref[...] = yl.astype(BF16)
            xo_ref[...] = x_ref[...] + mod_ref[2:3, :] * (yl * sc_ref[...])

    grp = pl.BlockSpec((T, Cg), lambda g, s: (0, g))
    sub = pl.BlockSpec((T, LANE), lambda g, s: (0, g * ns + s))
    return pl.pallas_call(
        body, name=name, grid=(G, ns),
        in_specs=[sub, grp, pl.BlockSpec((None, LANE, Cg), lambda g, s: (g, s, 0)),
                  pl.BlockSpec((1, Cg), lambda g, s: (0, g)), pl.BlockSpec((None, 3, Cg), lambda g, s: (0, 0, g))],
        out_specs=[grp, sub, grp],
        out_shape=[jax.ShapeDtypeStruct((T, D), F32), jax.ShapeDtypeStruct((T, D), BF16),
                   jax.ShapeDtypeStruct((T, D), BF16)],
        scratch_shapes=[pltpu.VMEM((T, Cg), F32)],
        compiler_params=_params(("arbitrary", "arbitrary")),
    )(h, x, pool_w, scale, mod)


def pool_bwd(dxo, pooled, yl, pool_w, scale, mod, *, name):
    T, D = dxo.shape
    G, Cg, _ = pool_w.shape
    ns = Cg // LANE

    def body(dxo_ref, pl_ref, yl_ref, w_ref, sc_ref, mod_ref, dh_ref, dw_ref, red_ref, dyl):
        g, s = pl.program_id(0), pl.program_id(1)

        @pl.when(s == 0)
        def _():
            dxo_v = dxo_ref[...]
            ylv = yl_ref[...].astype(F32)
            dy = mod_ref[2:3, :] * dxo_v
            dyl[...] = (dy * sc_ref[...]).astype(BF16)
            red_ref[...] = jnp.zeros_like(red_ref)
            red_ref[2:3, :] = jnp.sum(dxo_v * (ylv * sc_ref[...]), axis=0, keepdims=True)
            red_ref[4:5, :] = jnp.sum(dy * ylv, axis=0, keepdims=True)

        dylv = dyl[...]
        dpool = _nt(dylv, w_ref[...])
        dw_ref[...] = _tn(pl_ref[...], dylv).astype(BF16)
        for gi, win in enumerate(POOL_WINDOWS):
            @pl.when(g == gi)
            def _():
                spread = _window_sum(dpool / _window_count(T, LANE, win), win, win // 2)
                dh_ref[...] = spread - dpool

    grp = pl.BlockSpec((T, Cg), lambda g, s: (0, g))
    sub = pl.BlockSpec((T, LANE), lambda g, s: (0, g * ns + s))
    wsub = pl.BlockSpec((None, LANE, Cg), lambda g, s: (g, s, 0))
    return pl.pallas_call(
        body, name=name, grid=(G, ns),
        in_specs=[grp, sub, grp, wsub, pl.BlockSpec((1, Cg), lambda g, s: (0, g)),
                  pl.BlockSpec((None, 3, Cg), lambda g, s: (0, 0, g))],
        out_specs=[sub, wsub, pl.BlockSpec((8, Cg), lambda g, s: (0, g))],
        out_shape=[jax.ShapeDtypeStruct((T, D), F32), jax.ShapeDtypeStruct((G, Cg, Cg), BF16),
                   jax.ShapeDtypeStruct((8, D), F32)],
        scratch_shapes=[pltpu.VMEM((T, Cg), BF16)],
        compiler_params=_params(("arbitrary", "arbitrary")),
    )(dxo, pooled, yl, pool_w, scale, mod)


def final_loss(x, gn, target, *, tm, name):
    T, D = x.shape

    def body(x_ref, g_ref, t_ref, loss_ref, dx_ref, red_ref):
        @pl.when(pl.program_id(0) == 0)
        def _():
            loss_ref[...] = jnp.zeros_like(loss_ref)
            red_ref[...] = jnp.zeros_like(red_ref)

        xx, g = x_ref[...], g_ref[...]
        r = lax.rsqrt(jnp.mean(xx * xx, axis=-1, keepdims=True) + EPS)
        xhat = xx * r
        err = xhat * g - t_ref[...]
        loss_ref[...] += 0.5 * jnp.sum(jnp.mean(err * err, axis=-1, keepdims=True))
        dy = err / D
        red_ref[0:1, :] += jnp.sum(dy * xhat, axis=0, keepdims=True)
        dxh = dy * g
        dx_ref[...] = r * (dxh - xhat * jnp.mean(dxh * xhat, axis=-1, keepdims=True))

    tok = pl.BlockSpec((tm, D), lambda i: (i, 0))
    return pl.pallas_call(
        body, name=name, grid=(T // tm,),
        in_specs=[tok, pl.BlockSpec((1, D), lambda i: (0, 0)), tok],
        out_specs=[pl.BlockSpec((8, LANE), lambda i: (0, 0)), tok, pl.BlockSpec((8, D), lambda i: (0, 0))],
        out_shape=[jax.ShapeDtypeStruct((8, LANE), F32), jax.ShapeDtypeStruct((T, D), F32),
                   jax.ShapeDtypeStruct((8, D), F32)],
        compiler_params=_params(("arbitrary",)),
    )(x, gn, target)


def _rope_tables(t_lat, t_ctx):
    quarter = HEAD_DIM // 4
    pos = jnp.arange(t_lat)
    inv = ROPE_BASE ** (-jnp.arange(quarter, dtype=F32) / quarter)
    ang = jnp.concatenate([(pos // GRID_W).astype(F32)[:, None] * inv, (pos % GRID_W).astype(F32)[:, None] * inv], axis=-1)
    cos, sin = jnp.cos(ang), jnp.sin(ang)
    cos = jnp.concatenate([jnp.concatenate([cos, cos], axis=-1), jnp.ones((t_ctx, HEAD_DIM), F32)], axis=0)
    sin = jnp.concatenate([jnp.concatenate([-sin, sin], axis=-1), jnp.zeros((t_ctx, HEAD_DIM), F32)], axis=0)
    return cos, sin


def _ffn_grads(h, da, db, s, dy, tag):
    nb, T, cw = da.shape
    D = h.shape[1]
    tk = 512 if T % 512 == 0 else 256
    g_gate = matmul_tn(da, h, nb=nb, m=cw, n=D, a_blocked=True, b_blocked=False, tk=tk, name=f"wgrad_gate_{tag}")
    g_up = matmul_tn(db, h, nb=nb, m=cw, n=D, a_blocked=True, b_blocked=False, tk=tk, name=f"wgrad_up_{tag}")
    g_down = matmul_tn(s, dy, nb=nb, m=cw, n=D, a_blocked=True, b_blocked=False, tk=tk, name=f"wgrad_down_{tag}")
    return [g_gate, g_up, g_down]


def kernel(x, c, ctx, c_ctx, w_mod, b_mod, norm_ffn1, norm_mix, norm_ffn2, ffn1_w_gate, ffn1_w_up, ffn1_w_down, ffn2_w_gate, ffn2_w_up, ffn2_w_down, mix_w_in, mix_w_conv, mix_w_out, ret_decay_fwd, ret_decay_bwd, pool_w, pool_scale, final_norm, loss_target, m_c_ctx, m_w_mod, m_b_mod, m_norm_ffn1, m_norm_mix, m_norm_ffn2, m_ffn1_w_gate, m_ffn1_w_up, m_ffn1_w_down, m_ffn2_w_gate, m_ffn2_w_up, m_ffn2_w_down, m_mix_w_in, m_mix_w_conv, m_mix_w_out, m_ret_decay_fwd, m_ret_decay_bwd, m_pool_w, m_pool_scale, m_final_norm, v_c_ctx, v_w_mod, v_b_mod, v_norm_ffn1, v_norm_mix, v_norm_ffn2, v_ffn1_w_gate, v_ffn1_w_up, v_ffn1_w_down, v_ffn2_w_gate, v_ffn2_w_up, v_ffn2_w_down, v_mix_w_in, v_mix_w_conv, v_mix_w_out, v_ret_decay_fwd, v_ret_decay_bwd, v_pool_w, v_pool_scale, v_final_norm):
    t_lat, D = x.shape[1], x.shape[2]
    t_ctx = ctx.shape[1]
    T = t_lat + t_ctx
    heads = ret_decay_fwd.shape[1]
    mod_w = w_mod.shape[2]
    tm = 256
    tmf = 512 if t_lat % 512 == 0 else 256
    tq = 512 if t_lat % 512 == 0 else 256
    tk = T // 3 if (T % 3 == 0 and (T // 3) % 256 == 0) else 256

    ax, ay, ac = lax.axis_index("x"), lax.axis_index("y"), lax.axis_index("c")
    me = 4 * ax + 2 * ay + ac
    own_block = jnp.reshape(me, (1,)).astype(jnp.int32)
    peer_blocks = jnp.stack([4 * (1 - ax) + 2 * ay + ac, 4 * ax + 2 * (1 - ay) + ac,
                             4 * (1 - ax) + 2 * (1 - ay) + ac]).astype(jnp.int32)

    (c_all,) = all_gather([c], name="gather_cond")
    cond16 = jnp.concatenate([c_all.reshape(N_DEV, D), c_ctx[None, :], jnp.zeros((7, D), F32)], axis=0)
    b_blk = lax.dynamic_slice_in_dim(b_mod, me * mod_w, mod_w, axis=1)[:, None, :]
    m_blk = adaln_fwd(cond16, w_mod, b_blk, name="adaln_fwd")
    m_all, w_conv, pscale = all_gather([m_blk, mix_w_conv[0], pool_scale], name="gather_mod")
    mods = jnp.transpose(m_all, (1, 2, 0, 3)).reshape(2, 16, N_MOD, D)
    mod_x = lax.dynamic_index_in_dim(mods, me, axis=1, keepdims=False)
    mod_c = mods[0, 8]
    w_conv = jnp.transpose(w_conv, (1, 0, 2)).reshape(CONV_WIDTH, -1)
    pscale = pscale.reshape(1, D)

    n_grp, grp_rows, grp_w = pool_w.shape[1:]
    ffn1_w = [(jnp.swapaxes(ffn1_w_gate, 1, 2), "gate"), (jnp.swapaxes(ffn1_w_up, 1, 2), "up"), (ffn1_w_down, "down")]
    ffn2_w = [(jnp.swapaxes(ffn2_w_gate, 1, 2), "gate"), (jnp.swapaxes(ffn2_w_up, 1, 2), "up"), (ffn2_w_down, "down")]
    groups = {
        "01": [(w, 0, f"ffn1_{k}") for w, k in ffn1_w],
        "mix": [(mix_w_in, 0, "w_in"), (mix_w_out, 0, "w_out")],
        "02": [(w, 0, f"ffn2_{k}") for w, k in ffn2_w],
        "11": [(w, 1, f"ffn1_{k}") for w, k in ffn1_w] + [(pool_w.reshape(1, n_grp * grp_rows, grp_w), 0, "pool_w")],
        "12": [(w, 1, f"ffn2_{k}") for w, k in ffn2_w],
    }
    started = {}

    def start(tag, after):
        lands = [cast_place(w, l, own_block, name=f"cast_{nm}_{l}") for w, l, nm in groups[tag]]
        started[tag] = gather_start(lands, after, name=f"gather_start_{tag}")
        return started[tag][-1][0:1, 0:1]

    def finish(tag, after):
        return gather_finish(gather_wait(started[tag], after, name=f"gather_wait_{tag}"), name=f"gather_finish_{tag}")

    lg = jnp.concatenate([jax.nn.log_sigmoid(ret_decay_fwd), jax.nn.log_sigmoid(ret_decay_bwd)], axis=0)
    cos, sin = _rope_tables(t_lat, t_ctx)

    def mod3(l, k, with_ctx=False, tie=None):
        rows = mod_x[l, 3 * k:3 * k + 3][None]
        if with_ctx:
            rows = jnp.concatenate([rows, mod_c[3 * k:3 * k + 3][None]], axis=0)
        return rows if tie is None else rows + tie

    tie = start("01", m_all)
    x0 = jnp.concatenate([x[0], ctx[0]], axis=0) + tie
    wg01, wu01, wd01 = finish("01", x0)
    tie = start("mix", wd01)
    x1, h1, a1, b1, y1 = ffn_fwd(x0, mod3(0, 0, True, tie), norm_ffn1[0:1], wg01, wu01, wd01,
                                 tm=tm, t_first=t_lat, name="ffn_fwd_01")
    w_in, w_out = finish("mix", h1)
    tie = start("02", w_out)
    w_out = w_out.reshape(D, D)
    p, hm = proj_in(x1, mod3(0, 1, True, tie), norm_mix[0:1], w_in, tm=tm, t_first=t_lat, name="proj_in")
    qr, kr, vr = ret_prep(p, cos, sin, heads=heads, tm=tm, name="ret_prep")
    o = ret_fwd(qr, kr, vr, lg, t_lat=t_lat, tq=tq, tk=tk, name="ret_fwd")
    z = ret_post_fwd(o, p, heads=heads, d_model=D, name="ret_post_fwd")
    z = conv_fwd(z, p, w_conv, heads=heads, t_lat=t_lat, name="conv_fwd")
    x2, ym = out_proj(z, w_out, x1, mod3(0, 1), tm=tm, name="out_proj")
    wg02, wu02, wd02 = finish("02", ym)
    tie = start("11", wd02)
    x3, h3, a3, b3, y3 = ffn_fwd(x2, mod3(0, 2, tie=tie), norm_ffn2[0:1], wg02, wu02, wd02, tm=tmf, t_first=t_lat, name="ffn_fwd_02")
    wg11, wu11, wd11, pw = finish("11", h3)
    tie = start("12", wd11)
    pw = jnp.transpose(pw.reshape(N_DEV, n_grp, grp_rows, grp_w), (1, 0, 2, 3)).reshape(n_grp, N_DEV * grp_rows, grp_w)
    x4, h4, a4, b4, y4 = ffn_fwd(x3, mod3(1, 0, tie=tie), norm_ffn1[1:2], wg11, wu11, wd11, tm=tmf, t_first=t_lat, name="ffn_fwd_11")
    hp = norm_mod_fwd(x4, mod3(1, 1), norm_mix[1:2], tm=tm, name="pool_norm_fwd")
    x5, pooled, yl = pool_fwd(hp, x4, pw, pscale, mod3(1, 1), name="pool_fwd")
    wg12, wu12, wd12 = finish("12", yl)
    x6, h6, a6, b6, y6 = ffn_fwd(x5, mod3(1, 2), norm_ffn2[1:2], wg12, wu12, wd12, tm=tmf, t_first=t_lat, name="ffn_fwd_12")
    loss_part, dx6, red_fn = final_loss(x6, final_norm[None, :], loss_target[0], tm=tm, name="final_loss")
    loss = lax.psum(loss_part[0, 0], ("x", "y", "c"))

    reducing = {}

    def reduce_start(tag, grads, after=None):
        res = rs_sibling_exchange(grads, grads[0] if after is None else after, name=f"rs_sibling_{tag}")
        n = len(grads)
        stages, directs = res[:n], res[n:]
        sums = [chip_sum(g, st, peer_blocks, name=f"chip_sum_{tag}_{t}") for t, (g, st) in enumerate(zip(grads, stages))]
        handle = chip_exchange_start(sums, name=f"rs_start_{tag}")
        reducing[tag] = (grads, directs, handle)
        return handle[-1][0:1, 0:1]

    dx5, da6, db6, s6, dy6, red12 = ffn_bwd_tok(x5, dx6, y6, a6, b6, mod3(1, 2), norm_ffn2[1:2], wg12, wu12, wd12,
                                                tm=tmf, t_first=t_lat, name="ffn_bwd_12")
    tie = reduce_start("12", _ffn_grads(h6, da6, db6, s6, dy6, "12"))
    dhp, g_pw, red_pool = pool_bwd(dx5, pooled, yl, pw, pscale, mod3(1, 1, tie=tie), name="pool_bwd")
    dx4, red_pn = norm_mod_bwd(x4, dhp, dx5, mod3(1, 1), norm_mix[1:2], tm=tm, name="pool_norm_bwd")
    dx3, da4, db4, s4, dy4, red11 = ffn_bwd_tok(x3, dx4, y4, a4, b4, mod3(1, 0), norm_ffn1[1:2], wg11, wu11, wd11,
                                                tm=tmf, t_first=t_lat, name="ffn_bwd_11")
    g_pw = jnp.transpose(g_pw.reshape(n_grp, N_DEV, grp_rows, grp_w), (1, 0, 2, 3)).reshape(N_DEV, n_grp * grp_rows, grp_w)
    tie = reduce_start("11", list(_ffn_grads(h4, da4, db4, s4, dy4, "11")) + [g_pw])
    dx2, da3, db3, s3, dy3, red02 = ffn_bwd_tok(x2, dx3, y3, a3, b3, mod3(0, 2, tie=tie), norm_ffn2[0:1], wg02, wu02, wd02,
                                                tm=tmf, t_first=t_lat, name="ffn_bwd_02")
    tie = reduce_start("02", _ffn_grads(h3, da3, db3, s3, dy3, "02"))
    dz, dym, red_op = out_proj_bwd(dx2, ym, w_out, mod3(0, 1, tie=tie), tm=tm, name="out_proj_bwd")
    g_wout = matmul_tn(z, dym, nb=N_DEV, m=D // N_DEV, n=D, a_blocked=False, b_blocked=False,
                       tk=512 if t_lat % 512 == 0 else 256, name="wgrad_out")
    do, dp = ret_post_bwd(dz, o, p, heads=heads, name="ret_post_bwd")
    dp, g_conv = conv_bwd(dp, dz, p, w_conv, heads=heads, t_lat=t_lat, name="conv_bwd")
    dq, dk, dv, dlg = ret_bwd(qr, kr, vr, do, lg, t_lat=t_lat, tq=tq, tk=tk, name="ret_bwd")
    dp = ret_unprep(dp, dq, dk, dv, cos, sin, t_lat=t_lat, tm=tm, name="ret_unprep")
    dx1, red_mix = proj_bwd_tok(x1, dx2, dp, mod3(0, 1, True), norm_mix[0:1], w_in, tm=tm, name="proj_bwd")
    g_win = matmul_tn(hm, dp, nb=N_DEV, m=D, n=w_in.shape[2], a_blocked=False, b_blocked=False, tk=256, name="wgrad_in")
    tie = reduce_start("mix", [g_win, g_wout])
    dx0, da1, db1, s1, dy1, red01 = ffn_bwd_tok(x0, dx1, y1, a1, b1, mod3(0, 0, True, tie), norm_ffn1[0:1], wg01, wu01, wd01,
                                                tm=tm, t_first=t_lat, name="ffn_bwd_01")
    res = {}

    dm_x = jnp.stack([jnp.concatenate([red01[0, 0:3], red_mix[0, 0:2], red_op[2:3], red02[0, 0:3]], axis=0),
                      jnp.concatenate([red11[0, 0:3], red_pn[0:2], red_pool[2:3], red12[0, 0:3]], axis=0)])
    dm_c = jnp.concatenate([red01[1, 0:3], red_mix[1, 0:2], jnp.zeros((4, D), F32)], axis=0)
    d_lg = dlg[:, 0:2, 0].T
    d_dec_f = d_lg[0:1] * jax.nn.sigmoid(-ret_decay_fwd)
    d_dec_b = d_lg[1:2] * jax.nn.sigmoid(-ret_decay_bwd)
    pieces = [dm_x.reshape(-1), dm_c.reshape(-1),
              jnp.stack([red01[0, 3] + red01[1, 3], red11[0, 3]]).reshape(-1),
              jnp.stack([red_mix[0, 3] + red_mix[1, 3], red_pn[3]]).reshape(-1),
              jnp.stack([red02[0, 3], red12[0, 3]]).reshape(-1),
              red_fn[0], d_dec_f.reshape(-1), d_dec_b.reshape(-1), g_conv.reshape(-1), red_pool[4]]
    sizes = [int(a.shape[0]) for a in pieces]
    n_pack = sum(sizes)
    n_pad = -n_pack % 8192
    packed = jnp.concatenate(pieces + [jnp.zeros((n_pad,), F32)])[None, :]
    (packed_all,) = all_gather([packed], name="gather_partials")
    total = sum_partials(packed_all, name="sum_partials")[0]
    offs = [0]
    for s in sizes:
        offs.append(offs[-1] + s)
    seg = [total[offs[i]:offs[i + 1]] for i in range(len(sizes))]
    g_dm = seg[0].reshape(2, N_MOD * D)
    g_dmc = seg[1].reshape(N_MOD * D)
    g_b_mod = g_dm.at[0].add(g_dmc)
    g_norm_ffn1, g_norm_mix, g_norm_ffn2 = (seg[k].reshape(2, D) for k in (2, 3, 4))
    g_final = seg[5]
    g_dec_f, g_dec_b = seg[6].reshape(1, heads), seg[7].reshape(1, heads)
    g_conv_all = seg[8].reshape(CONV_WIDTH, -1)
    g_pscale_all = seg[9]
    conv_w = mix_w_conv.shape[2]
    g_w_conv = lax.dynamic_slice_in_dim(g_conv_all, me * conv_w, conv_w, axis=1)[None]
    ps_w = pool_scale.shape[1]
    g_pool_scale = lax.dynamic_slice_in_dim(g_pscale_all, me * ps_w, ps_w, axis=0)[None]

    dm_rows = packed_all[:, 0, 0:offs[1]].reshape(N_DEV, 2, N_MOD * D)
    dmc_rows = packed_all[:, 0, offs[1]:offs[2]].reshape(N_DEV, N_MOD * D)
    dm_x_blk = jnp.transpose(lax.dynamic_slice_in_dim(dm_rows, me * mod_w, mod_w, axis=2), (1, 0, 2))
    dm_c_blk = jnp.stack([lax.dynamic_slice_in_dim(dmc_rows, me * mod_w, mod_w, axis=1),
                          jnp.zeros((N_DEV, mod_w), F32)])
    g_w_mod, d_w_mod, nm_w_mod, nv_w_mod, cctx_part = adaln_bwd(cond16, dm_x_blk, dm_c_blk, w_mod, m_w_mod, v_w_mod,
                                                                name="adaln_bwd")
    cpad = jnp.concatenate([cctx_part[0], jnp.zeros((8192 - D,), F32)])[None, :] if D < 8192 else cctx_part[0:1]
    (cctx_all,) = all_gather([cpad], name="gather_cctx")
    g_c_ctx = sum_partials(cctx_all, name="sum_cctx")[0, :D]

    small = [("c_ctx", g_c_ctx, c_ctx, m_c_ctx, v_c_ctx), ("b_mod", g_b_mod, b_mod, m_b_mod, v_b_mod),
             ("norm_ffn1", g_norm_ffn1, norm_ffn1, m_norm_ffn1, v_norm_ffn1),
             ("norm_mix", g_norm_mix, norm_mix, m_norm_mix, v_norm_mix),
             ("norm_ffn2", g_norm_ffn2, norm_ffn2, m_norm_ffn2, v_norm_ffn2),
             ("mix_w_conv", g_w_conv, mix_w_conv, m_mix_w_conv, v_mix_w_conv),
             ("ret_decay_fwd", g_dec_f, ret_decay_fwd, m_ret_decay_fwd, v_ret_decay_fwd),
             ("ret_decay_bwd", g_dec_b, ret_decay_bwd, m_ret_decay_bwd, v_ret_decay_bwd),
             ("pool_scale", g_pool_scale, pool_scale, m_pool_scale, v_pool_scale),
             ("final_norm", g_final, final_norm, m_final_norm, v_final_norm)]
    ssz = [int(a[1].size) for a in small]
    spad = -sum(ssz) % LANE

    def pack(k):
        return jnp.concatenate([a[k].reshape(-1) for a in small] + [jnp.ones((spad,), F32)])[None, :]

    sd, sm, sv = adam_plain(pack(1), pack(2), pack(3), pack(4), name="adam_small")
    soff = [0]
    for s in ssz:
        soff.append(soff[-1] + s)
    for i, (nm, g, w, _, _) in enumerate(small):
        res[nm] = [g.reshape(w.shape)] + [a[0, soff[i]:soff[i + 1]].reshape(w.shape) for a in (sd, sm, sv)]
    res["w_mod"] = [g_w_mod, d_w_mod, nm_w_mod, nv_w_mod]

    after = sd
    tk01 = 512 if T % 512 == 0 else 256
    for kind, lhs, rhs in (("gate", da1, h1), ("up", db1, h1), ("down", s1, dy1)):
        g = matmul_tn(lhs, rhs, nb=N_DEV, m=lhs.shape[2], n=D, a_blocked=True, b_blocked=False, tk=tk01,
                      name=f"wgrad_{kind}_01", after=after)
        tie = reduce_start(f"01_{kind}", [g], g)
        after = reducing[f"01_{kind}"][2][2]
    grad_x = dx0[:t_lat][None] + tie

    last = [grad_x]

    def reduce_finish(tag):
        grads, directs, handle = reducing[tag]
        landed = chip_exchange_wait(handle, last[0], name=f"rs_wait_{tag}")
        return list(zip(grads, directs, landed))

    def big(w, m, v, l, part, prev, nm, transposed=False):
        if transposed:
            w, m, v = (jnp.swapaxes(a, 1, 2) for a in (w, m, v))
        outs = adam_reduced(own_block, part[0], part[1], part[2], w, m, v, l, prev, name=nm)
        last[0] = outs[0]
        return outs

    ffn1 = [(ffn1_w_gate, m_ffn1_w_gate, v_ffn1_w_gate), (ffn1_w_up, m_ffn1_w_up, v_ffn1_w_up),
            (ffn1_w_down, m_ffn1_w_down, v_ffn1_w_down)]
    ffn2 = [(ffn2_w_gate, m_ffn2_w_gate, v_ffn2_w_gate), (ffn2_w_up, m_ffn2_w_up, v_ffn2_w_up),
            (ffn2_w_down, m_ffn2_w_down, v_ffn2_w_down)]
    kinds = ["gate", "up", "down"]
    half = {}
    parts = reduce_finish("12")
    for t, (w, m, v) in enumerate(ffn2):
        half[f"ffn2_w_{kinds[t]}"] = big(w, m, v, 1, parts[t], None, f"adam_ffn2_w_{kinds[t]}_1", transposed=t < 2)
    parts = reduce_finish("11")
    for t, (w, m, v) in enumerate(ffn1):
        half[f"ffn1_w_{kinds[t]}"] = big(w, m, v, 1, parts[t], None, f"adam_ffn1_w_{kinds[t]}_1", transposed=t < 2)
    res["pool_w"] = [a.reshape(pool_w.shape) for a in
                     big(pool_w.reshape(1, n_grp * grp_rows, grp_w), m_pool_w.reshape(1, n_grp * grp_rows, grp_w),
                         v_pool_w.reshape(1, n_grp * grp_rows, grp_w), 0, parts[3], None, "adam_pool_w")]
    parts = reduce_finish("02")
    for t, (w, m, v) in enumerate(ffn2):
        nm = f"ffn2_w_{kinds[t]}"
        outs = big(w, m, v, 0, parts[t], half[nm], f"adam_{nm}_0", transposed=t < 2)
        res[nm] = [jnp.swapaxes(a, 1, 2) for a in outs] if t < 2 else outs
    parts = reduce_finish("mix")
    res["mix_w_in"] = big(mix_w_in, m_mix_w_in, v_mix_w_in, 0, parts[0], None, "adam_mix_w_in")
    res["mix_w_out"] = big(mix_w_out, m_mix_w_out, v_mix_w_out, 0, parts[1], None, "adam_mix_w_out")
    for t, (w, m, v) in enumerate(ffn1):
        nm = f"ffn1_w_{kinds[t]}"
        outs = big(w, m, v, 0, reduce_finish(f"01_{kinds[t]}")[0], half[nm], f"adam_{nm}_0", transposed=t < 2)
        res[nm] = [jnp.swapaxes(a, 1, 2) for a in outs] if t < 2 else outs

    order = ["c_ctx", "w_mod", "b_mod", "norm_ffn1", "norm_mix", "norm_ffn2", "ffn1_w_gate", "ffn1_w_up", "ffn1_w_down",
             "ffn2_w_gate", "ffn2_w_up", "ffn2_w_down", "mix_w_in", "mix_w_conv", "mix_w_out", "ret_decay_fwd",
             "ret_decay_bwd", "pool_w", "pool_scale", "final_norm"]
    return (loss, grad_x, *[res[n][0] for n in order], *[res[n][1] for n in order],
            *[res[n][2] for n in order], *[res[n][3] for n in order])
```

```python
import jax
import jax.numpy as jnp
from jax import lax
from jax.experimental import pallas as pl
from jax.experimental.pallas import tpu as pltpu

F32 = jnp.float32
BF16 = jnp.bfloat16
MESH = pl.DeviceIdType.MESH

N_DEV = 8
N_MOD = 9
EPS = 1e-6
GN_EPS = 1e-5
MACARON = 0.5
HEAD_DIM = 128
K_SCALE = HEAD_DIM ** -0.5
ROPE_BASE = 10000.0
GRID_W = 64
CONV_WIDTH = 3
POOL_WINDOWS = (2, 4, 8, 16)
POOL_PAD = 16

ADAM_LR = 0.001
ADAM_B1 = 0.9
ADAM_B2 = 0.999
ADAM_EPS = 1e-08
ADAM_WD = 0.01
ADAM_STEP = 10

LANE = 128
ROW_CHUNK = 128
MM_ROWS = 256
VMEM_LIMIT = 56 * 1024 * 1024
ANY = pl.BlockSpec(memory_space=pl.ANY)


def _params(sem=None):
    kw = dict(vmem_limit_bytes=VMEM_LIMIT)
    if sem is not None:
        kw["dimension_semantics"] = sem
    return pltpu.CompilerParams(**kw)


def _nt(a, b):
    return lax.dot_general(a, b, (((1,), (1,)), ((), ())), preferred_element_type=F32)


def _tn(a, b):
    return lax.dot_general(a, b, (((0,), (0,)), ((), ())), preferred_element_type=F32)


def _nn(a, b):
    return jnp.dot(a, b, preferred_element_type=F32)


def _silu_parts(a):
    sig = jax.nn.sigmoid(a)
    return sig, a * sig


def _dsilu(a, sig):
    return sig * (1.0 + a * (1.0 - sig))


def _norm_mod(x, g, shift, scale):
    r = lax.rsqrt(jnp.mean(x * x, axis=-1, keepdims=True) + EPS)
    return (x * r * g) * (1.0 + scale) + shift


def _norm_mod_bwd(x, dh, g, scale):
    r = lax.rsqrt(jnp.mean(x * x, axis=-1, keepdims=True) + EPS)
    xhat = x * r
    dn = dh * (1.0 + scale)
    dshift = jnp.sum(dh, axis=0, keepdims=True)
    dscale = jnp.sum(dh * (xhat * g), axis=0, keepdims=True)
    dg = jnp.sum(dn * xhat, axis=0, keepdims=True)
    dxh = dn * g
    dx = r * (dxh - xhat * jnp.mean(dxh * xhat, axis=-1, keepdims=True))
    return dx, dshift, dscale, dg


def _group_index(i, tiles_first, n_groups):
    if n_groups == 1:
        return 0
    return jnp.where(i >= tiles_first, 1, 0)


def _mesh_pos():
    x, y, c = lax.axis_index("x"), lax.axis_index("y"), lax.axis_index("c")
    chips = [(1 - x, y), (x, 1 - y), (1 - x, 1 - y)]
    return x, y, c, chips


def _flat(px, py, pc):
    return 4 * px + 2 * py + pc


def all_gather(shards, name):
    n = len(shards)

    def body(*refs):
        ins, outs = refs[:n], refs[n:2 * n]
        send_sems, recv_sems, local_sems = refs[2 * n:]
        x, y, c, chips = _mesh_pos()
        me, sibling = (x, y, c), (x, y, 1 - c)

        def copy(t, k, block, to, src=None):
            dst = outs[t].at[_flat(*block)]
            return pltpu.make_async_remote_copy(
                src_ref=dst if src is None else src, dst_ref=dst,
                send_sem=send_sems.at[t, k], recv_sem=recv_sems.at[t, k],
                device_id=to, device_id_type=MESH)

        mine = [pltpu.make_async_copy(ins[t], outs[t].at[_flat(*me)], local_sems.at[t]) for t in range(n)]
        for cp in mine:
            cp.start()
        first = []
        for t in range(n):
            first.append(copy(t, 0, me, sibling, src=ins[t]))
            first += [copy(t, 1 + j, me, (*chip, c), src=ins[t]) for j, chip in enumerate(chips)]
        for cp in first:
            cp.start()
        passed = []
        for t in range(n):
            for j, chip in enumerate(chips):
                copy(t, 1 + j, (*chip, c), me).wait_recv()
                fwd = copy(t, 4 + j, (*chip, c), sibling)
                fwd.start()
                passed.append(fwd)
        for t in range(n):
            copy(t, 0, sibling, me).wait_recv()
            for j, chip in enumerate(chips):
                copy(t, 4 + j, (*chip, 1 - c), me).wait_recv()
        for cp in first + passed:
            cp.wait_send()
        for cp in mine:
            cp.wait()

    return pl.pallas_call(
        body, name=name,
        out_shape=[jax.ShapeDtypeStruct((N_DEV,) + s.shape, s.dtype) for s in shards],
        in_specs=[ANY] * n, out_specs=[ANY] * n,
        scratch_shapes=[pltpu.SemaphoreType.DMA((n, 7)), pltpu.SemaphoreType.DMA((n, 7)), pltpu.SemaphoreType.DMA((n,))],
    )(*shards)


HBM_SPEC = pl.BlockSpec(memory_space=pltpu.HBM)
SEM_SPEC = pl.BlockSpec(memory_space=pltpu.SEMAPHORE)
DATAFLOW = pltpu.SideEffectType.DATAFLOW_SIDE_EFFECTING


def _in_hbm(a):
    return pltpu.with_memory_space_constraint(a, pltpu.HBM)


def _push_peers():
    x, y, c, chips = _mesh_pos()
    return [(*chip, c) for chip in chips] + [(x, y, 1 - c)]


def cast_place(w, layer, own_block, name):
    _, R, C = w.shape
    tr = _row_tile(R, C)

    def body(idx_ref, w_ref, o_ref):
        o_ref[...] = w_ref[...].astype(BF16)

    return pl.pallas_call(
        body, name=name,
        grid_spec=pltpu.PrefetchScalarGridSpec(
            num_scalar_prefetch=1, grid=(R // tr,),
            in_specs=[pl.BlockSpec((None, tr, C), lambda i, idx: (layer, i, 0))],
            out_specs=pl.BlockSpec((None, tr, C), lambda i, idx: (idx[0], i, 0))),
        out_shape=jax.ShapeDtypeStruct((N_DEV, R, C), BF16),
        compiler_params=_params(("arbitrary",)),
    )(own_block, w)


def gather_start(lands, after, name):
    n = len(lands)

    def body(*refs):
        lz = refs[:n]
        send_sems, recv_sems = refs[n + 1], refs[n + 2]
        token = refs[-1]
        x, y, c, _ = _mesh_pos()
        for t in range(n):
            mine = lz[t].at[_flat(x, y, c)]
            for k, peer in enumerate(_push_peers()):
                pltpu.make_async_remote_copy(
                    src_ref=mine, dst_ref=mine, send_sem=send_sems.at[4 * t + k],
                    recv_sem=recv_sems.at[4 * t + k], device_id=peer, device_id_type=MESH).start()
        token[...] = jnp.zeros_like(token)

    return pl.pallas_call(
        body, name=name,
        out_shape=(pltpu.SemaphoreType.DMA((4 * n,)), pltpu.SemaphoreType.DMA((4 * n,)),
                   *[pltpu.HBM(l.shape, l.dtype) for l in lands], jax.ShapeDtypeStruct((8, LANE), F32)),
        in_specs=[HBM_SPEC] * n + [ANY],
        out_specs=(SEM_SPEC, SEM_SPEC, *[HBM_SPEC] * n, pl.BlockSpec(memory_space=pltpu.VMEM)),
        input_output_aliases={i: 2 + i for i in range(n)},
        compiler_params=pltpu.CompilerParams(has_side_effects=DATAFLOW),
    )(*[_in_hbm(l) for l in lands], after)


def gather_wait(handle, after, name):
    send, recv = handle[0], handle[1]
    bufs = handle[2:-1]
    n = len(bufs)

    def body(*refs):
        lz = refs[:n]
        send_sems, recv_sems = refs[n], refs[n + 1]
        x, y, c, _ = _mesh_pos()
        for t in range(n):
            for k, peer in enumerate(_push_peers()):
                cp = pltpu.make_async_remote_copy(
                    src_ref=lz[t].at[_flat(x, y, c)], dst_ref=lz[t].at[_flat(*peer)], send_sem=send_sems.at[4 * t + k],
                    recv_sem=recv_sems.at[4 * t + k], device_id=peer, device_id_type=MESH)
                cp.wait_send()
                cp.wait_recv()

    outs = pl.pallas_call(
        body, name=name,
        out_shape=tuple(pltpu.HBM(b.shape, b.dtype) for b in bufs),
        in_specs=[HBM_SPEC] * n + [SEM_SPEC, SEM_SPEC, ANY],
        out_specs=tuple([HBM_SPEC] * n),
        input_output_aliases={i: i for i in range(n)},
        compiler_params=pltpu.CompilerParams(has_side_effects=DATAFLOW),
    )(*bufs, send, recv, after)
    return list(outs)


def gather_finish(lands, name):
    n = len(lands)

    def body(*refs):
        outs = refs[n:2 * n]
        send_sems, recv_sems = refs[2 * n:]
        x, y, c, chips = _mesh_pos()
        sibling = (x, y, 1 - c)

        def copy(t, j, core):
            blk = outs[t].at[_flat(*chips[j], core)]
            return pltpu.make_async_remote_copy(
                src_ref=blk, dst_ref=blk, send_sem=send_sems.at[t, j], recv_sem=recv_sems.at[t, j],
                device_id=sibling, device_id_type=MESH)

        sends = [copy(t, j, c) for t in range(n) for j in range(3)]
        for cp in sends:
            cp.start()
        for t in range(n):
            for j in range(3):
                copy(t, j, 1 - c).wait_recv()
        for cp in sends:
            cp.wait_send()

    return pl.pallas_call(
        body, name=name,
        out_shape=[jax.ShapeDtypeStruct(l.shape, l.dtype) for l in lands],
        in_specs=[ANY] * n, out_specs=[ANY] * n,
        input_output_aliases={t: t for t in range(n)},
        scratch_shapes=[pltpu.SemaphoreType.DMA((n, 3)), pltpu.SemaphoreType.DMA((n, 3))],
    )(*lands)


def chip_exchange_start(sums, name):
    n = len(sums)
    lands = [lax.empty(s.shape, s.dtype) for s in sums]

    def body(*refs):
        ins, lz = refs[:n], refs[n:2 * n]
        send_sems, recv_sems = refs[2 * n], refs[2 * n + 1]
        token = refs[-1]
        peers = _push_peers()
        for t in range(n):
            for j in range(3):
                pltpu.make_async_remote_copy(
                    src_ref=ins[t].at[j], dst_ref=lz[t].at[j], send_sem=send_sems.at[3 * t + j],
                    recv_sem=recv_sems.at[3 * t + j], device_id=peers[j], device_id_type=MESH).start()
        token[...] = jnp.zeros_like(token)

    return pl.pallas_call(
        body, name=name,
        out_shape=(pltpu.SemaphoreType.DMA((3 * n,)), pltpu.SemaphoreType.DMA((3 * n,)),
                   *[pltpu.HBM(s.shape, s.dtype) for s in sums], *[pltpu.HBM(s.shape, s.dtype) for s in sums],
                   jax.ShapeDtypeStruct((8, LANE), F32)),
        in_specs=[HBM_SPEC] * (2 * n),
        out_specs=(SEM_SPEC, SEM_SPEC, *[HBM_SPEC] * (2 * n), pl.BlockSpec(memory_space=pltpu.VMEM)),
        input_output_aliases={i: 2 + i for i in range(2 * n)},
        compiler_params=pltpu.CompilerParams(has_side_effects=DATAFLOW),
    )(*[_in_hbm(s) for s in sums], *[_in_hbm(l) for l in lands])


def chip_exchange_wait(handle, after, name):
    send, recv = handle[0], handle[1]
    n = (len(handle) - 3) // 2
    bufs = handle[2:2 + 2 * n]

    def body(*refs):
        ins, lz = refs[:n], refs[n:2 * n]
        send_sems, recv_sems = refs[2 * n], refs[2 * n + 1]
        peers = _push_peers()
        for t in range(n):
            for j in range(3):
                cp = pltpu.make_async_remote_copy(
                    src_ref=ins[t].at[j], dst_ref=lz[t].at[j], send_sem=send_sems.at[3 * t + j],
                    recv_sem=recv_sems.at[3 * t + j], device_id=peers[j], device_id_type=MESH)
                cp.wait_send()
                cp.wait_recv()

    outs = pl.pallas_call(
        body, name=name,
        out_shape=tuple(pltpu.HBM(b.shape, b.dtype) for b in bufs),
        in_specs=[HBM_SPEC] * (2 * n) + [SEM_SPEC, SEM_SPEC, ANY],
        out_specs=tuple([HBM_SPEC] * (2 * n)),
        input_output_aliases={i: i for i in range(2 * n)},
        compiler_params=pltpu.CompilerParams(has_side_effects=DATAFLOW),
    )(*bufs, send, recv, after)
    return list(outs[n:])


def rs_sibling_exchange(grads, after, name):
    n = len(grads)

    def body(*refs):
        ins, stages, directs = refs[:n], refs[n + 1:2 * n + 1], refs[2 * n + 1:3 * n + 1]
        send_sems, recv_sems = refs[3 * n + 1:]
        x, y, c, chips = _mesh_pos()
        sibling = (x, y, 1 - c)

        def copy(t, k):
            if k < 3:
                src, dst = ins[t].at[_flat(*chips[k], 1 - c)], stages[t].at[k]
            else:
                src, dst = ins[t].at[_flat(x, y, 1 - c)], directs[t]
            return pltpu.make_async_remote_copy(
                src_ref=src, dst_ref=dst, send_sem=send_sems.at[t, k], recv_sem=recv_sems.at[t, k],
                device_id=sibling, device_id_type=MESH)

        cps = [copy(t, k) for t in range(n) for k in range(4)]
        for cp in cps:
            cp.start()
        for cp in cps:
            cp.wait_recv()
        for cp in cps:
            cp.wait_send()

    return pl.pallas_call(
        body, name=name,
        out_shape=[jax.ShapeDtypeStruct((3,) + g.shape[1:], g.dtype) for g in grads]
        + [jax.ShapeDtypeStruct(g.shape[1:], g.dtype) for g in grads],
        in_specs=[ANY] * (n + 1), out_specs=[ANY] * (2 * n),
        scratch_shapes=[pltpu.SemaphoreType.DMA((n, 4)), pltpu.SemaphoreType.DMA((n, 4))],
    )(*grads, after)


def _row_tile(rows, cols, limit_bytes=3 << 19):
    best = None
    for t in range(16, rows + 1, 16):
        if rows % t == 0 and t * cols * 4 <= limit_bytes:
            best = t
    return best if best is not None else rows


def chip_sum(grad, stage, peer_blocks, name):
    _, R, C = grad.shape
    tr = _row_tile(R, C)

    def body(idx_ref, g_ref, s_ref, o_ref):
        o_ref[...] = (g_ref[...].astype(F32) + s_ref[...].astype(F32)).astype(BF16)

    return pl.pallas_call(
        body, name=name,
        grid_spec=pltpu.PrefetchScalarGridSpec(
            num_scalar_prefetch=1, grid=(3, R // tr),
            in_specs=[pl.BlockSpec((None, tr, C), lambda j, i, idx: (idx[j], i, 0)),
                      pl.BlockSpec((None, tr, C), lambda j, i, idx: (j, i, 0))],
            out_specs=pl.BlockSpec((None, tr, C), lambda j, i, idx: (j, i, 0))),
        out_shape=jax.ShapeDtypeStruct((3, R, C), BF16),
        compiler_params=_params(("arbitrary", "arbitrary")),
    )(peer_blocks, grad, stage)


def _adamw(w, g, m, v):
    m2 = ADAM_B1 * m + (1.0 - ADAM_B1) * g
    v2 = ADAM_B2 * v + (1.0 - ADAM_B2) * (g * g)
    m_hat = m2 / (1.0 - ADAM_B1 ** ADAM_STEP)
    v_hat = v2 / (1.0 - ADAM_B2 ** ADAM_STEP)
    delta = -ADAM_LR * (m_hat / (jnp.sqrt(v_hat) + ADAM_EPS) + ADAM_WD * w)
    return delta, m2, v2


def adam_reduced(own_block, grad, direct, landed, w, m, v, layer, prev, name):
    L, R, C = w.shape
    tr = _row_tile(R, C, 1 << 20)
    first = prev is None

    def body(idx_ref, g_ref, d_ref, l_ref, w_ref, m_ref, v_ref, *rest):
        og, od, om, ov = rest[-4:]
        g = g_ref[...].astype(F32) + d_ref[...].astype(F32)
        for j in range(3):
            g = g + l_ref[j].astype(F32)
        delta, m2, v2 = _adamw(w_ref[...], g, m_ref[...], v_ref[...])
        og[...] = g
        od[...] = delta
        om[...] = m2
        ov[...] = v2

    lay = pl.BlockSpec((None, tr, C), lambda i, idx: (layer, i, 0))
    in_specs = [pl.BlockSpec((None, tr, C), lambda i, idx: (idx[0], i, 0)),
                pl.BlockSpec((tr, C), lambda i, idx: (i, 0)),
                pl.BlockSpec((3, tr, C), lambda i, idx: (0, i, 0)),
                lay, lay, lay]
    args = [own_block, grad, direct, landed, w, m, v]
    aliases = {}
    if not first:
        in_specs += [ANY] * 4
        args += list(prev)
        aliases = {7 + k: k for k in range(4)}
    return pl.pallas_call(
        body, name=name,
        grid_spec=pltpu.PrefetchScalarGridSpec(
            num_scalar_prefetch=1, grid=(R // tr,), in_specs=in_specs, out_specs=[lay] * 4),
        out_shape=[jax.ShapeDtypeStruct((L, R, C), F32)] * 4,
        input_output_aliases=aliases,
        compiler_params=_params(("arbitrary",)),
    )(*args)


def sum_partials(parts, name):
    _, _, N = parts.shape
    tn = 8192

    def body(p_ref, o_ref):
        g = p_ref[0]
        for k in range(1, N_DEV):
            g = g + p_ref[k]
        o_ref[...] = g

    return pl.pallas_call(
        body, name=name, grid=(N // tn,),
        in_specs=[pl.BlockSpec((N_DEV, 1, tn), lambda i: (0, 0, i))],
        out_specs=pl.BlockSpec((1, tn), lambda i: (0, i)),
        out_shape=jax.ShapeDtypeStruct((1, N), F32),
    )(parts)


def adam_plain(g, w, m, v, name):
    _, N = g.shape

    def body(g_ref, w_ref, m_ref, v_ref, od, om, ov):
        delta, m2, v2 = _adamw(w_ref[...], g_ref[...], m_ref[...], v_ref[...])
        od[...] = delta
        om[...] = m2
        ov[...] = v2

    return pl.pallas_call(
        body, name=name, out_shape=[jax.ShapeDtypeStruct((1, N), F32)] * 3,
    )(g, w, m, v)


def adaln_fwd(cond16, w_mod, b_blk, name):
    L, D, W = w_mod.shape
    tn = 768 if W % 768 == 0 else W

    def body(c_ref, w_ref, b_ref, o_ref):
        c = c_ref[...]
        sc = (c * jax.nn.sigmoid(c)).astype(BF16)
        o_ref[...] = _nn(sc, w_ref[...].astype(BF16)) + b_ref[...]

    return pl.pallas_call(
        body, name=name, grid=(L, W // tn),
        in_specs=[pl.BlockSpec((16, D), lambda l, i: (0, 0)),
                  pl.BlockSpec((None, D, tn), lambda l, i: (l, 0, i)),
                  pl.BlockSpec((None, 1, tn), lambda l, i: (l, 0, i))],
        out_specs=pl.BlockSpec((None, 16, tn), lambda l, i: (l, 0, i)),
        out_shape=jax.ShapeDtypeStruct((L, 16, W), F32),
        compiler_params=_params(("arbitrary", "arbitrary")),
    )(cond16, w_mod, b_blk)


def adaln_bwd(cond16, dm_x, dm_c, w_mod, m_mod, v_mod, name):
    L, D, W = w_mod.shape
    tn = 256 if W % 256 == 0 else W
    nt = W // tn

    def body(c_ref, cT_ref, dx_ref, dc_ref, w_ref, m_ref, v_ref, og, od, om, ov, pc_ref):
        l, i = pl.program_id(0), pl.program_id(1)
        c = c_ref[...]
        sig, sl = _silu_parts(c)
        cT = cT_ref[...]
        sigT = jax.nn.sigmoid(cT)
        scT = (cT * sigT).astype(BF16)
        dmc = jnp.sum(dc_ref[...], axis=0, keepdims=True)
        dm16 = jnp.concatenate([dx_ref[...], jnp.broadcast_to(dmc, (8, tn))], axis=0)
        row = lax.broadcasted_iota(jnp.int32, (16, tn), 0)
        dm16 = jnp.where(row <= 8, dm16, 0.0).astype(BF16)
        w = w_ref[...]
        g = _nn(scT, dm16)
        delta, m2, v2 = _adamw(w, g, m_ref[...], v_ref[...])
        og[...] = g
        od[...] = delta
        om[...] = m2
        ov[...] = v2

        @pl.when((l == 0) & (i == 0))
        def _():
            pc_ref[...] = jnp.zeros_like(pc_ref)

        @pl.when(l == 0)
        def _():
            back = _nt(jnp.broadcast_to(dmc, (8, tn)).astype(BF16), w.astype(BF16))
            pc_ref[...] += back * _dsilu(c[8:9, :], sig[8:9, :])

    col = pl.BlockSpec((None, D, tn), lambda l, i: (l, 0, i))
    row8 = pl.BlockSpec((None, 8, tn), lambda l, i: (l, 0, i))
    return pl.pallas_call(
        body, name=name, grid=(L, nt),
        in_specs=[pl.BlockSpec((16, D), lambda l, i: (0, 0)), pl.BlockSpec((D, 16), lambda l, i: (0, 0)),
                  row8, row8, col, col, col],
        out_specs=[col, col, col, col, pl.BlockSpec((8, D), lambda l, i: (0, 0))],
        out_shape=[jax.ShapeDtypeStruct((L, D, W), F32)] * 4 + [jax.ShapeDtypeStruct((8, D), F32)],
        compiler_params=_params(("arbitrary", "arbitrary")),
    )(cond16, cond16.T, dm_x, dm_c, w_mod, m_mod, v_mod)


def _token_spec(tm, D):
    if tm > 256:
        return pl.BlockSpec((tm, D), lambda i, j: (i, 0), pipeline_mode=pl.Buffered(1))
    return pl.BlockSpec((tm, D), lambda i, j: (i, 0))


def ffn_fwd(x, mod, gn, wg, wu, wd, *, tm, t_first, name):
    T, D = x.shape
    nb, cw, _ = wg.shape
    G = mod.shape[0]
    nt = T // tm
    first = t_first // tm

    def body(x_ref, mod_ref, g_ref, wg_ref, wu_ref, wd_ref, xo_ref, h_ref, a_ref, b_ref, y_ref, hs, acc, a_s, b_s, s_s):
        j = pl.program_id(1)

        @pl.when(j == 0)
        def _():
            def head(r, carry):
                rows = pl.ds(pl.multiple_of(r * ROW_CHUNK, ROW_CHUNK), ROW_CHUNK)
                hb = _norm_mod(x_ref[rows, :], g_ref[...], mod_ref[0:1, :], mod_ref[1:2, :]).astype(BF16)
                hs[rows, :] = hb
                h_ref[rows, :] = hb
                return carry

            lax.fori_loop(0, tm // ROW_CHUNK, head, 0)
            acc[...] = jnp.zeros_like(acc)

        parts = [slice(r, r + MM_ROWS) for r in range(0, tm, MM_ROWS)]
        for rows in parts:
            a_s[rows, :] = _nt(hs[rows, :], wg_ref[...])
            b_s[rows, :] = _nt(hs[rows, :], wu_ref[...])
        for r in range(0, tm, ROW_CHUNK):
            rows = slice(r, r + ROW_CHUNK)
            av, bv = a_s[rows, :], b_s[rows, :]
            a_ref[rows, :] = av.astype(BF16)
            b_ref[rows, :] = bv.astype(BF16)
            _, sl = _silu_parts(av)
            s_s[rows, :] = (sl * bv).astype(BF16)
        for rows in parts:
            acc[rows, :] += _nn(s_s[rows, :], wd_ref[...])

        @pl.when(j == nb - 1)
        def _():
            def tail(r, carry):
                rows = pl.ds(pl.multiple_of(r * ROW_CHUNK, ROW_CHUNK), ROW_CHUNK)
                y = acc[rows, :]
                y_ref[rows, :] = y.astype(BF16)
                xo_ref[rows, :] = x_ref[rows, :] + (MACARON * mod_ref[2:3, :]) * y
                return carry

            lax.fori_loop(0, tm // ROW_CHUNK, tail, 0)

    tok = _token_spec(tm, D)
    act = pl.BlockSpec((None, tm, cw), lambda i, j: (j, i, 0))
    wblk = pl.BlockSpec((None, cw, D), lambda i, j: (j, 0, 0))
    return pl.pallas_call(
        body, name=name, grid=(nt, nb),
        in_specs=[tok, pl.BlockSpec((None, 3, D), lambda i, j: (_group_index(i, first, G), 0, 0)),
                  pl.BlockSpec((1, D), lambda i, j: (0, 0)), wblk, wblk, wblk],
        out_specs=[tok, tok, act, act, tok],
        out_shape=[jax.ShapeDtypeStruct((T, D), F32), jax.ShapeDtypeStruct((T, D), BF16),
                   jax.ShapeDtypeStruct((nb, T, cw), BF16), jax.ShapeDtypeStruct((nb, T, cw), BF16),
                   jax.ShapeDtypeStruct((T, D), BF16)],
        scratch_shapes=[pltpu.VMEM((tm, D), BF16), pltpu.VMEM((tm, D), F32), pltpu.VMEM((tm, cw), F32),
                        pltpu.VMEM((tm, cw), F32), pltpu.VMEM((tm, cw), BF16)],
        compiler_params=_params(("arbitrary", "arbitrary")),
    )(x, mod, gn, wg, wu, wd)


def ffn_bwd_tok(x, dxo, y, a, b, mod, gn, wg, wu, wd, *, tm, t_first, name):
    T, D = x.shape
    nb, cw, _ = wg.shape
    G = mod.shape[0]
    nt = T // tm
    first = t_first // tm

    def body(x_ref, dxo_ref, y_ref, a_ref, b_ref, mod_ref, g_ref, wg_ref, wu_ref, wd_ref,
             dx_ref, da_ref, db_ref, s_ref, dy_ref, red_ref, dys, dh, ds_s):
        i, j = pl.program_id(0), pl.program_id(1)

        @pl.when(j == 0)
        def _():
            def head(r, carry):
                rows = pl.ds(pl.multiple_of(r * ROW_CHUNK, ROW_CHUNK), ROW_CHUNK)
                dyb = ((MACARON * mod_ref[2:3, :]) * dxo_ref[rows, :]).astype(BF16)
                dys[rows, :] = dyb
                dy_ref[rows, :] = dyb
                return carry

            lax.fori_loop(0, tm // ROW_CHUNK, head, 0)
            dh[...] = jnp.zeros_like(dh)

        parts = [slice(r, r + MM_ROWS) for r in range(0, tm, MM_ROWS)]
        for rows in parts:
            ds_s[rows, :] = _nt(dys[rows, :], wd_ref[...])
        for r in range(0, tm, ROW_CHUNK):
            rows = slice(r, r + ROW_CHUNK)
            av = a_ref[rows, :].astype(F32)
            bv = b_ref[rows, :].astype(F32)
            ds = ds_s[rows, :]
            sig, sl = _silu_parts(av)
            s_ref[rows, :] = (sl * bv).astype(BF16)
            da_ref[rows, :] = (ds * bv * _dsilu(av, sig)).astype(BF16)
            db_ref[rows, :] = (ds * sl).astype(BF16)
        for rows in parts:
            dh[rows, :] += _nn(da_ref[rows, :], wg_ref[...]) + _nn(db_ref[rows, :], wu_ref[...])

        @pl.when((j == 0) & ((i == 0) | (i == first)))
        def _():
            red_ref[...] = jnp.zeros_like(red_ref)

        @pl.when(j == nb - 1)
        def _():
            def tail(r, carry):
                rows = pl.ds(pl.multiple_of(r * ROW_CHUNK, ROW_CHUNK), ROW_CHUNK)
                dxo_v = dxo_ref[rows, :]
                dxn, dshift, dscale, dg = _norm_mod_bwd(x_ref[rows, :], dh[rows, :], g_ref[...], mod_ref[1:2, :])
                dx_ref[rows, :] = dxo_v + dxn
                red_ref[0:1, :] += dshift
                red_ref[1:2, :] += dscale
                red_ref[2:3, :] += jnp.sum((MACARON * dxo_v) * y_ref[rows, :].astype(F32), axis=0, keepdims=True)
                red_ref[3:4, :] += dg
                return carry

            lax.fori_loop(0, tm // ROW_CHUNK, tail, 0)

    tok = _token_spec(tm, D)
    act = pl.BlockSpec((None, tm, cw), lambda i, j: (j, i, 0))
    wblk = pl.BlockSpec((None, cw, D), lambda i, j: (j, 0, 0))
    return pl.pallas_call(
        body, name=name, grid=(nt, nb),
        in_specs=[tok, tok, tok, act, act,
                  pl.BlockSpec((None, 3, D), lambda i, j: (_group_index(i, first, G), 0, 0)),
                  pl.BlockSpec((1, D), lambda i, j: (0, 0)), wblk, wblk, wblk],
        out_specs=[tok, act, act, act, tok,
                   pl.BlockSpec((None, 8, D), lambda i, j: (_group_index(i, first, G), 0, 0))],
        out_shape=[jax.ShapeDtypeStruct((T, D), F32)] + [jax.ShapeDtypeStruct((nb, T, cw), BF16)] * 3
        + [jax.ShapeDtypeStruct((T, D), BF16), jax.ShapeDtypeStruct((G, 8, D), F32)],
        scratch_shapes=[pltpu.VMEM((tm, D), BF16), pltpu.VMEM((tm, D), F32), pltpu.VMEM((tm, cw), F32)],
        compiler_params=_params(("arbitrary", "arbitrary")),
    )(x, dxo, y, a, b, mod, gn, wg, wu, wd)


def matmul_tn(a, b, *, nb, m, n, a_blocked, b_blocked, tk, name, after=None):
    T = a.shape[-2]
    extra = [] if after is None else [after]

    def spec(arr, blocked, width):
        if blocked:
            return pl.BlockSpec((None, tk, width), lambda j, k: (j, k, 0))
        if arr.shape[-1] == width:
            return pl.BlockSpec((tk, width), lambda j, k: (k, 0))
        return pl.BlockSpec((tk, width), lambda j, k: (k, j))

    nk = T // tk

    def body(a_ref, b_ref, *rest):
        if nk == 1:
            rest[-1][...] = _tn(a_ref[...], b_ref[...]).astype(BF16)
            return
        o_ref, acc = rest[-2:]
        k = pl.program_id(1)

        @pl.when(k == 0)
        def _():
            acc[...] = jnp.zeros_like(acc)

        acc[...] += _tn(a_ref[...], b_ref[...])

        @pl.when(k == nk - 1)
        def _():
            o_ref[...] = acc[...].astype(BF16)

    return pl.pallas_call(
        body, name=name, grid=(nb, nk),
        in_specs=[spec(a, a_blocked, m), spec(b, b_blocked, n)] + [ANY] * len(extra),
        out_specs=pl.BlockSpec((None, m, n), lambda j, k: (j, 0, 0)),
        out_shape=jax.ShapeDtypeStruct((nb, m, n), BF16),
        scratch_shapes=[] if nk == 1 else [pltpu.VMEM((m, n), F32)],
        compiler_params=_params(("arbitrary", "arbitrary")),
    )(a, b, *extra)


def proj_in(x, mod, gn, w_in, *, tm, t_first, name):
    T, D = x.shape
    nb, _, cw = w_in.shape
    G = mod.shape[0]
    nt = T // tm
    first = t_first // tm

    def body(x_ref, mod_ref, g_ref, w_ref, p_ref, h_ref, hs):
        @pl.when(pl.program_id(1) == 0)
        def _():
            hb = _norm_mod(x_ref[...], g_ref[...], mod_ref[0:1, :], mod_ref[1:2, :]).astype(BF16)
            hs[...] = hb
            h_ref[...] = hb

        p_ref[...] = _nn(hs[...], w_ref[...])

    tok = pl.BlockSpec((tm, D), lambda i, j: (i, 0))
    return pl.pallas_call(
        body, name=name, grid=(nt, nb),
        in_specs=[tok, pl.BlockSpec((None, 3, D), lambda i, j: (_group_index(i, first, G), 0, 0)),
                  pl.BlockSpec((1, D), lambda i, j: (0, 0)),
                  pl.BlockSpec((None, D, cw), lambda i, j: (j, 0, 0))],
        out_specs=[pl.BlockSpec((tm, cw), lambda i, j: (i, j)), tok],
        out_shape=[jax.ShapeDtypeStruct((T, nb * cw), F32), jax.ShapeDtypeStruct((T, D), BF16)],
        scratch_shapes=[pltpu.VMEM((tm, D), BF16)],
        compiler_params=_params(("arbitrary", "arbitrary")),
    )(x, mod, gn, w_in)


def proj_bwd_tok(x, dxo, dp, mod, gn, w_in, *, tm, name):
    T, D = x.shape
    nb, _, cw = w_in.shape
    G = mod.shape[0]
    nt = T // tm
    first = dxo.shape[0] // tm

    def body(x_ref, dxo_ref, dp_ref, mod_ref, g_ref, w_ref, dx_ref, red_ref, dh):
        i, j = pl.program_id(0), pl.program_id(1)

        @pl.when(j == 0)
        def _():
            dh[...] = jnp.zeros_like(dh)

        dh[...] += _nt(dp_ref[...], w_ref[...])

        @pl.when((j == 0) & ((i == 0) | (i == first)))
        def _():
            red_ref[...] = jnp.zeros_like(red_ref)

        @pl.when(j == nb - 1)
        def _():
            dxn, dshift, dscale, dg = _norm_mod_bwd(x_ref[...], dh[...], g_ref[...], mod_ref[1:2, :])
            dx_ref[...] = jnp.where(i < first, dxo_ref[...], 0.0) + dxn
            red_ref[0:1, :] += dshift
            red_ref[1:2, :] += dscale
            red_ref[3:4, :] += dg

    tok = pl.BlockSpec((tm, D), lambda i, j: (i, 0))
    return pl.pallas_call(
        body, name=name, grid=(nt, nb),
        in_specs=[tok, pl.BlockSpec((tm, D), lambda i, j: (jnp.minimum(i, first - 1), 0)),
                  pl.BlockSpec((tm, cw), lambda i, j: (i, j)),
                  pl.BlockSpec((None, 3, D), lambda i, j: (_group_index(i, first, G), 0, 0)),
                  pl.BlockSpec((1, D), lambda i, j: (0, 0)),
                  pl.BlockSpec((None, D, cw), lambda i, j: (j, 0, 0))],
        out_specs=[tok, pl.BlockSpec((None, 8, D), lambda i, j: (_group_index(i, first, G), 0, 0))],
        out_shape=[jax.ShapeDtypeStruct((T, D), F32), jax.ShapeDtypeStruct((G, 8, D), F32)],
        scratch_shapes=[pltpu.VMEM((tm, D), F32)],
        compiler_params=_params(("arbitrary", "arbitrary")),
    )(x, dxo, dp, mod, gn, w_in)


def _rope(t, cos, sin):
    return t * cos + pltpu.roll(t, HEAD_DIM // 2, axis=1) * sin


def ret_prep(p, cos, sin, *, heads, tm, name):
    T = p.shape[0]

    def body(q_ref, k_ref, v_ref, c_ref, s_ref, qo, ko, vo):
        cos_v, sin_v = c_ref[...], s_ref[...]
        qo[...] = _rope(q_ref[...], cos_v, sin_v).astype(BF16)
        ko[...] = _rope(k_ref[...] * K_SCALE, cos_v, sin_v).astype(BF16)
        vo[...] = v_ref[...].astype(BF16)

    def col(part):
        return pl.BlockSpec((tm, HEAD_DIM), lambda h, i: (i, part * heads + h))

    tab = pl.BlockSpec((tm, HEAD_DIM), lambda h, i: (i, 0))
    out = pl.BlockSpec((None, tm, HEAD_DIM), lambda h, i: (h, i, 0))
    return pl.pallas_call(
        body, name=name, grid=(heads, T // tm),
        in_specs=[col(0), col(1), col(2), tab, tab], out_specs=[out, out, out],
        out_shape=[jax.ShapeDtypeStruct((heads, T, HEAD_DIM), BF16)] * 3,
        compiler_params=_params(("arbitrary", "arbitrary")),
    )(p, p, p, cos, sin)


def _decay(n, m, lgf, lgb, t_lat, t_ctx):
    df = jnp.where(m < t_lat, n - m, n - m + (t_lat + t_ctx))
    db = m - n
    ef = jnp.where(df >= 0, jnp.exp(lgf * df), 0.0)
    eb = jnp.where(db >= 0, jnp.exp(lgb * db), 0.0)
    return ef, eb, df, db


def ret_fwd(q, k, v, lg, *, t_lat, tq, tk, name):
    H, T, _ = k.shape
    t_ctx = T - t_lat

    def body(lg_ref, q_ref, k_ref, v_ref, o_ref):
        h, qi, kj = pl.program_id(0), pl.program_id(1), pl.program_id(2)

        @pl.when(kj == 0)
        def _():
            o_ref[...] = jnp.zeros_like(o_ref)

        s = _nt(q_ref[...], k_ref[...])
        n = (qi * tq + lax.broadcasted_iota(jnp.int32, (tq, tk), 0)).astype(F32)
        m = (kj * tk + lax.broadcasted_iota(jnp.int32, (tq, tk), 1)).astype(F32)
        ef, eb, _, _ = _decay(n, m, lg_ref[0, h], lg_ref[1, h], t_lat, t_ctx)
        o_ref[...] += _nn((s * (ef + eb)).astype(BF16), v_ref[...])

    return pl.pallas_call(
        body, name=name, grid=(H, t_lat // tq, T // tk),
        in_specs=[pl.BlockSpec(memory_space=pltpu.SMEM),
                  pl.BlockSpec((None, tq, HEAD_DIM), lambda h, i, j: (h, i, 0)),
                  pl.BlockSpec((None, tk, HEAD_DIM), lambda h, i, j: (h, j, 0)),
                  pl.BlockSpec((None, tk, HEAD_DIM), lambda h, i, j: (h, j, 0))],
        out_specs=pl.BlockSpec((None, tq, HEAD_DIM), lambda h, i, j: (h, i, 0)),
        out_shape=jax.ShapeDtypeStruct((H, t_lat, HEAD_DIM), F32),
        compiler_params=_params(("arbitrary", "arbitrary", "arbitrary")),
    )(lg, q, k, v)


def ret_bwd(q, k, v, do, lg, *, t_lat, tq, tk, name):
    H, T, _ = k.shape
    t_ctx = T - t_lat

    def body(lg_ref, q_ref, k_ref, v_ref, do_ref, dq_ref, dk_ref, dv_ref, dlg_ref):
        h, kj, qi = pl.program_id(0), pl.program_id(1), pl.program_id(2)

        @pl.when((kj == 0) & (qi == 0))
        def _():
            dq_ref[...] = jnp.zeros_like(dq_ref)
            dlg_ref[...] = jnp.zeros_like(dlg_ref)

        @pl.when(qi == 0)
        def _():
            dk_ref[...] = jnp.zeros_like(dk_ref)
            dv_ref[...] = jnp.zeros_like(dv_ref)

        qv, kv, vv = q_ref[...], k_ref[...], v_ref[...]
        dob = do_ref[...].astype(BF16)
        st = _nt(kv, qv)
        dwt = _nt(vv, dob)
        m = (kj * tk + lax.broadcasted_iota(jnp.int32, (tk, tq), 0)).astype(F32)
        n = (qi * tq + lax.broadcasted_iota(jnp.int32, (tk, tq), 1)).astype(F32)
        ef, eb, df, db = _decay(n, m, lg_ref[0, h], lg_ref[1, h], t_lat, t_ctx)
        dec = ef + eb
        dv_ref[...] += _nn((st * dec).astype(BF16), dob)
        dst = (dwt * dec).astype(BF16)
        dk_ref[...] += _nn(dst, qv)
        rows = pl.ds(pl.multiple_of(qi * tq, tq), tq)
        dq_ref[rows, :] += _tn(dst, kv)
        gs = dwt * st
        dlg_ref[0:1, :] += jnp.sum(gs * (ef * df))
        dlg_ref[1:2, :] += jnp.sum(gs * (eb * db))

    kspec = pl.BlockSpec((None, tk, HEAD_DIM), lambda h, j, i: (h, j, 0))
    qspec = pl.BlockSpec((None, tq, HEAD_DIM), lambda h, j, i: (h, i, 0))
    return pl.pallas_call(
        body, name=name, grid=(H, T // tk, t_lat // tq),
        in_specs=[pl.BlockSpec(memory_space=pltpu.SMEM), qspec, kspec, kspec, qspec],
        out_specs=[pl.BlockSpec((None, t_lat, HEAD_DIM), lambda h, j, i: (h, 0, 0)), kspec, kspec,
                   pl.BlockSpec((None, 8, LANE), lambda h, j, i: (h, 0, 0))],
        out_shape=[jax.ShapeDtypeStruct((H, t_lat, HEAD_DIM), F32), jax.ShapeDtypeStruct((H, T, HEAD_DIM), F32),
                   jax.ShapeDtypeStruct((H, T, HEAD_DIM), F32), jax.ShapeDtypeStruct((H, 8, LANE), F32)],
        compiler_params=_params(("arbitrary", "arbitrary", "arbitrary")),
    )(lg, q, k, v, do)


def _group_norm(o):
    mu = jnp.mean(o, axis=-1, keepdims=True)
    ctr = o - mu
    r = lax.rsqrt(jnp.mean(ctr * ctr, axis=-1, keepdims=True) + GN_EPS)
    return ctr * r, r


def ret_post_fwd(o, p, *, heads, d_model, name):
    H, t_lat, _ = o.shape
    T = p.shape[0]

    def body(o_ref, g_ref, z_ref):
        on, _ = _group_norm(o_ref[...])
        _, sl = _silu_parts(g_ref[0:t_lat, :])
        z_ref[...] = (on * sl).astype(BF16)

    return pl.pallas_call(
        body, name=name, grid=(H,),
        in_specs=[pl.BlockSpec((None, t_lat, HEAD_DIM), lambda h: (h, 0, 0)),
                  pl.BlockSpec((T, HEAD_DIM), lambda h: (0, 3 * heads + h))],
        out_specs=pl.BlockSpec((t_lat, HEAD_DIM), lambda h: (0, h)),
        out_shape=jax.ShapeDtypeStruct((t_lat, d_model), BF16),
        compiler_params=_params(("arbitrary",)),
    )(o, p)


def _shift_rows(u, k):
    rows = u.shape[0]
    row = lax.broadcasted_iota(jnp.int32, u.shape, 0)
    rolled = pltpu.roll(u, k % rows, axis=0)
    return jnp.where((row >= k) & (row < rows + k), rolled, 0.0)


def conv_fwd(z, p, w_conv, *, heads, t_lat, name):
    T = p.shape[0]
    cb_n = w_conv.shape[1] // LANE
    base = 4 * heads

    def body(z_in, bg_ref, cg_ref, u_ref, w_ref, z_ref):
        cu = cg_ref[0:t_lat, :] * u_ref[0:t_lat, :]
        c3 = _shift_rows(cu, 1) * w_ref[0:1, :] + cu * w_ref[1:2, :] + _shift_rows(cu, -1) * w_ref[2:3, :]
        z_ref[...] = (bg_ref[0:t_lat, :] * c3).astype(BF16)

    def col(part):
        return pl.BlockSpec((T, LANE), lambda cb: (0, base + part * cb_n + cb))

    return pl.pallas_call(
        body, name=name, grid=(cb_n,),
        in_specs=[ANY, col(0), col(1), col(2), pl.BlockSpec((CONV_WIDTH, LANE), lambda cb: (0, cb))],
        out_specs=pl.BlockSpec((t_lat, LANE), lambda cb: (0, heads + cb)),
        out_shape=jax.ShapeDtypeStruct(z.shape, z.dtype),
        input_output_aliases={0: 0},
        compiler_params=_params(("arbitrary",)),
    )(z, p, p, p, w_conv)


def out_proj(z, w_out, x, mod, *, tm, name):
    t_lat, D = z.shape

    def body(z_ref, w_ref, x_ref, mod_ref, xo_ref, y_ref):
        y = _nn(z_ref[...], w_ref[...])
        y_ref[...] = y.astype(BF16)
        xo_ref[...] = x_ref[...] + mod_ref[2:3, :] * y

    tok = pl.BlockSpec((tm, D), lambda i: (i, 0))
    return pl.pallas_call(
        body, name=name, grid=(t_lat // tm,),
        in_specs=[tok, pl.BlockSpec((D, D), lambda i: (0, 0)), tok, pl.BlockSpec((None, 3, D), lambda i: (0, 0, 0))],
        out_specs=[tok, tok],
        out_shape=[jax.ShapeDtypeStruct((t_lat, D), F32), jax.ShapeDtypeStruct((t_lat, D), BF16)],
        compiler_params=_params(("arbitrary",)),
    )(z, w_out, x, mod)


def out_proj_bwd(dxo, y, w_out, mod, *, tm, name):
    t_lat, D = dxo.shape

    def body(dxo_ref, y_ref, w_ref, mod_ref, dz_ref, dy_ref, red_ref):
        @pl.when(pl.program_id(0) == 0)
        def _():
            red_ref[...] = jnp.zeros_like(red_ref)

        dxo_v = dxo_ref[...]
        dyb = (mod_ref[2:3, :] * dxo_v).astype(BF16)
        dy_ref[...] = dyb
        dz_ref[...] = _nt(dyb, w_ref[...])
        red_ref[2:3, :] += jnp.sum(dxo_v * y_ref[...].astype(F32), axis=0, keepdims=True)

    tok = pl.BlockSpec((tm, D), lambda i: (i, 0))
    return pl.pallas_call(
        body, name=name, grid=(t_lat // tm,),
        in_specs=[tok, tok, pl.BlockSpec((D, D), lambda i: (0, 0)), pl.BlockSpec((None, 3, D), lambda i: (0, 0, 0))],
        out_specs=[tok, tok, pl.BlockSpec((8, D), lambda i: (0, 0))],
        out_shape=[jax.ShapeDtypeStruct((t_lat, D), F32), jax.ShapeDtypeStruct((t_lat, D), BF16),
                   jax.ShapeDtypeStruct((8, D), F32)],
        compiler_params=_params(("arbitrary",)),
    )(dxo, y, w_out, mod)


def ret_post_bwd(dz, o, p, *, heads, name):
    H, t_lat, _ = o.shape
    T, in_w = p.shape

    def body(dz_ref, o_ref, g_ref, do_ref, dp_ref):
        on, r = _group_norm(o_ref[...])
        gg = g_ref[0:t_lat, :]
        sig, sl = _silu_parts(gg)
        dret = dz_ref[...]
        don = dret * sl
        do_ref[...] = r * (don - jnp.mean(don, axis=-1, keepdims=True)
                           - on * jnp.mean(don * on, axis=-1, keepdims=True))
        dp_ref[0:t_lat, :] = (dret * on * _dsilu(gg, sig)).astype(BF16)
        dp_ref[t_lat:T, :] = jnp.zeros((T - t_lat, HEAD_DIM), BF16)

    return pl.pallas_call(
        body, name=name, grid=(H,),
        in_specs=[pl.BlockSpec((t_lat, HEAD_DIM), lambda h: (0, h)),
                  pl.BlockSpec((None, t_lat, HEAD_DIM), lambda h: (h, 0, 0)),
                  pl.BlockSpec((T, HEAD_DIM), lambda h: (0, 3 * heads + h))],
        out_specs=[pl.BlockSpec((None, t_lat, HEAD_DIM), lambda h: (h, 0, 0)),
                   pl.BlockSpec((T, HEAD_DIM), lambda h: (0, 3 * heads + h))],
        out_shape=[jax.ShapeDtypeStruct((H, t_lat, HEAD_DIM), F32), jax.ShapeDtypeStruct((T, in_w), BF16)],
        compiler_params=_params(("arbitrary",)),
    )(dz, o, p)


def conv_bwd(dp, dz, p, w_conv, *, heads, t_lat, name):
    T = p.shape[0]
    cb_n = w_conv.shape[1] // LANE
    base = 4 * heads

    def body(dp_in, dz_ref, bg_ref, cg_ref, u_ref, w_ref, dp_ref, dw_ref):
        part = pl.program_id(1)
        cg, u = cg_ref[0:t_lat, :], u_ref[0:t_lat, :]
        cu = cg * u
        dconv = dz_ref[...]
        dp_ref[t_lat:T, :] = jnp.zeros((T - t_lat, LANE), BF16)

        @pl.when(part == 0)
        def _():
            c3 = _shift_rows(cu, 1) * w_ref[0:1, :] + cu * w_ref[1:2, :] + _shift_rows(cu, -1) * w_ref[2:3, :]
            dp_ref[0:t_lat, :] = (dconv * c3).astype(BF16)
            dc3 = dconv * bg_ref[0:t_lat, :]
            dw_ref[0:1, :] = jnp.sum(dc3 * _shift_rows(cu, 1), axis=0, keepdims=True)
            dw_ref[1:2, :] = jnp.sum(dc3 * cu, axis=0, keepdims=True)
            dw_ref[2:3, :] = jnp.sum(dc3 * _shift_rows(cu, -1), axis=0, keepdims=True)

        @pl.when(part > 0)
        def _():
            dc3 = dconv * bg_ref[0:t_lat, :]
            dcu = (_shift_rows(dc3, -1) * w_ref[0:1, :] + dc3 * w_ref[1:2, :] + _shift_rows(dc3, 1) * w_ref[2:3, :])
            dp_ref[0:t_lat, :] = (dcu * jnp.where(part == 1, u, cg)).astype(BF16)

    def col(part):
        return pl.BlockSpec((T, LANE), lambda cb, pt: (0, base + part * cb_n + cb))

    return pl.pallas_call(
        body, name=name, grid=(cb_n, 3),
        in_specs=[ANY, pl.BlockSpec((t_lat, LANE), lambda cb, pt: (0, heads + cb)), col(0), col(1), col(2),
                  pl.BlockSpec((CONV_WIDTH, LANE), lambda cb, pt: (0, cb))],
        out_specs=[pl.BlockSpec((T, LANE), lambda cb, pt: (0, base + pt * cb_n + cb)),
                   pl.BlockSpec((CONV_WIDTH, LANE), lambda cb, pt: (0, cb))],
        out_shape=[jax.ShapeDtypeStruct(dp.shape, dp.dtype), jax.ShapeDtypeStruct(w_conv.shape, F32)],
        input_output_aliases={0: 0},
        compiler_params=_params(("arbitrary", "arbitrary")),
    )(dp, dz, p, p, p, w_conv)


def ret_unprep(dp, dq, dk, dv, cos, sin, *, t_lat, tm, name):
    H, T, _ = dk.shape
    del tm

    def body(dp_in, dq_ref, dk_ref, dv_ref, c_ref, s_ref, dp_ref):
        part = pl.program_id(0)

        @pl.when(part == 0)
        def _():
            dp_ref[0:t_lat, :] = _rope(dq_ref[...], c_ref[0:t_lat, :], -s_ref[0:t_lat, :]).astype(BF16)
            dp_ref[t_lat:T, :] = jnp.zeros((T - t_lat, HEAD_DIM), BF16)

        @pl.when(part == 1)
        def _():
            dp_ref[...] = (_rope(dk_ref[...], c_ref[...], -s_ref[...]) * K_SCALE).astype(BF16)

        @pl.when(part == 2)
        def _():
            dp_ref[...] = dv_ref[...].astype(BF16)

    def head_of(part):
        return lambda pt, h: (jnp.where(pt == part, h, 0), 0, 0)

    tab = pl.BlockSpec((T, HEAD_DIM), lambda pt, h: (0, 0))
    return pl.pallas_call(
        body, name=name, grid=(3, H),
        in_specs=[ANY, pl.BlockSpec((None, t_lat, HEAD_DIM), head_of(0)), pl.BlockSpec((None, T, HEAD_DIM), head_of(1)),
                  pl.BlockSpec((None, T, HEAD_DIM), head_of(2)), tab, tab],
        out_specs=pl.BlockSpec((T, HEAD_DIM), lambda pt, h: (0, pt * H + h)),
        out_shape=jax.ShapeDtypeStruct(dp.shape, dp.dtype),
        input_output_aliases={0: 0},
        compiler_params=_params(("arbitrary", "arbitrary")),
    )(dp, dq, dk, dv, cos, sin)


def norm_mod_fwd(x, mod, gn, *, tm, name):
    T, D = x.shape

    def body(x_ref, mod_ref, g_ref, h_ref):
        h_ref[...] = _norm_mod(x_ref[...], g_ref[...], mod_ref[0:1, :], mod_ref[1:2, :])

    tok = pl.BlockSpec((tm, D), lambda i: (i, 0))
    return pl.pallas_call(
        body, name=name, grid=(T // tm,),
        in_specs=[tok, pl.BlockSpec((None, 3, D), lambda i: (0, 0, 0)), pl.BlockSpec((1, D), lambda i: (0, 0))],
        out_specs=tok, out_shape=jax.ShapeDtypeStruct((T, D), F32),
        compiler_params=_params(("arbitrary",)),
    )(x, mod, gn)


def norm_mod_bwd(x, dh, dxo, mod, gn, *, tm, name):
    T, D = x.shape

    def body(x_ref, dh_ref, dxo_ref, mod_ref, g_ref, dx_ref, red_ref):
        @pl.when(pl.program_id(0) == 0)
        def _():
            red_ref[...] = jnp.zeros_like(red_ref)

        dxn, dshift, dscale, dg = _norm_mod_bwd(x_ref[...], dh_ref[...], g_ref[...], mod_ref[1:2, :])
        dx_ref[...] = dxo_ref[...] + dxn
        red_ref[0:1, :] += dshift
        red_ref[1:2, :] += dscale
        red_ref[3:4, :] += dg

    tok = pl.BlockSpec((tm, D), lambda i: (i, 0))
    return pl.pallas_call(
        body, name=name, grid=(T // tm,),
        in_specs=[tok, tok, tok, pl.BlockSpec((None, 3, D), lambda i: (0, 0, 0)), pl.BlockSpec((1, D), lambda i: (0, 0))],
        out_specs=[tok, pl.BlockSpec((8, D), lambda i: (0, 0))],
        out_shape=[jax.ShapeDtypeStruct((T, D), F32), jax.ShapeDtypeStruct((8, D), F32)],
        compiler_params=_params(("arbitrary",)),
    )(x, dh, dxo, mod, gn)


def _window_sum(u, w, lead):
    T, C = u.shape
    ext = jnp.concatenate([u, jnp.zeros((POOL_PAD, C), F32)], axis=0)
    k = 1
    while k < w:
        ext = ext + _shift_rows(ext, k)
        k *= 2
    return _shift_rows(ext, -lead)[0:T, :]


def _window_count(T, C, w):
    t = lax.broadcasted_iota(jnp.int32, (T, C), 0)
    lo = jnp.clip(t - w // 2, 0, T)
    hi = jnp.clip(t + (w - w // 2), 0, T)
    return (hi - lo).astype(F32)


def pool_fwd(h, x, pool_w, scale, mod, *, name):
    T, D = h.shape
    G, Cg, _ = pool_w.shape
    ns = Cg // LANE

    def body(h_ref, x_ref, w_ref, sc_ref, mod_ref, xo_ref, pl_ref, yl_ref, acc):
        g, s = pl.program_id(0), pl.program_id(1)
        hv = h_ref[...]
        for gi, win in enumerate(POOL_WINDOWS):
            @pl.when(g == gi)
            def _():
                mean = _window_sum(hv, win, win // 2 - 1) / _window_count(T, LANE, win)
                pooled = (mean - hv).astype(BF16)
                pl_ref[...] = pooled
                contrib = _nn(pooled, w_ref[...])

                @pl.when(s == 0)
                def _():
                    acc[...] = contrib

                @pl.when(s > 0)
                def _():
                    acc[...] += contrib

        @pl.when(s == ns - 1)
        def _():
            yl = acc[...]
            yl_ref[...] = yl.astype(BF16)
            xo_ref[...] = x_ref[...] + mod_ref[2:3, :] * (yl * sc_ref[...])

    grp = pl.BlockSpec((T, Cg), lambda g, s: (0, g))
    sub = pl.BlockSpec((T, LANE), lambda g, s: (0, g * ns + s))
    return pl.pallas_call(
        body, name=name, grid=(G, ns),
        in_specs=[sub, grp, pl.BlockSpec((None, LANE, Cg), lambda g, s: (g, s, 0)),
                  pl.BlockSpec((1, Cg), lambda g, s: (0, g)), pl.BlockSpec((None, 3, Cg), lambda g, s: (0, 0, g))],
        out_specs=[grp, sub, grp],
        out_shape=[jax.ShapeDtypeStruct((T, D), F32), jax.ShapeDtypeStruct((T, D), BF16),
                   jax.ShapeDtypeStruct((T, D), BF16)],
        scratch_shapes=[pltpu.VMEM((T, Cg), F32)],
        compiler_params=_params(("arbitrary", "arbitrary")),
    )(h, x, pool_w, scale, mod)


def pool_bwd(dxo, pooled, yl, pool_w, scale, mod, *, name):
    T, D = dxo.shape
    G, Cg, _ = pool_w.shape
    ns = Cg // LANE

    def body(dxo_ref, pl_ref, yl_ref, w_ref, sc_ref, mod_ref, dh_ref, dw_ref, red_ref, dyl):
        g, s = pl.program_id(0), pl.program_id(1)

        @pl.when(s == 0)
        def _():
            dxo_v = dxo_ref[...]
            ylv = yl_ref[...].astype(F32)
            dy = mod_ref[2:3, :] * dxo_v
            dyl[...] = (dy * sc_ref[...]).astype(BF16)
            red_ref[...] = jnp.zeros_like(red_ref)
            red_ref[2:3, :] = jnp.sum(dxo_v * (ylv * sc_ref[...]), axis=0, keepdims=True)
            red_ref[4:5, :] = jnp.sum(dy * ylv, axis=0, keepdims=True)

        dylv = dyl[...]
        dpool = _nt(dylv, w_ref[...])
        dw_ref[...] = _tn(pl_ref[...], dylv).astype(BF16)
        for gi, win in enumerate(POOL_WINDOWS):
            @pl.when(g == gi)
            def _():
                spread = _window_sum(dpool / _window_count(T, LANE, win), win, win // 2)
                dh_ref[...] = spread - dpool

    grp = pl.BlockSpec((T, Cg), lambda g, s: (0, g))
    sub = pl.BlockSpec((T, LANE), lambda g, s: (0, g * ns + s))
    wsub = pl.BlockSpec((None, LANE, Cg), lambda g, s: (g, s, 0))
    return pl.pallas_call(
        body, name=name, grid=(G, ns),
        in_specs=[grp, sub, grp, wsub, pl.BlockSpec((1, Cg), lambda g, s: (0, g)),
                  pl.BlockSpec((None, 3, Cg), lambda g, s: (0, 0, g))],
        out_specs=[sub, wsub, pl.BlockSpec((8, Cg), lambda g, s: (0, g))],
        out_shape=[jax.ShapeDtypeStruct((T, D), F32), jax.ShapeDtypeStruct((G, Cg, Cg), BF16),
                   jax.ShapeDtypeStruct((8, D), F32)],
        scratch_shapes=[pltpu.VMEM((T, Cg), BF16)],
        compiler_params=_params(("arbitrary", "arbitrary")),
    )(dxo, pooled, yl, pool_w, scale, mod)


def final_loss(x, gn, target, *, tm, name):
    T, D = x.shape

    def body(x_ref, g_ref, t_ref, loss_ref, dx_ref, red_ref):
        @pl.when(pl.program_id(0) == 0)
        def _():
            loss_ref[...] = jnp.zeros_like(loss_ref)
            red_ref[...] = jnp.zeros_like(red_ref)

        xx, g = x_ref[...], g_ref[...]
        r = lax.rsqrt(jnp.mean(xx * xx, axis=-1, keepdims=True) + EPS)
        xhat = xx * r
        err = xhat * g - t_ref[...]
        loss_ref[...] += 0.5 * jnp.sum(jnp.mean(err * err, axis=-1, keepdims=True))
        dy = err / D
        red_ref[0:1, :] += jnp.sum(dy * xhat, axis=0, keepdims=True)
        dxh = dy * g
        dx_ref[...] = r * (dxh - xhat * jnp.mean(dxh * xhat, axis=-1, keepdims=True))

    tok = pl.BlockSpec((tm, D), lambda i: (i, 0))
    return pl.pallas_call(
        body, name=name, grid=(T // tm,),
        in_specs=[tok, pl.BlockSpec((1, D), lambda i: (0, 0)), tok],
        out_specs=[pl.BlockSpec((8, LANE), lambda i: (0, 0)), tok, pl.BlockSpec((8, D), lambda i: (0, 0))],
        out_shape=[jax.ShapeDtypeStruct((8, LANE), F32), jax.ShapeDtypeStruct((T, D), F32),
                   jax.ShapeDtypeStruct((8, D), F32)],
        compiler_params=_params(("arbitrary",)),
    )(x, gn, target)


def _rope_tables(t_lat, t_ctx):
    quarter = HEAD_DIM // 4
    pos = jnp.arange(t_lat)
    inv = ROPE_BASE ** (-jnp.arange(quarter, dtype=F32) / quarter)
    ang = jnp.concatenate([(pos // GRID_W).astype(F32)[:, None] * inv, (pos % GRID_W).astype(F32)[:, None] * inv], axis=-1)
    cos, sin = jnp.cos(ang), jnp.sin(ang)
    cos = jnp.concatenate([jnp.concatenate([cos, cos], axis=-1), jnp.ones((t_ctx, HEAD_DIM), F32)], axis=0)
    sin = jnp.concatenate([jnp.concatenate([-sin, sin], axis=-1), jnp.zeros((t_ctx, HEAD_DIM), F32)], axis=0)
    return cos, sin


def _ffn_grads(h, da, db, s, dy, tag):
    nb, T, cw = da.shape
    D = h.shape[1]
    tk = T
    g_gate = matmul_tn(da, h, nb=nb, m=cw, n=D, a_blocked=True, b_blocked=False, tk=tk, name=f"wgrad_gate_{tag}")
    g_up = matmul_tn(db, h, nb=nb, m=cw, n=D, a_blocked=True, b_blocked=False, tk=tk, name=f"wgrad_up_{tag}")
    g_down = matmul_tn(s, dy, nb=nb, m=cw, n=D, a_blocked=True, b_blocked=False, tk=tk, name=f"wgrad_down_{tag}")
    return [g_gate, g_up, g_down]


def kernel(x, c, ctx, c_ctx, w_mod, b_mod, norm_ffn1, norm_mix, norm_ffn2, ffn1_w_gate, ffn1_w_up, ffn1_w_down, ffn2_w_gate, ffn2_w_up, ffn2_w_down, mix_w_in, mix_w_conv, mix_w_out, ret_decay_fwd, ret_decay_bwd, pool_w, pool_scale, final_norm, loss_target, m_c_ctx, m_w_mod, m_b_mod, m_norm_ffn1, m_norm_mix, m_norm_ffn2, m_ffn1_w_gate, m_ffn1_w_up, m_ffn1_w_down, m_ffn2_w_gate, m_ffn2_w_up, m_ffn2_w_down, m_mix_w_in, m_mix_w_conv, m_mix_w_out, m_ret_decay_fwd, m_ret_decay_bwd, m_pool_w, m_pool_scale, m_final_norm, v_c_ctx, v_w_mod, v_b_mod, v_norm_ffn1, v_norm_mix, v_norm_ffn2, v_ffn1_w_gate, v_ffn1_w_up, v_ffn1_w_down, v_ffn2_w_gate, v_ffn2_w_up, v_ffn2_w_down, v_mix_w_in, v_mix_w_conv, v_mix_w_out, v_ret_decay_fwd, v_ret_decay_bwd, v_pool_w, v_pool_scale, v_final_norm):
    t_lat, D = x.shape[1], x.shape[2]
    t_ctx = ctx.shape[1]
    T = t_lat + t_ctx
    heads = ret_decay_fwd.shape[1]
    mod_w = w_mod.shape[2]
    tm = 256
    tmf = 512 if t_lat % 512 == 0 else 256
    tq = 512 if t_lat % 512 == 0 else 256
    tk = T // 3 if (T % 3 == 0 and (T // 3) % 256 == 0) else 256

    ax, ay, ac = lax.axis_index("x"), lax.axis_index("y"), lax.axis_index("c")
    me = 4 * ax + 2 * ay + ac
    own_block = jnp.reshape(me, (1,)).astype(jnp.int32)
    peer_blocks = jnp.stack([4 * (1 - ax) + 2 * ay + ac, 4 * ax + 2 * (1 - ay) + ac,
                             4 * (1 - ax) + 2 * (1 - ay) + ac]).astype(jnp.int32)

    (c_all,) = all_gather([c], name="gather_cond")
    cond16 = jnp.concatenate([c_all.reshape(N_DEV, D), c_ctx[None, :], jnp.zeros((7, D), F32)], axis=0)
    b_blk = lax.dynamic_slice_in_dim(b_mod, me * mod_w, mod_w, axis=1)[:, None, :]
    m_blk = adaln_fwd(cond16, w_mod, b_blk, name="adaln_fwd")
    m_all, w_conv, pscale = all_gather([m_blk, mix_w_conv[0], pool_scale], name="gather_mod")
    mods = jnp.transpose(m_all, (1, 2, 0, 3)).reshape(2, 16, N_MOD, D)
    mod_x = lax.dynamic_index_in_dim(mods, me, axis=1, keepdims=False)
    mod_c = mods[0, 8]
    w_conv = jnp.transpose(w_conv, (1, 0, 2)).reshape(CONV_WIDTH, -1)
    pscale = pscale.reshape(1, D)

    n_grp, grp_rows, grp_w = pool_w.shape[1:]
    ffn1_w = [(jnp.swapaxes(ffn1_w_gate, 1, 2), "gate"), (jnp.swapaxes(ffn1_w_up, 1, 2), "up"), (ffn1_w_down, "down")]
    ffn2_w = [(jnp.swapaxes(ffn2_w_gate, 1, 2), "gate"), (jnp.swapaxes(ffn2_w_up, 1, 2), "up"), (ffn2_w_down, "down")]
    groups = {
        "01": [(w, 0, f"ffn1_{k}") for w, k in ffn1_w],
        "mix": [(mix_w_in, 0, "w_in"), (mix_w_out, 0, "w_out")],
        "02": [(w, 0, f"ffn2_{k}") for w, k in ffn2_w],
        "11": [(w, 1, f"ffn1_{k}") for w, k in ffn1_w] + [(pool_w.reshape(1, n_grp * grp_rows, grp_w), 0, "pool_w")],
        "12": [(w, 1, f"ffn2_{k}") for w, k in ffn2_w],
    }
    started = {}

    def start(tag, after):
        lands = [cast_place(w, l, own_block, name=f"cast_{nm}_{l}") for w, l, nm in groups[tag]]
        started[tag] = gather_start(lands, after, name=f"gather_start_{tag}")
        return started[tag][-1][0:1, 0:1]

    def finish(tag, after):
        return gather_finish(gather_wait(started[tag], after, name=f"gather_wait_{tag}"), name=f"gather_finish_{tag}")

    lg = jnp.concatenate([jax.nn.log_sigmoid(ret_decay_fwd), jax.nn.log_sigmoid(ret_decay_bwd)], axis=0)
    cos, sin = _rope_tables(t_lat, t_ctx)

    def mod3(l, k, with_ctx=False, tie=None):
        rows = mod_x[l, 3 * k:3 * k + 3][None]
        if with_ctx:
            rows = jnp.concatenate([rows, mod_c[3 * k:3 * k + 3][None]], axis=0)
        return rows if tie is None else rows + tie

    tie = start("01", m_all)
    x0 = jnp.concatenate([x[0], ctx[0]], axis=0) + tie
    wg01, wu01, wd01 = finish("01", x0)
    tie = start("mix", wd01)
    x1, h1, a1, b1, y1 = ffn_fwd(x0, mod3(0, 0, True, tie), norm_ffn1[0:1], wg01, wu01, wd01,
                                 tm=tm, t_first=t_lat, name="ffn_fwd_01")
    w_in, w_out = finish("mix", h1)
    tie = start("02", w_out)
    w_out = w_out.reshape(D, D)
    p, hm = proj_in(x1, mod3(0, 1, True, tie), norm_mix[0:1], w_in, tm=tm, t_first=t_lat, name="proj_in")
    qr, kr, vr = ret_prep(p, cos, sin, heads=heads, tm=tm, name="ret_prep")
    o = ret_fwd(qr, kr, vr, lg, t_lat=t_lat, tq=tq, tk=tk, name="ret_fwd")
    wg02, wu02, wd02 = finish("02", o)
    tie = start("11", wd02)
    z = ret_post_fwd(o, p, heads=heads, d_model=D, name="ret_post_fwd")
    z = conv_fwd(z, p, w_conv, heads=heads, t_lat=t_lat, name="conv_fwd")
    x2, ym = out_proj(z, w_out, x1, mod3(0, 1, tie=tie), tm=tm, name="out_proj")
    x3, h3, a3, b3, y3 = ffn_fwd(x2, mod3(0, 2, tie=tie), norm_ffn2[0:1], wg02, wu02, wd02, tm=tmf, t_first=t_lat, name="ffn_fwd_02")
    wg11, wu11, wd11, pw = finish("11", h3)
    tie = start("12", wd11)
    pw = jnp.transpose(pw.reshape(N_DEV, n_grp, grp_rows, grp_w), (1, 0, 2, 3)).reshape(n_grp, N_DEV * grp_rows, grp_w)
    x4, h4, a4, b4, y4 = ffn_fwd(x3, mod3(1, 0, tie=tie), norm_ffn1[1:2], wg11, wu11, wd11, tm=tmf, t_first=t_lat, name="ffn_fwd_11")
    hp = norm_mod_fwd(x4, mod3(1, 1), norm_mix[1:2], tm=tm, name="pool_norm_fwd")
    x5, pooled, yl = pool_fwd(hp, x4, pw, pscale, mod3(1, 1), name="pool_fwd")
    wg12, wu12, wd12 = finish("12", yl)
    x6, h6, a6, b6, y6 = ffn_fwd(x5, mod3(1, 2), norm_ffn2[1:2], wg12, wu12, wd12, tm=tmf, t_first=t_lat, name="ffn_fwd_12")
    loss_part, dx6, red_fn = final_loss(x6, final_norm[None, :], loss_target[0], tm=tm, name="final_loss")
    loss = lax.psum(loss_part[0, 0], ("x", "y", "c"))

    reducing = {}

    def reduce_start(tag, grads, after=None):
        res = rs_sibling_exchange(grads, grads[0] if after is None else after, name=f"rs_sibling_{tag}")
        n = len(grads)
        stages, directs = res[:n], res[n:]
        sums = [chip_sum(g, st, peer_blocks, name=f"chip_sum_{tag}_{t}") for t, (g, st) in enumerate(zip(grads, stages))]
        handle = chip_exchange_start(sums, name=f"rs_start_{tag}")
        reducing[tag] = (grads, directs, handle)
        return handle[-1][0:1, 0:1]

    dx5, da6, db6, s6, dy6, red12 = ffn_bwd_tok(x5, dx6, y6, a6, b6, mod3(1, 2), norm_ffn2[1:2], wg12, wu12, wd12,
                                                tm=tmf, t_first=t_lat, name="ffn_bwd_12")
    tie = reduce_start("12", _ffn_grads(h6, da6, db6, s6, dy6, "12"))
    dhp, g_pw, red_pool = pool_bwd(dx5, pooled, yl, pw, pscale, mod3(1, 1, tie=tie), name="pool_bwd")
    dx4, red_pn = norm_mod_bwd(x4, dhp, dx5, mod3(1, 1), norm_mix[1:2], tm=tm, name="pool_norm_bwd")
    dx3, da4, db4, s4, dy4, red11 = ffn_bwd_tok(x3, dx4, y4, a4, b4, mod3(1, 0), norm_ffn1[1:2], wg11, wu11, wd11,
                                                tm=tmf, t_first=t_lat, name="ffn_bwd_11")
    g_pw = jnp.transpose(g_pw.reshape(n_grp, N_DEV, grp_rows, grp_w), (1, 0, 2, 3)).reshape(N_DEV, n_grp * grp_rows, grp_w)
    tie = reduce_start("11", list(_ffn_grads(h4, da4, db4, s4, dy4, "11")) + [g_pw])
    dx2, da3, db3, s3, dy3, red02 = ffn_bwd_tok(x2, dx3, y3, a3, b3, mod3(0, 2, tie=tie), norm_ffn2[0:1], wg02, wu02, wd02,
                                                tm=tmf, t_first=t_lat, name="ffn_bwd_02")
    tie = reduce_start("02", _ffn_grads(h3, da3, db3, s3, dy3, "02"))
    dz, dym, red_op = out_proj_bwd(dx2, ym, w_out, mod3(0, 1, tie=tie), tm=tm, name="out_proj_bwd")
    g_wout = matmul_tn(z, dym, nb=N_DEV, m=D // N_DEV, n=D, a_blocked=False, b_blocked=False,
                       tk=t_lat, name="wgrad_out")
    do, dp = ret_post_bwd(dz, o, p, heads=heads, name="ret_post_bwd")
    dp, g_conv = conv_bwd(dp, dz, p, w_conv, heads=heads, t_lat=t_lat, name="conv_bwd")
    dq, dk, dv, dlg = ret_bwd(qr, kr, vr, do, lg, t_lat=t_lat, tq=tq, tk=tk, name="ret_bwd")
    dp = ret_unprep(dp, dq, dk, dv, cos, sin, t_lat=t_lat, tm=tm, name="ret_unprep")
    dx1, red_mix = proj_bwd_tok(x1, dx2, dp, mod3(0, 1, True), norm_mix[0:1], w_in, tm=tm, name="proj_bwd")
    g_win = matmul_tn(hm, dp, nb=N_DEV, m=D, n=w_in.shape[2], a_blocked=False, b_blocked=False, tk=T, name="wgrad_in")
    tie = reduce_start("mix", [g_win, g_wout])
    dx0, da1, db1, s1, dy1, red01 = ffn_bwd_tok(x0, dx1, y1, a1, b1, mod3(0, 0, True, tie), norm_ffn1[0:1], wg01, wu01, wd01,
                                                tm=tm, t_first=t_lat, name="ffn_bwd_01")
    res = {}

    dm_x = jnp.stack([jnp.concatenate([red01[0, 0:3], red_mix[0, 0:2], red_op[2:3], red02[0, 0:3]], axis=0),
                      jnp.concatenate([red11[0, 0:3], red_pn[0:2], red_pool[2:3], red12[0, 0:3]], axis=0)])
    dm_c = jnp.concatenate([red01[1, 0:3], red_mix[1, 0:2], jnp.zeros((4, D), F32)], axis=0)
    d_lg = dlg[:, 0:2, 0].T
    d_dec_f = d_lg[0:1] * jax.nn.sigmoid(-ret_decay_fwd)
    d_dec_b = d_lg[1:2] * jax.nn.sigmoid(-ret_decay_bwd)
    pieces = [dm_x.reshape(-1), dm_c.reshape(-1),
              jnp.stack([red01[0, 3] + red01[1, 3], red11[0, 3]]).reshape(-1),
              jnp.stack([red_mix[0, 3] + red_mix[1, 3], red_pn[3]]).reshape(-1),
              jnp.stack([red02[0, 3], red12[0, 3]]).reshape(-1),
              red_fn[0], d_dec_f.reshape(-1), d_dec_b.reshape(-1), g_conv.reshape(-1), red_pool[4]]
    sizes = [int(a.shape[0]) for a in pieces]
    n_pack = sum(sizes)
    n_pad = -n_pack % 8192
    packed = jnp.concatenate(pieces + [jnp.zeros((n_pad,), F32)])[None, :]
    (packed_all,) = all_gather([packed], name="gather_partials")
    total = sum_partials(packed_all, name="sum_partials")[0]
    offs = [0]
    for s in sizes:
        offs.append(offs[-1] + s)
    seg = [total[offs[i]:offs[i + 1]] for i in range(len(sizes))]
    g_dm = seg[0].reshape(2, N_MOD * D)
    g_dmc = seg[1].reshape(N_MOD * D)
    g_b_mod = g_dm.at[0].add(g_dmc)
    g_norm_ffn1, g_norm_mix, g_norm_ffn2 = (seg[k].reshape(2, D) for k in (2, 3, 4))
    g_final = seg[5]
    g_dec_f, g_dec_b = seg[6].reshape(1, heads), seg[7].reshape(1, heads)
    g_conv_all = seg[8].reshape(CONV_WIDTH, -1)
    g_pscale_all = seg[9]
    conv_w = mix_w_conv.shape[2]
    g_w_conv = lax.dynamic_slice_in_dim(g_conv_all, me * conv_w, conv_w, axis=1)[None]
    ps_w = pool_scale.shape[1]
    g_pool_scale = lax.dynamic_slice_in_dim(g_pscale_all, me * ps_w, ps_w, axis=0)[None]

    dm_rows = packed_all[:, 0, 0:offs[1]].reshape(N_DEV, 2, N_MOD * D)
    dmc_rows = packed_all[:, 0, offs[1]:offs[2]].reshape(N_DEV, N_MOD * D)
    dm_x_blk = jnp.transpose(lax.dynamic_slice_in_dim(dm_rows, me * mod_w, mod_w, axis=2), (1, 0, 2))
    dm_c_blk = jnp.stack([lax.dynamic_slice_in_dim(dmc_rows, me * mod_w, mod_w, axis=1),
                          jnp.zeros((N_DEV, mod_w), F32)])
    g_w_mod, d_w_mod, nm_w_mod, nv_w_mod, cctx_part = adaln_bwd(cond16, dm_x_blk, dm_c_blk, w_mod, m_w_mod, v_w_mod,
                                                                name="adaln_bwd")
    cpad = jnp.concatenate([cctx_part[0], jnp.zeros((8192 - D,), F32)])[None, :] if D < 8192 else cctx_part[0:1]
    (cctx_all,) = all_gather([cpad], name="gather_cctx")
    g_c_ctx = sum_partials(cctx_all, name="sum_cctx")[0, :D]

    small = [("c_ctx", g_c_ctx, c_ctx, m_c_ctx, v_c_ctx), ("b_mod", g_b_mod, b_mod, m_b_mod, v_b_mod),
             ("norm_ffn1", g_norm_ffn1, norm_ffn1, m_norm_ffn1, v_norm_ffn1),
             ("norm_mix", g_norm_mix, norm_mix, m_norm_mix, v_norm_mix),
             ("norm_ffn2", g_norm_ffn2, norm_ffn2, m_norm_ffn2, v_norm_ffn2),
             ("mix_w_conv", g_w_conv, mix_w_conv, m_mix_w_conv, v_mix_w_conv),
             ("ret_decay_fwd", g_dec_f, ret_decay_fwd, m_ret_decay_fwd, v_ret_decay_fwd),
             ("ret_decay_bwd", g_dec_b, ret_decay_bwd, m_ret_decay_bwd, v_ret_decay_bwd),
             ("pool_scale", g_pool_scale, pool_scale, m_pool_scale, v_pool_scale),
             ("final_norm", g_final, final_norm, m_final_norm, v_final_norm)]
    ssz = [int(a[1].size) for a in small]
    spad = -sum(ssz) % LANE

    def pack(k):
        return jnp.concatenate([a[k].reshape(-1) for a in small] + [jnp.ones((spad,), F32)])[None, :]

    sd, sm, sv = adam_plain(pack(1), pack(2), pack(3), pack(4), name="adam_small")
    soff = [0]
    for s in ssz:
        soff.append(soff[-1] + s)
    for i, (nm, g, w, _, _) in enumerate(small):
        res[nm] = [g.reshape(w.shape)] + [a[0, soff[i]:soff[i + 1]].reshape(w.shape) for a in (sd, sm, sv)]
    res["w_mod"] = [g_w_mod, d_w_mod, nm_w_mod, nv_w_mod]

    after = sd
    for kind, lhs, rhs in (("gate", da1, h1), ("up", db1, h1), ("down", s1, dy1)):
        g = matmul_tn(lhs, rhs, nb=N_DEV, m=lhs.shape[2], n=D, a_blocked=True, b_blocked=False, tk=T,
                      name=f"wgrad_{kind}_01", after=after)
        tie = reduce_start(f"01_{kind}", [g], g)
        after = reducing[f"01_{kind}"][2][2]
    grad_x = dx0[:t_lat][None] + tie

    last = [grad_x]

    def reduce_finish(tag):
        grads, directs, handle = reducing[tag]
        landed = chip_exchange_wait(handle, last[0], name=f"rs_wait_{tag}")
        return list(zip(grads, directs, landed))

    def big(w, m, v, l, part, prev, nm, transposed=False):
        if transposed:
            w, m, v = (jnp.swapaxes(a, 1, 2) for a in (w, m, v))
        outs = adam_reduced(own_block, part[0], part[1], part[2], w, m, v, l, prev, name=nm)
        last[0] = outs[0]
        return outs

    ffn1 = [(ffn1_w_gate, m_ffn1_w_gate, v_ffn1_w_gate), (ffn1_w_up, m_ffn1_w_up, v_ffn1_w_up),
            (ffn1_w_down, m_ffn1_w_down, v_ffn1_w_down)]
    ffn2 = [(ffn2_w_gate, m_ffn2_w_gate, v_ffn2_w_gate), (ffn2_w_up, m_ffn2_w_up, v_ffn2_w_up),
            (ffn2_w_down, m_ffn2_w_down, v_ffn2_w_down)]
    kinds = ["gate", "up", "down"]
    half = {}
    parts = reduce_finish("12")
    for t, (w, m, v) in enumerate(ffn2):
        half[f"ffn2_w_{kinds[t]}"] = big(w, m, v, 1, parts[t], None, f"adam_ffn2_w_{kinds[t]}_1", transposed=t < 2)
    parts = reduce_finish("11")
    for t, (w, m, v) in enumerate(ffn1):
        half[f"ffn1_w_{kinds[t]}"] = big(w, m, v, 1, parts[t], None, f"adam_ffn1_w_{kinds[t]}_1", transposed=t < 2)
    res["pool_w"] = [a.reshape(pool_w.shape) for a in
                     big(pool_w.reshape(1, n_grp * grp_rows, grp_w), m_pool_w.reshape(1, n_grp * grp_rows, grp_w),
                         v_pool_w.reshape(1, n_grp * grp_rows, grp_w), 0, parts[3], None, "adam_pool_w")]
    parts = reduce_finish("02")
    for t, (w, m, v) in enumerate(ffn2):
        nm = f"ffn2_w_{kinds[t]}"
        outs = big(w, m, v, 0, parts[t], half[nm], f"adam_{nm}_0", transposed=t < 2)
        res[nm] = [jnp.swapaxes(a, 1, 2) for a in outs] if t < 2 else outs
    parts = reduce_finish("mix")
    res["mix_w_in"] = big(mix_w_in, m_mix_w_in, v_mix_w_in, 0, parts[0], None, "adam_mix_w_in")
    res["mix_w_out"] = big(mix_w_out, m_mix_w_out, v_mix_w_out, 0, parts[1], None, "adam_mix_w_out")
    for t, (w, m, v) in enumerate(ffn1):
        nm = f"ffn1_w_{kinds[t]}"
        outs = big(w, m, v, 0, reduce_finish(f"01_{kinds[t]}")[0], half[nm], f"adam_{nm}_0", transposed=t < 2)
        res[nm] = [jnp.swapaxes(a, 1, 2) for a in outs] if t < 2 else outs

    order = ["c_ctx", "w_mod", "b_mod", "norm_ffn1", "norm_mix", "norm_ffn2", "ffn1_w_gate", "ffn1_w_up", "ffn1_w_down",
             "ffn2_w_gate", "ffn2_w_up", "ffn2_w_down", "mix_w_in", "mix_w_conv", "mix_w_out", "ret_decay_fwd",
             "ret_decay_bwd", "pool_w", "pool_scale", "final_norm"]
    return (loss, grad_x, *[res[n][0] for n in order], *[res[n][1] for n in order],
            *[res[n][2] for n in order], *[res[n][3] for n in order])
```

```python
import jax
import jax.numpy as jnp
from jax import lax
from jax.experimental import pallas as pl
from jax.experimental.pallas import tpu as pltpu

F32 = jnp.float32
BF16 = jnp.bfloat16
MESH = pl.DeviceIdType.MESH

N_DEV = 8
N_MOD = 9
EPS = 1e-6
GN_EPS = 1e-5
MACARON = 0.5
HEAD_DIM = 128
K_SCALE = HEAD_DIM ** -0.5
ROPE_BASE = 10000.0
GRID_W = 64
CONV_WIDTH = 3
POOL_WINDOWS = (2, 4, 8, 16)
POOL_PAD = 16

ADAM_LR = 0.001
ADAM_B1 = 0.9
ADAM_B2 = 0.999
ADAM_EPS = 1e-08
ADAM_WD = 0.01
ADAM_STEP = 10

LANE = 128
ROW_CHUNK = 128
MM_ROWS = 256
VMEM_LIMIT = 56 * 1024 * 1024
ANY = pl.BlockSpec(memory_space=pl.ANY)


def _params(sem=None):
    kw = dict(vmem_limit_bytes=VMEM_LIMIT)
    if sem is not None:
        kw["dimension_semantics"] = sem
    return pltpu.CompilerParams(**kw)


def _nt(a, b):
    return lax.dot_general(a, b, (((1,), (1,)), ((), ())), preferred_element_type=F32)


def _tn(a, b):
    return lax.dot_general(a, b, (((0,), (0,)), ((), ())), preferred_element_type=F32)


def _nn(a, b):
    return jnp.dot(a, b, preferred_element_type=F32)


def _silu_parts(a):
    sig = jax.nn.sigmoid(a)
    return sig, a * sig


def _dsilu(a, sig):
    return sig * (1.0 + a * (1.0 - sig))


def _norm_mod(x, g, shift, scale):
    r = lax.rsqrt(jnp.mean(x * x, axis=-1, keepdims=True) + EPS)
    return (x * r * g) * (1.0 + scale) + shift


def _norm_mod_bwd(x, dh, g, scale):
    r = lax.rsqrt(jnp.mean(x * x, axis=-1, keepdims=True) + EPS)
    xhat = x * r
    dn = dh * (1.0 + scale)
    dshift = jnp.sum(dh, axis=0, keepdims=True)
    dscale = jnp.sum(dh * (xhat * g), axis=0, keepdims=True)
    dg = jnp.sum(dn * xhat, axis=0, keepdims=True)
    dxh = dn * g
    dx = r * (dxh - xhat * jnp.mean(dxh * xhat, axis=-1, keepdims=True))
    return dx, dshift, dscale, dg


def _group_index(i, tiles_first, n_groups):
    if n_groups == 1:
        return 0
    return jnp.where(i >= tiles_first, 1, 0)


def _mesh_pos():
    x, y, c = lax.axis_index("x"), lax.axis_index("y"), lax.axis_index("c")
    chips = [(1 - x, y), (x, 1 - y), (1 - x, 1 - y)]
    return x, y, c, chips


def _flat(px, py, pc):
    return 4 * px + 2 * py + pc


def all_gather(shards, name):
    n = len(shards)

    def body(*refs):
        ins, outs = refs[:n], refs[n:2 * n]
        send_sems, recv_sems, local_sems = refs[2 * n:]
        x, y, c, chips = _mesh_pos()
        me, sibling = (x, y, c), (x, y, 1 - c)

        def copy(t, k, block, to, src=None):
            dst = outs[t].at[_flat(*block)]
            return pltpu.make_async_remote_copy(
                src_ref=dst if src is None else src, dst_ref=dst,
                send_sem=send_sems.at[t, k], recv_sem=recv_sems.at[t, k],
                device_id=to, device_id_type=MESH)

        mine = [pltpu.make_async_copy(ins[t], outs[t].at[_flat(*me)], local_sems.at[t]) for t in range(n)]
        for cp in mine:
            cp.start()
        first = []
        for t in range(n):
            first.append(copy(t, 0, me, sibling, src=ins[t]))
            first += [copy(t, 1 + j, me, (*chip, c), src=ins[t]) for j, chip in enumerate(chips)]
        for cp in first:
            cp.start()
        passed = []
        for t in range(n):
            for j, chip in enumerate(chips):
                copy(t, 1 + j, (*chip, c), me).wait_recv()
                fwd = copy(t, 4 + j, (*chip, c), sibling)
                fwd.start()
                passed.append(fwd)
        for t in range(n):
            copy(t, 0, sibling, me).wait_recv()
            for j, chip in enumerate(chips):
                copy(t, 4 + j, (*chip, 1 - c), me).wait_recv()
        for cp in first + passed:
            cp.wait_send()
        for cp in mine:
            cp.wait()

    return pl.pallas_call(
        body, name=name,
        out_shape=[jax.ShapeDtypeStruct((N_DEV,) + s.shape, s.dtype) for s in shards],
        in_specs=[ANY] * n, out_specs=[ANY] * n,
        scratch_shapes=[pltpu.SemaphoreType.DMA((n, 7)), pltpu.SemaphoreType.DMA((n, 7)), pltpu.SemaphoreType.DMA((n,))],
    )(*shards)


HBM_SPEC = pl.BlockSpec(memory_space=pltpu.HBM)
SEM_SPEC = pl.BlockSpec(memory_space=pltpu.SEMAPHORE)
DATAFLOW = pltpu.SideEffectType.DATAFLOW_SIDE_EFFECTING


def _in_hbm(a):
    return pltpu.with_memory_space_constraint(a, pltpu.HBM)


def _push_peers():
    x, y, c, chips = _mesh_pos()
    return [(*chip, c) for chip in chips] + [(x, y, 1 - c)]


def cast_place(w, layer, own_block, name):
    _, R, C = w.shape
    tr = _row_tile(R, C)

    def body(idx_ref, w_ref, o_ref):
        o_ref[...] = w_ref[...].astype(BF16)

    return pl.pallas_call(
        body, name=name,
        grid_spec=pltpu.PrefetchScalarGridSpec(
            num_scalar_prefetch=1, grid=(R // tr,),
            in_specs=[pl.BlockSpec((None, tr, C), lambda i, idx: (layer, i, 0))],
            out_specs=pl.BlockSpec((None, tr, C), lambda i, idx: (idx[0], i, 0))),
        out_shape=jax.ShapeDtypeStruct((N_DEV, R, C), BF16),
        compiler_params=_params(("arbitrary",)),
    )(own_block, w)


def gather_start(lands, after, name):
    n = len(lands)

    def body(*refs):
        lz = refs[:n]
        send_sems, recv_sems = refs[n + 1], refs[n + 2]
        token = refs[-1]
        x, y, c, _ = _mesh_pos()
        for t in range(n):
            mine = lz[t].at[_flat(x, y, c)]
            for k, peer in enumerate(_push_peers()):
                pltpu.make_async_remote_copy(
                    src_ref=mine, dst_ref=mine, send_sem=send_sems.at[4 * t + k],
                    recv_sem=recv_sems.at[4 * t + k], device_id=peer, device_id_type=MESH).start()
        token[...] = jnp.zeros_like(token)

    return pl.pallas_call(
        body, name=name,
        out_shape=(pltpu.SemaphoreType.DMA((4 * n,)), pltpu.SemaphoreType.DMA((4 * n,)),
                   *[pltpu.HBM(l.shape, l.dtype) for l in lands], jax.ShapeDtypeStruct((8, LANE), F32)),
        in_specs=[HBM_SPEC] * n + [ANY],
        out_specs=(SEM_SPEC, SEM_SPEC, *[HBM_SPEC] * n, pl.BlockSpec(memory_space=pltpu.VMEM)),
        input_output_aliases={i: 2 + i for i in range(n)},
        compiler_params=pltpu.CompilerParams(has_side_effects=DATAFLOW),
    )(*[_in_hbm(l) for l in lands], after)


def gather_wait(handle, after, name):
    send, recv = handle[0], handle[1]
    bufs = handle[2:-1]
    n = len(bufs)

    def body(*refs):
        lz = refs[:n]
        send_sems, recv_sems = refs[n], refs[n + 1]
        x, y, c, _ = _mesh_pos()
        for t in range(n):
            for k, peer in enumerate(_push_peers()):
                cp = pltpu.make_async_remote_copy(
                    src_ref=lz[t].at[_flat(x, y, c)], dst_ref=lz[t].at[_flat(*peer)], send_sem=send_sems.at[4 * t + k],
                    recv_sem=recv_sems.at[4 * t + k], device_id=peer, device_id_type=MESH)
                cp.wait_send()
                cp.wait_recv()

    outs = pl.pallas_call(
        body, name=name,
        out_shape=tuple(pltpu.HBM(b.shape, b.dtype) for b in bufs),
        in_specs=[HBM_SPEC] * n + [SEM_SPEC, SEM_SPEC, ANY],
        out_specs=tuple([HBM_SPEC] * n),
        input_output_aliases={i: i for i in range(n)},
        compiler_params=pltpu.CompilerParams(has_side_effects=DATAFLOW),
    )(*bufs, send, recv, after)
    return list(outs)


def gather_finish(lands, name):
    n = len(lands)

    def body(*refs):
        outs = refs[n:2 * n]
        send_sems, recv_sems = refs[2 * n:]
        x, y, c, chips = _mesh_pos()
        sibling = (x, y, 1 - c)

        def copy(t, j, core):
            blk = outs[t].at[_flat(*chips[j], core)]
            return pltpu.make_async_remote_copy(
                src_ref=blk, dst_ref=blk, send_sem=send_sems.at[t, j], recv_sem=recv_sems.at[t, j],
                device_id=sibling, device_id_type=MESH)

        sends = [copy(t, j, c) for t in range(n) for j in range(3)]
        for cp in sends:
            cp.start()
        for t in range(n):
            for j in range(3):
                copy(t, j, 1 - c).wait_recv()
        for cp in sends:
            cp.wait_send()

    return pl.pallas_call(
        body, name=name,
        out_shape=[jax.ShapeDtypeStruct(l.shape, l.dtype) for l in lands],
        in_specs=[ANY] * n, out_specs=[ANY] * n,
        input_output_aliases={t: t for t in range(n)},
        scratch_shapes=[pltpu.SemaphoreType.DMA((n, 3)), pltpu.SemaphoreType.DMA((n, 3))],
    )(*lands)


def chip_exchange_start(sums, name):
    n = len(sums)
    lands = [lax.empty(s.shape, s.dtype) for s in sums]

    def body(*refs):
        ins, lz = refs[:n], refs[n:2 * n]
        send_sems, recv_sems = refs[2 * n], refs[2 * n + 1]
        token = refs[-1]
        peers = _push_peers()
        for t in range(n):
            for j in range(3):
                pltpu.make_async_remote_copy(
                    src_ref=ins[t].at[j], dst_ref=lz[t].at[j], send_sem=send_sems.at[3 * t + j],
                    recv_sem=recv_sems.at[3 * t + j], device_id=peers[j], device_id_type=MESH).start()
        token[...] = jnp.zeros_like(token)

    return pl.pallas_call(
        body, name=name,
        out_shape=(pltpu.SemaphoreType.DMA((3 * n,)), pltpu.SemaphoreType.DMA((3 * n,)),
                   *[pltpu.HBM(s.shape, s.dtype) for s in sums], *[pltpu.HBM(s.shape, s.dtype) for s in sums],
                   jax.ShapeDtypeStruct((8, LANE), F32)),
        in_specs=[HBM_SPEC] * (2 * n),
        out_specs=(SEM_SPEC, SEM_SPEC, *[HBM_SPEC] * (2 * n), pl.BlockSpec(memory_space=pltpu.VMEM)),
        input_output_aliases={i: 2 + i for i in range(2 * n)},
        compiler_params=pltpu.CompilerParams(has_side_effects=DATAFLOW),
    )(*[_in_hbm(s) for s in sums], *[_in_hbm(l) for l in lands])


def chip_exchange_wait(handle, after, name):
    send, recv = handle[0], handle[1]
    n = (len(handle) - 3) // 2
    bufs = handle[2:2 + 2 * n]

    def body(*refs):
        ins, lz = refs[:n], refs[n:2 * n]
        send_sems, recv_sems = refs[2 * n], refs[2 * n + 1]
        peers = _push_peers()
        for t in range(n):
            for j in range(3):
                cp = pltpu.make_async_remote_copy(
                    src_ref=ins[t].at[j], dst_ref=lz[t].at[j], send_sem=send_sems.at[3 * t + j],
                    recv_sem=recv_sems.at[3 * t + j], device_id=peers[j], device_id_type=MESH)
                cp.wait_send()
                cp.wait_recv()

    outs = pl.pallas_call(
        body, name=name,
        out_shape=tuple(pltpu.HBM(b.shape, b.dtype) for b in bufs),
        in_specs=[HBM_SPEC] * (2 * n) + [SEM_SPEC, SEM_SPEC, ANY],
        out_specs=tuple([HBM_SPEC] * (2 * n)),
        input_output_aliases={i: i for i in range(2 * n)},
        compiler_params=pltpu.CompilerParams(has_side_effects=DATAFLOW),
    )(*bufs, send, recv, after)
    return list(outs[n:])


def _sibling_copy(grads, stages, directs, send_sems, recv_sems, t, k):
    x, y, c, chips = _mesh_pos()
    if k < 3:
        src, dst = grads[t].at[_flat(*chips[k], 1 - c)], stages[t].at[k]
    else:
        src, dst = grads[t].at[_flat(x, y, 1 - c)], directs[t]
    return pltpu.make_async_remote_copy(
        src_ref=src, dst_ref=dst, send_sem=send_sems.at[4 * t + k], recv_sem=recv_sems.at[4 * t + k],
        device_id=(x, y, 1 - c), device_id_type=MESH)


def sibling_exchange_start(grads, after, name):
    n = len(grads)
    stages = [lax.empty((3,) + g.shape[1:], g.dtype) for g in grads]
    directs = [lax.empty(g.shape[1:], g.dtype) for g in grads]

    def body(*refs):
        send_sems, recv_sems = refs[3 * n + 1], refs[3 * n + 2]
        for t in range(n):
            for k in range(4):
                _sibling_copy(refs[:n], refs[n:2 * n], refs[2 * n:3 * n], send_sems, recv_sems, t, k).start()
        refs[-1][...] = jnp.zeros_like(refs[-1])

    bufs = list(grads) + stages + directs
    return pl.pallas_call(
        body, name=name,
        out_shape=(pltpu.SemaphoreType.DMA((4 * n,)), pltpu.SemaphoreType.DMA((4 * n,)),
                   *[pltpu.HBM(b.shape, b.dtype) for b in bufs], jax.ShapeDtypeStruct((8, LANE), F32)),
        in_specs=[HBM_SPEC] * (3 * n) + [ANY],
        out_specs=(SEM_SPEC, SEM_SPEC, *[HBM_SPEC] * (3 * n), pl.BlockSpec(memory_space=pltpu.VMEM)),
        input_output_aliases={i: 2 + i for i in range(3 * n)},
        compiler_params=pltpu.CompilerParams(has_side_effects=DATAFLOW),
    )(*[_in_hbm(b) for b in bufs], after)


def sibling_exchange_wait(handle, after, name):
    send, recv = handle[0], handle[1]
    bufs = handle[2:-1]
    n = len(bufs) // 3

    def body(*refs):
        send_sems, recv_sems = refs[3 * n], refs[3 * n + 1]
        for t in range(n):
            for k in range(4):
                cp = _sibling_copy(refs[:n], refs[n:2 * n], refs[2 * n:3 * n], send_sems, recv_sems, t, k)
                cp.wait_send()
                cp.wait_recv()

    outs = pl.pallas_call(
        body, name=name,
        out_shape=tuple(pltpu.HBM(b.shape, b.dtype) for b in bufs),
        in_specs=[HBM_SPEC] * (3 * n) + [SEM_SPEC, SEM_SPEC, ANY],
        out_specs=tuple([HBM_SPEC] * (3 * n)),
        input_output_aliases={i: i for i in range(3 * n)},
        compiler_params=pltpu.CompilerParams(has_side_effects=DATAFLOW),
    )(*bufs, send, recv, after)
    return list(outs[:n]), list(outs[n:2 * n]), list(outs[2 * n:])


def _row_tile(rows, cols, limit_bytes=3 << 19):
    best = None
    for t in range(16, rows + 1, 16):
        if rows % t == 0 and t * cols * 4 <= limit_bytes:
            best = t
    return best if best is not None else rows


def chip_sum(grad, stage, peer_blocks, name):
    _, R, C = grad.shape
    tr = _row_tile(R, C)

    def body(idx_ref, g_ref, s_ref, o_ref):
        o_ref[...] = (g_ref[...].astype(F32) + s_ref[...].astype(F32)).astype(BF16)

    return pl.pallas_call(
        body, name=name,
        grid_spec=pltpu.PrefetchScalarGridSpec(
            num_scalar_prefetch=1, grid=(3, R // tr),
            in_specs=[pl.BlockSpec((None, tr, C), lambda j, i, idx: (idx[j], i, 0)),
                      pl.BlockSpec((None, tr, C), lambda j, i, idx: (j, i, 0))],
            out_specs=pl.BlockSpec((None, tr, C), lambda j, i, idx: (j, i, 0))),
        out_shape=jax.ShapeDtypeStruct((3, R, C), BF16),
        compiler_params=_params(("arbitrary", "arbitrary")),
    )(peer_blocks, grad, stage)


def _adamw(w, g, m, v):
    m2 = ADAM_B1 * m + (1.0 - ADAM_B1) * g
    v2 = ADAM_B2 * v + (1.0 - ADAM_B2) * (g * g)
    m_hat = m2 / (1.0 - ADAM_B1 ** ADAM_STEP)
    v_hat = v2 / (1.0 - ADAM_B2 ** ADAM_STEP)
    delta = -ADAM_LR * (m_hat / (jnp.sqrt(v_hat) + ADAM_EPS) + ADAM_WD * w)
    return delta, m2, v2


def adam_reduced(own_block, grad, direct, landed, w, m, v, layer, prev, name):
    L, R, C = w.shape
    tr = _row_tile(R, C, 1 << 20)
    first = prev is None

    def body(idx_ref, g_ref, d_ref, l_ref, w_ref, m_ref, v_ref, *rest):
        og, od, om, ov = rest[-4:]
        g = g_ref[...].astype(F32) + d_ref[...].astype(F32)
        for j in range(3):
            g = g + l_ref[j].astype(F32)
        delta, m2, v2 = _adamw(w_ref[...], g, m_ref[...], v_ref[...])
        og[...] = g
        od[...] = delta
        om[...] = m2
        ov[...] = v2

    lay = pl.BlockSpec((None, tr, C), lambda i, idx: (layer, i, 0))
    in_specs = [pl.BlockSpec((None, tr, C), lambda i, idx: (idx[0], i, 0)),
                pl.BlockSpec((tr, C), lambda i, idx: (i, 0)),
                pl.BlockSpec((3, tr, C), lambda i, idx: (0, i, 0)),
                lay, lay, lay]
    args = [own_block, grad, direct, landed, w, m, v]
    aliases = {}
    if not first:
        in_specs += [ANY] * 4
        args += list(prev)
        aliases = {7 + k: k for k in range(4)}
    return pl.pallas_call(
        body, name=name,
        grid_spec=pltpu.PrefetchScalarGridSpec(
            num_scalar_prefetch=1, grid=(R // tr,), in_specs=in_specs, out_specs=[lay] * 4),
        out_shape=[jax.ShapeDtypeStruct((L, R, C), F32)] * 4,
        input_output_aliases=aliases,
        compiler_params=_params(("arbitrary",)),
    )(*args)


def sum_partials(parts, name):
    _, _, N = parts.shape
    tn = 8192

    def body(p_ref, o_ref):
        g = p_ref[0]
        for k in range(1, N_DEV):
            g = g + p_ref[k]
        o_ref[...] = g

    return pl.pallas_call(
        body, name=name, grid=(N // tn,),
        in_specs=[pl.BlockSpec((N_DEV, 1, tn), lambda i: (0, 0, i))],
        out_specs=pl.BlockSpec((1, tn), lambda i: (0, i)),
        out_shape=jax.ShapeDtypeStruct((1, N), F32),
    )(parts)


def adam_plain(g, w, m, v, name):
    _, N = g.shape

    def body(g_ref, w_ref, m_ref, v_ref, od, om, ov):
        delta, m2, v2 = _adamw(w_ref[...], g_ref[...], m_ref[...], v_ref[...])
        od[...] = delta
        om[...] = m2
        ov[...] = v2

    return pl.pallas_call(
        body, name=name, out_shape=[jax.ShapeDtypeStruct((1, N), F32)] * 3,
    )(g, w, m, v)


def adaln_fwd(cond16, w_mod, b_blk, name):
    L, D, W = w_mod.shape
    tn = 768 if W % 768 == 0 else W

    def body(c_ref, w_ref, b_ref, o_ref):
        c = c_ref[...]
        sc = (c * jax.nn.sigmoid(c)).astype(BF16)
        o_ref[...] = _nn(sc, w_ref[...].astype(BF16)) + b_ref[...]

    return pl.pallas_call(
        body, name=name, grid=(L, W // tn),
        in_specs=[pl.BlockSpec((16, D), lambda l, i: (0, 0)),
                  pl.BlockSpec((None, D, tn), lambda l, i: (l, 0, i)),
                  pl.BlockSpec((None, 1, tn), lambda l, i: (l, 0, i))],
        out_specs=pl.BlockSpec((None, 16, tn), lambda l, i: (l, 0, i)),
        out_shape=jax.ShapeDtypeStruct((L, 16, W), F32),
        compiler_params=_params(("arbitrary", "arbitrary")),
    )(cond16, w_mod, b_blk)


def adaln_bwd(cond16, dm_x, dm_c, w_mod, m_mod, v_mod, name):
    L, D, W = w_mod.shape
    tn = 256 if W % 256 == 0 else W
    nt = W // tn

    def body(c_ref, cT_ref, dx_ref, dc_ref, w_ref, m_ref, v_ref, og, od, om, ov, pc_ref):
        l, i = pl.program_id(0), pl.program_id(1)
        c = c_ref[...]
        sig, sl = _silu_parts(c)
        cT = cT_ref[...]
        sigT = jax.nn.sigmoid(cT)
        scT = (cT * sigT).astype(BF16)
        dmc = jnp.sum(dc_ref[...], axis=0, keepdims=True)
        dm16 = jnp.concatenate([dx_ref[...], jnp.broadcast_to(dmc, (8, tn))], axis=0)
        row = lax.broadcasted_iota(jnp.int32, (16, tn), 0)
        dm16 = jnp.where(row <= 8, dm16, 0.0).astype(BF16)
        w = w_ref[...]
        g = _nn(scT, dm16)
        delta, m2, v2 = _adamw(w, g, m_ref[...], v_ref[...])
        og[...] = g
        od[...] = delta
        om[...] = m2
        ov[...] = v2

        @pl.when((l == 0) & (i == 0))
        def _():
            pc_ref[...] = jnp.zeros_like(pc_ref)

        @pl.when(l == 0)
        def _():
            back = _nt(jnp.broadcast_to(dmc, (8, tn)).astype(BF16), w.astype(BF16))
            pc_ref[...] += back * _dsilu(c[8:9, :], sig[8:9, :])

    col = pl.BlockSpec((None, D, tn), lambda l, i: (l, 0, i))
    row8 = pl.BlockSpec((None, 8, tn), lambda l, i: (l, 0, i))
    return pl.pallas_call(
        body, name=name, grid=(L, nt),
        in_specs=[pl.BlockSpec((16, D), lambda l, i: (0, 0)), pl.BlockSpec((D, 16), lambda l, i: (0, 0)),
                  row8, row8, col, col, col],
        out_specs=[col, col, col, col, pl.BlockSpec((8, D), lambda l, i: (0, 0))],
        out_shape=[jax.ShapeDtypeStruct((L, D, W), F32)] * 4 + [jax.ShapeDtypeStruct((8, D), F32)],
        compiler_params=_params(("arbitrary", "arbitrary")),
    )(cond16, cond16.T, dm_x, dm_c, w_mod, m_mod, v_mod)


def _token_spec(tm, D):
    if tm > 256:
        return pl.BlockSpec((tm, D), lambda i, j: (i, 0), pipeline_mode=pl.Buffered(1))
    return pl.BlockSpec((tm, D), lambda i, j: (i, 0))


def ffn_fwd(x, mod, gn, wg, wu, wd, *, tm, t_first, name):
    T, D = x.shape
    nb, cw, _ = wg.shape
    G = mod.shape[0]
    nt = T // tm
    first = t_first // tm

    def body(x_ref, mod_ref, g_ref, wg_ref, wu_ref, wd_ref, xo_ref, h_ref, a_ref, b_ref, y_ref, hs, acc, a_s, b_s, s_s):
        j = pl.program_id(1)

        @pl.when(j == 0)
        def _():
            def head(r, carry):
                rows = pl.ds(pl.multiple_of(r * ROW_CHUNK, ROW_CHUNK), ROW_CHUNK)
                hb = _norm_mod(x_ref[rows, :], g_ref[...], mod_ref[0:1, :], mod_ref[1:2, :]).astype(BF16)
                hs[rows, :] = hb
                h_ref[rows, :] = hb
                return carry

            lax.fori_loop(0, tm // ROW_CHUNK, head, 0)
            acc[...] = jnp.zeros_like(acc)

        parts = [slice(r, r + MM_ROWS) for r in range(0, tm, MM_ROWS)]
        for rows in parts:
            a_s[rows, :] = _nt(hs[rows, :], wg_ref[...])
            b_s[rows, :] = _nt(hs[rows, :], wu_ref[...])
        for r in range(0, tm, ROW_CHUNK):
            rows = slice(r, r + ROW_CHUNK)
            av, bv = a_s[rows, :], b_s[rows, :]
            a_ref[rows, :] = av.astype(BF16)
            b_ref[rows, :] = bv.astype(BF16)
            _, sl = _silu_parts(av)
            s_s[rows, :] = (sl * bv).astype(BF16)
        for rows in parts:
            acc[rows, :] += _nn(s_s[rows, :], wd_ref[...])

        @pl.when(j == nb - 1)
        def _():
            def tail(r, carry):
                rows = pl.ds(pl.multiple_of(r * ROW_CHUNK, ROW_CHUNK), ROW_CHUNK)
                y = acc[rows, :]
                y_ref[rows, :] = y.astype(BF16)
                xo_ref[rows, :] = x_ref[rows, :] + (MACARON * mod_ref[2:3, :]) * y
                return carry

            lax.fori_loop(0, tm // ROW_CHUNK, tail, 0)

    tok = _token_spec(tm, D)
    act = pl.BlockSpec((None, tm, cw), lambda i, j: (j, i, 0))
    wblk = pl.BlockSpec((None, cw, D), lambda i, j: (j, 0, 0))
    return pl.pallas_call(
        body, name=name, grid=(nt, nb),
        in_specs=[tok, pl.BlockSpec((None, 3, D), lambda i, j: (_group_index(i, first, G), 0, 0)),
                  pl.BlockSpec((1, D), lambda i, j: (0, 0)), wblk, wblk, wblk],
        out_specs=[tok, tok, act, act, tok],
        out_shape=[jax.ShapeDtypeStruct((T, D), F32), jax.ShapeDtypeStruct((T, D), BF16),
                   jax.ShapeDtypeStruct((nb, T, cw), BF16), jax.ShapeDtypeStruct((nb, T, cw), BF16),
                   jax.ShapeDtypeStruct((T, D), BF16)],
        scratch_shapes=[pltpu.VMEM((tm, D), BF16), pltpu.VMEM((tm, D), F32), pltpu.VMEM((tm, cw), F32),
                        pltpu.VMEM((tm, cw), F32), pltpu.VMEM((tm, cw), BF16)],
        compiler_params=_params(("arbitrary", "arbitrary")),
    )(x, mod, gn, wg, wu, wd)


def ffn_bwd_tok(x, dxo, y, a, b, mod, gn, wg, wu, wd, *, tm, t_first, name):
    T, D = x.shape
    nb, cw, _ = wg.shape
    G = mod.shape[0]
    nt = T // tm
    first = t_first // tm

    def body(x_ref, dxo_ref, y_ref, a_ref, b_ref, mod_ref, g_ref, wg_ref, wu_ref, wd_ref,
             dx_ref, da_ref, db_ref, s_ref, dy_ref, red_ref, dys, dh, ds_s):
        i, j = pl.program_id(0), pl.program_id(1)

        @pl.when(j == 0)
        def _():
            def head(r, carry):
                rows = pl.ds(pl.multiple_of(r * ROW_CHUNK, ROW_CHUNK), ROW_CHUNK)
                dyb = ((MACARON * mod_ref[2:3, :]) * dxo_ref[rows, :]).astype(BF16)
                dys[rows, :] = dyb
                dy_ref[rows, :] = dyb
                return carry

            lax.fori_loop(0, tm // ROW_CHUNK, head, 0)
            dh[...] = jnp.zeros_like(dh)

        parts = [slice(r, r + MM_ROWS) for r in range(0, tm, MM_ROWS)]
        for rows in parts:
            ds_s[rows, :] = _nt(dys[rows, :], wd_ref[...])
        for r in range(0, tm, ROW_CHUNK):
            rows = slice(r, r + ROW_CHUNK)
            av = a_ref[rows, :].astype(F32)
            bv = b_ref[rows, :].astype(F32)
            ds = ds_s[rows, :]
            sig, sl = _silu_parts(av)
            s_ref[rows, :] = (sl * bv).astype(BF16)
            da_ref[rows, :] = (ds * bv * _dsilu(av, sig)).astype(BF16)
            db_ref[rows, :] = (ds * sl).astype(BF16)
        for rows in parts:
            dh[rows, :] += _nn(da_ref[rows, :], wg_ref[...]) + _nn(db_ref[rows, :], wu_ref[...])

        @pl.when((j == 0) & ((i == 0) | (i == first)))
        def _():
            red_ref[...] = jnp.zeros_like(red_ref)

        @pl.when(j == nb - 1)
        def _():
            def tail(r, carry):
                rows = pl.ds(pl.multiple_of(r * ROW_CHUNK, ROW_CHUNK), ROW_CHUNK)
                dxo_v = dxo_ref[rows, :]
                dxn, dshift, dscale, dg = _norm_mod_bwd(x_ref[rows, :], dh[rows, :], g_ref[...], mod_ref[1:2, :])
                dx_ref[rows, :] = dxo_v + dxn
                red_ref[0:1, :] += dshift
                red_ref[1:2, :] += dscale
                red_ref[2:3, :] += jnp.sum((MACARON * dxo_v) * y_ref[rows, :].astype(F32), axis=0, keepdims=True)
                red_ref[3:4, :] += dg
                return carry

            lax.fori_loop(0, tm // ROW_CHUNK, tail, 0)

    tok = _token_spec(tm, D)
    act = pl.BlockSpec((None, tm, cw), lambda i, j: (j, i, 0))
    wblk = pl.BlockSpec((None, cw, D), lambda i, j: (j, 0, 0))
    return pl.pallas_call(
        body, name=name, grid=(nt, nb),
        in_specs=[tok, tok, tok, act, act,
                  pl.BlockSpec((None, 3, D), lambda i, j: (_group_index(i, first, G), 0, 0)),
                  pl.BlockSpec((1, D), lambda i, j: (0, 0)), wblk, wblk, wblk],
        out_specs=[tok, act, act, act, tok,
                   pl.BlockSpec((None, 8, D), lambda i, j: (_group_index(i, first, G), 0, 0))],
        out_shape=[jax.ShapeDtypeStruct((T, D), F32)] + [jax.ShapeDtypeStruct((nb, T, cw), BF16)] * 3
        + [jax.ShapeDtypeStruct((T, D), BF16), jax.ShapeDtypeStruct((G, 8, D), F32)],
        scratch_shapes=[pltpu.VMEM((tm, D), BF16), pltpu.VMEM((tm, D), F32), pltpu.VMEM((tm, cw), F32)],
        compiler_params=_params(("arbitrary", "arbitrary")),
    )(x, dxo, y, a, b, mod, gn, wg, wu, wd)


def matmul_tn(a, b, *, nb, m, n, a_blocked, b_blocked, tk, name, after=None):
    T = a.shape[-2]
    extra = [] if after is None else [after]

    def spec(arr, blocked, width):
        if blocked:
            return pl.BlockSpec((None, tk, width), lambda j, k: (j, k, 0))
        if arr.shape[-1] == width:
            return pl.BlockSpec((tk, width), lambda j, k: (k, 0))
        return pl.BlockSpec((tk, width), lambda j, k: (k, j))

    nk = T // tk

    def body(a_ref, b_ref, *rest):
        if nk == 1:
            rest[-1][...] = _tn(a_ref[...], b_ref[...]).astype(BF16)
            return
        o_ref, acc = rest[-2:]
        k = pl.program_id(1)

        @pl.when(k == 0)
        def _():
            acc[...] = jnp.zeros_like(acc)

        acc[...] += _tn(a_ref[...], b_ref[...])

        @pl.when(k == nk - 1)
        def _():
            o_ref[...] = acc[...].astype(BF16)

    return pl.pallas_call(
        body, name=name, grid=(nb, nk),
        in_specs=[spec(a, a_blocked, m), spec(b, b_blocked, n)] + [ANY] * len(extra),
        out_specs=pl.BlockSpec((None, m, n), lambda j, k: (j, 0, 0)),
        out_shape=jax.ShapeDtypeStruct((nb, m, n), BF16),
        scratch_shapes=[] if nk == 1 else [pltpu.VMEM((m, n), F32)],
        compiler_params=_params(("arbitrary", "arbitrary")),
    )(a, b, *extra)


def proj_in(x, mod, gn, w_in, *, tm, t_first, name):
    T, D = x.shape
    nb, _, cw = w_in.shape
    G = mod.shape[0]
    nt = T // tm
    first = t_first // tm

    def body(x_ref, mod_ref, g_ref, w_ref, p_ref, h_ref, hs):
        @pl.when(pl.program_id(1) == 0)
        def _():
            hb = _norm_mod(x_ref[...], g_ref[...], mod_ref[0:1, :], mod_ref[1:2, :]).astype(BF16)
            hs[...] = hb
            h_ref[...] = hb

        p_ref[...] = _nn(hs[...], w_ref[...])

    tok = pl.BlockSpec((tm, D), lambda i, j: (i, 0))
    return pl.pallas_call(
        body, name=name, grid=(nt, nb),
        in_specs=[tok, pl.BlockSpec((None, 3, D), lambda i, j: (_group_index(i, first, G), 0, 0)),
                  pl.BlockSpec((1, D), lambda i, j: (0, 0)),
                  pl.BlockSpec((None, D, cw), lambda i, j: (j, 0, 0))],
        out_specs=[pl.BlockSpec((tm, cw), lambda i, j: (i, j)), tok],
        out_shape=[jax.ShapeDtypeStruct((T, nb * cw), F32), jax.ShapeDtypeStruct((T, D), BF16)],
        scratch_shapes=[pltpu.VMEM((tm, D), BF16)],
        compiler_params=_params(("arbitrary", "arbitrary")),
    )(x, mod, gn, w_in)


def proj_bwd_tok(x, dxo, dp, mod, gn, w_in, *, tm, name):
    T, D = x.shape
    nb, _, cw = w_in.shape
    G = mod.shape[0]
    nt = T // tm
    first = dxo.shape[0] // tm

    def body(x_ref, dxo_ref, dp_ref, mod_ref, g_ref, w_ref, dx_ref, red_ref, dh):
        i, j = pl.program_id(0), pl.program_id(1)

        @pl.when(j == 0)
        def _():
            dh[...] = jnp.zeros_like(dh)

        dh[...] += _nt(dp_ref[...], w_ref[...])

        @pl.when((j == 0) & ((i == 0) | (i == first)))
        def _():
            red_ref[...] = jnp.zeros_like(red_ref)

        @pl.when(j == nb - 1)
        def _():
            dxn, dshift, dscale, dg = _norm_mod_bwd(x_ref[...], dh[...], g_ref[...], mod_ref[1:2, :])
            dx_ref[...] = jnp.where(i < first, dxo_ref[...], 0.0) + dxn
            red_ref[0:1, :] += dshift
            red_ref[1:2, :] += dscale
            red_ref[3:4, :] += dg

    tok = pl.BlockSpec((tm, D), lambda i, j: (i, 0))
    return pl.pallas_call(
        body, name=name, grid=(nt, nb),
        in_specs=[tok, pl.BlockSpec((tm, D), lambda i, j: (jnp.minimum(i, first - 1), 0)),
                  pl.BlockSpec((tm, cw), lambda i, j: (i, j)),
                  pl.BlockSpec((None, 3, D), lambda i, j: (_group_index(i, first, G), 0, 0)),
                  pl.BlockSpec((1, D), lambda i, j: (0, 0)),
                  pl.BlockSpec((None, D, cw), lambda i, j: (j, 0, 0))],
        out_specs=[tok, pl.BlockSpec((None, 8, D), lambda i, j: (_group_index(i, first, G), 0, 0))],
        out_shape=[jax.ShapeDtypeStruct((T, D), F32), jax.ShapeDtypeStruct((G, 8, D), F32)],
        scratch_shapes=[pltpu.VMEM((tm, D), F32)],
        compiler_params=_params(("arbitrary", "arbitrary")),
    )(x, dxo, dp, mod, gn, w_in)


def _rope(t, cos, sin):
    return t * cos + pltpu.roll(t, HEAD_DIM // 2, axis=1) * sin


def ret_prep(p, cos, sin, *, heads, tm, name):
    T = p.shape[0]

    def body(q_ref, k_ref, v_ref, c_ref, s_ref, qo, ko, vo):
        cos_v, sin_v = c_ref[...], s_ref[...]
        qo[...] = _rope(q_ref[...], cos_v, sin_v).astype(BF16)
        ko[...] = _rope(k_ref[...] * K_SCALE, cos_v, sin_v).astype(BF16)
        vo[...] = v_ref[...].astype(BF16)

    def col(part):
        return pl.BlockSpec((tm, HEAD_DIM), lambda h, i: (i, part * heads + h))

    tab = pl.BlockSpec((tm, HEAD_DIM), lambda h, i: (i, 0))
    out = pl.BlockSpec((None, tm, HEAD_DIM), lambda h, i: (h, i, 0))
    return pl.pallas_call(
        body, name=name, grid=(heads, T // tm),
        in_specs=[col(0), col(1), col(2), tab, tab], out_specs=[out, out, out],
        out_shape=[jax.ShapeDtypeStruct((heads, T, HEAD_DIM), BF16)] * 3,
        compiler_params=_params(("arbitrary", "arbitrary")),
    )(p, p, p, cos, sin)


def _decay(n, m, lgf, lgb, t_lat, t_ctx):
    df = jnp.where(m < t_lat, n - m, n - m + (t_lat + t_ctx))
    db = m - n
    ef = jnp.where(df >= 0, jnp.exp(lgf * df), 0.0)
    eb = jnp.where(db >= 0, jnp.exp(lgb * db), 0.0)
    return ef, eb, df, db


def ret_fwd(q, k, v, lg, *, t_lat, tq, tk, name):
    H, T, _ = k.shape
    t_ctx = T - t_lat

    def body(lg_ref, q_ref, k_ref, v_ref, o_ref):
        h, qi, kj = pl.program_id(0), pl.program_id(1), pl.program_id(2)

        @pl.when(kj == 0)
        def _():
            o_ref[...] = jnp.zeros_like(o_ref)

        s = _nt(q_ref[...], k_ref[...])
        n = (qi * tq + lax.broadcasted_iota(jnp.int32, (tq, tk), 0)).astype(F32)
        m = (kj * tk + lax.broadcasted_iota(jnp.int32, (tq, tk), 1)).astype(F32)
        ef, eb, _, _ = _decay(n, m, lg_ref[0, h], lg_ref[1, h], t_lat, t_ctx)
        o_ref[...] += _nn((s * (ef + eb)).astype(BF16), v_ref[...])

    return pl.pallas_call(
        body, name=name, grid=(H, t_lat // tq, T // tk),
        in_specs=[pl.BlockSpec(memory_space=pltpu.SMEM),
                  pl.BlockSpec((None, tq, HEAD_DIM), lambda h, i, j: (h, i, 0)),
                  pl.BlockSpec((None, tk, HEAD_DIM), lambda h, i, j: (h, j, 0)),
                  pl.BlockSpec((None, tk, HEAD_DIM), lambda h, i, j: (h, j, 0))],
        out_specs=pl.BlockSpec((None, tq, HEAD_DIM), lambda h, i, j: (h, i, 0)),
        out_shape=jax.ShapeDtypeStruct((H, t_lat, HEAD_DIM), F32),
        compiler_params=_params(("arbitrary", "arbitrary", "arbitrary")),
    )(lg, q, k, v)


def ret_bwd(q, k, v, do, lg, *, t_lat, tq, tk, name):
    H, T, _ = k.shape
    t_ctx = T - t_lat

    def body(lg_ref, q_ref, k_ref, v_ref, do_ref, dq_ref, dk_ref, dv_ref, dlg_ref):
        h, kj, qi = pl.program_id(0), pl.program_id(1), pl.program_id(2)

        @pl.when((kj == 0) & (qi == 0))
        def _():
            dq_ref[...] = jnp.zeros_like(dq_ref)
            dlg_ref[...] = jnp.zeros_like(dlg_ref)

        @pl.when(qi == 0)
        def _():
            dk_ref[...] = jnp.zeros_like(dk_ref)
            dv_ref[...] = jnp.zeros_like(dv_ref)

        qv, kv, vv = q_ref[...], k_ref[...], v_ref[...]
        dob = do_ref[...].astype(BF16)
        st = _nt(kv, qv)
        dwt = _nt(vv, dob)
        m = (kj * tk + lax.broadcasted_iota(jnp.int32, (tk, tq), 0)).astype(F32)
        n = (qi * tq + lax.broadcasted_iota(jnp.int32, (tk, tq), 1)).astype(F32)
        ef, eb, df, db = _decay(n, m, lg_ref[0, h], lg_ref[1, h], t_lat, t_ctx)
        dec = ef + eb
        dv_ref[...] += _nn((st * dec).astype(BF16), dob)
        dst = (dwt * dec).astype(BF16)
        dk_ref[...] += _nn(dst, qv)
        rows = pl.ds(pl.multiple_of(qi * tq, tq), tq)
        dq_ref[rows, :] += _tn(dst, kv)
        gs = dwt * st
        dlg_ref[0:1, :] += jnp.sum(gs * (ef * df))
        dlg_ref[1:2, :] += jnp.sum(gs * (eb * db))

    kspec = pl.BlockSpec((None, tk, HEAD_DIM), lambda h, j, i: (h, j, 0))
    qspec = pl.BlockSpec((None, tq, HEAD_DIM), lambda h, j, i: (h, i, 0))
    return pl.pallas_call(
        body, name=name, grid=(H, T // tk, t_lat // tq),
        in_specs=[pl.BlockSpec(memory_space=pltpu.SMEM), qspec, kspec, kspec, qspec],
        out_specs=[pl.BlockSpec((None, t_lat, HEAD_DIM), lambda h, j, i: (h, 0, 0)), kspec, kspec,
                   pl.BlockSpec((None, 8, LANE), lambda h, j, i: (h, 0, 0))],
        out_shape=[jax.ShapeDtypeStruct((H, t_lat, HEAD_DIM), F32), jax.ShapeDtypeStruct((H, T, HEAD_DIM), F32),
                   jax.ShapeDtypeStruct((H, T, HEAD_DIM), F32), jax.ShapeDtypeStruct((H, 8, LANE), F32)],
        compiler_params=_params(("arbitrary", "arbitrary", "arbitrary")),
    )(lg, q, k, v, do)


def _group_norm(o):
    mu = jnp.mean(o, axis=-1, keepdims=True)
    ctr = o - mu
    r = lax.rsqrt(jnp.mean(ctr * ctr, axis=-1, keepdims=True) + GN_EPS)
    return ctr * r, r


def ret_post_fwd(o, p, *, heads, d_model, name):
    H, t_lat, _ = o.shape
    T = p.shape[0]

    def body(o_ref, g_ref, z_ref):
        on, _ = _group_norm(o_ref[...])
        _, sl = _silu_parts(g_ref[0:t_lat, :])
        z_ref[...] = (on * sl).astype(BF16)

    return pl.pallas_call(
        body, name=name, grid=(H,),
        in_specs=[pl.BlockSpec((None, t_lat, HEAD_DIM), lambda h: (h, 0, 0)),
                  pl.BlockSpec((T, HEAD_DIM), lambda h: (0, 3 * heads + h))],
        out_specs=pl.BlockSpec((t_lat, HEAD_DIM), lambda h: (0, h)),
        out_shape=jax.ShapeDtypeStruct((t_lat, d_model), BF16),
        compiler_params=_params(("arbitrary",)),
    )(o, p)


def _shift_rows(u, k):
    rows = u.shape[0]
    row = lax.broadcasted_iota(jnp.int32, u.shape, 0)
    rolled = pltpu.roll(u, k % rows, axis=0)
    return jnp.where((row >= k) & (row < rows + k), rolled, 0.0)


def conv_fwd(z, p, w_conv, *, heads, t_lat, name):
    T = p.shape[0]
    cb_n = w_conv.shape[1] // LANE
    base = 4 * heads

    def body(z_in, bg_ref, cg_ref, u_ref, w_ref, z_ref):
        cu = cg_ref[0:t_lat, :] * u_ref[0:t_lat, :]
        c3 = _shift_rows(cu, 1) * w_ref[0:1, :] + cu * w_ref[1:2, :] + _shift_rows(cu, -1) * w_ref[2:3, :]
        z_ref[...] = (bg_ref[0:t_lat, :] * c3).astype(BF16)

    def col(part):
        return pl.BlockSpec((T, LANE), lambda cb: (0, base + part * cb_n + cb))

    return pl.pallas_call(
        body, name=name, grid=(cb_n,),
        in_specs=[ANY, col(0), col(1), col(2), pl.BlockSpec((CONV_WIDTH, LANE), lambda cb: (0, cb))],
        out_specs=pl.BlockSpec((t_lat, LANE), lambda cb: (0, heads + cb)),
        out_shape=jax.ShapeDtypeStruct(z.shape, z.dtype),
        input_output_aliases={0: 0},
        compiler_params=_params(("arbitrary",)),
    )(z, p, p, p, w_conv)


def out_proj(z, w_out, x, mod, *, tm, name):
    t_lat, D = z.shape

    def body(z_ref, w_ref, x_ref, mod_ref, xo_ref, y_ref):
        y = _nn(z_ref[...], w_ref[...])
        y_ref[...] = y.astype(BF16)
        xo_ref[...] = x_ref[...] + mod_ref[2:3, :] * y

    tok = pl.BlockSpec((tm, D), lambda i: (i, 0))
    return pl.pallas_call(
        body, name=name, grid=(t_lat // tm,),
        in_specs=[tok, pl.BlockSpec((D, D), lambda i: (0, 0)), tok, pl.BlockSpec((None, 3, D), lambda i: (0, 0, 0))],
        out_specs=[tok, tok],
        out_shape=[jax.ShapeDtypeStruct((t_lat, D), F32), jax.ShapeDtypeStruct((t_lat, D), BF16)],
        compiler_params=_params(("arbitrary",)),
    )(z, w_out, x, mod)


def out_proj_bwd(dxo, y, w_out, mod, *, tm, name):
    t_lat, D = dxo.shape

    def body(dxo_ref, y_ref, w_ref, mod_ref, dz_ref, dy_ref, red_ref):
        @pl.when(pl.program_id(0) == 0)
        def _():
            red_ref[...] = jnp.zeros_like(red_ref)

        dxo_v = dxo_ref[...]
        dyb = (mod_ref[2:3, :] * dxo_v).astype(BF16)
        dy_ref[...] = dyb
        dz_ref[...] = _nt(dyb, w_ref[...])
        red_ref[2:3, :] += jnp.sum(dxo_v * y_ref[...].astype(F32), axis=0, keepdims=True)

    tok = pl.BlockSpec((tm, D), lambda i: (i, 0))
    return pl.pallas_call(
        body, name=name, grid=(t_lat // tm,),
        in_specs=[tok, tok, pl.BlockSpec((D, D), lambda i: (0, 0)), pl.BlockSpec((None, 3, D), lambda i: (0, 0, 0))],
        out_specs=[tok, tok, pl.BlockSpec((8, D), lambda i: (0, 0))],
        out_shape=[jax.ShapeDtypeStruct((t_lat, D), F32), jax.ShapeDtypeStruct((t_lat, D), BF16),
                   jax.ShapeDtypeStruct((8, D), F32)],
        compiler_params=_params(("arbitrary",)),
    )(dxo, y, w_out, mod)


def ret_post_bwd(dz, o, p, *, heads, name):
    H, t_lat, _ = o.shape
    T, in_w = p.shape

    def body(dz_ref, o_ref, g_ref, do_ref, dp_ref):
        on, r = _group_norm(o_ref[...])
        gg = g_ref[0:t_lat, :]
        sig, sl = _silu_parts(gg)
        dret = dz_ref[...]
        don = dret * sl
        do_ref[...] = r * (don - jnp.mean(don, axis=-1, keepdims=True)
                           - on * jnp.mean(don * on, axis=-1, keepdims=True))
        dp_ref[0:t_lat, :] = (dret * on * _dsilu(gg, sig)).astype(BF16)
        dp_ref[t_lat:T, :] = jnp.zeros((T - t_lat, HEAD_DIM), BF16)

    return pl.pallas_call(
        body, name=name, grid=(H,),
        in_specs=[pl.BlockSpec((t_lat, HEAD_DIM), lambda h: (0, h)),
                  pl.BlockSpec((None, t_lat, HEAD_DIM), lambda h: (h, 0, 0)),
                  pl.BlockSpec((T, HEAD_DIM), lambda h: (0, 3 * heads + h))],
        out_specs=[pl.BlockSpec((None, t_lat, HEAD_DIM), lambda h: (h, 0, 0)),
                   pl.BlockSpec((T, HEAD_DIM), lambda h: (0, 3 * heads + h))],
        out_shape=[jax.ShapeDtypeStruct((H, t_lat, HEAD_DIM), F32), jax.ShapeDtypeStruct((T, in_w), BF16)],
        compiler_params=_params(("arbitrary",)),
    )(dz, o, p)


def conv_bwd(dp, dz, p, w_conv, *, heads, t_lat, name):
    T = p.shape[0]
    cb_n = w_conv.shape[1] // LANE
    base = 4 * heads

    def body(dp_in, dz_ref, bg_ref, cg_ref, u_ref, w_ref, dp_ref, dw_ref):
        part = pl.program_id(1)
        cg, u = cg_ref[0:t_lat, :], u_ref[0:t_lat, :]
        cu = cg * u
        dconv = dz_ref[...]
        dp_ref[t_lat:T, :] = jnp.zeros((T - t_lat, LANE), BF16)

        @pl.when(part == 0)
        def _():
            c3 = _shift_rows(cu, 1) * w_ref[0:1, :] + cu * w_ref[1:2, :] + _shift_rows(cu, -1) * w_ref[2:3, :]
            dp_ref[0:t_lat, :] = (dconv * c3).astype(BF16)
            dc3 = dconv * bg_ref[0:t_lat, :]
            dw_ref[0:1, :] = jnp.sum(dc3 * _shift_rows(cu, 1), axis=0, keepdims=True)
            dw_ref[1:2, :] = jnp.sum(dc3 * cu, axis=0, keepdims=True)
            dw_ref[2:3, :] = jnp.sum(dc3 * _shift_rows(cu, -1), axis=0, keepdims=True)

        @pl.when(part > 0)
        def _():
            dc3 = dconv * bg_ref[0:t_lat, :]
            dcu = (_shift_rows(dc3, -1) * w_ref[0:1, :] + dc3 * w_ref[1:2, :] + _shift_rows(dc3, 1) * w_ref[2:3, :])
            dp_ref[0:t_lat, :] = (dcu * jnp.where(part == 1, u, cg)).astype(BF16)

    def col(part):
        return pl.BlockSpec((T, LANE), lambda cb, pt: (0, base + part * cb_n + cb))

    return pl.pallas_call(
        body, name=name, grid=(cb_n, 3),
        in_specs=[ANY, pl.BlockSpec((t_lat, LANE), lambda cb, pt: (0, heads + cb)), col(0), col(1), col(2),
                  pl.BlockSpec((CONV_WIDTH, LANE), lambda cb, pt: (0, cb))],
        out_specs=[pl.BlockSpec((T, LANE), lambda cb, pt: (0, base + pt * cb_n + cb)),
                   pl.BlockSpec((CONV_WIDTH, LANE), lambda cb, pt: (0, cb))],
        out_shape=[jax.ShapeDtypeStruct(dp.shape, dp.dtype), jax.ShapeDtypeStruct(w_conv.shape, F32)],
        input_output_aliases={0: 0},
        compiler_params=_params(("arbitrary", "arbitrary")),
    )(dp, dz, p, p, p, w_conv)


def ret_unprep(dp, dq, dk, dv, cos, sin, *, t_lat, name):
    H, T, _ = dk.shape

    def body(dp_in, dq_ref, dk_ref, dv_ref, c_ref, s_ref, dp_ref):
        part = pl.program_id(0)

        @pl.when(part == 0)
        def _():
            dp_ref[0:t_lat, :] = _rope(dq_ref[...], c_ref[0:t_lat, :], -s_ref[0:t_lat, :]).astype(BF16)
            dp_ref[t_lat:T, :] = jnp.zeros((T - t_lat, HEAD_DIM), BF16)

        @pl.when(part == 1)
        def _():
            dp_ref[...] = (_rope(dk_ref[...], c_ref[...], -s_ref[...]) * K_SCALE).astype(BF16)

        @pl.when(part == 2)
        def _():
            dp_ref[...] = dv_ref[...].astype(BF16)

    def head_of(part):
        return lambda pt, h: (jnp.where(pt == part, h, 0), 0, 0)

    tab = pl.BlockSpec((T, HEAD_DIM), lambda pt, h: (0, 0))
    return pl.pallas_call(
        body, name=name, grid=(3, H),
        in_specs=[ANY, pl.BlockSpec((None, t_lat, HEAD_DIM), head_of(0)), pl.BlockSpec((None, T, HEAD_DIM), head_of(1)),
                  pl.BlockSpec((None, T, HEAD_DIM), head_of(2)), tab, tab],
        out_specs=pl.BlockSpec((T, HEAD_DIM), lambda pt, h: (0, pt * H + h)),
        out_shape=jax.ShapeDtypeStruct(dp.shape, dp.dtype),
        input_output_aliases={0: 0},
        compiler_params=_params(("arbitrary", "arbitrary")),
    )(dp, dq, dk, dv, cos, sin)


def norm_mod_fwd(x, mod, gn, *, tm, name):
    T, D = x.shape

    def body(x_ref, mod_ref, g_ref, h_ref):
        h_ref[...] = _norm_mod(x_ref[...], g_ref[...], mod_ref[0:1, :], mod_ref[1:2, :])

    tok = pl.BlockSpec((tm, D), lambda i: (i, 0))
    return pl.pallas_call(
        body, name=name, grid=(T // tm,),
        in_specs=[tok, pl.BlockSpec((None, 3, D), lambda i: (0, 0, 0)), pl.BlockSpec((1, D), lambda i: (0, 0))],
        out_specs=tok, out_shape=jax.ShapeDtypeStruct((T, D), F32),
        compiler_params=_params(("arbitrary",)),
    )(x, mod, gn)


def norm_mod_bwd(x, dh, dxo, mod, gn, *, tm, name):
    T, D = x.shape

    def body(x_ref, dh_ref, dxo_ref, mod_ref, g_ref, dx_ref, red_ref):
        @pl.when(pl.program_id(0) == 0)
        def _():
            red_ref[...] = jnp.zeros_like(red_ref)

        dxn, dshift, dscale, dg = _norm_mod_bwd(x_ref[...], dh_ref[...], g_ref[...], mod_ref[1:2, :])
        dx_ref[...] = dxo_ref[...] + dxn
        red_ref[0:1, :] += dshift
        red_ref[1:2, :] += dscale
        red_ref[3:4, :] += dg

    tok = pl.BlockSpec((tm, D), lambda i: (i, 0))
    return pl.pallas_call(
        body, name=name, grid=(T // tm,),
        in_specs=[tok, tok, tok, pl.BlockSpec((None, 3, D), lambda i: (0, 0, 0)), pl.BlockSpec((1, D), lambda i: (0, 0))],
        out_specs=[tok, pl.BlockSpec((8, D), lambda i: (0, 0))],
        out_shape=[jax.ShapeDtypeStruct((T, D), F32), jax.ShapeDtypeStruct((8, D), F32)],
        compiler_params=_params(("arbitrary",)),
    )(x, dh, dxo, mod, gn)


def _window_sum(u, w, lead):
    T, C = u.shape
    ext = jnp.concatenate([u, jnp.zeros((POOL_PAD, C), F32)], axis=0)
    k = 1
    while k < w:
        ext = ext + _shift_rows(ext, k)
        k *= 2
    return _shift_rows(ext, -lead)[0:T, :]


def _window_count(T, C, w):
    t = lax.broadcasted_iota(jnp.int32, (T, C), 0)
    lo = jnp.clip(t - w // 2, 0, T)
    hi = jnp.clip(t + (w - w // 2), 0, T)
    return (hi - lo).astype(F32)


def pool_fwd(h, x, pool_w, scale, mod, *, name):
    T, D = h.shape
    G, Cg, _ = pool_w.shape
    ns = Cg // LANE

    def body(h_ref, x_ref, w_ref, sc_ref, mod_ref, xo_ref, pl_ref, yl_ref, acc):
        g, s = pl.program_id(0), pl.program_id(1)
        hv = h_ref[...]
        for gi, win in enumerate(POOL_WINDOWS):
            @pl.when(g == gi)
            def _():
                mean = _window_sum(hv, win, win // 2 - 1) / _window_count(T, LANE, win)
                pooled = (mean - hv).astype(BF16)
                pl_ref[...] = pooled
                contrib = _nn(pooled, w_ref[...])

                @pl.when(s == 0)
                def _():
                    acc[...] = contrib

                @pl.when(s > 0)
                def _():
                    acc[...] += contrib

        @pl.when(s == ns - 1)
        def _():
            yl = acc[...]
            yl_ref[...] = yl.astype(BF16)
            xo_ref[...] = x_ref[...] + mod_ref[2:3, :] * (yl * sc_ref[...])

    grp = pl.BlockSpec((T, Cg), lambda g, s: (0, g))
    sub = pl.BlockSpec((T, LANE), lambda g, s: (0, g * ns + s))
    return pl.pallas_call(
        body, name=name, grid=(G, ns),
        in_specs=[sub, grp, pl.BlockSpec((None, LANE, Cg), lambda g, s: (g, s, 0)),
                  pl.BlockSpec((1, Cg), lambda g, s: (0, g)), pl.BlockSpec((None, 3, Cg), lambda g, s: (0, 0, g))],
        out_specs=[grp, sub, grp],
        out_shape=[jax.ShapeDtypeStruct((T, D), F32), jax.ShapeDtypeStruct((T, D), BF16),
                   jax.ShapeDtypeStruct((T, D), BF16)],
        scratch_shapes=[pltpu.VMEM((T, Cg), F32)],
        compiler_params=_params(("arbitrary", "arbitrary")),
    )(h, x, pool_w, scale, mod)


def pool_bwd(dxo, pooled, yl, pool_w, scale, mod, *, name):
    T, D = dxo.shape
    G, Cg, _ = pool_w.shape
    ns = Cg // LANE

    def body(dxo_ref, pl_ref, yl_ref, w_ref, sc_ref, mod_ref, dh_ref, dw_ref, red_ref, dyl):
        g, s = pl.program_id(0), pl.program_id(1)

        @pl.when(s == 0)
        def _():
            dxo_v = dxo_ref[...]
            ylv = yl_ref[...].astype(F32)
            dy = mod_ref[2:3, :] * dxo_v
            dyl[...] = (dy * sc_ref[...]).astype(BF16)
            red_ref[...] = jnp.zeros_like(red_ref)
            red_ref[2:3, :] = jnp.sum(dxo_v * (ylv * sc_ref[...]), axis=0, keepdims=True)
            red_ref[4:5, :] = jnp.sum(dy * ylv, axis=0, keepdims=True)

        dylv = dyl[...]
        dpool = _nt(dylv, w_ref[...])
        dw_ref[...] = _tn(pl_ref[...], dylv).astype(BF16)
        for gi, win in enumerate(POOL_WINDOWS):
            @pl.when(g == gi)
            def _():
                spread = _window_sum(dpool / _window_count(T, LANE, win), win, win // 2)
                dh_ref[...] = spread - dpool

    grp = pl.BlockSpec((T, Cg), lambda g, s: (0, g))
    sub = pl.BlockSpec((T, LANE), lambda g, s: (0, g * ns + s))
    wsub = pl.BlockSpec((None, LANE, Cg), lambda g, s: (g, s, 0))
    return pl.pallas_call(
        body, name=name, grid=(G, ns),
        in_specs=[grp, sub, grp, wsub, pl.BlockSpec((1, Cg), lambda g, s: (0, g)),
                  pl.BlockSpec((None, 3, Cg), lambda g, s: (0, 0, g))],
        out_specs=[sub, wsub, pl.BlockSpec((8, Cg), lambda g, s: (0, g))],
        out_shape=[jax.ShapeDtypeStruct((T, D), F32), jax.ShapeDtypeStruct((G, Cg, Cg), BF16),
                   jax.ShapeDtypeStruct((8, D), F32)],
        scratch_shapes=[pltpu.VMEM((T, Cg), BF16)],
        compiler_params=_params(("arbitrary", "arbitrary")),
    )(dxo, pooled, yl, pool_w, scale, mod)


def final_loss(x, gn, target, *, tm, name):
    T, D = x.shape

    def body(x_ref, g_ref, t_ref, loss_ref, dx_ref, red_ref):
        @pl.when(pl.program_id(0) == 0)
        def _():
            loss_ref[...] = jnp.zeros_like(loss_ref)
            red_ref[...] = jnp.zeros_like(red_ref)

        xx, g = x_ref[...], g_ref[...]
        r = lax.rsqrt(jnp.mean(xx * xx, axis=-1, keepdims=True) + EPS)
        xhat = xx * r
        err = xhat * g - t_ref[...]
        loss_ref[...] += 0.5 * jnp.sum(jnp.mean(err * err, axis=-1, keepdims=True))
        dy = err / D
        red_ref[0:1, :] += jnp.sum(dy * xhat, axis=0, keepdims=True)
        dxh = dy * g
        dx_ref[...] = r * (dxh - xhat * jnp.mean(dxh * xhat, axis=-1, keepdims=True))

    tok = pl.BlockSpec((tm, D), lambda i: (i, 0))
    return pl.pallas_call(
        body, name=name, grid=(T // tm,),
        in_specs=[tok, pl.BlockSpec((1, D), lambda i: (0, 0)), tok],
        out_specs=[pl.BlockSpec((8, LANE), lambda i: (0, 0)), tok, pl.BlockSpec((8, D), lambda i: (0, 0))],
        out_shape=[jax.ShapeDtypeStruct((8, LANE), F32), jax.ShapeDtypeStruct((T, D), F32),
                   jax.ShapeDtypeStruct((8, D), F32)],
        compiler_params=_params(("arbitrary",)),
    )(x, gn, target)


def _rope_tables(t_lat, t_ctx):
    quarter = HEAD_DIM // 4
    pos = jnp.arange(t_lat)
    inv = ROPE_BASE ** (-jnp.arange(quarter, dtype=F32) / quarter)
    ang = jnp.concatenate([(pos // GRID_W).astype(F32)[:, None] * inv, (pos % GRID_W).astype(F32)[:, None] * inv], axis=-1)
    cos, sin = jnp.cos(ang), jnp.sin(ang)
    cos = jnp.concatenate([jnp.concatenate([cos, cos], axis=-1), jnp.ones((t_ctx, HEAD_DIM), F32)], axis=0)
    sin = jnp.concatenate([jnp.concatenate([-sin, sin], axis=-1), jnp.zeros((t_ctx, HEAD_DIM), F32)], axis=0)
    return cos, sin


def _ffn_grads(h, da, db, s, dy, tag):
    nb, T, cw = da.shape
    D = h.shape[1]
    tk = T
    g_gate = matmul_tn(da, h, nb=nb, m=cw, n=D, a_blocked=True, b_blocked=False, tk=tk, name=f"wgrad_gate_{tag}")
    g_up = matmul_tn(db, h, nb=nb, m=cw, n=D, a_blocked=True, b_blocked=False, tk=tk, name=f"wgrad_up_{tag}")
    g_down = matmul_tn(s, dy, nb=nb, m=cw, n=D, a_blocked=True, b_blocked=False, tk=tk, name=f"wgrad_down_{tag}")
    return [g_gate, g_up, g_down]


def kernel(x, c, ctx, c_ctx, w_mod, b_mod, norm_ffn1, norm_mix, norm_ffn2, ffn1_w_gate, ffn1_w_up, ffn1_w_down, ffn2_w_gate, ffn2_w_up, ffn2_w_down, mix_w_in, mix_w_conv, mix_w_out, ret_decay_fwd, ret_decay_bwd, pool_w, pool_scale, final_norm, loss_target, m_c_ctx, m_w_mod, m_b_mod, m_norm_ffn1, m_norm_mix, m_norm_ffn2, m_ffn1_w_gate, m_ffn1_w_up, m_ffn1_w_down, m_ffn2_w_gate, m_ffn2_w_up, m_ffn2_w_down, m_mix_w_in, m_mix_w_conv, m_mix_w_out, m_ret_decay_fwd, m_ret_decay_bwd, m_pool_w, m_pool_scale, m_final_norm, v_c_ctx, v_w_mod, v_b_mod, v_norm_ffn1, v_norm_mix, v_norm_ffn2, v_ffn1_w_gate, v_ffn1_w_up, v_ffn1_w_down, v_ffn2_w_gate, v_ffn2_w_up, v_ffn2_w_down, v_mix_w_in, v_mix_w_conv, v_mix_w_out, v_ret_decay_fwd, v_ret_decay_bwd, v_pool_w, v_pool_scale, v_final_norm):
    t_lat, D = x.shape[1], x.shape[2]
    t_ctx = ctx.shape[1]
    T = t_lat + t_ctx
    heads = ret_decay_fwd.shape[1]
    mod_w = w_mod.shape[2]
    tm = 256
    tmf = 512 if t_lat % 512 == 0 else 256
    tq = 512 if t_lat % 512 == 0 else 256
    tk = T // 3 if (T % 3 == 0 and (T // 3) % 256 == 0) else 256

    ax, ay, ac = lax.axis_index("x"), lax.axis_index("y"), lax.axis_index("c")
    me = 4 * ax + 2 * ay + ac
    own_block = jnp.reshape(me, (1,)).astype(jnp.int32)
    peer_blocks = jnp.stack([4 * (1 - ax) + 2 * ay + ac, 4 * ax + 2 * (1 - ay) + ac,
                             4 * (1 - ax) + 2 * (1 - ay) + ac]).astype(jnp.int32)

    (c_all,) = all_gather([c], name="gather_cond")
    cond16 = jnp.concatenate([c_all.reshape(N_DEV, D), c_ctx[None, :], jnp.zeros((7, D), F32)], axis=0)
    b_blk = lax.dynamic_slice_in_dim(b_mod, me * mod_w, mod_w, axis=1)[:, None, :]
    m_blk = adaln_fwd(cond16, w_mod, b_blk, name="adaln_fwd")
    m_all, w_conv, pscale = all_gather([m_blk, mix_w_conv[0], pool_scale], name="gather_mod")
    mods = jnp.transpose(m_all, (1, 2, 0, 3)).reshape(2, 16, N_MOD, D)
    mod_x = lax.dynamic_index_in_dim(mods, me, axis=1, keepdims=False)
    mod_c = mods[0, 8]
    w_conv = jnp.transpose(w_conv, (1, 0, 2)).reshape(CONV_WIDTH, -1)
    pscale = pscale.reshape(1, D)

    n_grp, grp_rows, grp_w = pool_w.shape[1:]
    ffn1_w = [(jnp.swapaxes(ffn1_w_gate, 1, 2), "gate"), (jnp.swapaxes(ffn1_w_up, 1, 2), "up"), (ffn1_w_down, "down")]
    ffn2_w = [(jnp.swapaxes(ffn2_w_gate, 1, 2), "gate"), (jnp.swapaxes(ffn2_w_up, 1, 2), "up"), (ffn2_w_down, "down")]
    groups = {
        "01": [(w, 0, f"ffn1_{k}") for w, k in ffn1_w],
        "mix": [(mix_w_in, 0, "w_in"), (mix_w_out, 0, "w_out")],
        "02": [(w, 0, f"ffn2_{k}") for w, k in ffn2_w],
        "11": [(w, 1, f"ffn1_{k}") for w, k in ffn1_w] + [(pool_w.reshape(1, n_grp * grp_rows, grp_w), 0, "pool_w")],
        "12": [(w, 1, f"ffn2_{k}") for w, k in ffn2_w],
    }
    started = {}

    def start(tag, after):
        lands = [cast_place(w, l, own_block, name=f"cast_{nm}_{l}") for w, l, nm in groups[tag]]
        started[tag] = gather_start(lands, after, name=f"gather_start_{tag}")
        return started[tag][-1][0:1, 0:1]

    def finish(tag, after):
        return gather_finish(gather_wait(started[tag], after, name=f"gather_wait_{tag}"), name=f"gather_finish_{tag}")

    lg = jnp.concatenate([jax.nn.log_sigmoid(ret_decay_fwd), jax.nn.log_sigmoid(ret_decay_bwd)], axis=0)
    cos, sin = _rope_tables(t_lat, t_ctx)

    def mod3(l, k, with_ctx=False, tie=None):
        rows = mod_x[l, 3 * k:3 * k + 3][None]
        if with_ctx:
            rows = jnp.concatenate([rows, mod_c[3 * k:3 * k + 3][None]], axis=0)
        return rows if tie is None else rows + tie

    tie = start("01", m_all)
    x0 = jnp.concatenate([x[0], ctx[0]], axis=0) + tie
    wg01, wu01, wd01 = finish("01", x0)
    tie = start("mix", wd01)
    x1, h1, a1, b1, y1 = ffn_fwd(x0, mod3(0, 0, True, tie), norm_ffn1[0:1], wg01, wu01, wd01,
                                 tm=tm, t_first=t_lat, name="ffn_fwd_01")
    w_in, w_out = finish("mix", h1)
    tie = start("02", w_out)
    w_out = w_out.reshape(D, D)
    p, hm = proj_in(x1, mod3(0, 1, True, tie), norm_mix[0:1], w_in, tm=tm, t_first=t_lat, name="proj_in")
    qr, kr, vr = ret_prep(p, cos, sin, heads=heads, tm=tm, name="ret_prep")
    o = ret_fwd(qr, kr, vr, lg, t_lat=t_lat, tq=tq, tk=tk, name="ret_fwd")
    wg02, wu02, wd02 = finish("02", o)
    tie = start("11", wd02)
    z = ret_post_fwd(o, p, heads=heads, d_model=D, name="ret_post_fwd")
    z = conv_fwd(z, p, w_conv, heads=heads, t_lat=t_lat, name="conv_fwd")
    x2, ym = out_proj(z, w_out, x1, mod3(0, 1, tie=tie), tm=tm, name="out_proj")
    x3, h3, a3, b3, y3 = ffn_fwd(x2, mod3(0, 2, tie=tie), norm_ffn2[0:1], wg02, wu02, wd02, tm=tmf, t_first=t_lat, name="ffn_fwd_02")
    wg11, wu11, wd11, pw = finish("11", h3)
    tie = start("12", wd11)
    pw = jnp.transpose(pw.reshape(N_DEV, n_grp, grp_rows, grp_w), (1, 0, 2, 3)).reshape(n_grp, N_DEV * grp_rows, grp_w)
    x4, h4, a4, b4, y4 = ffn_fwd(x3, mod3(1, 0, tie=tie), norm_ffn1[1:2], wg11, wu11, wd11, tm=tmf, t_first=t_lat, name="ffn_fwd_11")
    hp = norm_mod_fwd(x4, mod3(1, 1), norm_mix[1:2], tm=tm, name="pool_norm_fwd")
    x5, pooled, yl = pool_fwd(hp, x4, pw, pscale, mod3(1, 1), name="pool_fwd")
    wg12, wu12, wd12 = finish("12", yl)
    x6, h6, a6, b6, y6 = ffn_fwd(x5, mod3(1, 2), norm_ffn2[1:2], wg12, wu12, wd12, tm=tmf, t_first=t_lat, name="ffn_fwd_12")
    loss_part, dx6, red_fn = final_loss(x6, final_norm[None, :], loss_target[0], tm=tm, name="final_loss")
    loss = lax.psum(loss_part[0, 0], ("x", "y", "c"))

    reducing = {}

    def reduce_begin(tag, grads, after):
        reducing[tag] = sibling_exchange_start(grads, after, name=f"sib_start_{tag}")
        return reducing[tag][-1][0:1, 0:1]

    def reduce_middle(tag, after):
        grads, stages, directs = sibling_exchange_wait(reducing[tag], after, name=f"sib_wait_{tag}")
        sums = [chip_sum(g, st, peer_blocks, name=f"chip_sum_{tag}_{t}") for t, (g, st) in enumerate(zip(grads, stages))]
        handle = chip_exchange_start(sums, name=f"rs_start_{tag}")
        reducing[tag] = (grads, directs, handle)
        return handle[-1][0:1, 0:1]

    dx5, da6, db6, s6, dy6, red12 = ffn_bwd_tok(x5, dx6, y6, a6, b6, mod3(1, 2), norm_ffn2[1:2], wg12, wu12, wd12,
                                                tm=tmf, t_first=t_lat, name="ffn_bwd_12")
    tie = reduce_begin("12", _ffn_grads(h6, da6, db6, s6, dy6, "12"), dx5)
    dhp, g_pw, red_pool = pool_bwd(dx5, pooled, yl, pw, pscale, mod3(1, 1, tie=tie), name="pool_bwd")
    dx4, red_pn = norm_mod_bwd(x4, dhp, dx5, mod3(1, 1), norm_mix[1:2], tm=tm, name="pool_norm_bwd")
    dx3, da4, db4, s4, dy4, red11 = ffn_bwd_tok(x3, dx4, y4, a4, b4, mod3(1, 0), norm_ffn1[1:2], wg11, wu11, wd11,
                                                tm=tmf, t_first=t_lat, name="ffn_bwd_11")
    g_pw = jnp.transpose(g_pw.reshape(n_grp, N_DEV, grp_rows, grp_w), (1, 0, 2, 3)).reshape(N_DEV, n_grp * grp_rows, grp_w)
    tie = reduce_begin("11", list(_ffn_grads(h4, da4, db4, s4, dy4, "11")) + [g_pw], dx3)
    tie = tie + reduce_middle("12", reducing["11"][2])
    dx2, da3, db3, s3, dy3, red02 = ffn_bwd_tok(x2, dx3, y3, a3, b3, mod3(0, 2, tie=tie), norm_ffn2[0:1], wg02, wu02, wd02,
                                                tm=tmf, t_first=t_lat, name="ffn_bwd_02")
    tie = reduce_begin("02", _ffn_grads(h3, da3, db3, s3, dy3, "02"), dx2)
    tie = tie + reduce_middle("11", reducing["02"][2])
    dz, dym, red_op = out_proj_bwd(dx2, ym, w_out, mod3(0, 1, tie=tie), tm=tm, name="out_proj_bwd")
    g_wout = matmul_tn(z, dym, nb=N_DEV, m=D // N_DEV, n=D, a_blocked=False, b_blocked=False,
                       tk=t_lat, name="wgrad_out")
    do, dp = ret_post_bwd(dz, o, p, heads=heads, name="ret_post_bwd")
    dp, g_conv = conv_bwd(dp, dz, p, w_conv, heads=heads, t_lat=t_lat, name="conv_bwd")
    dq, dk, dv, dlg = ret_bwd(qr, kr, vr, do, lg, t_lat=t_lat, tq=tq, tk=tk, name="ret_bwd")
    dp = ret_unprep(dp, dq, dk, dv, cos, sin, t_lat=t_lat, name="ret_unprep")
    dx1, red_mix = proj_bwd_tok(x1, dx2, dp, mod3(0, 1, True), norm_mix[0:1], w_in, tm=tm, name="proj_bwd")
    g_win = matmul_tn(hm, dp, nb=N_DEV, m=D, n=w_in.shape[2], a_blocked=False, b_blocked=False, tk=T, name="wgrad_in")
    tie = reduce_begin("mix", [g_win, g_wout], dx1)
    tie = tie + reduce_middle("02", reducing["mix"][2])
    dx0, da1, db1, s1, dy1, red01 = ffn_bwd_tok(x0, dx1, y1, a1, b1, mod3(0, 0, True, tie), norm_ffn1[0:1], wg01, wu01, wd01,
                                                tm=tm, t_first=t_lat, name="ffn_bwd_01")
    res = {}

    dm_x = jnp.stack([jnp.concatenate([red01[0, 0:3], red_mix[0, 0:2], red_op[2:3], red02[0, 0:3]], axis=0),
                      jnp.concatenate([red11[0, 0:3], red_pn[0:2], red_pool[2:3], red12[0, 0:3]], axis=0)])
    dm_c = jnp.concatenate([red01[1, 0:3], red_mix[1, 0:2], jnp.zeros((4, D), F32)], axis=0)
    d_lg = dlg[:, 0:2, 0].T
    d_dec_f = d_lg[0:1] * jax.nn.sigmoid(-ret_decay_fwd)
    d_dec_b = d_lg[1:2] * jax.nn.sigmoid(-ret_decay_bwd)
    pieces = [dm_x.reshape(-1), dm_c.reshape(-1),
              jnp.stack([red01[0, 3] + red01[1, 3], red11[0, 3]]).reshape(-1),
              jnp.stack([red_mix[0, 3] + red_mix[1, 3], red_pn[3]]).reshape(-1),
              jnp.stack([red02[0, 3], red12[0, 3]]).reshape(-1),
              red_fn[0], d_dec_f.reshape(-1), d_dec_b.reshape(-1), g_conv.reshape(-1), red_pool[4]]
    sizes = [int(a.shape[0]) for a in pieces]
    n_pack = sum(sizes)
    n_pad = -n_pack % 8192
    packed = jnp.concatenate(pieces + [jnp.zeros((n_pad,), F32)])[None, :]
    (packed_all,) = all_gather([packed], name="gather_partials")
    total = sum_partials(packed_all, name="sum_partials")[0]
    offs = [0]
    for s in sizes:
        offs.append(offs[-1] + s)
    seg = [total[offs[i]:offs[i + 1]] for i in range(len(sizes))]
    g_dm = seg[0].reshape(2, N_MOD * D)
    g_dmc = seg[1].reshape(N_MOD * D)
    g_b_mod = g_dm.at[0].add(g_dmc)
    g_norm_ffn1, g_norm_mix, g_norm_ffn2 = (seg[k].reshape(2, D) for k in (2, 3, 4))
    g_final = seg[5]
    g_dec_f, g_dec_b = seg[6].reshape(1, heads), seg[7].reshape(1, heads)
    g_conv_all = seg[8].reshape(CONV_WIDTH, -1)
    g_pscale_all = seg[9]
    conv_w = mix_w_conv.shape[2]
    g_w_conv = lax.dynamic_slice_in_dim(g_conv_all, me * conv_w, conv_w, axis=1)[None]
    ps_w = pool_scale.shape[1]
    g_pool_scale = lax.dynamic_slice_in_dim(g_pscale_all, me * ps_w, ps_w, axis=0)[None]

    dm_rows = packed_all[:, 0, 0:offs[1]].reshape(N_DEV, 2, N_MOD * D)
    dmc_rows = packed_all[:, 0, offs[1]:offs[2]].reshape(N_DEV, N_MOD * D)
    dm_x_blk = jnp.transpose(lax.dynamic_slice_in_dim(dm_rows, me * mod_w, mod_w, axis=2), (1, 0, 2))
    dm_c_blk = jnp.stack([lax.dynamic_slice_in_dim(dmc_rows, me * mod_w, mod_w, axis=1),
                          jnp.zeros((N_DEV, mod_w), F32)])
    g_w_mod, d_w_mod, nm_w_mod, nv_w_mod, cctx_part = adaln_bwd(cond16, dm_x_blk, dm_c_blk, w_mod, m_w_mod, v_w_mod,
                                                                name="adaln_bwd")
    cpad = jnp.concatenate([cctx_part[0], jnp.zeros((8192 - D,), F32)])[None, :] if D < 8192 else cctx_part[0:1]
    (cctx_all,) = all_gather([cpad], name="gather_cctx")
    g_c_ctx = sum_partials(cctx_all, name="sum_cctx")[0, :D]

    small = [("c_ctx", g_c_ctx, c_ctx, m_c_ctx, v_c_ctx), ("b_mod", g_b_mod, b_mod, m_b_mod, v_b_mod),
             ("norm_ffn1", g_norm_ffn1, norm_ffn1, m_norm_ffn1, v_norm_ffn1),
             ("norm_mix", g_norm_mix, norm_mix, m_norm_mix, v_norm_mix),
             ("norm_ffn2", g_norm_ffn2, norm_ffn2, m_norm_ffn2, v_norm_ffn2),
             ("mix_w_conv", g_w_conv, mix_w_conv, m_mix_w_conv, v_mix_w_conv),
             ("ret_decay_fwd", g_dec_f, ret_decay_fwd, m_ret_decay_fwd, v_ret_decay_fwd),
             ("ret_decay_bwd", g_dec_b, ret_decay_bwd, m_ret_decay_bwd, v_ret_decay_bwd),
             ("pool_scale", g_pool_scale, pool_scale, m_pool_scale, v_pool_scale),
             ("final_norm", g_final, final_norm, m_final_norm, v_final_norm)]
    ssz = [int(a[1].size) for a in small]
    spad = -sum(ssz) % LANE

    def pack(k):
        return jnp.concatenate([a[k].reshape(-1) for a in small] + [jnp.ones((spad,), F32)])[None, :]

    sd, sm, sv = adam_plain(pack(1), pack(2), pack(3), pack(4), name="adam_small")
    soff = [0]
    for s in ssz:
        soff.append(soff[-1] + s)
    for i, (nm, g, w, _, _) in enumerate(small):
        res[nm] = [g.reshape(w.shape)] + [a[0, soff[i]:soff[i + 1]].reshape(w.shape) for a in (sd, sm, sv)]
    res["w_mod"] = [g_w_mod, d_w_mod, nm_w_mod, nv_w_mod]

    after, pending = sd, "mix"
    for kind, lhs, rhs in (("gate", da1, h1), ("up", db1, h1), ("down", s1, dy1)):
        g = matmul_tn(lhs, rhs, nb=N_DEV, m=lhs.shape[2], n=D, a_blocked=True, b_blocked=False, tk=T,
                      name=f"wgrad_{kind}_01", after=after)
        reduce_begin(f"01_{kind}", [g], after)
        reduce_middle(pending, reducing[f"01_{kind}"][2])
        after = reducing[pending][2][2]
        pending = f"01_{kind}"
    tie = reduce_middle(pending, after)
    grad_x = dx0[:t_lat][None] + tie

    last = [grad_x]

    def reduce_finish(tag):
        grads, directs, handle = reducing[tag]
        landed = chip_exchange_wait(handle, last[0], name=f"rs_wait_{tag}")
        return list(zip(grads, directs, landed))

    def big(w, m, v, l, part, prev, nm, transposed=False):
        if transposed:
            w, m, v = (jnp.swapaxes(a, 1, 2) for a in (w, m, v))
        outs = adam_reduced(own_block, part[0], part[1], part[2], w, m, v, l, prev, name=nm)
        last[0] = outs[0]
        return outs

    ffn1 = [(ffn1_w_gate, m_ffn1_w_gate, v_ffn1_w_gate), (ffn1_w_up, m_ffn1_w_up, v_ffn1_w_up),
            (ffn1_w_down, m_ffn1_w_down, v_ffn1_w_down)]
    ffn2 = [(ffn2_w_gate, m_ffn2_w_gate, v_ffn2_w_gate), (ffn2_w_up, m_ffn2_w_up, v_ffn2_w_up),
            (ffn2_w_down, m_ffn2_w_down, v_ffn2_w_down)]
    kinds = ["gate", "up", "down"]
    half = {}
    parts = reduce_finish("12")
    for t, (w, m, v) in enumerate(ffn2):
        half[f"ffn2_w_{kinds[t]}"] = big(w, m, v, 1, parts[t], None, f"adam_ffn2_w_{kinds[t]}_1", transposed=t < 2)
    parts = reduce_finish("11")
    for t, (w, m, v) in enumerate(ffn1):
        half[f"ffn1_w_{kinds[t]}"] = big(w, m, v, 1, parts[t], None, f"adam_ffn1_w_{kinds[t]}_1", transposed=t < 2)
    res["pool_w"] = [a.reshape(pool_w.shape) for a in
                     big(pool_w.reshape(1, n_grp * grp_rows, grp_w), m_pool_w.reshape(1, n_grp * grp_rows, grp_w),
                         v_pool_w.reshape(1, n_grp * grp_rows, grp_w), 0, parts[3], None, "adam_pool_w")]
    parts = reduce_finish("02")
    for t, (w, m, v) in enumerate(ffn2):
        nm = f"ffn2_w_{kinds[t]}"
        outs = big(w, m, v, 0, parts[t], half[nm], f"adam_{nm}_0", transposed=t < 2)
        res[nm] = [jnp.swapaxes(a, 1, 2) for a in outs] if t < 2 else outs
    parts = reduce_finish("mix")
    res["mix_w_in"] = big(mix_w_in, m_mix_w_in, v_mix_w_in, 0, parts[0], None, "adam_mix_w_in")
    res["mix_w_out"] = big(mix_w_out, m_mix_w_out, v_mix_w_out, 0, parts[1], None, "adam_mix_w_out")
    for t, (w, m, v) in enumerate(ffn1):
        nm = f"ffn1_w_{kinds[t]}"
        outs = big(w, m, v, 0, reduce_finish(f"01_{kinds[t]}")[0], half[nm], f"adam_{nm}_0", transposed=t < 2)
        res[nm] = [jnp.swapaxes(a, 1, 2) for a in outs] if t < 2 else outs

    order = ["c_ctx", "w_mod", "b_mod", "norm_ffn1", "norm_mix", "norm_ffn2", "ffn1_w_gate", "ffn1_w_up", "ffn1_w_down",
             "ffn2_w_gate", "ffn2_w_up", "ffn2_w_down", "mix_w_in", "mix_w_conv", "mix_w_out", "ret_decay_fwd",
             "ret_decay_bwd", "pool_w", "pool_scale", "final_norm"]
    return (loss, grad_x, *[res[n][0] for n in order], *[res[n][1] for n in order],
            *[res[n][2] for n in order], *[res[n][3] for n in order])
```

```python
import jax
import jax.numpy as jnp
from jax import lax
from jax.experimental import pallas as pl
from jax.experimental.pallas import tpu as pltpu

F32 = jnp.float32
BF16 = jnp.bfloat16
MESH = pl.DeviceIdType.MESH

N_DEV = 8
N_MOD = 9
EPS = 1e-6
GN_EPS = 1e-5
MACARON = 0.5
HEAD_DIM = 128
K_SCALE = HEAD_DIM ** -0.5
ROPE_BASE = 10000.0
GRID_W = 64
CONV_WIDTH = 3
POOL_WINDOWS = (2, 4, 8, 16)
POOL_PAD = 16

ADAM_LR = 0.001
ADAM_B1 = 0.9
ADAM_B2 = 0.999
ADAM_EPS = 1e-08
ADAM_WD = 0.01
ADAM_STEP = 10

LANE = 128
ROW_CHUNK = 128
MM_ROWS = 256
VMEM_LIMIT = 56 * 1024 * 1024
ANY = pl.BlockSpec(memory_space=pl.ANY)


def _params(sem=None):
    kw = dict(vmem_limit_bytes=VMEM_LIMIT)
    if sem is not None:
        kw["dimension_semantics"] = sem
    return pltpu.CompilerParams(**kw)


def _nt(a, b):
    return lax.dot_general(a, b, (((1,), (1,)), ((), ())), preferred_element_type=F32)


def _tn(a, b):
    return lax.dot_general(a, b, (((0,), (0,)), ((), ())), preferred_element_type=F32)


def _nn(a, b):
    return jnp.dot(a, b, preferred_element_type=F32)


def _silu_parts(a):
    sig = jax.nn.sigmoid(a)
    return sig, a * sig


def _dsilu(a, sig):
    return sig * (1.0 + a * (1.0 - sig))


def _norm_mod(x, g, shift, scale):
    r = lax.rsqrt(jnp.mean(x * x, axis=-1, keepdims=True) + EPS)
    return (x * r * g) * (1.0 + scale) + shift


def _norm_mod_bwd(x, dh, g, scale):
    r = lax.rsqrt(jnp.mean(x * x, axis=-1, keepdims=True) + EPS)
    xhat = x * r
    dn = dh * (1.0 + scale)
    dshift = jnp.sum(dh, axis=0, keepdims=True)
    dscale = jnp.sum(dh * (xhat * g), axis=0, keepdims=True)
    dg = jnp.sum(dn * xhat, axis=0, keepdims=True)
    dxh = dn * g
    dx = r * (dxh - xhat * jnp.mean(dxh * xhat, axis=-1, keepdims=True))
    return dx, dshift, dscale, dg


def _mod_row(mod_ref, k, row0, nrows, t_first):
    if mod_ref.shape[0] == 1:
        return mod_ref[0, k:k + 1, :]
    row = row0 + lax.broadcasted_iota(jnp.int32, (nrows, 1), 0)
    return jnp.where(row >= t_first, mod_ref[1, k:k + 1, :], mod_ref[0, k:k + 1, :])


def _group_index(i, tiles_first, n_groups):
    if n_groups == 1:
        return 0
    return jnp.where(i >= tiles_first, 1, 0)


def _mesh_pos():
    x, y, c = lax.axis_index("x"), lax.axis_index("y"), lax.axis_index("c")
    chips = [(1 - x, y), (x, 1 - y), (1 - x, 1 - y)]
    return x, y, c, chips


def _flat(px, py, pc):
    return 4 * px + 2 * py + pc


def all_gather(shards, name):
    n = len(shards)

    def body(*refs):
        ins, outs = refs[:n], refs[n:2 * n]
        send_sems, recv_sems, local_sems = refs[2 * n:]
        x, y, c, chips = _mesh_pos()
        me, sibling = (x, y, c), (x, y, 1 - c)

        def copy(t, k, block, to, src=None):
            dst = outs[t].at[_flat(*block)]
            return pltpu.make_async_remote_copy(
                src_ref=dst if src is None else src, dst_ref=dst,
                send_sem=send_sems.at[t, k], recv_sem=recv_sems.at[t, k],
                device_id=to, device_id_type=MESH)

        mine = [pltpu.make_async_copy(ins[t], outs[t].at[_flat(*me)], local_sems.at[t]) for t in range(n)]
        for cp in mine:
            cp.start()
        first = []
        for t in range(n):
            first.append(copy(t, 0, me, sibling, src=ins[t]))
            first += [copy(t, 1 + j, me, (*chip, c), src=ins[t]) for j, chip in enumerate(chips)]
        for cp in first:
            cp.start()
        passed = []
        for t in range(n):
            for j, chip in enumerate(chips):
                copy(t, 1 + j, (*chip, c), me).wait_recv()
                fwd = copy(t, 4 + j, (*chip, c), sibling)
                fwd.start()
                passed.append(fwd)
        for t in range(n):
            copy(t, 0, sibling, me).wait_recv()
            for j, chip in enumerate(chips):
                copy(t, 4 + j, (*chip, 1 - c), me).wait_recv()
        for cp in first + passed:
            cp.wait_send()
        for cp in mine:
            cp.wait()

    return pl.pallas_call(
        body, name=name,
        out_shape=[jax.ShapeDtypeStruct((N_DEV,) + s.shape, s.dtype) for s in shards],
        in_specs=[ANY] * n, out_specs=[ANY] * n,
        scratch_shapes=[pltpu.SemaphoreType.DMA((n, 7)), pltpu.SemaphoreType.DMA((n, 7)), pltpu.SemaphoreType.DMA((n,))],
    )(*shards)


HBM_SPEC = pl.BlockSpec(memory_space=pltpu.HBM)
SEM_SPEC = pl.BlockSpec(memory_space=pltpu.SEMAPHORE)
DATAFLOW = pltpu.SideEffectType.DATAFLOW_SIDE_EFFECTING


def _in_hbm(a):
    return pltpu.with_memory_space_constraint(a, pltpu.HBM)


def _push_peers():
    x, y, c, chips = _mesh_pos()
    return [(*chip, c) for chip in chips] + [(x, y, 1 - c)]


def cast_place(w, layer, own_block, name):
    _, R, C = w.shape
    tr = _row_tile(R, C)

    def body(idx_ref, w_ref, o_ref):
        o_ref[...] = w_ref[...].astype(BF16)

    return pl.pallas_call(
        body, name=name,
        grid_spec=pltpu.PrefetchScalarGridSpec(
            num_scalar_prefetch=1, grid=(R // tr,),
            in_specs=[pl.BlockSpec((None, tr, C), lambda i, idx: (layer, i, 0))],
            out_specs=pl.BlockSpec((None, tr, C), lambda i, idx: (idx[0], i, 0))),
        out_shape=jax.ShapeDtypeStruct((N_DEV, R, C), BF16),
        compiler_params=_params(("arbitrary",)),
    )(own_block, w)


def gather_start(lands, after, name):
    n = len(lands)

    def body(*refs):
        lz = refs[:n]
        send_sems, recv_sems = refs[n + 1], refs[n + 2]
        token = refs[-1]
        x, y, c, _ = _mesh_pos()
        for t in range(n):
            mine = lz[t].at[_flat(x, y, c)]
            for k, peer in enumerate(_push_peers()):
                pltpu.make_async_remote_copy(
                    src_ref=mine, dst_ref=mine, send_sem=send_sems.at[4 * t + k],
                    recv_sem=recv_sems.at[4 * t + k], device_id=peer, device_id_type=MESH).start()
        token[...] = jnp.zeros_like(token)

    return pl.pallas_call(
        body, name=name,
        out_shape=(pltpu.SemaphoreType.DMA((4 * n,)), pltpu.SemaphoreType.DMA((4 * n,)),
                   *[pltpu.HBM(l.shape, l.dtype) for l in lands], jax.ShapeDtypeStruct((8, LANE), F32)),
        in_specs=[HBM_SPEC] * n + [ANY],
        out_specs=(SEM_SPEC, SEM_SPEC, *[HBM_SPEC] * n, pl.BlockSpec(memory_space=pltpu.VMEM)),
        input_output_aliases={i: 2 + i for i in range(n)},
        compiler_params=pltpu.CompilerParams(has_side_effects=DATAFLOW),
    )(*[_in_hbm(l) for l in lands], after)


def gather_wait(handle, after, name):
    send, recv = handle[0], handle[1]
    bufs = handle[2:-1]
    n = len(bufs)

    def body(*refs):
        lz = refs[:n]
        send_sems, recv_sems = refs[n], refs[n + 1]
        x, y, c, _ = _mesh_pos()
        for t in range(n):
            for k, peer in enumerate(_push_peers()):
                cp = pltpu.make_async_remote_copy(
                    src_ref=lz[t].at[_flat(x, y, c)], dst_ref=lz[t].at[_flat(*peer)], send_sem=send_sems.at[4 * t + k],
                    recv_sem=recv_sems.at[4 * t + k], device_id=peer, device_id_type=MESH)
                cp.wait_send()
                cp.wait_recv()

    outs = pl.pallas_call(
        body, name=name,
        out_shape=tuple(pltpu.HBM(b.shape, b.dtype) for b in bufs),
        in_specs=[HBM_SPEC] * n + [SEM_SPEC, SEM_SPEC, ANY],
        out_specs=tuple([HBM_SPEC] * n),
        input_output_aliases={i: i for i in range(n)},
        compiler_params=pltpu.CompilerParams(has_side_effects=DATAFLOW),
    )(*bufs, send, recv, after)
    return list(outs)


def gather_finish(lands, name):
    n = len(lands)

    def body(*refs):
        outs = refs[n:2 * n]
        send_sems, recv_sems = refs[2 * n:]
        x, y, c, chips = _mesh_pos()
        sibling = (x, y, 1 - c)

        def copy(t, j, core):
            blk = outs[t].at[_flat(*chips[j], core)]
            return pltpu.make_async_remote_copy(
                src_ref=blk, dst_ref=blk, send_sem=send_sems.at[t, j], recv_sem=recv_sems.at[t, j],
                device_id=sibling, device_id_type=MESH)

        sends = [copy(t, j, c) for t in range(n) for j in range(3)]
        for cp in sends:
            cp.start()
        for t in range(n):
            for j in range(3):
                copy(t, j, 1 - c).wait_recv()
        for cp in sends:
            cp.wait_send()

    return pl.pallas_call(
        body, name=name,
        out_shape=[jax.ShapeDtypeStruct(l.shape, l.dtype) for l in lands],
        in_specs=[ANY] * n, out_specs=[ANY] * n,
        input_output_aliases={t: t for t in range(n)},
        scratch_shapes=[pltpu.SemaphoreType.DMA((n, 3)), pltpu.SemaphoreType.DMA((n, 3))],
    )(*lands)


def chip_exchange_start(sums, name):
    n = len(sums)
    lands = [lax.empty(s.shape, s.dtype) for s in sums]

    def body(*refs):
        ins, lz = refs[:n], refs[n:2 * n]
        send_sems, recv_sems = refs[2 * n], refs[2 * n + 1]
        token = refs[-1]
        peers = _push_peers()
        for t in range(n):
            for j in range(3):
                pltpu.make_async_remote_copy(
                    src_ref=ins[t].at[j], dst_ref=lz[t].at[j], send_sem=send_sems.at[3 * t + j],
                    recv_sem=recv_sems.at[3 * t + j], device_id=peers[j], device_id_type=MESH).start()
        token[...] = jnp.zeros_like(token)

    return pl.pallas_call(
        body, name=name,
        out_shape=(pltpu.SemaphoreType.DMA((3 * n,)), pltpu.SemaphoreType.DMA((3 * n,)),
                   *[pltpu.HBM(s.shape, s.dtype) for s in sums], *[pltpu.HBM(s.shape, s.dtype) for s in sums],
                   jax.ShapeDtypeStruct((8, LANE), F32)),
        in_specs=[HBM_SPEC] * (2 * n),
        out_specs=(SEM_SPEC, SEM_SPEC, *[HBM_SPEC] * (2 * n), pl.BlockSpec(memory_space=pltpu.VMEM)),
        input_output_aliases={i: 2 + i for i in range(2 * n)},
        compiler_params=pltpu.CompilerParams(has_side_effects=DATAFLOW),
    )(*[_in_hbm(s) for s in sums], *[_in_hbm(l) for l in lands])


def chip_exchange_wait(handle, after, name):
    send, recv = handle[0], handle[1]
    n = (len(handle) - 3) // 2
    bufs = handle[2:2 + 2 * n]

    def body(*refs):
        ins, lz = refs[:n], refs[n:2 * n]
        send_sems, recv_sems = refs[2 * n], refs[2 * n + 1]
        peers = _push_peers()
        for t in range(n):
            for j in range(3):
                cp = pltpu.make_async_remote_copy(
                    src_ref=ins[t].at[j], dst_ref=lz[t].at[j], send_sem=send_sems.at[3 * t + j],
                    recv_sem=recv_sems.at[3 * t + j], device_id=peers[j], device_id_type=MESH)
                cp.wait_send()
                cp.wait_recv()

    outs = pl.pallas_call(
        body, name=name,
        out_shape=tuple(pltpu.HBM(b.shape, b.dtype) for b in bufs),
        in_specs=[HBM_SPEC] * (2 * n) + [SEM_SPEC, SEM_SPEC, ANY],
        out_specs=tuple([HBM_SPEC] * (2 * n)),
        input_output_aliases={i: i for i in range(2 * n)},
        compiler_params=pltpu.CompilerParams(has_side_effects=DATAFLOW),
    )(*bufs, send, recv, after)
    return list(outs[n:])


def rs_sibling_exchange(grads, after, name):
    n = len(grads)

    def body(*refs):
        ins, stages, directs = refs[:n], refs[n + 1:2 * n + 1], refs[2 * n + 1:3 * n + 1]
        send_sems, recv_sems = refs[3 * n + 1:]
        x, y, c, chips = _mesh_pos()
        sibling = (x, y, 1 - c)

        def copy(t, k):
            if k < 3:
                src, dst = ins[t].at[_flat(*chips[k], 1 - c)], stages[t].at[k]
            else:
                src, dst = ins[t].at[_flat(x, y, 1 - c)], directs[t]
            return pltpu.make_async_remote_copy(
                src_ref=src, dst_ref=dst, send_sem=send_sems.at[t, k], recv_sem=recv_sems.at[t, k],
                device_id=sibling, device_id_type=MESH)

        cps = [copy(t, k) for t in range(n) for k in range(4)]
        for cp in cps:
            cp.start()
        for cp in cps:
            cp.wait_recv()
        for cp in cps:
            cp.wait_send()

    return pl.pallas_call(
        body, name=name,
        out_shape=[jax.ShapeDtypeStruct((3,) + g.shape[1:], g.dtype) for g in grads]
        + [jax.ShapeDtypeStruct(g.shape[1:], g.dtype) for g in grads],
        in_specs=[ANY] * (n + 1), out_specs=[ANY] * (2 * n),
        scratch_shapes=[pltpu.SemaphoreType.DMA((n, 4)), pltpu.SemaphoreType.DMA((n, 4))],
    )(*grads, after)


def _row_tile(rows, cols, limit_bytes=3 << 19):
    best = None
    for t in range(16, rows + 1, 16):
        if rows % t == 0 and t * cols * 4 <= limit_bytes:
            best = t
    return best if best is not None else rows


def chip_sum(grad, stage, peer_blocks, name):
    _, R, C = grad.shape
    tr = _row_tile(R, C)

    def body(idx_ref, g_ref, s_ref, o_ref):
        o_ref[...] = (g_ref[...].astype(F32) + s_ref[...].astype(F32)).astype(BF16)

    return pl.pallas_call(
        body, name=name,
        grid_spec=pltpu.PrefetchScalarGridSpec(
            num_scalar_prefetch=1, grid=(3, R // tr),
            in_specs=[pl.BlockSpec((None, tr, C), lambda j, i, idx: (idx[j], i, 0)),
                      pl.BlockSpec((None, tr, C), lambda j, i, idx: (j, i, 0))],
            out_specs=pl.BlockSpec((None, tr, C), lambda j, i, idx: (j, i, 0))),
        out_shape=jax.ShapeDtypeStruct((3, R, C), BF16),
        compiler_params=_params(("arbitrary", "arbitrary")),
    )(peer_blocks, grad, stage)


def _adamw(w, g, m, v):
    m2 = ADAM_B1 * m + (1.0 - ADAM_B1) * g
    v2 = ADAM_B2 * v + (1.0 - ADAM_B2) * (g * g)
    m_hat = m2 / (1.0 - ADAM_B1 ** ADAM_STEP)
    v_hat = v2 / (1.0 - ADAM_B2 ** ADAM_STEP)
    delta = -ADAM_LR * (m_hat / (jnp.sqrt(v_hat) + ADAM_EPS) + ADAM_WD * w)
    return delta, m2, v2


def adam_reduced(own_block, grad, direct, landed, w, m, v, layer, prev, name):
    L, R, C = w.shape
    tr = _row_tile(R, C, 1 << 20)
    first = prev is None

    def body(idx_ref, g_ref, d_ref, l_ref, w_ref, m_ref, v_ref, *rest):
        og, od, om, ov = rest[-4:]
        g = g_ref[...].astype(F32) + d_ref[...].astype(F32)
        for j in range(3):
            g = g + l_ref[j].astype(F32)
        delta, m2, v2 = _adamw(w_ref[...], g, m_ref[...], v_ref[...])
        og[...] = g
        od[...] = delta
        om[...] = m2
        ov[...] = v2

    lay = pl.BlockSpec((None, tr, C), lambda i, idx: (layer, i, 0))
    in_specs = [pl.BlockSpec((None, tr, C), lambda i, idx: (idx[0], i, 0)),
                pl.BlockSpec((tr, C), lambda i, idx: (i, 0)),
                pl.BlockSpec((3, tr, C), lambda i, idx: (0, i, 0)),
                lay, lay, lay]
    args = [own_block, grad, direct, landed, w, m, v]
    aliases = {}
    if not first:
        in_specs += [ANY] * 4
        args += list(prev)
        aliases = {7 + k: k for k in range(4)}
    return pl.pallas_call(
        body, name=name,
        grid_spec=pltpu.PrefetchScalarGridSpec(
            num_scalar_prefetch=1, grid=(R // tr,), in_specs=in_specs, out_specs=[lay] * 4),
        out_shape=[jax.ShapeDtypeStruct((L, R, C), F32)] * 4,
        input_output_aliases=aliases,
        compiler_params=_params(("arbitrary",)),
    )(*args)


def sum_partials(parts, name):
    _, _, N = parts.shape
    tn = 8192

    def body(p_ref, o_ref):
        g = p_ref[0]
        for k in range(1, N_DEV):
            g = g + p_ref[k]
        o_ref[...] = g

    return pl.pallas_call(
        body, name=name, grid=(N // tn,),
        in_specs=[pl.BlockSpec((N_DEV, 1, tn), lambda i: (0, 0, i))],
        out_specs=pl.BlockSpec((1, tn), lambda i: (0, i)),
        out_shape=jax.ShapeDtypeStruct((1, N), F32),
    )(parts)


def adam_plain(g, w, m, v, name):
    _, N = g.shape

    def body(g_ref, w_ref, m_ref, v_ref, od, om, ov):
        delta, m2, v2 = _adamw(w_ref[...], g_ref[...], m_ref[...], v_ref[...])
        od[...] = delta
        om[...] = m2
        ov[...] = v2

    return pl.pallas_call(
        body, name=name, out_shape=[jax.ShapeDtypeStruct((1, N), F32)] * 3,
    )(g, w, m, v)


def adaln_fwd(cond16, w_mod, b_blk, name):
    L, D, W = w_mod.shape
    tn = 768 if W % 768 == 0 else W

    def body(c_ref, w_ref, b_ref, o_ref):
        c = c_ref[...]
        sc = (c * jax.nn.sigmoid(c)).astype(BF16)
        o_ref[...] = _nn(sc, w_ref[...].astype(BF16)) + b_ref[...]

    return pl.pallas_call(
        body, name=name, grid=(L, W // tn),
        in_specs=[pl.BlockSpec((16, D), lambda l, i: (0, 0)),
                  pl.BlockSpec((None, D, tn), lambda l, i: (l, 0, i)),
                  pl.BlockSpec((None, 1, tn), lambda l, i: (l, 0, i))],
        out_specs=pl.BlockSpec((None, 16, tn), lambda l, i: (l, 0, i)),
        out_shape=jax.ShapeDtypeStruct((L, 16, W), F32),
        compiler_params=_params(("arbitrary", "arbitrary")),
    )(cond16, w_mod, b_blk)


def adaln_bwd(cond16, dm_x, dm_c, w_mod, m_mod, v_mod, name):
    L, D, W = w_mod.shape
    tn = 256 if W % 256 == 0 else W
    nt = W // tn

    def body(c_ref, cT_ref, dx_ref, dc_ref, w_ref, m_ref, v_ref, og, od, om, ov, pc_ref):
        l, i = pl.program_id(0), pl.program_id(1)
        c = c_ref[...]
        sig, sl = _silu_parts(c)
        cT = cT_ref[...]
        sigT = jax.nn.sigmoid(cT)
        scT = (cT * sigT).astype(BF16)
        dmc = jnp.sum(dc_ref[...], axis=0, keepdims=True)
        dm16 = jnp.concatenate([dx_ref[...], jnp.broadcast_to(dmc, (8, tn))], axis=0)
        row = lax.broadcasted_iota(jnp.int32, (16, tn), 0)
        dm16 = jnp.where(row <= 8, dm16, 0.0).astype(BF16)
        w = w_ref[...]
        g = _nn(scT, dm16)
        delta, m2, v2 = _adamw(w, g, m_ref[...], v_ref[...])
        og[...] = g
        od[...] = delta
        om[...] = m2
        ov[...] = v2

        @pl.when((l == 0) & (i == 0))
        def _():
            pc_ref[...] = jnp.zeros_like(pc_ref)

        @pl.when(l == 0)
        def _():
            back = _nt(jnp.broadcast_to(dmc, (8, tn)).astype(BF16), w.astype(BF16))
            pc_ref[...] += back * _dsilu(c[8:9, :], sig[8:9, :])

    col = pl.BlockSpec((None, D, tn), lambda l, i: (l, 0, i))
    row8 = pl.BlockSpec((None, 8, tn), lambda l, i: (l, 0, i))
    return pl.pallas_call(
        body, name=name, grid=(L, nt),
        in_specs=[pl.BlockSpec((16, D), lambda l, i: (0, 0)), pl.BlockSpec((D, 16), lambda l, i: (0, 0)),
                  row8, row8, col, col, col],
        out_specs=[col, col, col, col, pl.BlockSpec((8, D), lambda l, i: (0, 0))],
        out_shape=[jax.ShapeDtypeStruct((L, D, W), F32)] * 4 + [jax.ShapeDtypeStruct((8, D), F32)],
        compiler_params=_params(("arbitrary", "arbitrary")),
    )(cond16, cond16.T, dm_x, dm_c, w_mod, m_mod, v_mod)


def _token_spec(tm, D):
    if tm > 256:
        return pl.BlockSpec((tm, D), lambda i, j: (i, 0), pipeline_mode=pl.Buffered(1))
    return pl.BlockSpec((tm, D), lambda i, j: (i, 0))


def ffn_fwd(x, mod, gn, wg, wu, wd, *, tm, t_first, name):
    T, D = x.shape
    nb, cw, _ = wg.shape
    G = mod.shape[0]
    nt = T // tm
    mm_rows = MM_ROWS if tm % MM_ROWS == 0 else tm // 2

    def body(x_ref, mod_ref, g_ref, wg_ref, wu_ref, wd_ref, xo_ref, h_ref, a_ref, b_ref, y_ref, hs, acc, a_s, b_s, s_s):
        i, j = pl.program_id(0), pl.program_id(1)

        @pl.when(j == 0)
        def _():
            def head(r, carry):
                rows = pl.ds(pl.multiple_of(r * ROW_CHUNK, ROW_CHUNK), ROW_CHUNK)
                row0 = i * tm + r * ROW_CHUNK
                hb = _norm_mod(x_ref[rows, :], g_ref[...], _mod_row(mod_ref, 0, row0, ROW_CHUNK, t_first),
                               _mod_row(mod_ref, 1, row0, ROW_CHUNK, t_first)).astype(BF16)
                hs[rows, :] = hb
                h_ref[rows, :] = hb
                return carry

            lax.fori_loop(0, tm // ROW_CHUNK, head, 0)
            acc[...] = jnp.zeros_like(acc)

        parts = [slice(r, r + mm_rows) for r in range(0, tm, mm_rows)]
        for rows in parts:
            a_s[rows, :] = _nt(hs[rows, :], wg_ref[...])
            b_s[rows, :] = _nt(hs[rows, :], wu_ref[...])
        for r in range(0, tm, ROW_CHUNK):
            rows = slice(r, r + ROW_CHUNK)
            av, bv = a_s[rows, :], b_s[rows, :]
            a_ref[rows, :] = av.astype(BF16)
            b_ref[rows, :] = bv.astype(BF16)
            _, sl = _silu_parts(av)
            s_s[rows, :] = (sl * bv).astype(BF16)
        for rows in parts:
            acc[rows, :] += _nn(s_s[rows, :], wd_ref[...])

        @pl.when(j == nb - 1)
        def _():
            def tail(r, carry):
                rows = pl.ds(pl.multiple_of(r * ROW_CHUNK, ROW_CHUNK), ROW_CHUNK)
                y = acc[rows, :]
                y_ref[rows, :] = y.astype(BF16)
                gate = _mod_row(mod_ref, 2, i * tm + r * ROW_CHUNK, ROW_CHUNK, t_first)
                xo_ref[rows, :] = x_ref[rows, :] + (MACARON * gate) * y
                return carry

            lax.fori_loop(0, tm // ROW_CHUNK, tail, 0)

    tok = _token_spec(tm, D)
    act = pl.BlockSpec((None, tm, cw), lambda i, j: (j, i, 0))
    wblk = pl.BlockSpec((None, cw, D), lambda i, j: (j, 0, 0))
    return pl.pallas_call(
        body, name=name, grid=(nt, nb),
        in_specs=[tok, pl.BlockSpec((G, 3, D), lambda i, j: (0, 0, 0)),
                  pl.BlockSpec((1, D), lambda i, j: (0, 0)), wblk, wblk, wblk],
        out_specs=[tok, tok, act, act, tok],
        out_shape=[jax.ShapeDtypeStruct((T, D), F32), jax.ShapeDtypeStruct((T, D), BF16),
                   jax.ShapeDtypeStruct((nb, T, cw), BF16), jax.ShapeDtypeStruct((nb, T, cw), BF16),
                   jax.ShapeDtypeStruct((T, D), BF16)],
        scratch_shapes=[pltpu.VMEM((tm, D), BF16), pltpu.VMEM((tm, D), F32), pltpu.VMEM((tm, cw), F32),
                        pltpu.VMEM((tm, cw), F32), pltpu.VMEM((tm, cw), BF16)],
        compiler_params=_params(("arbitrary", "arbitrary")),
    )(x, mod, gn, wg, wu, wd)


def ffn_bwd_tok(x, dxo, y, a, b, mod, gn, wg, wu, wd, *, tm, t_first, name):
    T, D = x.shape
    nb, cw, _ = wg.shape
    G = mod.shape[0]
    nt = T // tm
    first = t_first // tm

    def body(x_ref, dxo_ref, y_ref, a_ref, b_ref, mod_ref, g_ref, wg_ref, wu_ref, wd_ref,
             dx_ref, da_ref, db_ref, s_ref, dy_ref, red_ref, dys, dh, ds_s):
        i, j = pl.program_id(0), pl.program_id(1)

        @pl.when(j == 0)
        def _():
            def head(r, carry):
                rows = pl.ds(pl.multiple_of(r * ROW_CHUNK, ROW_CHUNK), ROW_CHUNK)
                dyb = ((MACARON * mod_ref[2:3, :]) * dxo_ref[rows, :]).astype(BF16)
                dys[rows, :] = dyb
                dy_ref[rows, :] = dyb
                return carry

            lax.fori_loop(0, tm // ROW_CHUNK, head, 0)
            dh[...] = jnp.zeros_like(dh)

        parts = [slice(r, r + MM_ROWS) for r in range(0, tm, MM_ROWS)]
        for rows in parts:
            ds_s[rows, :] = _nt(dys[rows, :], wd_ref[...])
        for r in range(0, tm, ROW_CHUNK):
            rows = slice(r, r + ROW_CHUNK)
            av = a_ref[rows, :].astype(F32)
            bv = b_ref[rows, :].astype(F32)
            ds = ds_s[rows, :]
            sig, sl = _silu_parts(av)
            s_ref[rows, :] = (sl * bv).astype(BF16)
            da_ref[rows, :] = (ds * bv * _dsilu(av, sig)).astype(BF16)
            db_ref[rows, :] = (ds * sl).astype(BF16)
        for rows in parts:
            dh[rows, :] += _nn(da_ref[rows, :], wg_ref[...]) + _nn(db_ref[rows, :], wu_ref[...])

        @pl.when((j == 0) & ((i == 0) | (i == first)))
        def _():
            red_ref[...] = jnp.zeros_like(red_ref)

        @pl.when(j == nb - 1)
        def _():
            def tail(r, carry):
                rows = pl.ds(pl.multiple_of(r * ROW_CHUNK, ROW_CHUNK), ROW_CHUNK)
                dxo_v = dxo_ref[rows, :]
                dxn, dshift, dscale, dg = _norm_mod_bwd(x_ref[rows, :], dh[rows, :], g_ref[...], mod_ref[1:2, :])
                dx_ref[rows, :] = dxo_v + dxn
                red_ref[0:1, :] += dshift
                red_ref[1:2, :] += dscale
                red_ref[2:3, :] += jnp.sum((MACARON * dxo_v) * y_ref[rows, :].astype(F32), axis=0, keepdims=True)
                red_ref[3:4, :] += dg
                return carry

            lax.fori_loop(0, tm // ROW_CHUNK, tail, 0)

    tok = _token_spec(tm, D)
    act = pl.BlockSpec((None, tm, cw), lambda i, j: (j, i, 0))
    wblk = pl.BlockSpec((None, cw, D), lambda i, j: (j, 0, 0))
    return pl.pallas_call(
        body, name=name, grid=(nt, nb),
        in_specs=[tok, tok, tok, act, act,
                  pl.BlockSpec((None, 3, D), lambda i, j: (_group_index(i, first, G), 0, 0)),
                  pl.BlockSpec((1, D), lambda i, j: (0, 0)), wblk, wblk, wblk],
        out_specs=[tok, act, act, act, tok,
                   pl.BlockSpec((None, 8, D), lambda i, j: (_group_index(i, first, G), 0, 0))],
        out_shape=[jax.ShapeDtypeStruct((T, D), F32)] + [jax.ShapeDtypeStruct((nb, T, cw), BF16)] * 3
        + [jax.ShapeDtypeStruct((T, D), BF16), jax.ShapeDtypeStruct((G, 8, D), F32)],
        scratch_shapes=[pltpu.VMEM((tm, D), BF16), pltpu.VMEM((tm, D), F32), pltpu.VMEM((tm, cw), F32)],
        compiler_params=_params(("arbitrary", "arbitrary")),
    )(x, dxo, y, a, b, mod, gn, wg, wu, wd)


def matmul_tn(a, b, *, nb, m, n, a_blocked, b_blocked, tk, name, after=None):
    T = a.shape[-2]
    extra = [] if after is None else [after]

    def spec(arr, blocked, width):
        if blocked:
            return pl.BlockSpec((None, tk, width), lambda j, k: (j, k, 0))
        if arr.shape[-1] == width:
            return pl.BlockSpec((tk, width), lambda j, k: (k, 0))
        return pl.BlockSpec((tk, width), lambda j, k: (k, j))

    nk = T // tk

    def body(a_ref, b_ref, *rest):
        if nk == 1:
            rest[-1][...] = _tn(a_ref[...], b_ref[...]).astype(BF16)
            return
        o_ref, acc = rest[-2:]
        k = pl.program_id(1)

        @pl.when(k == 0)
        def _():
            acc[...] = jnp.zeros_like(acc)

        acc[...] += _tn(a_ref[...], b_ref[...])

        @pl.when(k == nk - 1)
        def _():
            o_ref[...] = acc[...].astype(BF16)

    return pl.pallas_call(
        body, name=name, grid=(nb, nk),
        in_specs=[spec(a, a_blocked, m), spec(b, b_blocked, n)] + [ANY] * len(extra),
        out_specs=pl.BlockSpec((None, m, n), lambda j, k: (j, 0, 0)),
        out_shape=jax.ShapeDtypeStruct((nb, m, n), BF16),
        scratch_shapes=[] if nk == 1 else [pltpu.VMEM((m, n), F32)],
        compiler_params=_params(("arbitrary", "arbitrary")),
    )(a, b, *extra)


def proj_in(x, mod, gn, w_in, *, tm, t_first, name):
    T, D = x.shape
    nb, _, cw = w_in.shape
    G = mod.shape[0]
    nt = T // tm

    def body(x_ref, mod_ref, g_ref, w_ref, p_ref, h_ref, hs):
        i = pl.program_id(0)

        @pl.when(pl.program_id(1) == 0)
        def _():
            def head(r, carry):
                rows = pl.ds(pl.multiple_of(r * ROW_CHUNK, ROW_CHUNK), ROW_CHUNK)
                row0 = i * tm + r * ROW_CHUNK
                hb = _norm_mod(x_ref[rows, :], g_ref[...], _mod_row(mod_ref, 0, row0, ROW_CHUNK, t_first),
                               _mod_row(mod_ref, 1, row0, ROW_CHUNK, t_first)).astype(BF16)
                hs[rows, :] = hb
                h_ref[rows, :] = hb
                return carry

            lax.fori_loop(0, tm // ROW_CHUNK, head, 0)

        for r in range(0, tm, MM_ROWS):
            p_ref[r:r + MM_ROWS, :] = _nn(hs[r:r + MM_ROWS, :], w_ref[...])

    tok = _token_spec(tm, D)
    return pl.pallas_call(
        body, name=name, grid=(nt, nb),
        in_specs=[tok, pl.BlockSpec((G, 3, D), lambda i, j: (0, 0, 0)),
                  pl.BlockSpec((1, D), lambda i, j: (0, 0)),
                  pl.BlockSpec((None, D, cw), lambda i, j: (j, 0, 0))],
        out_specs=[pl.BlockSpec((tm, cw), lambda i, j: (i, j)), tok],
        out_shape=[jax.ShapeDtypeStruct((T, nb * cw), F32), jax.ShapeDtypeStruct((T, D), BF16)],
        scratch_shapes=[pltpu.VMEM((tm, D), BF16)],
        compiler_params=_params(("arbitrary", "arbitrary")),
    )(x, mod, gn, w_in)


def proj_bwd_tok(x, dxo, dp, mod, gn, w_in, *, tm, name):
    T, D = x.shape
    nb, _, cw = w_in.shape
    G = mod.shape[0]
    nt = T // tm
    first = dxo.shape[0] // tm

    def body(x_ref, dxo_ref, dp_ref, mod_ref, g_ref, w_ref, dx_ref, red_ref, dh):
        i, j = pl.program_id(0), pl.program_id(1)

        @pl.when(j == 0)
        def _():
            dh[...] = jnp.zeros_like(dh)

        dh[...] += _nt(dp_ref[...], w_ref[...])

        @pl.when((j == 0) & ((i == 0) | (i == first)))
        def _():
            red_ref[...] = jnp.zeros_like(red_ref)

        @pl.when(j == nb - 1)
        def _():
            dxn, dshift, dscale, dg = _norm_mod_bwd(x_ref[...], dh[...], g_ref[...], mod_ref[1:2, :])
            dx_ref[...] = jnp.where(i < first, dxo_ref[...], 0.0) + dxn
            red_ref[0:1, :] += dshift
            red_ref[1:2, :] += dscale
            red_ref[3:4, :] += dg

    tok = pl.BlockSpec((tm, D), lambda i, j: (i, 0))
    return pl.pallas_call(
        body, name=name, grid=(nt, nb),
        in_specs=[tok, pl.BlockSpec((tm, D), lambda i, j: (jnp.minimum(i, first - 1), 0)),
                  pl.BlockSpec((tm, cw), lambda i, j: (i, j)),
                  pl.BlockSpec((None, 3, D), lambda i, j: (_group_index(i, first, G), 0, 0)),
                  pl.BlockSpec((1, D), lambda i, j: (0, 0)),
                  pl.BlockSpec((None, D, cw), lambda i, j: (j, 0, 0))],
        out_specs=[tok, pl.BlockSpec((None, 8, D), lambda i, j: (_group_index(i, first, G), 0, 0))],
        out_shape=[jax.ShapeDtypeStruct((T, D), F32), jax.ShapeDtypeStruct((G, 8, D), F32)],
        scratch_shapes=[pltpu.VMEM((tm, D), F32)],
        compiler_params=_params(("arbitrary", "arbitrary")),
    )(x, dxo, dp, mod, gn, w_in)


def _rope(t, cos, sin):
    return t * cos + pltpu.roll(t, HEAD_DIM // 2, axis=1) * sin


def ret_prep(p, cos, sin, *, heads, tm, name):
    T = p.shape[0]

    def body(q_ref, k_ref, v_ref, c_ref, s_ref, qo, ko, vo):
        cos_v, sin_v = c_ref[...], s_ref[...]
        qo[...] = _rope(q_ref[...], cos_v, sin_v).astype(BF16)
        ko[...] = _rope(k_ref[...] * K_SCALE, cos_v, sin_v).astype(BF16)
        vo[...] = v_ref[...].astype(BF16)

    def col(part):
        return pl.BlockSpec((tm, HEAD_DIM), lambda h, i: (i, part * heads + h))

    tab = pl.BlockSpec((tm, HEAD_DIM), lambda h, i: (i, 0))
    out = pl.BlockSpec((None, tm, HEAD_DIM), lambda h, i: (h, i, 0))
    return pl.pallas_call(
        body, name=name, grid=(heads, T // tm),
        in_specs=[col(0), col(1), col(2), tab, tab], out_specs=[out, out, out],
        out_shape=[jax.ShapeDtypeStruct((heads, T, HEAD_DIM), BF16)] * 3,
        compiler_params=_params(("arbitrary", "arbitrary")),
    )(p, p, p, cos, sin)


def _decay(n, m, lgf, lgb, t_lat, t_ctx):
    df = jnp.where(m < t_lat, n - m, n - m + (t_lat + t_ctx))
    db = m - n
    ef = jnp.where(df >= 0, jnp.exp(lgf * df), 0.0)
    eb = jnp.where(db >= 0, jnp.exp(lgb * db), 0.0)
    return ef, eb, df, db


def ret_fwd(q, k, v, lg, *, t_lat, tq, tk, name):
    H, T, _ = k.shape
    t_ctx = T - t_lat

    def body(lg_ref, q_ref, k_ref, v_ref, o_ref):
        h, qi, kj = pl.program_id(0), pl.program_id(1), pl.program_id(2)

        @pl.when(kj == 0)
        def _():
            o_ref[...] = jnp.zeros_like(o_ref)

        s = _nt(q_ref[...], k_ref[...])
        n = (qi * tq + lax.broadcasted_iota(jnp.int32, (tq, tk), 0)).astype(F32)
        m = (kj * tk + lax.broadcasted_iota(jnp.int32, (tq, tk), 1)).astype(F32)
        ef, eb, _, _ = _decay(n, m, lg_ref[0, h], lg_ref[1, h], t_lat, t_ctx)
        o_ref[...] += _nn((s * (ef + eb)).astype(BF16), v_ref[...])

    return pl.pallas_call(
        body, name=name, grid=(H, t_lat // tq, T // tk),
        in_specs=[pl.BlockSpec(memory_space=pltpu.SMEM),
                  pl.BlockSpec((None, tq, HEAD_DIM), lambda h, i, j: (h, i, 0)),
                  pl.BlockSpec((None, tk, HEAD_DIM), lambda h, i, j: (h, j, 0)),
                  pl.BlockSpec((None, tk, HEAD_DIM), lambda h, i, j: (h, j, 0))],
        out_specs=pl.BlockSpec((None, tq, HEAD_DIM), lambda h, i, j: (h, i, 0)),
        out_shape=jax.ShapeDtypeStruct((H, t_lat, HEAD_DIM), F32),
        compiler_params=_params(("arbitrary", "arbitrary", "arbitrary")),
    )(lg, q, k, v)


def ret_bwd(q, k, v, do, lg, *, t_lat, tq, tk, name):
    H, T, _ = k.shape
    t_ctx = T - t_lat

    def body(lg_ref, q_ref, k_ref, v_ref, do_ref, dq_ref, dk_ref, dv_ref, dlg_ref):
        h, kj, qi = pl.program_id(0), pl.program_id(1), pl.program_id(2)

        @pl.when((kj == 0) & (qi == 0))
        def _():
            dq_ref[...] = jnp.zeros_like(dq_ref)
            dlg_ref[...] = jnp.zeros_like(dlg_ref)

        @pl.when(qi == 0)
        def _():
            dk_ref[...] = jnp.zeros_like(dk_ref)
            dv_ref[...] = jnp.zeros_like(dv_ref)

        qv, kv, vv = q_ref[...], k_ref[...], v_ref[...]
        dob = do_ref[...].astype(BF16)
        st = _nt(kv, qv)
        dwt = _nt(vv, dob)
        m = (kj * tk + lax.broadcasted_iota(jnp.int32, (tk, tq), 0)).astype(F32)
        n = (qi * tq + lax.broadcasted_iota(jnp.int32, (tk, tq), 1)).astype(F32)
        ef, eb, df, db = _decay(n, m, lg_ref[0, h], lg_ref[1, h], t_lat, t_ctx)
        dec = ef + eb
        dv_ref[...] += _nn((st * dec).astype(BF16), dob)
        dst = (dwt * dec).astype(BF16)
        dk_ref[...] += _nn(dst, qv)
        rows = pl.ds(pl.multiple_of(qi * tq, tq), tq)
        dq_ref[rows, :] += _tn(dst, kv)
        gs = dwt * st
        dlg_ref[0:1, :] += jnp.sum(gs * (ef * df))
        dlg_ref[1:2, :] += jnp.sum(gs * (eb * db))

    kspec = pl.BlockSpec((None, tk, HEAD_DIM), lambda h, j, i: (h, j, 0))
    qspec = pl.BlockSpec((None, tq, HEAD_DIM), lambda h, j, i: (h, i, 0))
    return pl.pallas_call(
        body, name=name, grid=(H, T // tk, t_lat // tq),
        in_specs=[pl.BlockSpec(memory_space=pltpu.SMEM), qspec, kspec, kspec, qspec],
        out_specs=[pl.BlockSpec((None, t_lat, HEAD_DIM), lambda h, j, i: (h, 0, 0)), kspec, kspec,
                   pl.BlockSpec((None, 8, LANE), lambda h, j, i: (h, 0, 0))],
        out_shape=[jax.ShapeDtypeStruct((H, t_lat, HEAD_DIM), F32), jax.ShapeDtypeStruct((H, T, HEAD_DIM), F32),
                   jax.ShapeDtypeStruct((H, T, HEAD_DIM), F32), jax.ShapeDtypeStruct((H, 8, LANE), F32)],
        compiler_params=_params(("arbitrary", "arbitrary", "arbitrary")),
    )(lg, q, k, v, do)


def _group_norm(o):
    mu = jnp.mean(o, axis=-1, keepdims=True)
    ctr = o - mu
    r = lax.rsqrt(jnp.mean(ctr * ctr, axis=-1, keepdims=True) + GN_EPS)
    return ctr * r, r


def ret_post_fwd(o, p, *, heads, d_model, name):
    H, t_lat, _ = o.shape
    T = p.shape[0]

    def body(o_ref, g_ref, z_ref):
        on, _ = _group_norm(o_ref[...])
        _, sl = _silu_parts(g_ref[0:t_lat, :])
        z_ref[...] = (on * sl).astype(BF16)

    return pl.pallas_call(
        body, name=name, grid=(H,),
        in_specs=[pl.BlockSpec((None, t_lat, HEAD_DIM), lambda h: (h, 0, 0)),
                  pl.BlockSpec((T, HEAD_DIM), lambda h: (0, 3 * heads + h))],
        out_specs=pl.BlockSpec((t_lat, HEAD_DIM), lambda h: (0, h)),
        out_shape=jax.ShapeDtypeStruct((t_lat, d_model), BF16),
        compiler_params=_params(("arbitrary",)),
    )(o, p)


def _shift_rows(u, k):
    rows = u.shape[0]
    row = lax.broadcasted_iota(jnp.int32, u.shape, 0)
    rolled = pltpu.roll(u, k % rows, axis=0)
    return jnp.where((row >= k) & (row < rows + k), rolled, 0.0)


def conv_fwd(z, p, w_conv, *, heads, t_lat, name):
    T = p.shape[0]
    cb_n = w_conv.shape[1] // LANE
    base = 4 * heads

    def body(z_in, bg_ref, cg_ref, u_ref, w_ref, z_ref):
        cu = cg_ref[0:t_lat, :] * u_ref[0:t_lat, :]
        c3 = _shift_rows(cu, 1) * w_ref[0:1, :] + cu * w_ref[1:2, :] + _shift_rows(cu, -1) * w_ref[2:3, :]
        z_ref[...] = (bg_ref[0:t_lat, :] * c3).astype(BF16)

    def col(part):
        return pl.BlockSpec((T, LANE), lambda cb: (0, base + part * cb_n + cb))

    return pl.pallas_call(
        body, name=name, grid=(cb_n,),
        in_specs=[ANY, col(0), col(1), col(2), pl.BlockSpec((CONV_WIDTH, LANE), lambda cb: (0, cb))],
        out_specs=pl.BlockSpec((t_lat, LANE), lambda cb: (0, heads + cb)),
        out_shape=jax.ShapeDtypeStruct(z.shape, z.dtype),
        input_output_aliases={0: 0},
        compiler_params=_params(("arbitrary",)),
    )(z, p, p, p, w_conv)


def out_proj(z, w_out, x, mod, *, tm, name):
    t_lat, D = z.shape

    def body(z_ref, w_ref, x_ref, mod_ref, xo_ref, y_ref):
        y = _nn(z_ref[...], w_ref[...])
        y_ref[...] = y.astype(BF16)
        xo_ref[...] = x_ref[...] + mod_ref[2:3, :] * y

    tok = pl.BlockSpec((tm, D), lambda i: (i, 0))
    return pl.pallas_call(
        body, name=name, grid=(t_lat // tm,),
        in_specs=[tok, pl.BlockSpec((D, D), lambda i: (0, 0)), tok, pl.BlockSpec((None, 3, D), lambda i: (0, 0, 0))],
        out_specs=[tok, tok],
        out_shape=[jax.ShapeDtypeStruct((t_lat, D), F32), jax.ShapeDtypeStruct((t_lat, D), BF16)],
        compiler_params=_params(("arbitrary",)),
    )(z, w_out, x, mod)


def out_proj_bwd(dxo, y, w_out, mod, *, tm, name):
    t_lat, D = dxo.shape

    def body(dxo_ref, y_ref, w_ref, mod_ref, dz_ref, dy_ref, red_ref):
        @pl.when(pl.program_id(0) == 0)
        def _():
            red_ref[...] = jnp.zeros_like(red_ref)

        dxo_v = dxo_ref[...]
        dyb = (mod_ref[2:3, :] * dxo_v).astype(BF16)
        dy_ref[...] = dyb
        dz_ref[...] = _nt(dyb, w_ref[...])
        red_ref[2:3, :] += jnp.sum(dxo_v * y_ref[...].astype(F32), axis=0, keepdims=True)

    tok = pl.BlockSpec((tm, D), lambda i: (i, 0))
    return pl.pallas_call(
        body, name=name, grid=(t_lat // tm,),
        in_specs=[tok, tok, pl.BlockSpec((D, D), lambda i: (0, 0)), pl.BlockSpec((None, 3, D), lambda i: (0, 0, 0))],
        out_specs=[tok, tok, pl.BlockSpec((8, D), lambda i: (0, 0))],
        out_shape=[jax.ShapeDtypeStruct((t_lat, D), F32), jax.ShapeDtypeStruct((t_lat, D), BF16),
                   jax.ShapeDtypeStruct((8, D), F32)],
        compiler_params=_params(("arbitrary",)),
    )(dxo, y, w_out, mod)


def ret_post_bwd(dz, o, p, *, heads, name):
    H, t_lat, _ = o.shape
    T, in_w = p.shape

    def body(dz_ref, o_ref, g_ref, do_ref, dp_ref):
        on, r = _group_norm(o_ref[...])
        gg = g_ref[0:t_lat, :]
        sig, sl = _silu_parts(gg)
        dret = dz_ref[...]
        don = dret * sl
        do_ref[...] = r * (don - jnp.mean(don, axis=-1, keepdims=True)
                           - on * jnp.mean(don * on, axis=-1, keepdims=True))
        dp_ref[0:t_lat, :] = (dret * on * _dsilu(gg, sig)).astype(BF16)
        dp_ref[t_lat:T, :] = jnp.zeros((T - t_lat, HEAD_DIM), BF16)

    return pl.pallas_call(
        body, name=name, grid=(H,),
        in_specs=[pl.BlockSpec((t_lat, HEAD_DIM), lambda h: (0, h)),
                  pl.BlockSpec((None, t_lat, HEAD_DIM), lambda h: (h, 0, 0)),
                  pl.BlockSpec((T, HEAD_DIM), lambda h: (0, 3 * heads + h))],
        out_specs=[pl.BlockSpec((None, t_lat, HEAD_DIM), lambda h: (h, 0, 0)),
                   pl.BlockSpec((T, HEAD_DIM), lambda h: (0, 3 * heads + h))],
        out_shape=[jax.ShapeDtypeStruct((H, t_lat, HEAD_DIM), F32), jax.ShapeDtypeStruct((T, in_w), BF16)],
        compiler_params=_params(("arbitrary",)),
    )(dz, o, p)


def conv_bwd(dp, dz, p, w_conv, *, heads, t_lat, name):
    T = p.shape[0]
    cb_n = w_conv.shape[1] // LANE
    base = 4 * heads

    def body(dp_in, dz_ref, bg_ref, cg_ref, u_ref, w_ref, dp_ref, dw_ref):
        part = pl.program_id(1)
        cg, u = cg_ref[0:t_lat, :], u_ref[0:t_lat, :]
        cu = cg * u
        dconv = dz_ref[...]
        dp_ref[t_lat:T, :] = jnp.zeros((T - t_lat, LANE), BF16)

        @pl.when(part == 0)
        def _():
            c3 = _shift_rows(cu, 1) * w_ref[0:1, :] + cu * w_ref[1:2, :] + _shift_rows(cu, -1) * w_ref[2:3, :]
            dp_ref[0:t_lat, :] = (dconv * c3).astype(BF16)
            dc3 = dconv * bg_ref[0:t_lat, :]
            dw_ref[0:1, :] = jnp.sum(dc3 * _shift_rows(cu, 1), axis=0, keepdims=True)
            dw_ref[1:2, :] = jnp.sum(dc3 * cu, axis=0, keepdims=True)
            dw_ref[2:3, :] = jnp.sum(dc3 * _shift_rows(cu, -1), axis=0, keepdims=True)

        @pl.when(part > 0)
        def _():
            dc3 = dconv * bg_ref[0:t_lat, :]
            dcu = (_shift_rows(dc3, -1) * w_ref[0:1, :] + dc3 * w_ref[1:2, :] + _shift_rows(dc3, 1) * w_ref[2:3, :])
            dp_ref[0:t_lat, :] = (dcu * jnp.where(part == 1, u, cg)).astype(BF16)

    def col(part):
        return pl.BlockSpec((T, LANE), lambda cb, pt: (0, base + part * cb_n + cb))

    return pl.pallas_call(
        body, name=name, grid=(cb_n, 3),
        in_specs=[ANY, pl.BlockSpec((t_lat, LANE), lambda cb, pt: (0, heads + cb)), col(0), col(1), col(2),
                  pl.BlockSpec((CONV_WIDTH, LANE), lambda cb, pt: (0, cb))],
        out_specs=[pl.BlockSpec((T, LANE), lambda cb, pt: (0, base + pt * cb_n + cb)),
                   pl.BlockSpec((CONV_WIDTH, LANE), lambda cb, pt: (0, cb))],
        out_shape=[jax.ShapeDtypeStruct(dp.shape, dp.dtype), jax.ShapeDtypeStruct(w_conv.shape, F32)],
        input_output_aliases={0: 0},
        compiler_params=_params(("arbitrary", "arbitrary")),
    )(dp, dz, p, p, p, w_conv)


def ret_unprep(dp, dq, dk, dv, cos, sin, *, t_lat, name):
    H, T, _ = dk.shape

    def body(dp_in, dq_ref, dk_ref, dv_ref, c_ref, s_ref, dp_ref):
        part = pl.program_id(0)

        @pl.when(part == 0)
        def _():
            dp_ref[0:t_lat, :] = _rope(dq_ref[...], c_ref[0:t_lat, :], -s_ref[0:t_lat, :]).astype(BF16)
            dp_ref[t_lat:T, :] = jnp.zeros((T - t_lat, HEAD_DIM), BF16)

        @pl.when(part == 1)
        def _():
            dp_ref[...] = (_rope(dk_ref[...], c_ref[...], -s_ref[...]) * K_SCALE).astype(BF16)

        @pl.when(part == 2)
        def _():
            dp_ref[...] = dv_ref[...].astype(BF16)

    def head_of(part):
        return lambda pt, h: (jnp.where(pt == part, h, 0), 0, 0)

    tab = pl.BlockSpec((T, HEAD_DIM), lambda pt, h: (0, 0))
    return pl.pallas_call(
        body, name=name, grid=(3, H),
        in_specs=[ANY, pl.BlockSpec((None, t_lat, HEAD_DIM), head_of(0)), pl.BlockSpec((None, T, HEAD_DIM), head_of(1)),
                  pl.BlockSpec((None, T, HEAD_DIM), head_of(2)), tab, tab],
        out_specs=pl.BlockSpec((T, HEAD_DIM), lambda pt, h: (0, pt * H + h)),
        out_shape=jax.ShapeDtypeStruct(dp.shape, dp.dtype),
        input_output_aliases={0: 0},
        compiler_params=_params(("arbitrary", "arbitrary")),
    )(dp, dq, dk, dv, cos, sin)


def norm_mod_fwd(x, mod, gn, *, tm, name):
    T, D = x.shape

    def body(x_ref, mod_ref, g_ref, h_ref):
        h_ref[...] = _norm_mod(x_ref[...], g_ref[...], mod_ref[0:1, :], mod_ref[1:2, :])

    tok = pl.BlockSpec((tm, D), lambda i: (i, 0))
    return pl.pallas_call(
        body, name=name, grid=(T // tm,),
        in_specs=[tok, pl.BlockSpec((None, 3, D), lambda i: (0, 0, 0)), pl.BlockSpec((1, D), lambda i: (0, 0))],
        out_specs=tok, out_shape=jax.ShapeDtypeStruct((T, D), F32),
        compiler_params=_params(("arbitrary",)),
    )(x, mod, gn)


def norm_mod_bwd(x, dh, dxo, mod, gn, *, tm, name):
    T, D = x.shape

    def body(x_ref, dh_ref, dxo_ref, mod_ref, g_ref, dx_ref, red_ref):
        @pl.when(pl.program_id(0) == 0)
        def _():
            red_ref[...] = jnp.zeros_like(red_ref)

        dxn, dshift, dscale, dg = _norm_mod_bwd(x_ref[...], dh_ref[...], g_ref[...], mod_ref[1:2, :])
        dx_ref[...] = dxo_ref[...] + dxn
        red_ref[0:1, :] += dshift
        red_ref[1:2, :] += dscale
        red_ref[3:4, :] += dg

    tok = pl.BlockSpec((tm, D), lambda i: (i, 0))
    return pl.pallas_call(
        body, name=name, grid=(T // tm,),
        in_specs=[tok, tok, tok, pl.BlockSpec((None, 3, D), lambda i: (0, 0, 0)), pl.BlockSpec((1, D), lambda i: (0, 0))],
        out_specs=[tok, pl.BlockSpec((8, D), lambda i: (0, 0))],
        out_shape=[jax.ShapeDtypeStruct((T, D), F32), jax.ShapeDtypeStruct((8, D), F32)],
        compiler_params=_params(("arbitrary",)),
    )(x, dh, dxo, mod, gn)


def _window_sum(u, w, lead):
    T, C = u.shape
    ext = jnp.concatenate([u, jnp.zeros((POOL_PAD, C), F32)], axis=0)
    k = 1
    while k < w:
        ext = ext + _shift_rows(ext, k)
        k *= 2
    return _shift_rows(ext, -lead)[0:T, :]


def _window_count(T, C, w):
    t = lax.broadcasted_iota(jnp.int32, (T, C), 0)
    lo = jnp.clip(t - w // 2, 0, T)
    hi = jnp.clip(t + (w - w // 2), 0, T)
    return (hi - lo).astype(F32)


def pool_fwd(h, x, pool_w, scale, mod, *, name):
    T, D = h.shape
    G, Cg, _ = pool_w.shape
    ns = Cg // LANE

    def body(h_ref, x_ref, w_ref, sc_ref, mod_ref, xo_ref, pl_ref, yl_ref, acc):
        g, s = pl.program_id(0), pl.program_id(1)
        hv = h_ref[...]
        for gi, win in enumerate(POOL_WINDOWS):
            @pl.when(g == gi)
            def _():
                mean = _window_sum(hv, win, win // 2 - 1) / _window_count(T, LANE, win)
                pooled = (mean - hv).astype(BF16)
                pl_ref[...] = pooled
                contrib = _nn(pooled, w_ref[...])

                @pl.when(s == 0)
                def _():
                    acc[...] = contrib

                @pl.when(s > 0)
                def _():
                    acc[...] += contrib

        @pl.when(s == ns - 1)
        def _():
            yl = acc[...]
            yl_ref[...] = yl.astype(BF16)
            xo_ref[...] = x_ref[...] + mod_ref[2:3, :] * (yl * sc_ref[...])

    grp = pl.BlockSpec((T, Cg), lambda g, s: (0, g))
    sub = pl.BlockSpec((T, LANE), lambda g, s: (0, g * ns + s))
    return pl.pallas_call(
        body, name=name, grid=(G, ns),
        in_specs=[sub, grp, pl.BlockSpec((None, LANE, Cg), lambda g, s: (g, s, 0)),
                  pl.BlockSpec((1, Cg), lambda g, s: (0, g)), pl.BlockSpec((None, 3, Cg), lambda g, s: (0, 0, g))],
        out_specs=[grp, sub, grp],
        out_shape=[jax.ShapeDtypeStruct((T, D), F32), jax.ShapeDtypeStruct((T, D), BF16),
                   jax.ShapeDtypeStruct((T, D), BF16)],
        scratch_shapes=[pltpu.VMEM((T, Cg), F32)],
        compiler_params=_params(("arbitrary", "arbitrary")),
    )(h, x, pool_w, scale, mod)


def pool_bwd(dxo, pooled, yl, pool_w, scale, mod, *, name):
    T, D = dxo.shape
    G, Cg, _ = pool_w.shape
    ns = Cg // LANE

    def body(dxo_ref, pl_ref, yl_ref, w_ref, sc_ref, mod_ref, dh_ref, dw_ref, red_ref, dyl):
        g, s = pl.program_id(0), pl.program_id(1)

        @pl.when(s == 0)
        def _():
            dxo_v = dxo_ref[...]
            ylv = yl_ref[...].astype(F32)
            dy = mod_ref[2:3, :] * dxo_v
            dyl[...] = (dy * sc_ref[...]).astype(BF16)
            red_ref[...] = jnp.zeros_like(red_ref)
            red_ref[2:3, :] = jnp.sum(dxo_v * (ylv * sc_ref[...]), axis=0, keepdims=True)
            red_ref[4:5, :] = jnp.sum(dy * ylv, axis=0, keepdims=True)

        dylv = dyl[...]
        dpool = _nt(dylv, w_ref[...])
        dw_ref[...] = _tn(pl_ref[...], dylv).astype(BF16)
        for gi, win in enumerate(POOL_WINDOWS):
            @pl.when(g == gi)
            def _():
                spread = _window_sum(dpool / _window_count(T, LANE, win), win, win // 2)
                dh_ref[...] = spread - dpool

    grp = pl.BlockSpec((T, Cg), lambda g, s: (0, g))
    sub = pl.BlockSpec((T, LANE), lambda g, s: (0, g * ns + s))
    wsub = pl.BlockSpec((None, LANE, Cg), lambda g, s: (g, s, 0))
    return pl.pallas_call(
        body, name=name, grid=(G, ns),
        in_specs=[grp, sub, grp, wsub, pl.BlockSpec((1, Cg), lambda g, s: (0, g)),
                  pl.BlockSpec((None, 3, Cg), lambda g, s: (0, 0, g))],
        out_specs=[sub, wsub, pl.BlockSpec((8, Cg), lambda g, s: (0, g))],
        out_shape=[jax.ShapeDtypeStruct((T, D), F32), jax.ShapeDtypeStruct((G, Cg, Cg), BF16),
                   jax.ShapeDtypeStruct((8, D), F32)],
        scratch_shapes=[pltpu.VMEM((T, Cg), BF16)],
        compiler_params=_params(("arbitrary", "arbitrary")),
    )(dxo, pooled, yl, pool_w, scale, mod)


def final_loss(x, gn, target, *, tm, name):
    T, D = x.shape

    def body(x_ref, g_ref, t_ref, loss_ref, dx_ref, red_ref):
        @pl.when(pl.program_id(0) == 0)
        def _():
            loss_ref[...] = jnp.zeros_like(loss_ref)
            red_ref[...] = jnp.zeros_like(red_ref)

        xx, g = x_ref[...], g_ref[...]
        r = lax.rsqrt(jnp.mean(xx * xx, axis=-1, keepdims=True) + EPS)
        xhat = xx * r
        err = xhat * g - t_ref[...]
        loss_ref[...] += 0.5 * jnp.sum(jnp.mean(err * err, axis=-1, keepdims=True))
        dy = err / D
        red_ref[0:1, :] += jnp.sum(dy * xhat, axis=0, keepdims=True)
        dxh = dy * g
        dx_ref[...] = r * (dxh - xhat * jnp.mean(dxh * xhat, axis=-1, keepdims=True))

    tok = pl.BlockSpec((tm, D), lambda i: (i, 0))
    return pl.pallas_call(
        body, name=name, grid=(T // tm,),
        in_specs=[tok, pl.BlockSpec((1, D), lambda i: (0, 0)), tok],
        out_specs=[pl.BlockSpec((8, LANE), lambda i: (0, 0)), tok, pl.BlockSpec((8, D), lambda i: (0, 0))],
        out_shape=[jax.ShapeDtypeStruct((8, LANE), F32), jax.ShapeDtypeStruct((T, D), F32),
                   jax.ShapeDtypeStruct((8, D), F32)],
        compiler_params=_params(("arbitrary",)),
    )(x, gn, target)


def _rope_tables(t_lat, t_ctx):
    quarter = HEAD_DIM // 4
    pos = jnp.arange(t_lat)
    inv = ROPE_BASE ** (-jnp.arange(quarter, dtype=F32) / quarter)
    ang = jnp.concatenate([(pos // GRID_W).astype(F32)[:, None] * inv, (pos % GRID_W).astype(F32)[:, None] * inv], axis=-1)
    cos, sin = jnp.cos(ang), jnp.sin(ang)
    cos = jnp.concatenate([jnp.concatenate([cos, cos], axis=-1), jnp.ones((t_ctx, HEAD_DIM), F32)], axis=0)
    sin = jnp.concatenate([jnp.concatenate([-sin, sin], axis=-1), jnp.zeros((t_ctx, HEAD_DIM), F32)], axis=0)
    return cos, sin


def _ffn_grads(h, da, db, s, dy, tag):
    nb, T, cw = da.shape
    D = h.shape[1]
    tk = T
    g_gate = matmul_tn(da, h, nb=nb, m=cw, n=D, a_blocked=True, b_blocked=False, tk=tk, name=f"wgrad_gate_{tag}")
    g_up = matmul_tn(db, h, nb=nb, m=cw, n=D, a_blocked=True, b_blocked=False, tk=tk, name=f"wgrad_up_{tag}")
    g_down = matmul_tn(s, dy, nb=nb, m=cw, n=D, a_blocked=True, b_blocked=False, tk=tk, name=f"wgrad_down_{tag}")
    return [g_gate, g_up, g_down]


def kernel(x, c, ctx, c_ctx, w_mod, b_mod, norm_ffn1, norm_mix, norm_ffn2, ffn1_w_gate, ffn1_w_up, ffn1_w_down, ffn2_w_gate, ffn2_w_up, ffn2_w_down, mix_w_in, mix_w_conv, mix_w_out, ret_decay_fwd, ret_decay_bwd, pool_w, pool_scale, final_norm, loss_target, m_c_ctx, m_w_mod, m_b_mod, m_norm_ffn1, m_norm_mix, m_norm_ffn2, m_ffn1_w_gate, m_ffn1_w_up, m_ffn1_w_down, m_ffn2_w_gate, m_ffn2_w_up, m_ffn2_w_down, m_mix_w_in, m_mix_w_conv, m_mix_w_out, m_ret_decay_fwd, m_ret_decay_bwd, m_pool_w, m_pool_scale, m_final_norm, v_c_ctx, v_w_mod, v_b_mod, v_norm_ffn1, v_norm_mix, v_norm_ffn2, v_ffn1_w_gate, v_ffn1_w_up, v_ffn1_w_down, v_ffn2_w_gate, v_ffn2_w_up, v_ffn2_w_down, v_mix_w_in, v_mix_w_conv, v_mix_w_out, v_ret_decay_fwd, v_ret_decay_bwd, v_pool_w, v_pool_scale, v_final_norm):
    t_lat, D = x.shape[1], x.shape[2]
    t_ctx = ctx.shape[1]
    T = t_lat + t_ctx
    heads = ret_decay_fwd.shape[1]
    mod_w = w_mod.shape[2]
    tm = 256
    tmf = 512 if t_lat % 512 == 0 else 256
    tq = 512 if t_lat % 512 == 0 else 256
    tk = T // 3 if (T % 3 == 0 and (T // 3) % 256 == 0) else 256

    ax, ay, ac = lax.axis_index("x"), lax.axis_index("y"), lax.axis_index("c")
    me = 4 * ax + 2 * ay + ac
    own_block = jnp.reshape(me, (1,)).astype(jnp.int32)
    peer_blocks = jnp.stack([4 * (1 - ax) + 2 * ay + ac, 4 * ax + 2 * (1 - ay) + ac,
                             4 * (1 - ax) + 2 * (1 - ay) + ac]).astype(jnp.int32)

    (c_all,) = all_gather([c], name="gather_cond")
    cond16 = jnp.concatenate([c_all.reshape(N_DEV, D), c_ctx[None, :], jnp.zeros((7, D), F32)], axis=0)
    b_blk = lax.dynamic_slice_in_dim(b_mod, me * mod_w, mod_w, axis=1)[:, None, :]
    m_blk = adaln_fwd(cond16, w_mod, b_blk, name="adaln_fwd")
    m_all, w_conv, pscale = all_gather([m_blk, mix_w_conv[0], pool_scale], name="gather_mod")
    mods = jnp.transpose(m_all, (1, 2, 0, 3)).reshape(2, 16, N_MOD, D)
    mod_x = lax.dynamic_index_in_dim(mods, me, axis=1, keepdims=False)
    mod_c = mods[0, 8]
    w_conv = jnp.transpose(w_conv, (1, 0, 2)).reshape(CONV_WIDTH, -1)
    pscale = pscale.reshape(1, D)

    n_grp, grp_rows, grp_w = pool_w.shape[1:]
    ffn1_w = [(jnp.swapaxes(ffn1_w_gate, 1, 2), "gate"), (jnp.swapaxes(ffn1_w_up, 1, 2), "up"), (ffn1_w_down, "down")]
    ffn2_w = [(jnp.swapaxes(ffn2_w_gate, 1, 2), "gate"), (jnp.swapaxes(ffn2_w_up, 1, 2), "up"), (ffn2_w_down, "down")]
    groups = {
        "01": [(w, 0, f"ffn1_{k}") for w, k in ffn1_w],
        "mix": [(mix_w_in, 0, "w_in"), (mix_w_out, 0, "w_out")],
        "02": [(w, 0, f"ffn2_{k}") for w, k in ffn2_w],
        "11": [(w, 1, f"ffn1_{k}") for w, k in ffn1_w] + [(pool_w.reshape(1, n_grp * grp_rows, grp_w), 0, "pool_w")],
        "12": [(w, 1, f"ffn2_{k}") for w, k in ffn2_w],
    }
    started = {}

    def start(tag, after):
        lands = [cast_place(w, l, own_block, name=f"cast_{nm}_{l}") for w, l, nm in groups[tag]]
        started[tag] = gather_start(lands, after, name=f"gather_start_{tag}")
        return started[tag][-1][0:1, 0:1]

    def finish(tag, after):
        return gather_finish(gather_wait(started[tag], after, name=f"gather_wait_{tag}"), name=f"gather_finish_{tag}")

    lg = jnp.concatenate([jax.nn.log_sigmoid(ret_decay_fwd), jax.nn.log_sigmoid(ret_decay_bwd)], axis=0)
    cos, sin = _rope_tables(t_lat, t_ctx)

    def mod3(l, k, with_ctx=False, tie=None):
        rows = mod_x[l, 3 * k:3 * k + 3][None]
        if with_ctx:
            rows = jnp.concatenate([rows, mod_c[3 * k:3 * k + 3][None]], axis=0)
        return rows if tie is None else rows + tie

    tie = start("01", m_all)
    x0 = jnp.concatenate([x[0], ctx[0]], axis=0) + tie
    wg01, wu01, wd01 = finish("01", x0)
    tie = start("mix", wd01)
    x1, h1, a1, b1, y1 = ffn_fwd(x0, mod3(0, 0, True, tie), norm_ffn1[0:1], wg01, wu01, wd01,
                                 tm=384 if T % 384 == 0 else tm, t_first=t_lat, name="ffn_fwd_01")
    w_in, w_out = finish("mix", h1)
    tie = start("02", w_out)
    w_out = w_out.reshape(D, D)
    p, hm = proj_in(x1, mod3(0, 1, True, tie), norm_mix[0:1], w_in, tm=768 if T % 768 == 0 else tm, t_first=t_lat,
                    name="proj_in")
    qr, kr, vr = ret_prep(p, cos, sin, heads=heads, tm=tm, name="ret_prep")
    o = ret_fwd(qr, kr, vr, lg, t_lat=t_lat, tq=tq, tk=tk, name="ret_fwd")
    wg02, wu02, wd02 = finish("02", o)
    tie = start("11", wd02)
    z = ret_post_fwd(o, p, heads=heads, d_model=D, name="ret_post_fwd")
    z = conv_fwd(z, p, w_conv, heads=heads, t_lat=t_lat, name="conv_fwd")
    x2, ym = out_proj(z, w_out, x1, mod3(0, 1, tie=tie), tm=tm, name="out_proj")
    x3, h3, a3, b3, y3 = ffn_fwd(x2, mod3(0, 2, tie=tie), norm_ffn2[0:1], wg02, wu02, wd02, tm=tmf, t_first=t_lat, name="ffn_fwd_02")
    wg11, wu11, wd11, pw = finish("11", h3)
    tie = start("12", wd11)
    pw = jnp.transpose(pw.reshape(N_DEV, n_grp, grp_rows, grp_w), (1, 0, 2, 3)).reshape(n_grp, N_DEV * grp_rows, grp_w)
    x4, h4, a4, b4, y4 = ffn_fwd(x3, mod3(1, 0, tie=tie), norm_ffn1[1:2], wg11, wu11, wd11, tm=tmf, t_first=t_lat, name="ffn_fwd_11")
    hp = norm_mod_fwd(x4, mod3(1, 1), norm_mix[1:2], tm=tm, name="pool_norm_fwd")
    x5, pooled, yl = pool_fwd(hp, x4, pw, pscale, mod3(1, 1), name="pool_fwd")
    wg12, wu12, wd12 = finish("12", yl)
    x6, h6, a6, b6, y6 = ffn_fwd(x5, mod3(1, 2), norm_ffn2[1:2], wg12, wu12, wd12, tm=tmf, t_first=t_lat, name="ffn_fwd_12")
    loss_part, dx6, red_fn = final_loss(x6, final_norm[None, :], loss_target[0], tm=tm, name="final_loss")
    loss = lax.psum(loss_part[0, 0], ("x", "y", "c"))

    reducing = {}

    def reduce_start(tag, grads, after=None):
        res = rs_sibling_exchange(grads, grads[0] if after is None else after, name=f"rs_sibling_{tag}")
        n = len(grads)
        stages, directs = res[:n], res[n:]
        sums = [chip_sum(g, st, peer_blocks, name=f"chip_sum_{tag}_{t}") for t, (g, st) in enumerate(zip(grads, stages))]
        handle = chip_exchange_start(sums, name=f"rs_start_{tag}")
        reducing[tag] = (grads, directs, handle)
        return handle[-1][0:1, 0:1]

    dx5, da6, db6, s6, dy6, red12 = ffn_bwd_tok(x5, dx6, y6, a6, b6, mod3(1, 2), norm_ffn2[1:2], wg12, wu12, wd12,
                                                tm=tmf, t_first=t_lat, name="ffn_bwd_12")
    tie = reduce_start("12", _ffn_grads(h6, da6, db6, s6, dy6, "12"))
    dhp, g_pw, red_pool = pool_bwd(dx5, pooled, yl, pw, pscale, mod3(1, 1, tie=tie), name="pool_bwd")
    dx4, red_pn = norm_mod_bwd(x4, dhp, dx5, mod3(1, 1), norm_mix[1:2], tm=tm, name="pool_norm_bwd")
    dx3, da4, db4, s4, dy4, red11 = ffn_bwd_tok(x3, dx4, y4, a4, b4, mod3(1, 0), norm_ffn1[1:2], wg11, wu11, wd11,
                                                tm=tmf, t_first=t_lat, name="ffn_bwd_11")
    g_pw = jnp.transpose(g_pw.reshape(n_grp, N_DEV, grp_rows, grp_w), (1, 0, 2, 3)).reshape(N_DEV, n_grp * grp_rows, grp_w)
    tie = reduce_start("11", list(_ffn_grads(h4, da4, db4, s4, dy4, "11")) + [g_pw])
    dx2, da3, db3, s3, dy3, red02 = ffn_bwd_tok(x2, dx3, y3, a3, b3, mod3(0, 2, tie=tie), norm_ffn2[0:1], wg02, wu02, wd02,
                                                tm=tmf, t_first=t_lat, name="ffn_bwd_02")
    tie = reduce_start("02", _ffn_grads(h3, da3, db3, s3, dy3, "02"))
    dz, dym, red_op = out_proj_bwd(dx2, ym, w_out, mod3(0, 1, tie=tie), tm=tm, name="out_proj_bwd")
    g_wout = matmul_tn(z, dym, nb=N_DEV, m=D // N_DEV, n=D, a_blocked=False, b_blocked=False,
                       tk=t_lat, name="wgrad_out")
    do, dp = ret_post_bwd(dz, o, p, heads=heads, name="ret_post_bwd")
    dp, g_conv = conv_bwd(dp, dz, p, w_conv, heads=heads, t_lat=t_lat, name="conv_bwd")
    dq, dk, dv, dlg = ret_bwd(qr, kr, vr, do, lg, t_lat=t_lat, tq=tq, tk=tk, name="ret_bwd")
    dp = ret_unprep(dp, dq, dk, dv, cos, sin, t_lat=t_lat, name="ret_unprep")
    dx1, red_mix = proj_bwd_tok(x1, dx2, dp, mod3(0, 1, True), norm_mix[0:1], w_in, tm=tm, name="proj_bwd")
    g_win = matmul_tn(hm, dp, nb=N_DEV, m=D, n=w_in.shape[2], a_blocked=False, b_blocked=False, tk=T, name="wgrad_in")
    tie = reduce_start("mix", [g_win, g_wout])
    dx0, da1, db1, s1, dy1, red01 = ffn_bwd_tok(x0, dx1, y1, a1, b1, mod3(0, 0, True, tie), norm_ffn1[0:1], wg01, wu01, wd01,
                                                tm=tm, t_first=t_lat, name="ffn_bwd_01")
    res = {}

    dm_x = jnp.stack([jnp.concatenate([red01[0, 0:3], red_mix[0, 0:2], red_op[2:3], red02[0, 0:3]], axis=0),
                      jnp.concatenate([red11[0, 0:3], red_pn[0:2], red_pool[2:3], red12[0, 0:3]], axis=0)])
    dm_c = jnp.concatenate([red01[1, 0:3], red_mix[1, 0:2], jnp.zeros((4, D), F32)], axis=0)
    d_lg = dlg[:, 0:2, 0].T
    d_dec_f = d_lg[0:1] * jax.nn.sigmoid(-ret_decay_fwd)
    d_dec_b = d_lg[1:2] * jax.nn.sigmoid(-ret_decay_bwd)
    pieces = [dm_x.reshape(-1), dm_c.reshape(-1),
              jnp.stack([red01[0, 3] + red01[1, 3], red11[0, 3]]).reshape(-1),
              jnp.stack([red_mix[0, 3] + red_mix[1, 3], red_pn[3]]).reshape(-1),
              jnp.stack([red02[0, 3], red12[0, 3]]).reshape(-1),
              red_fn[0], d_dec_f.reshape(-1), d_dec_b.reshape(-1), g_conv.reshape(-1), red_pool[4]]
    sizes = [int(a.shape[0]) for a in pieces]
    n_pack = sum(sizes)
    n_pad = -n_pack % 8192
    packed = jnp.concatenate(pieces + [jnp.zeros((n_pad,), F32)])[None, :]
    (packed_all,) = all_gather([packed], name="gather_partials")
    total = sum_partials(packed_all, name="sum_partials")[0]
    offs = [0]
    for s in sizes:
        offs.append(offs[-1] + s)
    seg = [total[offs[i]:offs[i + 1]] for i in range(len(sizes))]
    g_dm = seg[0].reshape(2, N_MOD * D)
    g_dmc = seg[1].reshape(N_MOD * D)
    g_b_mod = g_dm.at[0].add(g_dmc)
    g_norm_ffn1, g_norm_mix, g_norm_ffn2 = (seg[k].reshape(2, D) for k in (2, 3, 4))
    g_final = seg[5]
    g_dec_f, g_dec_b = seg[6].reshape(1, heads), seg[7].reshape(1, heads)
    g_conv_all = seg[8].reshape(CONV_WIDTH, -1)
    g_pscale_all = seg[9]
    conv_w = mix_w_conv.shape[2]
    g_w_conv = lax.dynamic_slice_in_dim(g_conv_all, me * conv_w, conv_w, axis=1)[None]
    ps_w = pool_scale.shape[1]
    g_pool_scale = lax.dynamic_slice_in_dim(g_pscale_all, me * ps_w, ps_w, axis=0)[None]

    dm_rows = packed_all[:, 0, 0:offs[1]].reshape(N_DEV, 2, N_MOD * D)
    dmc_rows = packed_all[:, 0, offs[1]:offs[2]].reshape(N_DEV, N_MOD * D)
    dm_x_blk = jnp.transpose(lax.dynamic_slice_in_dim(dm_rows, me * mod_w, mod_w, axis=2), (1, 0, 2))
    dm_c_blk = jnp.stack([lax.dynamic_slice_in_dim(dmc_rows, me * mod_w, mod_w, axis=1),
                          jnp.zeros((N_DEV, mod_w), F32)])
    g_w_mod, d_w_mod, nm_w_mod, nv_w_mod, cctx_part = adaln_bwd(cond16, dm_x_blk, dm_c_blk, w_mod, m_w_mod, v_w_mod,
                                                                name="adaln_bwd")
    cpad = jnp.concatenate([cctx_part[0], jnp.zeros((8192 - D,), F32)])[None, :] if D < 8192 else cctx_part[0:1]
    (cctx_all,) = all_gather([cpad], name="gather_cctx")
    g_c_ctx = sum_partials(cctx_all, name="sum_cctx")[0, :D]

    small = [("c_ctx", g_c_ctx, c_ctx, m_c_ctx, v_c_ctx), ("b_mod", g_b_mod, b_mod, m_b_mod, v_b_mod),
             ("norm_ffn1", g_norm_ffn1, norm_ffn1, m_norm_ffn1, v_norm_ffn1),
             ("norm_mix", g_norm_mix, norm_mix, m_norm_mix, v_norm_mix),
             ("norm_ffn2", g_norm_ffn2, norm_ffn2, m_norm_ffn2, v_norm_ffn2),
             ("mix_w_conv", g_w_conv, mix_w_conv, m_mix_w_conv, v_mix_w_conv),
             ("ret_decay_fwd", g_dec_f, ret_decay_fwd, m_ret_decay_fwd, v_ret_decay_fwd),
             ("ret_decay_bwd", g_dec_b, ret_decay_bwd, m_ret_decay_bwd, v_ret_decay_bwd),
             ("pool_scale", g_pool_scale, pool_scale, m_pool_scale, v_pool_scale),
             ("final_norm", g_final, final_norm, m_final_norm, v_final_norm)]
    ssz = [int(a[1].size) for a in small]
    spad = -sum(ssz) % LANE

    def pack(k):
        return jnp.concatenate([a[k].reshape(-1) for a in small] + [jnp.ones((spad,), F32)])[None, :]

    sd, sm, sv = adam_plain(pack(1), pack(2), pack(3), pack(4), name="adam_small")
    soff = [0]
    for s in ssz:
        soff.append(soff[-1] + s)
    for i, (nm, g, w, _, _) in enumerate(small):
        res[nm] = [g.reshape(w.shape)] + [a[0, soff[i]:soff[i + 1]].reshape(w.shape) for a in (sd, sm, sv)]
    res["w_mod"] = [g_w_mod, d_w_mod, nm_w_mod, nv_w_mod]

    after = sd
    for kind, lhs, rhs in (("gate", da1, h1), ("up", db1, h1), ("down", s1, dy1)):
        g = matmul_tn(lhs, rhs, nb=N_DEV, m=lhs.shape[2], n=D, a_blocked=True, b_blocked=False, tk=T,
                      name=f"wgrad_{kind}_01", after=after)
        tie = reduce_start(f"01_{kind}", [g], g)
        after = reducing[f"01_{kind}"][2][2]
    grad_x = dx0[:t_lat][None] + tie

    last = [grad_x]

    def reduce_finish(tag):
        grads, directs, handle = reducing[tag]
        landed = chip_exchange_wait(handle, last[0], name=f"rs_wait_{tag}")
        return list(zip(grads, directs, landed))

    def big(w, m, v, l, part, prev, nm, transposed=False):
        if transposed:
            w, m, v = (jnp.swapaxes(a, 1, 2) for a in (w, m, v))
        outs = adam_reduced(own_block, part[0], part[1], part[2], w, m, v, l, prev, name=nm)
        last[0] = outs[0]
        return outs

    ffn1 = [(ffn1_w_gate, m_ffn1_w_gate, v_ffn1_w_gate), (ffn1_w_up, m_ffn1_w_up, v_ffn1_w_up),
            (ffn1_w_down, m_ffn1_w_down, v_ffn1_w_down)]
    ffn2 = [(ffn2_w_gate, m_ffn2_w_gate, v_ffn2_w_gate), (ffn2_w_up, m_ffn2_w_up, v_ffn2_w_up),
            (ffn2_w_down, m_ffn2_w_down, v_ffn2_w_down)]
    kinds = ["gate", "up", "down"]
    half = {}
    parts = reduce_finish("12")
    for t, (w, m, v) in enumerate(ffn2):
        half[f"ffn2_w_{kinds[t]}"] = big(w, m, v, 1, parts[t], None, f"adam_ffn2_w_{kinds[t]}_1", transposed=t < 2)
    parts = reduce_finish("11")
    for t, (w, m, v) in enumerate(ffn1):
        half[f"ffn1_w_{kinds[t]}"] = big(w, m, v, 1, parts[t], None, f"adam_ffn1_w_{kinds[t]}_1", transposed=t < 2)
    res["pool_w"] = [a.reshape(pool_w.shape) for a in
                     big(pool_w.reshape(1, n_grp * grp_rows, grp_w), m_pool_w.reshape(1, n_grp * grp_rows, grp_w),
                         v_pool_w.reshape(1, n_grp * grp_rows, grp_w), 0, parts[3], None, "adam_pool_w")]
    parts = reduce_finish("02")
    for t, (w, m, v) in enumerate(ffn2):
        nm = f"ffn2_w_{kinds[t]}"
        outs = big(w, m, v, 0, parts[t], half[nm], f"adam_{nm}_0", transposed=t < 2)
        res[nm] = [jnp.swapaxes(a, 1, 2) for a in outs] if t < 2 else outs
    parts = reduce_finish("mix")
    res["mix_w_in"] = big(mix_w_in, m_mix_w_in, v_mix_w_in, 0, parts[0], None, "adam_mix_w_in")
    res["mix_w_out"] = big(mix_w_out, m_mix_w_out, v_mix_w_out, 0, parts[1], None, "adam_mix_w_out")
    for t, (w, m, v) in enumerate(ffn1):
        nm = f"ffn1_w_{kinds[t]}"
        outs = big(w, m, v, 0, reduce_finish(f"01_{kinds[t]}")[0], half[nm], f"adam_{nm}_0", transposed=t < 2)
        res[nm] = [jnp.swapaxes(a, 1, 2) for a in outs] if t < 2 else outs

    order = ["c_ctx", "w_mod", "b_mod", "norm_ffn1", "norm_mix", "norm_ffn2", "ffn1_w_gate", "ffn1_w_up", "ffn1_w_down",
             "ffn2_w_gate", "ffn2_w_up", "ffn2_w_down", "mix_w_in", "mix_w_conv", "mix_w_out", "ret_decay_fwd",
             "ret_decay_bwd", "pool_w", "pool_scale", "final_norm"]
    return (loss, grad_x, *[res[n][0] for n in order], *[res[n][1] for n in order],
            *[res[n][2] for n in order], *[res[n][3] for n in order])
```

```python
import jax
import jax.numpy as jnp
from jax import lax
from jax.experimental import pallas as pl
from jax.experimental.pallas import tpu as pltpu

F32 = jnp.float32
BF16 = jnp.bfloat16
MESH = pl.DeviceIdType.MESH

N_DEV = 8
N_MOD = 9
EPS = 1e-6
GN_EPS = 1e-5
MACARON = 0.5
HEAD_DIM = 128
K_SCALE = HEAD_DIM ** -0.5
ROPE_BASE = 10000.0
GRID_W = 64
CONV_WIDTH = 3
POOL_WINDOWS = (2, 4, 8, 16)
POOL_PAD = 16

ADAM_LR = 0.001
ADAM_B1 = 0.9
ADAM_B2 = 0.999
ADAM_EPS = 1e-08
ADAM_WD = 0.01
ADAM_STEP = 10

LANE = 128
ROW_CHUNK = 128
MM_ROWS = 256
VMEM_LIMIT = 56 * 1024 * 1024
ANY = pl.BlockSpec(memory_space=pl.ANY)


def _params(sem=None):
    kw = dict(vmem_limit_bytes=VMEM_LIMIT)
    if sem is not None:
        kw["dimension_semantics"] = sem
    return pltpu.CompilerParams(**kw)


def _nt(a, b):
    return lax.dot_general(a, b, (((1,), (1,)), ((), ())), preferred_element_type=F32)


def _tn(a, b):
    return lax.dot_general(a, b, (((0,), (0,)), ((), ())), preferred_element_type=F32)


def _nn(a, b):
    return jnp.dot(a, b, preferred_element_type=F32)


def _silu_parts(a):
    sig = jax.nn.sigmoid(a)
    return sig, a * sig


def _dsilu(a, sig):
    return sig * (1.0 + a * (1.0 - sig))


def _norm_mod(x, g, shift, scale):
    r = lax.rsqrt(jnp.mean(x * x, axis=-1, keepdims=True) + EPS)
    return (x * r * g) * (1.0 + scale) + shift


def _norm_mod_bwd(x, dh, g, scale):
    r = lax.rsqrt(jnp.mean(x * x, axis=-1, keepdims=True) + EPS)
    xhat = x * r
    dn = dh * (1.0 + scale)
    dshift = jnp.sum(dh, axis=0, keepdims=True)
    dscale = jnp.sum(dh * (xhat * g), axis=0, keepdims=True)
    dg = jnp.sum(dn * xhat, axis=0, keepdims=True)
    dxh = dn * g
    dx = r * (dxh - xhat * jnp.mean(dxh * xhat, axis=-1, keepdims=True))
    return dx, dshift, dscale, dg


def _mod_row(mod_ref, k, row0, nrows, t_first):
    if mod_ref.shape[0] == 1:
        return mod_ref[0, k:k + 1, :]
    row = row0 + lax.broadcasted_iota(jnp.int32, (nrows, 1), 0)
    return jnp.where(row >= t_first, mod_ref[1, k:k + 1, :], mod_ref[0, k:k + 1, :])


def _group_index(i, tiles_first, n_groups):
    if n_groups == 1:
        return 0
    return jnp.where(i >= tiles_first, 1, 0)


def _mesh_pos():
    x, y, c = lax.axis_index("x"), lax.axis_index("y"), lax.axis_index("c")
    chips = [(1 - x, y), (x, 1 - y), (1 - x, 1 - y)]
    return x, y, c, chips


def _flat(px, py, pc):
    return 4 * px + 2 * py + pc


def all_gather(shards, name):
    n = len(shards)

    def body(*refs):
        ins, outs = refs[:n], refs[n:2 * n]
        send_sems, recv_sems, local_sems = refs[2 * n:]
        x, y, c, chips = _mesh_pos()
        me, sibling = (x, y, c), (x, y, 1 - c)

        def copy(t, k, block, to, src=None):
            dst = outs[t].at[_flat(*block)]
            return pltpu.make_async_remote_copy(
                src_ref=dst if src is None else src, dst_ref=dst,
                send_sem=send_sems.at[t, k], recv_sem=recv_sems.at[t, k],
                device_id=to, device_id_type=MESH)

        mine = [pltpu.make_async_copy(ins[t], outs[t].at[_flat(*me)], local_sems.at[t]) for t in range(n)]
        for cp in mine:
            cp.start()
        first = []
        for t in range(n):
            first.append(copy(t, 0, me, sibling, src=ins[t]))
            first += [copy(t, 1 + j, me, (*chip, c), src=ins[t]) for j, chip in enumerate(chips)]
        for cp in first:
            cp.start()
        passed = []
        for t in range(n):
            for j, chip in enumerate(chips):
                copy(t, 1 + j, (*chip, c), me).wait_recv()
                fwd = copy(t, 4 + j, (*chip, c), sibling)
                fwd.start()
                passed.append(fwd)
        for t in range(n):
            copy(t, 0, sibling, me).wait_recv()
            for j, chip in enumerate(chips):
                copy(t, 4 + j, (*chip, 1 - c), me).wait_recv()
        for cp in first + passed:
            cp.wait_send()
        for cp in mine:
            cp.wait()

    return pl.pallas_call(
        body, name=name,
        out_shape=[jax.ShapeDtypeStruct((N_DEV,) + s.shape, s.dtype) for s in shards],
        in_specs=[ANY] * n, out_specs=[ANY] * n,
        scratch_shapes=[pltpu.SemaphoreType.DMA((n, 7)), pltpu.SemaphoreType.DMA((n, 7)), pltpu.SemaphoreType.DMA((n,))],
    )(*shards)


HBM_SPEC = pl.BlockSpec(memory_space=pltpu.HBM)
SEM_SPEC = pl.BlockSpec(memory_space=pltpu.SEMAPHORE)
DATAFLOW = pltpu.SideEffectType.DATAFLOW_SIDE_EFFECTING


def _in_hbm(a):
    return pltpu.with_memory_space_constraint(a, pltpu.HBM)


def _push_peers():
    x, y, c, chips = _mesh_pos()
    return [(*chip, c) for chip in chips] + [(x, y, 1 - c)]


def cast_place(w, layer, own_block, name):
    _, R, C = w.shape
    tr = _row_tile(R, C)

    def body(idx_ref, w_ref, o_ref):
        o_ref[...] = w_ref[...].astype(BF16)

    return pl.pallas_call(
        body, name=name,
        grid_spec=pltpu.PrefetchScalarGridSpec(
            num_scalar_prefetch=1, grid=(R // tr,),
            in_specs=[pl.BlockSpec((None, tr, C), lambda i, idx: (layer, i, 0))],
            out_specs=pl.BlockSpec((None, tr, C), lambda i, idx: (idx[0], i, 0))),
        out_shape=jax.ShapeDtypeStruct((N_DEV, R, C), BF16),
        compiler_params=_params(("arbitrary",)),
    )(own_block, w)


def gather_start(lands, after, name):
    n = len(lands)

    def body(*refs):
        lz = refs[:n]
        send_sems, recv_sems = refs[n + 1], refs[n + 2]
        token = refs[-1]
        x, y, c, _ = _mesh_pos()
        for t in range(n):
            mine = lz[t].at[_flat(x, y, c)]
            for k, peer in enumerate(_push_peers()):
                pltpu.make_async_remote_copy(
                    src_ref=mine, dst_ref=mine, send_sem=send_sems.at[4 * t + k],
                    recv_sem=recv_sems.at[4 * t + k], device_id=peer, device_id_type=MESH).start()
        token[...] = jnp.zeros_like(token)

    return pl.pallas_call(
        body, name=name,
        out_shape=(pltpu.SemaphoreType.DMA((4 * n,)), pltpu.SemaphoreType.DMA((4 * n,)),
                   *[pltpu.HBM(l.shape, l.dtype) for l in lands], jax.ShapeDtypeStruct((8, LANE), F32)),
        in_specs=[HBM_SPEC] * n + [ANY],
        out_specs=(SEM_SPEC, SEM_SPEC, *[HBM_SPEC] * n, pl.BlockSpec(memory_space=pltpu.VMEM)),
        input_output_aliases={i: 2 + i for i in range(n)},
        compiler_params=pltpu.CompilerParams(has_side_effects=DATAFLOW),
    )(*[_in_hbm(l) for l in lands], after)


def gather_wait(handle, after, name):
    send, recv = handle[0], handle[1]
    bufs = handle[2:-1]
    n = len(bufs)

    def body(*refs):
        lz = refs[:n]
        send_sems, recv_sems = refs[n], refs[n + 1]
        x, y, c, _ = _mesh_pos()
        for t in range(n):
            for k, peer in enumerate(_push_peers()):
                cp = pltpu.make_async_remote_copy(
                    src_ref=lz[t].at[_flat(x, y, c)], dst_ref=lz[t].at[_flat(*peer)], send_sem=send_sems.at[4 * t + k],
                    recv_sem=recv_sems.at[4 * t + k], device_id=peer, device_id_type=MESH)
                cp.wait_send()
                cp.wait_recv()

    outs = pl.pallas_call(
        body, name=name,
        out_shape=tuple(pltpu.HBM(b.shape, b.dtype) for b in bufs),
        in_specs=[HBM_SPEC] * n + [SEM_SPEC, SEM_SPEC, ANY],
        out_specs=tuple([HBM_SPEC] * n),
        input_output_aliases={i: i for i in range(n)},
        compiler_params=pltpu.CompilerParams(has_side_effects=DATAFLOW),
    )(*bufs, send, recv, after)
    return list(outs)


def gather_finish(lands, name):
    n = len(lands)

    def body(*refs):
        outs = refs[n:2 * n]
        send_sems, recv_sems = refs[2 * n:]
        x, y, c, chips = _mesh_pos()
        sibling = (x, y, 1 - c)

        def copy(t, j, core):
            blk = outs[t].at[_flat(*chips[j], core)]
            return pltpu.make_async_remote_copy(
                src_ref=blk, dst_ref=blk, send_sem=send_sems.at[t, j], recv_sem=recv_sems.at[t, j],
                device_id=sibling, device_id_type=MESH)

        sends = [copy(t, j, c) for t in range(n) for j in range(3)]
        for cp in sends:
            cp.start()
        for t in range(n):
            for j in range(3):
                copy(t, j, 1 - c).wait_recv()
        for cp in sends:
            cp.wait_send()

    return pl.pallas_call(
        body, name=name,
        out_shape=[jax.ShapeDtypeStruct(l.shape, l.dtype) for l in lands],
        in_specs=[ANY] * n, out_specs=[ANY] * n,
        input_output_aliases={t: t for t in range(n)},
        scratch_shapes=[pltpu.SemaphoreType.DMA((n, 3)), pltpu.SemaphoreType.DMA((n, 3))],
    )(*lands)


def chip_exchange_start(sums, name):
    n = len(sums)
    lands = [lax.empty(s.shape, s.dtype) for s in sums]

    def body(*refs):
        ins, lz = refs[:n], refs[n:2 * n]
        send_sems, recv_sems = refs[2 * n], refs[2 * n + 1]
        token = refs[-1]
        peers = _push_peers()
        for t in range(n):
            for j in range(3):
                pltpu.make_async_remote_copy(
                    src_ref=ins[t].at[j], dst_ref=lz[t].at[j], send_sem=send_sems.at[3 * t + j],
                    recv_sem=recv_sems.at[3 * t + j], device_id=peers[j], device_id_type=MESH).start()
        token[...] = jnp.zeros_like(token)

    return pl.pallas_call(
        body, name=name,
        out_shape=(pltpu.SemaphoreType.DMA((3 * n,)), pltpu.SemaphoreType.DMA((3 * n,)),
                   *[pltpu.HBM(s.shape, s.dtype) for s in sums], *[pltpu.HBM(s.shape, s.dtype) for s in sums],
                   jax.ShapeDtypeStruct((8, LANE), F32)),
        in_specs=[HBM_SPEC] * (2 * n),
        out_specs=(SEM_SPEC, SEM_SPEC, *[HBM_SPEC] * (2 * n), pl.BlockSpec(memory_space=pltpu.VMEM)),
        input_output_aliases={i: 2 + i for i in range(2 * n)},
        compiler_params=pltpu.CompilerParams(has_side_effects=DATAFLOW),
    )(*[_in_hbm(s) for s in sums], *[_in_hbm(l) for l in lands])


def chip_exchange_wait(handle, after, name):
    send, recv = handle[0], handle[1]
    n = (len(handle) - 3) // 2
    bufs = handle[2:2 + 2 * n]

    def body(*refs):
        ins, lz = refs[:n], refs[n:2 * n]
        send_sems, recv_sems = refs[2 * n], refs[2 * n + 1]
        peers = _push_peers()
        for t in range(n):
            for j in range(3):
                cp = pltpu.make_async_remote_copy(
                    src_ref=ins[t].at[j], dst_ref=lz[t].at[j], send_sem=send_sems.at[3 * t + j],
                    recv_sem=recv_sems.at[3 * t + j], device_id=peers[j], device_id_type=MESH)
                cp.wait_send()
                cp.wait_recv()

    outs = pl.pallas_call(
        body, name=name,
        out_shape=tuple(pltpu.HBM(b.shape, b.dtype) for b in bufs),
        in_specs=[HBM_SPEC] * (2 * n) + [SEM_SPEC, SEM_SPEC, ANY],
        out_specs=tuple([HBM_SPEC] * (2 * n)),
        input_output_aliases={i: i for i in range(2 * n)},
        compiler_params=pltpu.CompilerParams(has_side_effects=DATAFLOW),
    )(*bufs, send, recv, after)
    return list(outs[n:])


def rs_sibling_exchange(grads, after, name):
    n = len(grads)

    def body(*refs):
        ins, stages, directs = refs[:n], refs[n + 1:2 * n + 1], refs[2 * n + 1:3 * n + 1]
        send_sems, recv_sems = refs[3 * n + 1:]
        x, y, c, chips = _mesh_pos()
        sibling = (x, y, 1 - c)

        def copy(t, k):
            if k < 3:
                src, dst = ins[t].at[_flat(*chips[k], 1 - c)], stages[t].at[k]
            else:
                src, dst = ins[t].at[_flat(x, y, 1 - c)], directs[t]
            return pltpu.make_async_remote_copy(
                src_ref=src, dst_ref=dst, send_sem=send_sems.at[t, k], recv_sem=recv_sems.at[t, k],
                device_id=sibling, device_id_type=MESH)

        cps = [copy(t, k) for t in range(n) for k in range(4)]
        for cp in cps:
            cp.start()
        for cp in cps:
            cp.wait_recv()
        for cp in cps:
            cp.wait_send()

    return pl.pallas_call(
        body, name=name,
        out_shape=[jax.ShapeDtypeStruct((3,) + g.shape[1:], g.dtype) for g in grads]
        + [jax.ShapeDtypeStruct(g.shape[1:], g.dtype) for g in grads],
        in_specs=[ANY] * (n + 1), out_specs=[ANY] * (2 * n),
        scratch_shapes=[pltpu.SemaphoreType.DMA((n, 4)), pltpu.SemaphoreType.DMA((n, 4))],
    )(*grads, after)


def _row_tile(rows, cols, limit_bytes=3 << 19):
    best = None
    for t in range(16, rows + 1, 16):
        if rows % t == 0 and t * cols * 4 <= limit_bytes:
            best = t
    return best if best is not None else rows


def chip_sum(grad, stage, peer_blocks, name):
    _, R, C = grad.shape
    tr = _row_tile(R, C)

    def body(idx_ref, g_ref, s_ref, o_ref):
        o_ref[...] = (g_ref[...].astype(F32) + s_ref[...].astype(F32)).astype(BF16)

    return pl.pallas_call(
        body, name=name,
        grid_spec=pltpu.PrefetchScalarGridSpec(
            num_scalar_prefetch=1, grid=(3, R // tr),
            in_specs=[pl.BlockSpec((None, tr, C), lambda j, i, idx: (idx[j], i, 0)),
                      pl.BlockSpec((None, tr, C), lambda j, i, idx: (j, i, 0))],
            out_specs=pl.BlockSpec((None, tr, C), lambda j, i, idx: (j, i, 0))),
        out_shape=jax.ShapeDtypeStruct((3, R, C), BF16),
        compiler_params=_params(("arbitrary", "arbitrary")),
    )(peer_blocks, grad, stage)


def _adamw(w, g, m, v):
    m2 = ADAM_B1 * m + (1.0 - ADAM_B1) * g
    v2 = ADAM_B2 * v + (1.0 - ADAM_B2) * (g * g)
    m_hat = m2 / (1.0 - ADAM_B1 ** ADAM_STEP)
    v_hat = v2 / (1.0 - ADAM_B2 ** ADAM_STEP)
    delta = -ADAM_LR * (m_hat / (jnp.sqrt(v_hat) + ADAM_EPS) + ADAM_WD * w)
    return delta, m2, v2


def adam_reduced(own_block, grad, direct, landed, w, m, v, layer, prev, name):
    L, R, C = w.shape
    tr = _row_tile(R, C, 1 << 20)
    first = prev is None

    def body(idx_ref, g_ref, d_ref, l_ref, w_ref, m_ref, v_ref, *rest):
        og, od, om, ov = rest[-4:]
        g = g_ref[...].astype(F32) + d_ref[...].astype(F32)
        for j in range(3):
            g = g + l_ref[j].astype(F32)
        delta, m2, v2 = _adamw(w_ref[...], g, m_ref[...], v_ref[...])
        og[...] = g
        od[...] = delta
        om[...] = m2
        ov[...] = v2

    lay = pl.BlockSpec((None, tr, C), lambda i, idx: (layer, i, 0))
    in_specs = [pl.BlockSpec((None, tr, C), lambda i, idx: (idx[0], i, 0)),
                pl.BlockSpec((tr, C), lambda i, idx: (i, 0)),
                pl.BlockSpec((3, tr, C), lambda i, idx: (0, i, 0)),
                lay, lay, lay]
    args = [own_block, grad, direct, landed, w, m, v]
    aliases = {}
    if not first:
        in_specs += [ANY] * 4
        args += list(prev)
        aliases = {7 + k: k for k in range(4)}
    return pl.pallas_call(
        body, name=name,
        grid_spec=pltpu.PrefetchScalarGridSpec(
            num_scalar_prefetch=1, grid=(R // tr,), in_specs=in_specs, out_specs=[lay] * 4),
        out_shape=[jax.ShapeDtypeStruct((L, R, C), F32)] * 4,
        input_output_aliases=aliases,
        compiler_params=_params(("arbitrary",)),
    )(*args)


def sum_partials(parts, name):
    _, _, N = parts.shape
    tn = 8192

    def body(p_ref, o_ref):
        g = p_ref[0]
        for k in range(1, N_DEV):
            g = g + p_ref[k]
        o_ref[...] = g

    return pl.pallas_call(
        body, name=name, grid=(N // tn,),
        in_specs=[pl.BlockSpec((N_DEV, 1, tn), lambda i: (0, 0, i))],
        out_specs=pl.BlockSpec((1, tn), lambda i: (0, i)),
        out_shape=jax.ShapeDtypeStruct((1, N), F32),
    )(parts)


def adam_plain(g, w, m, v, name):
    _, N = g.shape

    def body(g_ref, w_ref, m_ref, v_ref, od, om, ov):
        delta, m2, v2 = _adamw(w_ref[...], g_ref[...], m_ref[...], v_ref[...])
        od[...] = delta
        om[...] = m2
        ov[...] = v2

    return pl.pallas_call(
        body, name=name, out_shape=[jax.ShapeDtypeStruct((1, N), F32)] * 3,
    )(g, w, m, v)


def adaln_fwd(cond16, w_mod, b_blk, name):
    L, D, W = w_mod.shape
    tn = 768 if W % 768 == 0 else W

    def body(c_ref, w_ref, b_ref, o_ref):
        c = c_ref[...]
        sc = (c * jax.nn.sigmoid(c)).astype(BF16)
        o_ref[...] = _nn(sc, w_ref[...].astype(BF16)) + b_ref[...]

    return pl.pallas_call(
        body, name=name, grid=(L, W // tn),
        in_specs=[pl.BlockSpec((16, D), lambda l, i: (0, 0)),
                  pl.BlockSpec((None, D, tn), lambda l, i: (l, 0, i)),
                  pl.BlockSpec((None, 1, tn), lambda l, i: (l, 0, i))],
        out_specs=pl.BlockSpec((None, 16, tn), lambda l, i: (l, 0, i)),
        out_shape=jax.ShapeDtypeStruct((L, 16, W), F32),
        compiler_params=_params(("arbitrary", "arbitrary")),
    )(cond16, w_mod, b_blk)


def adaln_bwd(cond16, dm_x, dm_c, w_mod, m_mod, v_mod, name):
    L, D, W = w_mod.shape
    tn = 256 if W % 256 == 0 else W
    nt = W // tn

    def body(c_ref, cT_ref, dx_ref, dc_ref, w_ref, m_ref, v_ref, og, od, om, ov, pc_ref):
        l, i = pl.program_id(0), pl.program_id(1)
        c = c_ref[...]
        sig, sl = _silu_parts(c)
        cT = cT_ref[...]
        sigT = jax.nn.sigmoid(cT)
        scT = (cT * sigT).astype(BF16)
        dmc = jnp.sum(dc_ref[...], axis=0, keepdims=True)
        dm16 = jnp.concatenate([dx_ref[...], jnp.broadcast_to(dmc, (8, tn))], axis=0)
        row = lax.broadcasted_iota(jnp.int32, (16, tn), 0)
        dm16 = jnp.where(row <= 8, dm16, 0.0).astype(BF16)
        w = w_ref[...]
        g = _nn(scT, dm16)
        delta, m2, v2 = _adamw(w, g, m_ref[...], v_ref[...])
        og[...] = g
        od[...] = delta
        om[...] = m2
        ov[...] = v2

        @pl.when((l == 0) & (i == 0))
        def _():
            pc_ref[...] = jnp.zeros_like(pc_ref)

        @pl.when(l == 0)
        def _():
            back = _nt(jnp.broadcast_to(dmc, (8, tn)).astype(BF16), w.astype(BF16))
            pc_ref[...] += back * _dsilu(c[8:9, :], sig[8:9, :])

    col = pl.BlockSpec((None, D, tn), lambda l, i: (l, 0, i))
    row8 = pl.BlockSpec((None, 8, tn), lambda l, i: (l, 0, i))
    return pl.pallas_call(
        body, name=name, grid=(L, nt),
        in_specs=[pl.BlockSpec((16, D), lambda l, i: (0, 0)), pl.BlockSpec((D, 16), lambda l, i: (0, 0)),
                  row8, row8, col, col, col],
        out_specs=[col, col, col, col, pl.BlockSpec((8, D), lambda l, i: (0, 0))],
        out_shape=[jax.ShapeDtypeStruct((L, D, W), F32)] * 4 + [jax.ShapeDtypeStruct((8, D), F32)],
        compiler_params=_params(("arbitrary", "arbitrary")),
    )(cond16, cond16.T, dm_x, dm_c, w_mod, m_mod, v_mod)


def _token_spec(tm, D):
    if tm > 256:
        return pl.BlockSpec((tm, D), lambda i, j: (i, 0), pipeline_mode=pl.Buffered(1))
    return pl.BlockSpec((tm, D), lambda i, j: (i, 0))


def ffn_fwd(x, mod, gn, wg, wu, wd, *, tm, t_first, name):
    T, D = x.shape
    nb, cw, _ = wg.shape
    G = mod.shape[0]
    nt = T // tm
    mm_rows = MM_ROWS if tm % MM_ROWS == 0 else tm // 2

    def body(x_ref, mod_ref, g_ref, wg_ref, wu_ref, wd_ref, xo_ref, h_ref, a_ref, b_ref, y_ref, hs, acc, a_s, b_s, s_s):
        i, j = pl.program_id(0), pl.program_id(1)

        @pl.when(j == 0)
        def _():
            def head(r, carry):
                rows = pl.ds(pl.multiple_of(r * ROW_CHUNK, ROW_CHUNK), ROW_CHUNK)
                row0 = i * tm + r * ROW_CHUNK
                hb = _norm_mod(x_ref[rows, :], g_ref[...], _mod_row(mod_ref, 0, row0, ROW_CHUNK, t_first),
                               _mod_row(mod_ref, 1, row0, ROW_CHUNK, t_first)).astype(BF16)
                hs[rows, :] = hb
                h_ref[rows, :] = hb
                return carry

            lax.fori_loop(0, tm // ROW_CHUNK, head, 0)
            acc[...] = jnp.zeros_like(acc)

        parts = [slice(r, r + mm_rows) for r in range(0, tm, mm_rows)]
        for rows in parts:
            a_s[rows, :] = _nt(hs[rows, :], wg_ref[...])
            b_s[rows, :] = _nt(hs[rows, :], wu_ref[...])
        for r in range(0, tm, ROW_CHUNK):
            rows = slice(r, r + ROW_CHUNK)
            av, bv = a_s[rows, :], b_s[rows, :]
            a_ref[rows, :] = av.astype(BF16)
            b_ref[rows, :] = bv.astype(BF16)
            _, sl = _silu_parts(av)
            s_s[rows, :] = (sl * bv).astype(BF16)
        for rows in parts:
            acc[rows, :] += _nn(s_s[rows, :], wd_ref[...])

        @pl.when(j == nb - 1)
        def _():
            def tail(r, carry):
                rows = pl.ds(pl.multiple_of(r * ROW_CHUNK, ROW_CHUNK), ROW_CHUNK)
                y = acc[rows, :]
                y_ref[rows, :] = y.astype(BF16)
                gate = _mod_row(mod_ref, 2, i * tm + r * ROW_CHUNK, ROW_CHUNK, t_first)
                xo_ref[rows, :] = x_ref[rows, :] + (MACARON * gate) * y
                return carry

            lax.fori_loop(0, tm // ROW_CHUNK, tail, 0)

    tok = _token_spec(tm, D)
    act = pl.BlockSpec((None, tm, cw), lambda i, j: (j, i, 0))
    wblk = pl.BlockSpec((None, cw, D), lambda i, j: (j, 0, 0))
    return pl.pallas_call(
        body, name=name, grid=(nt, nb),
        in_specs=[tok, pl.BlockSpec((G, 3, D), lambda i, j: (0, 0, 0)),
                  pl.BlockSpec((1, D), lambda i, j: (0, 0)), wblk, wblk, wblk],
        out_specs=[tok, tok, act, act, tok],
        out_shape=[jax.ShapeDtypeStruct((T, D), F32), jax.ShapeDtypeStruct((T, D), BF16),
                   jax.ShapeDtypeStruct((nb, T, cw), BF16), jax.ShapeDtypeStruct((nb, T, cw), BF16),
                   jax.ShapeDtypeStruct((T, D), BF16)],
        scratch_shapes=[pltpu.VMEM((tm, D), BF16), pltpu.VMEM((tm, D), F32), pltpu.VMEM((tm, cw), F32),
                        pltpu.VMEM((tm, cw), F32), pltpu.VMEM((tm, cw), BF16)],
        compiler_params=_params(("arbitrary", "arbitrary")),
    )(x, mod, gn, wg, wu, wd)


def ffn_bwd_tok(x, dxo, y, a, b, mod, gn, wg, wu, wd, *, tm, t_first, name):
    T, D = x.shape
    nb, cw, _ = wg.shape
    G = mod.shape[0]
    nt = T // tm
    first = t_first // tm

    def body(x_ref, dxo_ref, y_ref, a_ref, b_ref, mod_ref, g_ref, wg_ref, wu_ref, wd_ref,
             dx_ref, da_ref, db_ref, s_ref, dy_ref, red_ref, dys, dh, ds_s):
        i, j = pl.program_id(0), pl.program_id(1)

        @pl.when(j == 0)
        def _():
            def head(r, carry):
                rows = pl.ds(pl.multiple_of(r * ROW_CHUNK, ROW_CHUNK), ROW_CHUNK)
                dyb = ((MACARON * mod_ref[2:3, :]) * dxo_ref[rows, :]).astype(BF16)
                dys[rows, :] = dyb
                dy_ref[rows, :] = dyb
                return carry

            lax.fori_loop(0, tm // ROW_CHUNK, head, 0)
            dh[...] = jnp.zeros_like(dh)

        parts = [slice(r, r + MM_ROWS) for r in range(0, tm, MM_ROWS)]
        for rows in parts:
            ds_s[rows, :] = _nt(dys[rows, :], wd_ref[...])
        for r in range(0, tm, ROW_CHUNK):
            rows = slice(r, r + ROW_CHUNK)
            av = a_ref[rows, :].astype(F32)
            bv = b_ref[rows, :].astype(F32)
            ds = ds_s[rows, :]
            sig, sl = _silu_parts(av)
            s_ref[rows, :] = (sl * bv).astype(BF16)
            da_ref[rows, :] = (ds * bv * _dsilu(av, sig)).astype(BF16)
            db_ref[rows, :] = (ds * sl).astype(BF16)
        for rows in parts:
            dh[rows, :] += _nn(da_ref[rows, :], wg_ref[...]) + _nn(db_ref[rows, :], wu_ref[...])

        @pl.when((j == 0) & ((i == 0) | (i == first)))
        def _():
            red_ref[...] = jnp.zeros_like(red_ref)

        @pl.when(j == nb - 1)
        def _():
            def tail(r, carry):
                rows = pl.ds(pl.multiple_of(r * ROW_CHUNK, ROW_CHUNK), ROW_CHUNK)
                dxo_v = dxo_ref[rows, :]
                dxn, dshift, dscale, dg = _norm_mod_bwd(x_ref[rows, :], dh[rows, :], g_ref[...], mod_ref[1:2, :])
                dx_ref[rows, :] = dxo_v + dxn
                red_ref[0:1, :] += dshift
                red_ref[1:2, :] += dscale
                red_ref[2:3, :] += jnp.sum((MACARON * dxo_v) * y_ref[rows, :].astype(F32), axis=0, keepdims=True)
                red_ref[3:4, :] += dg
                return carry

            lax.fori_loop(0, tm // ROW_CHUNK, tail, 0)

    tok = _token_spec(tm, D)
    act = pl.BlockSpec((None, tm, cw), lambda i, j: (j, i, 0))
    wblk = pl.BlockSpec((None, cw, D), lambda i, j: (j, 0, 0))
    return pl.pallas_call(
        body, name=name, grid=(nt, nb),
        in_specs=[tok, tok, tok, act, act,
                  pl.BlockSpec((None, 3, D), lambda i, j: (_group_index(i, first, G), 0, 0)),
                  pl.BlockSpec((1, D), lambda i, j: (0, 0)), wblk, wblk, wblk],
        out_specs=[tok, act, act, act, tok,
                   pl.BlockSpec((None, 8, D), lambda i, j: (_group_index(i, first, G), 0, 0))],
        out_shape=[jax.ShapeDtypeStruct((T, D), F32)] + [jax.ShapeDtypeStruct((nb, T, cw), BF16)] * 3
        + [jax.ShapeDtypeStruct((T, D), BF16), jax.ShapeDtypeStruct((G, 8, D), F32)],
        scratch_shapes=[pltpu.VMEM((tm, D), BF16), pltpu.VMEM((tm, D), F32), pltpu.VMEM((tm, cw), F32)],
        compiler_params=_params(("arbitrary", "arbitrary")),
    )(x, dxo, y, a, b, mod, gn, wg, wu, wd)


def matmul_tn(a, b, *, nb, m, n, a_blocked, b_blocked, tk, name, after=None):
    T = a.shape[-2]
    extra = [] if after is None else [after]

    def spec(arr, blocked, width):
        if blocked:
            return pl.BlockSpec((None, tk, width), lambda j, k: (j, k, 0))
        if arr.shape[-1] == width:
            return pl.BlockSpec((tk, width), lambda j, k: (k, 0))
        return pl.BlockSpec((tk, width), lambda j, k: (k, j))

    nk = T // tk

    def body(a_ref, b_ref, *rest):
        if nk == 1:
            rest[-1][...] = _tn(a_ref[...], b_ref[...]).astype(BF16)
            return
        o_ref, acc = rest[-2:]
        k = pl.program_id(1)

        @pl.when(k == 0)
        def _():
            acc[...] = jnp.zeros_like(acc)

        acc[...] += _tn(a_ref[...], b_ref[...])

        @pl.when(k == nk - 1)
        def _():
            o_ref[...] = acc[...].astype(BF16)

    return pl.pallas_call(
        body, name=name, grid=(nb, nk),
        in_specs=[spec(a, a_blocked, m), spec(b, b_blocked, n)] + [ANY] * len(extra),
        out_specs=pl.BlockSpec((None, m, n), lambda j, k: (j, 0, 0)),
        out_shape=jax.ShapeDtypeStruct((nb, m, n), BF16),
        scratch_shapes=[] if nk == 1 else [pltpu.VMEM((m, n), F32)],
        compiler_params=_params(("arbitrary", "arbitrary")),
    )(a, b, *extra)


def proj_in(x, mod, gn, w_in, *, tm, t_first, name):
    T, D = x.shape
    nb, _, cw = w_in.shape
    G = mod.shape[0]
    nt = T // tm

    def body(x_ref, mod_ref, g_ref, w_ref, p_ref, h_ref, hs):
        i = pl.program_id(0)

        @pl.when(pl.program_id(1) == 0)
        def _():
            def head(r, carry):
                rows = pl.ds(pl.multiple_of(r * ROW_CHUNK, ROW_CHUNK), ROW_CHUNK)
                row0 = i * tm + r * ROW_CHUNK
                hb = _norm_mod(x_ref[rows, :], g_ref[...], _mod_row(mod_ref, 0, row0, ROW_CHUNK, t_first),
                               _mod_row(mod_ref, 1, row0, ROW_CHUNK, t_first)).astype(BF16)
                hs[rows, :] = hb
                h_ref[rows, :] = hb
                return carry

            lax.fori_loop(0, tm // ROW_CHUNK, head, 0)

        for r in range(0, tm, MM_ROWS):
            p_ref[r:r + MM_ROWS, :] = _nn(hs[r:r + MM_ROWS, :], w_ref[...])

    tok = _token_spec(tm, D)
    return pl.pallas_call(
        body, name=name, grid=(nt, nb),
        in_specs=[tok, pl.BlockSpec((G, 3, D), lambda i, j: (0, 0, 0)),
                  pl.BlockSpec((1, D), lambda i, j: (0, 0)),
                  pl.BlockSpec((None, D, cw), lambda i, j: (j, 0, 0))],
        out_specs=[pl.BlockSpec((tm, cw), lambda i, j: (i, j)), tok],
        out_shape=[jax.ShapeDtypeStruct((T, nb * cw), F32), jax.ShapeDtypeStruct((T, D), BF16)],
        scratch_shapes=[pltpu.VMEM((tm, D), BF16)],
        compiler_params=_params(("arbitrary", "arbitrary")),
    )(x, mod, gn, w_in)


def proj_bwd_tok(x, dxo, dp, mod, gn, w_in, *, tm, name):
    T, D = x.shape
    nb, _, cw = w_in.shape
    G = mod.shape[0]
    nt = T // tm
    first = dxo.shape[0] // tm

    def body(x_ref, dxo_ref, dp_ref, mod_ref, g_ref, w_ref, dx_ref, red_ref, dh):
        i, j = pl.program_id(0), pl.program_id(1)

        @pl.when(j == 0)
        def _():
            dh[...] = jnp.zeros_like(dh)

        dh[...] += _nt(dp_ref[...], w_ref[...])

        @pl.when((j == 0) & ((i == 0) | (i == first)))
        def _():
            red_ref[...] = jnp.zeros_like(red_ref)

        @pl.when(j == nb - 1)
        def _():
            dxn, dshift, dscale, dg = _norm_mod_bwd(x_ref[...], dh[...], g_ref[...], mod_ref[1:2, :])
            dx_ref[...] = jnp.where(i < first, dxo_ref[...], 0.0) + dxn
            red_ref[0:1, :] += dshift
            red_ref[1:2, :] += dscale
            red_ref[3:4, :] += dg

    tok = pl.BlockSpec((tm, D), lambda i, j: (i, 0))
    return pl.pallas_call(
        body, name=name, grid=(nt, nb),
        in_specs=[tok, pl.BlockSpec((tm, D), lambda i, j: (jnp.minimum(i, first - 1), 0)),
                  pl.BlockSpec((tm, cw), lambda i, j: (i, j)),
                  pl.BlockSpec((None, 3, D), lambda i, j: (_group_index(i, first, G), 0, 0)),
                  pl.BlockSpec((1, D), lambda i, j: (0, 0)),
                  pl.BlockSpec((None, D, cw), lambda i, j: (j, 0, 0))],
        out_specs=[tok, pl.BlockSpec((None, 8, D), lambda i, j: (_group_index(i, first, G), 0, 0))],
        out_shape=[jax.ShapeDtypeStruct((T, D), F32), jax.ShapeDtypeStruct((G, 8, D), F32)],
        scratch_shapes=[pltpu.VMEM((tm, D), F32)],
        compiler_params=_params(("arbitrary", "arbitrary")),
    )(x, dxo, dp, mod, gn, w_in)


def _rope(t, cos, sin):
    return t * cos + pltpu.roll(t, HEAD_DIM // 2, axis=1) * sin


def ret_prep(p, cos, sin, *, heads, tm, name):
    T = p.shape[0]

    def body(q_ref, k_ref, v_ref, c_ref, s_ref, qo, ko, vo):
        cos_v, sin_v = c_ref[...], s_ref[...]
        qo[...] = _rope(q_ref[...], cos_v, sin_v).astype(BF16)
        ko[...] = _rope(k_ref[...] * K_SCALE, cos_v, sin_v).astype(BF16)
        vo[...] = v_ref[...].astype(BF16)

    def col(part):
        return pl.BlockSpec((tm, HEAD_DIM), lambda h, i: (i, part * heads + h))

    tab = pl.BlockSpec((tm, HEAD_DIM), lambda h, i: (i, 0))
    out = pl.BlockSpec((None, tm, HEAD_DIM), lambda h, i: (h, i, 0))
    return pl.pallas_call(
        body, name=name, grid=(heads, T // tm),
        in_specs=[col(0), col(1), col(2), tab, tab], out_specs=[out, out, out],
        out_shape=[jax.ShapeDtypeStruct((heads, T, HEAD_DIM), BF16)] * 3,
        compiler_params=_params(("arbitrary", "arbitrary")),
    )(p, p, p, cos, sin)


def _decay(n, m, lgf, lgb, t_lat, t_ctx):
    df = jnp.where(m < t_lat, n - m, n - m + (t_lat + t_ctx))
    db = m - n
    ef = jnp.where(df >= 0, jnp.exp(lgf * df), 0.0)
    eb = jnp.where(db >= 0, jnp.exp(lgb * db), 0.0)
    return ef, eb, df, db


def ret_fwd(q, k, v, lg, *, t_lat, tq, tk, name):
    H, T, _ = k.shape
    t_ctx = T - t_lat

    def body(lg_ref, q_ref, k_ref, v_ref, o_ref):
        h, qi, kj = pl.program_id(0), pl.program_id(1), pl.program_id(2)

        @pl.when(kj == 0)
        def _():
            o_ref[...] = jnp.zeros_like(o_ref)

        s = _nt(q_ref[...], k_ref[...])
        n = (qi * tq + lax.broadcasted_iota(jnp.int32, (tq, tk), 0)).astype(F32)
        m = (kj * tk + lax.broadcasted_iota(jnp.int32, (tq, tk), 1)).astype(F32)
        ef, eb, _, _ = _decay(n, m, lg_ref[0, h], lg_ref[1, h], t_lat, t_ctx)
        o_ref[...] += _nn((s * (ef + eb)).astype(BF16), v_ref[...])

    return pl.pallas_call(
        body, name=name, grid=(H, t_lat // tq, T // tk),
        in_specs=[pl.BlockSpec(memory_space=pltpu.SMEM),
                  pl.BlockSpec((None, tq, HEAD_DIM), lambda h, i, j: (h, i, 0)),
                  pl.BlockSpec((None, tk, HEAD_DIM), lambda h, i, j: (h, j, 0)),
                  pl.BlockSpec((None, tk, HEAD_DIM), lambda h, i, j: (h, j, 0))],
        out_specs=pl.BlockSpec((None, tq, HEAD_DIM), lambda h, i, j: (h, i, 0)),
        out_shape=jax.ShapeDtypeStruct((H, t_lat, HEAD_DIM), F32),
        compiler_params=_params(("arbitrary", "arbitrary", "arbitrary")),
    )(lg, q, k, v)


def ret_bwd(q, k, v, do, lg, *, t_lat, tq, tk, name):
    H, T, _ = k.shape
    t_ctx = T - t_lat

    def body(lg_ref, q_ref, k_ref, v_ref, do_ref, dq_ref, dk_ref, dv_ref, dlg_ref):
        h, kj, qi = pl.program_id(0), pl.program_id(1), pl.program_id(2)

        @pl.when((kj == 0) & (qi == 0))
        def _():
            dq_ref[...] = jnp.zeros_like(dq_ref)
            dlg_ref[...] = jnp.zeros_like(dlg_ref)

        @pl.when(qi == 0)
        def _():
            dk_ref[...] = jnp.zeros_like(dk_ref)
            dv_ref[...] = jnp.zeros_like(dv_ref)

        qv, kv, vv = q_ref[...], k_ref[...], v_ref[...]
        dob = do_ref[...].astype(BF16)
        st = _nt(kv, qv)
        dwt = _nt(vv, dob)
        m = (kj * tk + lax.broadcasted_iota(jnp.int32, (tk, tq), 0)).astype(F32)
        n = (qi * tq + lax.broadcasted_iota(jnp.int32, (tk, tq), 1)).astype(F32)
        ef, eb, df, db = _decay(n, m, lg_ref[0, h], lg_ref[1, h], t_lat, t_ctx)
        dec = ef + eb
        dv_ref[...] += _nn((st * dec).astype(BF16), dob)
        dst = (dwt * dec).astype(BF16)
        dk_ref[...] += _nn(dst, qv)
        rows = pl.ds(pl.multiple_of(qi * tq, tq), tq)
        dq_ref[rows, :] += _tn(dst, kv)
        gs = dwt * st
        dlg_ref[0:1, :] += jnp.sum(gs * (ef * df))
        dlg_ref[1:2, :] += jnp.sum(gs * (eb * db))

    kspec = pl.BlockSpec((None, tk, HEAD_DIM), lambda h, j, i: (h, j, 0))
    qspec = pl.BlockSpec((None, tq, HEAD_DIM), lambda h, j, i: (h, i, 0))
    return pl.pallas_call(
        body, name=name, grid=(H, T // tk, t_lat // tq),
        in_specs=[pl.BlockSpec(memory_space=pltpu.SMEM), qspec, kspec, kspec, qspec],
        out_specs=[pl.BlockSpec((None, t_lat, HEAD_DIM), lambda h, j, i: (h, 0, 0)), kspec, kspec,
                   pl.BlockSpec((None, 8, LANE), lambda h, j, i: (h, 0, 0))],
        out_shape=[jax.ShapeDtypeStruct((H, t_lat, HEAD_DIM), F32), jax.ShapeDtypeStruct((H, T, HEAD_DIM), F32),
                   jax.ShapeDtypeStruct((H, T, HEAD_DIM), F32), jax.ShapeDtypeStruct((H, 8, LANE), F32)],
        compiler_params=_params(("arbitrary", "arbitrary", "arbitrary")),
    )(lg, q, k, v, do)


def _group_norm(o):
    mu = jnp.mean(o, axis=-1, keepdims=True)
    ctr = o - mu
    r = lax.rsqrt(jnp.mean(ctr * ctr, axis=-1, keepdims=True) + GN_EPS)
    return ctr * r, r


def ret_post_fwd(o, p, *, heads, d_model, name):
    H, t_lat, _ = o.shape
    T = p.shape[0]

    def body(o_ref, g_ref, z_ref):
        on, _ = _group_norm(o_ref[...])
        _, sl = _silu_parts(g_ref[0:t_lat, :])
        z_ref[...] = (on * sl).astype(BF16)

    return pl.pallas_call(
        body, name=name, grid=(H,),
        in_specs=[pl.BlockSpec((None, t_lat, HEAD_DIM), lambda h: (h, 0, 0)),
                  pl.BlockSpec((T, HEAD_DIM), lambda h: (0, 3 * heads + h))],
        out_specs=pl.BlockSpec((t_lat, HEAD_DIM), lambda h: (0, h)),
        out_shape=jax.ShapeDtypeStruct((t_lat, d_model), BF16),
        compiler_params=_params(("arbitrary",)),
    )(o, p)


def _shift_rows(u, k):
    rows = u.shape[0]
    row = lax.broadcasted_iota(jnp.int32, u.shape, 0)
    rolled = pltpu.roll(u, k % rows, axis=0)
    return jnp.where((row >= k) & (row < rows + k), rolled, 0.0)


def conv_fwd(z, p, w_conv, *, heads, t_lat, name):
    T = p.shape[0]
    cb_n = w_conv.shape[1] // LANE
    base = 4 * heads

    def body(z_in, bg_ref, cg_ref, u_ref, w_ref, z_ref):
        cu = cg_ref[0:t_lat, :] * u_ref[0:t_lat, :]
        c3 = _shift_rows(cu, 1) * w_ref[0:1, :] + cu * w_ref[1:2, :] + _shift_rows(cu, -1) * w_ref[2:3, :]
        z_ref[...] = (bg_ref[0:t_lat, :] * c3).astype(BF16)

    def col(part):
        return pl.BlockSpec((T, LANE), lambda cb: (0, base + part * cb_n + cb))

    return pl.pallas_call(
        body, name=name, grid=(cb_n,),
        in_specs=[ANY, col(0), col(1), col(2), pl.BlockSpec((CONV_WIDTH, LANE), lambda cb: (0, cb))],
        out_specs=pl.BlockSpec((t_lat, LANE), lambda cb: (0, heads + cb)),
        out_shape=jax.ShapeDtypeStruct(z.shape, z.dtype),
        input_output_aliases={0: 0},
        compiler_params=_params(("arbitrary",)),
    )(z, p, p, p, w_conv)


def out_proj(z, w_out, x, mod, *, tm, name):
    t_lat, D = z.shape

    def body(z_ref, w_ref, x_ref, mod_ref, xo_ref, y_ref):
        y = _nn(z_ref[...], w_ref[...])
        y_ref[...] = y.astype(BF16)
        xo_ref[...] = x_ref[...] + mod_ref[2:3, :] * y

    tok = pl.BlockSpec((tm, D), lambda i: (i, 0))
    return pl.pallas_call(
        body, name=name, grid=(t_lat // tm,),
        in_specs=[tok, pl.BlockSpec((D, D), lambda i: (0, 0)), tok, pl.BlockSpec((None, 3, D), lambda i: (0, 0, 0))],
        out_specs=[tok, tok],
        out_shape=[jax.ShapeDtypeStruct((t_lat, D), F32), jax.ShapeDtypeStruct((t_lat, D), BF16)],
        compiler_params=_params(("arbitrary",)),
    )(z, w_out, x, mod)


def out_proj_bwd(dxo, y, w_out, mod, *, tm, name):
    t_lat, D = dxo.shape

    def body(dxo_ref, y_ref, w_ref, mod_ref, dz_ref, dy_ref, red_ref):
        @pl.when(pl.program_id(0) == 0)
        def _():
            red_ref[...] = jnp.zeros_like(red_ref)

        dxo_v = dxo_ref[...]
        dyb = (mod_ref[2:3, :] * dxo_v).astype(BF16)
        dy_ref[...] = dyb
        dz_ref[...] = _nt(dyb, w_ref[...])
        red_ref[2:3, :] += jnp.sum(dxo_v * y_ref[...].astype(F32), axis=0, keepdims=True)

    tok = pl.BlockSpec((tm, D), lambda i: (i, 0))
    return pl.pallas_call(
        body, name=name, grid=(t_lat // tm,),
        in_specs=[tok, tok, pl.BlockSpec((D, D), lambda i: (0, 0)), pl.BlockSpec((None, 3, D), lambda i: (0, 0, 0))],
        out_specs=[tok, tok, pl.BlockSpec((8, D), lambda i: (0, 0))],
        out_shape=[jax.ShapeDtypeStruct((t_lat, D), F32), jax.ShapeDtypeStruct((t_lat, D), BF16),
                   jax.ShapeDtypeStruct((8, D), F32)],
        compiler_params=_params(("arbitrary",)),
    )(dxo, y, w_out, mod)


def ret_post_bwd(dz, o, p, *, heads, name):
    H, t_lat, _ = o.shape
    T, in_w = p.shape

    def body(dz_ref, o_ref, g_ref, do_ref, dp_ref):
        on, r = _group_norm(o_ref[...])
        gg = g_ref[0:t_lat, :]
        sig, sl = _silu_parts(gg)
        dret = dz_ref[...]
        don = dret * sl
        do_ref[...] = r * (don - jnp.mean(don, axis=-1, keepdims=True)
                           - on * jnp.mean(don * on, axis=-1, keepdims=True))
        dp_ref[0:t_lat, :] = (dret * on * _dsilu(gg, sig)).astype(BF16)
        dp_ref[t_lat:T, :] = jnp.zeros((T - t_lat, HEAD_DIM), BF16)

    return pl.pallas_call(
        body, name=name, grid=(H,),
        in_specs=[pl.BlockSpec((t_lat, HEAD_DIM), lambda h: (0, h)),
                  pl.BlockSpec((None, t_lat, HEAD_DIM), lambda h: (h, 0, 0)),
                  pl.BlockSpec((T, HEAD_DIM), lambda h: (0, 3 * heads + h))],
        out_specs=[pl.BlockSpec((None, t_lat, HEAD_DIM), lambda h: (h, 0, 0)),
                   pl.BlockSpec((T, HEAD_DIM), lambda h: (0, 3 * heads + h))],
        out_shape=[jax.ShapeDtypeStruct((H, t_lat, HEAD_DIM), F32), jax.ShapeDtypeStruct((T, in_w), BF16)],
        compiler_params=_params(("arbitrary",)),
    )(dz, o, p)


def conv_bwd(dp, dz, p, w_conv, *, heads, t_lat, name):
    T = p.shape[0]
    cb_n = w_conv.shape[1] // LANE
    base = 4 * heads

    def body(dp_in, dz_ref, bg_ref, cg_ref, u_ref, w_ref, dp_ref, dw_ref):
        part = pl.program_id(1)
        cg, u = cg_ref[0:t_lat, :], u_ref[0:t_lat, :]
        cu = cg * u
        dconv = dz_ref[...]
        dp_ref[t_lat:T, :] = jnp.zeros((T - t_lat, LANE), BF16)

        @pl.when(part == 0)
        def _():
            c3 = _shift_rows(cu, 1) * w_ref[0:1, :] + cu * w_ref[1:2, :] + _shift_rows(cu, -1) * w_ref[2:3, :]
            dp_ref[0:t_lat, :] = (dconv * c3).astype(BF16)
            dc3 = dconv * bg_ref[0:t_lat, :]
            dw_ref[0:1, :] = jnp.sum(dc3 * _shift_rows(cu, 1), axis=0, keepdims=True)
            dw_ref[1:2, :] = jnp.sum(dc3 * cu, axis=0, keepdims=True)
            dw_ref[2:3, :] = jnp.sum(dc3 * _shift_rows(cu, -1), axis=0, keepdims=True)

        @pl.when(part > 0)
        def _():
            dc3 = dconv * bg_ref[0:t_lat, :]
            dcu = (_shift_rows(dc3, -1) * w_ref[0:1, :] + dc3 * w_ref[1:2, :] + _shift_rows(dc3, 1) * w_ref[2:3, :])
            dp_ref[0:t_lat, :] = (dcu * jnp.where(part == 1, u, cg)).astype(BF16)

    def col(part):
        return pl.BlockSpec((T, LANE), lambda cb, pt: (0, base + part * cb_n + cb))

    return pl.pallas_call(
        body, name=name, grid=(cb_n, 3),
        in_specs=[ANY, pl.BlockSpec((t_lat, LANE), lambda cb, pt: (0, heads + cb)), col(0), col(1), col(2),
                  pl.BlockSpec((CONV_WIDTH, LANE), lambda cb, pt: (0, cb))],
        out_specs=[pl.BlockSpec((T, LANE), lambda cb, pt: (0, base + pt * cb_n + cb)),
                   pl.BlockSpec((CONV_WIDTH, LANE), lambda cb, pt: (0, cb))],
        out_shape=[jax.ShapeDtypeStruct(dp.shape, dp.dtype), jax.ShapeDtypeStruct(w_conv.shape, F32)],
        input_output_aliases={0: 0},
        compiler_params=_params(("arbitrary", "arbitrary")),
    )(dp, dz, p, p, p, w_conv)


def ret_unprep(dp, dq, dk, dv, cos, sin, *, t_lat, name):
    H, T, _ = dk.shape

    def body(dp_in, dq_ref, dk_ref, dv_ref, c_ref, s_ref, dp_ref):
        part = pl.program_id(0)

        @pl.when(part == 0)
        def _():
            dp_ref[0:t_lat, :] = _rope(dq_ref[...], c_ref[0:t_lat, :], -s_ref[0:t_lat, :]).astype(BF16)
            dp_ref[t_lat:T, :] = jnp.zeros((T - t_lat, HEAD_DIM), BF16)

        @pl.when(part == 1)
        def _():
            dp_ref[...] = (_rope(dk_ref[...], c_ref[...], -s_ref[...]) * K_SCALE).astype(BF16)

        @pl.when(part == 2)
        def _():
            dp_ref[...] = dv_ref[...].astype(BF16)

    def head_of(part):
        return lambda pt, h: (jnp.where(pt == part, h, 0), 0, 0)

    tab = pl.BlockSpec((T, HEAD_DIM), lambda pt, h: (0, 0))
    return pl.pallas_call(
        body, name=name, grid=(3, H),
        in_specs=[ANY, pl.BlockSpec((None, t_lat, HEAD_DIM), head_of(0)), pl.BlockSpec((None, T, HEAD_DIM), head_of(1)),
                  pl.BlockSpec((None, T, HEAD_DIM), head_of(2)), tab, tab],
        out_specs=pl.BlockSpec((T, HEAD_DIM), lambda pt, h: (0, pt * H + h)),
        out_shape=jax.ShapeDtypeStruct(dp.shape, dp.dtype),
        input_output_aliases={0: 0},
        compiler_params=_params(("arbitrary", "arbitrary")),
    )(dp, dq, dk, dv, cos, sin)


def norm_mod_fwd(x, mod, gn, *, tm, name):
    T, D = x.shape

    def body(x_ref, mod_ref, g_ref, h_ref):
        h_ref[...] = _norm_mod(x_ref[...], g_ref[...], mod_ref[0:1, :], mod_ref[1:2, :])

    tok = pl.BlockSpec((tm, D), lambda i: (i, 0))
    return pl.pallas_call(
        body, name=name, grid=(T // tm,),
        in_specs=[tok, pl.BlockSpec((None, 3, D), lambda i: (0, 0, 0)), pl.BlockSpec((1, D), lambda i: (0, 0))],
        out_specs=tok, out_shape=jax.ShapeDtypeStruct((T, D), F32),
        compiler_params=_params(("arbitrary",)),
    )(x, mod, gn)


def norm_mod_bwd(x, dh, dxo, mod, gn, *, tm, name):
    T, D = x.shape

    def body(x_ref, dh_ref, dxo_ref, mod_ref, g_ref, dx_ref, red_ref):
        @pl.when(pl.program_id(0) == 0)
        def _():
            red_ref[...] = jnp.zeros_like(red_ref)

        dxn, dshift, dscale, dg = _norm_mod_bwd(x_ref[...], dh_ref[...], g_ref[...], mod_ref[1:2, :])
        dx_ref[...] = dxo_ref[...] + dxn
        red_ref[0:1, :] += dshift
        red_ref[1:2, :] += dscale
        red_ref[3:4, :] += dg

    tok = pl.BlockSpec((tm, D), lambda i: (i, 0))
    return pl.pallas_call(
        body, name=name, grid=(T // tm,),
        in_specs=[tok, tok, tok, pl.BlockSpec((None, 3, D), lambda i: (0, 0, 0)), pl.BlockSpec((1, D), lambda i: (0, 0))],
        out_specs=[tok, pl.BlockSpec((8, D), lambda i: (0, 0))],
        out_shape=[jax.ShapeDtypeStruct((T, D), F32), jax.ShapeDtypeStruct((8, D), F32)],
        compiler_params=_params(("arbitrary",)),
    )(x, dh, dxo, mod, gn)


def _window_sum(u, w, lead):
    T, C = u.shape
    ext = jnp.concatenate([u, jnp.zeros((POOL_PAD, C), F32)], axis=0)
    k = 1
    while k < w:
        ext = ext + _shift_rows(ext, k)
        k *= 2
    return _shift_rows(ext, -lead)[0:T, :]


def _window_count(T, C, w):
    t = lax.broadcasted_iota(jnp.int32, (T, C), 0)
    lo = jnp.clip(t - w // 2, 0, T)
    hi = jnp.clip(t + (w - w // 2), 0, T)
    return (hi - lo).astype(F32)


def pool_fwd(h, x, pool_w, scale, mod, *, name):
    T, D = h.shape
    G, Cg, _ = pool_w.shape
    ns = Cg // LANE

    def body(h_ref, x_ref, w_ref, sc_ref, mod_ref, xo_ref, pl_ref, yl_ref, acc):
        g, s = pl.program_id(0), pl.program_id(1)
        hv = h_ref[...]
        for gi, win in enumerate(POOL_WINDOWS):
            @pl.when(g == gi)
            def _():
                mean = _window_sum(hv, win, win // 2 - 1) / _window_count(T, LANE, win)
                pooled = (mean - hv).astype(BF16)
                pl_ref[...] = pooled
                contrib = _nn(pooled, w_ref[...])

                @pl.when(s == 0)
                def _():
                    acc[...] = contrib

                @pl.when(s > 0)
                def _():
                    acc[...] += contrib

        @pl.when(s == ns - 1)
        def _():
            yl = acc[...]
            yl_ref[...] = yl.astype(BF16)
            xo_ref[...] = x_ref[...] + mod_ref[2:3, :] * (yl * sc_ref[...])

    grp = pl.BlockSpec((T, Cg), lambda g, s: (0, g))
    sub = pl.BlockSpec((T, LANE), lambda g, s: (0, g * ns + s))
    return pl.pallas_call(
        body, name=name, grid=(G, ns),
        in_specs=[sub, grp, pl.BlockSpec((None, LANE, Cg), lambda g, s: (g, s, 0)),
                  pl.BlockSpec((1, Cg), lambda g, s: (0, g)), pl.BlockSpec((None, 3, Cg), lambda g, s: (0, 0, g))],
        out_specs=[grp, sub, grp],
        out_shape=[jax.ShapeDtypeStruct((T, D), F32), jax.ShapeDtypeStruct((T, D), BF16),
                   jax.ShapeDtypeStruct((T, D), BF16)],
        scratch_shapes=[pltpu.VMEM((T, Cg), F32)],
        compiler_params=_params(("arbitrary", "arbitrary")),
    )(h, x, pool_w, scale, mod)


def pool_bwd(dxo, pooled, yl, pool_w, scale, mod, *, name):
    T, D = dxo.shape
    G, Cg, _ = pool_w.shape
    ns = Cg // LANE

    def body(dxo_ref, pl_ref, yl_ref, w_ref, sc_ref, mod_ref, dh_ref, dw_ref, red_ref, dyl):
        g, s = pl.program_id(0), pl.program_id(1)

        @pl.when(s == 0)
        def _():
            dxo_v = dxo_ref[...]
            ylv = yl_ref[...].astype(F32)
            dy = mod_ref[2:3, :] * dxo_v
            dyl[...] = (dy * sc_ref[...]).astype(BF16)
            red_ref[...] = jnp.zeros_like(red_ref)
            red_ref[2:3, :] = jnp.sum(dxo_v * (ylv * sc_ref[...]), axis=0, keepdims=True)
            red_ref[4:5, :] = jnp.sum(dy * ylv, axis=0, keepdims=True)

        dylv = dyl[...]
        dpool = _nt(dylv, w_ref[...])
        dw_ref[...] = _tn(pl_ref[...], dylv).astype(BF16)
        for gi, win in enumerate(POOL_WINDOWS):
            @pl.when(g == gi)
            def _():
                spread = _window_sum(dpool / _window_count(T, LANE, win), win, win // 2)
                dh_ref[...] = spread - dpool

    grp = pl.BlockSpec((T, Cg), lambda g, s: (0, g))
    sub = pl.BlockSpec((T, LANE), lambda g, s: (0, g * ns + s))
    wsub = pl.BlockSpec((None, LANE, Cg), lambda g, s: (g, s, 0))
    return pl.pallas_call(
        body, name=name, grid=(G, ns),
        in_specs=[grp, sub, grp, wsub, pl.BlockSpec((1, Cg), lambda g, s: (0, g)),
                  pl.BlockSpec((None, 3, Cg), lambda g, s: (0, 0, g))],
        out_specs=[sub, wsub, pl.BlockSpec((8, Cg), lambda g, s: (0, g))],
        out_shape=[jax.ShapeDtypeStruct((T, D), F32), jax.ShapeDtypeStruct((G, Cg, Cg), BF16),
                   jax.ShapeDtypeStruct((8, D), F32)],
        scratch_shapes=[pltpu.VMEM((T, Cg), BF16)],
        compiler_params=_params(("arbitrary", "arbitrary")),
    )(dxo, pooled, yl, pool_w, scale, mod)


def final_loss(x, gn, target, *, tm, name):
    T, D = x.shape

    def body(x_ref, g_ref, t_ref, loss_ref, dx_ref, red_ref):
        @pl.when(pl.program_id(0) == 0)
        def _():
            loss_ref[...] = jnp.zeros_like(loss_ref)
            red_ref[...] = jnp.zeros_like(red_ref)

        xx, g = x_ref[...], g_ref[...]
        r = lax.rsqrt(jnp.mean(xx * xx, axis=-1, keepdims=True) + EPS)
        xhat = xx * r
        err = xhat * g - t_ref[...]
        loss_ref[...] += 0.5 * jnp.sum(jnp.mean(err * err, axis=-1, keepdims=True))
        dy = err / D
        red_ref[0:1, :] += jnp.sum(dy * xhat, axis=0, keepdims=True)
        dxh = dy * g
        dx_ref[...] = r * (dxh - xhat * jnp.mean(dxh * xhat, axis=-1, keepdims=True))

    tok = pl.BlockSpec((tm, D), lambda i: (i, 0))
    return pl.pallas_call(
        body, name=name, grid=(T // tm,),
        in_specs=[tok, pl.BlockSpec((1, D), lambda i: (0, 0)), tok],
        out_specs=[pl.BlockSpec((8, LANE), lambda i: (0, 0)), tok, pl.BlockSpec((8, D), lambda i: (0, 0))],
        out_shape=[jax.ShapeDtypeStruct((8, LANE), F32), jax.ShapeDtypeStruct((T, D), F32),
                   jax.ShapeDtypeStruct((8, D), F32)],
        compiler_params=_params(("arbitrary",)),
    )(x, gn, target)


def _rope_tables(t_lat, t_ctx):
    quarter = HEAD_DIM // 4
    pos = jnp.arange(t_lat)
    inv = ROPE_BASE ** (-jnp.arange(quarter, dtype=F32) / quarter)
    ang = jnp.concatenate([(pos // GRID_W).astype(F32)[:, None] * inv, (pos % GRID_W).astype(F32)[:, None] * inv], axis=-1)
    cos, sin = jnp.cos(ang), jnp.sin(ang)
    cos = jnp.concatenate([jnp.concatenate([cos, cos], axis=-1), jnp.ones((t_ctx, HEAD_DIM), F32)], axis=0)
    sin = jnp.concatenate([jnp.concatenate([-sin, sin], axis=-1), jnp.zeros((t_ctx, HEAD_DIM), F32)], axis=0)
    return cos, sin


def _ffn_grads(h, da, db, s, dy, tag):
    nb, T, cw = da.shape
    D = h.shape[1]
    tk = T
    g_gate = matmul_tn(da, h, nb=nb, m=cw, n=D, a_blocked=True, b_blocked=False, tk=tk, name=f"wgrad_gate_{tag}")
    g_up = matmul_tn(db, h, nb=nb, m=cw, n=D, a_blocked=True, b_blocked=False, tk=tk, name=f"wgrad_up_{tag}")
    g_down = matmul_tn(s, dy, nb=nb, m=cw, n=D, a_blocked=True, b_blocked=False, tk=tk, name=f"wgrad_down_{tag}")
    return [g_gate, g_up, g_down]


def kernel(x, c, ctx, c_ctx, w_mod, b_mod, norm_ffn1, norm_mix, norm_ffn2, ffn1_w_gate, ffn1_w_up, ffn1_w_down, ffn2_w_gate, ffn2_w_up, ffn2_w_down, mix_w_in, mix_w_conv, mix_w_out, ret_decay_fwd, ret_decay_bwd, pool_w, pool_scale, final_norm, loss_target, m_c_ctx, m_w_mod, m_b_mod, m_norm_ffn1, m_norm_mix, m_norm_ffn2, m_ffn1_w_gate, m_ffn1_w_up, m_ffn1_w_down, m_ffn2_w_gate, m_ffn2_w_up, m_ffn2_w_down, m_mix_w_in, m_mix_w_conv, m_mix_w_out, m_ret_decay_fwd, m_ret_decay_bwd, m_pool_w, m_pool_scale, m_final_norm, v_c_ctx, v_w_mod, v_b_mod, v_norm_ffn1, v_norm_mix, v_norm_ffn2, v_ffn1_w_gate, v_ffn1_w_up, v_ffn1_w_down, v_ffn2_w_gate, v_ffn2_w_up, v_ffn2_w_down, v_mix_w_in, v_mix_w_conv, v_mix_w_out, v_ret_decay_fwd, v_ret_decay_bwd, v_pool_w, v_pool_scale, v_final_norm):
    t_lat, D = x.shape[1], x.shape[2]
    t_ctx = ctx.shape[1]
    T = t_lat + t_ctx
    heads = ret_decay_fwd.shape[1]
    mod_w = w_mod.shape[2]
    tm = 256
    tmf = 512 if t_lat % 512 == 0 else 256
    tq = 512 if t_lat % 512 == 0 else 256
    tk = T // 3 if (T % 3 == 0 and (T // 3) % 256 == 0) else 256

    ax, ay, ac = lax.axis_index("x"), lax.axis_index("y"), lax.axis_index("c")
    me = 4 * ax + 2 * ay + ac
    own_block = jnp.reshape(me, (1,)).astype(jnp.int32)
    peer_blocks = jnp.stack([4 * (1 - ax) + 2 * ay + ac, 4 * ax + 2 * (1 - ay) + ac,
                             4 * (1 - ax) + 2 * (1 - ay) + ac]).astype(jnp.int32)

    (c_all,) = all_gather([c], name="gather_cond")
    cond16 = jnp.concatenate([c_all.reshape(N_DEV, D), c_ctx[None, :], jnp.zeros((7, D), F32)], axis=0)
    b_blk = lax.dynamic_slice_in_dim(b_mod, me * mod_w, mod_w, axis=1)[:, None, :]
    m_blk = adaln_fwd(cond16, w_mod, b_blk, name="adaln_fwd")
    m_all, w_conv, pscale = all_gather([m_blk, mix_w_conv[0], pool_scale], name="gather_mod")
    mods = jnp.transpose(m_all, (1, 2, 0, 3)).reshape(2, 16, N_MOD, D)
    mod_x = lax.dynamic_index_in_dim(mods, me, axis=1, keepdims=False)
    mod_c = mods[0, 8]
    w_conv = jnp.transpose(w_conv, (1, 0, 2)).reshape(CONV_WIDTH, -1)
    pscale = pscale.reshape(1, D)

    n_grp, grp_rows, grp_w = pool_w.shape[1:]
    ffn1_w = [(jnp.swapaxes(ffn1_w_gate, 1, 2), "gate"), (jnp.swapaxes(ffn1_w_up, 1, 2), "up"), (ffn1_w_down, "down")]
    ffn2_w = [(jnp.swapaxes(ffn2_w_gate, 1, 2), "gate"), (jnp.swapaxes(ffn2_w_up, 1, 2), "up"), (ffn2_w_down, "down")]
    groups = {
        "01": [(w, 0, f"ffn1_{k}") for w, k in ffn1_w],
        "mix": [(mix_w_in, 0, "w_in"), (mix_w_out, 0, "w_out")],
        "02": [(w, 0, f"ffn2_{k}") for w, k in ffn2_w],
        "11": [(w, 1, f"ffn1_{k}") for w, k in ffn1_w] + [(pool_w.reshape(1, n_grp * grp_rows, grp_w), 0, "pool_w")],
        "12": [(w, 1, f"ffn2_{k}") for w, k in ffn2_w],
    }
    started = {}

    def start(tag, after):
        lands = [cast_place(w, l, own_block, name=f"cast_{nm}_{l}") for w, l, nm in groups[tag]]
        started[tag] = gather_start(lands, after, name=f"gather_start_{tag}")
        return started[tag][-1][0:1, 0:1]

    def finish(tag, after):
        return gather_finish(gather_wait(started[tag], after, name=f"gather_wait_{tag}"), name=f"gather_finish_{tag}")

    lg = jnp.concatenate([jax.nn.log_sigmoid(ret_decay_fwd), jax.nn.log_sigmoid(ret_decay_bwd)], axis=0)
    cos, sin = _rope_tables(t_lat, t_ctx)

    def mod3(l, k, with_ctx=False, tie=None):
        rows = mod_x[l, 3 * k:3 * k + 3][None]
        if with_ctx:
            rows = jnp.concatenate([rows, mod_c[3 * k:3 * k + 3][None]], axis=0)
        return rows if tie is None else rows + tie

    tie = start("01", m_all)
    x0 = jnp.concatenate([x[0], ctx[0]], axis=0) + tie
    wg01, wu01, wd01 = finish("01", x0)
    tie = start("mix", wd01)
    x1, h1, a1, b1, y1 = ffn_fwd(x0, mod3(0, 0, True, tie), norm_ffn1[0:1], wg01, wu01, wd01,
                                 tm=tm, t_first=t_lat, name="ffn_fwd_01")
    w_in, w_out = finish("mix", h1)
    tie = start("02", w_out)
    w_out = w_out.reshape(D, D)
    p, hm = proj_in(x1, mod3(0, 1, True, tie), norm_mix[0:1], w_in, tm=768 if T % 768 == 0 else tm, t_first=t_lat,
                    name="proj_in")
    qr, kr, vr = ret_prep(p, cos, sin, heads=heads, tm=tm, name="ret_prep")
    o = ret_fwd(qr, kr, vr, lg, t_lat=t_lat, tq=tq, tk=tk, name="ret_fwd")
    wg02, wu02, wd02 = finish("02", o)
    tie = start("11", wd02)
    z = ret_post_fwd(o, p, heads=heads, d_model=D, name="ret_post_fwd")
    z = conv_fwd(z, p, w_conv, heads=heads, t_lat=t_lat, name="conv_fwd")
    x2, ym = out_proj(z, w_out, x1, mod3(0, 1, tie=tie), tm=tm, name="out_proj")
    x3, h3, a3, b3, y3 = ffn_fwd(x2, mod3(0, 2, tie=tie), norm_ffn2[0:1], wg02, wu02, wd02, tm=tmf, t_first=t_lat, name="ffn_fwd_02")
    wg11, wu11, wd11, pw = finish("11", h3)
    tie = start("12", wd11)
    pw = jnp.transpose(pw.reshape(N_DEV, n_grp, grp_rows, grp_w), (1, 0, 2, 3)).reshape(n_grp, N_DEV * grp_rows, grp_w)
    x4, h4, a4, b4, y4 = ffn_fwd(x3, mod3(1, 0, tie=tie), norm_ffn1[1:2], wg11, wu11, wd11, tm=tmf, t_first=t_lat, name="ffn_fwd_11")
    hp = norm_mod_fwd(x4, mod3(1, 1), norm_mix[1:2], tm=tm, name="pool_norm_fwd")
    x5, pooled, yl = pool_fwd(hp, x4, pw, pscale, mod3(1, 1), name="pool_fwd")
    wg12, wu12, wd12 = finish("12", yl)
    x6, h6, a6, b6, y6 = ffn_fwd(x5, mod3(1, 2), norm_ffn2[1:2], wg12, wu12, wd12, tm=tmf, t_first=t_lat, name="ffn_fwd_12")
    loss_part, dx6, red_fn = final_loss(x6, final_norm[None, :], loss_target[0], tm=tm, name="final_loss")
    loss = lax.psum(loss_part[0, 0], ("x", "y", "c"))

    reducing = {}

    def reduce_start(tag, grads, after=None):
        res = rs_sibling_exchange(grads, grads[0] if after is None else after, name=f"rs_sibling_{tag}")
        n = len(grads)
        stages, directs = res[:n], res[n:]
        sums = [chip_sum(g, st, peer_blocks, name=f"chip_sum_{tag}_{t}") for t, (g, st) in enumerate(zip(grads, stages))]
        handle = chip_exchange_start(sums, name=f"rs_start_{tag}")
        reducing[tag] = (grads, directs, handle)
        return handle[-1][0:1, 0:1]

    dx5, da6, db6, s6, dy6, red12 = ffn_bwd_tok(x5, dx6, y6, a6, b6, mod3(1, 2), norm_ffn2[1:2], wg12, wu12, wd12,
                                                tm=tmf, t_first=t_lat, name="ffn_bwd_12")
    tie = reduce_start("12", _ffn_grads(h6, da6, db6, s6, dy6, "12"))
    dhp, g_pw, red_pool = pool_bwd(dx5, pooled, yl, pw, pscale, mod3(1, 1, tie=tie), name="pool_bwd")
    dx4, red_pn = norm_mod_bwd(x4, dhp, dx5, mod3(1, 1), norm_mix[1:2], tm=tm, name="pool_norm_bwd")
    dx3, da4, db4, s4, dy4, red11 = ffn_bwd_tok(x3, dx4, y4, a4, b4, mod3(1, 0), norm_ffn1[1:2], wg11, wu11, wd11,
                                                tm=tmf, t_first=t_lat, name="ffn_bwd_11")
    g_pw = jnp.transpose(g_pw.reshape(n_grp, N_DEV, grp_rows, grp_w), (1, 0, 2, 3)).reshape(N_DEV, n_grp * grp_rows, grp_w)
    tie = reduce_start("11", list(_ffn_grads(h4, da4, db4, s4, dy4, "11")) + [g_pw])
    dx2, da3, db3, s3, dy3, red02 = ffn_bwd_tok(x2, dx3, y3, a3, b3, mod3(0, 2, tie=tie), norm_ffn2[0:1], wg02, wu02, wd02,
                                                tm=tmf, t_first=t_lat, name="ffn_bwd_02")
    tie = reduce_start("02", _ffn_grads(h3, da3, db3, s3, dy3, "02"))
    dz, dym, red_op = out_proj_bwd(dx2, ym, w_out, mod3(0, 1, tie=tie), tm=tm, name="out_proj_bwd")
    g_wout = matmul_tn(z, dym, nb=N_DEV, m=D // N_DEV, n=D, a_blocked=False, b_blocked=False,
                       tk=t_lat, name="wgrad_out")
    do, dp = ret_post_bwd(dz, o, p, heads=heads, name="ret_post_bwd")
    dp, g_conv = conv_bwd(dp, dz, p, w_conv, heads=heads, t_lat=t_lat, name="conv_bwd")
    dq, dk, dv, dlg = ret_bwd(qr, kr, vr, do, lg, t_lat=t_lat, tq=tq, tk=tk, name="ret_bwd")
    dp = ret_unprep(dp, dq, dk, dv, cos, sin, t_lat=t_lat, name="ret_unprep")
    dx1, red_mix = proj_bwd_tok(x1, dx2, dp, mod3(0, 1, True), norm_mix[0:1], w_in, tm=tm, name="proj_bwd")
    g_win = matmul_tn(hm, dp, nb=N_DEV, m=D, n=w_in.shape[2], a_blocked=False, b_blocked=False, tk=T, name="wgrad_in")
    tie = reduce_start("mix", [g_win, g_wout])
    dx0, da1, db1, s1, dy1, red01 = ffn_bwd_tok(x0, dx1, y1, a1, b1, mod3(0, 0, True, tie), norm_ffn1[0:1], wg01, wu01, wd01,
                                                tm=tm, t_first=t_lat, name="ffn_bwd_01")
    res = {}

    dm_x = jnp.stack([jnp.concatenate([red01[0, 0:3], red_mix[0, 0:2], red_op[2:3], red02[0, 0:3]], axis=0),
                      jnp.concatenate([red11[0, 0:3], red_pn[0:2], red_pool[2:3], red12[0, 0:3]], axis=0)])
    dm_c = jnp.concatenate([red01[1, 0:3], red_mix[1, 0:2], jnp.zeros((4, D), F32)], axis=0)
    d_lg = dlg[:, 0:2, 0].T
    d_dec_f = d_lg[0:1] * jax.nn.sigmoid(-ret_decay_fwd)
    d_dec_b = d_lg[1:2] * jax.nn.sigmoid(-ret_decay_bwd)
    pieces = [dm_x.reshape(-1), dm_c.reshape(-1),
              jnp.stack([red01[0, 3] + red01[1, 3], red11[0, 3]]).reshape(-1),
              jnp.stack([red_mix[0, 3] + red_mix[1, 3], red_pn[3]]).reshape(-1),
              jnp.stack([red02[0, 3], red12[0, 3]]).reshape(-1),
              red_fn[0], d_dec_f.reshape(-1), d_dec_b.reshape(-1), g_conv.reshape(-1), red_pool[4]]
    sizes = [int(a.shape[0]) for a in pieces]
    n_pack = sum(sizes)
    n_pad = -n_pack % 8192
    packed = jnp.concatenate(pieces + [jnp.zeros((n_pad,), F32)])[None, :]
    (packed_all,) = all_gather([packed], name="gather_partials")
    total = sum_partials(packed_all, name="sum_partials")[0]
    offs = [0]
    for s in sizes:
        offs.append(offs[-1] + s)
    seg = [total[offs[i]:offs[i + 1]] for i in range(len(sizes))]
    g_dm = seg[0].reshape(2, N_MOD * D)
    g_dmc = seg[1].reshape(N_MOD * D)
    g_b_mod = g_dm.at[0].add(g_dmc)
    g_norm_ffn1, g_norm_mix, g_norm_ffn2 = (seg[k].reshape(2, D) for k in (2, 3, 4))
    g_final = seg[5]
    g_dec_f, g_dec_b = seg[6].reshape(1, heads), seg[7].reshape(1, heads)
    g_conv_all = seg[8].reshape(CONV_WIDTH, -1)
    g_pscale_all = seg[9]
    conv_w = mix_w_conv.shape[2]
    g_w_conv = lax.dynamic_slice_in_dim(g_conv_all, me * conv_w, conv_w, axis=1)[None]
    ps_w = pool_scale.shape[1]
    g_pool_scale = lax.dynamic_slice_in_dim(g_pscale_all, me * ps_w, ps_w, axis=0)[None]

    dm_rows = packed_all[:, 0, 0:offs[1]].reshape(N_DEV, 2, N_MOD * D)
    dmc_rows = packed_all[:, 0, offs[1]:offs[2]].reshape(N_DEV, N_MOD * D)
    dm_x_blk = jnp.transpose(lax.dynamic_slice_in_dim(dm_rows, me * mod_w, mod_w, axis=2), (1, 0, 2))
    dm_c_blk = jnp.stack([lax.dynamic_slice_in_dim(dmc_rows, me * mod_w, mod_w, axis=1),
                          jnp.zeros((N_DEV, mod_w), F32)])
    g_w_mod, d_w_mod, nm_w_mod, nv_w_mod, cctx_part = adaln_bwd(cond16, dm_x_blk, dm_c_blk, w_mod, m_w_mod, v_w_mod,
                                                                name="adaln_bwd")
    cpad = jnp.concatenate([cctx_part[0], jnp.zeros((8192 - D,), F32)])[None, :] if D < 8192 else cctx_part[0:1]
    (cctx_all,) = all_gather([cpad], name="gather_cctx")
    g_c_ctx = sum_partials(cctx_all, name="sum_cctx")[0, :D]

    small = [("c_ctx", g_c_ctx, c_ctx, m_c_ctx, v_c_ctx), ("b_mod", g_b_mod, b_mod, m_b_mod, v_b_mod),
             ("norm_ffn1", g_norm_ffn1, norm_ffn1, m_norm_ffn1, v_norm_ffn1),
             ("norm_mix", g_norm_mix, norm_mix, m_norm_mix, v_norm_mix),
             ("norm_ffn2", g_norm_ffn2, norm_ffn2, m_norm_ffn2, v_norm_ffn2),
             ("mix_w_conv", g_w_conv, mix_w_conv, m_mix_w_conv, v_mix_w_conv),
             ("ret_decay_fwd", g_dec_f, ret_decay_fwd, m_ret_decay_fwd, v_ret_decay_fwd),
             ("ret_decay_bwd", g_dec_b, ret_decay_bwd, m_ret_decay_bwd, v_ret_decay_bwd),
             ("pool_scale", g_pool_scale, pool_scale, m_pool_scale, v_pool_scale),
             ("final_norm", g_final, final_norm, m_final_norm, v_final_norm)]
    ssz = [int(a[1].size) for a in small]
    spad = -sum(ssz) % LANE

    def pack(k):
        return jnp.concatenate([a[k].reshape(-1) for a in small] + [jnp.ones((spad,), F32)])[None, :]

    sd, sm, sv = adam_plain(pack(1), pack(2), pack(3), pack(4), name="adam_small")
    soff = [0]
    for s in ssz:
        soff.append(soff[-1] + s)
    for i, (nm, g, w, _, _) in enumerate(small):
        res[nm] = [g.reshape(w.shape)] + [a[0, soff[i]:soff[i + 1]].reshape(w.shape) for a in (sd, sm, sv)]
    res["w_mod"] = [g_w_mod, d_w_mod, nm_w_mod, nv_w_mod]

    after = sd
    for kind, lhs, rhs in (("gate", da1, h1), ("up", db1, h1), ("down", s1, dy1)):
        g = matmul_tn(lhs, rhs, nb=N_DEV, m=lhs.shape[2], n=D, a_blocked=True, b_blocked=False, tk=T,
                      name=f"wgrad_{kind}_01", after=after)
        tie = reduce_start(f"01_{kind}", [g], g)
        after = reducing[f"01_{kind}"][2][2]
    grad_x = dx0[:t_lat][None] + tie

    last = [grad_x]

    def reduce_finish(tag):
        grads, directs, handle = reducing[tag]
        landed = chip_exchange_wait(handle, last[0], name=f"rs_wait_{tag}")
        return list(zip(grads, directs, landed))

    def big(w, m, v, l, part, prev, nm, transposed=False):
        if transposed:
            w, m, v = (jnp.swapaxes(a, 1, 2) for a in (w, m, v))
        outs = adam_reduced(own_block, part[0], part[1], part[2], w, m, v, l, prev, name=nm)
        last[0] = outs[0]
        return outs

    ffn1 = [(ffn1_w_gate, m_ffn1_w_gate, v_ffn1_w_gate), (ffn1_w_up, m_ffn1_w_up, v_ffn1_w_up),
            (ffn1_w_down, m_ffn1_w_down, v_ffn1_w_down)]
    ffn2 = [(ffn2_w_gate, m_ffn2_w_gate, v_ffn2_w_gate), (ffn2_w_up, m_ffn2_w_up, v_ffn2_w_up),
            (ffn2_w_down, m_ffn2_w_down, v_ffn2_w_down)]
    kinds = ["gate", "up", "down"]
    half = {}
    parts = reduce_finish("12")
    for t, (w, m, v) in enumerate(ffn2):
        half[f"ffn2_w_{kinds[t]}"] = big(w, m, v, 1, parts[t], None, f"adam_ffn2_w_{kinds[t]}_1", transposed=t < 2)
    parts = reduce_finish("11")
    for t, (w, m, v) in enumerate(ffn1):
        half[f"ffn1_w_{kinds[t]}"] = big(w, m, v, 1, parts[t], None, f"adam_ffn1_w_{kinds[t]}_1", transposed=t < 2)
    res["pool_w"] = [a.reshape(pool_w.shape) for a in
                     big(pool_w.reshape(1, n_grp * grp_rows, grp_w), m_pool_w.reshape(1, n_grp * grp_rows, grp_w),
                         v_pool_w.reshape(1, n_grp * grp_rows, grp_w), 0, parts[3], None, "adam_pool_w")]
    parts = reduce_finish("02")
    for t, (w, m, v) in enumerate(ffn2):
        nm = f"ffn2_w_{kinds[t]}"
        outs = big(w, m, v, 0, parts[t], half[nm], f"adam_{nm}_0", transposed=t < 2)
        res[nm] = [jnp.swapaxes(a, 1, 2) for a in outs] if t < 2 else outs
    parts = reduce_finish("mix")
    res["mix_w_in"] = big(mix_w_in, m_mix_w_in, v_mix_w_in, 0, parts[0], None, "adam_mix_w_in")
    res["mix_w_out"] = big(mix_w_out, m_mix_w_out, v_mix_w_out, 0, parts[1], None, "adam_mix_w_out")
    for t, (w, m, v) in enumerate(ffn1):
        nm = f"ffn1_w_{kinds[t]}"
        outs = big(w, m, v, 0, reduce_finish(f"01_{kinds[t]}")[0], half[nm], f"adam_{nm}_0", transposed=t < 2)
        res[nm] = [jnp.swapaxes(a, 1, 2) for a in outs] if t < 2 else outs

    order = ["c_ctx", "w_mod", "b_mod", "norm_ffn1", "norm_mix", "norm_ffn2", "ffn1_w_gate", "ffn1_w_up", "ffn1_w_down",
             "ffn2_w_gate", "ffn2_w_up", "ffn2_w_down", "mix_w_in", "mix_w_conv", "mix_w_out", "ret_decay_fwd",
             "ret_decay_bwd", "pool_w", "pool_scale", "final_norm"]
    return (loss, grad_x, *[res[n][0] for n in order], *[res[n][1] for n in order],
            *[res[n][2] for n in order], *[res[n][3] for n in order])
```

```python
import jax
import jax.numpy as jnp
from jax import lax
from jax.experimental import pallas as pl
from jax.experimental.pallas import tpu as pltpu

F32 = jnp.float32
BF16 = jnp.bfloat16
MESH = pl.DeviceIdType.MESH

N_DEV = 8
N_MOD = 9
EPS = 1e-6
GN_EPS = 1e-5
MACARON = 0.5
HEAD_DIM = 128
K_SCALE = HEAD_DIM ** -0.5
ROPE_BASE = 10000.0
GRID_W = 64
CONV_WIDTH = 3
POOL_WINDOWS = (2, 4, 8, 16)
POOL_PAD = 16

ADAM_LR = 0.001
ADAM_B1 = 0.9
ADAM_B2 = 0.999
ADAM_EPS = 1e-08
ADAM_WD = 0.01
ADAM_STEP = 10

LANE = 128
ROW_CHUNK = 128
MM_ROWS = 256
VMEM_LIMIT = 56 * 1024 * 1024
ANY = pl.BlockSpec(memory_space=pl.ANY)


def _params(sem=None):
    kw = dict(vmem_limit_bytes=VMEM_LIMIT)
    if sem is not None:
        kw["dimension_semantics"] = sem
    return pltpu.CompilerParams(**kw)


def _nt(a, b):
    return lax.dot_general(a, b, (((1,), (1,)), ((), ())), preferred_element_type=F32)


def _tn(a, b):
    return lax.dot_general(a, b, (((0,), (0,)), ((), ())), preferred_element_type=F32)


def _nn(a, b):
    return jnp.dot(a, b, preferred_element_type=F32)


def _silu_parts(a):
    sig = jax.nn.sigmoid(a)
    return sig, a * sig


def _dsilu(a, sig):
    return sig * (1.0 + a * (1.0 - sig))


def _norm_mod(x, g, shift, scale):
    r = lax.rsqrt(jnp.mean(x * x, axis=-1, keepdims=True) + EPS)
    return (x * r * g) * (1.0 + scale) + shift


def _norm_mod_bwd(x, dh, g, scale):
    r = lax.rsqrt(jnp.mean(x * x, axis=-1, keepdims=True) + EPS)
    xhat = x * r
    dn = dh * (1.0 + scale)
    dshift = jnp.sum(dh, axis=0, keepdims=True)
    dscale = jnp.sum(dh * (xhat * g), axis=0, keepdims=True)
    dg = jnp.sum(dn * xhat, axis=0, keepdims=True)
    dxh = dn * g
    dx = r * (dxh - xhat * jnp.mean(dxh * xhat, axis=-1, keepdims=True))
    return dx, dshift, dscale, dg


def _mod_row(mod_ref, k, row0, nrows, t_first):
    if mod_ref.shape[0] == 1:
        return mod_ref[0, k:k + 1, :]
    row = row0 + lax.broadcasted_iota(jnp.int32, (nrows, 1), 0)
    return jnp.where(row >= t_first, mod_ref[1, k:k + 1, :], mod_ref[0, k:k + 1, :])


def _group_index(i, tiles_first, n_groups):
    if n_groups == 1:
        return 0
    return jnp.where(i >= tiles_first, 1, 0)


def _mesh_pos():
    x, y, c = lax.axis_index("x"), lax.axis_index("y"), lax.axis_index("c")
    chips = [(1 - x, y), (x, 1 - y), (1 - x, 1 - y)]
    return x, y, c, chips


def _flat(px, py, pc):
    return 4 * px + 2 * py + pc


def all_gather(shards, name):
    n = len(shards)

    def body(*refs):
        ins, outs = refs[:n], refs[n:2 * n]
        send_sems, recv_sems, local_sems = refs[2 * n:]
        x, y, c, chips = _mesh_pos()
        me, sibling = (x, y, c), (x, y, 1 - c)

        def copy(t, k, block, to, src=None):
            dst = outs[t].at[_flat(*block)]
            return pltpu.make_async_remote_copy(
                src_ref=dst if src is None else src, dst_ref=dst,
                send_sem=send_sems.at[t, k], recv_sem=recv_sems.at[t, k],
                device_id=to, device_id_type=MESH)

        mine = [pltpu.make_async_copy(ins[t], outs[t].at[_flat(*me)], local_sems.at[t]) for t in range(n)]
        for cp in mine:
            cp.start()
        first = []
        for t in range(n):
            first.append(copy(t, 0, me, sibling, src=ins[t]))
            first += [copy(t, 1 + j, me, (*chip, c), src=ins[t]) for j, chip in enumerate(chips)]
        for cp in first:
            cp.start()
        passed = []
        for t in range(n):
            for j, chip in enumerate(chips):
                copy(t, 1 + j, (*chip, c), me).wait_recv()
                fwd = copy(t, 4 + j, (*chip, c), sibling)
                fwd.start()
                passed.append(fwd)
        for t in range(n):
            copy(t, 0, sibling, me).wait_recv()
            for j, chip in enumerate(chips):
                copy(t, 4 + j, (*chip, 1 - c), me).wait_recv()
        for cp in first + passed:
            cp.wait_send()
        for cp in mine:
            cp.wait()

    return pl.pallas_call(
        body, name=name,
        out_shape=[jax.ShapeDtypeStruct((N_DEV,) + s.shape, s.dtype) for s in shards],
        in_specs=[ANY] * n, out_specs=[ANY] * n,
        scratch_shapes=[pltpu.SemaphoreType.DMA((n, 7)), pltpu.SemaphoreType.DMA((n, 7)), pltpu.SemaphoreType.DMA((n,))],
    )(*shards)


HBM_SPEC = pl.BlockSpec(memory_space=pltpu.HBM)
SEM_SPEC = pl.BlockSpec(memory_space=pltpu.SEMAPHORE)
DATAFLOW = pltpu.SideEffectType.DATAFLOW_SIDE_EFFECTING


def _in_hbm(a):
    return pltpu.with_memory_space_constraint(a, pltpu.HBM)


def _push_peers():
    x, y, c, chips = _mesh_pos()
    return [(*chip, c) for chip in chips] + [(x, y, 1 - c)]


def cast_place(w, layer, own_block, name):
    _, R, C = w.shape
    tr = _row_tile(R, C)

    def body(idx_ref, w_ref, o_ref):
        o_ref[...] = w_ref[...].astype(BF16)

    return pl.pallas_call(
        body, name=name,
        grid_spec=pltpu.PrefetchScalarGridSpec(
            num_scalar_prefetch=1, grid=(R // tr,),
            in_specs=[pl.BlockSpec((None, tr, C), lambda i, idx: (layer, i, 0))],
            out_specs=pl.BlockSpec((None, tr, C), lambda i, idx: (idx[0], i, 0))),
        out_shape=jax.ShapeDtypeStruct((N_DEV, R, C), BF16),
        compiler_params=_params(("arbitrary",)),
    )(own_block, w)


def gather_start(lands, after, name):
    n = len(lands)

    def body(*refs):
        lz = refs[:n]
        send_sems, recv_sems = refs[n + 1], refs[n + 2]
        token = refs[-1]
        x, y, c, _ = _mesh_pos()
        for t in range(n):
            mine = lz[t].at[_flat(x, y, c)]
            for k, peer in enumerate(_push_peers()):
                pltpu.make_async_remote_copy(
                    src_ref=mine, dst_ref=mine, send_sem=send_sems.at[4 * t + k],
                    recv_sem=recv_sems.at[4 * t + k], device_id=peer, device_id_type=MESH).start()
        token[...] = jnp.zeros_like(token)

    return pl.pallas_call(
        body, name=name,
        out_shape=(pltpu.SemaphoreType.DMA((4 * n,)), pltpu.SemaphoreType.DMA((4 * n,)),
                   *[pltpu.HBM(l.shape, l.dtype) for l in lands], jax.ShapeDtypeStruct((8, LANE), F32)),
        in_specs=[HBM_SPEC] * n + [ANY],
        out_specs=(SEM_SPEC, SEM_SPEC, *[HBM_SPEC] * n, pl.BlockSpec(memory_space=pltpu.VMEM)),
        input_output_aliases={i: 2 + i for i in range(n)},
        compiler_params=pltpu.CompilerParams(has_side_effects=DATAFLOW),
    )(*[_in_hbm(l) for l in lands], after)


def gather_wait(handle, after, name):
    send, recv = handle[0], handle[1]
    bufs = handle[2:-1]
    n = len(bufs)

    def body(*refs):
        lz = refs[:n]
        send_sems, recv_sems = refs[n], refs[n + 1]
        x, y, c, _ = _mesh_pos()
        for t in range(n):
            for k, peer in enumerate(_push_peers()):
                cp = pltpu.make_async_remote_copy(
                    src_ref=lz[t].at[_flat(x, y, c)], dst_ref=lz[t].at[_flat(*peer)], send_sem=send_sems.at[4 * t + k],
                    recv_sem=recv_sems.at[4 * t + k], device_id=peer, device_id_type=MESH)
                cp.wait_send()
                cp.wait_recv()

    outs = pl.pallas_call(
        body, name=name,
        out_shape=tuple(pltpu.HBM(b.shape, b.dtype) for b in bufs),
        in_specs=[HBM_SPEC] * n + [SEM_SPEC, SEM_SPEC, ANY],
        out_specs=tuple([HBM_SPEC] * n),
        input_output_aliases={i: i for i in range(n)},
        compiler_params=pltpu.CompilerParams(has_side_effects=DATAFLOW),
    )(*bufs, send, recv, after)
    return list(outs)


def gather_finish(lands, name):
    n = len(lands)

    def body(*refs):
        outs = refs[n:2 * n]
        send_sems, recv_sems = refs[2 * n:]
        x, y, c, chips = _mesh_pos()
        sibling = (x, y, 1 - c)

        def copy(t, j, core):
            blk = outs[t].at[_flat(*chips[j], core)]
            return pltpu.make_async_remote_copy(
                src_ref=blk, dst_ref=blk, send_sem=send_sems.at[t, j], recv_sem=recv_sems.at[t, j],
                device_id=sibling, device_id_type=MESH)

        sends = [copy(t, j, c) for t in range(n) for j in range(3)]
        for cp in sends:
            cp.start()
        for t in range(n):
            for j in range(3):
                copy(t, j, 1 - c).wait_recv()
        for cp in sends:
            cp.wait_send()

    return pl.pallas_call(
        body, name=name,
        out_shape=[jax.ShapeDtypeStruct(l.shape, l.dtype) for l in lands],
        in_specs=[ANY] * n, out_specs=[ANY] * n,
        input_output_aliases={t: t for t in range(n)},
        scratch_shapes=[pltpu.SemaphoreType.DMA((n, 3)), pltpu.SemaphoreType.DMA((n, 3))],
    )(*lands)


def chip_exchange_start(sums, name):
    n = len(sums)
    lands = [lax.empty(s.shape, s.dtype) for s in sums]

    def body(*refs):
        ins, lz = refs[:n], refs[n:2 * n]
        send_sems, recv_sems = refs[2 * n], refs[2 * n + 1]
        token = refs[-1]
        peers = _push_peers()
        for t in range(n):
            for j in range(3):
                pltpu.make_async_remote_copy(
                    src_ref=ins[t].at[j], dst_ref=lz[t].at[j], send_sem=send_sems.at[3 * t + j],
                    recv_sem=recv_sems.at[3 * t + j], device_id=peers[j], device_id_type=MESH).start()
        token[...] = jnp.zeros_like(token)

    return pl.pallas_call(
        body, name=name,
        out_shape=(pltpu.SemaphoreType.DMA((3 * n,)), pltpu.SemaphoreType.DMA((3 * n,)),
                   *[pltpu.HBM(s.shape, s.dtype) for s in sums], *[pltpu.HBM(s.shape, s.dtype) for s in sums],
                   jax.ShapeDtypeStruct((8, LANE), F32)),
        in_specs=[HBM_SPEC] * (2 * n),
        out_specs=(SEM_SPEC, SEM_SPEC, *[HBM_SPEC] * (2 * n), pl.BlockSpec(memory_space=pltpu.VMEM)),
        input_output_aliases={i: 2 + i for i in range(2 * n)},
        compiler_params=pltpu.CompilerParams(has_side_effects=DATAFLOW),
    )(*[_in_hbm(s) for s in sums], *[_in_hbm(l) for l in lands])


def chip_exchange_wait(handle, after, name):
    send, recv = handle[0], handle[1]
    n = (len(handle) - 3) // 2
    bufs = handle[2:2 + 2 * n]

    def body(*refs):
        ins, lz = refs[:n], refs[n:2 * n]
        send_sems, recv_sems = refs[2 * n], refs[2 * n + 1]
        peers = _push_peers()
        for t in range(n):
            for j in range(3):
                cp = pltpu.make_async_remote_copy(
                    src_ref=ins[t].at[j], dst_ref=lz[t].at[j], send_sem=send_sems.at[3 * t + j],
                    recv_sem=recv_sems.at[3 * t + j], device_id=peers[j], device_id_type=MESH)
                cp.wait_send()
                cp.wait_recv()

    outs = pl.pallas_call(
        body, name=name,
        out_shape=tuple(pltpu.HBM(b.shape, b.dtype) for b in bufs),
        in_specs=[HBM_SPEC] * (2 * n) + [SEM_SPEC, SEM_SPEC, ANY],
        out_specs=tuple([HBM_SPEC] * (2 * n)),
        input_output_aliases={i: i for i in range(2 * n)},
        compiler_params=pltpu.CompilerParams(has_side_effects=DATAFLOW),
    )(*bufs, send, recv, after)
    return list(outs[n:])


def _sibling_copy(grads, stages, directs, send_sems, recv_sems, t, k):
    x, y, c, chips = _mesh_pos()
    if k < 3:
        src, dst = grads[t].at[_flat(*chips[k], 1 - c)], stages[t].at[k]
    else:
        src, dst = grads[t].at[_flat(x, y, 1 - c)], directs[t]
    return pltpu.make_async_remote_copy(
        src_ref=src, dst_ref=dst, send_sem=send_sems.at[4 * t + k], recv_sem=recv_sems.at[4 * t + k],
        device_id=(x, y, 1 - c), device_id_type=MESH)


def sibling_exchange_start(grads, after, name):
    n = len(grads)
    stages = [lax.empty((3,) + g.shape[1:], g.dtype) for g in grads]
    directs = [lax.empty(g.shape[1:], g.dtype) for g in grads]

    def body(*refs):
        send_sems, recv_sems = refs[3 * n + 1], refs[3 * n + 2]
        for t in range(n):
            for k in range(4):
                _sibling_copy(refs[:n], refs[n:2 * n], refs[2 * n:3 * n], send_sems, recv_sems, t, k).start()
        refs[-1][...] = jnp.zeros_like(refs[-1])

    bufs = list(grads) + stages + directs
    return pl.pallas_call(
        body, name=name,
        out_shape=(pltpu.SemaphoreType.DMA((4 * n,)), pltpu.SemaphoreType.DMA((4 * n,)),
                   *[pltpu.HBM(b.shape, b.dtype) for b in bufs], jax.ShapeDtypeStruct((8, LANE), F32)),
        in_specs=[HBM_SPEC] * (3 * n) + [ANY],
        out_specs=(SEM_SPEC, SEM_SPEC, *[HBM_SPEC] * (3 * n), pl.BlockSpec(memory_space=pltpu.VMEM)),
        input_output_aliases={i: 2 + i for i in range(3 * n)},
        compiler_params=pltpu.CompilerParams(has_side_effects=DATAFLOW),
    )(*[_in_hbm(b) for b in bufs], after)


def sibling_exchange_wait(handle, after, name):
    send, recv = handle[0], handle[1]
    bufs = handle[2:-1]
    n = len(bufs) // 3

    def body(*refs):
        send_sems, recv_sems = refs[3 * n], refs[3 * n + 1]
        for t in range(n):
            for k in range(4):
                cp = _sibling_copy(refs[:n], refs[n:2 * n], refs[2 * n:3 * n], send_sems, recv_sems, t, k)
                cp.wait_send()
                cp.wait_recv()

    outs = pl.pallas_call(
        body, name=name,
        out_shape=tuple(pltpu.HBM(b.shape, b.dtype) for b in bufs),
        in_specs=[HBM_SPEC] * (3 * n) + [SEM_SPEC, SEM_SPEC, ANY],
        out_specs=tuple([HBM_SPEC] * (3 * n)),
        input_output_aliases={i: i for i in range(3 * n)},
        compiler_params=pltpu.CompilerParams(has_side_effects=DATAFLOW),
    )(*bufs, send, recv, after)
    return list(outs[:n]), list(outs[n:2 * n]), list(outs[2 * n:])


def rs_sibling_exchange(grads, after, name):
    n = len(grads)

    def body(*refs):
        ins, stages, directs = refs[:n], refs[n + 1:2 * n + 1], refs[2 * n + 1:3 * n + 1]
        send_sems, recv_sems = refs[3 * n + 1:]
        x, y, c, chips = _mesh_pos()
        sibling = (x, y, 1 - c)

        def copy(t, k):
            if k < 3:
                src, dst = ins[t].at[_flat(*chips[k], 1 - c)], stages[t].at[k]
            else:
                src, dst = ins[t].at[_flat(x, y, 1 - c)], directs[t]
            return pltpu.make_async_remote_copy(
                src_ref=src, dst_ref=dst, send_sem=send_sems.at[t, k], recv_sem=recv_sems.at[t, k],
                device_id=sibling, device_id_type=MESH)

        cps = [copy(t, k) for t in range(n) for k in range(4)]
        for cp in cps:
            cp.start()
        for cp in cps:
            cp.wait_recv()
        for cp in cps:
            cp.wait_send()

    return pl.pallas_call(
        body, name=name,
        out_shape=[jax.ShapeDtypeStruct((3,) + g.shape[1:], g.dtype) for g in grads]
        + [jax.ShapeDtypeStruct(g.shape[1:], g.dtype) for g in grads],
        in_specs=[ANY] * (n + 1), out_specs=[ANY] * (2 * n),
        scratch_shapes=[pltpu.SemaphoreType.DMA((n, 4)), pltpu.SemaphoreType.DMA((n, 4))],
    )(*grads, after)


def _row_tile(rows, cols, limit_bytes=3 << 19):
    best = None
    for t in range(16, rows + 1, 16):
        if rows % t == 0 and t * cols * 4 <= limit_bytes:
            best = t
    return best if best is not None else rows


def chip_sum(grad, stage, peer_blocks, name):
    _, R, C = grad.shape
    tr = _row_tile(R, C)

    def body(idx_ref, g_ref, s_ref, o_ref):
        o_ref[...] = (g_ref[...].astype(F32) + s_ref[...].astype(F32)).astype(BF16)

    return pl.pallas_call(
        body, name=name,
        grid_spec=pltpu.PrefetchScalarGridSpec(
            num_scalar_prefetch=1, grid=(3, R // tr),
            in_specs=[pl.BlockSpec((None, tr, C), lambda j, i, idx: (idx[j], i, 0)),
                      pl.BlockSpec((None, tr, C), lambda j, i, idx: (j, i, 0))],
            out_specs=pl.BlockSpec((None, tr, C), lambda j, i, idx: (j, i, 0))),
        out_shape=jax.ShapeDtypeStruct((3, R, C), BF16),
        compiler_params=_params(("arbitrary", "arbitrary")),
    )(peer_blocks, grad, stage)


def _adamw(w, g, m, v):
    m2 = ADAM_B1 * m + (1.0 - ADAM_B1) * g
    v2 = ADAM_B2 * v + (1.0 - ADAM_B2) * (g * g)
    m_hat = m2 / (1.0 - ADAM_B1 ** ADAM_STEP)
    v_hat = v2 / (1.0 - ADAM_B2 ** ADAM_STEP)
    delta = -ADAM_LR * (m_hat / (jnp.sqrt(v_hat) + ADAM_EPS) + ADAM_WD * w)
    return delta, m2, v2


def adam_reduced(own_block, grad, direct, landed, w, m, v, layer, prev, name):
    L, R, C = w.shape
    tr = _row_tile(R, C, 1 << 20)
    first = prev is None

    def body(idx_ref, g_ref, d_ref, l_ref, w_ref, m_ref, v_ref, *rest):
        og, od, om, ov = rest[-4:]
        g = g_ref[...].astype(F32) + d_ref[...].astype(F32)
        for j in range(3):
            g = g + l_ref[j].astype(F32)
        delta, m2, v2 = _adamw(w_ref[...], g, m_ref[...], v_ref[...])
        og[...] = g
        od[...] = delta
        om[...] = m2
        ov[...] = v2

    lay = pl.BlockSpec((None, tr, C), lambda i, idx: (layer, i, 0))
    in_specs = [pl.BlockSpec((None, tr, C), lambda i, idx: (idx[0], i, 0)),
                pl.BlockSpec((tr, C), lambda i, idx: (i, 0)),
                pl.BlockSpec((3, tr, C), lambda i, idx: (0, i, 0)),
                lay, lay, lay]
    args = [own_block, grad, direct, landed, w, m, v]
    aliases = {}
    if not first:
        in_specs += [ANY] * 4
        args += list(prev)
        aliases = {7 + k: k for k in range(4)}
    return pl.pallas_call(
        body, name=name,
        grid_spec=pltpu.PrefetchScalarGridSpec(
            num_scalar_prefetch=1, grid=(R // tr,), in_specs=in_specs, out_specs=[lay] * 4),
        out_shape=[jax.ShapeDtypeStruct((L, R, C), F32)] * 4,
        input_output_aliases=aliases,
        compiler_params=_params(("arbitrary",)),
    )(*args)


def sum_partials(parts, name):
    _, _, N = parts.shape
    tn = 8192

    def body(p_ref, o_ref):
        g = p_ref[0]
        for k in range(1, N_DEV):
            g = g + p_ref[k]
        o_ref[...] = g

    return pl.pallas_call(
        body, name=name, grid=(N // tn,),
        in_specs=[pl.BlockSpec((N_DEV, 1, tn), lambda i: (0, 0, i))],
        out_specs=pl.BlockSpec((1, tn), lambda i: (0, i)),
        out_shape=jax.ShapeDtypeStruct((1, N), F32),
    )(parts)


def adam_plain(g, w, m, v, name):
    _, N = g.shape

    def body(g_ref, w_ref, m_ref, v_ref, od, om, ov):
        delta, m2, v2 = _adamw(w_ref[...], g_ref[...], m_ref[...], v_ref[...])
        od[...] = delta
        om[...] = m2
        ov[...] = v2

    return pl.pallas_call(
        body, name=name, out_shape=[jax.ShapeDtypeStruct((1, N), F32)] * 3,
    )(g, w, m, v)


def adaln_fwd(cond16, w_mod, b_blk, name):
    L, D, W = w_mod.shape
    tn = 768 if W % 768 == 0 else W

    def body(c_ref, w_ref, b_ref, o_ref):
        c = c_ref[...]
        sc = (c * jax.nn.sigmoid(c)).astype(BF16)
        o_ref[...] = _nn(sc, w_ref[...].astype(BF16)) + b_ref[...]

    return pl.pallas_call(
        body, name=name, grid=(L, W // tn),
        in_specs=[pl.BlockSpec((16, D), lambda l, i: (0, 0)),
                  pl.BlockSpec((None, D, tn), lambda l, i: (l, 0, i)),
                  pl.BlockSpec((None, 1, tn), lambda l, i: (l, 0, i))],
        out_specs=pl.BlockSpec((None, 16, tn), lambda l, i: (l, 0, i)),
        out_shape=jax.ShapeDtypeStruct((L, 16, W), F32),
        compiler_params=_params(("arbitrary", "arbitrary")),
    )(cond16, w_mod, b_blk)


def adaln_bwd(cond16, dm_x, dm_c, w_mod, m_mod, v_mod, name):
    L, D, W = w_mod.shape
    tn = 256 if W % 256 == 0 else W
    nt = W // tn

    def body(c_ref, cT_ref, dx_ref, dc_ref, w_ref, m_ref, v_ref, og, od, om, ov, pc_ref):
        l, i = pl.program_id(0), pl.program_id(1)
        c = c_ref[...]
        sig, sl = _silu_parts(c)
        cT = cT_ref[...]
        sigT = jax.nn.sigmoid(cT)
        scT = (cT * sigT).astype(BF16)
        dmc = jnp.sum(dc_ref[...], axis=0, keepdims=True)
        dm16 = jnp.concatenate([dx_ref[...], jnp.broadcast_to(dmc, (8, tn))], axis=0)
        row = lax.broadcasted_iota(jnp.int32, (16, tn), 0)
        dm16 = jnp.where(row <= 8, dm16, 0.0).astype(BF16)
        w = w_ref[...]
        g = _nn(scT, dm16)
        delta, m2, v2 = _adamw(w, g, m_ref[...], v_ref[...])
        og[...] = g
        od[...] = delta
        om[...] = m2
        ov[...] = v2

        @pl.when((l == 0) & (i == 0))
        def _():
            pc_ref[...] = jnp.zeros_like(pc_ref)

        @pl.when(l == 0)
        def _():
            back = _nt(jnp.broadcast_to(dmc, (8, tn)).astype(BF16), w.astype(BF16))
            pc_ref[...] += back * _dsilu(c[8:9, :], sig[8:9, :])

    col = pl.BlockSpec((None, D, tn), lambda l, i: (l, 0, i))
    row8 = pl.BlockSpec((None, 8, tn), lambda l, i: (l, 0, i))
    return pl.pallas_call(
        body, name=name, grid=(L, nt),
        in_specs=[pl.BlockSpec((16, D), lambda l, i: (0, 0)), pl.BlockSpec((D, 16), lambda l, i: (0, 0)),
                  row8, row8, col, col, col],
        out_specs=[col, col, col, col, pl.BlockSpec((8, D), lambda l, i: (0, 0))],
        out_shape=[jax.ShapeDtypeStruct((L, D, W), F32)] * 4 + [jax.ShapeDtypeStruct((8, D), F32)],
        compiler_params=_params(("arbitrary", "arbitrary")),
    )(cond16, cond16.T, dm_x, dm_c, w_mod, m_mod, v_mod)


def _token_spec(tm, D):
    if tm > 256:
        return pl.BlockSpec((tm, D), lambda i, j: (i, 0), pipeline_mode=pl.Buffered(1))
    return pl.BlockSpec((tm, D), lambda i, j: (i, 0))


def ffn_fwd(x, mod, gn, wg, wu, wd, *, tm, t_first, name):
    T, D = x.shape
    nb, cw, _ = wg.shape
    G = mod.shape[0]
    nt = T // tm
    mm_rows = MM_ROWS if tm % MM_ROWS == 0 else tm // 2

    def body(x_ref, mod_ref, g_ref, wg_ref, wu_ref, wd_ref, xo_ref, h_ref, a_ref, b_ref, y_ref, hs, acc, a_s, b_s, s_s):
        i, j = pl.program_id(0), pl.program_id(1)

        @pl.when(j == 0)
        def _():
            def head(r, carry):
                rows = pl.ds(pl.multiple_of(r * ROW_CHUNK, ROW_CHUNK), ROW_CHUNK)
                row0 = i * tm + r * ROW_CHUNK
                hb = _norm_mod(x_ref[rows, :], g_ref[...], _mod_row(mod_ref, 0, row0, ROW_CHUNK, t_first),
                               _mod_row(mod_ref, 1, row0, ROW_CHUNK, t_first)).astype(BF16)
                hs[rows, :] = hb
                h_ref[rows, :] = hb
                return carry

            lax.fori_loop(0, tm // ROW_CHUNK, head, 0)
            acc[...] = jnp.zeros_like(acc)

        parts = [slice(r, r + mm_rows) for r in range(0, tm, mm_rows)]
        for rows in parts:
            a_s[rows, :] = _nt(hs[rows, :], wg_ref[...])
            b_s[rows, :] = _nt(hs[rows, :], wu_ref[...])
        for r in range(0, tm, ROW_CHUNK):
            rows = slice(r, r + ROW_CHUNK)
            av, bv = a_s[rows, :], b_s[rows, :]
            a_ref[rows, :] = av.astype(BF16)
            b_ref[rows, :] = bv.astype(BF16)
            _, sl = _silu_parts(av)
            s_s[rows, :] = (sl * bv).astype(BF16)
        for rows in parts:
            acc[rows, :] += _nn(s_s[rows, :], wd_ref[...])

        @pl.when(j == nb - 1)
        def _():
            def tail(r, carry):
                rows = pl.ds(pl.multiple_of(r * ROW_CHUNK, ROW_CHUNK), ROW_CHUNK)
                y = acc[rows, :]
                y_ref[rows, :] = y.astype(BF16)
                gate = _mod_row(mod_ref, 2, i * tm + r * ROW_CHUNK, ROW_CHUNK, t_first)
                xo_ref[rows, :] = x_ref[rows, :] + (MACARON * gate) * y
                return carry

            lax.fori_loop(0, tm // ROW_CHUNK, tail, 0)

    tok = _token_spec(tm, D)
    act = pl.BlockSpec((None, tm, cw), lambda i, j: (j, i, 0))
    wblk = pl.BlockSpec((None, cw, D), lambda i, j: (j, 0, 0))
    return pl.pallas_call(
        body, name=name, grid=(nt, nb),
        in_specs=[tok, pl.BlockSpec((G, 3, D), lambda i, j: (0, 0, 0)),
                  pl.BlockSpec((1, D), lambda i, j: (0, 0)), wblk, wblk, wblk],
        out_specs=[tok, tok, act, act, tok],
        out_shape=[jax.ShapeDtypeStruct((T, D), F32), jax.ShapeDtypeStruct((T, D), BF16),
                   jax.ShapeDtypeStruct((nb, T, cw), BF16), jax.ShapeDtypeStruct((nb, T, cw), BF16),
                   jax.ShapeDtypeStruct((T, D), BF16)],
        scratch_shapes=[pltpu.VMEM((tm, D), BF16), pltpu.VMEM((tm, D), F32), pltpu.VMEM((tm, cw), F32),
                        pltpu.VMEM((tm, cw), F32), pltpu.VMEM((tm, cw), BF16)],
        compiler_params=_params(("arbitrary", "arbitrary")),
    )(x, mod, gn, wg, wu, wd)


def ffn_bwd_tok(x, dxo, y, a, b, mod, gn, wg, wu, wd, *, tm, t_first, name):
    T, D = x.shape
    nb, cw, _ = wg.shape
    G = mod.shape[0]
    nt = T // tm
    first = t_first // tm

    def body(x_ref, dxo_ref, y_ref, a_ref, b_ref, mod_ref, g_ref, wg_ref, wu_ref, wd_ref,
             dx_ref, da_ref, db_ref, s_ref, dy_ref, red_ref, dys, dh, ds_s):
        i, j = pl.program_id(0), pl.program_id(1)

        @pl.when(j == 0)
        def _():
            def head(r, carry):
                rows = pl.ds(pl.multiple_of(r * ROW_CHUNK, ROW_CHUNK), ROW_CHUNK)
                dyb = ((MACARON * mod_ref[2:3, :]) * dxo_ref[rows, :]).astype(BF16)
                dys[rows, :] = dyb
                dy_ref[rows, :] = dyb
                return carry

            lax.fori_loop(0, tm // ROW_CHUNK, head, 0)
            dh[...] = jnp.zeros_like(dh)

        parts = [slice(r, r + MM_ROWS) for r in range(0, tm, MM_ROWS)]
        for rows in parts:
            ds_s[rows, :] = _nt(dys[rows, :], wd_ref[...])
        for r in range(0, tm, ROW_CHUNK):
            rows = slice(r, r + ROW_CHUNK)
            av = a_ref[rows, :].astype(F32)
            bv = b_ref[rows, :].astype(F32)
            ds = ds_s[rows, :]
            sig, sl = _silu_parts(av)
            s_ref[rows, :] = (sl * bv).astype(BF16)
            da_ref[rows, :] = (ds * bv * _dsilu(av, sig)).astype(BF16)
            db_ref[rows, :] = (ds * sl).astype(BF16)
        for rows in parts:
            dh[rows, :] += _nn(da_ref[rows, :], wg_ref[...]) + _nn(db_ref[rows, :], wu_ref[...])

        @pl.when((j == 0) & ((i == 0) | (i == first)))
        def _():
            red_ref[...] = jnp.zeros_like(red_ref)

        @pl.when(j == nb - 1)
        def _():
            def tail(r, carry):
                rows = pl.ds(pl.multiple_of(r * ROW_CHUNK, ROW_CHUNK), ROW_CHUNK)
                dxo_v = dxo_ref[rows, :]
                dxn, dshift, dscale, dg = _norm_mod_bwd(x_ref[rows, :], dh[rows, :], g_ref[...], mod_ref[1:2, :])
                dx_ref[rows, :] = dxo_v + dxn
                red_ref[0:1, :] += dshift
                red_ref[1:2, :] += dscale
                red_ref[2:3, :] += jnp.sum((MACARON * dxo_v) * y_ref[rows, :].astype(F32), axis=0, keepdims=True)
                red_ref[3:4, :] += dg
                return carry

            lax.fori_loop(0, tm // ROW_CHUNK, tail, 0)

    tok = _token_spec(tm, D)
    act = pl.BlockSpec((None, tm, cw), lambda i, j: (j, i, 0))
    wblk = pl.BlockSpec((None, cw, D), lambda i, j: (j, 0, 0))
    return pl.pallas_call(
        body, name=name, grid=(nt, nb),
        in_specs=[tok, tok, tok, act, act,
                  pl.BlockSpec((None, 3, D), lambda i, j: (_group_index(i, first, G), 0, 0)),
                  pl.BlockSpec((1, D), lambda i, j: (0, 0)), wblk, wblk, wblk],
        out_specs=[tok, act, act, act, tok,
                   pl.BlockSpec((None, 8, D), lambda i, j: (_group_index(i, first, G), 0, 0))],
        out_shape=[jax.ShapeDtypeStruct((T, D), F32)] + [jax.ShapeDtypeStruct((nb, T, cw), BF16)] * 3
        + [jax.ShapeDtypeStruct((T, D), BF16), jax.ShapeDtypeStruct((G, 8, D), F32)],
        scratch_shapes=[pltpu.VMEM((tm, D), BF16), pltpu.VMEM((tm, D), F32), pltpu.VMEM((tm, cw), F32)],
        compiler_params=_params(("arbitrary", "arbitrary")),
    )(x, dxo, y, a, b, mod, gn, wg, wu, wd)


def matmul_tn(a, b, *, nb, m, n, a_blocked, b_blocked, tk, name, after=None):
    T = a.shape[-2]
    extra = [] if after is None else [after]

    def spec(arr, blocked, width):
        if blocked:
            return pl.BlockSpec((None, tk, width), lambda j, k: (j, k, 0))
        if arr.shape[-1] == width:
            return pl.BlockSpec((tk, width), lambda j, k: (k, 0))
        return pl.BlockSpec((tk, width), lambda j, k: (k, j))

    nk = T // tk

    def body(a_ref, b_ref, *rest):
        if nk == 1:
            rest[-1][...] = _tn(a_ref[...], b_ref[...]).astype(BF16)
            return
        o_ref, acc = rest[-2:]
        k = pl.program_id(1)

        @pl.when(k == 0)
        def _():
            acc[...] = jnp.zeros_like(acc)

        acc[...] += _tn(a_ref[...], b_ref[...])

        @pl.when(k == nk - 1)
        def _():
            o_ref[...] = acc[...].astype(BF16)

    return pl.pallas_call(
        body, name=name, grid=(nb, nk),
        in_specs=[spec(a, a_blocked, m), spec(b, b_blocked, n)] + [ANY] * len(extra),
        out_specs=pl.BlockSpec((None, m, n), lambda j, k: (j, 0, 0)),
        out_shape=jax.ShapeDtypeStruct((nb, m, n), BF16),
        scratch_shapes=[] if nk == 1 else [pltpu.VMEM((m, n), F32)],
        compiler_params=_params(("arbitrary", "arbitrary")),
    )(a, b, *extra)


def proj_in(x, mod, gn, w_in, *, tm, t_first, name):
    T, D = x.shape
    nb, _, cw = w_in.shape
    G = mod.shape[0]
    nt = T // tm

    def body(x_ref, mod_ref, g_ref, w_ref, p_ref, h_ref, hs):
        i = pl.program_id(0)

        @pl.when(pl.program_id(1) == 0)
        def _():
            def head(r, carry):
                rows = pl.ds(pl.multiple_of(r * ROW_CHUNK, ROW_CHUNK), ROW_CHUNK)
                row0 = i * tm + r * ROW_CHUNK
                hb = _norm_mod(x_ref[rows, :], g_ref[...], _mod_row(mod_ref, 0, row0, ROW_CHUNK, t_first),
                               _mod_row(mod_ref, 1, row0, ROW_CHUNK, t_first)).astype(BF16)
                hs[rows, :] = hb
                h_ref[rows, :] = hb
                return carry

            lax.fori_loop(0, tm // ROW_CHUNK, head, 0)

        for r in range(0, tm, MM_ROWS):
            p_ref[r:r + MM_ROWS, :] = _nn(hs[r:r + MM_ROWS, :], w_ref[...])

    tok = _token_spec(tm, D)
    return pl.pallas_call(
        body, name=name, grid=(nt, nb),
        in_specs=[tok, pl.BlockSpec((G, 3, D), lambda i, j: (0, 0, 0)),
                  pl.BlockSpec((1, D), lambda i, j: (0, 0)),
                  pl.BlockSpec((None, D, cw), lambda i, j: (j, 0, 0))],
        out_specs=[pl.BlockSpec((tm, cw), lambda i, j: (i, j)), tok],
        out_shape=[jax.ShapeDtypeStruct((T, nb * cw), F32), jax.ShapeDtypeStruct((T, D), BF16)],
        scratch_shapes=[pltpu.VMEM((tm, D), BF16)],
        compiler_params=_params(("arbitrary", "arbitrary")),
    )(x, mod, gn, w_in)


def proj_bwd_tok(x, dxo, dp, mod, gn, w_in, *, tm, name):
    T, D = x.shape
    nb, _, cw = w_in.shape
    G = mod.shape[0]
    nt = T // tm
    first = dxo.shape[0] // tm

    def body(x_ref, dxo_ref, dp_ref, mod_ref, g_ref, w_ref, dx_ref, red_ref, dh):
        i, j = pl.program_id(0), pl.program_id(1)

        @pl.when(j == 0)
        def _():
            dh[...] = jnp.zeros_like(dh)

        dh[...] += _nt(dp_ref[...], w_ref[...])

        @pl.when((j == 0) & ((i == 0) | (i == first)))
        def _():
            red_ref[...] = jnp.zeros_like(red_ref)

        @pl.when(j == nb - 1)
        def _():
            dxn, dshift, dscale, dg = _norm_mod_bwd(x_ref[...], dh[...], g_ref[...], mod_ref[1:2, :])
            dx_ref[...] = jnp.where(i < first, dxo_ref[...], 0.0) + dxn
            red_ref[0:1, :] += dshift
            red_ref[1:2, :] += dscale
            red_ref[3:4, :] += dg

    tok = pl.BlockSpec((tm, D), lambda i, j: (i, 0))
    return pl.pallas_call(
        body, name=name, grid=(nt, nb),
        in_specs=[tok, pl.BlockSpec((tm, D), lambda i, j: (jnp.minimum(i, first - 1), 0)),
                  pl.BlockSpec((tm, cw), lambda i, j: (i, j)),
                  pl.BlockSpec((None, 3, D), lambda i, j: (_group_index(i, first, G), 0, 0)),
                  pl.BlockSpec((1, D), lambda i, j: (0, 0)),
                  pl.BlockSpec((None, D, cw), lambda i, j: (j, 0, 0))],
        out_specs=[tok, pl.BlockSpec((None, 8, D), lambda i, j: (_group_index(i, first, G), 0, 0))],
        out_shape=[jax.ShapeDtypeStruct((T, D), F32), jax.ShapeDtypeStruct((G, 8, D), F32)],
        scratch_shapes=[pltpu.VMEM((tm, D), F32)],
        compiler_params=_params(("arbitrary", "arbitrary")),
    )(x, dxo, dp, mod, gn, w_in)


def _rope(t, cos, sin):
    return t * cos + pltpu.roll(t, HEAD_DIM // 2, axis=1) * sin


def ret_prep(p, cos, sin, *, heads, tm, name):
    T = p.shape[0]

    def body(q_ref, k_ref, v_ref, c_ref, s_ref, qo, ko, vo):
        cos_v, sin_v = c_ref[...], s_ref[...]
        qo[...] = _rope(q_ref[...], cos_v, sin_v).astype(BF16)
        ko[...] = _rope(k_ref[...] * K_SCALE, cos_v, sin_v).astype(BF16)
        vo[...] = v_ref[...].astype(BF16)

    def col(part):
        return pl.BlockSpec((tm, HEAD_DIM), lambda h, i: (i, part * heads + h))

    tab = pl.BlockSpec((tm, HEAD_DIM), lambda h, i: (i, 0))
    out = pl.BlockSpec((None, tm, HEAD_DIM), lambda h, i: (h, i, 0))
    return pl.pallas_call(
        body, name=name, grid=(heads, T // tm),
        in_specs=[col(0), col(1), col(2), tab, tab], out_specs=[out, out, out],
        out_shape=[jax.ShapeDtypeStruct((heads, T, HEAD_DIM), BF16)] * 3,
        compiler_params=_params(("arbitrary", "arbitrary")),
    )(p, p, p, cos, sin)


def _decay(n, m, lgf, lgb, t_lat, t_ctx):
    df = jnp.where(m < t_lat, n - m, n - m + (t_lat + t_ctx))
    db = m - n
    ef = jnp.where(df >= 0, jnp.exp(lgf * df), 0.0)
    eb = jnp.where(db >= 0, jnp.exp(lgb * db), 0.0)
    return ef, eb, df, db


def ret_fwd(q, k, v, lg, *, t_lat, tq, tk, name):
    H, T, _ = k.shape
    t_ctx = T - t_lat

    def body(lg_ref, q_ref, k_ref, v_ref, o_ref):
        h, qi, kj = pl.program_id(0), pl.program_id(1), pl.program_id(2)

        @pl.when(kj == 0)
        def _():
            o_ref[...] = jnp.zeros_like(o_ref)

        s = _nt(q_ref[...], k_ref[...])
        n = (qi * tq + lax.broadcasted_iota(jnp.int32, (tq, tk), 0)).astype(F32)
        m = (kj * tk + lax.broadcasted_iota(jnp.int32, (tq, tk), 1)).astype(F32)
        ef, eb, _, _ = _decay(n, m, lg_ref[0, h], lg_ref[1, h], t_lat, t_ctx)
        o_ref[...] += _nn((s * (ef + eb)).astype(BF16), v_ref[...])

    return pl.pallas_call(
        body, name=name, grid=(H, t_lat // tq, T // tk),
        in_specs=[pl.BlockSpec(memory_space=pltpu.SMEM),
                  pl.BlockSpec((None, tq, HEAD_DIM), lambda h, i, j: (h, i, 0)),
                  pl.BlockSpec((None, tk, HEAD_DIM), lambda h, i, j: (h, j, 0)),
                  pl.BlockSpec((None, tk, HEAD_DIM), lambda h, i, j: (h, j, 0))],
        out_specs=pl.BlockSpec((None, tq, HEAD_DIM), lambda h, i, j: (h, i, 0)),
        out_shape=jax.ShapeDtypeStruct((H, t_lat, HEAD_DIM), F32),
        compiler_params=_params(("arbitrary", "arbitrary", "arbitrary")),
    )(lg, q, k, v)


def ret_bwd(q, k, v, do, lg, *, t_lat, tq, tk, name):
    H, T, _ = k.shape
    t_ctx = T - t_lat

    def body(lg_ref, q_ref, k_ref, v_ref, do_ref, dq_ref, dk_ref, dv_ref, dlg_ref):
        h, kj, qi = pl.program_id(0), pl.program_id(1), pl.program_id(2)

        @pl.when((kj == 0) & (qi == 0))
        def _():
            dq_ref[...] = jnp.zeros_like(dq_ref)
            dlg_ref[...] = jnp.zeros_like(dlg_ref)

        @pl.when(qi == 0)
        def _():
            dk_ref[...] = jnp.zeros_like(dk_ref)
            dv_ref[...] = jnp.zeros_like(dv_ref)

        qv, kv, vv = q_ref[...], k_ref[...], v_ref[...]
        dob = do_ref[...].astype(BF16)
        st = _nt(kv, qv)
        dwt = _nt(vv, dob)
        m = (kj * tk + lax.broadcasted_iota(jnp.int32, (tk, tq), 0)).astype(F32)
        n = (qi * tq + lax.broadcasted_iota(jnp.int32, (tk, tq), 1)).astype(F32)
        ef, eb, df, db = _decay(n, m, lg_ref[0, h], lg_ref[1, h], t_lat, t_ctx)
        dec = ef + eb
        dv_ref[...] += _nn((st * dec).astype(BF16), dob)
        dst = (dwt * dec).astype(BF16)
        dk_ref[...] += _nn(dst, qv)
        rows = pl.ds(pl.multiple_of(qi * tq, tq), tq)
        dq_ref[rows, :] += _tn(dst, kv)
        gs = dwt * st
        dlg_ref[0:1, :] += jnp.sum(gs * (ef * df))
        dlg_ref[1:2, :] += jnp.sum(gs * (eb * db))

    kspec = pl.BlockSpec((None, tk, HEAD_DIM), lambda h, j, i: (h, j, 0))
    qspec = pl.BlockSpec((None, tq, HEAD_DIM), lambda h, j, i: (h, i, 0))
    return pl.pallas_call(
        body, name=name, grid=(H, T // tk, t_lat // tq),
        in_specs=[pl.BlockSpec(memory_space=pltpu.SMEM), qspec, kspec, kspec, qspec],
        out_specs=[pl.BlockSpec((None, t_lat, HEAD_DIM), lambda h, j, i: (h, 0, 0)), kspec, kspec,
                   pl.BlockSpec((None, 8, LANE), lambda h, j, i: (h, 0, 0))],
        out_shape=[jax.ShapeDtypeStruct((H, t_lat, HEAD_DIM), F32), jax.ShapeDtypeStruct((H, T, HEAD_DIM), F32),
                   jax.ShapeDtypeStruct((H, T, HEAD_DIM), F32), jax.ShapeDtypeStruct((H, 8, LANE), F32)],
        compiler_params=_params(("arbitrary", "arbitrary", "arbitrary")),
    )(lg, q, k, v, do)


def _group_norm(o):
    mu = jnp.mean(o, axis=-1, keepdims=True)
    ctr = o - mu
    r = lax.rsqrt(jnp.mean(ctr * ctr, axis=-1, keepdims=True) + GN_EPS)
    return ctr * r, r


def ret_post_fwd(o, p, *, heads, d_model, name):
    H, t_lat, _ = o.shape
    T = p.shape[0]

    def body(o_ref, g_ref, z_ref):
        on, _ = _group_norm(o_ref[...])
        _, sl = _silu_parts(g_ref[0:t_lat, :])
        z_ref[...] = (on * sl).astype(BF16)

    return pl.pallas_call(
        body, name=name, grid=(H,),
        in_specs=[pl.BlockSpec((None, t_lat, HEAD_DIM), lambda h: (h, 0, 0)),
                  pl.BlockSpec((T, HEAD_DIM), lambda h: (0, 3 * heads + h))],
        out_specs=pl.BlockSpec((t_lat, HEAD_DIM), lambda h: (0, h)),
        out_shape=jax.ShapeDtypeStruct((t_lat, d_model), BF16),
        compiler_params=_params(("arbitrary",)),
    )(o, p)


def _shift_rows(u, k):
    rows = u.shape[0]
    row = lax.broadcasted_iota(jnp.int32, u.shape, 0)
    rolled = pltpu.roll(u, k % rows, axis=0)
    return jnp.where((row >= k) & (row < rows + k), rolled, 0.0)


def conv_fwd(z, p, w_conv, *, heads, t_lat, name):
    T = p.shape[0]
    cb_n = w_conv.shape[1] // LANE
    base = 4 * heads

    def body(z_in, bg_ref, cg_ref, u_ref, w_ref, z_ref):
        cu = cg_ref[0:t_lat, :] * u_ref[0:t_lat, :]
        c3 = _shift_rows(cu, 1) * w_ref[0:1, :] + cu * w_ref[1:2, :] + _shift_rows(cu, -1) * w_ref[2:3, :]
        z_ref[...] = (bg_ref[0:t_lat, :] * c3).astype(BF16)

    def col(part):
        return pl.BlockSpec((T, LANE), lambda cb: (0, base + part * cb_n + cb))

    return pl.pallas_call(
        body, name=name, grid=(cb_n,),
        in_specs=[ANY, col(0), col(1), col(2), pl.BlockSpec((CONV_WIDTH, LANE), lambda cb: (0, cb))],
        out_specs=pl.BlockSpec((t_lat, LANE), lambda cb: (0, heads + cb)),
        out_shape=jax.ShapeDtypeStruct(z.shape, z.dtype),
        input_output_aliases={0: 0},
        compiler_params=_params(("arbitrary",)),
    )(z, p, p, p, w_conv)


def out_proj(z, w_out, x, mod, *, tm, name):
    t_lat, D = z.shape

    def body(z_ref, w_ref, x_ref, mod_ref, xo_ref, y_ref):
        y = _nn(z_ref[...], w_ref[...])
        y_ref[...] = y.astype(BF16)
        xo_ref[...] = x_ref[...] + mod_ref[2:3, :] * y

    tok = pl.BlockSpec((tm, D), lambda i: (i, 0))
    return pl.pallas_call(
        body, name=name, grid=(t_lat // tm,),
        in_specs=[tok, pl.BlockSpec((D, D), lambda i: (0, 0)), tok, pl.BlockSpec((None, 3, D), lambda i: (0, 0, 0))],
        out_specs=[tok, tok],
        out_shape=[jax.ShapeDtypeStruct((t_lat, D), F32), jax.ShapeDtypeStruct((t_lat, D), BF16)],
        compiler_params=_params(("arbitrary",)),
    )(z, w_out, x, mod)


def out_proj_bwd(dxo, y, w_out, mod, *, tm, name):
    t_lat, D = dxo.shape

    def body(dxo_ref, y_ref, w_ref, mod_ref, dz_ref, dy_ref, red_ref):
        @pl.when(pl.program_id(0) == 0)
        def _():
            red_ref[...] = jnp.zeros_like(red_ref)

        dxo_v = dxo_ref[...]
        dyb = (mod_ref[2:3, :] * dxo_v).astype(BF16)
        dy_ref[...] = dyb
        dz_ref[...] = _nt(dyb, w_ref[...])
        red_ref[2:3, :] += jnp.sum(dxo_v * y_ref[...].astype(F32), axis=0, keepdims=True)

    tok = pl.BlockSpec((tm, D), lambda i: (i, 0))
    return pl.pallas_call(
        body, name=name, grid=(t_lat // tm,),
        in_specs=[tok, tok, pl.BlockSpec((D, D), lambda i: (0, 0)), pl.BlockSpec((None, 3, D), lambda i: (0, 0, 0))],
        out_specs=[tok, tok, pl.BlockSpec((8, D), lambda i: (0, 0))],
        out_shape=[jax.ShapeDtypeStruct((t_lat, D), F32), jax.ShapeDtypeStruct((t_lat, D), BF16),
                   jax.ShapeDtypeStruct((8, D), F32)],
        compiler_params=_params(("arbitrary",)),
    )(dxo, y, w_out, mod)


def ret_post_bwd(dz, o, p, *, heads, name):
    H, t_lat, _ = o.shape
    T, in_w = p.shape

    def body(dz_ref, o_ref, g_ref, do_ref, dp_ref):
        on, r = _group_norm(o_ref[...])
        gg = g_ref[0:t_lat, :]
        sig, sl = _silu_parts(gg)
        dret = dz_ref[...]
        don = dret * sl
        do_ref[...] = r * (don - jnp.mean(don, axis=-1, keepdims=True)
                           - on * jnp.mean(don * on, axis=-1, keepdims=True))
        dp_ref[0:t_lat, :] = (dret * on * _dsilu(gg, sig)).astype(BF16)
        dp_ref[t_lat:T, :] = jnp.zeros((T - t_lat, HEAD_DIM), BF16)

    return pl.pallas_call(
        body, name=name, grid=(H,),
        in_specs=[pl.BlockSpec((t_lat, HEAD_DIM), lambda h: (0, h)),
                  pl.BlockSpec((None, t_lat, HEAD_DIM), lambda h: (h, 0, 0)),
                  pl.BlockSpec((T, HEAD_DIM), lambda h: (0, 3 * heads + h))],
        out_specs=[pl.BlockSpec((None, t_lat, HEAD_DIM), lambda h: (h, 0, 0)),
                   pl.BlockSpec((T, HEAD_DIM), lambda h: (0, 3 * heads + h))],
        out_shape=[jax.ShapeDtypeStruct((H, t_lat, HEAD_DIM), F32), jax.ShapeDtypeStruct((T, in_w), BF16)],
        compiler_params=_params(("arbitrary",)),
    )(dz, o, p)


def conv_bwd(dp, dz, p, w_conv, *, heads, t_lat, name):
    T = p.shape[0]
    cb_n = w_conv.shape[1] // LANE
    base = 4 * heads

    def body(dp_in, dz_ref, bg_ref, cg_ref, u_ref, w_ref, dp_ref, dw_ref):
        part = pl.program_id(1)
        cg, u = cg_ref[0:t_lat, :], u_ref[0:t_lat, :]
        cu = cg * u
        dconv = dz_ref[...]
        dp_ref[t_lat:T, :] = jnp.zeros((T - t_lat, LANE), BF16)

        @pl.when(part == 0)
        def _():
            c3 = _shift_rows(cu, 1) * w_ref[0:1, :] + cu * w_ref[1:2, :] + _shift_rows(cu, -1) * w_ref[2:3, :]
            dp_ref[0:t_lat, :] = (dconv * c3).astype(BF16)
            dc3 = dconv * bg_ref[0:t_lat, :]
            dw_ref[0:1, :] = jnp.sum(dc3 * _shift_rows(cu, 1), axis=0, keepdims=True)
            dw_ref[1:2, :] = jnp.sum(dc3 * cu, axis=0, keepdims=True)
            dw_ref[2:3, :] = jnp.sum(dc3 * _shift_rows(cu, -1), axis=0, keepdims=True)

        @pl.when(part > 0)
        def _():
            dc3 = dconv * bg_ref[0:t_lat, :]
            dcu = (_shift_rows(dc3, -1) * w_ref[0:1, :] + dc3 * w_ref[1:2, :] + _shift_rows(dc3, 1) * w_ref[2:3, :])
            dp_ref[0:t_lat, :] = (dcu * jnp.where(part == 1, u, cg)).astype(BF16)

    def col(part):
        return pl.BlockSpec((T, LANE), lambda cb, pt: (0, base + part * cb_n + cb))

    return pl.pallas_call(
        body, name=name, grid=(cb_n, 3),
        in_specs=[ANY, pl.BlockSpec((t_lat, LANE), lambda cb, pt: (0, heads + cb)), col(0), col(1), col(2),
                  pl.BlockSpec((CONV_WIDTH, LANE), lambda cb, pt: (0, cb))],
        out_specs=[pl.BlockSpec((T, LANE), lambda cb, pt: (0, base + pt * cb_n + cb)),
                   pl.BlockSpec((CONV_WIDTH, LANE), lambda cb, pt: (0, cb))],
        out_shape=[jax.ShapeDtypeStruct(dp.shape, dp.dtype), jax.ShapeDtypeStruct(w_conv.shape, F32)],
        input_output_aliases={0: 0},
        compiler_params=_params(("arbitrary", "arbitrary")),
    )(dp, dz, p, p, p, w_conv)


def ret_unprep(dp, dq, dk, dv, cos, sin, *, t_lat, name):
    H, T, _ = dk.shape

    def body(dp_in, dq_ref, dk_ref, dv_ref, c_ref, s_ref, dp_ref):
        part = pl.program_id(0)

        @pl.when(part == 0)
        def _():
            dp_ref[0:t_lat, :] = _rope(dq_ref[...], c_ref[0:t_lat, :], -s_ref[0:t_lat, :]).astype(BF16)
            dp_ref[t_lat:T, :] = jnp.zeros((T - t_lat, HEAD_DIM), BF16)

        @pl.when(part == 1)
        def _():
            dp_ref[...] = (_rope(dk_ref[...], c_ref[...], -s_ref[...]) * K_SCALE).astype(BF16)

        @pl.when(part == 2)
        def _():
            dp_ref[...] = dv_ref[...].astype(BF16)

    def head_of(part):
        return lambda pt, h: (jnp.where(pt == part, h, 0), 0, 0)

    tab = pl.BlockSpec((T, HEAD_DIM), lambda pt, h: (0, 0))
    return pl.pallas_call(
        body, name=name, grid=(3, H),
        in_specs=[ANY, pl.BlockSpec((None, t_lat, HEAD_DIM), head_of(0)), pl.BlockSpec((None, T, HEAD_DIM), head_of(1)),
                  pl.BlockSpec((None, T, HEAD_DIM), head_of(2)), tab, tab],
        out_specs=pl.BlockSpec((T, HEAD_DIM), lambda pt, h: (0, pt * H + h)),
        out_shape=jax.ShapeDtypeStruct(dp.shape, dp.dtype),
        input_output_aliases={0: 0},
        compiler_params=_params(("arbitrary", "arbitrary")),
    )(dp, dq, dk, dv, cos, sin)


def norm_mod_fwd(x, mod, gn, *, tm, name):
    T, D = x.shape

    def body(x_ref, mod_ref, g_ref, h_ref):
        h_ref[...] = _norm_mod(x_ref[...], g_ref[...], mod_ref[0:1, :], mod_ref[1:2, :])

    tok = pl.BlockSpec((tm, D), lambda i: (i, 0))
    return pl.pallas_call(
        body, name=name, grid=(T // tm,),
        in_specs=[tok, pl.BlockSpec((None, 3, D), lambda i: (0, 0, 0)), pl.BlockSpec((1, D), lambda i: (0, 0))],
        out_specs=tok, out_shape=jax.ShapeDtypeStruct((T, D), F32),
        compiler_params=_params(("arbitrary",)),
    )(x, mod, gn)


def norm_mod_bwd(x, dh, dxo, mod, gn, *, tm, name):
    T, D = x.shape

    def body(x_ref, dh_ref, dxo_ref, mod_ref, g_ref, dx_ref, red_ref):
        @pl.when(pl.program_id(0) == 0)
        def _():
            red_ref[...] = jnp.zeros_like(red_ref)

        dxn, dshift, dscale, dg = _norm_mod_bwd(x_ref[...], dh_ref[...], g_ref[...], mod_ref[1:2, :])
        dx_ref[...] = dxo_ref[...] + dxn
        red_ref[0:1, :] += dshift
        red_ref[1:2, :] += dscale
        red_ref[3:4, :] += dg

    tok = pl.BlockSpec((tm, D), lambda i: (i, 0))
    return pl.pallas_call(
        body, name=name, grid=(T // tm,),
        in_specs=[tok, tok, tok, pl.BlockSpec((None, 3, D), lambda i: (0, 0, 0)), pl.BlockSpec((1, D), lambda i: (0, 0))],
        out_specs=[tok, pl.BlockSpec((8, D), lambda i: (0, 0))],
        out_shape=[jax.ShapeDtypeStruct((T, D), F32), jax.ShapeDtypeStruct((8, D), F32)],
        compiler_params=_params(("arbitrary",)),
    )(x, dh, dxo, mod, gn)


def _window_sum(u, w, lead):
    T, C = u.shape
    ext = jnp.concatenate([u, jnp.zeros((POOL_PAD, C), F32)], axis=0)
    k = 1
    while k < w:
        ext = ext + _shift_rows(ext, k)
        k *= 2
    return _shift_rows(ext, -lead)[0:T, :]


def _window_count(T, C, w):
    t = lax.broadcasted_iota(jnp.int32, (T, C), 0)
    lo = jnp.clip(t - w // 2, 0, T)
    hi = jnp.clip(t + (w - w // 2), 0, T)
    return (hi - lo).astype(F32)


def pool_fwd(h, x, pool_w, scale, mod, *, name):
    T, D = h.shape
    G, Cg, _ = pool_w.shape
    ns = Cg // LANE

    def body(h_ref, x_ref, w_ref, sc_ref, mod_ref, xo_ref, pl_ref, yl_ref, acc):
        g, s = pl.program_id(0), pl.program_id(1)
        hv = h_ref[...]
        for gi, win in enumerate(POOL_WINDOWS):
            @pl.when(g == gi)
            def _():
                mean = _window_sum(hv, win, win // 2 - 1) / _window_count(T, LANE, win)
                pooled = (mean - hv).astype(BF16)
                pl_ref[...] = pooled
                contrib = _nn(pooled, w_ref[...])

                @pl.when(s == 0)
                def _():
                    acc[...] = contrib

                @pl.when(s > 0)
                def _():
                    acc[...] += contrib

        @pl.when(s == ns - 1)
        def _():
            yl = acc[...]
            yl_ref[...] = yl.astype(BF16)
            xo_ref[...] = x_ref[...] + mod_ref[2:3, :] * (yl * sc_ref[...])

    grp = pl.BlockSpec((T, Cg), lambda g, s: (0, g))
    sub = pl.BlockSpec((T, LANE), lambda g, s: (0, g * ns + s))
    return pl.pallas_call(
        body, name=name, grid=(G, ns),
        in_specs=[sub, grp, pl.BlockSpec((None, LANE, Cg), lambda g, s: (g, s, 0)),
                  pl.BlockSpec((1, Cg), lambda g, s: (0, g)), pl.BlockSpec((None, 3, Cg), lambda g, s: (0, 0, g))],
        out_specs=[grp, sub, grp],
        out_shape=[jax.ShapeDtypeStruct((T, D), F32), jax.ShapeDtypeStruct((T, D), BF16),
                   jax.ShapeDtypeStruct((T, D), BF16)],
        scratch_shapes=[pltpu.VMEM((T, Cg), F32)],
        compiler_params=_params(("arbitrary", "arbitrary")),
    )(h, x, pool_w, scale, mod)


def pool_bwd(dxo, pooled, yl, pool_w, scale, mod, *, name):
    T, D = dxo.shape
    G, Cg, _ = pool_w.shape
    ns = Cg // LANE

    def body(dxo_ref, pl_ref, yl_ref, w_ref, sc_ref, mod_ref, dh_ref, dw_ref, red_ref, dyl):
        g, s = pl.program_id(0), pl.program_id(1)

        @pl.when(s == 0)
        def _():
            dxo_v = dxo_ref[...]
            ylv = yl_ref[...].astype(F32)
            dy = mod_ref[2:3, :] * dxo_v
            dyl[...] = (dy * sc_ref[...]).astype(BF16)
            red_ref[...] = jnp.zeros_like(red_ref)
            red_ref[2:3, :] = jnp.sum(dxo_v * (ylv * sc_ref[...]), axis=0, keepdims=True)
            red_ref[4:5, :] = jnp.sum(dy * ylv, axis=0, keepdims=True)

        dylv = dyl[...]
        dpool = _nt(dylv, w_ref[...])
        dw_ref[...] = _tn(pl_ref[...], dylv).astype(BF16)
        for gi, win in enumerate(POOL_WINDOWS):
            @pl.when(g == gi)
            def _():
                spread = _window_sum(dpool / _window_count(T, LANE, win), win, win // 2)
                dh_ref[...] = spread - dpool

    grp = pl.BlockSpec((T, Cg), lambda g, s: (0, g))
    sub = pl.BlockSpec((T, LANE), lambda g, s: (0, g * ns + s))
    wsub = pl.BlockSpec((None, LANE, Cg), lambda g, s: (g, s, 0))
    return pl.pallas_call(
        body, name=name, grid=(G, ns),
        in_specs=[grp, sub, grp, wsub, pl.BlockSpec((1, Cg), lambda g, s: (0, g)),
                  pl.BlockSpec((None, 3, Cg), lambda g, s: (0, 0, g))],
        out_specs=[sub, wsub, pl.BlockSpec((8, Cg), lambda g, s: (0, g))],
        out_shape=[jax.ShapeDtypeStruct((T, D), F32), jax.ShapeDtypeStruct((G, Cg, Cg), BF16),
                   jax.ShapeDtypeStruct((8, D), F32)],
        scratch_shapes=[pltpu.VMEM((T, Cg), BF16)],
        compiler_params=_params(("arbitrary", "arbitrary")),
    )(dxo, pooled, yl, pool_w, scale, mod)


def final_loss(x, gn, target, *, tm, name):
    T, D = x.shape

    def body(x_ref, g_ref, t_ref, loss_ref, dx_ref, red_ref):
        @pl.when(pl.program_id(0) == 0)
        def _():
            loss_ref[...] = jnp.zeros_like(loss_ref)
            red_ref[...] = jnp.zeros_like(red_ref)

        xx, g = x_ref[...], g_ref[...]
        r = lax.rsqrt(jnp.mean(xx * xx, axis=-1, keepdims=True) + EPS)
        xhat = xx * r
        err = xhat * g - t_ref[...]
        loss_ref[...] += 0.5 * jnp.sum(jnp.mean(err * err, axis=-1, keepdims=True))
        dy = err / D
        red_ref[0:1, :] += jnp.sum(dy * xhat, axis=0, keepdims=True)
        dxh = dy * g
        dx_ref[...] = r * (dxh - xhat * jnp.mean(dxh * xhat, axis=-1, keepdims=True))

    tok = pl.BlockSpec((tm, D), lambda i: (i, 0))
    return pl.pallas_call(
        body, name=name, grid=(T // tm,),
        in_specs=[tok, pl.BlockSpec((1, D), lambda i: (0, 0)), tok],
        out_specs=[pl.BlockSpec((8, LANE), lambda i: (0, 0)), tok, pl.BlockSpec((8, D), lambda i: (0, 0))],
        out_shape=[jax.ShapeDtypeStruct((8, LANE), F32), jax.ShapeDtypeStruct((T, D), F32),
                   jax.ShapeDtypeStruct((8, D), F32)],
        compiler_params=_params(("arbitrary",)),
    )(x, gn, target)


def _rope_tables(t_lat, t_ctx):
    quarter = HEAD_DIM // 4
    pos = jnp.arange(t_lat)
    inv = ROPE_BASE ** (-jnp.arange(quarter, dtype=F32) / quarter)
    ang = jnp.concatenate([(pos // GRID_W).astype(F32)[:, None] * inv, (pos % GRID_W).astype(F32)[:, None] * inv], axis=-1)
    cos, sin = jnp.cos(ang), jnp.sin(ang)
    cos = jnp.concatenate([jnp.concatenate([cos, cos], axis=-1), jnp.ones((t_ctx, HEAD_DIM), F32)], axis=0)
    sin = jnp.concatenate([jnp.concatenate([-sin, sin], axis=-1), jnp.zeros((t_ctx, HEAD_DIM), F32)], axis=0)
    return cos, sin


def _ffn_grads(h, da, db, s, dy, tag):
    nb, T, cw = da.shape
    D = h.shape[1]
    tk = T
    g_gate = matmul_tn(da, h, nb=nb, m=cw, n=D, a_blocked=True, b_blocked=False, tk=tk, name=f"wgrad_gate_{tag}")
    g_up = matmul_tn(db, h, nb=nb, m=cw, n=D, a_blocked=True, b_blocked=False, tk=tk, name=f"wgrad_up_{tag}")
    g_down = matmul_tn(s, dy, nb=nb, m=cw, n=D, a_blocked=True, b_blocked=False, tk=tk, name=f"wgrad_down_{tag}")
    return [g_gate, g_up, g_down]


def kernel(x, c, ctx, c_ctx, w_mod, b_mod, norm_ffn1, norm_mix, norm_ffn2, ffn1_w_gate, ffn1_w_up, ffn1_w_down, ffn2_w_gate, ffn2_w_up, ffn2_w_down, mix_w_in, mix_w_conv, mix_w_out, ret_decay_fwd, ret_decay_bwd, pool_w, pool_scale, final_norm, loss_target, m_c_ctx, m_w_mod, m_b_mod, m_norm_ffn1, m_norm_mix, m_norm_ffn2, m_ffn1_w_gate, m_ffn1_w_up, m_ffn1_w_down, m_ffn2_w_gate, m_ffn2_w_up, m_ffn2_w_down, m_mix_w_in, m_mix_w_conv, m_mix_w_out, m_ret_decay_fwd, m_ret_decay_bwd, m_pool_w, m_pool_scale, m_final_norm, v_c_ctx, v_w_mod, v_b_mod, v_norm_ffn1, v_norm_mix, v_norm_ffn2, v_ffn1_w_gate, v_ffn1_w_up, v_ffn1_w_down, v_ffn2_w_gate, v_ffn2_w_up, v_ffn2_w_down, v_mix_w_in, v_mix_w_conv, v_mix_w_out, v_ret_decay_fwd, v_ret_decay_bwd, v_pool_w, v_pool_scale, v_final_norm):
    t_lat, D = x.shape[1], x.shape[2]
    t_ctx = ctx.shape[1]
    T = t_lat + t_ctx
    heads = ret_decay_fwd.shape[1]
    mod_w = w_mod.shape[2]
    tm = 256
    tmf = 512 if t_lat % 512 == 0 else 256
    tq = 512 if t_lat % 512 == 0 else 256
    tk = T // 3 if (T % 3 == 0 and (T // 3) % 256 == 0) else 256

    ax, ay, ac = lax.axis_index("x"), lax.axis_index("y"), lax.axis_index("c")
    me = 4 * ax + 2 * ay + ac
    own_block = jnp.reshape(me, (1,)).astype(jnp.int32)
    peer_blocks = jnp.stack([4 * (1 - ax) + 2 * ay + ac, 4 * ax + 2 * (1 - ay) + ac,
                             4 * (1 - ax) + 2 * (1 - ay) + ac]).astype(jnp.int32)

    (c_all,) = all_gather([c], name="gather_cond")
    cond16 = jnp.concatenate([c_all.reshape(N_DEV, D), c_ctx[None, :], jnp.zeros((7, D), F32)], axis=0)
    b_blk = lax.dynamic_slice_in_dim(b_mod, me * mod_w, mod_w, axis=1)[:, None, :]
    m_blk = adaln_fwd(cond16, w_mod, b_blk, name="adaln_fwd")
    m_all, w_conv, pscale = all_gather([m_blk, mix_w_conv[0], pool_scale], name="gather_mod")
    mods = jnp.transpose(m_all, (1, 2, 0, 3)).reshape(2, 16, N_MOD, D)
    mod_x = lax.dynamic_index_in_dim(mods, me, axis=1, keepdims=False)
    mod_c = mods[0, 8]
    w_conv = jnp.transpose(w_conv, (1, 0, 2)).reshape(CONV_WIDTH, -1)
    pscale = pscale.reshape(1, D)

    n_grp, grp_rows, grp_w = pool_w.shape[1:]
    ffn1_w = [(jnp.swapaxes(ffn1_w_gate, 1, 2), "gate"), (jnp.swapaxes(ffn1_w_up, 1, 2), "up"), (ffn1_w_down, "down")]
    ffn2_w = [(jnp.swapaxes(ffn2_w_gate, 1, 2), "gate"), (jnp.swapaxes(ffn2_w_up, 1, 2), "up"), (ffn2_w_down, "down")]
    groups = {
        "01": [(w, 0, f"ffn1_{k}") for w, k in ffn1_w],
        "mix": [(mix_w_in, 0, "w_in"), (mix_w_out, 0, "w_out")],
        "02": [(w, 0, f"ffn2_{k}") for w, k in ffn2_w],
        "11": [(w, 1, f"ffn1_{k}") for w, k in ffn1_w] + [(pool_w.reshape(1, n_grp * grp_rows, grp_w), 0, "pool_w")],
        "12": [(w, 1, f"ffn2_{k}") for w, k in ffn2_w],
    }
    started = {}

    def start(tag, after):
        lands = [cast_place(w, l, own_block, name=f"cast_{nm}_{l}") for w, l, nm in groups[tag]]
        started[tag] = gather_start(lands, after, name=f"gather_start_{tag}")
        return started[tag][-1][0:1, 0:1]

    def finish(tag, after):
        return gather_finish(gather_wait(started[tag], after, name=f"gather_wait_{tag}"), name=f"gather_finish_{tag}")

    lg = jnp.concatenate([jax.nn.log_sigmoid(ret_decay_fwd), jax.nn.log_sigmoid(ret_decay_bwd)], axis=0)
    cos, sin = _rope_tables(t_lat, t_ctx)

    def mod3(l, k, with_ctx=False, tie=None):
        rows = mod_x[l, 3 * k:3 * k + 3][None]
        if with_ctx:
            rows = jnp.concatenate([rows, mod_c[3 * k:3 * k + 3][None]], axis=0)
        return rows if tie is None else rows + tie

    tie = start("01", m_all)
    x0 = jnp.concatenate([x[0], ctx[0]], axis=0) + tie
    wg01, wu01, wd01 = finish("01", x0)
    tie = start("mix", wd01)
    x1, h1, a1, b1, y1 = ffn_fwd(x0, mod3(0, 0, True, tie), norm_ffn1[0:1], wg01, wu01, wd01,
                                 tm=tm, t_first=t_lat, name="ffn_fwd_01")
    w_in, w_out = finish("mix", h1)
    tie = start("02", w_out)
    w_out = w_out.reshape(D, D)
    p, hm = proj_in(x1, mod3(0, 1, True, tie), norm_mix[0:1], w_in, tm=768 if T % 768 == 0 else tm, t_first=t_lat,
                    name="proj_in")
    qr, kr, vr = ret_prep(p, cos, sin, heads=heads, tm=tm, name="ret_prep")
    o = ret_fwd(qr, kr, vr, lg, t_lat=t_lat, tq=tq, tk=tk, name="ret_fwd")
    wg02, wu02, wd02 = finish("02", o)
    tie = start("11", wd02)
    z = ret_post_fwd(o, p, heads=heads, d_model=D, name="ret_post_fwd")
    z = conv_fwd(z, p, w_conv, heads=heads, t_lat=t_lat, name="conv_fwd")
    x2, ym = out_proj(z, w_out, x1, mod3(0, 1, tie=tie), tm=tm, name="out_proj")
    x3, h3, a3, b3, y3 = ffn_fwd(x2, mod3(0, 2, tie=tie), norm_ffn2[0:1], wg02, wu02, wd02, tm=tmf, t_first=t_lat, name="ffn_fwd_02")
    wg11, wu11, wd11, pw = finish("11", h3)
    tie = start("12", wd11)
    pw = jnp.transpose(pw.reshape(N_DEV, n_grp, grp_rows, grp_w), (1, 0, 2, 3)).reshape(n_grp, N_DEV * grp_rows, grp_w)
    x4, h4, a4, b4, y4 = ffn_fwd(x3, mod3(1, 0, tie=tie), norm_ffn1[1:2], wg11, wu11, wd11, tm=tmf, t_first=t_lat, name="ffn_fwd_11")
    hp = norm_mod_fwd(x4, mod3(1, 1), norm_mix[1:2], tm=tm, name="pool_norm_fwd")
    x5, pooled, yl = pool_fwd(hp, x4, pw, pscale, mod3(1, 1), name="pool_fwd")
    wg12, wu12, wd12 = finish("12", yl)
    x6, h6, a6, b6, y6 = ffn_fwd(x5, mod3(1, 2), norm_ffn2[1:2], wg12, wu12, wd12, tm=tmf, t_first=t_lat, name="ffn_fwd_12")
    loss_part, dx6, red_fn = final_loss(x6, final_norm[None, :], loss_target[0], tm=tm, name="final_loss")
    loss = lax.psum(loss_part[0, 0], ("x", "y", "c"))

    reducing = {}

    def reduce_start(tag, grads, after=None):
        res = rs_sibling_exchange(grads, grads[0] if after is None else after, name=f"rs_sibling_{tag}")
        n = len(grads)
        stages, directs = res[:n], res[n:]
        sums = [chip_sum(g, st, peer_blocks, name=f"chip_sum_{tag}_{t}") for t, (g, st) in enumerate(zip(grads, stages))]
        handle = chip_exchange_start(sums, name=f"rs_start_{tag}")
        reducing[tag] = (grads, directs, handle)
        return handle[-1][0:1, 0:1]

    def reduce_begin(tag, grads, after):
        reducing[tag] = sibling_exchange_start(grads, after, name=f"sib_start_{tag}")
        return reducing[tag][-1][0:1, 0:1]

    def reduce_middle(tag, after):
        grads, stages, directs = sibling_exchange_wait(reducing[tag], after, name=f"sib_wait_{tag}")
        sums = [chip_sum(g, st, peer_blocks, name=f"chip_sum_{tag}_{t}") for t, (g, st) in enumerate(zip(grads, stages))]
        handle = chip_exchange_start(sums, name=f"rs_start_{tag}")
        reducing[tag] = (grads, directs, handle)
        return handle[-1][0:1, 0:1]

    dx5, da6, db6, s6, dy6, red12 = ffn_bwd_tok(x5, dx6, y6, a6, b6, mod3(1, 2), norm_ffn2[1:2], wg12, wu12, wd12,
                                                tm=tmf, t_first=t_lat, name="ffn_bwd_12")
    tie = reduce_begin("12", _ffn_grads(h6, da6, db6, s6, dy6, "12"), dx5)
    dhp, g_pw, red_pool = pool_bwd(dx5, pooled, yl, pw, pscale, mod3(1, 1, tie=tie), name="pool_bwd")
    dx4, red_pn = norm_mod_bwd(x4, dhp, dx5, mod3(1, 1), norm_mix[1:2], tm=tm, name="pool_norm_bwd")
    tie = reduce_middle("12", dx4)
    dx3, da4, db4, s4, dy4, red11 = ffn_bwd_tok(x3, dx4, y4, a4, b4, mod3(1, 0, tie=tie), norm_ffn1[1:2], wg11, wu11, wd11,
                                                tm=tmf, t_first=t_lat, name="ffn_bwd_11")
    g_pw = jnp.transpose(g_pw.reshape(n_grp, N_DEV, grp_rows, grp_w), (1, 0, 2, 3)).reshape(N_DEV, n_grp * grp_rows, grp_w)
    tie = reduce_begin("11", list(_ffn_grads(h4, da4, db4, s4, dy4, "11")) + [g_pw], dx3)
    dx2, da3, db3, s3, dy3, red02 = ffn_bwd_tok(x2, dx3, y3, a3, b3, mod3(0, 2, tie=tie), norm_ffn2[0:1], wg02, wu02, wd02,
                                                tm=tmf, t_first=t_lat, name="ffn_bwd_02")
    tie = reduce_middle("11", dx2)
    tie = tie + reduce_begin("02", _ffn_grads(h3, da3, db3, s3, dy3, "02"), dx2)
    dz, dym, red_op = out_proj_bwd(dx2, ym, w_out, mod3(0, 1, tie=tie), tm=tm, name="out_proj_bwd")
    g_wout = matmul_tn(z, dym, nb=N_DEV, m=D // N_DEV, n=D, a_blocked=False, b_blocked=False,
                       tk=t_lat, name="wgrad_out")
    do, dp = ret_post_bwd(dz, o, p, heads=heads, name="ret_post_bwd")
    dp, g_conv = conv_bwd(dp, dz, p, w_conv, heads=heads, t_lat=t_lat, name="conv_bwd")
    dq, dk, dv, dlg = ret_bwd(qr, kr, vr, do, lg, t_lat=t_lat, tq=tq, tk=tk, name="ret_bwd")
    tie = reduce_middle("02", dq)
    dp = ret_unprep(dp, dq, dk, dv, cos, sin, t_lat=t_lat, name="ret_unprep")
    dx1, red_mix = proj_bwd_tok(x1, dx2, dp, mod3(0, 1, True, tie), norm_mix[0:1], w_in, tm=tm, name="proj_bwd")
    g_win = matmul_tn(hm, dp, nb=N_DEV, m=D, n=w_in.shape[2], a_blocked=False, b_blocked=False, tk=T, name="wgrad_in")
    tie = reduce_start("mix", [g_win, g_wout])
    dx0, da1, db1, s1, dy1, red01 = ffn_bwd_tok(x0, dx1, y1, a1, b1, mod3(0, 0, True, tie), norm_ffn1[0:1], wg01, wu01, wd01,
                                                tm=tm, t_first=t_lat, name="ffn_bwd_01")
    res = {}

    dm_x = jnp.stack([jnp.concatenate([red01[0, 0:3], red_mix[0, 0:2], red_op[2:3], red02[0, 0:3]], axis=0),
                      jnp.concatenate([red11[0, 0:3], red_pn[0:2], red_pool[2:3], red12[0, 0:3]], axis=0)])
    dm_c = jnp.concatenate([red01[1, 0:3], red_mix[1, 0:2], jnp.zeros((4, D), F32)], axis=0)
    d_lg = dlg[:, 0:2, 0].T
    d_dec_f = d_lg[0:1] * jax.nn.sigmoid(-ret_decay_fwd)
    d_dec_b = d_lg[1:2] * jax.nn.sigmoid(-ret_decay_bwd)
    pieces = [dm_x.reshape(-1), dm_c.reshape(-1),
              jnp.stack([red01[0, 3] + red01[1, 3], red11[0, 3]]).reshape(-1),
              jnp.stack([red_mix[0, 3] + red_mix[1, 3], red_pn[3]]).reshape(-1),
              jnp.stack([red02[0, 3], red12[0, 3]]).reshape(-1),
              red_fn[0], d_dec_f.reshape(-1), d_dec_b.reshape(-1), g_conv.reshape(-1), red_pool[4]]
    sizes = [int(a.shape[0]) for a in pieces]
    n_pack = sum(sizes)
    n_pad = -n_pack % 8192
    packed = jnp.concatenate(pieces + [jnp.zeros((n_pad,), F32)])[None, :]
    (packed_all,) = all_gather([packed], name="gather_partials")
    total = sum_partials(packed_all, name="sum_partials")[0]
    offs = [0]
    for s in sizes:
        offs.append(offs[-1] + s)
    seg = [total[offs[i]:offs[i + 1]] for i in range(len(sizes))]
    g_dm = seg[0].reshape(2, N_MOD * D)
    g_dmc = seg[1].reshape(N_MOD * D)
    g_b_mod = g_dm.at[0].add(g_dmc)
    g_norm_ffn1, g_norm_mix, g_norm_ffn2 = (seg[k].reshape(2, D) for k in (2, 3, 4))
    g_final = seg[5]
    g_dec_f, g_dec_b = seg[6].reshape(1, heads), seg[7].reshape(1, heads)
    g_conv_all = seg[8].reshape(CONV_WIDTH, -1)
    g_pscale_all = seg[9]
    conv_w = mix_w_conv.shape[2]
    g_w_conv = lax.dynamic_slice_in_dim(g_conv_all, me * conv_w, conv_w, axis=1)[None]
    ps_w = pool_scale.shape[1]
    g_pool_scale = lax.dynamic_slice_in_dim(g_pscale_all, me * ps_w, ps_w, axis=0)[None]

    dm_rows = packed_all[:, 0, 0:offs[1]].reshape(N_DEV, 2, N_MOD * D)
    dmc_rows = packed_all[:, 0, offs[1]:offs[2]].reshape(N_DEV, N_MOD * D)
    dm_x_blk = jnp.transpose(lax.dynamic_slice_in_dim(dm_rows, me * mod_w, mod_w, axis=2), (1, 0, 2))
    dm_c_blk = jnp.stack([lax.dynamic_slice_in_dim(dmc_rows, me * mod_w, mod_w, axis=1),
                          jnp.zeros((N_DEV, mod_w), F32)])
    g_w_mod, d_w_mod, nm_w_mod, nv_w_mod, cctx_part = adaln_bwd(cond16, dm_x_blk, dm_c_blk, w_mod, m_w_mod, v_w_mod,
                                                                name="adaln_bwd")
    cpad = jnp.concatenate([cctx_part[0], jnp.zeros((8192 - D,), F32)])[None, :] if D < 8192 else cctx_part[0:1]
    (cctx_all,) = all_gather([cpad], name="gather_cctx")
    g_c_ctx = sum_partials(cctx_all, name="sum_cctx")[0, :D]

    small = [("c_ctx", g_c_ctx, c_ctx, m_c_ctx, v_c_ctx), ("b_mod", g_b_mod, b_mod, m_b_mod, v_b_mod),
             ("norm_ffn1", g_norm_ffn1, norm_ffn1, m_norm_ffn1, v_norm_ffn1),
             ("norm_mix", g_norm_mix, norm_mix, m_norm_mix, v_norm_mix),
             ("norm_ffn2", g_norm_ffn2, norm_ffn2, m_norm_ffn2, v_norm_ffn2),
             ("mix_w_conv", g_w_conv, mix_w_conv, m_mix_w_conv, v_mix_w_conv),
             ("ret_decay_fwd", g_dec_f, ret_decay_fwd, m_ret_decay_fwd, v_ret_decay_fwd),
             ("ret_decay_bwd", g_dec_b, ret_decay_bwd, m_ret_decay_bwd, v_ret_decay_bwd),
             ("pool_scale", g_pool_scale, pool_scale, m_pool_scale, v_pool_scale),
             ("final_norm", g_final, final_norm, m_final_norm, v_final_norm)]
    ssz = [int(a[1].size) for a in small]
    spad = -sum(ssz) % LANE

    def pack(k):
        return jnp.concatenate([a[k].reshape(-1) for a in small] + [jnp.ones((spad,), F32)])[None, :]

    sd, sm, sv = adam_plain(pack(1), pack(2), pack(3), pack(4), name="adam_small")
    soff = [0]
    for s in ssz:
        soff.append(soff[-1] + s)
    for i, (nm, g, w, _, _) in enumerate(small):
        res[nm] = [g.reshape(w.shape)] + [a[0, soff[i]:soff[i + 1]].reshape(w.shape) for a in (sd, sm, sv)]
    res["w_mod"] = [g_w_mod, d_w_mod, nm_w_mod, nv_w_mod]

    after = sd
    for kind, lhs, rhs in (("gate", da1, h1), ("up", db1, h1), ("down", s1, dy1)):
        g = matmul_tn(lhs, rhs, nb=N_DEV, m=lhs.shape[2], n=D, a_blocked=True, b_blocked=False, tk=T,
                      name=f"wgrad_{kind}_01", after=after)
        tie = reduce_start(f"01_{kind}", [g], g)
        after = reducing[f"01_{kind}"][2][2]
    grad_x = dx0[:t_lat][None] + tie

    last = [grad_x]

    def reduce_finish(tag):
        grads, directs, handle = reducing[tag]
        landed = chip_exchange_wait(handle, last[0], name=f"rs_wait_{tag}")
        return list(zip(grads, directs, landed))

    def big(w, m, v, l, part, prev, nm, transposed=False):
        if transposed:
            w, m, v = (jnp.swapaxes(a, 1, 2) for a in (w, m, v))
        outs = adam_reduced(own_block, part[0], part[1], part[2], w, m, v, l, prev, name=nm)
        last[0] = outs[0]
        return outs

    ffn1 = [(ffn1_w_gate, m_ffn1_w_gate, v_ffn1_w_gate), (ffn1_w_up, m_ffn1_w_up, v_ffn1_w_up),
            (ffn1_w_down, m_ffn1_w_down, v_ffn1_w_down)]
    ffn2 = [(ffn2_w_gate, m_ffn2_w_gate, v_ffn2_w_gate), (ffn2_w_up, m_ffn2_w_up, v_ffn2_w_up),
            (ffn2_w_down, m_ffn2_w_down, v_ffn2_w_down)]
    kinds = ["gate", "up", "down"]
    half = {}
    parts = reduce_finish("12")
    for t, (w, m, v) in enumerate(ffn2):
        half[f"ffn2_w_{kinds[t]}"] = big(w, m, v, 1, parts[t], None, f"adam_ffn2_w_{kinds[t]}_1", transposed=t < 2)
    parts = reduce_finish("11")
    for t, (w, m, v) in enumerate(ffn1):
        half[f"ffn1_w_{kinds[t]}"] = big(w, m, v, 1, parts[t], None, f"adam_ffn1_w_{kinds[t]}_1", transposed=t < 2)
    res["pool_w"] = [a.reshape(pool_w.shape) for a in
                     big(pool_w.reshape(1, n_grp * grp_rows, grp_w), m_pool_w.reshape(1, n_grp * grp_rows, grp_w),
                         v_pool_w.reshape(1, n_grp * grp_rows, grp_w), 0, parts[3], None, "adam_pool_w")]
    parts = reduce_finish("02")
    for t, (w, m, v) in enumerate(ffn2):
        nm = f"ffn2_w_{kinds[t]}"
        outs = big(w, m, v, 0, parts[t], half[nm], f"adam_{nm}_0", transposed=t < 2)
        res[nm] = [jnp.swapaxes(a, 1, 2) for a in outs] if t < 2 else outs
    parts = reduce_finish("mix")
    res["mix_w_in"] = big(mix_w_in, m_mix_w_in, v_mix_w_in, 0, parts[0], None, "adam_mix_w_in")
    res["mix_w_out"] = big(mix_w_out, m_mix_w_out, v_mix_w_out, 0, parts[1], None, "adam_mix_w_out")
    for t, (w, m, v) in enumerate(ffn1):
        nm = f"ffn1_w_{kinds[t]}"
        outs = big(w, m, v, 0, reduce_finish(f"01_{kinds[t]}")[0], half[nm], f"adam_{nm}_0", transposed=t < 2)
        res[nm] = [jnp.swapaxes(a, 1, 2) for a in outs] if t < 2 else outs

    order = ["c_ctx", "w_mod", "b_mod", "norm_ffn1", "norm_mix", "norm_ffn2", "ffn1_w_gate", "ffn1_w_up", "ffn1_w_down",
             "ffn2_w_gate", "ffn2_w_up", "ffn2_w_down", "mix_w_in", "mix_w_conv", "mix_w_out", "ret_decay_fwd",
             "ret_decay_bwd", "pool_w", "pool_scale", "final_norm"]
    return (loss, grad_x, *[res[n][0] for n in order], *[res[n][1] for n in order],
            *[res[n][2] for n in order], *[res[n][3] for n in order])
```

```python
import jax
import jax.numpy as jnp
from jax import lax
from jax.experimental import pallas as pl
from jax.experimental.pallas import tpu as pltpu

F32 = jnp.float32
BF16 = jnp.bfloat16
MESH = pl.DeviceIdType.MESH

N_DEV = 8
N_MOD = 9
EPS = 1e-6
GN_EPS = 1e-5
MACARON = 0.5
HEAD_DIM = 128
K_SCALE = HEAD_DIM ** -0.5
ROPE_BASE = 10000.0
GRID_W = 64
CONV_WIDTH = 3
POOL_WINDOWS = (2, 4, 8, 16)
POOL_PAD = 16

ADAM_LR = 0.001
ADAM_B1 = 0.9
ADAM_B2 = 0.999
ADAM_EPS = 1e-08
ADAM_WD = 0.01
ADAM_STEP = 10

LANE = 128
ROW_CHUNK = 128
MM_ROWS = 256
VMEM_LIMIT = 56 * 1024 * 1024
ANY = pl.BlockSpec(memory_space=pl.ANY)


def _params(sem=None):
    kw = dict(vmem_limit_bytes=VMEM_LIMIT)
    if sem is not None:
        kw["dimension_semantics"] = sem
    return pltpu.CompilerParams(**kw)


def _nt(a, b):
    return lax.dot_general(a, b, (((1,), (1,)), ((), ())), preferred_element_type=F32)


def _tn(a, b):
    return lax.dot_general(a, b, (((0,), (0,)), ((), ())), preferred_element_type=F32)


def _nn(a, b):
    return jnp.dot(a, b, preferred_element_type=F32)


def _silu_parts(a):
    sig = jax.nn.sigmoid(a)
    return sig, a * sig


def _dsilu(a, sig):
    return sig * (1.0 + a * (1.0 - sig))


def _norm_mod(x, g, shift, scale):
    r = lax.rsqrt(jnp.mean(x * x, axis=-1, keepdims=True) + EPS)
    return (x * r * g) * (1.0 + scale) + shift


def _norm_mod_bwd(x, dh, g, scale):
    r = lax.rsqrt(jnp.mean(x * x, axis=-1, keepdims=True) + EPS)
    xhat = x * r
    dn = dh * (1.0 + scale)
    dshift = jnp.sum(dh, axis=0, keepdims=True)
    dscale = jnp.sum(dh * (xhat * g), axis=0, keepdims=True)
    dg = jnp.sum(dn * xhat, axis=0, keepdims=True)
    dxh = dn * g
    dx = r * (dxh - xhat * jnp.mean(dxh * xhat, axis=-1, keepdims=True))
    return dx, dshift, dscale, dg


def _mod_row(mod_ref, k, row0, nrows, t_first):
    if mod_ref.shape[0] == 1:
        return mod_ref[0, k:k + 1, :]
    row = row0 + lax.broadcasted_iota(jnp.int32, (nrows, 1), 0)
    return jnp.where(row >= t_first, mod_ref[1, k:k + 1, :], mod_ref[0, k:k + 1, :])


def _group_index(i, tiles_first, n_groups):
    if n_groups == 1:
        return 0
    return jnp.where(i >= tiles_first, 1, 0)


def _mesh_pos():
    x, y, c = lax.axis_index("x"), lax.axis_index("y"), lax.axis_index("c")
    chips = [(1 - x, y), (x, 1 - y), (1 - x, 1 - y)]
    return x, y, c, chips


def _flat(px, py, pc):
    return 4 * px + 2 * py + pc


def all_gather(shards, name):
    n = len(shards)

    def body(*refs):
        ins, outs = refs[:n], refs[n:2 * n]
        send_sems, recv_sems, local_sems = refs[2 * n:]
        x, y, c, chips = _mesh_pos()
        me, sibling = (x, y, c), (x, y, 1 - c)

        def copy(t, k, block, to, src=None):
            dst = outs[t].at[_flat(*block)]
            return pltpu.make_async_remote_copy(
                src_ref=dst if src is None else src, dst_ref=dst,
                send_sem=send_sems.at[t, k], recv_sem=recv_sems.at[t, k],
                device_id=to, device_id_type=MESH)

        mine = [pltpu.make_async_copy(ins[t], outs[t].at[_flat(*me)], local_sems.at[t]) for t in range(n)]
        for cp in mine:
            cp.start()
        first = []
        for t in range(n):
            first.append(copy(t, 0, me, sibling, src=ins[t]))
            first += [copy(t, 1 + j, me, (*chip, c), src=ins[t]) for j, chip in enumerate(chips)]
        for cp in first:
            cp.start()
        passed = []
        for t in range(n):
            for j, chip in enumerate(chips):
                copy(t, 1 + j, (*chip, c), me).wait_recv()
                fwd = copy(t, 4 + j, (*chip, c), sibling)
                fwd.start()
                passed.append(fwd)
        for t in range(n):
            copy(t, 0, sibling, me).wait_recv()
            for j, chip in enumerate(chips):
                copy(t, 4 + j, (*chip, 1 - c), me).wait_recv()
        for cp in first + passed:
            cp.wait_send()
        for cp in mine:
            cp.wait()

    return pl.pallas_call(
        body, name=name,
        out_shape=[jax.ShapeDtypeStruct((N_DEV,) + s.shape, s.dtype) for s in shards],
        in_specs=[ANY] * n, out_specs=[ANY] * n,
        scratch_shapes=[pltpu.SemaphoreType.DMA((n, 7)), pltpu.SemaphoreType.DMA((n, 7)), pltpu.SemaphoreType.DMA((n,))],
    )(*shards)


HBM_SPEC = pl.BlockSpec(memory_space=pltpu.HBM)
SEM_SPEC = pl.BlockSpec(memory_space=pltpu.SEMAPHORE)
DATAFLOW = pltpu.SideEffectType.DATAFLOW_SIDE_EFFECTING


def _in_hbm(a):
    return pltpu.with_memory_space_constraint(a, pltpu.HBM)


def _push_peers():
    x, y, c, chips = _mesh_pos()
    return [(*chip, c) for chip in chips] + [(x, y, 1 - c)]


def cast_place(w, layer, own_block, name):
    _, R, C = w.shape
    tr = _row_tile(R, C)

    def body(idx_ref, w_ref, o_ref):
        o_ref[...] = w_ref[...].astype(BF16)

    return pl.pallas_call(
        body, name=name,
        grid_spec=pltpu.PrefetchScalarGridSpec(
            num_scalar_prefetch=1, grid=(R // tr,),
            in_specs=[pl.BlockSpec((None, tr, C), lambda i, idx: (layer, i, 0))],
            out_specs=pl.BlockSpec((None, tr, C), lambda i, idx: (idx[0], i, 0))),
        out_shape=jax.ShapeDtypeStruct((N_DEV, R, C), BF16),
        compiler_params=_params(("arbitrary",)),
    )(own_block, w)


def gather_start(lands, after, name):
    n = len(lands)

    def body(*refs):
        lz = refs[:n]
        send_sems, recv_sems = refs[n + 1], refs[n + 2]
        token = refs[-1]
        x, y, c, _ = _mesh_pos()
        for t in range(n):
            mine = lz[t].at[_flat(x, y, c)]
            for k, peer in enumerate(_push_peers()):
                pltpu.make_async_remote_copy(
                    src_ref=mine, dst_ref=mine, send_sem=send_sems.at[4 * t + k],
                    recv_sem=recv_sems.at[4 * t + k], device_id=peer, device_id_type=MESH).start()
        token[...] = jnp.zeros_like(token)

    return pl.pallas_call(
        body, name=name,
        out_shape=(pltpu.SemaphoreType.DMA((4 * n,)), pltpu.SemaphoreType.DMA((4 * n,)),
                   *[pltpu.HBM(l.shape, l.dtype) for l in lands], jax.ShapeDtypeStruct((8, LANE), F32)),
        in_specs=[HBM_SPEC] * n + [ANY],
        out_specs=(SEM_SPEC, SEM_SPEC, *[HBM_SPEC] * n, pl.BlockSpec(memory_space=pltpu.VMEM)),
        input_output_aliases={i: 2 + i for i in range(n)},
        compiler_params=pltpu.CompilerParams(has_side_effects=DATAFLOW),
    )(*[_in_hbm(l) for l in lands], after)


def gather_wait(handle, after, name):
    send, recv = handle[0], handle[1]
    bufs = handle[2:-1]
    n = len(bufs)

    def body(*refs):
        lz = refs[:n]
        send_sems, recv_sems = refs[n], refs[n + 1]
        x, y, c, _ = _mesh_pos()
        for t in range(n):
            for k, peer in enumerate(_push_peers()):
                cp = pltpu.make_async_remote_copy(
                    src_ref=lz[t].at[_flat(x, y, c)], dst_ref=lz[t].at[_flat(*peer)], send_sem=send_sems.at[4 * t + k],
                    recv_sem=recv_sems.at[4 * t + k], device_id=peer, device_id_type=MESH)
                cp.wait_send()
                cp.wait_recv()

    outs = pl.pallas_call(
        body, name=name,
        out_shape=tuple(pltpu.HBM(b.shape, b.dtype) for b in bufs),
        in_specs=[HBM_SPEC] * n + [SEM_SPEC, SEM_SPEC, ANY],
        out_specs=tuple([HBM_SPEC] * n),
        input_output_aliases={i: i for i in range(n)},
        compiler_params=pltpu.CompilerParams(has_side_effects=DATAFLOW),
    )(*bufs, send, recv, after)
    return list(outs)


def gather_finish(lands, name):
    n = len(lands)

    def body(*refs):
        outs = refs[n:2 * n]
        send_sems, recv_sems = refs[2 * n:]
        x, y, c, chips = _mesh_pos()
        sibling = (x, y, 1 - c)

        def copy(t, j, core):
            blk = outs[t].at[_flat(*chips[j], core)]
            return pltpu.make_async_remote_copy(
                src_ref=blk, dst_ref=blk, send_sem=send_sems.at[t, j], recv_sem=recv_sems.at[t, j],
                device_id=sibling, device_id_type=MESH)

        sends = [copy(t, j, c) for t in range(n) for j in range(3)]
        for cp in sends:
            cp.start()
        for t in range(n):
            for j in range(3):
                copy(t, j, 1 - c).wait_recv()
        for cp in sends:
            cp.wait_send()

    return pl.pallas_call(
        body, name=name,
        out_shape=[jax.ShapeDtypeStruct(l.shape, l.dtype) for l in lands],
        in_specs=[ANY] * n, out_specs=[ANY] * n,
        input_output_aliases={t: t for t in range(n)},
        scratch_shapes=[pltpu.SemaphoreType.DMA((n, 3)), pltpu.SemaphoreType.DMA((n, 3))],
    )(*lands)


def chip_exchange_start(sums, name):
    n = len(sums)
    lands = [lax.empty(s.shape, s.dtype) for s in sums]

    def body(*refs):
        ins, lz = refs[:n], refs[n:2 * n]
        send_sems, recv_sems = refs[2 * n], refs[2 * n + 1]
        token = refs[-1]
        peers = _push_peers()
        for t in range(n):
            for j in range(3):
                pltpu.make_async_remote_copy(
                    src_ref=ins[t].at[j], dst_ref=lz[t].at[j], send_sem=send_sems.at[3 * t + j],
                    recv_sem=recv_sems.at[3 * t + j], device_id=peers[j], device_id_type=MESH).start()
        token[...] = jnp.zeros_like(token)

    return pl.pallas_call(
        body, name=name,
        out_shape=(pltpu.SemaphoreType.DMA((3 * n,)), pltpu.SemaphoreType.DMA((3 * n,)),
                   *[pltpu.HBM(s.shape, s.dtype) for s in sums], *[pltpu.HBM(s.shape, s.dtype) for s in sums],
                   jax.ShapeDtypeStruct((8, LANE), F32)),
        in_specs=[HBM_SPEC] * (2 * n),
        out_specs=(SEM_SPEC, SEM_SPEC, *[HBM_SPEC] * (2 * n), pl.BlockSpec(memory_space=pltpu.VMEM)),
        input_output_aliases={i: 2 + i for i in range(2 * n)},
        compiler_params=pltpu.CompilerParams(has_side_effects=DATAFLOW),
    )(*[_in_hbm(s) for s in sums], *[_in_hbm(l) for l in lands])


def chip_exchange_wait(handle, after, name):
    send, recv = handle[0], handle[1]
    n = (len(handle) - 3) // 2
    bufs = handle[2:2 + 2 * n]

    def body(*refs):
        ins, lz = refs[:n], refs[n:2 * n]
        send_sems, recv_sems = refs[2 * n], refs[2 * n + 1]
        peers = _push_peers()
        for t in range(n):
            for j in range(3):
                cp = pltpu.make_async_remote_copy(
                    src_ref=ins[t].at[j], dst_ref=lz[t].at[j], send_sem=send_sems.at[3 * t + j],
                    recv_sem=recv_sems.at[3 * t + j], device_id=peers[j], device_id_type=MESH)
                cp.wait_send()
                cp.wait_recv()

    outs = pl.pallas_call(
        body, name=name,
        out_shape=tuple(pltpu.HBM(b.shape, b.dtype) for b in bufs),
        in_specs=[HBM_SPEC] * (2 * n) + [SEM_SPEC, SEM_SPEC, ANY],
        out_specs=tuple([HBM_SPEC] * (2 * n)),
        input_output_aliases={i: i for i in range(2 * n)},
        compiler_params=pltpu.CompilerParams(has_side_effects=DATAFLOW),
    )(*bufs, send, recv, after)
    return list(outs[n:])


def _sibling_copy(grads, stages, directs, send_sems, recv_sems, t, k):
    x, y, c, chips = _mesh_pos()
    if k < 3:
        src, dst = grads[t].at[_flat(*chips[k], 1 - c)], stages[t].at[k]
    else:
        src, dst = grads[t].at[_flat(x, y, 1 - c)], directs[t]
    return pltpu.make_async_remote_copy(
        src_ref=src, dst_ref=dst, send_sem=send_sems.at[4 * t + k], recv_sem=recv_sems.at[4 * t + k],
        device_id=(x, y, 1 - c), device_id_type=MESH)


def sibling_exchange_start(grads, after, name):
    n = len(grads)
    stages = [lax.empty((3,) + g.shape[1:], g.dtype) for g in grads]
    directs = [lax.empty(g.shape[1:], g.dtype) for g in grads]

    def body(*refs):
        send_sems, recv_sems = refs[3 * n + 1], refs[3 * n + 2]
        for t in range(n):
            for k in range(4):
                _sibling_copy(refs[:n], refs[n:2 * n], refs[2 * n:3 * n], send_sems, recv_sems, t, k).start()
        refs[-1][...] = jnp.zeros_like(refs[-1])

    bufs = list(grads) + stages + directs
    return pl.pallas_call(
        body, name=name,
        out_shape=(pltpu.SemaphoreType.DMA((4 * n,)), pltpu.SemaphoreType.DMA((4 * n,)),
                   *[pltpu.HBM(b.shape, b.dtype) for b in bufs], jax.ShapeDtypeStruct((8, LANE), F32)),
        in_specs=[HBM_SPEC] * (3 * n) + [ANY],
        out_specs=(SEM_SPEC, SEM_SPEC, *[HBM_SPEC] * (3 * n), pl.BlockSpec(memory_space=pltpu.VMEM)),
        input_output_aliases={i: 2 + i for i in range(3 * n)},
        compiler_params=pltpu.CompilerParams(has_side_effects=DATAFLOW),
    )(*[_in_hbm(b) for b in bufs], after)


def sibling_exchange_wait(handle, after, name):
    send, recv = handle[0], handle[1]
    bufs = handle[2:-1]
    n = len(bufs) // 3

    def body(*refs):
        send_sems, recv_sems = refs[3 * n], refs[3 * n + 1]
        for t in range(n):
            for k in range(4):
                cp = _sibling_copy(refs[:n], refs[n:2 * n], refs[2 * n:3 * n], send_sems, recv_sems, t, k)
                cp.wait_send()
                cp.wait_recv()

    outs = pl.pallas_call(
        body, name=name,
        out_shape=tuple(pltpu.HBM(b.shape, b.dtype) for b in bufs),
        in_specs=[HBM_SPEC] * (3 * n) + [SEM_SPEC, SEM_SPEC, ANY],
        out_specs=tuple([HBM_SPEC] * (3 * n)),
        input_output_aliases={i: i for i in range(3 * n)},
        compiler_params=pltpu.CompilerParams(has_side_effects=DATAFLOW),
    )(*bufs, send, recv, after)
    return list(outs[:n]), list(outs[n:2 * n]), list(outs[2 * n:])


def rs_sibling_exchange(grads, after, name):
    n = len(grads)

    def body(*refs):
        ins, stages, directs = refs[:n], refs[n + 1:2 * n + 1], refs[2 * n + 1:3 * n + 1]
        send_sems, recv_sems = refs[3 * n + 1:]
        x, y, c, chips = _mesh_pos()
        sibling = (x, y, 1 - c)

        def copy(t, k):
            if k < 3:
                src, dst = ins[t].at[_flat(*chips[k], 1 - c)], stages[t].at[k]
            else:
                src, dst = ins[t].at[_flat(x, y, 1 - c)], directs[t]
            return pltpu.make_async_remote_copy(
                src_ref=src, dst_ref=dst, send_sem=send_sems.at[t, k], recv_sem=recv_sems.at[t, k],
                device_id=sibling, device_id_type=MESH)

        cps = [copy(t, k) for t in range(n) for k in range(4)]
        for cp in cps:
            cp.start()
        for cp in cps:
            cp.wait_recv()
        for cp in cps:
            cp.wait_send()

    return pl.pallas_call(
        body, name=name,
        out_shape=[jax.ShapeDtypeStruct((3,) + g.shape[1:], g.dtype) for g in grads]
        + [jax.ShapeDtypeStruct(g.shape[1:], g.dtype) for g in grads],
        in_specs=[ANY] * (n + 1), out_specs=[ANY] * (2 * n),
        scratch_shapes=[pltpu.SemaphoreType.DMA((n, 4)), pltpu.SemaphoreType.DMA((n, 4))],
    )(*grads, after)


def _row_tile(rows, cols, limit_bytes=3 << 19):
    best = None
    for t in range(16, rows + 1, 16):
        if rows % t == 0 and t * cols * 4 <= limit_bytes:
            best = t
    return best if best is not None else rows


def chip_sums(grads, stages, peer_blocks, name):
    n = len(grads)
    _, R, C = grads[0].shape
    tr = _row_tile(R, C)

    def body(idx_ref, *refs):
        for t in range(n):
            refs[2 * n + t][...] = (refs[t][...].astype(F32) + refs[n + t][...].astype(F32)).astype(BF16)

    own = pl.BlockSpec((None, tr, C), lambda j, i, idx: (idx[j], i, 0))
    slot = pl.BlockSpec((None, tr, C), lambda j, i, idx: (j, i, 0))
    return pl.pallas_call(
        body, name=name,
        grid_spec=pltpu.PrefetchScalarGridSpec(
            num_scalar_prefetch=1, grid=(3, R // tr), in_specs=[own] * n + [slot] * n, out_specs=[slot] * n),
        out_shape=[jax.ShapeDtypeStruct((3, R, C), BF16)] * n,
        compiler_params=_params(("arbitrary", "arbitrary")),
    )(peer_blocks, *grads, *stages)


def chip_sum_all(grads, stages, peer_blocks, tag):
    out, by_shape = [None] * len(grads), {}
    for t, g in enumerate(grads):
        by_shape.setdefault(g.shape, []).append(t)
    for k, ts in enumerate(by_shape.values()):
        res = chip_sums([grads[t] for t in ts], [stages[t] for t in ts], peer_blocks, name=f"chip_sum_{tag}_{k}")
        for t, r in zip(ts, res):
            out[t] = r
    return out


def _adamw(w, g, m, v):
    m2 = ADAM_B1 * m + (1.0 - ADAM_B1) * g
    v2 = ADAM_B2 * v + (1.0 - ADAM_B2) * (g * g)
    m_hat = m2 / (1.0 - ADAM_B1 ** ADAM_STEP)
    v_hat = v2 / (1.0 - ADAM_B2 ** ADAM_STEP)
    delta = -ADAM_LR * (m_hat / (jnp.sqrt(v_hat) + ADAM_EPS) + ADAM_WD * w)
    return delta, m2, v2


def adam_reduced(own_block, grad, direct, landed, w, m, v, layer, prev, name):
    L, R, C = w.shape
    tr = _row_tile(R, C, 1 << 20)
    first = prev is None

    def body(idx_ref, g_ref, d_ref, l_ref, w_ref, m_ref, v_ref, *rest):
        og, od, om, ov = rest[-4:]
        g = g_ref[...].astype(F32) + d_ref[...].astype(F32)
        for j in range(3):
            g = g + l_ref[j].astype(F32)
        delta, m2, v2 = _adamw(w_ref[...], g, m_ref[...], v_ref[...])
        og[...] = g
        od[...] = delta
        om[...] = m2
        ov[...] = v2

    lay = pl.BlockSpec((None, tr, C), lambda i, idx: (layer, i, 0))
    in_specs = [pl.BlockSpec((None, tr, C), lambda i, idx: (idx[0], i, 0)),
                pl.BlockSpec((tr, C), lambda i, idx: (i, 0)),
                pl.BlockSpec((3, tr, C), lambda i, idx: (0, i, 0)),
                lay, lay, lay]
    args = [own_block, grad, direct, landed, w, m, v]
    aliases = {}
    if not first:
        in_specs += [ANY] * 4
        args += list(prev)
        aliases = {7 + k: k for k in range(4)}
    return pl.pallas_call(
        body, name=name,
        grid_spec=pltpu.PrefetchScalarGridSpec(
            num_scalar_prefetch=1, grid=(R // tr,), in_specs=in_specs, out_specs=[lay] * 4),
        out_shape=[jax.ShapeDtypeStruct((L, R, C), F32)] * 4,
        input_output_aliases=aliases,
        compiler_params=_params(("arbitrary",)),
    )(*args)


def sum_partials(parts, name):
    _, _, N = parts.shape
    tn = 8192

    def body(p_ref, o_ref):
        g = p_ref[0]
        for k in range(1, N_DEV):
            g = g + p_ref[k]
        o_ref[...] = g

    return pl.pallas_call(
        body, name=name, grid=(N // tn,),
        in_specs=[pl.BlockSpec((N_DEV, 1, tn), lambda i: (0, 0, i))],
        out_specs=pl.BlockSpec((1, tn), lambda i: (0, i)),
        out_shape=jax.ShapeDtypeStruct((1, N), F32),
    )(parts)


def adam_plain(g, w, m, v, name):
    _, N = g.shape

    def body(g_ref, w_ref, m_ref, v_ref, od, om, ov):
        delta, m2, v2 = _adamw(w_ref[...], g_ref[...], m_ref[...], v_ref[...])
        od[...] = delta
        om[...] = m2
        ov[...] = v2

    return pl.pallas_call(
        body, name=name, out_shape=[jax.ShapeDtypeStruct((1, N), F32)] * 3,
    )(g, w, m, v)


def adaln_fwd(cond16, w_mod, b_blk, name):
    L, D, W = w_mod.shape
    tn = 768 if W % 768 == 0 else W

    def body(c_ref, w_ref, b_ref, o_ref):
        c = c_ref[...]
        sc = (c * jax.nn.sigmoid(c)).astype(BF16)
        o_ref[...] = _nn(sc, w_ref[...].astype(BF16)) + b_ref[...]

    return pl.pallas_call(
        body, name=name, grid=(L, W // tn),
        in_specs=[pl.BlockSpec((16, D), lambda l, i: (0, 0)),
                  pl.BlockSpec((None, D, tn), lambda l, i: (l, 0, i)),
                  pl.BlockSpec((None, 1, tn), lambda l, i: (l, 0, i))],
        out_specs=pl.BlockSpec((None, 16, tn), lambda l, i: (l, 0, i)),
        out_shape=jax.ShapeDtypeStruct((L, 16, W), F32),
        compiler_params=_params(("arbitrary", "arbitrary")),
    )(cond16, w_mod, b_blk)


def adaln_bwd(cond16, dm_x, dm_c, w_mod, m_mod, v_mod, name):
    L, D, W = w_mod.shape
    tn = 256 if W % 256 == 0 else W
    nt = W // tn

    def body(c_ref, cT_ref, dx_ref, dc_ref, w_ref, m_ref, v_ref, og, od, om, ov, pc_ref):
        l, i = pl.program_id(0), pl.program_id(1)
        c = c_ref[...]
        sig, sl = _silu_parts(c)
        cT = cT_ref[...]
        sigT = jax.nn.sigmoid(cT)
        scT = (cT * sigT).astype(BF16)
        dmc = jnp.sum(dc_ref[...], axis=0, keepdims=True)
        dm16 = jnp.concatenate([dx_ref[...], jnp.broadcast_to(dmc, (8, tn))], axis=0)
        row = lax.broadcasted_iota(jnp.int32, (16, tn), 0)
        dm16 = jnp.where(row <= 8, dm16, 0.0).astype(BF16)
        w = w_ref[...]
        g = _nn(scT, dm16)
        delta, m2, v2 = _adamw(w, g, m_ref[...], v_ref[...])
        og[...] = g
        od[...] = delta
        om[...] = m2
        ov[...] = v2

        @pl.when((l == 0) & (i == 0))
        def _():
            pc_ref[...] = jnp.zeros_like(pc_ref)

        @pl.when(l == 0)
        def _():
            back = _nt(jnp.broadcast_to(dmc, (8, tn)).astype(BF16), w.astype(BF16))
            pc_ref[...] += back * _dsilu(c[8:9, :], sig[8:9, :])

    col = pl.BlockSpec((None, D, tn), lambda l, i: (l, 0, i))
    row8 = pl.BlockSpec((None, 8, tn), lambda l, i: (l, 0, i))
    return pl.pallas_call(
        body, name=name, grid=(L, nt),
        in_specs=[pl.BlockSpec((16, D), lambda l, i: (0, 0)), pl.BlockSpec((D, 16), lambda l, i: (0, 0)),
                  row8, row8, col, col, col],
        out_specs=[col, col, col, col, pl.BlockSpec((8, D), lambda l, i: (0, 0))],
        out_shape=[jax.ShapeDtypeStruct((L, D, W), F32)] * 4 + [jax.ShapeDtypeStruct((8, D), F32)],
        compiler_params=_params(("arbitrary", "arbitrary")),
    )(cond16, cond16.T, dm_x, dm_c, w_mod, m_mod, v_mod)


def _token_spec(tm, D):
    if tm > 256:
        return pl.BlockSpec((tm, D), lambda i, j: (i, 0), pipeline_mode=pl.Buffered(1))
    return pl.BlockSpec((tm, D), lambda i, j: (i, 0))


def ffn_fwd(x, mod, gn, wg, wu, wd, *, tm, t_first, name):
    T, D = x.shape
    nb, cw, _ = wg.shape
    G = mod.shape[0]
    nt = T // tm
    mm_rows = MM_ROWS if tm % MM_ROWS == 0 else tm // 2

    def body(x_ref, mod_ref, g_ref, wg_ref, wu_ref, wd_ref, xo_ref, h_ref, a_ref, b_ref, y_ref, hs, acc, a_s, b_s, s_s):
        i, j = pl.program_id(0), pl.program_id(1)

        @pl.when(j == 0)
        def _():
            def head(r, carry):
                rows = pl.ds(pl.multiple_of(r * ROW_CHUNK, ROW_CHUNK), ROW_CHUNK)
                row0 = i * tm + r * ROW_CHUNK
                hb = _norm_mod(x_ref[rows, :], g_ref[...], _mod_row(mod_ref, 0, row0, ROW_CHUNK, t_first),
                               _mod_row(mod_ref, 1, row0, ROW_CHUNK, t_first)).astype(BF16)
                hs[rows, :] = hb
                h_ref[rows, :] = hb
                return carry

            lax.fori_loop(0, tm // ROW_CHUNK, head, 0)
            acc[...] = jnp.zeros_like(acc)

        parts = [slice(r, r + mm_rows) for r in range(0, tm, mm_rows)]
        for rows in parts:
            a_s[rows, :] = _nt(hs[rows, :], wg_ref[...])
            b_s[rows, :] = _nt(hs[rows, :], wu_ref[...])
        for r in range(0, tm, ROW_CHUNK):
            rows = slice(r, r + ROW_CHUNK)
            av, bv = a_s[rows, :], b_s[rows, :]
            a_ref[rows, :] = av.astype(BF16)
            b_ref[rows, :] = bv.astype(BF16)
            _, sl = _silu_parts(av)
            s_s[rows, :] = (sl * bv).astype(BF16)
        for rows in parts:
            acc[rows, :] += _nn(s_s[rows, :], wd_ref[...])

        @pl.when(j == nb - 1)
        def _():
            def tail(r, carry):
                rows = pl.ds(pl.multiple_of(r * ROW_CHUNK, ROW_CHUNK), ROW_CHUNK)
                y = acc[rows, :]
                y_ref[rows, :] = y.astype(BF16)
                gate = _mod_row(mod_ref, 2, i * tm + r * ROW_CHUNK, ROW_CHUNK, t_first)
                xo_ref[rows, :] = x_ref[rows, :] + (MACARON * gate) * y
                return carry

            lax.fori_loop(0, tm // ROW_CHUNK, tail, 0)

    tok = _token_spec(tm, D)
    act = pl.BlockSpec((None, tm, cw), lambda i, j: (j, i, 0))
    wblk = pl.BlockSpec((None, cw, D), lambda i, j: (j, 0, 0))
    return pl.pallas_call(
        body, name=name, grid=(nt, nb),
        in_specs=[tok, pl.BlockSpec((G, 3, D), lambda i, j: (0, 0, 0)),
                  pl.BlockSpec((1, D), lambda i, j: (0, 0)), wblk, wblk, wblk],
        out_specs=[tok, tok, act, act, tok],
        out_shape=[jax.ShapeDtypeStruct((T, D), F32), jax.ShapeDtypeStruct((T, D), BF16),
                   jax.ShapeDtypeStruct((nb, T, cw), BF16), jax.ShapeDtypeStruct((nb, T, cw), BF16),
                   jax.ShapeDtypeStruct((T, D), BF16)],
        scratch_shapes=[pltpu.VMEM((tm, D), BF16), pltpu.VMEM((tm, D), F32), pltpu.VMEM((tm, cw), F32),
                        pltpu.VMEM((tm, cw), F32), pltpu.VMEM((tm, cw), BF16)],
        compiler_params=_params(("arbitrary", "arbitrary")),
    )(x, mod, gn, wg, wu, wd)


def ffn_bwd_tok(x, dxo, y, a, b, mod, gn, wg, wu, wd, *, tm, t_first, name):
    T, D = x.shape
    nb, cw, _ = wg.shape
    G = mod.shape[0]
    nt = T // tm
    first = t_first // tm

    def body(x_ref, dxo_ref, y_ref, a_ref, b_ref, mod_ref, g_ref, wg_ref, wu_ref, wd_ref,
             dx_ref, da_ref, db_ref, s_ref, dy_ref, red_ref, dys, dh, ds_s):
        i, j = pl.program_id(0), pl.program_id(1)

        @pl.when(j == 0)
        def _():
            def head(r, carry):
                rows = pl.ds(pl.multiple_of(r * ROW_CHUNK, ROW_CHUNK), ROW_CHUNK)
                dyb = ((MACARON * mod_ref[2:3, :]) * dxo_ref[rows, :]).astype(BF16)
                dys[rows, :] = dyb
                dy_ref[rows, :] = dyb
                return carry

            lax.fori_loop(0, tm // ROW_CHUNK, head, 0)
            dh[...] = jnp.zeros_like(dh)

        parts = [slice(r, r + MM_ROWS) for r in range(0, tm, MM_ROWS)]
        for rows in parts:
            ds_s[rows, :] = _nt(dys[rows, :], wd_ref[...])
        for r in range(0, tm, ROW_CHUNK):
            rows = slice(r, r + ROW_CHUNK)
            av = a_ref[rows, :].astype(F32)
            bv = b_ref[rows, :].astype(F32)
            ds = ds_s[rows, :]
            sig, sl = _silu_parts(av)
            s_ref[rows, :] = (sl * bv).astype(BF16)
            da_ref[rows, :] = (ds * bv * _dsilu(av, sig)).astype(BF16)
            db_ref[rows, :] = (ds * sl).astype(BF16)
        for rows in parts:
            dh[rows, :] += _nn(da_ref[rows, :], wg_ref[...]) + _nn(db_ref[rows, :], wu_ref[...])

        @pl.when((j == 0) & ((i == 0) | (i == first)))
        def _():
            red_ref[...] = jnp.zeros_like(red_ref)

        @pl.when(j == nb - 1)
        def _():
            def tail(r, carry):
                rows = pl.ds(pl.multiple_of(r * ROW_CHUNK, ROW_CHUNK), ROW_CHUNK)
                dxo_v = dxo_ref[rows, :]
                dxn, dshift, dscale, dg = _norm_mod_bwd(x_ref[rows, :], dh[rows, :], g_ref[...], mod_ref[1:2, :])
                dx_ref[rows, :] = dxo_v + dxn
                red_ref[0:1, :] += dshift
                red_ref[1:2, :] += dscale
                red_ref[2:3, :] += jnp.sum((MACARON * dxo_v) * y_ref[rows, :].astype(F32), axis=0, keepdims=True)
                red_ref[3:4, :] += dg
                return carry

            lax.fori_loop(0, tm // ROW_CHUNK, tail, 0)

    tok = _token_spec(tm, D)
    act = pl.BlockSpec((None, tm, cw), lambda i, j: (j, i, 0))
    wblk = pl.BlockSpec((None, cw, D), lambda i, j: (j, 0, 0))
    return pl.pallas_call(
        body, name=name, grid=(nt, nb),
        in_specs=[tok, tok, tok, act, act,
                  pl.BlockSpec((None, 3, D), lambda i, j: (_group_index(i, first, G), 0, 0)),
                  pl.BlockSpec((1, D), lambda i, j: (0, 0)), wblk, wblk, wblk],
        out_specs=[tok, act, act, act, tok,
                   pl.BlockSpec((None, 8, D), lambda i, j: (_group_index(i, first, G), 0, 0))],
        out_shape=[jax.ShapeDtypeStruct((T, D), F32)] + [jax.ShapeDtypeStruct((nb, T, cw), BF16)] * 3
        + [jax.ShapeDtypeStruct((T, D), BF16), jax.ShapeDtypeStruct((G, 8, D), F32)],
        scratch_shapes=[pltpu.VMEM((tm, D), BF16), pltpu.VMEM((tm, D), F32), pltpu.VMEM((tm, cw), F32)],
        compiler_params=_params(("arbitrary", "arbitrary")),
    )(x, dxo, y, a, b, mod, gn, wg, wu, wd)


def matmul_tn(a, b, *, nb, m, n, a_blocked, b_blocked, tk, name, after=None):
    T = a.shape[-2]
    extra = [] if after is None else [after]

    def spec(arr, blocked, width):
        if blocked:
            return pl.BlockSpec((None, tk, width), lambda j, k: (j, k, 0))
        if arr.shape[-1] == width:
            return pl.BlockSpec((tk, width), lambda j, k: (k, 0))
        return pl.BlockSpec((tk, width), lambda j, k: (k, j))

    nk = T // tk

    def body(a_ref, b_ref, *rest):
        if nk == 1:
            rest[-1][...] = _tn(a_ref[...], b_ref[...]).astype(BF16)
            return
        o_ref, acc = rest[-2:]
        k = pl.program_id(1)

        @pl.when(k == 0)
        def _():
            acc[...] = jnp.zeros_like(acc)

        acc[...] += _tn(a_ref[...], b_ref[...])

        @pl.when(k == nk - 1)
        def _():
            o_ref[...] = acc[...].astype(BF16)

    return pl.pallas_call(
        body, name=name, grid=(nb, nk),
        in_specs=[spec(a, a_blocked, m), spec(b, b_blocked, n)] + [ANY] * len(extra),
        out_specs=pl.BlockSpec((None, m, n), lambda j, k: (j, 0, 0)),
        out_shape=jax.ShapeDtypeStruct((nb, m, n), BF16),
        scratch_shapes=[] if nk == 1 else [pltpu.VMEM((m, n), F32)],
        compiler_params=_params(("arbitrary", "arbitrary")),
    )(a, b, *extra)


def proj_in(x, mod, gn, w_in, *, tm, t_first, name):
    T, D = x.shape
    nb, _, cw = w_in.shape
    G = mod.shape[0]
    nt = T // tm

    def body(x_ref, mod_ref, g_ref, w_ref, p_ref, h_ref, hs):
        i = pl.program_id(0)

        @pl.when(pl.program_id(1) == 0)
        def _():
            def head(r, carry):
                rows = pl.ds(pl.multiple_of(r * ROW_CHUNK, ROW_CHUNK), ROW_CHUNK)
                row0 = i * tm + r * ROW_CHUNK
                hb = _norm_mod(x_ref[rows, :], g_ref[...], _mod_row(mod_ref, 0, row0, ROW_CHUNK, t_first),
                               _mod_row(mod_ref, 1, row0, ROW_CHUNK, t_first)).astype(BF16)
                hs[rows, :] = hb
                h_ref[rows, :] = hb
                return carry

            lax.fori_loop(0, tm // ROW_CHUNK, head, 0)

        for r in range(0, tm, MM_ROWS):
            p_ref[r:r + MM_ROWS, :] = _nn(hs[r:r + MM_ROWS, :], w_ref[...])

    tok = _token_spec(tm, D)
    return pl.pallas_call(
        body, name=name, grid=(nt, nb),
        in_specs=[tok, pl.BlockSpec((G, 3, D), lambda i, j: (0, 0, 0)),
                  pl.BlockSpec((1, D), lambda i, j: (0, 0)),
                  pl.BlockSpec((None, D, cw), lambda i, j: (j, 0, 0))],
        out_specs=[pl.BlockSpec((tm, cw), lambda i, j: (i, j)), tok],
        out_shape=[jax.ShapeDtypeStruct((T, nb * cw), F32), jax.ShapeDtypeStruct((T, D), BF16)],
        scratch_shapes=[pltpu.VMEM((tm, D), BF16)],
        compiler_params=_params(("arbitrary", "arbitrary")),
    )(x, mod, gn, w_in)


def proj_bwd_tok(x, dxo, dp, mod, gn, w_in, *, tm, name):
    T, D = x.shape
    nb, _, cw = w_in.shape
    G = mod.shape[0]
    nt = T // tm
    first = dxo.shape[0] // tm

    def body(x_ref, dxo_ref, dp_ref, mod_ref, g_ref, w_ref, dx_ref, red_ref, dh):
        i, j = pl.program_id(0), pl.program_id(1)

        @pl.when(j == 0)
        def _():
            dh[...] = jnp.zeros_like(dh)

        dh[...] += _nt(dp_ref[...], w_ref[...])

        @pl.when((j == 0) & ((i == 0) | (i == first)))
        def _():
            red_ref[...] = jnp.zeros_like(red_ref)

        @pl.when(j == nb - 1)
        def _():
            dxn, dshift, dscale, dg = _norm_mod_bwd(x_ref[...], dh[...], g_ref[...], mod_ref[1:2, :])
            dx_ref[...] = jnp.where(i < first, dxo_ref[...], 0.0) + dxn
            red_ref[0:1, :] += dshift
            red_ref[1:2, :] += dscale
            red_ref[3:4, :] += dg

    tok = pl.BlockSpec((tm, D), lambda i, j: (i, 0))
    return pl.pallas_call(
        body, name=name, grid=(nt, nb),
        in_specs=[tok, pl.BlockSpec((tm, D), lambda i, j: (jnp.minimum(i, first - 1), 0)),
                  pl.BlockSpec((tm, cw), lambda i, j: (i, j)),
                  pl.BlockSpec((None, 3, D), lambda i, j: (_group_index(i, first, G), 0, 0)),
                  pl.BlockSpec((1, D), lambda i, j: (0, 0)),
                  pl.BlockSpec((None, D, cw), lambda i, j: (j, 0, 0))],
        out_specs=[tok, pl.BlockSpec((None, 8, D), lambda i, j: (_group_index(i, first, G), 0, 0))],
        out_shape=[jax.ShapeDtypeStruct((T, D), F32), jax.ShapeDtypeStruct((G, 8, D), F32)],
        scratch_shapes=[pltpu.VMEM((tm, D), F32)],
        compiler_params=_params(("arbitrary", "arbitrary")),
    )(x, dxo, dp, mod, gn, w_in)


def _rope(t, cos, sin):
    return t * cos + pltpu.roll(t, HEAD_DIM // 2, axis=1) * sin


def ret_prep(p, cos, sin, *, heads, tm, name):
    T = p.shape[0]

    def body(q_ref, k_ref, v_ref, c_ref, s_ref, qo, ko, vo):
        cos_v, sin_v = c_ref[...], s_ref[...]
        qo[...] = _rope(q_ref[...], cos_v, sin_v).astype(BF16)
        ko[...] = _rope(k_ref[...] * K_SCALE, cos_v, sin_v).astype(BF16)
        vo[...] = v_ref[...].astype(BF16)

    def col(part):
        return pl.BlockSpec((tm, HEAD_DIM), lambda h, i: (i, part * heads + h))

    tab = pl.BlockSpec((tm, HEAD_DIM), lambda h, i: (i, 0))
    out = pl.BlockSpec((None, tm, HEAD_DIM), lambda h, i: (h, i, 0))
    return pl.pallas_call(
        body, name=name, grid=(heads, T // tm),
        in_specs=[col(0), col(1), col(2), tab, tab], out_specs=[out, out, out],
        out_shape=[jax.ShapeDtypeStruct((heads, T, HEAD_DIM), BF16)] * 3,
        compiler_params=_params(("arbitrary", "arbitrary")),
    )(p, p, p, cos, sin)


def _decay(n, m, lgf, lgb, t_lat, t_ctx):
    df = jnp.where(m < t_lat, n - m, n - m + (t_lat + t_ctx))
    db = m - n
    ef = jnp.where(df >= 0, jnp.exp(lgf * df), 0.0)
    eb = jnp.where(db >= 0, jnp.exp(lgb * db), 0.0)
    return ef, eb, df, db


def ret_fwd(q, k, v, lg, *, t_lat, tq, tk, name):
    H, T, _ = k.shape
    t_ctx = T - t_lat

    def body(lg_ref, q_ref, k_ref, v_ref, o_ref):
        h, qi, kj = pl.program_id(0), pl.program_id(1), pl.program_id(2)

        @pl.when(kj == 0)
        def _():
            o_ref[...] = jnp.zeros_like(o_ref)

        s = _nt(q_ref[...], k_ref[...])
        n = (qi * tq + lax.broadcasted_iota(jnp.int32, (tq, tk), 0)).astype(F32)
        m = (kj * tk + lax.broadcasted_iota(jnp.int32, (tq, tk), 1)).astype(F32)
        ef, eb, _, _ = _decay(n, m, lg_ref[0, h], lg_ref[1, h], t_lat, t_ctx)
        o_ref[...] += _nn((s * (ef + eb)).astype(BF16), v_ref[...])

    return pl.pallas_call(
        body, name=name, grid=(H, t_lat // tq, T // tk),
        in_specs=[pl.BlockSpec(memory_space=pltpu.SMEM),
                  pl.BlockSpec((None, tq, HEAD_DIM), lambda h, i, j: (h, i, 0)),
                  pl.BlockSpec((None, tk, HEAD_DIM), lambda h, i, j: (h, j, 0)),
                  pl.BlockSpec((None, tk, HEAD_DIM), lambda h, i, j: (h, j, 0))],
        out_specs=pl.BlockSpec((None, tq, HEAD_DIM), lambda h, i, j: (h, i, 0)),
        out_shape=jax.ShapeDtypeStruct((H, t_lat, HEAD_DIM), F32),
        compiler_params=_params(("arbitrary", "arbitrary", "arbitrary")),
    )(lg, q, k, v)


def ret_bwd(q, k, v, do, lg, *, t_lat, tq, tk, name):
    H, T, _ = k.shape
    t_ctx = T - t_lat

    def body(lg_ref, q_ref, k_ref, v_ref, do_ref, dq_ref, dk_ref, dv_ref, dlg_ref):
        h, kj, qi = pl.program_id(0), pl.program_id(1), pl.program_id(2)

        @pl.when((kj == 0) & (qi == 0))
        def _():
            dq_ref[...] = jnp.zeros_like(dq_ref)
            dlg_ref[...] = jnp.zeros_like(dlg_ref)

        @pl.when(qi == 0)
        def _():
            dk_ref[...] = jnp.zeros_like(dk_ref)
            dv_ref[...] = jnp.zeros_like(dv_ref)

        qv, kv, vv = q_ref[...], k_ref[...], v_ref[...]
        dob = do_ref[...].astype(BF16)
        st = _nt(kv, qv)
        dwt = _nt(vv, dob)
        m = (kj * tk + lax.broadcasted_iota(jnp.int32, (tk, tq), 0)).astype(F32)
        n = (qi * tq + lax.broadcasted_iota(jnp.int32, (tk, tq), 1)).astype(F32)
        ef, eb, df, db = _decay(n, m, lg_ref[0, h], lg_ref[1, h], t_lat, t_ctx)
        dec = ef + eb
        dv_ref[...] += _nn((st * dec).astype(BF16), dob)
        dst = (dwt * dec).astype(BF16)
        dk_ref[...] += _nn(dst, qv)
        rows = pl.ds(pl.multiple_of(qi * tq, tq), tq)
        dq_ref[rows, :] += _tn(dst, kv)
        gs = dwt * st
        dlg_ref[0:1, :] += jnp.sum(gs * (ef * df))
        dlg_ref[1:2, :] += jnp.sum(gs * (eb * db))

    kspec = pl.BlockSpec((None, tk, HEAD_DIM), lambda h, j, i: (h, j, 0))
    qspec = pl.BlockSpec((None, tq, HEAD_DIM), lambda h, j, i: (h, i, 0))
    return pl.pallas_call(
        body, name=name, grid=(H, T // tk, t_lat // tq),
        in_specs=[pl.BlockSpec(memory_space=pltpu.SMEM), qspec, kspec, kspec, qspec],
        out_specs=[pl.BlockSpec((None, t_lat, HEAD_DIM), lambda h, j, i: (h, 0, 0)), kspec, kspec,
                   pl.BlockSpec((None, 8, LANE), lambda h, j, i: (h, 0, 0))],
        out_shape=[jax.ShapeDtypeStruct((H, t_lat, HEAD_DIM), F32), jax.ShapeDtypeStruct((H, T, HEAD_DIM), F32),
                   jax.ShapeDtypeStruct((H, T, HEAD_DIM), F32), jax.ShapeDtypeStruct((H, 8, LANE), F32)],
        compiler_params=_params(("arbitrary", "arbitrary", "arbitrary")),
    )(lg, q, k, v, do)


def _group_norm(o):
    mu = jnp.mean(o, axis=-1, keepdims=True)
    ctr = o - mu
    r = lax.rsqrt(jnp.mean(ctr * ctr, axis=-1, keepdims=True) + GN_EPS)
    return ctr * r, r


def ret_post_fwd(o, p, *, heads, d_model, name):
    H, t_lat, _ = o.shape
    T = p.shape[0]

    def body(o_ref, g_ref, z_ref):
        on, _ = _group_norm(o_ref[...])
        _, sl = _silu_parts(g_ref[0:t_lat, :])
        z_ref[...] = (on * sl).astype(BF16)

    return pl.pallas_call(
        body, name=name, grid=(H,),
        in_specs=[pl.BlockSpec((None, t_lat, HEAD_DIM), lambda h: (h, 0, 0)),
                  pl.BlockSpec((T, HEAD_DIM), lambda h: (0, 3 * heads + h))],
        out_specs=pl.BlockSpec((t_lat, HEAD_DIM), lambda h: (0, h)),
        out_shape=jax.ShapeDtypeStruct((t_lat, d_model), BF16),
        compiler_params=_params(("arbitrary",)),
    )(o, p)


def _shift_rows(u, k):
    rows = u.shape[0]
    row = lax.broadcasted_iota(jnp.int32, u.shape, 0)
    rolled = pltpu.roll(u, k % rows, axis=0)
    return jnp.where((row >= k) & (row < rows + k), rolled, 0.0)


def conv_fwd(z, p, w_conv, *, heads, t_lat, name):
    T = p.shape[0]
    cb_n = w_conv.shape[1] // LANE
    base = 4 * heads

    def body(z_in, bg_ref, cg_ref, u_ref, w_ref, z_ref):
        cu = cg_ref[0:t_lat, :] * u_ref[0:t_lat, :]
        c3 = _shift_rows(cu, 1) * w_ref[0:1, :] + cu * w_ref[1:2, :] + _shift_rows(cu, -1) * w_ref[2:3, :]
        z_ref[...] = (bg_ref[0:t_lat, :] * c3).astype(BF16)

    def col(part):
        return pl.BlockSpec((T, LANE), lambda cb: (0, base + part * cb_n + cb))

    return pl.pallas_call(
        body, name=name, grid=(cb_n,),
        in_specs=[ANY, col(0), col(1), col(2), pl.BlockSpec((CONV_WIDTH, LANE), lambda cb: (0, cb))],
        out_specs=pl.BlockSpec((t_lat, LANE), lambda cb: (0, heads + cb)),
        out_shape=jax.ShapeDtypeStruct(z.shape, z.dtype),
        input_output_aliases={0: 0},
        compiler_params=_params(("arbitrary",)),
    )(z, p, p, p, w_conv)


def out_proj(z, w_out, x, mod, *, tm, name):
    t_lat, D = z.shape

    def body(z_ref, w_ref, x_ref, mod_ref, xo_ref, y_ref):
        y = _nn(z_ref[...], w_ref[...])
        y_ref[...] = y.astype(BF16)
        xo_ref[...] = x_ref[...] + mod_ref[2:3, :] * y

    tok = pl.BlockSpec((tm, D), lambda i: (i, 0))
    return pl.pallas_call(
        body, name=name, grid=(t_lat // tm,),
        in_specs=[tok, pl.BlockSpec((D, D), lambda i: (0, 0)), tok, pl.BlockSpec((None, 3, D), lambda i: (0, 0, 0))],
        out_specs=[tok, tok],
        out_shape=[jax.ShapeDtypeStruct((t_lat, D), F32), jax.ShapeDtypeStruct((t_lat, D), BF16)],
        compiler_params=_params(("arbitrary",)),
    )(z, w_out, x, mod)


def out_proj_bwd(dxo, y, w_out, mod, *, tm, name):
    t_lat, D = dxo.shape

    def body(dxo_ref, y_ref, w_ref, mod_ref, dz_ref, dy_ref, red_ref):
        @pl.when(pl.program_id(0) == 0)
        def _():
            red_ref[...] = jnp.zeros_like(red_ref)

        dxo_v = dxo_ref[...]
        dyb = (mod_ref[2:3, :] * dxo_v).astype(BF16)
        dy_ref[...] = dyb
        dz_ref[...] = _nt(dyb, w_ref[...])
        red_ref[2:3, :] += jnp.sum(dxo_v * y_ref[...].astype(F32), axis=0, keepdims=True)

    tok = pl.BlockSpec((tm, D), lambda i: (i, 0))
    return pl.pallas_call(
        body, name=name, grid=(t_lat // tm,),
        in_specs=[tok, tok, pl.BlockSpec((D, D), lambda i: (0, 0)), pl.BlockSpec((None, 3, D), lambda i: (0, 0, 0))],
        out_specs=[tok, tok, pl.BlockSpec((8, D), lambda i: (0, 0))],
        out_shape=[jax.ShapeDtypeStruct((t_lat, D), F32), jax.ShapeDtypeStruct((t_lat, D), BF16),
                   jax.ShapeDtypeStruct((8, D), F32)],
        compiler_params=_params(("arbitrary",)),
    )(dxo, y, w_out, mod)


def ret_post_bwd(dz, o, p, *, heads, name):
    H, t_lat, _ = o.shape
    T, in_w = p.shape

    def body(dz_ref, o_ref, g_ref, do_ref, dp_ref):
        on, r = _group_norm(o_ref[...])
        gg = g_ref[0:t_lat, :]
        sig, sl = _silu_parts(gg)
        dret = dz_ref[...]
        don = dret * sl
        do_ref[...] = r * (don - jnp.mean(don, axis=-1, keepdims=True)
                           - on * jnp.mean(don * on, axis=-1, keepdims=True))
        dp_ref[0:t_lat, :] = (dret * on * _dsilu(gg, sig)).astype(BF16)
        dp_ref[t_lat:T, :] = jnp.zeros((T - t_lat, HEAD_DIM), BF16)

    return pl.pallas_call(
        body, name=name, grid=(H,),
        in_specs=[pl.BlockSpec((t_lat, HEAD_DIM), lambda h: (0, h)),
                  pl.BlockSpec((None, t_lat, HEAD_DIM), lambda h: (h, 0, 0)),
                  pl.BlockSpec((T, HEAD_DIM), lambda h: (0, 3 * heads + h))],
        out_specs=[pl.BlockSpec((None, t_lat, HEAD_DIM), lambda h: (h, 0, 0)),
                   pl.BlockSpec((T, HEAD_DIM), lambda h: (0, 3 * heads + h))],
        out_shape=[jax.ShapeDtypeStruct((H, t_lat, HEAD_DIM), F32), jax.ShapeDtypeStruct((T, in_w), BF16)],
        compiler_params=_params(("arbitrary",)),
    )(dz, o, p)


def conv_bwd(dp, dz, p, w_conv, *, heads, t_lat, name):
    T = p.shape[0]
    cb_n = w_conv.shape[1] // LANE
    base = 4 * heads

    def body(dp_in, dz_ref, bg_ref, cg_ref, u_ref, w_ref, dp_ref, dw_ref):
        part = pl.program_id(1)
        cg, u = cg_ref[0:t_lat, :], u_ref[0:t_lat, :]
        cu = cg * u
        dconv = dz_ref[...]
        dp_ref[t_lat:T, :] = jnp.zeros((T - t_lat, LANE), BF16)

        @pl.when(part == 0)
        def _():
            c3 = _shift_rows(cu, 1) * w_ref[0:1, :] + cu * w_ref[1:2, :] + _shift_rows(cu, -1) * w_ref[2:3, :]
            dp_ref[0:t_lat, :] = (dconv * c3).astype(BF16)
            dc3 = dconv * bg_ref[0:t_lat, :]
            dw_ref[0:1, :] = jnp.sum(dc3 * _shift_rows(cu, 1), axis=0, keepdims=True)
            dw_ref[1:2, :] = jnp.sum(dc3 * cu, axis=0, keepdims=True)
            dw_ref[2:3, :] = jnp.sum(dc3 * _shift_rows(cu, -1), axis=0, keepdims=True)

        @pl.when(part > 0)
        def _():
            dc3 = dconv * bg_ref[0:t_lat, :]
            dcu = (_shift_rows(dc3, -1) * w_ref[0:1, :] + dc3 * w_ref[1:2, :] + _shift_rows(dc3, 1) * w_ref[2:3, :])
            dp_ref[0:t_lat, :] = (dcu * jnp.where(part == 1, u, cg)).astype(BF16)

    def col(part):
        return pl.BlockSpec((T, LANE), lambda cb, pt: (0, base + part * cb_n + cb))

    return pl.pallas_call(
        body, name=name, grid=(cb_n, 3),
        in_specs=[ANY, pl.BlockSpec((t_lat, LANE), lambda cb, pt: (0, heads + cb)), col(0), col(1), col(2),
                  pl.BlockSpec((CONV_WIDTH, LANE), lambda cb, pt: (0, cb))],
        out_specs=[pl.BlockSpec((T, LANE), lambda cb, pt: (0, base + pt * cb_n + cb)),
                   pl.BlockSpec((CONV_WIDTH, LANE), lambda cb, pt: (0, cb))],
        out_shape=[jax.ShapeDtypeStruct(dp.shape, dp.dtype), jax.ShapeDtypeStruct(w_conv.shape, F32)],
        input_output_aliases={0: 0},
        compiler_params=_params(("arbitrary", "arbitrary")),
    )(dp, dz, p, p, p, w_conv)


def ret_unprep(dp, dq, dk, dv, cos, sin, *, t_lat, name):
    H, T, _ = dk.shape

    def body(dp_in, dq_ref, dk_ref, dv_ref, c_ref, s_ref, dp_ref):
        part = pl.program_id(0)

        @pl.when(part == 0)
        def _():
            dp_ref[0:t_lat, :] = _rope(dq_ref[...], c_ref[0:t_lat, :], -s_ref[0:t_lat, :]).astype(BF16)
            dp_ref[t_lat:T, :] = jnp.zeros((T - t_lat, HEAD_DIM), BF16)

        @pl.when(part == 1)
        def _():
            dp_ref[...] = (_rope(dk_ref[...], c_ref[...], -s_ref[...]) * K_SCALE).astype(BF16)

        @pl.when(part == 2)
        def _():
            dp_ref[...] = dv_ref[...].astype(BF16)

    def head_of(part):
        return lambda pt, h: (jnp.where(pt == part, h, 0), 0, 0)

    tab = pl.BlockSpec((T, HEAD_DIM), lambda pt, h: (0, 0))
    return pl.pallas_call(
        body, name=name, grid=(3, H),
        in_specs=[ANY, pl.BlockSpec((None, t_lat, HEAD_DIM), head_of(0)), pl.BlockSpec((None, T, HEAD_DIM), head_of(1)),
                  pl.BlockSpec((None, T, HEAD_DIM), head_of(2)), tab, tab],
        out_specs=pl.BlockSpec((T, HEAD_DIM), lambda pt, h: (0, pt * H + h)),
        out_shape=jax.ShapeDtypeStruct(dp.shape, dp.dtype),
        input_output_aliases={0: 0},
        compiler_params=_params(("arbitrary", "arbitrary")),
    )(dp, dq, dk, dv, cos, sin)


def norm_mod_fwd(x, mod, gn, *, tm, name):
    T, D = x.shape

    def body(x_ref, mod_ref, g_ref, h_ref):
        h_ref[...] = _norm_mod(x_ref[...], g_ref[...], mod_ref[0:1, :], mod_ref[1:2, :])

    tok = pl.BlockSpec((tm, D), lambda i: (i, 0))
    return pl.pallas_call(
        body, name=name, grid=(T // tm,),
        in_specs=[tok, pl.BlockSpec((None, 3, D), lambda i: (0, 0, 0)), pl.BlockSpec((1, D), lambda i: (0, 0))],
        out_specs=tok, out_shape=jax.ShapeDtypeStruct((T, D), F32),
        compiler_params=_params(("arbitrary",)),
    )(x, mod, gn)


def norm_mod_bwd(x, dh, dxo, mod, gn, *, tm, name):
    T, D = x.shape

    def body(x_ref, dh_ref, dxo_ref, mod_ref, g_ref, dx_ref, red_ref):
        @pl.when(pl.program_id(0) == 0)
        def _():
            red_ref[...] = jnp.zeros_like(red_ref)

        dxn, dshift, dscale, dg = _norm_mod_bwd(x_ref[...], dh_ref[...], g_ref[...], mod_ref[1:2, :])
        dx_ref[...] = dxo_ref[...] + dxn
        red_ref[0:1, :] += dshift
        red_ref[1:2, :] += dscale
        red_ref[3:4, :] += dg

    tok = pl.BlockSpec((tm, D), lambda i: (i, 0))
    return pl.pallas_call(
        body, name=name, grid=(T // tm,),
        in_specs=[tok, tok, tok, pl.BlockSpec((None, 3, D), lambda i: (0, 0, 0)), pl.BlockSpec((1, D), lambda i: (0, 0))],
        out_specs=[tok, pl.BlockSpec((8, D), lambda i: (0, 0))],
        out_shape=[jax.ShapeDtypeStruct((T, D), F32), jax.ShapeDtypeStruct((8, D), F32)],
        compiler_params=_params(("arbitrary",)),
    )(x, dh, dxo, mod, gn)


def _window_sum(u, w, lead):
    T, C = u.shape
    ext = jnp.concatenate([u, jnp.zeros((POOL_PAD, C), F32)], axis=0)
    k = 1
    while k < w:
        ext = ext + _shift_rows(ext, k)
        k *= 2
    return _shift_rows(ext, -lead)[0:T, :]


def _window_count(T, C, w):
    t = lax.broadcasted_iota(jnp.int32, (T, C), 0)
    lo = jnp.clip(t - w // 2, 0, T)
    hi = jnp.clip(t + (w - w // 2), 0, T)
    return (hi - lo).astype(F32)


def pool_fwd(h, x, pool_w, scale, mod, *, name):
    T, D = h.shape
    G, Cg, _ = pool_w.shape
    ns = Cg // LANE

    def body(h_ref, x_ref, w_ref, sc_ref, mod_ref, xo_ref, pl_ref, yl_ref, acc):
        g, s = pl.program_id(0), pl.program_id(1)
        hv = h_ref[...]
        for gi, win in enumerate(POOL_WINDOWS):
            @pl.when(g == gi)
            def _():
                mean = _window_sum(hv, win, win // 2 - 1) / _window_count(T, LANE, win)
                pooled = (mean - hv).astype(BF16)
                pl_ref[...] = pooled
                contrib = _nn(pooled, w_ref[...])

                @pl.when(s == 0)
                def _():
                    acc[...] = contrib

                @pl.when(s > 0)
                def _():
                    acc[...] += contrib

        @pl.when(s == ns - 1)
        def _():
            yl = acc[...]
            yl_ref[...] = yl.astype(BF16)
            xo_ref[...] = x_ref[...] + mod_ref[2:3, :] * (yl * sc_ref[...])

    grp = pl.BlockSpec((T, Cg), lambda g, s: (0, g))
    sub = pl.BlockSpec((T, LANE), lambda g, s: (0, g * ns + s))
    return pl.pallas_call(
        body, name=name, grid=(G, ns),
        in_specs=[sub, grp, pl.BlockSpec((None, LANE, Cg), lambda g, s: (g, s, 0)),
                  pl.BlockSpec((1, Cg), lambda g, s: (0, g)), pl.BlockSpec((None, 3, Cg), lambda g, s: (0, 0, g))],
        out_specs=[grp, sub, grp],
        out_shape=[jax.ShapeDtypeStruct((T, D), F32), jax.ShapeDtypeStruct((T, D), BF16),
                   jax.ShapeDtypeStruct((T, D), BF16)],
        scratch_shapes=[pltpu.VMEM((T, Cg), F32)],
        compiler_params=_params(("arbitrary", "arbitrary")),
    )(h, x, pool_w, scale, mod)


def pool_bwd(dxo, pooled, yl, pool_w, scale, mod, *, name):
    T, D = dxo.shape
    G, Cg, _ = pool_w.shape
    ns = Cg // LANE

    def body(dxo_ref, pl_ref, yl_ref, w_ref, sc_ref, mod_ref, dh_ref, dw_ref, red_ref, dyl):
        g, s = pl.program_id(0), pl.program_id(1)

        @pl.when(s == 0)
        def _():
            dxo_v = dxo_ref[...]
            ylv = yl_ref[...].astype(F32)
            dy = mod_ref[2:3, :] * dxo_v
            dyl[...] = (dy * sc_ref[...]).astype(BF16)
            red_ref[...] = jnp.zeros_like(red_ref)
            red_ref[2:3, :] = jnp.sum(dxo_v * (ylv * sc_ref[...]), axis=0, keepdims=True)
            red_ref[4:5, :] = jnp.sum(dy * ylv, axis=0, keepdims=True)

        dylv = dyl[...]
        dpool = _nt(dylv, w_ref[...])
        dw_ref[...] = _tn(pl_ref[...], dylv).astype(BF16)
        for gi, win in enumerate(POOL_WINDOWS):
            @pl.when(g == gi)
            def _():
                spread = _window_sum(dpool / _window_count(T, LANE, win), win, win // 2)
                dh_ref[...] = spread - dpool

    grp = pl.BlockSpec((T, Cg), lambda g, s: (0, g))
    sub = pl.BlockSpec((T, LANE), lambda g, s: (0, g * ns + s))
    wsub = pl.BlockSpec((None, LANE, Cg), lambda g, s: (g, s, 0))
    return pl.pallas_call(
        body, name=name, grid=(G, ns),
        in_specs=[grp, sub, grp, wsub, pl.BlockSpec((1, Cg), lambda g, s: (0, g)),
                  pl.BlockSpec((None, 3, Cg), lambda g, s: (0, 0, g))],
        out_specs=[sub, wsub, pl.BlockSpec((8, Cg), lambda g, s: (0, g))],
        out_shape=[jax.ShapeDtypeStruct((T, D), F32), jax.ShapeDtypeStruct((G, Cg, Cg), BF16),
                   jax.ShapeDtypeStruct((8, D), F32)],
        scratch_shapes=[pltpu.VMEM((T, Cg), BF16)],
        compiler_params=_params(("arbitrary", "arbitrary")),
    )(dxo, pooled, yl, pool_w, scale, mod)


def final_loss(x, gn, target, *, tm, name):
    T, D = x.shape

    def body(x_ref, g_ref, t_ref, loss_ref, dx_ref, red_ref):
        @pl.when(pl.program_id(0) == 0)
        def _():
            loss_ref[...] = jnp.zeros_like(loss_ref)
            red_ref[...] = jnp.zeros_like(red_ref)

        xx, g = x_ref[...], g_ref[...]
        r = lax.rsqrt(jnp.mean(xx * xx, axis=-1, keepdims=True) + EPS)
        xhat = xx * r
        err = xhat * g - t_ref[...]
        loss_ref[...] += 0.5 * jnp.sum(jnp.mean(err * err, axis=-1, keepdims=True))
        dy = err / D
        red_ref[0:1, :] += jnp.sum(dy * xhat, axis=0, keepdims=True)
        dxh = dy * g
        dx_ref[...] = r * (dxh - xhat * jnp.mean(dxh * xhat, axis=-1, keepdims=True))

    tok = pl.BlockSpec((tm, D), lambda i: (i, 0))
    return pl.pallas_call(
        body, name=name, grid=(T // tm,),
        in_specs=[tok, pl.BlockSpec((1, D), lambda i: (0, 0)), tok],
        out_specs=[pl.BlockSpec((8, LANE), lambda i: (0, 0)), tok, pl.BlockSpec((8, D), lambda i: (0, 0))],
        out_shape=[jax.ShapeDtypeStruct((8, LANE), F32), jax.ShapeDtypeStruct((T, D), F32),
                   jax.ShapeDtypeStruct((8, D), F32)],
        compiler_params=_params(("arbitrary",)),
    )(x, gn, target)


def _rope_tables(t_lat, t_ctx):
    quarter = HEAD_DIM // 4
    pos = jnp.arange(t_lat)
    inv = ROPE_BASE ** (-jnp.arange(quarter, dtype=F32) / quarter)
    ang = jnp.concatenate([(pos // GRID_W).astype(F32)[:, None] * inv, (pos % GRID_W).astype(F32)[:, None] * inv], axis=-1)
    cos, sin = jnp.cos(ang), jnp.sin(ang)
    cos = jnp.concatenate([jnp.concatenate([cos, cos], axis=-1), jnp.ones((t_ctx, HEAD_DIM), F32)], axis=0)
    sin = jnp.concatenate([jnp.concatenate([-sin, sin], axis=-1), jnp.zeros((t_ctx, HEAD_DIM), F32)], axis=0)
    return cos, sin


def _ffn_grads(h, da, db, s, dy, tag):
    nb, T, cw = da.shape
    D = h.shape[1]
    tk = T
    g_gate = matmul_tn(da, h, nb=nb, m=cw, n=D, a_blocked=True, b_blocked=False, tk=tk, name=f"wgrad_gate_{tag}")
    g_up = matmul_tn(db, h, nb=nb, m=cw, n=D, a_blocked=True, b_blocked=False, tk=tk, name=f"wgrad_up_{tag}")
    g_down = matmul_tn(s, dy, nb=nb, m=cw, n=D, a_blocked=True, b_blocked=False, tk=tk, name=f"wgrad_down_{tag}")
    return [g_gate, g_up, g_down]


def kernel(x, c, ctx, c_ctx, w_mod, b_mod, norm_ffn1, norm_mix, norm_ffn2, ffn1_w_gate, ffn1_w_up, ffn1_w_down, ffn2_w_gate, ffn2_w_up, ffn2_w_down, mix_w_in, mix_w_conv, mix_w_out, ret_decay_fwd, ret_decay_bwd, pool_w, pool_scale, final_norm, loss_target, m_c_ctx, m_w_mod, m_b_mod, m_norm_ffn1, m_norm_mix, m_norm_ffn2, m_ffn1_w_gate, m_ffn1_w_up, m_ffn1_w_down, m_ffn2_w_gate, m_ffn2_w_up, m_ffn2_w_down, m_mix_w_in, m_mix_w_conv, m_mix_w_out, m_ret_decay_fwd, m_ret_decay_bwd, m_pool_w, m_pool_scale, m_final_norm, v_c_ctx, v_w_mod, v_b_mod, v_norm_ffn1, v_norm_mix, v_norm_ffn2, v_ffn1_w_gate, v_ffn1_w_up, v_ffn1_w_down, v_ffn2_w_gate, v_ffn2_w_up, v_ffn2_w_down, v_mix_w_in, v_mix_w_conv, v_mix_w_out, v_ret_decay_fwd, v_ret_decay_bwd, v_pool_w, v_pool_scale, v_final_norm):
    t_lat, D = x.shape[1], x.shape[2]
    t_ctx = ctx.shape[1]
    T = t_lat + t_ctx
    heads = ret_decay_fwd.shape[1]
    mod_w = w_mod.shape[2]
    tm = 256
    tmf = 512 if t_lat % 512 == 0 else 256
    tq = 512 if t_lat % 512 == 0 else 256
    tk = T // 3 if (T % 3 == 0 and (T // 3) % 256 == 0) else 256

    ax, ay, ac = lax.axis_index("x"), lax.axis_index("y"), lax.axis_index("c")
    me = 4 * ax + 2 * ay + ac
    own_block = jnp.reshape(me, (1,)).astype(jnp.int32)
    peer_blocks = jnp.stack([4 * (1 - ax) + 2 * ay + ac, 4 * ax + 2 * (1 - ay) + ac,
                             4 * (1 - ax) + 2 * (1 - ay) + ac]).astype(jnp.int32)

    (c_all,) = all_gather([c], name="gather_cond")
    cond16 = jnp.concatenate([c_all.reshape(N_DEV, D), c_ctx[None, :], jnp.zeros((7, D), F32)], axis=0)
    b_blk = lax.dynamic_slice_in_dim(b_mod, me * mod_w, mod_w, axis=1)[:, None, :]
    m_blk = adaln_fwd(cond16, w_mod, b_blk, name="adaln_fwd")
    m_all, w_conv, pscale = all_gather([m_blk, mix_w_conv[0], pool_scale], name="gather_mod")
    mods = jnp.transpose(m_all, (1, 2, 0, 3)).reshape(2, 16, N_MOD, D)
    mod_x = lax.dynamic_index_in_dim(mods, me, axis=1, keepdims=False)
    mod_c = mods[0, 8]
    w_conv = jnp.transpose(w_conv, (1, 0, 2)).reshape(CONV_WIDTH, -1)
    pscale = pscale.reshape(1, D)

    n_grp, grp_rows, grp_w = pool_w.shape[1:]
    ffn1_w = [(jnp.swapaxes(ffn1_w_gate, 1, 2), "gate"), (jnp.swapaxes(ffn1_w_up, 1, 2), "up"), (ffn1_w_down, "down")]
    ffn2_w = [(jnp.swapaxes(ffn2_w_gate, 1, 2), "gate"), (jnp.swapaxes(ffn2_w_up, 1, 2), "up"), (ffn2_w_down, "down")]
    groups = {
        "01": [(w, 0, f"ffn1_{k}") for w, k in ffn1_w],
        "mix": [(mix_w_in, 0, "w_in"), (mix_w_out, 0, "w_out")],
        "02": [(w, 0, f"ffn2_{k}") for w, k in ffn2_w],
        "11": [(w, 1, f"ffn1_{k}") for w, k in ffn1_w] + [(pool_w.reshape(1, n_grp * grp_rows, grp_w), 0, "pool_w")],
        "12": [(w, 1, f"ffn2_{k}") for w, k in ffn2_w],
    }
    started = {}

    def start(tag, after):
        lands = [cast_place(w, l, own_block, name=f"cast_{nm}_{l}") for w, l, nm in groups[tag]]
        started[tag] = gather_start(lands, after, name=f"gather_start_{tag}")
        return started[tag][-1][0:1, 0:1]

    def finish(tag, after):
        return gather_finish(gather_wait(started[tag], after, name=f"gather_wait_{tag}"), name=f"gather_finish_{tag}")

    lg = jnp.concatenate([jax.nn.log_sigmoid(ret_decay_fwd), jax.nn.log_sigmoid(ret_decay_bwd)], axis=0)
    cos, sin = _rope_tables(t_lat, t_ctx)

    def mod3(l, k, with_ctx=False, tie=None):
        rows = mod_x[l, 3 * k:3 * k + 3][None]
        if with_ctx:
            rows = jnp.concatenate([rows, mod_c[3 * k:3 * k + 3][None]], axis=0)
        return rows if tie is None else rows + tie

    tie = start("01", m_all)
    x0 = jnp.concatenate([x[0], ctx[0]], axis=0) + tie
    wg01, wu01, wd01 = finish("01", x0)
    tie = start("mix", wd01)
    x1, h1, a1, b1, y1 = ffn_fwd(x0, mod3(0, 0, True, tie), norm_ffn1[0:1], wg01, wu01, wd01,
                                 tm=tm, t_first=t_lat, name="ffn_fwd_01")
    w_in, w_out = finish("mix", h1)
    tie = start("02", w_out)
    w_out = w_out.reshape(D, D)
    p, hm = proj_in(x1, mod3(0, 1, True, tie), norm_mix[0:1], w_in, tm=768 if T % 768 == 0 else tm, t_first=t_lat,
                    name="proj_in")
    qr, kr, vr = ret_prep(p, cos, sin, heads=heads, tm=tm, name="ret_prep")
    o = ret_fwd(qr, kr, vr, lg, t_lat=t_lat, tq=tq, tk=tk, name="ret_fwd")
    wg02, wu02, wd02 = finish("02", o)
    tie = start("11", wd02)
    z = ret_post_fwd(o, p, heads=heads, d_model=D, name="ret_post_fwd")
    z = conv_fwd(z, p, w_conv, heads=heads, t_lat=t_lat, name="conv_fwd")
    x2, ym = out_proj(z, w_out, x1, mod3(0, 1, tie=tie), tm=tm, name="out_proj")
    x3, h3, a3, b3, y3 = ffn_fwd(x2, mod3(0, 2, tie=tie), norm_ffn2[0:1], wg02, wu02, wd02, tm=tmf, t_first=t_lat, name="ffn_fwd_02")
    wg11, wu11, wd11, pw = finish("11", h3)
    tie = start("12", wd11)
    pw = jnp.transpose(pw.reshape(N_DEV, n_grp, grp_rows, grp_w), (1, 0, 2, 3)).reshape(n_grp, N_DEV * grp_rows, grp_w)
    x4, h4, a4, b4, y4 = ffn_fwd(x3, mod3(1, 0, tie=tie), norm_ffn1[1:2], wg11, wu11, wd11, tm=tmf, t_first=t_lat, name="ffn_fwd_11")
    hp = norm_mod_fwd(x4, mod3(1, 1), norm_mix[1:2], tm=tm, name="pool_norm_fwd")
    x5, pooled, yl = pool_fwd(hp, x4, pw, pscale, mod3(1, 1), name="pool_fwd")
    wg12, wu12, wd12 = finish("12", yl)
    x6, h6, a6, b6, y6 = ffn_fwd(x5, mod3(1, 2), norm_ffn2[1:2], wg12, wu12, wd12, tm=tmf, t_first=t_lat, name="ffn_fwd_12")
    loss_part, dx6, red_fn = final_loss(x6, final_norm[None, :], loss_target[0], tm=tm, name="final_loss")
    loss = lax.psum(loss_part[0, 0], ("x", "y", "c"))

    reducing = {}

    def reduce_start(tag, grads, after=None):
        res = rs_sibling_exchange(grads, grads[0] if after is None else after, name=f"rs_sibling_{tag}")
        n = len(grads)
        stages, directs = res[:n], res[n:]
        sums = chip_sum_all(grads, stages, peer_blocks, tag)
        handle = chip_exchange_start(sums, name=f"rs_start_{tag}")
        reducing[tag] = (grads, directs, handle)
        return handle[-1][0:1, 0:1]

    def reduce_begin(tag, grads, after):
        reducing[tag] = sibling_exchange_start(grads, after, name=f"sib_start_{tag}")
        return reducing[tag][-1][0:1, 0:1]

    def reduce_middle(tag, after):
        grads, stages, directs = sibling_exchange_wait(reducing[tag], after, name=f"sib_wait_{tag}")
        sums = chip_sum_all(grads, stages, peer_blocks, tag)
        handle = chip_exchange_start(sums, name=f"rs_start_{tag}")
        reducing[tag] = (grads, directs, handle)
        return handle[-1][0:1, 0:1]

    dx5, da6, db6, s6, dy6, red12 = ffn_bwd_tok(x5, dx6, y6, a6, b6, mod3(1, 2), norm_ffn2[1:2], wg12, wu12, wd12,
                                                tm=tmf, t_first=t_lat, name="ffn_bwd_12")
    tie = reduce_begin("12", _ffn_grads(h6, da6, db6, s6, dy6, "12"), dx5)
    dhp, g_pw, red_pool = pool_bwd(dx5, pooled, yl, pw, pscale, mod3(1, 1, tie=tie), name="pool_bwd")
    dx4, red_pn = norm_mod_bwd(x4, dhp, dx5, mod3(1, 1), norm_mix[1:2], tm=tm, name="pool_norm_bwd")
    tie = reduce_middle("12", dx4)
    dx3, da4, db4, s4, dy4, red11 = ffn_bwd_tok(x3, dx4, y4, a4, b4, mod3(1, 0, tie=tie), norm_ffn1[1:2], wg11, wu11, wd11,
                                                tm=tmf, t_first=t_lat, name="ffn_bwd_11")
    g_pw = jnp.transpose(g_pw.reshape(n_grp, N_DEV, grp_rows, grp_w), (1, 0, 2, 3)).reshape(N_DEV, n_grp * grp_rows, grp_w)
    tie = reduce_begin("11", list(_ffn_grads(h4, da4, db4, s4, dy4, "11")) + [g_pw], dx3)
    dx2, da3, db3, s3, dy3, red02 = ffn_bwd_tok(x2, dx3, y3, a3, b3, mod3(0, 2, tie=tie), norm_ffn2[0:1], wg02, wu02, wd02,
                                                tm=tmf, t_first=t_lat, name="ffn_bwd_02")
    tie = reduce_middle("11", dx2)
    tie = tie + reduce_begin("02", _ffn_grads(h3, da3, db3, s3, dy3, "02"), dx2)
    dz, dym, red_op = out_proj_bwd(dx2, ym, w_out, mod3(0, 1, tie=tie), tm=tm, name="out_proj_bwd")
    g_wout = matmul_tn(z, dym, nb=N_DEV, m=D // N_DEV, n=D, a_blocked=False, b_blocked=False,
                       tk=t_lat, name="wgrad_out")
    do, dp = ret_post_bwd(dz, o, p, heads=heads, name="ret_post_bwd")
    dp, g_conv = conv_bwd(dp, dz, p, w_conv, heads=heads, t_lat=t_lat, name="conv_bwd")
    dq, dk, dv, dlg = ret_bwd(qr, kr, vr, do, lg, t_lat=t_lat, tq=tq, tk=tk, name="ret_bwd")
    tie = reduce_middle("02", dq)
    dp = ret_unprep(dp, dq, dk, dv, cos, sin, t_lat=t_lat, name="ret_unprep")
    dx1, red_mix = proj_bwd_tok(x1, dx2, dp, mod3(0, 1, True, tie), norm_mix[0:1], w_in, tm=tm, name="proj_bwd")
    g_win = matmul_tn(hm, dp, nb=N_DEV, m=D, n=w_in.shape[2], a_blocked=False, b_blocked=False, tk=T, name="wgrad_in")
    tie = reduce_start("mix", [g_win, g_wout])
    dx0, da1, db1, s1, dy1, red01 = ffn_bwd_tok(x0, dx1, y1, a1, b1, mod3(0, 0, True, tie), norm_ffn1[0:1], wg01, wu01, wd01,
                                                tm=tm, t_first=t_lat, name="ffn_bwd_01")
    res = {}

    dm_x = jnp.stack([jnp.concatenate([red01[0, 0:3], red_mix[0, 0:2], red_op[2:3], red02[0, 0:3]], axis=0),
                      jnp.concatenate([red11[0, 0:3], red_pn[0:2], red_pool[2:3], red12[0, 0:3]], axis=0)])
    dm_c = jnp.concatenate([red01[1, 0:3], red_mix[1, 0:2], jnp.zeros((4, D), F32)], axis=0)
    d_lg = dlg[:, 0:2, 0].T
    d_dec_f = d_lg[0:1] * jax.nn.sigmoid(-ret_decay_fwd)
    d_dec_b = d_lg[1:2] * jax.nn.sigmoid(-ret_decay_bwd)
    pieces = [dm_x.reshape(-1), dm_c.reshape(-1),
              jnp.stack([red01[0, 3] + red01[1, 3], red11[0, 3]]).reshape(-1),
              jnp.stack([red_mix[0, 3] + red_mix[1, 3], red_pn[3]]).reshape(-1),
              jnp.stack([red02[0, 3], red12[0, 3]]).reshape(-1),
              red_fn[0], d_dec_f.reshape(-1), d_dec_b.reshape(-1), g_conv.reshape(-1), red_pool[4]]
    sizes = [int(a.shape[0]) for a in pieces]
    n_pack = sum(sizes)
    n_pad = -n_pack % 8192
    packed = jnp.concatenate(pieces + [jnp.zeros((n_pad,), F32)])[None, :]
    (packed_all,) = all_gather([packed], name="gather_partials")
    total = sum_partials(packed_all, name="sum_partials")[0]
    offs = [0]
    for s in sizes:
        offs.append(offs[-1] + s)
    seg = [total[offs[i]:offs[i + 1]] for i in range(len(sizes))]
    g_dm = seg[0].reshape(2, N_MOD * D)
    g_dmc = seg[1].reshape(N_MOD * D)
    g_b_mod = g_dm.at[0].add(g_dmc)
    g_norm_ffn1, g_norm_mix, g_norm_ffn2 = (seg[k].reshape(2, D) for k in (2, 3, 4))
    g_final = seg[5]
    g_dec_f, g_dec_b = seg[6].reshape(1, heads), seg[7].reshape(1, heads)
    g_conv_all = seg[8].reshape(CONV_WIDTH, -1)
    g_pscale_all = seg[9]
    conv_w = mix_w_conv.shape[2]
    g_w_conv = lax.dynamic_slice_in_dim(g_conv_all, me * conv_w, conv_w, axis=1)[None]
    ps_w = pool_scale.shape[1]
    g_pool_scale = lax.dynamic_slice_in_dim(g_pscale_all, me * ps_w, ps_w, axis=0)[None]

    dm_rows = packed_all[:, 0, 0:offs[1]].reshape(N_DEV, 2, N_MOD * D)
    dmc_rows = packed_all[:, 0, offs[1]:offs[2]].reshape(N_DEV, N_MOD * D)
    dm_x_blk = jnp.transpose(lax.dynamic_slice_in_dim(dm_rows, me * mod_w, mod_w, axis=2), (1, 0, 2))
    dm_c_blk = jnp.stack([lax.dynamic_slice_in_dim(dmc_rows, me * mod_w, mod_w, axis=1),
                          jnp.zeros((N_DEV, mod_w), F32)])
    g_w_mod, d_w_mod, nm_w_mod, nv_w_mod, cctx_part = adaln_bwd(cond16, dm_x_blk, dm_c_blk, w_mod, m_w_mod, v_w_mod,
                                                                name="adaln_bwd")
    cpad = jnp.concatenate([cctx_part[0], jnp.zeros((8192 - D,), F32)])[None, :] if D < 8192 else cctx_part[0:1]
    (cctx_all,) = all_gather([cpad], name="gather_cctx")
    g_c_ctx = sum_partials(cctx_all, name="sum_cctx")[0, :D]

    small = [("c_ctx", g_c_ctx, c_ctx, m_c_ctx, v_c_ctx), ("b_mod", g_b_mod, b_mod, m_b_mod, v_b_mod),
             ("norm_ffn1", g_norm_ffn1, norm_ffn1, m_norm_ffn1, v_norm_ffn1),
             ("norm_mix", g_norm_mix, norm_mix, m_norm_mix, v_norm_mix),
             ("norm_ffn2", g_norm_ffn2, norm_ffn2, m_norm_ffn2, v_norm_ffn2),
             ("mix_w_conv", g_w_conv, mix_w_conv, m_mix_w_conv, v_mix_w_conv),
             ("ret_decay_fwd", g_dec_f, ret_decay_fwd, m_ret_decay_fwd, v_ret_decay_fwd),
             ("ret_decay_bwd", g_dec_b, ret_decay_bwd, m_ret_decay_bwd, v_ret_decay_bwd),
             ("pool_scale", g_pool_scale, pool_scale, m_pool_scale, v_pool_scale),
             ("final_norm", g_final, final_norm, m_final_norm, v_final_norm)]
    ssz = [int(a[1].size) for a in small]
    spad = -sum(ssz) % LANE

    def pack(k):
        return jnp.concatenate([a[k].reshape(-1) for a in small] + [jnp.ones((spad,), F32)])[None, :]

    sd, sm, sv = adam_plain(pack(1), pack(2), pack(3), pack(4), name="adam_small")
    soff = [0]
    for s in ssz:
        soff.append(soff[-1] + s)
    for i, (nm, g, w, _, _) in enumerate(small):
        res[nm] = [g.reshape(w.shape)] + [a[0, soff[i]:soff[i + 1]].reshape(w.shape) for a in (sd, sm, sv)]
    res["w_mod"] = [g_w_mod, d_w_mod, nm_w_mod, nv_w_mod]

    after = sd
    for kind, lhs, rhs in (("gate", da1, h1), ("up", db1, h1), ("down", s1, dy1)):
        g = matmul_tn(lhs, rhs, nb=N_DEV, m=lhs.shape[2], n=D, a_blocked=True, b_blocked=False, tk=T,
                      name=f"wgrad_{kind}_01", after=after)
        tie = reduce_start(f"01_{kind}", [g], g)
        after = reducing[f"01_{kind}"][2][2]
    grad_x = dx0[:t_lat][None] + tie

    last = [grad_x]

    def reduce_finish(tag):
        grads, directs, handle = reducing[tag]
        landed = chip_exchange_wait(handle, last[0], name=f"rs_wait_{tag}")
        return list(zip(grads, directs, landed))

    def big(w, m, v, l, part, prev, nm, transposed=False):
        if transposed:
            w, m, v = (jnp.swapaxes(a, 1, 2) for a in (w, m, v))
        outs = adam_reduced(own_block, part[0], part[1], part[2], w, m, v, l, prev, name=nm)
        last[0] = outs[0]
        return outs

    ffn1 = [(ffn1_w_gate, m_ffn1_w_gate, v_ffn1_w_gate), (ffn1_w_up, m_ffn1_w_up, v_ffn1_w_up),
            (ffn1_w_down, m_ffn1_w_down, v_ffn1_w_down)]
    ffn2 = [(ffn2_w_gate, m_ffn2_w_gate, v_ffn2_w_gate), (ffn2_w_up, m_ffn2_w_up, v_ffn2_w_up),
            (ffn2_w_down, m_ffn2_w_down, v_ffn2_w_down)]
    kinds = ["gate", "up", "down"]
    half = {}
    parts = reduce_finish("12")
    for t, (w, m, v) in enumerate(ffn2):
        half[f"ffn2_w_{kinds[t]}"] = big(w, m, v, 1, parts[t], None, f"adam_ffn2_w_{kinds[t]}_1", transposed=t < 2)
    parts = reduce_finish("11")
    for t, (w, m, v) in enumerate(ffn1):
        half[f"ffn1_w_{kinds[t]}"] = big(w, m, v, 1, parts[t], None, f"adam_ffn1_w_{kinds[t]}_1", transposed=t < 2)
    res["pool_w"] = [a.reshape(pool_w.shape) for a in
                     big(pool_w.reshape(1, n_grp * grp_rows, grp_w), m_pool_w.reshape(1, n_grp * grp_rows, grp_w),
                         v_pool_w.reshape(1, n_grp * grp_rows, grp_w), 0, parts[3], None, "adam_pool_w")]
    parts = reduce_finish("02")
    for t, (w, m, v) in enumerate(ffn2):
        nm = f"ffn2_w_{kinds[t]}"
        outs = big(w, m, v, 0, parts[t], half[nm], f"adam_{nm}_0", transposed=t < 2)
        res[nm] = [jnp.swapaxes(a, 1, 2) for a in outs] if t < 2 else outs
    parts = reduce_finish("mix")
    res["mix_w_in"] = big(mix_w_in, m_mix_w_in, v_mix_w_in, 0, parts[0], None, "adam_mix_w_in")
    res["mix_w_out"] = big(mix_w_out, m_mix_w_out, v_mix_w_out, 0, parts[1], None, "adam_mix_w_out")
    for t, (w, m, v) in enumerate(ffn1):
        nm = f"ffn1_w_{kinds[t]}"
        outs = big(w, m, v, 0, reduce_finish(f"01_{kinds[t]}")[0], half[nm], f"adam_{nm}_0", transposed=t < 2)
        res[nm] = [jnp.swapaxes(a, 1, 2) for a in outs] if t < 2 else outs

    order = ["c_ctx", "w_mod", "b_mod", "norm_ffn1", "norm_mix", "norm_ffn2", "ffn1_w_gate", "ffn1_w_up", "ffn1_w_down",
             "ffn2_w_gate", "ffn2_w_up", "ffn2_w_down", "mix_w_in", "mix_w_conv", "mix_w_out", "ret_decay_fwd",
             "ret_decay_bwd", "pool_w", "pool_scale", "final_norm"]
    return (loss, grad_x, *[res[n][0] for n in order], *[res[n][1] for n in order],
            *[res[n][2] for n in order], *[res[n][3] for n in order])
```
